```python
import jax, jax.numpy as jnp
from jax import lax
import numpy as np

D_MODEL = 1024
BATCH = 4
SEQ = 4096
DEPTH = 4
DEC_BATCH = 8
DEC_SEQ = 64
PAST_LEN = 4096

CHUNK = 64
Q_BLOCK = 128
EPS = 1e-6
F32 = jnp.float32

RW_HEADS = 8
RW_HEAD_DIM = 64
RW_WIDTH = RW_HEADS * RW_HEAD_DIM
RW_DECAY_LORA = 64
RW_ICLR_LORA = 64
RW_GATE_LORA = 128
RW_SPLITS = (RW_WIDTH, RW_WIDTH, RW_WIDTH, RW_DECAY_LORA, RW_ICLR_LORA, RW_GATE_LORA)
RW_COLS = sum(RW_SPLITS)
RW_GN_EPS = 64e-5

SA_HEADS = 8
SA_KV_HEADS = 2
SA_HEAD_DIM = 64
SA_WIDTH = SA_HEADS * SA_HEAD_DIM
IDX_HEADS = 4
IDX_DIM = 64
TOPK_MAX = 256
ROPE_THETA = 500000.0
ROPE_DIM = SA_HEAD_DIM // 4
INDEX_SCALE = (IDX_DIM ** -0.5) * (IDX_HEADS ** -0.5)
SA_SPLITS = (SA_WIDTH, SA_KV_HEADS * SA_HEAD_DIM, SA_KV_HEADS * SA_HEAD_DIM, IDX_HEADS * IDX_DIM, IDX_DIM, IDX_HEADS)
SA_COLS = sum(SA_SPLITS)

RET_HEADS = 4
RET_HEAD_DIM = 128
RET_WIDTH = RET_HEADS * RET_HEAD_DIM
RET_ROPE_BASE = 10000.0
RET_COLS = 4 * RET_WIDTH

N_BRANCH = 3
GATE_COLS = N_BRANCH * D_MODEL
IN_COLS = RW_COLS + SA_COLS + RET_COLS + GATE_COLS
D_FF = 4 * D_MODEL

kernel_name = 'hybrid_rwkv7_dsa_retention_stream_step'


def rmsnorm(x, g):
    xf = x.astype(F32)
    y = xf * lax.rsqrt(jnp.mean(xf * xf, axis=-1, keepdims=True) + EPS)
    return (y * g.astype(F32)).astype(x.dtype)


def split_cols(t, widths):
    return jnp.split(t, [int(c) for c in np.cumsum(widths)[:-1]], axis=-1)


def apply_rotary(x, pos, freqs):
    half = freqs.shape[0]
    rot = 2 * half
    xf = x.astype(F32)
    ang = pos.astype(F32)[:, None] * freqs[None, :]
    cos = jnp.cos(ang)[:, None, :]
    sin = jnp.sin(ang)[:, None, :]
    x1, x2 = xf[..., :half], xf[..., half:rot]
    out = jnp.concatenate([x1 * cos - x2 * sin, x2 * cos + x1 * sin, xf[..., rot:]], axis=-1)
    return out.astype(x.dtype)


def partial_rope_freqs():
    return 1.0 / (ROPE_THETA ** (jnp.arange(0, ROPE_DIM, 2, dtype=F32) / ROPE_DIM))


def retention_freqs():
    return 1.0 / (RET_ROPE_BASE ** jnp.linspace(0.0, 1.0, RET_HEAD_DIM // 2, dtype=F32))


def rwkv_scan(r, decay, k, v, kk, a, S0):
    def step(S, inp):
        r_t, w_t, k_t, v_t, kk_t, a_t = inp
        removed = jnp.einsum('bhvk,bhk->bhv', S, kk_t)
        S = (S * w_t[:, :, None, :]
             - removed[..., None] * (kk_t * a_t)[:, :, None, :]
             + v_t[..., None] * k_t[:, :, None, :])
        return S, jnp.einsum('bhvk,bhk->bhv', S, r_t)
    xs = tuple(jnp.moveaxis(t, 1, 0) for t in (r, decay, k, v, kk, a))
    S, y = lax.scan(step, S0, xs)
    return jnp.moveaxis(y, 0, 1), S


def rwkv_mixer(P, shift_prev, S0, lp):
    B, L, _ = P.shape
    prev = jnp.concatenate([shift_prev[:, None, :].astype(P.dtype), P[:, :-1]], axis=1)
    X = P + (prev - P) * lp['rwkv_mu']
    r, k, v, xw, xa, xg = split_cols(X.astype(F32), RW_SPLITS)
    w = -jax.nn.softplus(-(lp['rwkv_w0'].astype(F32) + jnp.tanh(xw) @ lp['rwkv_w2'].astype(F32))) - 0.5
    decay = jnp.exp(-jnp.exp(w))
    a = jax.nn.sigmoid(lp['rwkv_a0'].astype(F32) + xa @ lp['rwkv_a2'].astype(F32))
    g = jax.nn.sigmoid(xg) @ lp['rwkv_g2'].astype(F32)
    hv = lambda t: t.reshape(B, L, RW_HEADS, RW_HEAD_DIM)
    hp = lambda t: t.astype(F32).reshape(RW_HEADS, RW_HEAD_DIM)
    r, k, v, decay, a = hv(r), hv(k), hv(v), hv(decay), hv(a)
    kk = k * hp(lp['rwkv_k_k'])
    kk = kk / jnp.maximum(jnp.sqrt(jnp.sum(kk * kk, axis=-1, keepdims=True)), 1e-12)
    k = k * (1.0 + (a - 1.0) * hp(lp['rwkv_k_a']))
    y, S = rwkv_scan(r, decay, k, v, kk, a, S0.astype(F32))
    mean = jnp.mean(y, axis=-1, keepdims=True)
    var = jnp.mean(jnp.square(y - mean), axis=-1, keepdims=True)
    y = (y - mean) * lax.rsqrt(var + RW_GN_EPS) * hp(lp['rwkv_lnx_g'])
    y = y + jnp.sum(r * k * lp['rwkv_r_k'].astype(F32), axis=-1, keepdims=True) * v
    y = y.reshape(B, L, RW_WIDTH) * g
    return y.astype(P.dtype), S, P[:, -1]


def dsa_block(q, qi, wi, k_all, v_all, ki_all, limit, topk):
    B, Qb = q.shape[:2]
    Lk = k_all.shape[1]
    dots = jnp.einsum('bqhd,bsd->bqhs', qi.astype(F32), ki_all.astype(F32))
    score = jnp.einsum('bqhs,bqh->bqs', jax.nn.relu(dots), wi.astype(F32)) * INDEX_SCALE
    admissible = jnp.arange(Lk)[None, :] < limit[:, None]
    score = jnp.where(admissible[None], score, -jnp.inf)
    _, idx = lax.top_k(score, topk)
    valid = idx < limit[None, :, None]
    k_sel = jax.vmap(lambda t, i: t[i])(k_all, idx)
    v_sel = jax.vmap(lambda t, i: t[i])(v_all, idx)
    qg = q.astype(F32).reshape(B, Qb, SA_KV_HEADS, SA_HEADS // SA_KV_HEADS, SA_HEAD_DIM)
    logits = jnp.einsum('bqcgd,bqncd->bqcgn', qg, k_sel.astype(F32)) * (SA_HEAD_DIM ** -0.5)
    logits = jnp.where(valid[:, :, None, None, :], logits, -jnp.inf)
    p = jax.nn.softmax(logits, axis=-1)
    o = jnp.einsum('bqcgn,bqncd->bqcgd', p, v_sel.astype(F32))
    return o.reshape(B, Qb, SA_WIDTH)


def dsa_mixer(P, pos, k_past, v_past, ik_past):
    B, L, _ = P.shape
    q, k, v, qi, ki, wi = split_cols(P, SA_SPLITS)
    freqs = partial_rope_freqs()
    q = apply_rotary(q.reshape(B, L, SA_HEADS, SA_HEAD_DIM), pos, freqs)
    k = apply_rotary(k.reshape(B, L, SA_KV_HEADS, SA_HEAD_DIM), pos, freqs)
    v = v.reshape(B, L, SA_KV_HEADS, SA_HEAD_DIM)
    qi = apply_rotary(qi.reshape(B, L, IDX_HEADS, IDX_DIM), pos, freqs)
    ki = apply_rotary(ki.reshape(B, L, 1, IDX_DIM), pos, freqs)[:, :, 0]
    k_all = jnp.concatenate([k_past.astype(k.dtype), k], axis=1)
    v_all = jnp.concatenate([v_past.astype(v.dtype), v], axis=1)
    ki_all = jnp.concatenate([ik_past.astype(ki.dtype), ki], axis=1)
    Lk = k_all.shape[1]
    topk = min(TOPK_MAX, Lk // 4)
    limit = jnp.minimum((pos // CHUNK + 1) * CHUNK, Lk)
    qb = min(Q_BLOCK, L)
    nb = L // qb
    blocks = lambda t: jnp.moveaxis(t.reshape((B, nb, qb) + t.shape[2:]), 1, 0)

    def attend(args):
        q_b, qi_b, wi_b, lim_b = args
        return dsa_block(q_b, qi_b, wi_b, k_all, v_all, ki_all, lim_b, topk)

    o = lax.map(attend, (blocks(q), blocks(qi), blocks(wi), limit.reshape(nb, qb)))
    y = jnp.moveaxis(o, 0, 1).reshape(B, L, SA_WIDTH)
    return y.astype(P.dtype), k, v, ki


def retention_chunk(q, k, v, S, log_gamma):
    c = q.shape[1]
    idx = jnp.arange(c, dtype=F32)
    dist = jnp.abs(idx[:, None] - idx[None, :])
    dmask = jnp.exp(dist[None] * log_gamma[:, None, None])
    scores = jnp.einsum('bqhd,bmhd->bhqm', q, k) * dmask[None]
    intra = jnp.einsum('bhqm,bmhe->bqhe', scores, v)
    q_decay = jnp.exp((idx[:, None] + 1.0) * log_gamma[None, :])
    cross = jnp.einsum('bqhd,bhde->bqhe', q, S) * q_decay[None, :, :, None]
    k_decay = jnp.exp((c - 1.0 - idx)[:, None] * log_gamma[None, :])
    S_new = (S * jnp.exp(c * log_gamma)[None, :, None, None]
             + jnp.einsum('bmhd,bmhe->bhde', k * k_decay[None, :, :, None], v))
    return intra + cross, S_new


def retention_mixer(P, pos, S0):
    B, L, _ = P.shape
    q, k, v, g = split_cols(P, (RET_WIDTH,) * 4)
    hv = lambda t: t.astype(F32).reshape(B, L, RET_HEADS, RET_HEAD_DIM)
    freqs = retention_freqs()
    q = apply_rotary(hv(q), pos, freqs)
    k = apply_rotary(hv(k), pos, freqs) * (RET_HEAD_DIM ** -0.5)
    v = hv(v)
    log_gamma = jnp.log1p(-jnp.exp2(-5.0 - jnp.arange(RET_HEADS, dtype=F32)))
    c = min(CHUNK, L)
    nc = L // c
    chunks = lambda t: jnp.moveaxis(t.reshape(B, nc, c, RET_HEADS, t.shape[-1]), 1, 0)

    def step(S, inp):
        qc, kc, vc = inp
        o, S = retention_chunk(qc, kc, vc, S, log_gamma)
        return S, o

    S, o = lax.scan(step, S0.astype(F32), (chunks(q), chunks(k), chunks(v)))
    o = jnp.moveaxis(o, 0, 1).reshape(B, L, RET_HEADS, RET_HEAD_DIM)
    o = o * lax.rsqrt(jnp.mean(o * o, axis=-1, keepdims=True) + EPS)
    y = jax.nn.silu(g.astype(F32)) * o.reshape(B, L, RET_WIDTH)
    return y.astype(P.dtype), S


def trunk_layer(x, past, lp):
    k_past, v_past, ik_past, s_rw, shift_rw, s_ret = past
    B, L, _ = x.shape
    pos = k_past.shape[1] + jnp.arange(L, dtype=jnp.int32)
    h = rmsnorm(x, lp['norm1_g'])
    P = h @ lp['w_in']
    P_rw, P_sa, P_ret, P_gate = split_cols(P, (RW_COLS, SA_COLS, RET_COLS, GATE_COLS))
    y_rw, s_rw_new, shift_new = rwkv_mixer(P_rw, shift_rw, s_rw, lp)
    y_sa, k_new, v_new, ik_new = dsa_mixer(P_sa, pos, k_past, v_past, ik_past)
    y_ret, s_ret_new = retention_mixer(P_ret, pos, s_ret)
    gates = jax.nn.sigmoid(P_gate.astype(F32)).astype(x.dtype).reshape(B, L, N_BRANCH, D_MODEL)
    merged = (gates[:, :, 0] * (y_rw @ lp['w_br_rwkv'])
              + gates[:, :, 1] * (y_sa @ lp['w_br_dsa'])
              + gates[:, :, 2] * (y_ret @ lp['w_br_ret']))
    x = x + merged @ lp['w_o']
    h2 = rmsnorm(x, lp['norm2_g'])
    x = x + jnp.square(jax.nn.relu(h2 @ lp['w_up'])) @ lp['w_down']
    return x, (k_new, v_new, ik_new, s_rw_new, shift_new, s_ret_new)


def setup_inputs(seed: int = 0) -> dict:
    key = jax.random.key(seed)
    ks = iter(jax.random.split(key, 32))
    nrm = lambda shape, scale: jax.random.normal(next(ks), shape, F32) * scale
    D = D_MODEL
    return {
        'x_prompt': nrm((BATCH, SEQ, D), 1.0),
        'x_sample': nrm((DEC_BATCH, DEC_SEQ, D), 1.0),
        'cache_dsa_k': nrm((DEPTH, DEC_BATCH, PAST_LEN, SA_KV_HEADS, SA_HEAD_DIM), 1.0),
        'cache_dsa_v': nrm((DEPTH, DEC_BATCH, PAST_LEN, SA_KV_HEADS, SA_HEAD_DIM), 1.0),
        'cache_dsa_ik': nrm((DEPTH, DEC_BATCH, PAST_LEN, IDX_DIM), 1.0),
        'state_rwkv': nrm((DEPTH, DEC_BATCH, RW_HEADS, RW_HEAD_DIM, RW_HEAD_DIM), 1.0),
        'state_rwkv_shift': nrm((DEPTH, DEC_BATCH, RW_COLS), 1.0),
        'state_ret': nrm((DEPTH, DEC_BATCH, RET_HEADS, RET_HEAD_DIM, RET_HEAD_DIM), 1.0),
        'norm1_g': 1.0 + nrm((DEPTH, D), 0.05),
        'w_in': nrm((DEPTH, D, IN_COLS), D ** -0.5),
        'rwkv_mu': jax.random.uniform(next(ks), (DEPTH, RW_COLS), F32),
        'rwkv_w0': nrm((DEPTH, RW_WIDTH), 0.5),
        'rwkv_w2': nrm((DEPTH, RW_DECAY_LORA, RW_WIDTH), 0.5 * RW_DECAY_LORA ** -0.5),
        'rwkv_a0': nrm((DEPTH, RW_WIDTH), 0.1),
        'rwkv_a2': nrm((DEPTH, RW_ICLR_LORA, RW_WIDTH), 0.5 * RW_ICLR_LORA ** -0.5),
        'rwkv_g2': nrm((DEPTH, RW_GATE_LORA, RW_WIDTH), RW_GATE_LORA ** -0.5),
        'rwkv_k_k': 0.85 + nrm((DEPTH, RW_WIDTH), 0.1),
        'rwkv_k_a': 1.0 + nrm((DEPTH, RW_WIDTH), 0.1),
        'rwkv_r_k': nrm((DEPTH, RW_HEADS, RW_HEAD_DIM), 0.1),
        'rwkv_lnx_g': 1.0 + nrm((DEPTH, RW_WIDTH), 0.05),
        'w_br_rwkv': nrm((DEPTH, RW_WIDTH, D), RW_WIDTH ** -0.5),
        'w_br_dsa': nrm((DEPTH, SA_WIDTH, D), SA_WIDTH ** -0.5),
        'w_br_ret': nrm((DEPTH, RET_WIDTH, D), RET_WIDTH ** -0.5),
        'w_o': nrm((DEPTH, D, D), D ** -0.5),
        'norm2_g': 1.0 + nrm((DEPTH, D), 0.05),
        'w_up': nrm((DEPTH, D, D_FF), D ** -0.5),
        'w_down': nrm((DEPTH, D_FF, D), D_FF ** -0.5),
        'final_norm_g': 1.0 + nrm((D,), 0.05),
    }


def reference(x_prompt, x_sample, cache_dsa_k, cache_dsa_v, cache_dsa_ik, state_rwkv,
              state_rwkv_shift, state_ret, norm1_g, w_in, rwkv_mu, rwkv_w0, rwkv_w2,
              rwkv_a0, rwkv_a2, rwkv_g2, rwkv_k_k, rwkv_k_a, rwkv_r_k, rwkv_lnx_g,
              w_br_rwkv, w_br_dsa, w_br_ret, w_o, norm2_g, w_up, w_down, final_norm_g):
    params = {
        'norm1_g': norm1_g, 'w_in': w_in, 'rwkv_mu': rwkv_mu, 'rwkv_w0': rwkv_w0,
        'rwkv_w2': rwkv_w2, 'rwkv_a0': rwkv_a0, 'rwkv_a2': rwkv_a2, 'rwkv_g2': rwkv_g2,
        'rwkv_k_k': rwkv_k_k, 'rwkv_k_a': rwkv_k_a, 'rwkv_r_k': rwkv_r_k,
        'rwkv_lnx_g': rwkv_lnx_g, 'w_br_rwkv': w_br_rwkv, 'w_br_dsa': w_br_dsa,
        'w_br_ret': w_br_ret, 'w_o': w_o, 'norm2_g': norm2_g, 'w_up': w_up, 'w_down': w_down,
    }
    Bp = x_prompt.shape[0]
    dt = x_prompt.dtype
    xp, xs = x_prompt, x_sample
    p_states = [[] for _ in range(6)]
    s_states = [[] for _ in range(6)]
    for i in range(DEPTH):
        lp = {name: arr[i] for name, arr in params.items()}
        past_p = (jnp.zeros((Bp, 0, SA_KV_HEADS, SA_HEAD_DIM), dt),
                  jnp.zeros((Bp, 0, SA_KV_HEADS, SA_HEAD_DIM), dt),
                  jnp.zeros((Bp, 0, IDX_DIM), dt),
                  jnp.zeros((Bp, RW_HEADS, RW_HEAD_DIM, RW_HEAD_DIM), F32),
                  jnp.zeros((Bp, RW_COLS), dt),
                  jnp.zeros((Bp, RET_HEADS, RET_HEAD_DIM, RET_HEAD_DIM), F32))
        past_s = (cache_dsa_k[i], cache_dsa_v[i], cache_dsa_ik[i],
                  state_rwkv[i], state_rwkv_shift[i], state_ret[i])
        xp, new_p = trunk_layer(xp, past_p, lp)
        xs, new_s = trunk_layer(xs, past_s, lp)
        for j in range(6):
            p_states[j].append(new_p[j])
            s_states[j].append(new_s[j])
    y_prompt = rmsnorm(xp, final_norm_g)
    y_sample = rmsnorm(xs, final_norm_g)
    p_k, p_v, p_ik, p_rwkv, p_shift, p_ret = [jnp.stack(t, axis=0) for t in p_states]
    s_k, s_v, s_ik, s_rwkv, s_shift, s_ret = [jnp.stack(t, axis=0) for t in s_states]
    return (y_prompt, y_sample, p_k, p_v, p_ik, p_rwkv, p_shift, p_ret,
            s_k, s_v, s_ik, s_rwkv, s_shift, s_ret)
```

```python
import functools

import numpy as np
import jax
import jax.numpy as jnp
from jax import lax
from jax.experimental import pallas as pl
from jax.experimental.pallas import tpu as pltpu

F32 = jnp.float32
BF16 = jnp.bfloat16
I32 = jnp.int32

D_MODEL = 1024
CHUNK = 64
Q_BLOCK = 128
EPS = 1e-6

RW_HEADS = 8
RW_HEAD_DIM = 64
RW_WIDTH = RW_HEADS * RW_HEAD_DIM
RW_DECAY_LORA = 64
RW_ICLR_LORA = 64
RW_GATE_LORA = 128
RW_LORA = RW_DECAY_LORA + RW_ICLR_LORA + RW_GATE_LORA
RW_COLS = 3 * RW_WIDTH + RW_LORA
RW_GN_EPS = 64e-5

SA_HEADS = 8
SA_KV_HEADS = 2
SA_HEAD_DIM = 64
SA_WIDTH = SA_HEADS * SA_HEAD_DIM
SA_KV_WIDTH = SA_KV_HEADS * SA_HEAD_DIM
IDX_HEADS = 4
IDX_DIM = 64
IDX_WIDTH = IDX_HEADS * IDX_DIM
TOPK_MAX = 256
ROPE_THETA = 500000.0
ROPE_DIM = SA_HEAD_DIM // 4
INDEX_SCALE = (IDX_DIM ** -0.5) * (IDX_HEADS ** -0.5)
SA_COLS = SA_WIDTH + 2 * SA_KV_WIDTH + IDX_WIDTH + IDX_DIM + IDX_HEADS

RET_HEADS = 4
RET_HEAD_DIM = 128
RET_WIDTH = RET_HEADS * RET_HEAD_DIM
RET_ROPE_BASE = 10000.0
RET_COLS = 4 * RET_WIDTH

N_BRANCH = 3
GATE_COLS = N_BRANCH * D_MODEL
IN_COLS = RW_COLS + SA_COLS + RET_COLS + GATE_COLS
D_FF = 4 * D_MODEL

LANES = 128

P_GATE = 0
P_RET = P_GATE + GATE_COLS
P_RKV = P_RET + RET_COLS
P_SAQ = P_RKV + 3 * RW_WIDTH
P_LORA = P_SAQ + SA_WIDTH
P_QI = P_LORA + RW_LORA
P_SAK = P_QI + IDX_WIDTH
P_SAV = P_SAK + SA_KV_WIDTH
P_KIWI = P_SAV + SA_KV_WIDTH
P_COLS = 8192
INT_MIN = -2 ** 31
NEG_BIG = -1e30
VMEM_LIMIT = 56 * 1024 * 1024


def _bf(x):
    return x.astype(BF16)


def _dot(a, b):
    return jnp.dot(a, b, preferred_element_type=F32)


def _dot_nt(a, b):
    return lax.dot_general(a, b, (((1,), (1,)), ((), ())), preferred_element_type=F32)


def _dot_tn(a, b):
    return lax.dot_general(a, b, (((0,), (0,)), ((), ())), preferred_element_type=F32)


def _dot_split2(a_exact, x):
    hi = _bf(x)
    lo = _bf(x - hi.astype(F32))
    return _dot(a_exact, hi) + _dot(a_exact, lo)


def _params(sem):
    return pltpu.CompilerParams(dimension_semantics=sem, vmem_limit_bytes=VMEM_LIMIT)


def _in_proj_kernel(x_ref, g_ref, w_ref, o_ref, h_ref):
    @pl.when(pl.program_id(1) == 0)
    def _():
        x = x_ref[...]
        ms = jnp.mean(x * x, axis=-1, keepdims=True)
        h_ref[...] = _bf(x * lax.rsqrt(ms + EPS) * g_ref[...])

    o_ref[...] = _dot(h_ref[...], w_ref[...])


def _in_proj(x2d, g, w_pad):
    T = x2d.shape[0]
    tm = min(1024, T)
    tn = 1024
    return pl.pallas_call(
        _in_proj_kernel,
        grid=(T // tm, P_COLS // tn),
        in_specs=[pl.BlockSpec((tm, D_MODEL), lambda i, j: (i, 0)),
                  pl.BlockSpec((1, D_MODEL), lambda i, j: (0, 0)),
                  pl.BlockSpec((D_MODEL, tn), lambda i, j: (0, j))],
        out_specs=pl.BlockSpec((tm, tn), lambda i, j: (i, j)),
        out_shape=jax.ShapeDtypeStruct((T, P_COLS), F32),
        scratch_shapes=[pltpu.VMEM((tm, D_MODEL), BF16)],
        compiler_params=_params(("parallel", "arbitrary")),
        name="in_proj",
    )(x2d, g, w_pad)


def _rot_slab(x, c, s_lo, s_hi, shift):
    return x * c + pltpu.roll(x, LANES - shift, 1) * s_lo + pltpu.roll(x, shift, 1) * s_hi


def _sa_prep_kernel(q_ref, qi_ref, k_ref, kw_ref, c_ref, s1_ref, s2_ref, ck_ref, s1k_ref, s2k_ref,
                    qo_ref, qio_ref, ko_ref, kwo_ref):
    c, s1, s2 = c_ref[...], s1_ref[...], s2_ref[...]
    half = ROPE_DIM // 2
    scale = SA_HEAD_DIM ** -0.5
    for s in range(SA_WIDTH // LANES):
        sl = slice(s * LANES, (s + 1) * LANES)
        qo_ref[:, sl] = _rot_slab(q_ref[:, sl], c, s1, s2, half) * scale
    for s in range(IDX_WIDTH // LANES):
        sl = slice(s * LANES, (s + 1) * LANES)
        qio_ref[:, sl] = _rot_slab(qi_ref[:, sl], c, s1, s2, half)
    ko_ref[...] = _rot_slab(k_ref[...], c, s1, s2, half)
    kwo_ref[...] = _rot_slab(kw_ref[...], ck_ref[...], s1k_ref[...], s2k_ref[...], half)


def _sa_prep(P, L, tabs):
    T = P.shape[0]
    tm = min(512, L)
    nl = L // tm
    tab_spec = pl.BlockSpec((tm, LANES), lambda i: (i % nl, 0))
    return pl.pallas_call(
        _sa_prep_kernel,
        grid=(T // tm,),
        in_specs=[pl.BlockSpec((tm, SA_WIDTH), lambda i: (i, P_SAQ // SA_WIDTH)),
                  pl.BlockSpec((tm, IDX_WIDTH), lambda i: (i, P_QI // IDX_WIDTH)),
                  pl.BlockSpec((tm, LANES), lambda i: (i, P_SAK // LANES)),
                  pl.BlockSpec((tm, LANES), lambda i: (i, P_KIWI // LANES))] + [tab_spec] * 6,
        out_specs=[pl.BlockSpec((tm, SA_WIDTH), lambda i: (i, 0)),
                   pl.BlockSpec((tm, IDX_WIDTH), lambda i: (i, 0)),
                   pl.BlockSpec((tm, LANES), lambda i: (i, 0)),
                   pl.BlockSpec((tm, LANES), lambda i: (i, 0))],
        out_shape=[jax.ShapeDtypeStruct((T, SA_WIDTH), F32),
                   jax.ShapeDtypeStruct((T, IDX_WIDTH), F32),
                   jax.ShapeDtypeStruct((T, LANES), F32),
                   jax.ShapeDtypeStruct((T, LANES), F32)],
        compiler_params=_params(("parallel",)),
        name="sa_prep",
    )(P, P, P, P, *tabs)


def _sa_tables(pos):
    half = ROPE_DIM // 2
    freqs = 1.0 / (ROPE_THETA ** (jnp.arange(0, ROPE_DIM, 2, dtype=F32) / ROPE_DIM))
    ang = pos.astype(F32)[:, None] * freqs[None, :]
    cos, sin = jnp.cos(ang), jnp.sin(ang)
    n = pos.shape[0]
    pad = SA_HEAD_DIM - ROPE_DIM
    c_head = jnp.concatenate([cos, cos, jnp.ones((n, pad), F32)], axis=1)
    s1_head = jnp.concatenate([-sin, jnp.zeros((n, half + pad), F32)], axis=1)
    s2_head = jnp.concatenate([jnp.zeros((n, half), F32), sin, jnp.zeros((n, pad), F32)], axis=1)
    one, zero = jnp.ones((n, SA_HEAD_DIM), F32), jnp.zeros((n, SA_HEAD_DIM), F32)
    two = lambda t: jnp.concatenate([t, t], axis=1)
    return (two(c_head), two(s1_head), two(s2_head),
            jnp.concatenate([c_head, one], axis=1), jnp.concatenate([s1_head, zero], axis=1),
            jnp.concatenate([s2_head, zero], axis=1))


def _softplus(u):
    return jnp.maximum(u, 0.0) + jnp.log(1.0 + jnp.exp(-jnp.abs(u)))


def _rwkv_kernel(r_ref, k_ref, v_ref, lo_ref, shr_ref, shk_ref, shv_ref, shlo_ref, s0_ref,
                 mur_ref, muk_ref, muv_ref, mulo_ref, w0_ref, w2_ref, a0_ref, a2_ref, g2_ref,
                 kk_ref, ka_ref, rk_ref, lnx_ref, e_ref, tin_ref, tsuf_ref,
                 y_ref, so_ref,
                 cr_ref, ck_ref, cv_ref, clo_ref, s_ref,
                 kap_ref, rt_ref, bh_ref, kh_ref, bt_ref, kt_ref, vv_ref, gc_ref, yy_ref):
    c = pl.program_id(1)
    TT = r_ref.shape[0]
    nch = TT // CHUNK
    N = RW_HEAD_DIM

    @pl.when(c == 0)
    def _():
        cr_ref[0:1, :] = shr_ref[0]
        ck_ref[0:1, :] = shk_ref[0]
        cv_ref[0:1, :] = shv_ref[0]
        clo_ref[0:1, :] = shlo_ref[0]
        s_ref[...] = s0_ref[0]

    def lerp(p_ref, carry_ref, mu_ref):
        p = p_ref[...]
        rolled = pltpu.roll(p, 1, 0)
        row0 = lax.broadcasted_iota(I32, p.shape, 0) == 0
        prev = jnp.where(row0, carry_ref[0:1, :], rolled)
        carry_ref[0:1, :] = p[TT - 1:TT, :]
        return p + (prev - p) * mu_ref[...]

    xr = lerp(r_ref, cr_ref, mur_ref)
    xk = lerp(k_ref, ck_ref, muk_ref)
    xv = lerp(v_ref, cv_ref, muv_ref)
    xlo = lerp(lo_ref, clo_ref, mulo_ref)
    xw = xlo[:, :RW_DECAY_LORA]
    xa = xlo[:, RW_DECAY_LORA:RW_DECAY_LORA + RW_ICLR_LORA]
    xg = xlo[:, RW_DECAY_LORA + RW_ICLR_LORA:]

    z = w0_ref[...] + _dot(_bf(jnp.tanh(xw)), w2_ref[...])
    w = -_softplus(-z) - 0.5
    ld = -jnp.exp(w)
    a = jax.nn.sigmoid(a0_ref[...] + _dot(_bf(xa), a2_ref[...]))
    gate = _dot(_bf(jax.nn.sigmoid(xg)), g2_ref[...])
    e_blk = e_ref[...]
    kk = xk * kk_ref[...]
    kk = kk / jnp.maximum(jnp.sqrt(_dot(_bf(kk * kk), e_blk)), 1e-12)
    k2 = xk * (1.0 + (a - 1.0) * ka_ref[...])
    bb = kk * a
    bonus = _dot(_bf(xr * k2 * rk_ref[...]), e_blk) * xv

    lin = _dot_split2(tin_ref[...], ld)
    lsuf = _dot_split2(tsuf_ref[...], ld)
    e_in = jnp.exp(lin)
    e_ninv = jnp.exp(-lin)
    e_suf = jnp.exp(lsuf)
    kap_ref[...] = kk * jnp.exp(lin - ld)
    rt_ref[...] = xr * e_in
    bh_ref[...] = bb * e_ninv
    kh_ref[...] = k2 * e_ninv
    bt_ref[...] = bb * e_suf
    kt_ref[...] = k2 * e_suf
    vv_ref[...] = xv
    gc_ref[...] = jnp.exp(lin + lsuf)

    ri = lax.broadcasted_iota(I32, (CHUNK, CHUNK), 0)
    ci = lax.broadcasted_iota(I32, (CHUNK, CHUNK), 1)
    strict = ri > ci
    incl = ri >= ci
    eye = (ri == ci).astype(F32)

    def chunk_body(ch, carry):
        rows = pl.ds(pl.multiple_of(ch * CHUNK, CHUNK), CHUNK)
        for h in range(RW_HEADS):
            sl = slice(h * N, (h + 1) * N)
            kap, rt = kap_ref[rows, sl], rt_ref[rows, sl]
            bh, kh = bh_ref[rows, sl], kh_ref[rows, sl]
            vh = vv_ref[rows, sl]
            lhs = _bf(jnp.concatenate([kap, rt], axis=0))
            gmat = _dot_nt(lhs, _bf(jnp.concatenate([bh, kh], axis=0)))
            n_ab = jnp.where(strict, gmat[:CHUNK, :CHUNK], 0.0)
            m_ak = jnp.where(strict, gmat[:CHUNK, CHUNK:], 0.0)
            m_rb = jnp.where(incl, gmat[CHUNK:, :CHUNK], 0.0)
            m_rk = jnp.where(incl, gmat[CHUNK:, CHUNK:], 0.0)
            s_h = s_ref[h]
            q0 = _dot_nt(lhs, _bf(s_h))
            x_inv = eye - n_ab
            pw = n_ab
            for _ in range(5):
                pwb = _bf(pw)
                pw = _dot(pwb, pwb)
                x_inv = x_inv + _dot(_bf(x_inv), _bf(pw))
            rhs = q0[:CHUNK] + _dot(_bf(m_ak), _bf(vh))
            rho = _dot(_bf(x_inv), _bf(rhs))
            u = _bf(jnp.concatenate([-rho, vh], axis=0))
            y_h = q0[CHUNK:] + _dot(_bf(jnp.concatenate([m_rb, m_rk], axis=1)), u)
            yy_ref[rows, sl] = y_h
            zed = _bf(jnp.concatenate([bt_ref[rows, sl], kt_ref[rows, sl]], axis=0))
            gch = gc_ref[pl.ds(pl.multiple_of(ch * CHUNK, CHUNK), 1), sl]
            s_ref[h] = s_h * gch + _dot_tn(u, zed)
        return carry

    lax.fori_loop(0, nch, chunk_body, 0)

    y = yy_ref[...]
    mean = _dot(_bf(y), e_blk) * (1.0 / N)
    d = y - mean
    var = _dot(_bf(d * d), e_blk) * (1.0 / N)
    yn = d * lax.rsqrt(var + RW_GN_EPS) * lnx_ref[...]
    y_ref[...] = (yn + bonus) * gate

    @pl.when(c == pl.num_programs(1) - 1)
    def _():
        so_ref[0] = s_ref[...]


def _rwkv(P, B, L, shift_prev, s0, lp):
    T = P.shape[0]
    TT = min(256, L)
    nt = L // TT
    W = RW_WIDTH
    row = lambda t: t.reshape(1, -1).astype(F32)
    mu = lp['rwkv_mu']
    sh = shift_prev.astype(F32)
    pieces = lambda t: (t[..., 0:W], t[..., W:2 * W], t[..., 2 * W:3 * W], t[..., 3 * W:])
    mu_r, mu_k, mu_v, mu_lo = [row(t) for t in pieces(mu)]
    sh_r, sh_k, sh_v, sh_lo = [t.reshape(B, 1, -1) for t in pieces(sh)]
    hid = jnp.arange(W) // RW_HEAD_DIM
    e_blk = (hid[:, None] == hid[None, :]).astype(BF16)
    ti = jnp.arange(TT)
    same = (ti[:, None] // CHUNK) == (ti[None, :] // CHUNK)
    tri_in = (same & (ti[None, :] <= ti[:, None])).astype(BF16)
    tri_suf = (same & (ti[None, :] > ti[:, None])).astype(BF16)

    tok = lambda w, blk: pl.BlockSpec((TT, w), lambda b, c: (b * nt + c, blk))
    full = lambda shape: pl.BlockSpec(shape, lambda b, c: (0,) * len(shape))
    shs = lambda w: pl.BlockSpec((1, 1, w), lambda b, c: (b, 0, 0))
    st_spec = pl.BlockSpec((1, RW_HEADS, RW_HEAD_DIM, RW_HEAD_DIM), lambda b, c: (b, 0, 0, 0))
    big = lambda: pltpu.VMEM((TT, W), F32)
    y, s_out = pl.pallas_call(
        _rwkv_kernel,
        grid=(B, nt),
        in_specs=[tok(W, P_RKV // W), tok(W, P_RKV // W + 1), tok(W, P_RKV // W + 2),
                  tok(RW_LORA, P_LORA // RW_LORA),
                  shs(W), shs(W), shs(W), shs(RW_LORA), st_spec,
                  full((1, W)), full((1, W)), full((1, W)), full((1, RW_LORA)),
                  full((1, W)), full((RW_DECAY_LORA, W)), full((1, W)), full((RW_ICLR_LORA, W)),
                  full((RW_GATE_LORA, W)), full((1, W)), full((1, W)), full((1, W)), full((1, W)),
                  full((W, W)), full((TT, TT)), full((TT, TT))],
        out_specs=[pl.BlockSpec((TT, W), lambda b, c: (b * nt + c, 0)), st_spec],
        out_shape=[jax.ShapeDtypeStruct((T, W), F32),
                   jax.ShapeDtypeStruct((B, RW_HEADS, RW_HEAD_DIM, RW_HEAD_DIM), F32)],
        scratch_shapes=[pltpu.VMEM((8, W), F32), pltpu.VMEM((8, W), F32), pltpu.VMEM((8, W), F32),
                        pltpu.VMEM((8, RW_LORA), F32),
                        pltpu.VMEM((RW_HEADS, RW_HEAD_DIM, RW_HEAD_DIM), F32)] + [big() for _ in range(9)],
        compiler_params=_params(("parallel", "arbitrary")),
        name="rwkv",
    )(P, P, P, P, sh_r, sh_k, sh_v, sh_lo, s0.astype(F32),
      mu_r, mu_k, mu_v, mu_lo, row(lp['rwkv_w0']), _bf(lp['rwkv_w2']), row(lp['rwkv_a0']),
      _bf(lp['rwkv_a2']), _bf(lp['rwkv_g2']), row(lp['rwkv_k_k']), row(lp['rwkv_k_a']),
      row(lp['rwkv_r_k']), row(lp['rwkv_lnx_g']), e_blk, tri_in, tri_suf)
    return y, s_out


def _dsa_kernel(q_ref, qi_ref, kw_ref, k_ref, v_ref, ki_ref, y_ref,
                kb_ref, vb_ref, kib_ref, keys_ref, bias_ref, logit_ref, pstar_ref,
                *, q_off, past, lk_real, topk, kc):
    i = pl.program_id(1)
    qb = q_ref.shape[1]
    lk = k_ref.shape[1]
    nk = lk // kc
    kf = float(topk)

    @pl.when(i == 0)
    def _():
        kb_ref[...] = _bf(k_ref[0])
        vb_ref[...] = _bf(v_ref[0])
        kib_ref[...] = _bf(ki_ref[0])

    rowpos = past + q_off + i * qb + lax.broadcasted_iota(I32, (qb, 1), 0)
    limit = jnp.minimum((rowpos // CHUNK + 1) * CHUNK, lk_real)
    lane_idx = lax.broadcasted_iota(I32, (qb, kc), 1)
    kw = kw_ref[0]
    qi = qi_ref[0]
    qi_h = [_bf(qi[:, h * IDX_DIM:(h + 1) * IDX_DIM]) for h in range(IDX_HEADS)]
    wi_h = [jnp.broadcast_to(kw[:, IDX_DIM + h:IDX_DIM + h + 1] * INDEX_SCALE, (qb, kc))
            for h in range(IDX_HEADS)]

    def lane_fold(m, op):
        acc = m[:, 0:LANES]
        for t in range(1, kc // LANES):
            acc = op(acc, m[:, t * LANES:(t + 1) * LANES])
        return acc

    for j in range(nk):
        ks = slice(j * kc, (j + 1) * kc)
        ki_j = kib_ref[ks, 0:IDX_DIM]
        s = jnp.zeros((qb, kc), F32)
        for h in range(IDX_HEADS):
            s = s + jnp.maximum(_dot_nt(qi_h[h], ki_j), 0.0) * wi_h[h]
        s = jnp.where(s == 0.0, 0.0, s)
        bits = lax.bitcast_convert_type(s, I32)
        key = bits ^ ((bits >> 31) & 0x7FFFFFFF)
        adm = (lane_idx + j * kc) < limit
        keys_ref[:, ks] = jnp.where(adm, key, INT_MIN)

    def count(pred):
        acc = jnp.zeros((qb, LANES), F32)
        for j in range(nk):
            ks = slice(j * kc, (j + 1) * kc)
            acc = acc + lane_fold(jnp.where(pred(keys_ref[:, ks], j), 1.0, 0.0), jnp.add)
        return jnp.sum(acc, axis=1, keepdims=True)

    def bit_body(it, prefix):
        cand = prefix | lax.shift_left(jnp.int32(1), 31 - it)
        thr = cand ^ INT_MIN
        cnt = count(lambda kj, j: kj >= thr)
        return jnp.where(cnt >= kf, cand, prefix)

    prefix = lax.fori_loop(0, 32, bit_body, jnp.zeros((qb, 1), I32))
    tau = prefix ^ INT_MIN
    cnt_ge = count(lambda kj, j: kj >= tau)
    cnt_gt = count(lambda kj, j: kj > tau)
    need = kf - cnt_gt
    excess = jnp.logical_and(cnt_ge > kf, tau != INT_MIN)
    big_idx = jnp.int32(2 ** 30)
    pstar_ref[...] = jnp.full(pstar_ref.shape, big_idx, I32)

    @pl.when(jnp.max(jnp.where(excess, 1.0, 0.0)) > 0.0)
    def _():
        nbits = max(1, int(lk - 1).bit_length())

        def idx_body(it, p):
            cand = p | lax.shift_left(jnp.int32(1), nbits - 1 - it)
            g = count(lambda kj, j: jnp.logical_and(kj == tau, (lane_idx + j * kc) < cand))
            return jnp.where(g < need, cand, p)

        p = lax.fori_loop(0, nbits, idx_body, jnp.zeros((qb, 1), I32))
        pstar_ref[...] = jnp.broadcast_to(jnp.where(excess, p, big_idx), pstar_ref.shape)

    pstar = pstar_ref[:, 0:1]
    for j in range(nk):
        ks = slice(j * kc, (j + 1) * kc)
        kj = keys_ref[:, ks]
        idx = lane_idx + j * kc
        sel = jnp.logical_or(kj > tau, jnp.logical_and(kj == tau, idx <= pstar))
        sel = jnp.logical_and(sel, idx < limit)
        bias_ref[:, ks] = jnp.where(sel, 0.0, NEG_BIG)

    q = q_ref[0]
    outs = []
    group = SA_HEADS // SA_KV_HEADS
    for h in range(SA_HEADS):
        cs = slice((h // group) * SA_HEAD_DIM, (h // group + 1) * SA_HEAD_DIM)
        qh = _bf(q[:, h * SA_HEAD_DIM:(h + 1) * SA_HEAD_DIM])
        m_acc = jnp.full((qb, LANES), NEG_BIG, F32)
        for j in range(nk):
            ks = slice(j * kc, (j + 1) * kc)
            logit = _dot_nt(qh, kb_ref[ks, cs]) + bias_ref[:, ks]
            logit_ref[:, ks] = logit
            m_acc = jnp.maximum(m_acc, lane_fold(logit, jnp.maximum))
        m = jnp.max(m_acc, axis=1, keepdims=True)
        s_acc = jnp.zeros((qb, LANES), F32)
        o = jnp.zeros((qb, SA_HEAD_DIM), F32)
        for j in range(nk):
            ks = slice(j * kc, (j + 1) * kc)
            p = jnp.exp(logit_ref[:, ks] - m)
            s_acc = s_acc + lane_fold(p, jnp.add)
            o = o + _dot(_bf(p), vb_ref[ks, cs])
        outs.append(o / jnp.sum(s_acc, axis=1, keepdims=True))
    y_ref[0] = jnp.concatenate(outs, axis=1)


def _dsa_call(q3, qi3, kw3, k_all, v_all, ki_all, v_blk, *, q_off, n_q, lk, past, lk_real, topk, qb, kc):
    B = q3.shape[0]
    off = q_off // qb
    kern = functools.partial(_dsa_kernel, q_off=q_off, past=past, lk_real=lk_real, topk=topk, kc=kc)
    return pl.pallas_call(
        kern,
        grid=(B, n_q // qb),
        in_specs=[pl.BlockSpec((1, qb, SA_WIDTH), lambda b, i: (b, off + i, 0)),
                  pl.BlockSpec((1, qb, IDX_WIDTH), lambda b, i: (b, off + i, 0)),
                  pl.BlockSpec((1, qb, LANES), lambda b, i: (b, off + i, 0)),
                  pl.BlockSpec((1, lk, LANES), lambda b, i: (b, 0, 0)),
                  pl.BlockSpec((1, lk, LANES), lambda b, i: (b, 0, v_blk)),
                  pl.BlockSpec((1, lk, LANES), lambda b, i: (b, 0, 0))],
        out_specs=pl.BlockSpec((1, qb, SA_WIDTH), lambda b, i: (b, i, 0)),
        out_shape=jax.ShapeDtypeStruct((B, n_q, SA_WIDTH), F32),
        scratch_shapes=[pltpu.VMEM((lk, LANES), BF16), pltpu.VMEM((lk, LANES), BF16),
                        pltpu.VMEM((lk, LANES), BF16),
                        pltpu.VMEM((qb, lk), I32), pltpu.VMEM((qb, lk), F32), pltpu.VMEM((qb, lk), F32),
                        pltpu.VMEM((qb, LANES), I32)],
        compiler_params=_params(("parallel", "arbitrary")),
        name="dsa",
    )(q3, qi3, kw3, k_all, v_all, ki_all)


def _round_up(x, m):
    return (x + m - 1) // m * m


def _dsa(P, B, L, past, q_rot, qi_rot, k_rot, kw_rot, k_past, v_past, ik_past):
    lk_real = past + L
    topk = min(TOPK_MAX, lk_real // 4)
    qb = min(Q_BLOCK, L)
    q3 = q_rot.reshape(B, L, SA_WIDTH)
    qi3 = qi_rot.reshape(B, L, IDX_WIDTH)
    kw3 = kw_rot.reshape(B, L, LANES)
    k3 = k_rot.reshape(B, L, LANES)
    common = dict(past=past, lk_real=lk_real, topk=topk, qb=qb)
    if past == 0:
        ncls = 4 if (L % 4 == 0 and (L // 4) % 512 == 0) else 1
        cl = L // ncls
        kc = 512 if cl % 512 == 0 else LANES
        P3 = P.reshape(B, L, P_COLS)
        ys = [_dsa_call(q3, qi3, kw3, k3, P3, kw3, P_SAV // LANES, q_off=c * cl, n_q=cl,
                        lk=(c + 1) * cl, kc=kc, **common) for c in range(ncls)]
        y = ys[0] if ncls == 1 else jnp.concatenate(ys, axis=1)
    else:
        kc = 512
        lk = _round_up(lk_real, kc)
        zpad = jnp.zeros((B, lk - lk_real, LANES), F32)
        v_new = P.reshape(B, L, P_COLS)[:, :, P_SAV:P_SAV + LANES]
        ik_p = jnp.concatenate([ik_past.astype(F32), jnp.zeros((B, past, LANES - IDX_DIM), F32)], axis=2)
        k_all = jnp.concatenate([k_past.reshape(B, past, LANES).astype(F32), k3, zpad], axis=1)
        v_all = jnp.concatenate([v_past.reshape(B, past, LANES).astype(F32), v_new, zpad], axis=1)
        ki_all = jnp.concatenate([ik_p, kw3, zpad], axis=1)
        y = _dsa_call(q3, qi3, kw3, k_all, v_all, ki_all, 0, q_off=0, n_q=L, lk=lk, kc=kc, **common)
    return y.reshape(B * L, SA_WIDTH)


def _ret_kernel(q_ref, k_ref, v_ref, g_ref, cos_ref, sin_ref, dm_ref, qd_ref, kd_ref, gc_ref, s0_ref,
                y_ref, so_ref, s_ref):
    c = pl.program_id(1)
    D = RET_HEAD_DIM

    @pl.when(c == 0)
    def _():
        s_ref[...] = s0_ref[0]

    cos, sin = cos_ref[...], sin_ref[...]
    outs = []
    for h in range(RET_HEADS):
        sl = slice(h * D, (h + 1) * D)
        q = q_ref[:, sl]
        k = k_ref[:, sl]
        q = q * cos + pltpu.roll(q, D // 2, 1) * sin
        k = (k * cos + pltpu.roll(k, D // 2, 1) * sin) * (D ** -0.5)
        v = v_ref[:, sl]
        qb_, vb_ = _bf(q), _bf(v)
        s_h = s_ref[h]
        scores = _dot_nt(qb_, _bf(k)) * dm_ref[h]
        o = _dot(_bf(scores), vb_) + _dot(qb_, _bf(s_h)) * qd_ref[:, sl]
        s_ref[h] = s_h * gc_ref[:, sl] + _dot_tn(_bf(k * kd_ref[:, sl]), vb_)
        o = o * lax.rsqrt(jnp.mean(o * o, axis=-1, keepdims=True) + EPS)
        outs.append(jax.nn.silu(g_ref[:, sl]) * o)
    y_ref[...] = jnp.concatenate(outs, axis=1)

    @pl.when(c == pl.num_programs(1) - 1)
    def _():
        so_ref[0] = s_ref[...]


def _retention(P, B, L, pos, s0):
    T = P.shape[0]
    c = min(CHUNK, L)
    nc = L // c
    D = RET_HEAD_DIM
    freqs = 1.0 / (RET_ROPE_BASE ** jnp.linspace(0.0, 1.0, D // 2, dtype=F32))
    ang = pos.astype(F32)[:, None] * freqs[None, :]
    cos = jnp.concatenate([jnp.cos(ang)] * 2, axis=1)
    sin = jnp.concatenate([-jnp.sin(ang), jnp.sin(ang)], axis=1)
    log_gamma = jnp.log1p(-jnp.exp2(-5.0 - jnp.arange(RET_HEADS, dtype=F32)))
    idx = jnp.arange(c, dtype=F32)
    dmask = jnp.exp(jnp.abs(idx[:, None] - idx[None, :])[None] * log_gamma[:, None, None])
    lanes = lambda t: jnp.repeat(t, D, axis=1)
    qdec = lanes(jnp.exp((idx[:, None] + 1.0) * log_gamma[None, :]))
    kdec = lanes(jnp.exp((c - 1.0 - idx)[:, None] * log_gamma[None, :]))
    gchunk = lanes(jnp.exp(c * log_gamma)[None, :])

    W = RET_WIDTH
    tok = lambda blk: pl.BlockSpec((c, W), lambda b, i: (b * nc + i, blk))
    full = lambda shape: pl.BlockSpec(shape, lambda b, i: (0,) * len(shape))
    st_spec = pl.BlockSpec((1, RET_HEADS, D, D), lambda b, i: (b, 0, 0, 0))
    y, s_out = pl.pallas_call(
        _ret_kernel,
        grid=(B, nc),
        in_specs=[tok(P_RET // W), tok(P_RET // W + 1), tok(P_RET // W + 2), tok(P_RET // W + 3),
                  pl.BlockSpec((c, D), lambda b, i: (i, 0)), pl.BlockSpec((c, D), lambda b, i: (i, 0)),
                  full((RET_HEADS, c, c)), full((c, W)), full((c, W)), full((1, W)), st_spec],
        out_specs=[pl.BlockSpec((c, W), lambda b, i: (b * nc + i, 0)), st_spec],
        out_shape=[jax.ShapeDtypeStruct((T, W), F32), jax.ShapeDtypeStruct((B, RET_HEADS, D, D), F32)],
        scratch_shapes=[pltpu.VMEM((RET_HEADS, D, D), F32)],
        compiler_params=_params(("parallel", "arbitrary")),
        name="retention",
    )(P, P, P, P, cos, sin, dmask, qdec, kdec, gchunk, s0.astype(F32))
    return y, s_out


def _merge_kernel(x_ref, g0_ref, g1_ref, g2_ref, yr_ref, ys_ref, yt_ref, wr_ref, ws_ref, wt_ref, wo_ref,
                  o_ref):
    m = (jax.nn.sigmoid(g0_ref[...]) * _dot(_bf(yr_ref[...]), wr_ref[...])
         + jax.nn.sigmoid(g1_ref[...]) * _dot(_bf(ys_ref[...]), ws_ref[...])
         + jax.nn.sigmoid(g2_ref[...]) * _dot(_bf(yt_ref[...]), wt_ref[...]))
    o_ref[...] = x_ref[...] + _dot(_bf(m), wo_ref[...])


def _merge(x2d, P, y_rw, y_sa, y_ret, w_rw, w_sa, w_ret, w_o):
    T = x2d.shape[0]
    tm = min(512, T)
    D = D_MODEL
    tok = lambda w, blk: pl.BlockSpec((tm, w), lambda i: (i, blk))
    full = lambda shape: pl.BlockSpec(shape, lambda i: (0, 0))
    return pl.pallas_call(
        _merge_kernel,
        grid=(T // tm,),
        in_specs=[tok(D, 0), tok(D, 0), tok(D, 1), tok(D, 2), tok(RW_WIDTH, 0), tok(SA_WIDTH, 0),
                  tok(RET_WIDTH, 0), full((RW_WIDTH, D)), full((SA_WIDTH, D)), full((RET_WIDTH, D)),
                  full((D, D))],
        out_specs=tok(D, 0),
        out_shape=jax.ShapeDtypeStruct((T, D), F32),
        compiler_params=_params(("parallel",)),
        name="merge",
    )(x2d, P, P, P, y_rw, y_sa, y_ret, w_rw, w_sa, w_ret, w_o)


def _mlp_kernel(x_ref, g_ref, wu_ref, wd_ref, gf_ref, o_ref, h_ref, acc_ref, *, final_norm):
    j = pl.program_id(1)

    @pl.when(j == 0)
    def _():
        x = x_ref[...]
        ms = jnp.mean(x * x, axis=-1, keepdims=True)
        h_ref[...] = _bf(x * lax.rsqrt(ms + EPS) * g_ref[...])
        acc_ref[...] = jnp.zeros_like(acc_ref)

    u = jnp.maximum(_dot(h_ref[...], wu_ref[...]), 0.0)
    acc_ref[...] += _dot(_bf(u * u), wd_ref[...])

    @pl.when(j == pl.num_programs(1) - 1)
    def _():
        xn = x_ref[...] + acc_ref[...]
        if final_norm:
            ms = jnp.mean(xn * xn, axis=-1, keepdims=True)
            xn = xn * lax.rsqrt(ms + EPS) * gf_ref[...]
        o_ref[...] = xn


def _mlp(x2d, g, w_up, w_down, g_final, final_norm):
    T = x2d.shape[0]
    tm = min(1024, T)
    tf = 1024
    D = D_MODEL
    return pl.pallas_call(
        functools.partial(_mlp_kernel, final_norm=final_norm),
        grid=(T // tm, D_FF // tf),
        in_specs=[pl.BlockSpec((tm, D), lambda i, j: (i, 0)),
                  pl.BlockSpec((1, D), lambda i, j: (0, 0)),
                  pl.BlockSpec((D, tf), lambda i, j: (0, j)),
                  pl.BlockSpec((tf, D), lambda i, j: (j, 0)),
                  pl.BlockSpec((1, D), lambda i, j: (0, 0))],
        out_specs=pl.BlockSpec((tm, D), lambda i, j: (i, 0)),
        out_shape=jax.ShapeDtypeStruct((T, D), F32),
        scratch_shapes=[pltpu.VMEM((tm, D), BF16), pltpu.VMEM((tm, D), F32)],
        compiler_params=_params(("parallel", "arbitrary")),
        name="mlp",
    )(x2d, g, w_up, w_down, g_final)


def _permute_w_in(w):
    rw, sa, ret = 0, RW_COLS, RW_COLS + SA_COLS
    gate = ret + RET_COLS
    sa_k = sa + SA_WIDTH
    sa_v = sa_k + SA_KV_WIDTH
    sa_qi = sa_v + SA_KV_WIDTH
    sa_ki = sa_qi + IDX_WIDTH
    used = P_KIWI + IDX_DIM + IDX_HEADS
    parts = [w[:, gate:gate + GATE_COLS], w[:, ret:ret + RET_COLS], w[:, rw:rw + 3 * RW_WIDTH],
             w[:, sa:sa + SA_WIDTH], w[:, rw + 3 * RW_WIDTH:rw + RW_COLS], w[:, sa_qi:sa_qi + IDX_WIDTH],
             w[:, sa_k:sa_k + SA_KV_WIDTH], w[:, sa_v:sa_v + SA_KV_WIDTH],
             w[:, sa_ki:sa_ki + IDX_DIM + IDX_HEADS], jnp.zeros((w.shape[0], P_COLS - used), w.dtype)]
    return _bf(jnp.concatenate(parts, axis=1))


def _layer(x2d, B, L, past, caches, lp, wts, g_final, final_norm):
    k_past, v_past, ik_past, s_rw, shift_rw, s_ret = caches
    pos = past + jnp.arange(L, dtype=jnp.int32)
    row = lambda t: t.reshape(1, -1).astype(F32)
    P = _in_proj(x2d, row(lp['norm1_g']), wts['w_in'])
    q_rot, qi_rot, k_rot, kw_rot = _sa_prep(P, L, _sa_tables(pos))
    y_rw, s_rw_new = _rwkv(P, B, L, shift_rw, s_rw, lp)
    y_sa = _dsa(P, B, L, past, q_rot, qi_rot, k_rot, kw_rot, k_past, v_past, ik_past)
    y_ret, s_ret_new = _retention(P, B, L, pos, s_ret)
    x2d = _merge(x2d, P, y_rw, y_sa, y_ret, wts['w_br_rwkv'], wts['w_br_dsa'], wts['w_br_ret'], wts['w_o'])
    x2d = _mlp(x2d, row(lp['norm2_g']), wts['w_up'], wts['w_down'], g_final, final_norm)
    P3 = P.reshape(B, L, P_COLS)
    last = P3[:, L - 1]
    shift_new = jnp.concatenate([last[:, P_RKV:P_RKV + 3 * RW_WIDTH], last[:, P_LORA:P_LORA + RW_LORA]], axis=1)
    k_new = k_rot.reshape(B, L, SA_KV_HEADS, SA_HEAD_DIM)
    v_new = P3[:, :, P_SAV:P_SAV + SA_KV_WIDTH].reshape(B, L, SA_KV_HEADS, SA_HEAD_DIM)
    ik_new = kw_rot.reshape(B, L, LANES)[:, :, :IDX_DIM]
    return x2d, (k_new, v_new, ik_new, s_rw_new, shift_new, s_ret_new)


def kernel(x_prompt, x_sample, cache_dsa_k, cache_dsa_v, cache_dsa_ik, state_rwkv, state_rwkv_shift, state_ret, norm1_g, w_in, rwkv_mu, rwkv_w0, rwkv_w2, rwkv_a0, rwkv_a2, rwkv_g2, rwkv_k_k, rwkv_k_a, rwkv_r_k, rwkv_lnx_g, w_br_rwkv, w_br_dsa, w_br_ret, w_o, norm2_g, w_up, w_down, final_norm_g):
    params = {
        'norm1_g': norm1_g, 'rwkv_mu': rwkv_mu, 'rwkv_w0': rwkv_w0, 'rwkv_w2': rwkv_w2,
        'rwkv_a0': rwkv_a0, 'rwkv_a2': rwkv_a2, 'rwkv_g2': rwkv_g2, 'rwkv_k_k': rwkv_k_k,
        'rwkv_k_a': rwkv_k_a, 'rwkv_r_k': rwkv_r_k, 'rwkv_lnx_g': rwkv_lnx_g, 'norm2_g': norm2_g,
    }
    depth = w_in.shape[0]
    Bp, Lp, D = x_prompt.shape
    Bs, Ls, _ = x_sample.shape
    past_s = cache_dsa_k.shape[2]
    xp = x_prompt.reshape(Bp * Lp, D).astype(F32)
    xs = x_sample.reshape(Bs * Ls, D).astype(F32)
    g_final = final_norm_g.reshape(1, D).astype(F32)
    zero_p = (None, None, None,
              jnp.zeros((Bp, RW_HEADS, RW_HEAD_DIM, RW_HEAD_DIM), F32), jnp.zeros((Bp, RW_COLS), F32),
              jnp.zeros((Bp, RET_HEADS, RET_HEAD_DIM, RET_HEAD_DIM), F32))
    p_states = [[] for _ in range(6)]
    s_states = [[] for _ in range(6)]
    for i in range(depth):
        lp = {name: arr[i] for name, arr in params.items()}
        wts = {'w_in': _permute_w_in(w_in[i]), 'w_br_rwkv': _bf(w_br_rwkv[i]), 'w_br_dsa': _bf(w_br_dsa[i]),
               'w_br_ret': _bf(w_br_ret[i]), 'w_o': _bf(w_o[i]), 'w_up': _bf(w_up[i]), 'w_down': _bf(w_down[i])}
        final = i == depth - 1
        cache_s = (cache_dsa_k[i], cache_dsa_v[i], cache_dsa_ik[i], state_rwkv[i], state_rwkv_shift[i],
                   state_ret[i])
        xp, new_p = _layer(xp, Bp, Lp, 0, zero_p, lp, wts, g_final, final)
        xs, new_s = _layer(xs, Bs, Ls, past_s, cache_s, lp, wts, g_final, final)
        for j in range(6):
            p_states[j].append(new_p[j])
            s_states[j].append(new_s[j])
    y_prompt = xp.reshape(Bp, Lp, D)
    y_sample = xs.reshape(Bs, Ls, D)
    p_out = [jnp.stack(t, axis=0) for t in p_states]
    s_out = [jnp.stack(t, axis=0) for t in s_states]
    return (y_prompt, y_sample, *p_out, *s_out)
```

```python
import functools

import numpy as np
import jax
import jax.numpy as jnp
from jax import lax
from jax.experimental import pallas as pl
from jax.experimental.pallas import tpu as pltpu

F32 = jnp.float32
BF16 = jnp.bfloat16
I32 = jnp.int32

D_MODEL = 1024
CHUNK = 64
Q_BLOCK = 128
EPS = 1e-6

RW_HEADS = 8
RW_HEAD_DIM = 64
RW_WIDTH = RW_HEADS * RW_HEAD_DIM
RW_DECAY_LORA = 64
RW_ICLR_LORA = 64
RW_GATE_LORA = 128
RW_LORA = RW_DECAY_LORA + RW_ICLR_LORA + RW_GATE_LORA
RW_COLS = 3 * RW_WIDTH + RW_LORA
RW_GN_EPS = 64e-5

SA_HEADS = 8
SA_KV_HEADS = 2
SA_HEAD_DIM = 64
SA_WIDTH = SA_HEADS * SA_HEAD_DIM
SA_KV_WIDTH = SA_KV_HEADS * SA_HEAD_DIM
IDX_HEADS = 4
IDX_DIM = 64
IDX_WIDTH = IDX_HEADS * IDX_DIM
TOPK_MAX = 256
ROPE_THETA = 500000.0
ROPE_DIM = SA_HEAD_DIM // 4
INDEX_SCALE = (IDX_DIM ** -0.5) * (IDX_HEADS ** -0.5)
SA_COLS = SA_WIDTH + 2 * SA_KV_WIDTH + IDX_WIDTH + IDX_DIM + IDX_HEADS

RET_HEADS = 4
RET_HEAD_DIM = 128
RET_WIDTH = RET_HEADS * RET_HEAD_DIM
RET_ROPE_BASE = 10000.0
RET_COLS = 4 * RET_WIDTH

N_BRANCH = 3
GATE_COLS = N_BRANCH * D_MODEL
IN_COLS = RW_COLS + SA_COLS + RET_COLS + GATE_COLS
D_FF = 4 * D_MODEL

LANES = 128

P_GATE = 0
P_RET = P_GATE + GATE_COLS
P_RKV = P_RET + RET_COLS
P_SAQ = P_RKV + 3 * RW_WIDTH
P_LORA = P_SAQ + SA_WIDTH
P_QI = P_LORA + RW_LORA
P_SAK = P_QI + IDX_WIDTH
P_SAV = P_SAK + SA_KV_WIDTH
P_KIWI = P_SAV + SA_KV_WIDTH
P_COLS = 8192
INT_MIN = -2 ** 31
NEG_BIG = -1e30
LOG2_E = 1.4426950408889634
VMEM_LIMIT = 56 * 1024 * 1024


def _bf(x):
    return x.astype(BF16)


def _dot(a, b):
    return jnp.dot(a, b, preferred_element_type=F32)


def _dot_nt(a, b):
    return lax.dot_general(a, b, (((1,), (1,)), ((), ())), preferred_element_type=F32)


def _dot_tn(a, b):
    return lax.dot_general(a, b, (((0,), (0,)), ((), ())), preferred_element_type=F32)


def _dot_split2(a_exact, x):
    hi = _bf(x)
    lo = _bf(x - hi.astype(F32))
    return _dot(a_exact, hi) + _dot(a_exact, lo)


def _params(sem):
    return pltpu.CompilerParams(dimension_semantics=sem, vmem_limit_bytes=VMEM_LIMIT)


def _in_proj_kernel(x_ref, g_ref, w_ref, o_ref, h_ref):
    @pl.when(pl.program_id(1) == 0)
    def _():
        x = x_ref[...]
        ms = jnp.mean(x * x, axis=-1, keepdims=True)
        h_ref[...] = _bf(x * lax.rsqrt(ms + EPS) * g_ref[...])

    o_ref[...] = _dot(h_ref[...], w_ref[...])


def _in_proj(x2d, g, w_pad):
    T = x2d.shape[0]
    tm = min(1024, T)
    tn = 1024
    return pl.pallas_call(
        _in_proj_kernel,
        grid=(T // tm, P_COLS // tn),
        in_specs=[pl.BlockSpec((tm, D_MODEL), lambda i, j: (i, 0)),
                  pl.BlockSpec((1, D_MODEL), lambda i, j: (0, 0)),
                  pl.BlockSpec((D_MODEL, tn), lambda i, j: (0, j))],
        out_specs=pl.BlockSpec((tm, tn), lambda i, j: (i, j)),
        out_shape=jax.ShapeDtypeStruct((T, P_COLS), F32),
        scratch_shapes=[pltpu.VMEM((tm, D_MODEL), BF16)],
        compiler_params=_params(("parallel", "arbitrary")),
        name="in_proj",
    )(x2d, g, w_pad)


def _rot_slab(x, c, s_lo, s_hi, shift):
    return x * c + pltpu.roll(x, LANES - shift, 1) * s_lo + pltpu.roll(x, shift, 1) * s_hi


def _sa_prep_kernel(q_ref, qi_ref, k_ref, kw_ref, c_ref, s1_ref, s2_ref, ck_ref, s1k_ref, s2k_ref,
                    qo_ref, qio_ref, ko_ref, kwo_ref):
    c, s1, s2 = c_ref[...], s1_ref[...], s2_ref[...]
    half = ROPE_DIM // 2
    scale = SA_HEAD_DIM ** -0.5 * LOG2_E
    for s in range(SA_WIDTH // LANES):
        sl = slice(s * LANES, (s + 1) * LANES)
        qo_ref[:, sl] = _rot_slab(q_ref[:, sl], c, s1, s2, half) * scale
    for s in range(IDX_WIDTH // LANES):
        sl = slice(s * LANES, (s + 1) * LANES)
        qio_ref[:, sl] = _rot_slab(qi_ref[:, sl], c, s1, s2, half)
    ko_ref[...] = _rot_slab(k_ref[...], c, s1, s2, half)
    kwo_ref[...] = _rot_slab(kw_ref[...], ck_ref[...], s1k_ref[...], s2k_ref[...], half)


def _sa_prep(P, L, tabs):
    T = P.shape[0]
    tm = min(512, L)
    nl = L // tm
    tab_spec = pl.BlockSpec((tm, LANES), lambda i: (i % nl, 0))
    return pl.pallas_call(
        _sa_prep_kernel,
        grid=(T // tm,),
        in_specs=[pl.BlockSpec((tm, SA_WIDTH), lambda i: (i, P_SAQ // SA_WIDTH)),
                  pl.BlockSpec((tm, IDX_WIDTH), lambda i: (i, P_QI // IDX_WIDTH)),
                  pl.BlockSpec((tm, LANES), lambda i: (i, P_SAK // LANES)),
                  pl.BlockSpec((tm, LANES), lambda i: (i, P_KIWI // LANES))] + [tab_spec] * 6,
        out_specs=[pl.BlockSpec((tm, SA_WIDTH), lambda i: (i, 0)),
                   pl.BlockSpec((tm, IDX_WIDTH), lambda i: (i, 0)),
                   pl.BlockSpec((tm, LANES), lambda i: (i, 0)),
                   pl.BlockSpec((tm, LANES), lambda i: (i, 0))],
        out_shape=[jax.ShapeDtypeStruct((T, SA_WIDTH), F32),
                   jax.ShapeDtypeStruct((T, IDX_WIDTH), F32),
                   jax.ShapeDtypeStruct((T, LANES), F32),
                   jax.ShapeDtypeStruct((T, LANES), F32)],
        compiler_params=_params(("parallel",)),
        name="sa_prep",
    )(P, P, P, P, *tabs)


def _sa_tables(pos):
    half = ROPE_DIM // 2
    freqs = 1.0 / (ROPE_THETA ** (jnp.arange(0, ROPE_DIM, 2, dtype=F32) / ROPE_DIM))
    ang = pos.astype(F32)[:, None] * freqs[None, :]
    cos, sin = jnp.cos(ang), jnp.sin(ang)
    n = pos.shape[0]
    pad = SA_HEAD_DIM - ROPE_DIM
    c_head = jnp.concatenate([cos, cos, jnp.ones((n, pad), F32)], axis=1)
    s1_head = jnp.concatenate([-sin, jnp.zeros((n, half + pad), F32)], axis=1)
    s2_head = jnp.concatenate([jnp.zeros((n, half), F32), sin, jnp.zeros((n, pad), F32)], axis=1)
    one, zero = jnp.ones((n, SA_HEAD_DIM), F32), jnp.zeros((n, SA_HEAD_DIM), F32)
    two = lambda t: jnp.concatenate([t, t], axis=1)
    return (two(c_head), two(s1_head), two(s2_head),
            jnp.concatenate([c_head, one], axis=1), jnp.concatenate([s1_head, zero], axis=1),
            jnp.concatenate([s2_head, zero], axis=1))


def _softplus(u):
    return jnp.maximum(u, 0.0) + jnp.log(1.0 + jnp.exp(-jnp.abs(u)))


def _rwkv_kernel(r_ref, k_ref, v_ref, lo_ref, shr_ref, shk_ref, shv_ref, shlo_ref, s0_ref,
                 mur_ref, muk_ref, muv_ref, mulo_ref, w0_ref, w2_ref, a0_ref, a2_ref, g2_ref,
                 kk_ref, ka_ref, rk_ref, lnx_ref, e_ref, tin_ref, tsuf_ref,
                 y_ref, so_ref,
                 cr_ref, ck_ref, cv_ref, clo_ref, s_ref,
                 kap_ref, rt_ref, bh_ref, kh_ref, bt_ref, kt_ref, vv_ref, gc_ref, yy_ref, rr_ref,
                 ac_ref, cc_ref):
    c = pl.program_id(1)
    TT = r_ref.shape[0]
    nch = TT // CHUNK
    N = RW_HEAD_DIM

    @pl.when(c == 0)
    def _():
        cr_ref[0:1, :] = shr_ref[0]
        ck_ref[0:1, :] = shk_ref[0]
        cv_ref[0:1, :] = shv_ref[0]
        clo_ref[0:1, :] = shlo_ref[0]
        s_ref[...] = s0_ref[0]

    def lerp(p_ref, carry_ref, mu_ref):
        p = p_ref[...]
        rolled = pltpu.roll(p, 1, 0)
        row0 = lax.broadcasted_iota(I32, p.shape, 0) == 0
        prev = jnp.where(row0, carry_ref[0:1, :], rolled)
        carry_ref[0:1, :] = p[TT - 1:TT, :]
        return p + (prev - p) * mu_ref[...]

    xr = lerp(r_ref, cr_ref, mur_ref)
    xk = lerp(k_ref, ck_ref, muk_ref)
    xv = lerp(v_ref, cv_ref, muv_ref)
    xlo = lerp(lo_ref, clo_ref, mulo_ref)
    xw = xlo[:, :RW_DECAY_LORA]
    xa = xlo[:, RW_DECAY_LORA:RW_DECAY_LORA + RW_ICLR_LORA]
    xg = xlo[:, RW_DECAY_LORA + RW_ICLR_LORA:]

    z = w0_ref[...] + _dot(_bf(jnp.tanh(xw)), w2_ref[...])
    w = -_softplus(-z) - 0.5
    ld = -jnp.exp(w)
    a = jax.nn.sigmoid(a0_ref[...] + _dot(_bf(xa), a2_ref[...]))
    gate = _dot(_bf(jax.nn.sigmoid(xg)), g2_ref[...])
    e_blk = e_ref[...]
    kk = xk * kk_ref[...]
    kk = kk / jnp.maximum(jnp.sqrt(_dot(_bf(kk * kk), e_blk)), 1e-12)
    k2 = xk * (1.0 + (a - 1.0) * ka_ref[...])
    bb = kk * a
    bonus = _dot(_bf(xr * k2 * rk_ref[...]), e_blk) * xv

    lin = _dot_split2(tin_ref[...], ld)
    lsuf = _dot_split2(tsuf_ref[...], ld)
    e_in = jnp.exp(lin)
    e_ninv = jnp.exp(-lin)
    e_suf = jnp.exp(lsuf)
    kap_ref[...] = kk * jnp.exp(lin - ld)
    rt_ref[...] = xr * e_in
    bh_ref[...] = bb * e_ninv
    kh_ref[...] = k2 * e_ninv
    bt_ref[...] = bb * e_suf
    kt_ref[...] = k2 * e_suf
    vv_ref[...] = xv
    gc_ref[...] = jnp.exp(lin + lsuf)

    ri = lax.broadcasted_iota(I32, (CHUNK, CHUNK), 0)
    ci = lax.broadcasted_iota(I32, (CHUNK, CHUNK), 1)
    strict = ri > ci
    incl = ri >= ci
    eye = (ri == ci).astype(F32)
    heads = range(RW_HEADS)
    hsl = [slice(h * N, (h + 1) * N) for h in heads]

    def coef_body(ch, carry):
        rows = pl.ds(pl.multiple_of(ch * CHUNK, CHUNK), CHUNK)
        kap = [kap_ref[rows, sl] for sl in hsl]
        rt = [rt_ref[rows, sl] for sl in hsl]
        vh = [vv_ref[rows, sl] for sl in hsl]
        gmat = [_dot_nt(_bf(jnp.concatenate([kap[h], rt[h]], axis=0)),
                        _bf(jnp.concatenate([bh_ref[rows, hsl[h]], kh_ref[rows, hsl[h]]], axis=0)))
                for h in heads]
        n_ab = [jnp.where(strict, g[:CHUNK, :CHUNK], 0.0) for g in gmat]
        m_rb = [jnp.where(incl, g[CHUNK:, :CHUNK], 0.0) for g in gmat]
        m_v = [_bf(jnp.concatenate([jnp.where(strict, g[:CHUNK, CHUNK:], 0.0),
                                    jnp.where(incl, g[CHUNK:, CHUNK:], 0.0)], axis=0)) for g in gmat]
        mv = [_dot(m_v[h], _bf(vh[h])) for h in heads]
        x_inv = [eye - n for n in n_ab]
        pw = n_ab
        for _ in range(5):
            pwb = [_bf(p) for p in pw]
            pw = [_dot(p, p) for p in pwb]
            x_inv = [x + _dot(_bf(x), _bf(p)) for x, p in zip(x_inv, pw)]
        w = [_dot(_bf(x_inv[h]), _bf(jnp.concatenate([kap[h], mv[h][:CHUNK]], axis=1))) for h in heads]
        wb = [_bf(t) for t in w]
        ry = [jnp.concatenate([rt[h], mv[h][CHUNK:]], axis=1) - _dot(_bf(m_rb[h]), wb[h]) for h in heads]
        dmat = [_dot_tn(wb[h], _bf(bt_ref[rows, hsl[h]])) for h in heads]
        vtk = [_dot_tn(_bf(vh[h]), _bf(kt_ref[rows, hsl[h]])) for h in heads]
        for h in heads:
            rr_ref[rows, hsl[h]] = ry[h][:, :N]
            yy_ref[rows, hsl[h]] = ry[h][:, N:]
            ac_ref[ch, h] = -dmat[h][:N]
            cc_ref[ch, h] = vtk[h] - dmat[h][N:]
        return carry

    lax.fori_loop(0, nch, coef_body, 0)

    def state_body(ch, carry):
        rows = pl.ds(pl.multiple_of(ch * CHUNK, CHUNK), CHUNK)
        s_old = [s_ref[h] for h in heads]
        sb = [_bf(t) for t in s_old]
        s_new = [_dot(sb[h], _bf(ac_ref[ch, h])) for h in heads]
        y_c = [_dot_nt(_bf(rr_ref[rows, hsl[h]]), sb[h]) for h in heads]
        for h in heads:
            gch = gc_ref[pl.ds(pl.multiple_of(ch * CHUNK, CHUNK), 1), hsl[h]]
            s_ref[h] = s_old[h] * gch + s_new[h] + cc_ref[ch, h]
            yy_ref[rows, hsl[h]] = yy_ref[rows, hsl[h]] + y_c[h]
        return carry

    lax.fori_loop(0, nch, state_body, 0)

    y = yy_ref[...]
    mean = _dot(_bf(y), e_blk) * (1.0 / N)
    d = y - mean
    var = _dot(_bf(d * d), e_blk) * (1.0 / N)
    yn = d * lax.rsqrt(var + RW_GN_EPS) * lnx_ref[...]
    y_ref[...] = (yn + bonus) * gate

    @pl.when(c == pl.num_programs(1) - 1)
    def _():
        so_ref[0] = s_ref[...]


def _rwkv(P, B, L, shift_prev, s0, lp):
    T = P.shape[0]
    TT = min(256, L)
    nt = L // TT
    W = RW_WIDTH
    row = lambda t: t.reshape(1, -1).astype(F32)
    mu = lp['rwkv_mu']
    sh = shift_prev.astype(F32)
    pieces = lambda t: (t[..., 0:W], t[..., W:2 * W], t[..., 2 * W:3 * W], t[..., 3 * W:])
    mu_r, mu_k, mu_v, mu_lo = [row(t) for t in pieces(mu)]
    sh_r, sh_k, sh_v, sh_lo = [t.reshape(B, 1, -1) for t in pieces(sh)]
    hid = jnp.arange(W) // RW_HEAD_DIM
    e_blk = (hid[:, None] == hid[None, :]).astype(BF16)
    ti = jnp.arange(TT)
    same = (ti[:, None] // CHUNK) == (ti[None, :] // CHUNK)
    tri_in = (same & (ti[None, :] <= ti[:, None])).astype(BF16)
    tri_suf = (same & (ti[None, :] > ti[:, None])).astype(BF16)

    tok = lambda w, blk: pl.BlockSpec((TT, w), lambda b, c: (b * nt + c, blk))
    full = lambda shape: pl.BlockSpec(shape, lambda b, c: (0,) * len(shape))
    shs = lambda w: pl.BlockSpec((1, 1, w), lambda b, c: (b, 0, 0))
    st_spec = pl.BlockSpec((1, RW_HEADS, RW_HEAD_DIM, RW_HEAD_DIM), lambda b, c: (b, 0, 0, 0))
    big = lambda: pltpu.VMEM((TT, W), F32)
    y, s_out = pl.pallas_call(
        _rwkv_kernel,
        grid=(B, nt),
        in_specs=[tok(W, P_RKV // W), tok(W, P_RKV // W + 1), tok(W, P_RKV // W + 2),
                  tok(RW_LORA, P_LORA // RW_LORA),
                  shs(W), shs(W), shs(W), shs(RW_LORA), st_spec,
                  full((1, W)), full((1, W)), full((1, W)), full((1, RW_LORA)),
                  full((1, W)), full((RW_DECAY_LORA, W)), full((1, W)), full((RW_ICLR_LORA, W)),
                  full((RW_GATE_LORA, W)), full((1, W)), full((1, W)), full((1, W)), full((1, W)),
                  full((W, W)), full((TT, TT)), full((TT, TT))],
        out_specs=[pl.BlockSpec((TT, W), lambda b, c: (b * nt + c, 0)), st_spec],
        out_shape=[jax.ShapeDtypeStruct((T, W), F32),
                   jax.ShapeDtypeStruct((B, RW_HEADS, RW_HEAD_DIM, RW_HEAD_DIM), F32)],
        scratch_shapes=[pltpu.VMEM((8, W), F32), pltpu.VMEM((8, W), F32), pltpu.VMEM((8, W), F32),
                        pltpu.VMEM((8, RW_LORA), F32),
                        pltpu.VMEM((RW_HEADS, RW_HEAD_DIM, RW_HEAD_DIM), F32)] + [big() for _ in range(10)]
        + [pltpu.VMEM((TT // CHUNK, RW_HEADS, RW_HEAD_DIM, RW_HEAD_DIM), F32) for _ in range(2)],
        compiler_params=_params(("parallel", "arbitrary")),
        name="rwkv",
    )(P, P, P, P, sh_r, sh_k, sh_v, sh_lo, s0.astype(F32),
      mu_r, mu_k, mu_v, mu_lo, row(lp['rwkv_w0']), _bf(lp['rwkv_w2']), row(lp['rwkv_a0']),
      _bf(lp['rwkv_a2']), _bf(lp['rwkv_g2']), row(lp['rwkv_k_k']), row(lp['rwkv_k_a']),
      row(lp['rwkv_r_k']), row(lp['rwkv_lnx_g']), e_blk, tri_in, tri_suf)
    return y, s_out


ONES_ROWS = 16


def _dsa_kernel(q_ref, qi_ref, kw_ref, k_ref, v_ref, ki_ref, y_ref,
                kb_ref, vt_ref, kib_ref, keys_ref, bias_ref, logit_ref, pstar_ref,
                *, q_off, past, lk_real, topk, kc):
    i = pl.program_id(1)
    qb = q_ref.shape[1]
    lk = k_ref.shape[1]
    nk = lk // kc
    kf = float(topk)
    HD = SA_HEAD_DIM

    @pl.when(i == 0)
    def _():
        kb_ref[...] = _bf(k_ref[0])
        kib_ref[...] = _bf(ki_ref[0])
        vt = v_ref[0].T
        for c in range(SA_KV_HEADS):
            vt_ref[c, 0:HD, :] = _bf(vt[c * HD:(c + 1) * HD, :])
            vt_ref[c, HD:HD + ONES_ROWS, :] = jnp.ones((ONES_ROWS, lk), BF16)

    qpos = past + q_off + i * qb + lax.broadcasted_iota(I32, (1, qb), 1)
    limit = jnp.minimum((qpos // CHUNK + 1) * CHUNK, lk_real)
    sub_idx = lax.broadcasted_iota(I32, (kc, qb), 0)
    kwt = kw_ref[0].T
    wi_h = [kwt[IDX_DIM + h:IDX_DIM + h + 1, :] * INDEX_SCALE for h in range(IDX_HEADS)]
    qi = qi_ref[0]
    qi_h = [_bf(qi[:, h * IDX_DIM:(h + 1) * IDX_DIM]) for h in range(IDX_HEADS)]
    chunks = [slice(j * kc, (j + 1) * kc) for j in range(nk)]

    for j, ks in enumerate(chunks):
        ki_j = kib_ref[ks, 0:IDX_DIM]
        s = jnp.zeros((kc, qb), F32)
        for h in range(IDX_HEADS):
            s = s + jnp.maximum(_dot_nt(ki_j, qi_h[h]), 0.0) * wi_h[h]
        s = jnp.where(s == 0.0, 0.0, s)
        bits = lax.bitcast_convert_type(s, I32)
        key = bits ^ ((bits >> 31) & 0x7FFFFFFF)
        keys_ref[ks, :] = jnp.where((sub_idx + j * kc) < limit, key, INT_MIN)

    def count(pred):
        acc = jnp.zeros((1, qb), F32)
        for j, ks in enumerate(chunks):
            acc = acc + jnp.sum(jnp.where(pred(keys_ref[ks, :], j), 1.0, 0.0), axis=0, keepdims=True)
        return acc

    def bit_body(it, prefix):
        cand = prefix | lax.shift_left(jnp.int32(1), 31 - it)
        thr = cand ^ INT_MIN
        cnt = count(lambda kj, j: kj >= thr)
        return jnp.where(cnt >= kf, cand, prefix)

    prefix = lax.fori_loop(0, 32, bit_body, jnp.zeros((1, qb), I32))
    tau = prefix ^ INT_MIN
    cnt_ge = count(lambda kj, j: kj >= tau)
    cnt_gt = count(lambda kj, j: kj > tau)
    need = kf - cnt_gt
    excess = jnp.logical_and(cnt_ge > kf, tau != INT_MIN)
    p_default = jnp.where(tau == INT_MIN, -1, 2 ** 30).astype(I32)
    pstar_ref[...] = jnp.broadcast_to(p_default, pstar_ref.shape)

    @pl.when(jnp.max(jnp.where(excess, 1.0, 0.0)) > 0.0)
    def _():
        nbits = max(1, int(lk - 1).bit_length())

        def idx_body(it, p):
            cand = p | lax.shift_left(jnp.int32(1), nbits - 1 - it)
            g = count(lambda kj, j: jnp.logical_and(kj == tau, (sub_idx + j * kc) < cand))
            return jnp.where(g < need, cand, p)

        p = lax.fori_loop(0, nbits, idx_body, jnp.zeros((1, qb), I32))
        pstar_ref[...] = jnp.broadcast_to(jnp.where(excess, p, p_default), pstar_ref.shape)

    pstar = pstar_ref[0:1, :]
    for j, ks in enumerate(chunks):
        kj = keys_ref[ks, :]
        sel = jnp.logical_or(kj > tau, jnp.logical_and(kj == tau, (sub_idx + j * kc) <= pstar))
        bias_ref[ks, :] = jnp.where(sel, 0.0, NEG_BIG)

    q = q_ref[0]
    outs = []
    group = SA_HEADS // SA_KV_HEADS
    for h in range(SA_HEADS):
        c = h // group
        cs = slice(c * HD, (c + 1) * HD)
        qh = _bf(q[:, h * HD:(h + 1) * HD])
        m = jnp.full((1, qb), NEG_BIG, F32)
        for ks in chunks:
            logit = _dot_nt(kb_ref[ks, cs], qh) + bias_ref[ks, :]
            logit_ref[ks, :] = logit
            m = jnp.maximum(m, jnp.max(logit, axis=0, keepdims=True))
        acc = jnp.zeros((HD + ONES_ROWS, qb), F32)
        for ks in chunks:
            p = jnp.exp2(logit_ref[ks, :] - m)
            acc = acc + _dot(vt_ref[c, :, ks], _bf(p))
        outs.append(acc[:HD] / acc[HD:HD + 1])
    y_ref[0] = jnp.concatenate(outs, axis=0)


def _dsa_call(q3, qi3, kw3, k_all, v_all, ki_all, v_blk, *, q_off, n_q, lk, past, lk_real, topk, qb, kc):
    B = q3.shape[0]
    off = q_off // qb
    kern = functools.partial(_dsa_kernel, q_off=q_off, past=past, lk_real=lk_real, topk=topk, kc=kc)
    return pl.pallas_call(
        kern,
        grid=(B, n_q // qb),
        in_specs=[pl.BlockSpec((1, qb, SA_WIDTH), lambda b, i: (b, off + i, 0)),
                  pl.BlockSpec((1, qb, IDX_WIDTH), lambda b, i: (b, off + i, 0)),
                  pl.BlockSpec((1, qb, LANES), lambda b, i: (b, off + i, 0)),
                  pl.BlockSpec((1, lk, LANES), lambda b, i: (b, 0, 0)),
                  pl.BlockSpec((1, lk, LANES), lambda b, i: (b, 0, v_blk)),
                  pl.BlockSpec((1, lk, LANES), lambda b, i: (b, 0, 0))],
        out_specs=pl.BlockSpec((1, SA_WIDTH, qb), lambda b, i: (b, 0, i)),
        out_shape=jax.ShapeDtypeStruct((B, SA_WIDTH, n_q), F32),
        scratch_shapes=[pltpu.VMEM((lk, LANES), BF16),
                        pltpu.VMEM((SA_KV_HEADS, SA_HEAD_DIM + ONES_ROWS, lk), BF16),
                        pltpu.VMEM((lk, LANES), BF16),
                        pltpu.VMEM((lk, qb), I32), pltpu.VMEM((lk, qb), F32), pltpu.VMEM((lk, qb), F32),
                        pltpu.VMEM((8, qb), I32)],
        compiler_params=_params(("parallel", "arbitrary")),
        name="dsa",
    )(q3, qi3, kw3, k_all, v_all, ki_all)


def _round_up(x, m):
    return (x + m - 1) // m * m


def _dsa(P, B, L, past, q_rot, qi_rot, k_rot, kw_rot, k_past, v_past, ik_past):
    lk_real = past + L
    topk = min(TOPK_MAX, lk_real // 4)
    qb = Q_BLOCK
    lq = _round_up(L, qb)
    qpad = lambda t: t if lq == L else jnp.pad(t, ((0, 0), (0, lq - L), (0, 0)))
    q3 = qpad(q_rot.reshape(B, L, SA_WIDTH))
    qi3 = qpad(qi_rot.reshape(B, L, IDX_WIDTH))
    kw3 = kw_rot.reshape(B, L, LANES)
    k3 = k_rot.reshape(B, L, LANES)
    common = dict(past=past, lk_real=lk_real, topk=topk, qb=qb)
    if past == 0:
        ncls = next(n for n in (8, 4, 2, 1) if L % n == 0 and ((L // n) % 512 == 0 or n == 1))
        cl = L // ncls
        kc = 512 if cl % 512 == 0 else LANES
        P3 = P.reshape(B, L, P_COLS)
        ys = [_dsa_call(q3, qi3, kw3, k3, P3, kw3, P_SAV // LANES, q_off=c * cl, n_q=cl,
                        lk=(c + 1) * cl, kc=kc, **common) for c in range(ncls)]
        return ys[0] if ncls == 1 else jnp.concatenate(ys, axis=2)
    kc = 512
    lk = _round_up(lk_real, kc)
    zpad = jnp.zeros((B, lk - lk_real, LANES), F32)
    v_new = P.reshape(B, L, P_COLS)[:, :, P_SAV:P_SAV + LANES]
    ik_p = jnp.concatenate([ik_past.astype(F32), jnp.zeros((B, past, LANES - IDX_DIM), F32)], axis=2)
    k_all = jnp.concatenate([k_past.reshape(B, past, LANES).astype(F32), k3, zpad], axis=1)
    v_all = jnp.concatenate([v_past.reshape(B, past, LANES).astype(F32), v_new, zpad], axis=1)
    ki_all = jnp.concatenate([ik_p, kw3, zpad], axis=1)
    y = _dsa_call(q3, qi3, qpad(kw3), k_all, v_all, ki_all, 0, q_off=0, n_q=lq, lk=lk, kc=kc, **common)
    return y[:, :, :L]


def _ret_kernel(q_ref, k_ref, v_ref, g_ref, cos_ref, sin_ref, dm_ref, qd_ref, kd_ref, gc_ref, s0_ref,
                y_ref, so_ref, s_ref):
    c = pl.program_id(1)
    D = RET_HEAD_DIM
    cs = dm_ref.shape[1]
    nch = q_ref.shape[0] // cs
    heads = range(RET_HEADS)
    hsl = [slice(h * D, (h + 1) * D) for h in heads]
    rows = [slice(ch * cs, (ch + 1) * cs) for ch in range(nch)]
    prob = [(ch, h) for ch in range(nch) for h in heads]

    @pl.when(c == 0)
    def _():
        s_ref[...] = s0_ref[0]

    cos, sin = cos_ref[...], sin_ref[...]
    qb_, kb_, kd_, vb_ = [], [], [], []
    for h in heads:
        q = q_ref[:, hsl[h]]
        k = k_ref[:, hsl[h]]
        k = (k * cos + pltpu.roll(k, D // 2, 1) * sin) * (D ** -0.5)
        qb_.append(_bf(q * cos + pltpu.roll(q, D // 2, 1) * sin))
        kb_.append(_bf(k))
        kd_.append([_bf(k[r] * kd_ref[:, hsl[h]]) for r in rows])
        vb_.append(_bf(v_ref[:, hsl[h]]))
    scores = {(ch, h): _dot_nt(qb_[h][rows[ch]], kb_[h][rows[ch]]) * dm_ref[h] for ch, h in prob}
    ktv = {(ch, h): _dot_tn(kd_[h][ch], vb_[h][rows[ch]]) for ch, h in prob}
    intra = {(ch, h): _dot(_bf(scores[ch, h]), vb_[h][rows[ch]]) for ch, h in prob}
    s_at = {}
    for h in heads:
        s = s_ref[h]
        for ch in range(nch):
            s_at[ch, h] = _bf(s)
            s = s * gc_ref[:, hsl[h]] + ktv[ch, h]
        s_ref[h] = s
    cross = {(ch, h): _dot(qb_[h][rows[ch]], s_at[ch, h]) for ch, h in prob}
    for ch in range(nch):
        outs = []
        for h in heads:
            o = intra[ch, h] + cross[ch, h] * qd_ref[:, hsl[h]]
            o = o * lax.rsqrt(jnp.mean(o * o, axis=-1, keepdims=True) + EPS)
            outs.append(jax.nn.silu(g_ref[rows[ch], hsl[h]]) * o)
        y_ref[rows[ch], :] = jnp.concatenate(outs, axis=1)

    @pl.when(c == pl.num_programs(1) - 1)
    def _():
        so_ref[0] = s_ref[...]


def _retention(P, B, L, pos, s0):
    T = P.shape[0]
    c = min(CHUNK, L)
    nc = L // c
    D = RET_HEAD_DIM
    freqs = 1.0 / (RET_ROPE_BASE ** jnp.linspace(0.0, 1.0, D // 2, dtype=F32))
    ang = pos.astype(F32)[:, None] * freqs[None, :]
    cos = jnp.concatenate([jnp.cos(ang)] * 2, axis=1)
    sin = jnp.concatenate([-jnp.sin(ang), jnp.sin(ang)], axis=1)
    log_gamma = jnp.log1p(-jnp.exp2(-5.0 - jnp.arange(RET_HEADS, dtype=F32)))
    idx = jnp.arange(c, dtype=F32)
    dmask = jnp.exp(jnp.abs(idx[:, None] - idx[None, :])[None] * log_gamma[:, None, None])
    lanes = lambda t: jnp.repeat(t, D, axis=1)
    qdec = lanes(jnp.exp((idx[:, None] + 1.0) * log_gamma[None, :]))
    kdec = lanes(jnp.exp((c - 1.0 - idx)[:, None] * log_gamma[None, :]))
    gchunk = lanes(jnp.exp(c * log_gamma)[None, :])

    W = RET_WIDTH
    tt = min(4 * c, L)
    nt = L // tt
    tok = lambda blk: pl.BlockSpec((tt, W), lambda b, i: (b * nt + i, blk))
    full = lambda shape: pl.BlockSpec(shape, lambda b, i: (0,) * len(shape))
    st_spec = pl.BlockSpec((1, RET_HEADS, D, D), lambda b, i: (b, 0, 0, 0))
    y, s_out = pl.pallas_call(
        _ret_kernel,
        grid=(B, nt),
        in_specs=[tok(P_RET // W), tok(P_RET // W + 1), tok(P_RET // W + 2), tok(P_RET // W + 3),
                  pl.BlockSpec((tt, D), lambda b, i: (i, 0)), pl.BlockSpec((tt, D), lambda b, i: (i, 0)),
                  full((RET_HEADS, c, c)), full((c, W)), full((c, W)), full((1, W)), st_spec],
        out_specs=[pl.BlockSpec((tt, W), lambda b, i: (b * nt + i, 0)), st_spec],
        out_shape=[jax.ShapeDtypeStruct((T, W), F32), jax.ShapeDtypeStruct((B, RET_HEADS, D, D), F32)],
        scratch_shapes=[pltpu.VMEM((RET_HEADS, D, D), F32)],
        compiler_params=_params(("parallel", "arbitrary")),
        name="retention",
    )(P, P, P, P, cos, sin, dmask, qdec, kdec, gchunk, s0.astype(F32))
    return y, s_out


def _merge_kernel(x_ref, g0_ref, g1_ref, g2_ref, yr_ref, ys_ref, yt_ref, wr_ref, ws_ref, wt_ref, wo_ref,
                  o_ref):
    m = (jax.nn.sigmoid(g0_ref[...]) * _dot(_bf(yr_ref[...]), wr_ref[...])
         + jax.nn.sigmoid(g1_ref[...]) * _dot_tn(_bf(ys_ref[0]), ws_ref[...])
         + jax.nn.sigmoid(g2_ref[...]) * _dot(_bf(yt_ref[...]), wt_ref[...]))
    o_ref[...] = x_ref[...] + _dot(_bf(m), wo_ref[...])


def _merge(x2d, P, y_rw, y_sa_t, y_ret, w_rw, w_sa, w_ret, w_o):
    T = x2d.shape[0]
    L = y_sa_t.shape[2]
    tm = min(512, L)
    nl = L // tm
    D = D_MODEL
    tok = lambda w, blk: pl.BlockSpec((tm, w), lambda i: (i, blk))
    full = lambda shape: pl.BlockSpec(shape, lambda i: (0, 0))
    return pl.pallas_call(
        _merge_kernel,
        grid=(T // tm,),
        in_specs=[tok(D, 0), tok(D, 0), tok(D, 1), tok(D, 2), tok(RW_WIDTH, 0),
                  pl.BlockSpec((1, SA_WIDTH, tm), lambda i: (i // nl, 0, i % nl)),
                  tok(RET_WIDTH, 0), full((RW_WIDTH, D)), full((SA_WIDTH, D)), full((RET_WIDTH, D)),
                  full((D, D))],
        out_specs=tok(D, 0),
        out_shape=jax.ShapeDtypeStruct((T, D), F32),
        compiler_params=_params(("parallel",)),
        name="merge",
    )(x2d, P, P, P, y_rw, y_sa_t, y_ret, w_rw, w_sa, w_ret, w_o)


def _mlp_kernel(x_ref, g_ref, wu_ref, wd_ref, gf_ref, o_ref, h_ref, acc_ref, *, final_norm):
    j = pl.program_id(1)

    @pl.when(j == 0)
    def _():
        x = x_ref[...]
        ms = jnp.mean(x * x, axis=-1, keepdims=True)
        h_ref[...] = _bf(x * lax.rsqrt(ms + EPS) * g_ref[...])
        acc_ref[...] = jnp.zeros_like(acc_ref)

    u = jnp.maximum(_dot(h_ref[...], wu_ref[...]), 0.0)
    acc_ref[...] += _dot(_bf(u * u), wd_ref[...])

    @pl.when(j == pl.num_programs(1) - 1)
    def _():
        xn = x_ref[...] + acc_ref[...]
        if final_norm:
            ms = jnp.mean(xn * xn, axis=-1, keepdims=True)
            xn = xn * lax.rsqrt(ms + EPS) * gf_ref[...]
        o_ref[...] = xn


def _mlp(x2d, g, w_up, w_down, g_final, final_norm):
    T = x2d.shape[0]
    tm = min(1024, T)
    tf = 1024
    D = D_MODEL
    return pl.pallas_call(
        functools.partial(_mlp_kernel, final_norm=final_norm),
        grid=(T // tm, D_FF // tf),
        in_specs=[pl.BlockSpec((tm, D), lambda i, j: (i, 0)),
                  pl.BlockSpec((1, D), lambda i, j: (0, 0)),
                  pl.BlockSpec((D, tf), lambda i, j: (0, j)),
                  pl.BlockSpec((tf, D), lambda i, j: (j, 0)),
                  pl.BlockSpec((1, D), lambda i, j: (0, 0))],
        out_specs=pl.BlockSpec((tm, D), lambda i, j: (i, 0)),
        out_shape=jax.ShapeDtypeStruct((T, D), F32),
        scratch_shapes=[pltpu.VMEM((tm, D), BF16), pltpu.VMEM((tm, D), F32)],
        compiler_params=_params(("parallel", "arbitrary")),
        name="mlp",
    )(x2d, g, w_up, w_down, g_final)


def _permute_w_in(w):
    rw, sa, ret = 0, RW_COLS, RW_COLS + SA_COLS
    gate = ret + RET_COLS
    sa_k = sa + SA_WIDTH
    sa_v = sa_k + SA_KV_WIDTH
    sa_qi = sa_v + SA_KV_WIDTH
    sa_ki = sa_qi + IDX_WIDTH
    used = P_KIWI + IDX_DIM + IDX_HEADS
    parts = [w[:, gate:gate + GATE_COLS], w[:, ret:ret + RET_COLS], w[:, rw:rw + 3 * RW_WIDTH],
             w[:, sa:sa + SA_WIDTH], w[:, rw + 3 * RW_WIDTH:rw + RW_COLS], w[:, sa_qi:sa_qi + IDX_WIDTH],
             w[:, sa_k:sa_k + SA_KV_WIDTH], w[:, sa_v:sa_v + SA_KV_WIDTH],
             w[:, sa_ki:sa_ki + IDX_DIM + IDX_HEADS], jnp.zeros((w.shape[0], P_COLS - used), w.dtype)]
    return _bf(jnp.concatenate(parts, axis=1))


def _layer(x2d, B, L, past, caches, lp, wts, g_final, final_norm):
    k_past, v_past, ik_past, s_rw, shift_rw, s_ret = caches
    pos = past + jnp.arange(L, dtype=jnp.int32)
    row = lambda t: t.reshape(1, -1).astype(F32)
    P = _in_proj(x2d, row(lp['norm1_g']), wts['w_in'])
    q_rot, qi_rot, k_rot, kw_rot = _sa_prep(P, L, _sa_tables(pos))
    y_rw, s_rw_new = _rwkv(P, B, L, shift_rw, s_rw, lp)
    y_sa = _dsa(P, B, L, past, q_rot, qi_rot, k_rot, kw_rot, k_past, v_past, ik_past)
    y_ret, s_ret_new = _retention(P, B, L, pos, s_ret)
    x2d = _merge(x2d, P, y_rw, y_sa, y_ret, wts['w_br_rwkv'], wts['w_br_dsa'], wts['w_br_ret'], wts['w_o'])
    x2d = _mlp(x2d, row(lp['norm2_g']), wts['w_up'], wts['w_down'], g_final, final_norm)
    P3 = P.reshape(B, L, P_COLS)
    last = P3[:, L - 1]
    shift_new = jnp.concatenate([last[:, P_RKV:P_RKV + 3 * RW_WIDTH], last[:, P_LORA:P_LORA + RW_LORA]], axis=1)
    k_new = k_rot.reshape(B, L, SA_KV_HEADS, SA_HEAD_DIM)
    v_new = P3[:, :, P_SAV:P_SAV + SA_KV_WIDTH].reshape(B, L, SA_KV_HEADS, SA_HEAD_DIM)
    ik_new = kw_rot.reshape(B, L, LANES)[:, :, :IDX_DIM]
    return x2d, (k_new, v_new, ik_new, s_rw_new, shift_new, s_ret_new)


def kernel(x_prompt, x_sample, cache_dsa_k, cache_dsa_v, cache_dsa_ik, state_rwkv, state_rwkv_shift, state_ret, norm1_g, w_in, rwkv_mu, rwkv_w0, rwkv_w2, rwkv_a0, rwkv_a2, rwkv_g2, rwkv_k_k, rwkv_k_a, rwkv_r_k, rwkv_lnx_g, w_br_rwkv, w_br_dsa, w_br_ret, w_o, norm2_g, w_up, w_down, final_norm_g):
    params = {
        'norm1_g': norm1_g, 'rwkv_mu': rwkv_mu, 'rwkv_w0': rwkv_w0, 'rwkv_w2': rwkv_w2,
        'rwkv_a0': rwkv_a0, 'rwkv_a2': rwkv_a2, 'rwkv_g2': rwkv_g2, 'rwkv_k_k': rwkv_k_k,
        'rwkv_k_a': rwkv_k_a, 'rwkv_r_k': rwkv_r_k, 'rwkv_lnx_g': rwkv_lnx_g, 'norm2_g': norm2_g,
    }
    depth = w_in.shape[0]
    Bp, Lp, D = x_prompt.shape
    Bs, Ls, _ = x_sample.shape
    past_s = cache_dsa_k.shape[2]
    xp = x_prompt.reshape(Bp * Lp, D).astype(F32)
    xs = x_sample.reshape(Bs * Ls, D).astype(F32)
    g_final = final_norm_g.reshape(1, D).astype(F32)
    zero_p = (None, None, None,
              jnp.zeros((Bp, RW_HEADS, RW_HEAD_DIM, RW_HEAD_DIM), F32), jnp.zeros((Bp, RW_COLS), F32),
              jnp.zeros((Bp, RET_HEADS, RET_HEAD_DIM, RET_HEAD_DIM), F32))
    p_states = [[] for _ in range(6)]
    s_states = [[] for _ in range(6)]
    for i in range(depth):
        lp = {name: arr[i] for name, arr in params.items()}
        wts = {'w_in': _permute_w_in(w_in[i]), 'w_br_rwkv': _bf(w_br_rwkv[i]), 'w_br_dsa': _bf(w_br_dsa[i]),
               'w_br_ret': _bf(w_br_ret[i]), 'w_o': _bf(w_o[i]), 'w_up': _bf(w_up[i]), 'w_down': _bf(w_down[i])}
        final = i == depth - 1
        cache_s = (cache_dsa_k[i], cache_dsa_v[i], cache_dsa_ik[i], state_rwkv[i], state_rwkv_shift[i],
                   state_ret[i])
        xp, new_p = _layer(xp, Bp, Lp, 0, zero_p, lp, wts, g_final, final)
        xs, new_s = _layer(xs, Bs, Ls, past_s, cache_s, lp, wts, g_final, final)
        for j in range(6):
            p_states[j].append(new_p[j])
            s_states[j].append(new_s[j])
    y_prompt = xp.reshape(Bp, Lp, D)
    y_sample = xs.reshape(Bs, Ls, D)
    p_out = [jnp.stack(t, axis=0) for t in p_states]
    s_out = [jnp.stack(t, axis=0) for t in s_states]
    return (y_prompt, y_sample, *p_out, *s_out)
```

```python
import functools

import numpy as np
import jax
import jax.numpy as jnp
from jax import lax
from jax.experimental import pallas as pl
from jax.experimental.pallas import tpu as pltpu

F32 = jnp.float32
BF16 = jnp.bfloat16
I32 = jnp.int32

D_MODEL = 1024
CHUNK = 64
Q_BLOCK = 128
EPS = 1e-6

RW_HEADS = 8
RW_HEAD_DIM = 64
RW_WIDTH = RW_HEADS * RW_HEAD_DIM
RW_DECAY_LORA = 64
RW_ICLR_LORA = 64
RW_GATE_LORA = 128
RW_LORA = RW_DECAY_LORA + RW_ICLR_LORA + RW_GATE_LORA
RW_COLS = 3 * RW_WIDTH + RW_LORA
RW_GN_EPS = 64e-5

SA_HEADS = 8
SA_KV_HEADS = 2
SA_HEAD_DIM = 64
SA_WIDTH = SA_HEADS * SA_HEAD_DIM
SA_KV_WIDTH = SA_KV_HEADS * SA_HEAD_DIM
IDX_HEADS = 4
IDX_DIM = 64
IDX_WIDTH = IDX_HEADS * IDX_DIM
TOPK_MAX = 256
ROPE_THETA = 500000.0
ROPE_DIM = SA_HEAD_DIM // 4
INDEX_SCALE = (IDX_DIM ** -0.5) * (IDX_HEADS ** -0.5)
SA_COLS = SA_WIDTH + 2 * SA_KV_WIDTH + IDX_WIDTH + IDX_DIM + IDX_HEADS

RET_HEADS = 4
RET_HEAD_DIM = 128
RET_WIDTH = RET_HEADS * RET_HEAD_DIM
RET_ROPE_BASE = 10000.0
RET_COLS = 4 * RET_WIDTH

N_BRANCH = 3
GATE_COLS = N_BRANCH * D_MODEL
IN_COLS = RW_COLS + SA_COLS + RET_COLS + GATE_COLS
D_FF = 4 * D_MODEL

LANES = 128
TILE_ELEMS = 64 * 8 * LANES

P_GATE = 0
P_RET = P_GATE + GATE_COLS
P_RKV = P_RET + RET_COLS
P_SAQ = P_RKV + 3 * RW_WIDTH
P_LORA = P_SAQ + SA_WIDTH
P_QI = P_LORA + RW_LORA
P_SAK = P_QI + IDX_WIDTH
P_SAV = P_SAK + SA_KV_WIDTH
P_KIWI = P_SAV + SA_KV_WIDTH
P_COLS = 8192
INT_MIN = -2 ** 31
NEG_BIG = -1e30
LOG2_E = 1.4426950408889634
VMEM_LIMIT = 56 * 1024 * 1024


def _bf(x):
    return x.astype(BF16)


def _dot(a, b):
    return jnp.dot(a, b, preferred_element_type=F32)


def _dot_nt(a, b):
    return lax.dot_general(a, b, (((1,), (1,)), ((), ())), preferred_element_type=F32)


def _dot_tn(a, b):
    return lax.dot_general(a, b, (((0,), (0,)), ((), ())), preferred_element_type=F32)


def _dot_split2(a_exact, x):
    hi = _bf(x)
    lo = _bf(x - hi.astype(F32))
    return _dot(a_exact, hi) + _dot(a_exact, lo)


def _params(sem):
    return pltpu.CompilerParams(dimension_semantics=sem, vmem_limit_bytes=VMEM_LIMIT)


def _in_proj_kernel(x_ref, g_ref, w_ref, o_ref, h_ref):
    @pl.when(pl.program_id(1) == 0)
    def _():
        x = x_ref[...]
        ms = jnp.mean(x * x, axis=-1, keepdims=True)
        h_ref[...] = _bf(x * lax.rsqrt(ms + EPS) * g_ref[...])

    o_ref[...] = _dot(h_ref[...], w_ref[0])


def _in_proj(x2d, g, w_all, layer):
    T = x2d.shape[0]
    tm = min(1024, T)
    tn = 1024
    return pl.pallas_call(
        _in_proj_kernel,
        grid=(T // tm, P_COLS // tn),
        in_specs=[pl.BlockSpec((tm, D_MODEL), lambda i, j: (i, 0)),
                  pl.BlockSpec((1, D_MODEL), lambda i, j: (0, 0)),
                  pl.BlockSpec((1, D_MODEL, tn), lambda i, j: (layer, 0, j))],
        out_specs=pl.BlockSpec((tm, tn), lambda i, j: (i, j)),
        out_shape=jax.ShapeDtypeStruct((T, P_COLS), F32),
        scratch_shapes=[pltpu.VMEM((tm, D_MODEL), BF16)],
        compiler_params=_params(("parallel", "arbitrary")),
        name="in_proj",
    )(x2d, g, w_all)


KIWI_COLS = IDX_DIM + IDX_HEADS
TAIL0 = RW_COLS + SA_COLS - KIWI_COLS
TAIL_W = (IN_COLS - TAIL0 + LANES - 1) // LANES * LANES


def _w_prep_kernel(w_ref, o_ref):
    x = w_ref[0]
    x = jnp.where(lax.broadcasted_iota(I32, x.shape, 1) < IN_COLS, x, 0.0)
    lane = lax.broadcasted_iota(I32, (x.shape[0], LANES), 1)

    def put(off, v):
        o_ref[0, :, off:off + v.shape[1]] = _bf(v)

    nslab = TAIL_W // LANES
    rolled = [pltpu.roll(x[:, TAIL0 + s * LANES:TAIL0 + (s + 1) * LANES], LANES - KIWI_COLS, 1)
              for s in range(nslab)]
    for s in range((RET_COLS + GATE_COLS) // LANES):
        v = jnp.where(lane < LANES - KIWI_COLS, rolled[s], rolled[s + 1])
        put((P_RET if s < RET_COLS // LANES else P_GATE - RET_COLS) + s * LANES, v)
    sa = RW_COLS
    put(P_RKV, x[:, 0:3 * RW_WIDTH])
    put(P_LORA, x[:, 3 * RW_WIDTH:RW_COLS])
    put(P_SAQ, x[:, sa:sa + SA_WIDTH])
    put(P_SAK, x[:, sa + SA_WIDTH:sa + SA_WIDTH + SA_KV_WIDTH])
    put(P_SAV, x[:, sa + SA_WIDTH + SA_KV_WIDTH:sa + SA_WIDTH + 2 * SA_KV_WIDTH])
    put(P_QI, x[:, sa + SA_WIDTH + 2 * SA_KV_WIDTH:TAIL0])
    put(P_KIWI, jnp.where(lane < KIWI_COLS, x[:, TAIL0:TAIL0 + LANES], 0.0))
    put(P_KIWI + LANES, jnp.zeros((x.shape[0], P_COLS - P_KIWI - LANES), F32))


def _w_prep(w_in):
    depth, d, _ = w_in.shape
    tm = 256
    return pl.pallas_call(
        _w_prep_kernel,
        grid=(depth, d // tm),
        in_specs=[pl.BlockSpec((1, tm, TAIL0 + TAIL_W), lambda l, i: (l, i, 0))],
        out_specs=pl.BlockSpec((1, tm, P_COLS), lambda l, i: (l, i, 0)),
        out_shape=jax.ShapeDtypeStruct((depth, d, P_COLS), BF16),
        compiler_params=_params(("parallel", "parallel")),
        name="w_prep",
    )(w_in)


def _rot_slab(x, c, s_lo, s_hi, shift):
    return x * c + pltpu.roll(x, LANES - shift, 1) * s_lo + pltpu.roll(x, shift, 1) * s_hi


def _sa_prep_kernel(q_ref, qi_ref, k_ref, kw_ref, c_ref, s1_ref, s2_ref, ck_ref, s1k_ref, s2k_ref,
                    qo_ref, qio_ref, ko_ref, kwo_ref):
    c, s1, s2 = c_ref[...], s1_ref[...], s2_ref[...]
    half = ROPE_DIM // 2
    scale = SA_HEAD_DIM ** -0.5 * LOG2_E
    for s in range(SA_WIDTH // LANES):
        sl = slice(s * LANES, (s + 1) * LANES)
        qo_ref[:, sl] = _rot_slab(q_ref[:, sl], c, s1, s2, half) * scale
    for s in range(IDX_WIDTH // LANES):
        sl = slice(s * LANES, (s + 1) * LANES)
        qio_ref[:, sl] = _rot_slab(qi_ref[:, sl], c, s1, s2, half)
    ko_ref[...] = _rot_slab(k_ref[...], c, s1, s2, half)
    kwo_ref[...] = _rot_slab(kw_ref[...], ck_ref[...], s1k_ref[...], s2k_ref[...], half)


def _sa_prep(P, L, tabs):
    T = P.shape[0]
    tm = min(512, L)
    nl = L // tm
    tab_spec = pl.BlockSpec((tm, LANES), lambda i: (i % nl, 0))
    return pl.pallas_call(
        _sa_prep_kernel,
        grid=(T // tm,),
        in_specs=[pl.BlockSpec((tm, SA_WIDTH), lambda i: (i, P_SAQ // SA_WIDTH)),
                  pl.BlockSpec((tm, IDX_WIDTH), lambda i: (i, P_QI // IDX_WIDTH)),
                  pl.BlockSpec((tm, LANES), lambda i: (i, P_SAK // LANES)),
                  pl.BlockSpec((tm, LANES), lambda i: (i, P_KIWI // LANES))] + [tab_spec] * 6,
        out_specs=[pl.BlockSpec((tm, SA_WIDTH), lambda i: (i, 0)),
                   pl.BlockSpec((tm, IDX_WIDTH), lambda i: (i, 0)),
                   pl.BlockSpec((tm, LANES), lambda i: (i, 0)),
                   pl.BlockSpec((tm, LANES), lambda i: (i, 0))],
        out_shape=[jax.ShapeDtypeStruct((T, SA_WIDTH), F32),
                   jax.ShapeDtypeStruct((T, IDX_WIDTH), F32),
                   jax.ShapeDtypeStruct((T, LANES), F32),
                   jax.ShapeDtypeStruct((T, LANES), F32)],
        compiler_params=_params(("parallel",)),
        name="sa_prep",
    )(P, P, P, P, *tabs)


def _sa_tables(pos):
    half = ROPE_DIM // 2
    freqs = 1.0 / (ROPE_THETA ** (jnp.arange(0, ROPE_DIM, 2, dtype=F32) / ROPE_DIM))
    ang = pos.astype(F32)[:, None] * freqs[None, :]
    cos, sin = jnp.cos(ang), jnp.sin(ang)
    n = pos.shape[0]
    pad = SA_HEAD_DIM - ROPE_DIM
    c_head = jnp.concatenate([cos, cos, jnp.ones((n, pad), F32)], axis=1)
    s1_head = jnp.concatenate([-sin, jnp.zeros((n, half + pad), F32)], axis=1)
    s2_head = jnp.concatenate([jnp.zeros((n, half), F32), sin, jnp.zeros((n, pad), F32)], axis=1)
    one, zero = jnp.ones((n, SA_HEAD_DIM), F32), jnp.zeros((n, SA_HEAD_DIM), F32)
    two = lambda t: jnp.concatenate([t, t], axis=1)
    return (two(c_head), two(s1_head), two(s2_head),
            jnp.concatenate([c_head, one], axis=1), jnp.concatenate([s1_head, zero], axis=1),
            jnp.concatenate([s2_head, zero], axis=1))


def _softplus(u):
    return jnp.maximum(u, 0.0) + jnp.log(1.0 + jnp.exp(-jnp.abs(u)))


def _rwkv_kernel(r_ref, k_ref, v_ref, lo_ref, shr_ref, shk_ref, shv_ref, shlo_ref, s0_ref,
                 mur_ref, muk_ref, muv_ref, mulo_ref, w0_ref, w2_ref, a0_ref, a2_ref, g2_ref,
                 kk_ref, ka_ref, rk_ref, lnx_ref, e_ref, tin_ref, tsuf_ref,
                 y_ref, so_ref,
                 cr_ref, ck_ref, cv_ref, clo_ref, s_ref,
                 kap_ref, rt_ref, bh_ref, kh_ref, bt_ref, kt_ref, vv_ref, gc_ref, yy_ref, rr_ref,
                 ac_ref, cc_ref):
    c = pl.program_id(1)
    TT = r_ref.shape[0]
    nch = TT // CHUNK
    N = RW_HEAD_DIM

    @pl.when(c == 0)
    def _():
        cr_ref[0:1, :] = shr_ref[0]
        ck_ref[0:1, :] = shk_ref[0]
        cv_ref[0:1, :] = shv_ref[0]
        clo_ref[0:1, :] = shlo_ref[0]
        s_ref[...] = s0_ref[0]

    def lerp(p_ref, carry_ref, mu_ref):
        p = p_ref[...]
        rolled = pltpu.roll(p, 1, 0)
        row0 = lax.broadcasted_iota(I32, p.shape, 0) == 0
        prev = jnp.where(row0, carry_ref[0:1, :], rolled)
        carry_ref[0:1, :] = p[TT - 1:TT, :]
        return p + (prev - p) * mu_ref[...]

    xr = lerp(r_ref, cr_ref, mur_ref)
    xk = lerp(k_ref, ck_ref, muk_ref)
    xv = lerp(v_ref, cv_ref, muv_ref)
    xlo = lerp(lo_ref, clo_ref, mulo_ref)
    xw = xlo[:, :RW_DECAY_LORA]
    xa = xlo[:, RW_DECAY_LORA:RW_DECAY_LORA + RW_ICLR_LORA]
    xg = xlo[:, RW_DECAY_LORA + RW_ICLR_LORA:]

    z = w0_ref[...] + _dot(_bf(jnp.tanh(xw)), w2_ref[...])
    w = -_softplus(-z) - 0.5
    ld = -jnp.exp(w)
    a = jax.nn.sigmoid(a0_ref[...] + _dot(_bf(xa), a2_ref[...]))
    gate = _dot(_bf(jax.nn.sigmoid(xg)), g2_ref[...])
    e_blk = e_ref[...]
    kk = xk * kk_ref[...]
    kk = kk / jnp.maximum(jnp.sqrt(_dot(_bf(kk * kk), e_blk)), 1e-12)
    k2 = xk * (1.0 + (a - 1.0) * ka_ref[...])
    bb = kk * a
    bonus = _dot(_bf(xr * k2 * rk_ref[...]), e_blk) * xv

    lin = _dot_split2(tin_ref[...], ld)
    lsuf = _dot_split2(tsuf_ref[...], ld)
    e_in = jnp.exp(lin)
    e_ninv = jnp.exp(-lin)
    e_suf = jnp.exp(lsuf)
    kap_ref[...] = kk * jnp.exp(lin - ld)
    rt_ref[...] = xr * e_in
    bh_ref[...] = bb * e_ninv
    kh_ref[...] = k2 * e_ninv
    bt_ref[...] = bb * e_suf
    kt_ref[...] = k2 * e_suf
    vv_ref[...] = xv
    gc_ref[...] = jnp.exp(lin + lsuf)

    ri = lax.broadcasted_iota(I32, (CHUNK, CHUNK), 0)
    ci = lax.broadcasted_iota(I32, (CHUNK, CHUNK), 1)
    strict = ri > ci
    incl = ri >= ci
    eye = (ri == ci).astype(F32)
    heads = range(RW_HEADS)
    hsl = [slice(h * N, (h + 1) * N) for h in heads]

    per_it = 2 if nch % 2 == 0 else 1

    def coef_body(it, carry):
        chs = [it * per_it + t for t in range(per_it)]
        pairs = [(t, h) for t in range(per_it) for h in heads]
        rows = [pl.ds(pl.multiple_of(ch * CHUNK, CHUNK), CHUNK) for ch in chs]
        ld = lambda ref, p: ref[rows[p[0]], hsl[p[1]]]
        kap = [ld(kap_ref, p) for p in pairs]
        rt = [ld(rt_ref, p) for p in pairs]
        vh = [ld(vv_ref, p) for p in pairs]
        idx = range(len(pairs))
        gmat = [_dot_nt(_bf(jnp.concatenate([kap[n], rt[n]], axis=0)),
                        _bf(jnp.concatenate([ld(bh_ref, pairs[n]), ld(kh_ref, pairs[n])], axis=0)))
                for n in idx]
        n_ab = [jnp.where(strict, g[:CHUNK, :CHUNK], 0.0) for g in gmat]
        m_rb = [jnp.where(incl, g[CHUNK:, :CHUNK], 0.0) for g in gmat]
        m_v = [_bf(jnp.concatenate([jnp.where(strict, g[:CHUNK, CHUNK:], 0.0),
                                    jnp.where(incl, g[CHUNK:, CHUNK:], 0.0)], axis=0)) for g in gmat]
        mv = [_dot(m_v[n], _bf(vh[n])) for n in idx]
        x_inv = [eye - t for t in n_ab]
        pw = n_ab
        for _ in range(5):
            pwb = [_bf(p) for p in pw]
            pw = [_dot(p, p) for p in pwb]
            x_inv = [x + _dot(_bf(x), _bf(p)) for x, p in zip(x_inv, pw)]
        w = [_dot(_bf(x_inv[n]), _bf(jnp.concatenate([kap[n], mv[n][:CHUNK]], axis=1))) for n in idx]
        wb = [_bf(t) for t in w]
        ry = [jnp.concatenate([rt[n], mv[n][CHUNK:]], axis=1) - _dot(_bf(m_rb[n]), wb[n]) for n in idx]
        dmat = [_dot_tn(wb[n], _bf(ld(bt_ref, pairs[n]))) for n in idx]
        vtk = [_dot_tn(_bf(vh[n]), _bf(ld(kt_ref, pairs[n]))) for n in idx]
        for n, (t, h) in enumerate(pairs):
            rr_ref[rows[t], hsl[h]] = ry[n][:, :N]
            yy_ref[rows[t], hsl[h]] = ry[n][:, N:]
            ac_ref[chs[t], h] = -dmat[n][:N]
            cc_ref[chs[t], h] = vtk[n] - dmat[n][N:]
        return carry

    lax.fori_loop(0, nch // per_it, coef_body, 0)

    def state_body(ch, carry):
        rows = pl.ds(pl.multiple_of(ch * CHUNK, CHUNK), CHUNK)
        s_old = [s_ref[h] for h in heads]
        sb = [_bf(t) for t in s_old]
        s_new = [_dot(sb[h], _bf(ac_ref[ch, h])) for h in heads]
        y_c = [_dot_nt(_bf(rr_ref[rows, hsl[h]]), sb[h]) for h in heads]
        for h in heads:
            gch = gc_ref[pl.ds(pl.multiple_of(ch * CHUNK, CHUNK), 1), hsl[h]]
            s_ref[h] = s_old[h] * gch + s_new[h] + cc_ref[ch, h]
            yy_ref[rows, hsl[h]] = yy_ref[rows, hsl[h]] + y_c[h]
        return carry

    lax.fori_loop(0, nch, state_body, 0)

    y = yy_ref[...]
    mean = _dot(_bf(y), e_blk) * (1.0 / N)
    d = y - mean
    var = _dot(_bf(d * d), e_blk) * (1.0 / N)
    yn = d * lax.rsqrt(var + RW_GN_EPS) * lnx_ref[...]
    y_ref[...] = (yn + bonus) * gate

    @pl.when(c == pl.num_programs(1) - 1)
    def _():
        so_ref[0] = s_ref[...]


def _rwkv(P, B, L, shift_prev, s0, lp):
    T = P.shape[0]
    TT = min(256, L)
    nt = L // TT
    W = RW_WIDTH
    row = lambda t: t.reshape(1, -1).astype(F32)
    mu = lp['rwkv_mu']
    sh = shift_prev.astype(F32)
    pieces = lambda t: (t[..., 0:W], t[..., W:2 * W], t[..., 2 * W:3 * W], t[..., 3 * W:])
    mu_r, mu_k, mu_v, mu_lo = [row(t) for t in pieces(mu)]
    sh_r, sh_k, sh_v, sh_lo = [t.reshape(B, 1, -1) for t in pieces(sh)]
    hid = jnp.arange(W) // RW_HEAD_DIM
    e_blk = (hid[:, None] == hid[None, :]).astype(BF16)
    ti = jnp.arange(TT)
    same = (ti[:, None] // CHUNK) == (ti[None, :] // CHUNK)
    tri_in = (same & (ti[None, :] <= ti[:, None])).astype(BF16)
    tri_suf = (same & (ti[None, :] > ti[:, None])).astype(BF16)

    tok = lambda w, blk: pl.BlockSpec((TT, w), lambda b, c: (b * nt + c, blk))
    full = lambda shape: pl.BlockSpec(shape, lambda b, c: (0,) * len(shape))
    shs = lambda w: pl.BlockSpec((1, 1, w), lambda b, c: (b, 0, 0))
    st_spec = pl.BlockSpec((1, RW_HEADS, RW_HEAD_DIM, RW_HEAD_DIM), lambda b, c: (b, 0, 0, 0))
    big = lambda: pltpu.VMEM((TT, W), F32)
    y, s_out = pl.pallas_call(
        _rwkv_kernel,
        grid=(B, nt),
        in_specs=[tok(W, P_RKV // W), tok(W, P_RKV // W + 1), tok(W, P_RKV // W + 2),
                  tok(RW_LORA, P_LORA // RW_LORA),
                  shs(W), shs(W), shs(W), shs(RW_LORA), st_spec,
                  full((1, W)), full((1, W)), full((1, W)), full((1, RW_LORA)),
                  full((1, W)), full((RW_DECAY_LORA, W)), full((1, W)), full((RW_ICLR_LORA, W)),
                  full((RW_GATE_LORA, W)), full((1, W)), full((1, W)), full((1, W)), full((1, W)),
                  full((W, W)), full((TT, TT)), full((TT, TT))],
        out_specs=[pl.BlockSpec((TT, W), lambda b, c: (b * nt + c, 0)), st_spec],
        out_shape=[jax.ShapeDtypeStruct((T, W), F32),
                   jax.ShapeDtypeStruct((B, RW_HEADS, RW_HEAD_DIM, RW_HEAD_DIM), F32)],
        scratch_shapes=[pltpu.VMEM((8, W), F32), pltpu.VMEM((8, W), F32), pltpu.VMEM((8, W), F32),
                        pltpu.VMEM((8, RW_LORA), F32),
                        pltpu.VMEM((RW_HEADS, RW_HEAD_DIM, RW_HEAD_DIM), F32)] + [big() for _ in range(10)]
        + [pltpu.VMEM((TT // CHUNK, RW_HEADS, RW_HEAD_DIM, RW_HEAD_DIM), F32) for _ in range(2)],
        compiler_params=_params(("parallel", "arbitrary")),
        name="rwkv",
    )(P, P, P, P, sh_r, sh_k, sh_v, sh_lo, s0.astype(F32),
      mu_r, mu_k, mu_v, mu_lo, row(lp['rwkv_w0']), _bf(lp['rwkv_w2']), row(lp['rwkv_a0']),
      _bf(lp['rwkv_a2']), _bf(lp['rwkv_g2']), row(lp['rwkv_k_k']), row(lp['rwkv_k_a']),
      row(lp['rwkv_r_k']), row(lp['rwkv_lnx_g']), e_blk, tri_in, tri_suf)
    return y, s_out


ONES_ROWS = 16
BOUND_SLACK = 1.05
SAFE_BOUND = 60.0


def _dsa_kernel(q_ref, qi_ref, kw_ref, k_ref, v_ref, ki_ref, y_ref,
                kb_ref, vt_ref, kib_ref, kmax_ref, keys_ref, bias_ref, logit_ref, pstar_ref,
                *, q_off, past, lk_real, topk, kc):
    i = pl.program_id(1)
    qb = q_ref.shape[1]
    lk = k_ref.shape[1]
    nk = lk // kc
    kf = float(topk)
    HD = SA_HEAD_DIM

    ones_sq = jnp.ones((LANES, LANES), BF16)

    def head_slab(x, odd):
        lane = lax.broadcasted_iota(I32, x.shape, 1)
        return jnp.where(lane < HD, pltpu.roll(x, HD, 1) if odd else x, 0.0)

    @pl.when(i == 0)
    def _():
        kib_ref[...] = _bf(ki_ref[0])
        k = k_ref[0]
        lane = lax.broadcasted_iota(I32, k.shape, 1)
        for c in range(SA_KV_HEADS):
            kc_b = _bf(head_slab(k, c == 1))
            kb_ref[c] = jnp.where(lane == HD, jnp.ones_like(kc_b), kc_b)
            kf32 = kc_b.astype(F32)
            n2 = _dot(_bf(kf32 * kf32), ones_sq)
            kmax_ref[c] = jnp.broadcast_to(jnp.max(n2, axis=0, keepdims=True), (8, LANES))
        vt = v_ref[0].T
        for c in range(SA_KV_HEADS):
            vt_ref[c, 0:HD, :] = _bf(vt[c * HD:(c + 1) * HD, :])
            vt_ref[c, HD:HD + ONES_ROWS, :] = jnp.ones((ONES_ROWS, lk), BF16)

    qpos = past + q_off + i * qb + lax.broadcasted_iota(I32, (1, qb), 1)
    limit = jnp.minimum((qpos // CHUNK + 1) * CHUNK, lk_real)
    sub_idx = lax.broadcasted_iota(I32, (kc, qb), 0)
    kwt = kw_ref[0].T
    wi_h = [kwt[IDX_DIM + h:IDX_DIM + h + 1, :] * INDEX_SCALE for h in range(IDX_HEADS)]
    qi = qi_ref[0]
    qi_h = [_bf(qi[:, h * IDX_DIM:(h + 1) * IDX_DIM]) for h in range(IDX_HEADS)]
    chunks = [slice(j * kc, (j + 1) * kc) for j in range(nk)]

    for j, ks in enumerate(chunks):
        ki_j = kib_ref[ks, 0:IDX_DIM]
        s = jnp.zeros((kc, qb), F32)
        for h in range(IDX_HEADS):
            s = s + jnp.maximum(_dot_nt(ki_j, qi_h[h]), 0.0) * wi_h[h]
        s = jnp.where(s == 0.0, 0.0, s)
        bits = lax.bitcast_convert_type(s, I32)
        key = bits ^ ((bits >> 31) & 0x7FFFFFFF)
        keys_ref[ks, :] = jnp.where((sub_idx + j * kc) < limit, key, INT_MIN)

    def count(pred):
        acc = jnp.zeros((1, qb), F32)
        for j, ks in enumerate(chunks):
            acc = acc + jnp.sum(jnp.where(pred(keys_ref[ks, :], j), 1.0, 0.0), axis=0, keepdims=True)
        return acc

    def bit_body(it, prefix):
        cand = prefix | lax.shift_left(jnp.int32(1), 31 - it)
        thr = cand ^ INT_MIN
        cnt = count(lambda kj, j: kj >= thr)
        return jnp.where(cnt >= kf, cand, prefix)

    prefix = lax.fori_loop(0, 32, bit_body, jnp.zeros((1, qb), I32))
    tau = prefix ^ INT_MIN
    cnt_ge = count(lambda kj, j: kj >= tau)
    cnt_gt = count(lambda kj, j: kj > tau)
    need = kf - cnt_gt
    excess = jnp.logical_and(cnt_ge > kf, tau != INT_MIN)
    p_default = jnp.where(tau == INT_MIN, -1, 2 ** 30).astype(I32)
    pstar_ref[...] = jnp.broadcast_to(p_default, pstar_ref.shape)

    @pl.when(jnp.max(jnp.where(excess, 1.0, 0.0)) > 0.0)
    def _():
        nbits = max(1, int(lk - 1).bit_length())

        def idx_body(it, p):
            cand = p | lax.shift_left(jnp.int32(1), nbits - 1 - it)
            g = count(lambda kj, j: jnp.logical_and(kj == tau, (sub_idx + j * kc) < cand))
            return jnp.where(g < need, cand, p)

        p = lax.fori_loop(0, nbits, idx_body, jnp.zeros((1, qb), I32))
        pstar_ref[...] = jnp.broadcast_to(jnp.where(excess, p, p_default), pstar_ref.shape)

    pstar = pstar_ref[0:1, :]
    for j, ks in enumerate(chunks):
        kj = keys_ref[ks, :]
        sel = jnp.logical_or(kj > tau, jnp.logical_and(kj == tau, (sub_idx + j * kc) <= pstar))
        bias_ref[ks, :] = jnp.where(sel, 0.0, NEG_BIG)

    q = q_ref[0]
    group = SA_HEADS // SA_KV_HEADS
    lane_q = lax.broadcasted_iota(I32, (qb, LANES), 1)
    q_heads, bounds = [], []
    for h in range(SA_HEADS):
        qh = _bf(head_slab(q[:, (h // 2) * LANES:(h // 2 + 1) * LANES], h % 2 == 1))
        qf = qh.astype(F32)
        qn2 = _dot(_bf(qf * qf), ones_sq)
        q_heads.append(qh)
        bounds.append(jnp.sqrt(qn2 * kmax_ref[h // group, 0:1, :]) * BOUND_SLACK)
    worst = bounds[0]
    for b in bounds[1:]:
        worst = jnp.maximum(worst, b)
    safe = jnp.max(worst) <= SAFE_BOUND

    def finish(acc):
        return acc[:HD] / acc[HD:HD + 1]

    @pl.when(safe)
    def _():
        outs = []
        for h in range(SA_HEADS):
            c = h // group
            qa = jnp.where(lane_q == HD, _bf(-bounds[h]), q_heads[h])
            acc = jnp.zeros((HD + ONES_ROWS, qb), F32)
            for ks in chunks:
                p = jnp.exp2(_dot_nt(kb_ref[c, ks, :], qa) + bias_ref[ks, :])
                acc = acc + _dot(vt_ref[c, :, ks], _bf(p))
            outs.append(finish(acc))
        y_ref[0] = jnp.concatenate(outs, axis=0)

    @pl.when(jnp.logical_not(safe))
    def _():
        outs = []
        for h in range(SA_HEADS):
            c = h // group
            m = jnp.full((1, qb), NEG_BIG, F32)
            for ks in chunks:
                logit = _dot_nt(kb_ref[c, ks, :], q_heads[h]) + bias_ref[ks, :]
                logit_ref[ks, :] = logit
                m = jnp.maximum(m, jnp.max(logit, axis=0, keepdims=True))
            acc = jnp.zeros((HD + ONES_ROWS, qb), F32)
            for ks in chunks:
                acc = acc + _dot(vt_ref[c, :, ks], _bf(jnp.exp2(logit_ref[ks, :] - m)))
            outs.append(finish(acc))
        y_ref[0] = jnp.concatenate(outs, axis=0)


def _dsa_call(q3, qi3, kw3, k_all, v_all, ki_all, v_blk, *, q_off, n_q, lk, past, lk_real, topk, qb, kc):
    B = q3.shape[0]
    off = q_off // qb
    kern = functools.partial(_dsa_kernel, q_off=q_off, past=past, lk_real=lk_real, topk=topk, kc=kc)
    return pl.pallas_call(
        kern,
        grid=(B, n_q // qb),
        in_specs=[pl.BlockSpec((1, qb, SA_WIDTH), lambda b, i: (b, off + i, 0)),
                  pl.BlockSpec((1, qb, IDX_WIDTH), lambda b, i: (b, off + i, 0)),
                  pl.BlockSpec((1, qb, LANES), lambda b, i: (b, off + i, 0)),
                  pl.BlockSpec((1, lk, LANES), lambda b, i: (b, 0, 0)),
                  pl.BlockSpec((1, lk, LANES), lambda b, i: (b, 0, v_blk)),
                  pl.BlockSpec((1, lk, LANES), lambda b, i: (b, 0, 0))],
        out_specs=pl.BlockSpec((1, SA_WIDTH, qb), lambda b, i: (b, 0, i)),
        out_shape=jax.ShapeDtypeStruct((B, SA_WIDTH, n_q), F32),
        scratch_shapes=[pltpu.VMEM((SA_KV_HEADS, lk, LANES), BF16),
                        pltpu.VMEM((SA_KV_HEADS, SA_HEAD_DIM + ONES_ROWS, lk), BF16),
                        pltpu.VMEM((lk, LANES), BF16),
                        pltpu.VMEM((SA_KV_HEADS, 8, LANES), F32),
                        pltpu.VMEM((lk, qb), I32), pltpu.VMEM((lk, qb), F32), pltpu.VMEM((lk, qb), F32),
                        pltpu.VMEM((8, qb), I32)],
        compiler_params=_params(("parallel", "arbitrary")),
        name="dsa",
    )(q3, qi3, kw3, k_all, v_all, ki_all)


def _round_up(x, m):
    return (x + m - 1) // m * m


def _dsa(P, B, L, past, q_rot, qi_rot, k_rot, kw_rot, k_past, v_past, ik_past):
    lk_real = past + L
    topk = min(TOPK_MAX, lk_real // 4)
    qb = 2 * LANES if (past == 0 and L % (2 * LANES) == 0) else LANES
    lq = _round_up(L, qb)
    qpad = lambda t: t if lq == L else jnp.pad(t, ((0, 0), (0, lq - L), (0, 0)))
    q3 = qpad(q_rot.reshape(B, L, SA_WIDTH))
    qi3 = qpad(qi_rot.reshape(B, L, IDX_WIDTH))
    kw3 = kw_rot.reshape(B, L, LANES)
    k3 = k_rot.reshape(B, L, LANES)
    common = dict(past=past, lk_real=lk_real, topk=topk, qb=qb)
    if past == 0:
        ncls = next(n for n in (8, 4, 2, 1) if L % n == 0 and ((L // n) % 512 == 0 or n == 1))
        cl = L // ncls
        kc = TILE_ELEMS // qb if cl % (TILE_ELEMS // qb) == 0 else LANES
        P3 = P.reshape(B, L, P_COLS)
        ys = [_dsa_call(q3, qi3, kw3, k3, P3, kw3, P_SAV // LANES, q_off=c * cl, n_q=cl,
                        lk=(c + 1) * cl, kc=kc, **common) for c in range(ncls)]
        return ys[0] if ncls == 1 else jnp.concatenate(ys, axis=2)
    kc = 512
    lk = _round_up(lk_real, kc)
    zpad = jnp.zeros((B, lk - lk_real, LANES), F32)
    v_new = P.reshape(B, L, P_COLS)[:, :, P_SAV:P_SAV + LANES]
    ik_p = jnp.concatenate([ik_past.astype(F32), jnp.zeros((B, past, LANES - IDX_DIM), F32)], axis=2)
    k_all = jnp.concatenate([k_past.reshape(B, past, LANES).astype(F32), k3, zpad], axis=1)
    v_all = jnp.concatenate([v_past.reshape(B, past, LANES).astype(F32), v_new, zpad], axis=1)
    ki_all = jnp.concatenate([ik_p, kw3, zpad], axis=1)
    y = _dsa_call(q3, qi3, qpad(kw3), k_all, v_all, ki_all, 0, q_off=0, n_q=lq, lk=lk, kc=kc, **common)
    return y[:, :, :L]


def _ret_kernel(q_ref, k_ref, v_ref, g_ref, cos_ref, sin_ref, dm_ref, qd_ref, kd_ref, gc_ref, s0_ref,
                y_ref, so_ref, s_ref):
    c = pl.program_id(1)
    D = RET_HEAD_DIM
    cs = dm_ref.shape[1]
    nch = q_ref.shape[0] // cs
    heads = range(RET_HEADS)
    hsl = [slice(h * D, (h + 1) * D) for h in heads]
    rows = [slice(ch * cs, (ch + 1) * cs) for ch in range(nch)]
    prob = [(ch, h) for ch in range(nch) for h in heads]

    @pl.when(c == 0)
    def _():
        s_ref[...] = s0_ref[0]

    cos, sin = cos_ref[...], sin_ref[...]
    qb_, kb_, kd_, vb_ = [], [], [], []
    for h in heads:
        q = q_ref[:, hsl[h]]
        k = k_ref[:, hsl[h]]
        k = (k * cos + pltpu.roll(k, D // 2, 1) * sin) * (D ** -0.5)
        qb_.append(_bf(q * cos + pltpu.roll(q, D // 2, 1) * sin))
        kb_.append(_bf(k))
        kd_.append([_bf(k[r] * kd_ref[:, hsl[h]]) for r in rows])
        vb_.append(_bf(v_ref[:, hsl[h]]))
    scores = {(ch, h): _dot_nt(qb_[h][rows[ch]], kb_[h][rows[ch]]) * dm_ref[h] for ch, h in prob}
    ktv = {(ch, h): _dot_tn(kd_[h][ch], vb_[h][rows[ch]]) for ch, h in prob}
    intra = {(ch, h): _dot(_bf(scores[ch, h]), vb_[h][rows[ch]]) for ch, h in prob}
    s_at = {}
    for h in heads:
        s = s_ref[h]
        for ch in range(nch):
            s_at[ch, h] = _bf(s)
            s = s * gc_ref[:, hsl[h]] + ktv[ch, h]
        s_ref[h] = s
    cross = {(ch, h): _dot(qb_[h][rows[ch]], s_at[ch, h]) for ch, h in prob}
    for ch in range(nch):
        outs = []
        for h in heads:
            o = intra[ch, h] + cross[ch, h] * qd_ref[:, hsl[h]]
            o = o * lax.rsqrt(jnp.mean(o * o, axis=-1, keepdims=True) + EPS)
            outs.append(jax.nn.silu(g_ref[rows[ch], hsl[h]]) * o)
        y_ref[rows[ch], :] = jnp.concatenate(outs, axis=1)

    @pl.when(c == pl.num_programs(1) - 1)
    def _():
        so_ref[0] = s_ref[...]


def _retention(P, B, L, pos, s0):
    T = P.shape[0]
    c = min(CHUNK, L)
    nc = L // c
    D = RET_HEAD_DIM
    freqs = 1.0 / (RET_ROPE_BASE ** jnp.linspace(0.0, 1.0, D // 2, dtype=F32))
    ang = pos.astype(F32)[:, None] * freqs[None, :]
    cos = jnp.concatenate([jnp.cos(ang)] * 2, axis=1)
    sin = jnp.concatenate([-jnp.sin(ang), jnp.sin(ang)], axis=1)
    log_gamma = jnp.log1p(-jnp.exp2(-5.0 - jnp.arange(RET_HEADS, dtype=F32)))
    idx = jnp.arange(c, dtype=F32)
    dmask = jnp.exp(jnp.abs(idx[:, None] - idx[None, :])[None] * log_gamma[:, None, None])
    lanes = lambda t: jnp.repeat(t, D, axis=1)
    qdec = lanes(jnp.exp((idx[:, None] + 1.0) * log_gamma[None, :]))
    kdec = lanes(jnp.exp((c - 1.0 - idx)[:, None] * log_gamma[None, :]))
    gchunk = lanes(jnp.exp(c * log_gamma)[None, :])

    W = RET_WIDTH
    tt = min(4 * c, L)
    nt = L // tt
    tok = lambda blk: pl.BlockSpec((tt, W), lambda b, i: (b * nt + i, blk))
    full = lambda shape: pl.BlockSpec(shape, lambda b, i: (0,) * len(shape))
    st_spec = pl.BlockSpec((1, RET_HEADS, D, D), lambda b, i: (b, 0, 0, 0))
    y, s_out = pl.pallas_call(
        _ret_kernel,
        grid=(B, nt),
        in_specs=[tok(P_RET // W), tok(P_RET // W + 1), tok(P_RET // W + 2), tok(P_RET // W + 3),
                  pl.BlockSpec((tt, D), lambda b, i: (i, 0)), pl.BlockSpec((tt, D), lambda b, i: (i, 0)),
                  full((RET_HEADS, c, c)), full((c, W)), full((c, W)), full((1, W)), st_spec],
        out_specs=[pl.BlockSpec((tt, W), lambda b, i: (b * nt + i, 0)), st_spec],
        out_shape=[jax.ShapeDtypeStruct((T, W), F32), jax.ShapeDtypeStruct((B, RET_HEADS, D, D), F32)],
        scratch_shapes=[pltpu.VMEM((RET_HEADS, D, D), F32)],
        compiler_params=_params(("parallel", "arbitrary")),
        name="retention",
    )(P, P, P, P, cos, sin, dmask, qdec, kdec, gchunk, s0.astype(F32))
    return y, s_out


def _merge_kernel(x_ref, g0_ref, g1_ref, g2_ref, yr_ref, ys_ref, yt_ref, wr_ref, ws_ref, wt_ref, wo_ref,
                  o_ref):
    m = (jax.nn.sigmoid(g0_ref[...]) * _dot(_bf(yr_ref[...]), wr_ref[...])
         + jax.nn.sigmoid(g1_ref[...]) * _dot_tn(_bf(ys_ref[0]), ws_ref[...])
         + jax.nn.sigmoid(g2_ref[...]) * _dot(_bf(yt_ref[...]), wt_ref[...]))
    o_ref[...] = x_ref[...] + _dot(_bf(m), wo_ref[...])


def _merge(x2d, P, y_rw, y_sa_t, y_ret, w_rw, w_sa, w_ret, w_o):
    T = x2d.shape[0]
    L = y_sa_t.shape[2]
    tm = min(512, L)
    nl = L // tm
    D = D_MODEL
    tok = lambda w, blk: pl.BlockSpec((tm, w), lambda i: (i, blk))
    full = lambda shape: pl.BlockSpec(shape, lambda i: (0, 0))
    return pl.pallas_call(
        _merge_kernel,
        grid=(T // tm,),
        in_specs=[tok(D, 0), tok(D, 0), tok(D, 1), tok(D, 2), tok(RW_WIDTH, 0),
                  pl.BlockSpec((1, SA_WIDTH, tm), lambda i: (i // nl, 0, i % nl)),
                  tok(RET_WIDTH, 0), full((RW_WIDTH, D)), full((SA_WIDTH, D)), full((RET_WIDTH, D)),
                  full((D, D))],
        out_specs=tok(D, 0),
        out_shape=jax.ShapeDtypeStruct((T, D), F32),
        compiler_params=_params(("parallel",)),
        name="merge",
    )(x2d, P, P, P, y_rw, y_sa_t, y_ret, w_rw, w_sa, w_ret, w_o)


def _mlp_kernel(x_ref, g_ref, wu_ref, wd_ref, gf_ref, o_ref, h_ref, acc_ref, *, final_norm):
    j = pl.program_id(1)

    @pl.when(j == 0)
    def _():
        x = x_ref[...]
        ms = jnp.mean(x * x, axis=-1, keepdims=True)
        h_ref[...] = _bf(x * lax.rsqrt(ms + EPS) * g_ref[...])
        acc_ref[...] = jnp.zeros_like(acc_ref)

    u = jnp.maximum(_dot(h_ref[...], wu_ref[...]), 0.0)
    acc_ref[...] += _dot(_bf(u * u), wd_ref[...])

    @pl.when(j == pl.num_programs(1) - 1)
    def _():
        xn = x_ref[...] + acc_ref[...]
        if final_norm:
            ms = jnp.mean(xn * xn, axis=-1, keepdims=True)
            xn = xn * lax.rsqrt(ms + EPS) * gf_ref[...]
        o_ref[...] = xn


def _mlp(x2d, g, w_up, w_down, g_final, final_norm):
    T = x2d.shape[0]
    tm = min(1024, T)
    tf = 1024
    D = D_MODEL
    return pl.pallas_call(
        functools.partial(_mlp_kernel, final_norm=final_norm),
        grid=(T // tm, D_FF // tf),
        in_specs=[pl.BlockSpec((tm, D), lambda i, j: (i, 0)),
                  pl.BlockSpec((1, D), lambda i, j: (0, 0)),
                  pl.BlockSpec((D, tf), lambda i, j: (0, j)),
                  pl.BlockSpec((tf, D), lambda i, j: (j, 0)),
                  pl.BlockSpec((1, D), lambda i, j: (0, 0))],
        out_specs=pl.BlockSpec((tm, D), lambda i, j: (i, 0)),
        out_shape=jax.ShapeDtypeStruct((T, D), F32),
        scratch_shapes=[pltpu.VMEM((tm, D), BF16), pltpu.VMEM((tm, D), F32)],
        compiler_params=_params(("parallel", "arbitrary")),
        name="mlp",
    )(x2d, g, w_up, w_down, g_final)


def _layer(x2d, B, L, past, caches, lp, wts, g_final, final_norm):
    k_past, v_past, ik_past, s_rw, shift_rw, s_ret = caches
    pos = past + jnp.arange(L, dtype=jnp.int32)
    row = lambda t: t.reshape(1, -1).astype(F32)
    P = _in_proj(x2d, row(lp['norm1_g']), wts['w_in'], wts['layer'])
    q_rot, qi_rot, k_rot, kw_rot = _sa_prep(P, L, _sa_tables(pos))
    y_rw, s_rw_new = _rwkv(P, B, L, shift_rw, s_rw, lp)
    y_sa = _dsa(P, B, L, past, q_rot, qi_rot, k_rot, kw_rot, k_past, v_past, ik_past)
    y_ret, s_ret_new = _retention(P, B, L, pos, s_ret)
    x2d = _merge(x2d, P, y_rw, y_sa, y_ret, wts['w_br_rwkv'], wts['w_br_dsa'], wts['w_br_ret'], wts['w_o'])
    x2d = _mlp(x2d, row(lp['norm2_g']), wts['w_up'], wts['w_down'], g_final, final_norm)
    P3 = P.reshape(B, L, P_COLS)
    last = P3[:, L - 1]
    shift_new = jnp.concatenate([last[:, P_RKV:P_RKV + 3 * RW_WIDTH], last[:, P_LORA:P_LORA + RW_LORA]], axis=1)
    k_new = k_rot.reshape(B, L, SA_KV_HEADS, SA_HEAD_DIM)
    v_new = P3[:, :, P_SAV:P_SAV + SA_KV_WIDTH].reshape(B, L, SA_KV_HEADS, SA_HEAD_DIM)
    ik_new = kw_rot.reshape(B, L, LANES)[:, :, :IDX_DIM]
    return x2d, (k_new, v_new, ik_new, s_rw_new, shift_new, s_ret_new)


def kernel(x_prompt, x_sample, cache_dsa_k, cache_dsa_v, cache_dsa_ik, state_rwkv, state_rwkv_shift, state_ret, norm1_g, w_in, rwkv_mu, rwkv_w0, rwkv_w2, rwkv_a0, rwkv_a2, rwkv_g2, rwkv_k_k, rwkv_k_a, rwkv_r_k, rwkv_lnx_g, w_br_rwkv, w_br_dsa, w_br_ret, w_o, norm2_g, w_up, w_down, final_norm_g):
    params = {
        'norm1_g': norm1_g, 'rwkv_mu': rwkv_mu, 'rwkv_w0': rwkv_w0, 'rwkv_w2': rwkv_w2,
        'rwkv_a0': rwkv_a0, 'rwkv_a2': rwkv_a2, 'rwkv_g2': rwkv_g2, 'rwkv_k_k': rwkv_k_k,
        'rwkv_k_a': rwkv_k_a, 'rwkv_r_k': rwkv_r_k, 'rwkv_lnx_g': rwkv_lnx_g, 'norm2_g': norm2_g,
    }
    depth = w_in.shape[0]
    Bp, Lp, D = x_prompt.shape
    Bs, Ls, _ = x_sample.shape
    past_s = cache_dsa_k.shape[2]
    xp = x_prompt.reshape(Bp * Lp, D).astype(F32)
    xs = x_sample.reshape(Bs * Ls, D).astype(F32)
    g_final = final_norm_g.reshape(1, D).astype(F32)
    zero_p = (None, None, None,
              jnp.zeros((Bp, RW_HEADS, RW_HEAD_DIM, RW_HEAD_DIM), F32), jnp.zeros((Bp, RW_COLS), F32),
              jnp.zeros((Bp, RET_HEADS, RET_HEAD_DIM, RET_HEAD_DIM), F32))
    p_states = [[] for _ in range(6)]
    s_states = [[] for _ in range(6)]
    w_all = _w_prep(w_in)
    for i in range(depth):
        lp = {name: arr[i] for name, arr in params.items()}
        wts = {'w_in': w_all, 'layer': i, 'w_br_rwkv': _bf(w_br_rwkv[i]), 'w_br_dsa': _bf(w_br_dsa[i]),
               'w_br_ret': _bf(w_br_ret[i]), 'w_o': _bf(w_o[i]), 'w_up': _bf(w_up[i]), 'w_down': _bf(w_down[i])}
        final = i == depth - 1
        cache_s = (cache_dsa_k[i], cache_dsa_v[i], cache_dsa_ik[i], state_rwkv[i], state_rwkv_shift[i],
                   state_ret[i])
        xp, new_p = _layer(xp, Bp, Lp, 0, zero_p, lp, wts, g_final, final)
        xs, new_s = _layer(xs, Bs, Ls, past_s, cache_s, lp, wts, g_final, final)
        for j in range(6):
            p_states[j].append(new_p[j])
            s_states[j].append(new_s[j])
    y_prompt = xp.reshape(Bp, Lp, D)
    y_sample = xs.reshape(Bs, Ls, D)
    p_out = [jnp.stack(t, axis=0) for t in p_states]
    s_out = [jnp.stack(t, axis=0) for t in s_states]
    return (y_prompt, y_sample, *p_out, *s_out)
```

```python
import functools

import numpy as np
import jax
import jax.numpy as jnp
from jax import lax
from jax.experimental import pallas as pl
from jax.experimental.pallas import tpu as pltpu

F32 = jnp.float32
BF16 = jnp.bfloat16
I32 = jnp.int32

D_MODEL = 1024
CHUNK = 64
Q_BLOCK = 128
EPS = 1e-6

RW_HEADS = 8
RW_HEAD_DIM = 64
RW_WIDTH = RW_HEADS * RW_HEAD_DIM
RW_DECAY_LORA = 64
RW_ICLR_LORA = 64
RW_GATE_LORA = 128
RW_LORA = RW_DECAY_LORA + RW_ICLR_LORA + RW_GATE_LORA
RW_COLS = 3 * RW_WIDTH + RW_LORA
RW_GN_EPS = 64e-5

SA_HEADS = 8
SA_KV_HEADS = 2
SA_HEAD_DIM = 64
SA_WIDTH = SA_HEADS * SA_HEAD_DIM
SA_KV_WIDTH = SA_KV_HEADS * SA_HEAD_DIM
IDX_HEADS = 4
IDX_DIM = 64
IDX_WIDTH = IDX_HEADS * IDX_DIM
TOPK_MAX = 256
ROPE_THETA = 500000.0
ROPE_DIM = SA_HEAD_DIM // 4
INDEX_SCALE = (IDX_DIM ** -0.5) * (IDX_HEADS ** -0.5)
SA_COLS = SA_WIDTH + 2 * SA_KV_WIDTH + IDX_WIDTH + IDX_DIM + IDX_HEADS

RET_HEADS = 4
RET_HEAD_DIM = 128
RET_WIDTH = RET_HEADS * RET_HEAD_DIM
RET_ROPE_BASE = 10000.0
RET_COLS = 4 * RET_WIDTH

N_BRANCH = 3
GATE_COLS = N_BRANCH * D_MODEL
IN_COLS = RW_COLS + SA_COLS + RET_COLS + GATE_COLS
D_FF = 4 * D_MODEL

LANES = 128
TILE_ELEMS = 64 * 8 * LANES

P_GATE = 0
P_RET = P_GATE + GATE_COLS
P_RKV = P_RET + RET_COLS
P_SAQ = P_RKV + 3 * RW_WIDTH
P_LORA = P_SAQ + SA_WIDTH
P_QI = P_LORA + RW_LORA
P_SAK = P_QI + IDX_WIDTH
P_SAV = P_SAK + SA_KV_WIDTH
P_KIWI = P_SAV + SA_KV_WIDTH
P_COLS = 8192
INT_MIN = -2 ** 31
NEG_BIG = -1e30
LOG2_E = 1.4426950408889634
VMEM_LIMIT = 56 * 1024 * 1024


def _bf(x):
    return x.astype(BF16)


def _dot(a, b):
    return jnp.dot(a, b, preferred_element_type=F32)


def _dot_nt(a, b):
    return lax.dot_general(a, b, (((1,), (1,)), ((), ())), preferred_element_type=F32)


def _dot_tn(a, b):
    return lax.dot_general(a, b, (((0,), (0,)), ((), ())), preferred_element_type=F32)


def _dot_split2(a_exact, x):
    hi = _bf(x)
    lo = _bf(x - hi.astype(F32))
    return _dot(a_exact, hi) + _dot(a_exact, lo)


def _params(sem):
    return pltpu.CompilerParams(dimension_semantics=sem, vmem_limit_bytes=VMEM_LIMIT)


def _in_proj_kernel(x_ref, g_ref, w_ref, o_ref, h_ref):
    @pl.when(pl.program_id(1) == 0)
    def _():
        x = x_ref[...]
        ms = jnp.mean(x * x, axis=-1, keepdims=True)
        h_ref[...] = _bf(x * lax.rsqrt(ms + EPS) * g_ref[...])

    o_ref[...] = _dot(h_ref[...], w_ref[0])


def _in_proj(x2d, g, w_all, layer):
    T = x2d.shape[0]
    tm = min(1024, T)
    tn = 1024
    return pl.pallas_call(
        _in_proj_kernel,
        grid=(T // tm, P_COLS // tn),
        in_specs=[pl.BlockSpec((tm, D_MODEL), lambda i, j: (i, 0)),
                  pl.BlockSpec((1, D_MODEL), lambda i, j: (0, 0)),
                  pl.BlockSpec((1, D_MODEL, tn), lambda i, j: (layer, 0, j))],
        out_specs=pl.BlockSpec((tm, tn), lambda i, j: (i, j)),
        out_shape=jax.ShapeDtypeStruct((T, P_COLS), F32),
        scratch_shapes=[pltpu.VMEM((tm, D_MODEL), BF16)],
        compiler_params=_params(("parallel", "arbitrary")),
        name="in_proj",
    )(x2d, g, w_all)


KIWI_COLS = IDX_DIM + IDX_HEADS
TAIL0 = RW_COLS + SA_COLS - KIWI_COLS
TAIL_W = (IN_COLS - TAIL0 + LANES - 1) // LANES * LANES


def _w_prep_kernel(w_ref, o_ref):
    x = w_ref[0]
    x = jnp.where(lax.broadcasted_iota(I32, x.shape, 1) < IN_COLS, x, 0.0)
    lane = lax.broadcasted_iota(I32, (x.shape[0], LANES), 1)

    def put(off, v):
        o_ref[0, :, off:off + v.shape[1]] = _bf(v)

    nslab = TAIL_W // LANES
    rolled = [pltpu.roll(x[:, TAIL0 + s * LANES:TAIL0 + (s + 1) * LANES], LANES - KIWI_COLS, 1)
              for s in range(nslab)]
    for s in range((RET_COLS + GATE_COLS) // LANES):
        v = jnp.where(lane < LANES - KIWI_COLS, rolled[s], rolled[s + 1])
        put((P_RET if s < RET_COLS // LANES else P_GATE - RET_COLS) + s * LANES, v)
    sa = RW_COLS
    put(P_RKV, x[:, 0:3 * RW_WIDTH])
    put(P_LORA, x[:, 3 * RW_WIDTH:RW_COLS])
    put(P_SAQ, x[:, sa:sa + SA_WIDTH])
    put(P_SAK, x[:, sa + SA_WIDTH:sa + SA_WIDTH + SA_KV_WIDTH])
    put(P_SAV, x[:, sa + SA_WIDTH + SA_KV_WIDTH:sa + SA_WIDTH + 2 * SA_KV_WIDTH])
    put(P_QI, x[:, sa + SA_WIDTH + 2 * SA_KV_WIDTH:TAIL0])
    put(P_KIWI, jnp.where(lane < KIWI_COLS, x[:, TAIL0:TAIL0 + LANES], 0.0))
    put(P_KIWI + LANES, jnp.zeros((x.shape[0], P_COLS - P_KIWI - LANES), F32))


def _w_prep(w_in):
    depth, d, _ = w_in.shape
    tm = 256
    return pl.pallas_call(
        _w_prep_kernel,
        grid=(depth, d // tm),
        in_specs=[pl.BlockSpec((1, tm, TAIL0 + TAIL_W), lambda l, i: (l, i, 0))],
        out_specs=pl.BlockSpec((1, tm, P_COLS), lambda l, i: (l, i, 0)),
        out_shape=jax.ShapeDtypeStruct((depth, d, P_COLS), BF16),
        compiler_params=_params(("parallel", "parallel")),
        name="w_prep",
    )(w_in)


def _rot_slab(x, c, s_lo, s_hi, shift):
    return x * c + pltpu.roll(x, LANES - shift, 1) * s_lo + pltpu.roll(x, shift, 1) * s_hi


def _sa_prep_kernel(q_ref, qi_ref, k_ref, kw_ref, c_ref, s1_ref, s2_ref, ck_ref, s1k_ref, s2k_ref,
                    qo_ref, qio_ref, ko_ref, kwo_ref):
    c, s1, s2 = c_ref[...], s1_ref[...], s2_ref[...]
    half = ROPE_DIM // 2
    scale = SA_HEAD_DIM ** -0.5 * LOG2_E
    for s in range(SA_WIDTH // LANES):
        sl = slice(s * LANES, (s + 1) * LANES)
        qo_ref[:, sl] = _rot_slab(q_ref[:, sl], c, s1, s2, half) * scale
    for s in range(IDX_WIDTH // LANES):
        sl = slice(s * LANES, (s + 1) * LANES)
        qio_ref[:, sl] = _rot_slab(qi_ref[:, sl], c, s1, s2, half)
    ko_ref[...] = _rot_slab(k_ref[...], c, s1, s2, half)
    kwo_ref[...] = _rot_slab(kw_ref[...], ck_ref[...], s1k_ref[...], s2k_ref[...], half)


def _sa_prep(P, L, tabs):
    T = P.shape[0]
    tm = min(512, L)
    nl = L // tm
    tab_spec = pl.BlockSpec((tm, LANES), lambda i: (i % nl, 0))
    return pl.pallas_call(
        _sa_prep_kernel,
        grid=(T // tm,),
        in_specs=[pl.BlockSpec((tm, SA_WIDTH), lambda i: (i, P_SAQ // SA_WIDTH)),
                  pl.BlockSpec((tm, IDX_WIDTH), lambda i: (i, P_QI // IDX_WIDTH)),
                  pl.BlockSpec((tm, LANES), lambda i: (i, P_SAK // LANES)),
                  pl.BlockSpec((tm, LANES), lambda i: (i, P_KIWI // LANES))] + [tab_spec] * 6,
        out_specs=[pl.BlockSpec((tm, SA_WIDTH), lambda i: (i, 0)),
                   pl.BlockSpec((tm, IDX_WIDTH), lambda i: (i, 0)),
                   pl.BlockSpec((tm, LANES), lambda i: (i, 0)),
                   pl.BlockSpec((tm, LANES), lambda i: (i, 0))],
        out_shape=[jax.ShapeDtypeStruct((T, SA_WIDTH), F32),
                   jax.ShapeDtypeStruct((T, IDX_WIDTH), F32),
                   jax.ShapeDtypeStruct((T, LANES), F32),
                   jax.ShapeDtypeStruct((T, LANES), F32)],
        compiler_params=_params(("parallel",)),
        name="sa_prep",
    )(P, P, P, P, *tabs)


def _sa_tables(pos):
    half = ROPE_DIM // 2
    freqs = 1.0 / (ROPE_THETA ** (jnp.arange(0, ROPE_DIM, 2, dtype=F32) / ROPE_DIM))
    ang = pos.astype(F32)[:, None] * freqs[None, :]
    cos, sin = jnp.cos(ang), jnp.sin(ang)
    n = pos.shape[0]
    pad = SA_HEAD_DIM - ROPE_DIM
    c_head = jnp.concatenate([cos, cos, jnp.ones((n, pad), F32)], axis=1)
    s1_head = jnp.concatenate([-sin, jnp.zeros((n, half + pad), F32)], axis=1)
    s2_head = jnp.concatenate([jnp.zeros((n, half), F32), sin, jnp.zeros((n, pad), F32)], axis=1)
    one, zero = jnp.ones((n, SA_HEAD_DIM), F32), jnp.zeros((n, SA_HEAD_DIM), F32)
    two = lambda t: jnp.concatenate([t, t], axis=1)
    return (two(c_head), two(s1_head), two(s2_head),
            jnp.concatenate([c_head, one], axis=1), jnp.concatenate([s1_head, zero], axis=1),
            jnp.concatenate([s2_head, zero], axis=1))


def _softplus(u):
    return jnp.maximum(u, 0.0) + jnp.log(1.0 + jnp.exp(-jnp.abs(u)))


def _rwkv_kernel(r_ref, k_ref, v_ref, lo_ref, shr_ref, shk_ref, shv_ref, shlo_ref, s0_ref,
                 mur_ref, muk_ref, muv_ref, mulo_ref, w0_ref, w2_ref, a0_ref, a2_ref, g2_ref,
                 kk_ref, ka_ref, rk_ref, lnx_ref, e_ref, tin_ref, tsuf_ref,
                 y_ref, so_ref,
                 cr_ref, ck_ref, cv_ref, clo_ref, s_ref,
                 kap_ref, rt_ref, bh_ref, kh_ref, bt_ref, kt_ref, vv_ref, gc_ref, yy_ref, rr_ref,
                 ac_ref, cc_ref):
    c = pl.program_id(1)
    TT = r_ref.shape[0]
    nch = TT // CHUNK
    N = RW_HEAD_DIM

    @pl.when(c == 0)
    def _():
        cr_ref[0:1, :] = shr_ref[0]
        ck_ref[0:1, :] = shk_ref[0]
        cv_ref[0:1, :] = shv_ref[0]
        clo_ref[0:1, :] = shlo_ref[0]
        s_ref[...] = s0_ref[0]

    def lerp(p_ref, carry_ref, mu_ref):
        p = p_ref[...]
        rolled = pltpu.roll(p, 1, 0)
        row0 = lax.broadcasted_iota(I32, p.shape, 0) == 0
        prev = jnp.where(row0, carry_ref[0:1, :], rolled)
        carry_ref[0:1, :] = p[TT - 1:TT, :]
        return p + (prev - p) * mu_ref[...]

    xr = lerp(r_ref, cr_ref, mur_ref)
    xk = lerp(k_ref, ck_ref, muk_ref)
    xv = lerp(v_ref, cv_ref, muv_ref)
    xlo = lerp(lo_ref, clo_ref, mulo_ref)
    xw = xlo[:, :RW_DECAY_LORA]
    xa = xlo[:, RW_DECAY_LORA:RW_DECAY_LORA + RW_ICLR_LORA]
    xg = xlo[:, RW_DECAY_LORA + RW_ICLR_LORA:]

    z = w0_ref[...] + _dot(_bf(jnp.tanh(xw)), w2_ref[...])
    w = -_softplus(-z) - 0.5
    ld = -jnp.exp(w)
    a = jax.nn.sigmoid(a0_ref[...] + _dot(_bf(xa), a2_ref[...]))
    gate = _dot(_bf(jax.nn.sigmoid(xg)), g2_ref[...])
    e_blk = e_ref[...]
    kk = xk * kk_ref[...]
    kk = kk / jnp.maximum(jnp.sqrt(_dot(_bf(kk * kk), e_blk)), 1e-12)
    k2 = xk * (1.0 + (a - 1.0) * ka_ref[...])
    bb = kk * a
    bonus = _dot(_bf(xr * k2 * rk_ref[...]), e_blk) * xv

    lin = _dot_split2(tin_ref[...], ld)
    lsuf = _dot_split2(tsuf_ref[...], ld)
    e_in = jnp.exp(lin)
    e_ninv = jnp.exp(-lin)
    e_suf = jnp.exp(lsuf)
    kap_ref[...] = kk * jnp.exp(lin - ld)
    rt_ref[...] = xr * e_in
    bh_ref[...] = bb * e_ninv
    kh_ref[...] = k2 * e_ninv
    bt_ref[...] = bb * e_suf
    kt_ref[...] = k2 * e_suf
    vv_ref[...] = xv
    gc_ref[...] = jnp.exp(lin + lsuf)

    ri = lax.broadcasted_iota(I32, (CHUNK, CHUNK), 0)
    ci = lax.broadcasted_iota(I32, (CHUNK, CHUNK), 1)
    strict = ri > ci
    incl = ri >= ci
    eye = (ri == ci).astype(F32)
    heads = range(RW_HEADS)
    hsl = [slice(h * N, (h + 1) * N) for h in heads]

    per_it = 2 if nch % 2 == 0 else 1

    def coef_body(it, carry):
        chs = [it * per_it + t for t in range(per_it)]
        pairs = [(t, h) for t in range(per_it) for h in heads]
        rows = [pl.ds(pl.multiple_of(ch * CHUNK, CHUNK), CHUNK) for ch in chs]
        ld = lambda ref, p: ref[rows[p[0]], hsl[p[1]]]
        kap = [ld(kap_ref, p) for p in pairs]
        rt = [ld(rt_ref, p) for p in pairs]
        vh = [ld(vv_ref, p) for p in pairs]
        idx = range(len(pairs))
        gmat = [_dot_nt(_bf(jnp.concatenate([kap[n], rt[n]], axis=0)),
                        _bf(jnp.concatenate([ld(bh_ref, pairs[n]), ld(kh_ref, pairs[n])], axis=0)))
                for n in idx]
        n_ab = [jnp.where(strict, g[:CHUNK, :CHUNK], 0.0) for g in gmat]
        m_rb = [jnp.where(incl, g[CHUNK:, :CHUNK], 0.0) for g in gmat]
        m_v = [_bf(jnp.concatenate([jnp.where(strict, g[:CHUNK, CHUNK:], 0.0),
                                    jnp.where(incl, g[CHUNK:, CHUNK:], 0.0)], axis=0)) for g in gmat]
        mv = [_dot(m_v[n], _bf(vh[n])) for n in idx]
        x_inv = [eye - t for t in n_ab]
        pw = n_ab
        for _ in range(5):
            pwb = [_bf(p) for p in pw]
            pw = [_dot(p, p) for p in pwb]
            x_inv = [x + _dot(_bf(x), _bf(p)) for x, p in zip(x_inv, pw)]
        w = [_dot(_bf(x_inv[n]), _bf(jnp.concatenate([kap[n], mv[n][:CHUNK]], axis=1))) for n in idx]
        wb = [_bf(t) for t in w]
        ry = [jnp.concatenate([rt[n], mv[n][CHUNK:]], axis=1) - _dot(_bf(m_rb[n]), wb[n]) for n in idx]
        dmat = [_dot_tn(wb[n], _bf(ld(bt_ref, pairs[n]))) for n in idx]
        vtk = [_dot_tn(_bf(vh[n]), _bf(ld(kt_ref, pairs[n]))) for n in idx]
        for n, (t, h) in enumerate(pairs):
            rr_ref[rows[t], hsl[h]] = ry[n][:, :N]
            yy_ref[rows[t], hsl[h]] = ry[n][:, N:]
            ac_ref[chs[t], h] = -dmat[n][:N]
            cc_ref[chs[t], h] = vtk[n] - dmat[n][N:]
        return carry

    lax.fori_loop(0, nch // per_it, coef_body, 0)

    def state_body(ch, carry):
        rows = pl.ds(pl.multiple_of(ch * CHUNK, CHUNK), CHUNK)
        s_old = [s_ref[h] for h in heads]
        sb = [_bf(t) for t in s_old]
        s_new = [_dot(sb[h], _bf(ac_ref[ch, h])) for h in heads]
        y_c = [_dot_nt(_bf(rr_ref[rows, hsl[h]]), sb[h]) for h in heads]
        for h in heads:
            gch = gc_ref[pl.ds(pl.multiple_of(ch * CHUNK, CHUNK), 1), hsl[h]]
            s_ref[h] = s_old[h] * gch + s_new[h] + cc_ref[ch, h]
            yy_ref[rows, hsl[h]] = yy_ref[rows, hsl[h]] + y_c[h]
        return carry

    lax.fori_loop(0, nch, state_body, 0)

    y = yy_ref[...]
    mean = _dot(_bf(y), e_blk) * (1.0 / N)
    d = y - mean
    var = _dot(_bf(d * d), e_blk) * (1.0 / N)
    yn = d * lax.rsqrt(var + RW_GN_EPS) * lnx_ref[...]
    y_ref[...] = (yn + bonus) * gate

    @pl.when(c == pl.num_programs(1) - 1)
    def _():
        so_ref[0] = s_ref[...]


def _rwkv(P, B, L, shift_prev, s0, lp):
    T = P.shape[0]
    TT = min(256, L)
    nt = L // TT
    W = RW_WIDTH
    row = lambda t: t.reshape(1, -1).astype(F32)
    mu = lp['rwkv_mu']
    sh = shift_prev.astype(F32)
    pieces = lambda t: (t[..., 0:W], t[..., W:2 * W], t[..., 2 * W:3 * W], t[..., 3 * W:])
    mu_r, mu_k, mu_v, mu_lo = [row(t) for t in pieces(mu)]
    sh_r, sh_k, sh_v, sh_lo = [t.reshape(B, 1, -1) for t in pieces(sh)]
    hid = jnp.arange(W) // RW_HEAD_DIM
    e_blk = (hid[:, None] == hid[None, :]).astype(BF16)
    ti = jnp.arange(TT)
    same = (ti[:, None] // CHUNK) == (ti[None, :] // CHUNK)
    tri_in = (same & (ti[None, :] <= ti[:, None])).astype(BF16)
    tri_suf = (same & (ti[None, :] > ti[:, None])).astype(BF16)

    tok = lambda w, blk: pl.BlockSpec((TT, w), lambda b, c: (b * nt + c, blk))
    full = lambda shape: pl.BlockSpec(shape, lambda b, c: (0,) * len(shape))
    shs = lambda w: pl.BlockSpec((1, 1, w), lambda b, c: (b, 0, 0))
    st_spec = pl.BlockSpec((1, RW_HEADS, RW_HEAD_DIM, RW_HEAD_DIM), lambda b, c: (b, 0, 0, 0))
    big = lambda: pltpu.VMEM((TT, W), F32)
    y, s_out = pl.pallas_call(
        _rwkv_kernel,
        grid=(B, nt),
        in_specs=[tok(W, P_RKV // W), tok(W, P_RKV // W + 1), tok(W, P_RKV // W + 2),
                  tok(RW_LORA, P_LORA // RW_LORA),
                  shs(W), shs(W), shs(W), shs(RW_LORA), st_spec,
                  full((1, W)), full((1, W)), full((1, W)), full((1, RW_LORA)),
                  full((1, W)), full((RW_DECAY_LORA, W)), full((1, W)), full((RW_ICLR_LORA, W)),
                  full((RW_GATE_LORA, W)), full((1, W)), full((1, W)), full((1, W)), full((1, W)),
                  full((W, W)), full((TT, TT)), full((TT, TT))],
        out_specs=[pl.BlockSpec((TT, W), lambda b, c: (b * nt + c, 0)), st_spec],
        out_shape=[jax.ShapeDtypeStruct((T, W), F32),
                   jax.ShapeDtypeStruct((B, RW_HEADS, RW_HEAD_DIM, RW_HEAD_DIM), F32)],
        scratch_shapes=[pltpu.VMEM((8, W), F32), pltpu.VMEM((8, W), F32), pltpu.VMEM((8, W), F32),
                        pltpu.VMEM((8, RW_LORA), F32),
                        pltpu.VMEM((RW_HEADS, RW_HEAD_DIM, RW_HEAD_DIM), F32)] + [big() for _ in range(10)]
        + [pltpu.VMEM((TT // CHUNK, RW_HEADS, RW_HEAD_DIM, RW_HEAD_DIM), F32) for _ in range(2)],
        compiler_params=_params(("parallel", "arbitrary")),
        name="rwkv",
    )(P, P, P, P, sh_r, sh_k, sh_v, sh_lo, s0.astype(F32),
      mu_r, mu_k, mu_v, mu_lo, row(lp['rwkv_w0']), _bf(lp['rwkv_w2']), row(lp['rwkv_a0']),
      _bf(lp['rwkv_a2']), _bf(lp['rwkv_g2']), row(lp['rwkv_k_k']), row(lp['rwkv_k_a']),
      row(lp['rwkv_r_k']), row(lp['rwkv_lnx_g']), e_blk, tri_in, tri_suf)
    return y, s_out


ONES_ROWS = 16
BOUND_SLACK = 1.05
SAFE_BOUND = 60.0


def _dsa_kernel(q_ref, qi_ref, kw_ref, k_ref, v_ref, ki_ref, y_ref,
                kb_ref, vt_ref, kib_ref, kmax_ref, keys_ref, bias_ref, logit_ref, pstar_ref,
                *, past, lk_real, topk, kc):
    i = pl.program_id(1)
    qb = q_ref.shape[1]
    lk = k_ref.shape[1]
    kf = float(topk)
    HD = SA_HEAD_DIM

    ones_sq = jnp.ones((LANES, LANES), BF16)

    def head_slab(x, odd):
        lane = lax.broadcasted_iota(I32, x.shape, 1)
        return jnp.where(lane < HD, pltpu.roll(x, HD, 1) if odd else x, 0.0)

    @pl.when(i == 0)
    def _():
        kib_ref[...] = _bf(ki_ref[0])
        k = k_ref[0]
        lane = lax.broadcasted_iota(I32, k.shape, 1)
        for c in range(SA_KV_HEADS):
            kc_b = _bf(head_slab(k, c == 1))
            kb_ref[c] = jnp.where(lane == HD, jnp.ones_like(kc_b), kc_b)
            kf32 = kc_b.astype(F32)
            n2 = _dot(_bf(kf32 * kf32), ones_sq)
            kmax_ref[c] = jnp.broadcast_to(jnp.max(n2, axis=0, keepdims=True), (8, LANES))
        vt = v_ref[0].T
        for c in range(SA_KV_HEADS):
            vt_ref[c, 0:HD, :] = _bf(vt[c * HD:(c + 1) * HD, :])
            vt_ref[c, HD:HD + ONES_ROWS, :] = jnp.ones((ONES_ROWS, lk), BF16)

    qpos = past + i * qb + lax.broadcasted_iota(I32, (1, qb), 1)
    limit = jnp.minimum((qpos // CHUNK + 1) * CHUNK, lk_real)
    hi = jnp.minimum(past + (i + 1) * qb, lk_real)
    nk = lax.shift_right_logical(hi + (2 * kc - 1), kc.bit_length())
    sub_idx = lax.broadcasted_iota(I32, (kc, qb), 0)
    kwt = kw_ref[0].T
    wi_h = [kwt[IDX_DIM + h:IDX_DIM + h + 1, :] * INDEX_SCALE for h in range(IDX_HEADS)]
    qi = qi_ref[0]
    qi_h = [_bf(qi[:, h * IDX_DIM:(h + 1) * IDX_DIM]) for h in range(IDX_HEADS)]

    def chunk(j):
        return pl.ds(pl.multiple_of(j * kc, kc), kc)

    def pair(t):
        return (2 * t, 2 * t + 1)

    def score_body(t, carry):
        js = pair(t)
        dots = [[_dot_nt(kib_ref[chunk(j), 0:IDX_DIM], qi_h[h]) for h in range(IDX_HEADS)] for j in js]
        for j, d in zip(js, dots):
            s = jnp.zeros((kc, qb), F32)
            for h in range(IDX_HEADS):
                s = s + jnp.maximum(d[h], 0.0) * wi_h[h]
            s = jnp.where(s == 0.0, 0.0, s)
            bits = lax.bitcast_convert_type(s, I32)
            key = bits ^ ((bits >> 31) & 0x7FFFFFFF)
            keys_ref[chunk(j), :] = jnp.where((sub_idx + j * kc) < limit, key, INT_MIN)
        return carry

    lax.fori_loop(0, nk, score_body, 0)

    def fold8(m):
        parts = [m[r * 8:(r + 1) * 8] for r in range(kc // 8)]
        while len(parts) > 1:
            parts = [a + b for a, b in zip(parts[0::2], parts[1::2])]
        return parts[0]

    def count(pred):
        def body(t, acc):
            for j in pair(t):
                acc = acc + fold8(jnp.where(pred(keys_ref[chunk(j), :], j), 1.0, 0.0))
            return acc

        return jnp.sum(lax.fori_loop(0, nk, body, jnp.zeros((8, qb), F32)), axis=0, keepdims=True)

    def bit_body(it, prefix):
        cand = prefix | lax.shift_left(jnp.int32(1), 31 - it)
        thr = cand ^ INT_MIN
        cnt = count(lambda kj, j: kj >= thr)
        return jnp.where(cnt >= kf, cand, prefix)

    prefix = lax.fori_loop(0, 32, bit_body, jnp.zeros((1, qb), I32))
    tau = prefix ^ INT_MIN
    cnt_ge = count(lambda kj, j: kj >= tau)
    cnt_gt = count(lambda kj, j: kj > tau)
    need = kf - cnt_gt
    excess = jnp.logical_and(cnt_ge > kf, tau != INT_MIN)
    p_default = jnp.where(tau == INT_MIN, -1, 2 ** 30).astype(I32)
    pstar_ref[...] = jnp.broadcast_to(p_default, pstar_ref.shape)

    @pl.when(jnp.max(jnp.where(excess, 1.0, 0.0)) > 0.0)
    def _():
        nbits = max(1, int(lk - 1).bit_length())

        def idx_body(it, p):
            cand = p | lax.shift_left(jnp.int32(1), nbits - 1 - it)
            g = count(lambda kj, j: jnp.logical_and(kj == tau, (sub_idx + j * kc) < cand))
            return jnp.where(g < need, cand, p)

        p = lax.fori_loop(0, nbits, idx_body, jnp.zeros((1, qb), I32))
        pstar_ref[...] = jnp.broadcast_to(jnp.where(excess, p, p_default), pstar_ref.shape)

    pstar = pstar_ref[0:1, :]

    def bias_body(t, carry):
        for j in pair(t):
            kj = keys_ref[chunk(j), :]
            sel = jnp.logical_or(kj > tau, jnp.logical_and(kj == tau, (sub_idx + j * kc) <= pstar))
            bias_ref[chunk(j), :] = jnp.where(sel, 0.0, NEG_BIG)
        return carry

    lax.fori_loop(0, nk, bias_body, 0)

    q = q_ref[0]
    group = SA_HEADS // SA_KV_HEADS
    lane_q = lax.broadcasted_iota(I32, (qb, LANES), 1)
    q_heads, bounds = [], []
    for h in range(SA_HEADS):
        qh = _bf(head_slab(q[:, (h // 2) * LANES:(h // 2 + 1) * LANES], h % 2 == 1))
        qf = qh.astype(F32)
        qn2 = _dot(_bf(qf * qf), ones_sq)
        q_heads.append(qh)
        bounds.append(jnp.sqrt(qn2 * kmax_ref[h // group, 0:1, :]) * BOUND_SLACK)
    worst = bounds[0]
    for b in bounds[1:]:
        worst = jnp.maximum(worst, b)
    safe = jnp.max(worst) <= SAFE_BOUND

    zero_acc = jnp.zeros((HD + ONES_ROWS, qb), F32)

    def vt_chunk(c, j):
        return vt_ref[c, :, pl.ds(pl.multiple_of(j * kc, kc), kc)]

    @pl.when(safe)
    def _():
        for c in range(SA_KV_HEADS):
            hs = range(c * group, (c + 1) * group)
            qa = [jnp.where(lane_q == HD, _bf(-bounds[h]), q_heads[h]) for h in hs]

            def body(t, accs):
                js = pair(t)
                logit = [[_dot_nt(kb_ref[c, chunk(j), :], qa[g]) for g in range(group)] for j in js]
                p = [[_bf(jnp.exp2(lg + bias_ref[chunk(j), :])) for lg in row]
                     for j, row in zip(js, logit)]
                pv = [[_dot(vt_chunk(c, j), pg) for pg in row] for j, row in zip(js, p)]
                return tuple(accs[g] + pv[0][g] + pv[1][g] for g in range(group))

            accs = lax.fori_loop(0, nk, body, (zero_acc,) * group)
            for g, h in enumerate(hs):
                y_ref[0, h * HD:(h + 1) * HD, :] = accs[g][:HD] / accs[g][HD:HD + 1]

    @pl.when(jnp.logical_not(safe))
    def _():
        for h in range(SA_HEADS):
            c = h // group

            def max_body(t, m):
                for j in pair(t):
                    logit = _dot_nt(kb_ref[c, chunk(j), :], q_heads[h]) + bias_ref[chunk(j), :]
                    logit_ref[chunk(j), :] = logit
                    m = jnp.maximum(m, jnp.max(logit, axis=0, keepdims=True))
                return m

            m = lax.fori_loop(0, nk, max_body, jnp.full((1, qb), NEG_BIG, F32))

            def sum_body(t, acc):
                for j in pair(t):
                    acc = acc + _dot(vt_chunk(c, j), _bf(jnp.exp2(logit_ref[chunk(j), :] - m)))
                return acc

            acc = lax.fori_loop(0, nk, sum_body, zero_acc)
            y_ref[0, h * HD:(h + 1) * HD, :] = acc[:HD] / acc[HD:HD + 1]


def _dsa_call(q3, qi3, kw3, k_all, v_all, ki_all, v_blk, *, lk, past, lk_real, topk, qb, kc):
    B, n_q = q3.shape[:2]
    kern = functools.partial(_dsa_kernel, past=past, lk_real=lk_real, topk=topk, kc=kc)
    return pl.pallas_call(
        kern,
        grid=(B, n_q // qb),
        in_specs=[pl.BlockSpec((1, qb, SA_WIDTH), lambda b, i: (b, i, 0)),
                  pl.BlockSpec((1, qb, IDX_WIDTH), lambda b, i: (b, i, 0)),
                  pl.BlockSpec((1, qb, LANES), lambda b, i: (b, i, 0)),
                  pl.BlockSpec((1, lk, LANES), lambda b, i: (b, 0, 0)),
                  pl.BlockSpec((1, lk, LANES), lambda b, i: (b, 0, v_blk)),
                  pl.BlockSpec((1, lk, LANES), lambda b, i: (b, 0, 0))],
        out_specs=pl.BlockSpec((1, SA_WIDTH, qb), lambda b, i: (b, 0, i)),
        out_shape=jax.ShapeDtypeStruct((B, SA_WIDTH, n_q), F32),
        scratch_shapes=[pltpu.VMEM((SA_KV_HEADS, lk, LANES), BF16),
                        pltpu.VMEM((SA_KV_HEADS, SA_HEAD_DIM + ONES_ROWS, lk), BF16),
                        pltpu.VMEM((lk, LANES), BF16),
                        pltpu.VMEM((SA_KV_HEADS, 8, LANES), F32),
                        pltpu.VMEM((lk, qb), I32), pltpu.VMEM((lk, qb), F32), pltpu.VMEM((lk, qb), F32),
                        pltpu.VMEM((8, qb), I32)],
        compiler_params=_params(("parallel", "arbitrary")),
        name="dsa",
    )(q3, qi3, kw3, k_all, v_all, ki_all)


def _round_up(x, m):
    return (x + m - 1) // m * m


def _dsa(P, B, L, past, q_rot, qi_rot, k_rot, kw_rot, k_past, v_past, ik_past):
    lk_real = past + L
    topk = min(TOPK_MAX, lk_real // 4)
    qb = 2 * LANES if (past == 0 and L % (2 * LANES) == 0) else LANES
    lq = _round_up(L, qb)
    qpad = lambda t: t if lq == L else jnp.pad(t, ((0, 0), (0, lq - L), (0, 0)))
    q3 = qpad(q_rot.reshape(B, L, SA_WIDTH))
    qi3 = qpad(qi_rot.reshape(B, L, IDX_WIDTH))
    kw3 = kw_rot.reshape(B, L, LANES)
    k3 = k_rot.reshape(B, L, LANES)
    common = dict(past=past, lk_real=lk_real, topk=topk, qb=qb)
    kc = TILE_ELEMS // qb // 2 if qb == LANES else TILE_ELEMS // qb
    if past == 0:
        while L % (2 * kc):
            kc //= 2
        P3 = P.reshape(B, L, P_COLS)
        return _dsa_call(q3, qi3, kw3, k3, P3, kw3, P_SAV // LANES, lk=L, kc=kc, **common)
    lk = _round_up(lk_real, 2 * kc)
    zpad = jnp.zeros((B, lk - lk_real, LANES), F32)
    v_new = P.reshape(B, L, P_COLS)[:, :, P_SAV:P_SAV + LANES]
    ik_p = jnp.concatenate([ik_past.astype(F32), jnp.zeros((B, past, LANES - IDX_DIM), F32)], axis=2)
    k_all = jnp.concatenate([k_past.reshape(B, past, LANES).astype(F32), k3, zpad], axis=1)
    v_all = jnp.concatenate([v_past.reshape(B, past, LANES).astype(F32), v_new, zpad], axis=1)
    ki_all = jnp.concatenate([ik_p, kw3, zpad], axis=1)
    y = _dsa_call(q3, qi3, qpad(kw3), k_all, v_all, ki_all, 0, lk=lk, kc=kc, **common)
    return y[:, :, :L]


def _ret_kernel(q_ref, k_ref, v_ref, g_ref, cos_ref, sin_ref, dm_ref, qd_ref, kd_ref, gc_ref, s0_ref,
                y_ref, so_ref, s_ref):
    c = pl.program_id(1)
    D = RET_HEAD_DIM
    cs = dm_ref.shape[1]
    nch = q_ref.shape[0] // cs
    heads = range(RET_HEADS)
    hsl = [slice(h * D, (h + 1) * D) for h in heads]
    rows = [slice(ch * cs, (ch + 1) * cs) for ch in range(nch)]
    prob = [(ch, h) for ch in range(nch) for h in heads]

    @pl.when(c == 0)
    def _():
        s_ref[...] = s0_ref[0]

    cos, sin = cos_ref[...], sin_ref[...]
    qb_, kb_, kd_, vb_ = [], [], [], []
    for h in heads:
        q = q_ref[:, hsl[h]]
        k = k_ref[:, hsl[h]]
        k = (k * cos + pltpu.roll(k, D // 2, 1) * sin) * (D ** -0.5)
        qb_.append(_bf(q * cos + pltpu.roll(q, D // 2, 1) * sin))
        kb_.append(_bf(k))
        kd_.append([_bf(k[r] * kd_ref[:, hsl[h]]) for r in rows])
        vb_.append(_bf(v_ref[:, hsl[h]]))
    scores = {(ch, h): _dot_nt(qb_[h][rows[ch]], kb_[h][rows[ch]]) * dm_ref[h] for ch, h in prob}
    ktv = {(ch, h): _dot_tn(kd_[h][ch], vb_[h][rows[ch]]) for ch, h in prob}
    intra = {(ch, h): _dot(_bf(scores[ch, h]), vb_[h][rows[ch]]) for ch, h in prob}
    s_at = {}
    for h in heads:
        s = s_ref[h]
        for ch in range(nch):
            s_at[ch, h] = _bf(s)
            s = s * gc_ref[:, hsl[h]] + ktv[ch, h]
        s_ref[h] = s
    cross = {(ch, h): _dot(qb_[h][rows[ch]], s_at[ch, h]) for ch, h in prob}
    for ch in range(nch):
        outs = []
        for h in heads:
            o = intra[ch, h] + cross[ch, h] * qd_ref[:, hsl[h]]
            o = o * lax.rsqrt(jnp.mean(o * o, axis=-1, keepdims=True) + EPS)
            outs.append(jax.nn.silu(g_ref[rows[ch], hsl[h]]) * o)
        y_ref[rows[ch], :] = jnp.concatenate(outs, axis=1)

    @pl.when(c == pl.num_programs(1) - 1)
    def _():
        so_ref[0] = s_ref[...]


def _retention(P, B, L, pos, s0):
    T = P.shape[0]
    c = min(CHUNK, L)
    nc = L // c
    D = RET_HEAD_DIM
    freqs = 1.0 / (RET_ROPE_BASE ** jnp.linspace(0.0, 1.0, D // 2, dtype=F32))
    ang = pos.astype(F32)[:, None] * freqs[None, :]
    cos = jnp.concatenate([jnp.cos(ang)] * 2, axis=1)
    sin = jnp.concatenate([-jnp.sin(ang), jnp.sin(ang)], axis=1)
    log_gamma = jnp.log1p(-jnp.exp2(-5.0 - jnp.arange(RET_HEADS, dtype=F32)))
    idx = jnp.arange(c, dtype=F32)
    dmask = jnp.exp(jnp.abs(idx[:, None] - idx[None, :])[None] * log_gamma[:, None, None])
    lanes = lambda t: jnp.repeat(t, D, axis=1)
    qdec = lanes(jnp.exp((idx[:, None] + 1.0) * log_gamma[None, :]))
    kdec = lanes(jnp.exp((c - 1.0 - idx)[:, None] * log_gamma[None, :]))
    gchunk = lanes(jnp.exp(c * log_gamma)[None, :])

    W = RET_WIDTH
    tt = min(4 * c, L)
    nt = L // tt
    tok = lambda blk: pl.BlockSpec((tt, W), lambda b, i: (b * nt + i, blk))
    full = lambda shape: pl.BlockSpec(shape, lambda b, i: (0,) * len(shape))
    st_spec = pl.BlockSpec((1, RET_HEADS, D, D), lambda b, i: (b, 0, 0, 0))
    y, s_out = pl.pallas_call(
        _ret_kernel,
        grid=(B, nt),
        in_specs=[tok(P_RET // W), tok(P_RET // W + 1), tok(P_RET // W + 2), tok(P_RET // W + 3),
                  pl.BlockSpec((tt, D), lambda b, i: (i, 0)), pl.BlockSpec((tt, D), lambda b, i: (i, 0)),
                  full((RET_HEADS, c, c)), full((c, W)), full((c, W)), full((1, W)), st_spec],
        out_specs=[pl.BlockSpec((tt, W), lambda b, i: (b * nt + i, 0)), st_spec],
        out_shape=[jax.ShapeDtypeStruct((T, W), F32), jax.ShapeDtypeStruct((B, RET_HEADS, D, D), F32)],
        scratch_shapes=[pltpu.VMEM((RET_HEADS, D, D), F32)],
        compiler_params=_params(("parallel", "arbitrary")),
        name="retention",
    )(P, P, P, P, cos, sin, dmask, qdec, kdec, gchunk, s0.astype(F32))
    return y, s_out


def _merge_kernel(x_ref, g0_ref, g1_ref, g2_ref, yr_ref, ys_ref, yt_ref, wr_ref, ws_ref, wt_ref, wo_ref,
                  o_ref):
    m = (jax.nn.sigmoid(g0_ref[...]) * _dot(_bf(yr_ref[...]), wr_ref[...])
         + jax.nn.sigmoid(g1_ref[...]) * _dot_tn(_bf(ys_ref[0]), ws_ref[...])
         + jax.nn.sigmoid(g2_ref[...]) * _dot(_bf(yt_ref[...]), wt_ref[...]))
    o_ref[...] = x_ref[...] + _dot(_bf(m), wo_ref[...])


def _merge(x2d, P, y_rw, y_sa_t, y_ret, w_rw, w_sa, w_ret, w_o):
    T = x2d.shape[0]
    L = y_sa_t.shape[2]
    tm = min(512, L)
    nl = L // tm
    D = D_MODEL
    tok = lambda w, blk: pl.BlockSpec((tm, w), lambda i: (i, blk))
    full = lambda shape: pl.BlockSpec(shape, lambda i: (0, 0))
    return pl.pallas_call(
        _merge_kernel,
        grid=(T // tm,),
        in_specs=[tok(D, 0), tok(D, 0), tok(D, 1), tok(D, 2), tok(RW_WIDTH, 0),
                  pl.BlockSpec((1, SA_WIDTH, tm), lambda i: (i // nl, 0, i % nl)),
                  tok(RET_WIDTH, 0), full((RW_WIDTH, D)), full((SA_WIDTH, D)), full((RET_WIDTH, D)),
                  full((D, D))],
        out_specs=tok(D, 0),
        out_shape=jax.ShapeDtypeStruct((T, D), F32),
        compiler_params=_params(("parallel",)),
        name="merge",
    )(x2d, P, P, P, y_rw, y_sa_t, y_ret, w_rw, w_sa, w_ret, w_o)


def _mlp_kernel(x_ref, g_ref, wu_ref, wd_ref, gf_ref, o_ref, h_ref, acc_ref, *, final_norm):
    j = pl.program_id(1)

    @pl.when(j == 0)
    def _():
        x = x_ref[...]
        ms = jnp.mean(x * x, axis=-1, keepdims=True)
        h_ref[...] = _bf(x * lax.rsqrt(ms + EPS) * g_ref[...])
        acc_ref[...] = jnp.zeros_like(acc_ref)

    u = jnp.maximum(_dot(h_ref[...], wu_ref[...]), 0.0)
    acc_ref[...] += _dot(_bf(u * u), wd_ref[...])

    @pl.when(j == pl.num_programs(1) - 1)
    def _():
        xn = x_ref[...] + acc_ref[...]
        if final_norm:
            ms = jnp.mean(xn * xn, axis=-1, keepdims=True)
            xn = xn * lax.rsqrt(ms + EPS) * gf_ref[...]
        o_ref[...] = xn


def _mlp(x2d, g, w_up, w_down, g_final, final_norm):
    T = x2d.shape[0]
    tm = min(1024, T)
    tf = 1024
    D = D_MODEL
    return pl.pallas_call(
        functools.partial(_mlp_kernel, final_norm=final_norm),
        grid=(T // tm, D_FF // tf),
        in_specs=[pl.BlockSpec((tm, D), lambda i, j: (i, 0)),
                  pl.BlockSpec((1, D), lambda i, j: (0, 0)),
                  pl.BlockSpec((D, tf), lambda i, j: (0, j)),
                  pl.BlockSpec((tf, D), lambda i, j: (j, 0)),
                  pl.BlockSpec((1, D), lambda i, j: (0, 0))],
        out_specs=pl.BlockSpec((tm, D), lambda i, j: (i, 0)),
        out_shape=jax.ShapeDtypeStruct((T, D), F32),
        scratch_shapes=[pltpu.VMEM((tm, D), BF16), pltpu.VMEM((tm, D), F32)],
        compiler_params=_params(("parallel", "arbitrary")),
        name="mlp",
    )(x2d, g, w_up, w_down, g_final)


def _layer(x2d, B, L, past, caches, lp, wts, g_final, final_norm):
    k_past, v_past, ik_past, s_rw, shift_rw, s_ret = caches
    pos = past + jnp.arange(L, dtype=jnp.int32)
    row = lambda t: t.reshape(1, -1).astype(F32)
    P = _in_proj(x2d, row(lp['norm1_g']), wts['w_in'], wts['layer'])
    q_rot, qi_rot, k_rot, kw_rot = _sa_prep(P, L, _sa_tables(pos))
    y_rw, s_rw_new = _rwkv(P, B, L, shift_rw, s_rw, lp)
    y_sa = _dsa(P, B, L, past, q_rot, qi_rot, k_rot, kw_rot, k_past, v_past, ik_past)
    y_ret, s_ret_new = _retention(P, B, L, pos, s_ret)
    x2d = _merge(x2d, P, y_rw, y_sa, y_ret, wts['w_br_rwkv'], wts['w_br_dsa'], wts['w_br_ret'], wts['w_o'])
    x2d = _mlp(x2d, row(lp['norm2_g']), wts['w_up'], wts['w_down'], g_final, final_norm)
    P3 = P.reshape(B, L, P_COLS)
    last = P3[:, L - 1]
    shift_new = jnp.concatenate([last[:, P_RKV:P_RKV + 3 * RW_WIDTH], last[:, P_LORA:P_LORA + RW_LORA]], axis=1)
    k_new = k_rot.reshape(B, L, SA_KV_HEADS, SA_HEAD_DIM)
    v_new = P3[:, :, P_SAV:P_SAV + SA_KV_WIDTH].reshape(B, L, SA_KV_HEADS, SA_HEAD_DIM)
    ik_new = kw_rot.reshape(B, L, LANES)[:, :, :IDX_DIM]
    return x2d, (k_new, v_new, ik_new, s_rw_new, shift_new, s_ret_new)


def kernel(x_prompt, x_sample, cache_dsa_k, cache_dsa_v, cache_dsa_ik, state_rwkv, state_rwkv_shift, state_ret, norm1_g, w_in, rwkv_mu, rwkv_w0, rwkv_w2, rwkv_a0, rwkv_a2, rwkv_g2, rwkv_k_k, rwkv_k_a, rwkv_r_k, rwkv_lnx_g, w_br_rwkv, w_br_dsa, w_br_ret, w_o, norm2_g, w_up, w_down, final_norm_g):
    params = {
        'norm1_g': norm1_g, 'rwkv_mu': rwkv_mu, 'rwkv_w0': rwkv_w0, 'rwkv_w2': rwkv_w2,
        'rwkv_a0': rwkv_a0, 'rwkv_a2': rwkv_a2, 'rwkv_g2': rwkv_g2, 'rwkv_k_k': rwkv_k_k,
        'rwkv_k_a': rwkv_k_a, 'rwkv_r_k': rwkv_r_k, 'rwkv_lnx_g': rwkv_lnx_g, 'norm2_g': norm2_g,
    }
    depth = w_in.shape[0]
    Bp, Lp, D = x_prompt.shape
    Bs, Ls, _ = x_sample.shape
    past_s = cache_dsa_k.shape[2]
    xp = x_prompt.reshape(Bp * Lp, D).astype(F32)
    xs = x_sample.reshape(Bs * Ls, D).astype(F32)
    g_final = final_norm_g.reshape(1, D).astype(F32)
    zero_p = (None, None, None,
              jnp.zeros((Bp, RW_HEADS, RW_HEAD_DIM, RW_HEAD_DIM), F32), jnp.zeros((Bp, RW_COLS), F32),
              jnp.zeros((Bp, RET_HEADS, RET_HEAD_DIM, RET_HEAD_DIM), F32))
    p_states = [[] for _ in range(6)]
    s_states = [[] for _ in range(6)]
    w_all = _w_prep(w_in)
    for i in range(depth):
        lp = {name: arr[i] for name, arr in params.items()}
        wts = {'w_in': w_all, 'layer': i, 'w_br_rwkv': _bf(w_br_rwkv[i]), 'w_br_dsa': _bf(w_br_dsa[i]),
               'w_br_ret': _bf(w_br_ret[i]), 'w_o': _bf(w_o[i]), 'w_up': _bf(w_up[i]), 'w_down': _bf(w_down[i])}
        final = i == depth - 1
        cache_s = (cache_dsa_k[i], cache_dsa_v[i], cache_dsa_ik[i], state_rwkv[i], state_rwkv_shift[i],
                   state_ret[i])
        xp, new_p = _layer(xp, Bp, Lp, 0, zero_p, lp, wts, g_final, final)
        xs, new_s = _layer(xs, Bs, Ls, past_s, cache_s, lp, wts, g_final, final)
        for j in range(6):
            p_states[j].append(new_p[j])
            s_states[j].append(new_s[j])
    y_prompt = xp.reshape(Bp, Lp, D)
    y_sample = xs.reshape(Bs, Ls, D)
    p_out = [jnp.stack(t, axis=0) for t in p_states]
    s_out = [jnp.stack(t, axis=0) for t in s_states]
    return (y_prompt, y_sample, *p_out, *s_out)
```

```python
import functools

import numpy as np
import jax
import jax.numpy as jnp
from jax import lax
from jax.experimental import pallas as pl
from jax.experimental.pallas import tpu as pltpu

F32 = jnp.float32
BF16 = jnp.bfloat16
I32 = jnp.int32
I16 = jnp.int16

D_MODEL = 1024
CHUNK = 64
Q_BLOCK = 128
EPS = 1e-6

RW_HEADS = 8
RW_HEAD_DIM = 64
RW_WIDTH = RW_HEADS * RW_HEAD_DIM
RW_DECAY_LORA = 64
RW_ICLR_LORA = 64
RW_GATE_LORA = 128
RW_LORA = RW_DECAY_LORA + RW_ICLR_LORA + RW_GATE_LORA
RW_COLS = 3 * RW_WIDTH + RW_LORA
RW_GN_EPS = 64e-5

SA_HEADS = 8
SA_KV_HEADS = 2
SA_HEAD_DIM = 64
SA_WIDTH = SA_HEADS * SA_HEAD_DIM
SA_KV_WIDTH = SA_KV_HEADS * SA_HEAD_DIM
IDX_HEADS = 4
IDX_DIM = 64
IDX_WIDTH = IDX_HEADS * IDX_DIM
TOPK_MAX = 256
ROPE_THETA = 500000.0
ROPE_DIM = SA_HEAD_DIM // 4
INDEX_SCALE = (IDX_DIM ** -0.5) * (IDX_HEADS ** -0.5)
SA_COLS = SA_WIDTH + 2 * SA_KV_WIDTH + IDX_WIDTH + IDX_DIM + IDX_HEADS

RET_HEADS = 4
RET_HEAD_DIM = 128
RET_WIDTH = RET_HEADS * RET_HEAD_DIM
RET_ROPE_BASE = 10000.0
RET_COLS = 4 * RET_WIDTH

N_BRANCH = 3
GATE_COLS = N_BRANCH * D_MODEL
IN_COLS = RW_COLS + SA_COLS + RET_COLS + GATE_COLS
D_FF = 4 * D_MODEL

LANES = 128
TILE_ELEMS = 64 * 8 * LANES

P_GATE = 0
P_RET = P_GATE + GATE_COLS
P_RKV = P_RET + RET_COLS
P_SAQ = P_RKV + 3 * RW_WIDTH
P_LORA = P_SAQ + SA_WIDTH
P_QI = P_LORA + RW_LORA
P_SAK = P_QI + IDX_WIDTH
P_SAV = P_SAK + SA_KV_WIDTH
P_KIWI = P_SAV + SA_KV_WIDTH
P_COLS = 8192
INT_MIN = -2 ** 31
NEG_BIG = -1e30
LOG2_E = 1.4426950408889634
VMEM_LIMIT = 56 * 1024 * 1024


def _bf(x):
    return x.astype(BF16)


def _dot(a, b):
    return jnp.dot(a, b, preferred_element_type=F32)


def _dot_nt(a, b):
    return lax.dot_general(a, b, (((1,), (1,)), ((), ())), preferred_element_type=F32)


def _dot_tn(a, b):
    return lax.dot_general(a, b, (((0,), (0,)), ((), ())), preferred_element_type=F32)


def _dot_split2(a_exact, x):
    hi = _bf(x)
    lo = _bf(x - hi.astype(F32))
    return _dot(a_exact, hi) + _dot(a_exact, lo)


def _params(sem):
    return pltpu.CompilerParams(dimension_semantics=sem, vmem_limit_bytes=VMEM_LIMIT)


def _in_proj_kernel(x_ref, g_ref, w_ref, o_ref, h_ref):
    @pl.when(pl.program_id(1) == 0)
    def _():
        x = x_ref[...]
        ms = jnp.mean(x * x, axis=-1, keepdims=True)
        h_ref[...] = _bf(x * lax.rsqrt(ms + EPS) * g_ref[...])

    o_ref[...] = _dot(h_ref[...], w_ref[0])


def _in_proj(x2d, g, w_all, layer):
    T = x2d.shape[0]
    tm = min(1024, T)
    tn = 1024
    return pl.pallas_call(
        _in_proj_kernel,
        grid=(T // tm, P_COLS // tn),
        in_specs=[pl.BlockSpec((tm, D_MODEL), lambda i, j: (i, 0)),
                  pl.BlockSpec((1, D_MODEL), lambda i, j: (0, 0)),
                  pl.BlockSpec((1, D_MODEL, tn), lambda i, j: (layer, 0, j))],
        out_specs=pl.BlockSpec((tm, tn), lambda i, j: (i, j)),
        out_shape=jax.ShapeDtypeStruct((T, P_COLS), F32),
        scratch_shapes=[pltpu.VMEM((tm, D_MODEL), BF16)],
        compiler_params=_params(("parallel", "arbitrary")),
        name="in_proj",
    )(x2d, g, w_all)


KIWI_COLS = IDX_DIM + IDX_HEADS
TAIL0 = RW_COLS + SA_COLS - KIWI_COLS
TAIL_W = (IN_COLS - TAIL0 + LANES - 1) // LANES * LANES


def _w_prep_kernel(w_ref, o_ref):
    x = w_ref[0]
    x = jnp.where(lax.broadcasted_iota(I32, x.shape, 1) < IN_COLS, x, 0.0)
    lane = lax.broadcasted_iota(I32, (x.shape[0], LANES), 1)

    def put(off, v):
        o_ref[0, :, off:off + v.shape[1]] = _bf(v)

    nslab = TAIL_W // LANES
    rolled = [pltpu.roll(x[:, TAIL0 + s * LANES:TAIL0 + (s + 1) * LANES], LANES - KIWI_COLS, 1)
              for s in range(nslab)]
    for s in range((RET_COLS + GATE_COLS) // LANES):
        v = jnp.where(lane < LANES - KIWI_COLS, rolled[s], rolled[s + 1])
        put((P_RET if s < RET_COLS // LANES else P_GATE - RET_COLS) + s * LANES, v)
    sa = RW_COLS
    put(P_RKV, x[:, 0:3 * RW_WIDTH])
    put(P_LORA, x[:, 3 * RW_WIDTH:RW_COLS])
    put(P_SAQ, x[:, sa:sa + SA_WIDTH])
    put(P_SAK, x[:, sa + SA_WIDTH:sa + SA_WIDTH + SA_KV_WIDTH])
    put(P_SAV, x[:, sa + SA_WIDTH + SA_KV_WIDTH:sa + SA_WIDTH + 2 * SA_KV_WIDTH])
    put(P_QI, x[:, sa + SA_WIDTH + 2 * SA_KV_WIDTH:TAIL0])
    put(P_KIWI, jnp.where(lane < KIWI_COLS, x[:, TAIL0:TAIL0 + LANES], 0.0))
    put(P_KIWI + LANES, jnp.zeros((x.shape[0], P_COLS - P_KIWI - LANES), F32))


def _w_prep(w_in):
    depth, d, _ = w_in.shape
    tm = 256
    return pl.pallas_call(
        _w_prep_kernel,
        grid=(depth, d // tm),
        in_specs=[pl.BlockSpec((1, tm, TAIL0 + TAIL_W), lambda l, i: (l, i, 0))],
        out_specs=pl.BlockSpec((1, tm, P_COLS), lambda l, i: (l, i, 0)),
        out_shape=jax.ShapeDtypeStruct((depth, d, P_COLS), BF16),
        compiler_params=_params(("parallel", "parallel")),
        name="w_prep",
    )(w_in)


def _rot_slab(x, c, s_lo, s_hi, shift):
    return x * c + pltpu.roll(x, LANES - shift, 1) * s_lo + pltpu.roll(x, shift, 1) * s_hi


def _sa_prep_kernel(q_ref, qi_ref, k_ref, kw_ref, c_ref, s1_ref, s2_ref, ck_ref, s1k_ref, s2k_ref,
                    qo_ref, qio_ref, ko_ref, kwo_ref):
    c, s1, s2 = c_ref[...], s1_ref[...], s2_ref[...]
    half = ROPE_DIM // 2
    scale = SA_HEAD_DIM ** -0.5 * LOG2_E
    for s in range(SA_WIDTH // LANES):
        sl = slice(s * LANES, (s + 1) * LANES)
        qo_ref[:, sl] = _rot_slab(q_ref[:, sl], c, s1, s2, half) * scale
    for s in range(IDX_WIDTH // LANES):
        sl = slice(s * LANES, (s + 1) * LANES)
        qio_ref[:, sl] = _rot_slab(qi_ref[:, sl], c, s1, s2, half)
    ko_ref[...] = _rot_slab(k_ref[...], c, s1, s2, half)
    kwo_ref[...] = _rot_slab(kw_ref[...], ck_ref[...], s1k_ref[...], s2k_ref[...], half)


def _sa_prep(P, L, tabs):
    T = P.shape[0]
    tm = min(512, L)
    nl = L // tm
    tab_spec = pl.BlockSpec((tm, LANES), lambda i: (i % nl, 0))
    return pl.pallas_call(
        _sa_prep_kernel,
        grid=(T // tm,),
        in_specs=[pl.BlockSpec((tm, SA_WIDTH), lambda i: (i, P_SAQ // SA_WIDTH)),
                  pl.BlockSpec((tm, IDX_WIDTH), lambda i: (i, P_QI // IDX_WIDTH)),
                  pl.BlockSpec((tm, LANES), lambda i: (i, P_SAK // LANES)),
                  pl.BlockSpec((tm, LANES), lambda i: (i, P_KIWI // LANES))] + [tab_spec] * 6,
        out_specs=[pl.BlockSpec((tm, SA_WIDTH), lambda i: (i, 0)),
                   pl.BlockSpec((tm, IDX_WIDTH), lambda i: (i, 0)),
                   pl.BlockSpec((tm, LANES), lambda i: (i, 0)),
                   pl.BlockSpec((tm, LANES), lambda i: (i, 0))],
        out_shape=[jax.ShapeDtypeStruct((T, SA_WIDTH), F32),
                   jax.ShapeDtypeStruct((T, IDX_WIDTH), F32),
                   jax.ShapeDtypeStruct((T, LANES), F32),
                   jax.ShapeDtypeStruct((T, LANES), F32)],
        compiler_params=_params(("parallel",)),
        name="sa_prep",
    )(P, P, P, P, *tabs)


def _sa_tables(pos):
    half = ROPE_DIM // 2
    freqs = 1.0 / (ROPE_THETA ** (jnp.arange(0, ROPE_DIM, 2, dtype=F32) / ROPE_DIM))
    ang = pos.astype(F32)[:, None] * freqs[None, :]
    cos, sin = jnp.cos(ang), jnp.sin(ang)
    n = pos.shape[0]
    pad = SA_HEAD_DIM - ROPE_DIM
    c_head = jnp.concatenate([cos, cos, jnp.ones((n, pad), F32)], axis=1)
    s1_head = jnp.concatenate([-sin, jnp.zeros((n, half + pad), F32)], axis=1)
    s2_head = jnp.concatenate([jnp.zeros((n, half), F32), sin, jnp.zeros((n, pad), F32)], axis=1)
    one, zero = jnp.ones((n, SA_HEAD_DIM), F32), jnp.zeros((n, SA_HEAD_DIM), F32)
    two = lambda t: jnp.concatenate([t, t], axis=1)
    return (two(c_head), two(s1_head), two(s2_head),
            jnp.concatenate([c_head, one], axis=1), jnp.concatenate([s1_head, zero], axis=1),
            jnp.concatenate([s2_head, zero], axis=1))


def _softplus(u):
    return jnp.maximum(u, 0.0) + jnp.log(1.0 + jnp.exp(-jnp.abs(u)))


def _rwkv_kernel(r_ref, k_ref, v_ref, lo_ref, shr_ref, shk_ref, shv_ref, shlo_ref, s0_ref,
                 mur_ref, muk_ref, muv_ref, mulo_ref, w0_ref, w2_ref, a0_ref, a2_ref, g2_ref,
                 kk_ref, ka_ref, rk_ref, lnx_ref, e_ref, tin_ref, tsuf_ref,
                 y_ref, so_ref,
                 cr_ref, ck_ref, cv_ref, clo_ref, s_ref,
                 kap_ref, rt_ref, bh_ref, kh_ref, bt_ref, kt_ref, vv_ref, gc_ref, yy_ref, rr_ref,
                 ac_ref, cc_ref):
    c = pl.program_id(1)
    TT = r_ref.shape[0]
    nch = TT // CHUNK
    N = RW_HEAD_DIM

    @pl.when(c == 0)
    def _():
        cr_ref[0:1, :] = shr_ref[0]
        ck_ref[0:1, :] = shk_ref[0]
        cv_ref[0:1, :] = shv_ref[0]
        clo_ref[0:1, :] = shlo_ref[0]
        s_ref[...] = s0_ref[0]

    def lerp(p_ref, carry_ref, mu_ref):
        p = p_ref[...]
        rolled = pltpu.roll(p, 1, 0)
        row0 = lax.broadcasted_iota(I32, p.shape, 0) == 0
        prev = jnp.where(row0, carry_ref[0:1, :], rolled)
        carry_ref[0:1, :] = p[TT - 1:TT, :]
        return p + (prev - p) * mu_ref[...]

    xr = lerp(r_ref, cr_ref, mur_ref)
    xk = lerp(k_ref, ck_ref, muk_ref)
    xv = lerp(v_ref, cv_ref, muv_ref)
    xlo = lerp(lo_ref, clo_ref, mulo_ref)
    xw = xlo[:, :RW_DECAY_LORA]
    xa = xlo[:, RW_DECAY_LORA:RW_DECAY_LORA + RW_ICLR_LORA]
    xg = xlo[:, RW_DECAY_LORA + RW_ICLR_LORA:]

    z = w0_ref[...] + _dot(_bf(jnp.tanh(xw)), w2_ref[...])
    w = -_softplus(-z) - 0.5
    ld = -jnp.exp(w)
    a = jax.nn.sigmoid(a0_ref[...] + _dot(_bf(xa), a2_ref[...]))
    gate = _dot(_bf(jax.nn.sigmoid(xg)), g2_ref[...])
    e_blk = e_ref[...]
    kk = xk * kk_ref[...]
    kk = kk / jnp.maximum(jnp.sqrt(_dot(_bf(kk * kk), e_blk)), 1e-12)
    k2 = xk * (1.0 + (a - 1.0) * ka_ref[...])
    bb = kk * a
    bonus = _dot(_bf(xr * k2 * rk_ref[...]), e_blk) * xv

    lin = _dot_split2(tin_ref[...], ld)
    lsuf = _dot_split2(tsuf_ref[...], ld)
    e_in = jnp.exp(lin)
    e_ninv = jnp.exp(-lin)
    e_suf = jnp.exp(lsuf)
    kap_ref[...] = kk * jnp.exp(lin - ld)
    rt_ref[...] = xr * e_in
    bh_ref[...] = bb * e_ninv
    kh_ref[...] = k2 * e_ninv
    bt_ref[...] = bb * e_suf
    kt_ref[...] = k2 * e_suf
    vv_ref[...] = xv
    gc_ref[...] = jnp.exp(lin + lsuf)

    ri = lax.broadcasted_iota(I32, (CHUNK, CHUNK), 0)
    ci = lax.broadcasted_iota(I32, (CHUNK, CHUNK), 1)
    strict = ri > ci
    incl = ri >= ci
    eye = (ri == ci).astype(F32)
    heads = range(RW_HEADS)
    hsl = [slice(h * N, (h + 1) * N) for h in heads]

    per_it = 2 if nch % 2 == 0 else 1

    def coef_body(it, carry):
        chs = [it * per_it + t for t in range(per_it)]
        pairs = [(t, h) for t in range(per_it) for h in heads]
        rows = [pl.ds(pl.multiple_of(ch * CHUNK, CHUNK), CHUNK) for ch in chs]
        ld = lambda ref, p: ref[rows[p[0]], hsl[p[1]]]
        kap = [ld(kap_ref, p) for p in pairs]
        rt = [ld(rt_ref, p) for p in pairs]
        vh = [ld(vv_ref, p) for p in pairs]
        idx = range(len(pairs))
        gmat = [_dot_nt(_bf(jnp.concatenate([kap[n], rt[n]], axis=0)),
                        _bf(jnp.concatenate([ld(bh_ref, pairs[n]), ld(kh_ref, pairs[n])], axis=0)))
                for n in idx]
        n_ab = [jnp.where(strict, g[:CHUNK, :CHUNK], 0.0) for g in gmat]
        m_rb = [jnp.where(incl, g[CHUNK:, :CHUNK], 0.0) for g in gmat]
        m_v = [_bf(jnp.concatenate([jnp.where(strict, g[:CHUNK, CHUNK:], 0.0),
                                    jnp.where(incl, g[CHUNK:, CHUNK:], 0.0)], axis=0)) for g in gmat]
        mv = [_dot(m_v[n], _bf(vh[n])) for n in idx]
        x_inv = [eye - t for t in n_ab]
        pw = n_ab
        for _ in range(5):
            pwb = [_bf(p) for p in pw]
            pw = [_dot(p, p) for p in pwb]
            x_inv = [x + _dot(_bf(x), _bf(p)) for x, p in zip(x_inv, pw)]
        w = [_dot(_bf(x_inv[n]), _bf(jnp.concatenate([kap[n], mv[n][:CHUNK]], axis=1))) for n in idx]
        wb = [_bf(t) for t in w]
        ry = [jnp.concatenate([rt[n], mv[n][CHUNK:]], axis=1) - _dot(_bf(m_rb[n]), wb[n]) for n in idx]
        dmat = [_dot_tn(wb[n], _bf(ld(bt_ref, pairs[n]))) for n in idx]
        vtk = [_dot_tn(_bf(vh[n]), _bf(ld(kt_ref, pairs[n]))) for n in idx]
        for n, (t, h) in enumerate(pairs):
            rr_ref[rows[t], hsl[h]] = ry[n][:, :N]
            yy_ref[rows[t], hsl[h]] = ry[n][:, N:]
            ac_ref[chs[t], h] = -dmat[n][:N]
            cc_ref[chs[t], h] = vtk[n] - dmat[n][N:]
        return carry

    lax.fori_loop(0, nch // per_it, coef_body, 0)

    def state_body(ch, carry):
        rows = pl.ds(pl.multiple_of(ch * CHUNK, CHUNK), CHUNK)
        s_old = [s_ref[h] for h in heads]
        sb = [_bf(t) for t in s_old]
        s_new = [_dot(sb[h], _bf(ac_ref[ch, h])) for h in heads]
        y_c = [_dot_nt(_bf(rr_ref[rows, hsl[h]]), sb[h]) for h in heads]
        for h in heads:
            gch = gc_ref[pl.ds(pl.multiple_of(ch * CHUNK, CHUNK), 1), hsl[h]]
            s_ref[h] = s_old[h] * gch + s_new[h] + cc_ref[ch, h]
            yy_ref[rows, hsl[h]] = yy_ref[rows, hsl[h]] + y_c[h]
        return carry

    lax.fori_loop(0, nch, state_body, 0)

    y = yy_ref[...]
    mean = _dot(_bf(y), e_blk) * (1.0 / N)
    d = y - mean
    var = _dot(_bf(d * d), e_blk) * (1.0 / N)
    yn = d * lax.rsqrt(var + RW_GN_EPS) * lnx_ref[...]
    y_ref[...] = (yn + bonus) * gate

    @pl.when(c == pl.num_programs(1) - 1)
    def _():
        so_ref[0] = s_ref[...]


def _rwkv(P, B, L, shift_prev, s0, lp):
    T = P.shape[0]
    TT = min(256, L)
    nt = L // TT
    W = RW_WIDTH
    row = lambda t: t.reshape(1, -1).astype(F32)
    mu = lp['rwkv_mu']
    sh = shift_prev.astype(F32)
    pieces = lambda t: (t[..., 0:W], t[..., W:2 * W], t[..., 2 * W:3 * W], t[..., 3 * W:])
    mu_r, mu_k, mu_v, mu_lo = [row(t) for t in pieces(mu)]
    sh_r, sh_k, sh_v, sh_lo = [t.reshape(B, 1, -1) for t in pieces(sh)]
    hid = jnp.arange(W) // RW_HEAD_DIM
    e_blk = (hid[:, None] == hid[None, :]).astype(BF16)
    ti = jnp.arange(TT)
    same = (ti[:, None] // CHUNK) == (ti[None, :] // CHUNK)
    tri_in = (same & (ti[None, :] <= ti[:, None])).astype(BF16)
    tri_suf = (same & (ti[None, :] > ti[:, None])).astype(BF16)

    tok = lambda w, blk: pl.BlockSpec((TT, w), lambda b, c: (b * nt + c, blk))
    full = lambda shape: pl.BlockSpec(shape, lambda b, c: (0,) * len(shape))
    shs = lambda w: pl.BlockSpec((1, 1, w), lambda b, c: (b, 0, 0))
    st_spec = pl.BlockSpec((1, RW_HEADS, RW_HEAD_DIM, RW_HEAD_DIM), lambda b, c: (b, 0, 0, 0))
    big = lambda: pltpu.VMEM((TT, W), F32)
    y, s_out = pl.pallas_call(
        _rwkv_kernel,
        grid=(B, nt),
        in_specs=[tok(W, P_RKV // W), tok(W, P_RKV // W + 1), tok(W, P_RKV // W + 2),
                  tok(RW_LORA, P_LORA // RW_LORA),
                  shs(W), shs(W), shs(W), shs(RW_LORA), st_spec,
                  full((1, W)), full((1, W)), full((1, W)), full((1, RW_LORA)),
                  full((1, W)), full((RW_DECAY_LORA, W)), full((1, W)), full((RW_ICLR_LORA, W)),
                  full((RW_GATE_LORA, W)), full((1, W)), full((1, W)), full((1, W)), full((1, W)),
                  full((W, W)), full((TT, TT)), full((TT, TT))],
        out_specs=[pl.BlockSpec((TT, W), lambda b, c: (b * nt + c, 0)), st_spec],
        out_shape=[jax.ShapeDtypeStruct((T, W), F32),
                   jax.ShapeDtypeStruct((B, RW_HEADS, RW_HEAD_DIM, RW_HEAD_DIM), F32)],
        scratch_shapes=[pltpu.VMEM((8, W), F32), pltpu.VMEM((8, W), F32), pltpu.VMEM((8, W), F32),
                        pltpu.VMEM((8, RW_LORA), F32),
                        pltpu.VMEM((RW_HEADS, RW_HEAD_DIM, RW_HEAD_DIM), F32)] + [big() for _ in range(10)]
        + [pltpu.VMEM((TT // CHUNK, RW_HEADS, RW_HEAD_DIM, RW_HEAD_DIM), F32) for _ in range(2)],
        compiler_params=_params(("parallel", "arbitrary")),
        name="rwkv",
    )(P, P, P, P, sh_r, sh_k, sh_v, sh_lo, s0.astype(F32),
      mu_r, mu_k, mu_v, mu_lo, row(lp['rwkv_w0']), _bf(lp['rwkv_w2']), row(lp['rwkv_a0']),
      _bf(lp['rwkv_a2']), _bf(lp['rwkv_g2']), row(lp['rwkv_k_k']), row(lp['rwkv_k_a']),
      row(lp['rwkv_r_k']), row(lp['rwkv_lnx_g']), e_blk, tri_in, tri_suf)
    return y, s_out


ONES_ROWS = 16
BOUND_SLACK = 1.05
SAFE_BOUND = 60.0


def _dsa_kernel(q_ref, qi_ref, kw_ref, k_ref, v_ref, ki_ref, y_ref,
                kb_ref, vt_ref, kib_ref, kmax_ref, keys_ref, hi_ref, lo_ref, bias_ref, logit_ref, pstar_ref,
                *, past, lk_real, topk, kc):
    i = pl.program_id(1)
    qb = q_ref.shape[1]
    lk = k_ref.shape[1]
    kf = float(topk)
    HD = SA_HEAD_DIM

    ones_sq = jnp.ones((LANES, LANES), BF16)

    def head_slab(x, odd):
        lane = lax.broadcasted_iota(I32, x.shape, 1)
        return jnp.where(lane < HD, pltpu.roll(x, HD, 1) if odd else x, 0.0)

    @pl.when(i == 0)
    def _():
        kib_ref[...] = _bf(ki_ref[0])
        k = k_ref[0]
        lane = lax.broadcasted_iota(I32, k.shape, 1)
        for c in range(SA_KV_HEADS):
            kc_b = _bf(head_slab(k, c == 1))
            kb_ref[c] = jnp.where(lane == HD, jnp.ones_like(kc_b), kc_b)
            kf32 = kc_b.astype(F32)
            n2 = _dot(_bf(kf32 * kf32), ones_sq)
            kmax_ref[c] = jnp.broadcast_to(jnp.max(n2, axis=0, keepdims=True), (8, LANES))
        vt = v_ref[0].T
        for c in range(SA_KV_HEADS):
            vt_ref[c, 0:HD, :] = _bf(vt[c * HD:(c + 1) * HD, :])
            vt_ref[c, HD:HD + ONES_ROWS, :] = jnp.ones((ONES_ROWS, lk), BF16)

    qpos = past + i * qb + lax.broadcasted_iota(I32, (1, qb), 1)
    limit = jnp.minimum((qpos // CHUNK + 1) * CHUNK, lk_real)
    hi = jnp.minimum(past + (i + 1) * qb, lk_real)
    nk = lax.shift_right_logical(hi + (2 * kc - 1), kc.bit_length())
    sub_idx = lax.broadcasted_iota(I32, (kc, qb), 0)
    kwt = kw_ref[0].T
    wi_h = [kwt[IDX_DIM + h:IDX_DIM + h + 1, :] * INDEX_SCALE for h in range(IDX_HEADS)]
    qi = qi_ref[0]
    qi_h = [_bf(qi[:, h * IDX_DIM:(h + 1) * IDX_DIM]) for h in range(IDX_HEADS)]

    def chunk(j):
        return pl.ds(pl.multiple_of(j * kc, kc), kc)

    def pair(t):
        return (2 * t, 2 * t + 1)

    def score_body(t, carry):
        js = pair(t)
        dots = [[_dot_nt(kib_ref[chunk(j), 0:IDX_DIM], qi_h[h]) for h in range(IDX_HEADS)] for j in js]
        for j, d in zip(js, dots):
            s = jnp.zeros((kc, qb), F32)
            for h in range(IDX_HEADS):
                s = s + jnp.maximum(d[h], 0.0) * wi_h[h]
            s = jnp.where(s == 0.0, 0.0, s)
            bits = lax.bitcast_convert_type(s, I32)
            key = bits ^ ((bits >> 31) & 0x7FFFFFFF)
            key = jnp.where((sub_idx + j * kc) < limit, key, INT_MIN)
            keys_ref[chunk(j), :] = key
            hi_ref[chunk(j), :] = (key >> 16).astype(I16)
        return carry

    lax.fori_loop(0, nk, score_body, 0)

    def fold(m, rows):
        parts = [m[r * rows:(r + 1) * rows] for r in range(kc // rows)]
        while len(parts) > 1:
            parts = [a + b for a, b in zip(parts[0::2], parts[1::2])]
        return parts[0]

    def count(pred):
        def body(t, acc):
            for j in pair(t):
                acc = acc + fold(jnp.where(pred(keys_ref[chunk(j), :], j), 1.0, 0.0), 8)
            return acc

        return jnp.sum(lax.fori_loop(0, nk, body, jnp.zeros((8, qb), F32)), axis=0, keepdims=True)

    def count16(ref, pred):
        def body(t, acc):
            for j in pair(t):
                acc = acc + fold(jnp.where(pred(ref[chunk(j), :]), jnp.int16(1), jnp.int16(0)), 16)
            return acc

        acc = lax.fori_loop(0, nk, body, jnp.zeros((16, qb), I16))
        return jnp.sum(acc.astype(F32), axis=0, keepdims=True)

    def kth_bits16(ref, need):
        def bit_body(it, u):
            cand = u | lax.shift_left(jnp.int32(1), 15 - it)
            thr = (cand - 32768).astype(I16)
            return jnp.where(count16(ref, lambda x: x >= thr) >= need, cand, u)

        return lax.fori_loop(0, 16, bit_body, jnp.zeros((1, qb), I32))

    t_hi = kth_bits16(hi_ref, kf) - 32768
    t_hi16 = t_hi.astype(I16)
    need_lo = kf - count16(hi_ref, lambda x: x > t_hi16)

    def lo_body(t, carry):
        for j in pair(t):
            lo = ((keys_ref[chunk(j), :] & 0xFFFF) - 32768).astype(I16)
            lo_ref[chunk(j), :] = jnp.where(hi_ref[chunk(j), :] == t_hi16, lo, jnp.int16(-32768))
        return carry

    lax.fori_loop(0, nk, lo_body, 0)
    tau = t_hi * 65536 + kth_bits16(lo_ref, need_lo)
    cnt_ge = count(lambda kj, j: kj >= tau)
    cnt_gt = count(lambda kj, j: kj > tau)
    need = kf - cnt_gt
    excess = jnp.logical_and(cnt_ge > kf, tau != INT_MIN)
    p_default = jnp.where(tau == INT_MIN, -1, 2 ** 30).astype(I32)
    pstar_ref[...] = jnp.broadcast_to(p_default, pstar_ref.shape)

    @pl.when(jnp.max(jnp.where(excess, 1.0, 0.0)) > 0.0)
    def _():
        nbits = max(1, int(lk - 1).bit_length())

        def idx_body(it, p):
            cand = p | lax.shift_left(jnp.int32(1), nbits - 1 - it)
            g = count(lambda kj, j: jnp.logical_and(kj == tau, (sub_idx + j * kc) < cand))
            return jnp.where(g < need, cand, p)

        p = lax.fori_loop(0, nbits, idx_body, jnp.zeros((1, qb), I32))
        pstar_ref[...] = jnp.broadcast_to(jnp.where(excess, p, p_default), pstar_ref.shape)

    pstar = pstar_ref[0:1, :]

    def bias_body(t, carry):
        for j in pair(t):
            kj = keys_ref[chunk(j), :]
            sel = jnp.logical_or(kj > tau, jnp.logical_and(kj == tau, (sub_idx + j * kc) <= pstar))
            bias_ref[chunk(j), :] = jnp.where(sel, 0.0, NEG_BIG)
        return carry

    lax.fori_loop(0, nk, bias_body, 0)

    q = q_ref[0]
    group = SA_HEADS // SA_KV_HEADS
    lane_q = lax.broadcasted_iota(I32, (qb, LANES), 1)
    q_heads, bounds = [], []
    for h in range(SA_HEADS):
        qh = _bf(head_slab(q[:, (h // 2) * LANES:(h // 2 + 1) * LANES], h % 2 == 1))
        qf = qh.astype(F32)
        qn2 = _dot(_bf(qf * qf), ones_sq)
        q_heads.append(qh)
        bounds.append(jnp.sqrt(qn2 * kmax_ref[h // group, 0:1, :]) * BOUND_SLACK)
    worst = bounds[0]
    for b in bounds[1:]:
        worst = jnp.maximum(worst, b)
    safe = jnp.max(worst) <= SAFE_BOUND

    zero_acc = jnp.zeros((HD + ONES_ROWS, qb), F32)

    def vt_chunk(c, j):
        return vt_ref[c, :, pl.ds(pl.multiple_of(j * kc, kc), kc)]

    @pl.when(safe)
    def _():
        hs = range(SA_HEADS)
        qa = [jnp.where(lane_q == HD, _bf(-bounds[h]), q_heads[h]) for h in hs]

        def body(t, accs):
            js = pair(t)
            logit = [[_dot_nt(kb_ref[h // group, chunk(j), :], qa[h]) for h in hs] for j in js]
            p = [[_bf(jnp.exp2(lg + bias_ref[chunk(j), :])) for lg in row]
                 for j, row in zip(js, logit)]
            pv = [[_dot(vt_chunk(h // group, j), row[h]) for h in hs] for j, row in zip(js, p)]
            return tuple(accs[h] + pv[0][h] + pv[1][h] for h in hs)

        accs = lax.fori_loop(0, nk, body, (zero_acc,) * SA_HEADS)
        for h in hs:
            y_ref[0, h * HD:(h + 1) * HD, :] = accs[h][:HD] / accs[h][HD:HD + 1]

    @pl.when(jnp.logical_not(safe))
    def _():
        for h in range(SA_HEADS):
            c = h // group

            def max_body(t, m):
                for j in pair(t):
                    logit = _dot_nt(kb_ref[c, chunk(j), :], q_heads[h]) + bias_ref[chunk(j), :]
                    logit_ref[chunk(j), :] = logit
                    m = jnp.maximum(m, jnp.max(logit, axis=0, keepdims=True))
                return m

            m = lax.fori_loop(0, nk, max_body, jnp.full((1, qb), NEG_BIG, F32))

            def sum_body(t, acc):
                for j in pair(t):
                    acc = acc + _dot(vt_chunk(c, j), _bf(jnp.exp2(logit_ref[chunk(j), :] - m)))
                return acc

            acc = lax.fori_loop(0, nk, sum_body, zero_acc)
            y_ref[0, h * HD:(h + 1) * HD, :] = acc[:HD] / acc[HD:HD + 1]


def _dsa_call(q3, qi3, kw3, k_all, v_all, ki_all, v_blk, *, lk, past, lk_real, topk, qb, kc):
    B, n_q = q3.shape[:2]
    kern = functools.partial(_dsa_kernel, past=past, lk_real=lk_real, topk=topk, kc=kc)
    return pl.pallas_call(
        kern,
        grid=(B, n_q // qb),
        in_specs=[pl.BlockSpec((1, qb, SA_WIDTH), lambda b, i: (b, i, 0)),
                  pl.BlockSpec((1, qb, IDX_WIDTH), lambda b, i: (b, i, 0)),
                  pl.BlockSpec((1, qb, LANES), lambda b, i: (b, i, 0)),
                  pl.BlockSpec((1, lk, LANES), lambda b, i: (b, 0, 0)),
                  pl.BlockSpec((1, lk, LANES), lambda b, i: (b, 0, v_blk)),
                  pl.BlockSpec((1, lk, LANES), lambda b, i: (b, 0, 0))],
        out_specs=pl.BlockSpec((1, SA_WIDTH, qb), lambda b, i: (b, 0, i)),
        out_shape=jax.ShapeDtypeStruct((B, SA_WIDTH, n_q), F32),
        scratch_shapes=[pltpu.VMEM((SA_KV_HEADS, lk, LANES), BF16),
                        pltpu.VMEM((SA_KV_HEADS, SA_HEAD_DIM + ONES_ROWS, lk), BF16),
                        pltpu.VMEM((lk, LANES), BF16),
                        pltpu.VMEM((SA_KV_HEADS, 8, LANES), F32),
                        pltpu.VMEM((lk, qb), I32), pltpu.VMEM((lk, qb), I16), pltpu.VMEM((lk, qb), I16),
                        pltpu.VMEM((lk, qb), F32), pltpu.VMEM((lk, qb), F32),
                        pltpu.VMEM((8, qb), I32)],
        compiler_params=_params(("parallel", "arbitrary")),
        name="dsa",
    )(q3, qi3, kw3, k_all, v_all, ki_all)


def _round_up(x, m):
    return (x + m - 1) // m * m


def _dsa(P, B, L, past, q_rot, qi_rot, k_rot, kw_rot, k_past, v_past, ik_past):
    lk_real = past + L
    topk = min(TOPK_MAX, lk_real // 4)
    qb = 2 * LANES if (past == 0 and L % (2 * LANES) == 0) else LANES
    lq = _round_up(L, qb)
    qpad = lambda t: t if lq == L else jnp.pad(t, ((0, 0), (0, lq - L), (0, 0)))
    q3 = qpad(q_rot.reshape(B, L, SA_WIDTH))
    qi3 = qpad(qi_rot.reshape(B, L, IDX_WIDTH))
    kw3 = kw_rot.reshape(B, L, LANES)
    k3 = k_rot.reshape(B, L, LANES)
    common = dict(past=past, lk_real=lk_real, topk=topk, qb=qb)
    kc = TILE_ELEMS // qb // 2 if qb == LANES else TILE_ELEMS // qb
    if past == 0:
        while L % (2 * kc):
            kc //= 2
        P3 = P.reshape(B, L, P_COLS)
        return _dsa_call(q3, qi3, kw3, k3, P3, kw3, P_SAV // LANES, lk=L, kc=kc, **common)
    lk = _round_up(lk_real, 2 * kc)
    zpad = jnp.zeros((B, lk - lk_real, LANES), F32)
    v_new = P.reshape(B, L, P_COLS)[:, :, P_SAV:P_SAV + LANES]
    ik_p = jnp.concatenate([ik_past.astype(F32), jnp.zeros((B, past, LANES - IDX_DIM), F32)], axis=2)
    k_all = jnp.concatenate([k_past.reshape(B, past, LANES).astype(F32), k3, zpad], axis=1)
    v_all = jnp.concatenate([v_past.reshape(B, past, LANES).astype(F32), v_new, zpad], axis=1)
    ki_all = jnp.concatenate([ik_p, kw3, zpad], axis=1)
    y = _dsa_call(q3, qi3, qpad(kw3), k_all, v_all, ki_all, 0, lk=lk, kc=kc, **common)
    return y[:, :, :L]


def _ret_kernel(q_ref, k_ref, v_ref, g_ref, cos_ref, sin_ref, dm_ref, qd_ref, kd_ref, gc_ref, s0_ref,
                y_ref, so_ref, s_ref):
    c = pl.program_id(1)
    D = RET_HEAD_DIM
    cs = dm_ref.shape[1]
    nch = q_ref.shape[0] // cs
    heads = range(RET_HEADS)
    hsl = [slice(h * D, (h + 1) * D) for h in heads]
    rows = [slice(ch * cs, (ch + 1) * cs) for ch in range(nch)]
    prob = [(ch, h) for ch in range(nch) for h in heads]

    @pl.when(c == 0)
    def _():
        s_ref[...] = s0_ref[0]

    cos, sin = cos_ref[...], sin_ref[...]
    qb_, kb_, kd_, vb_ = [], [], [], []
    for h in heads:
        q = q_ref[:, hsl[h]]
        k = k_ref[:, hsl[h]]
        k = (k * cos + pltpu.roll(k, D // 2, 1) * sin) * (D ** -0.5)
        qb_.append(_bf(q * cos + pltpu.roll(q, D // 2, 1) * sin))
        kb_.append(_bf(k))
        kd_.append([_bf(k[r] * kd_ref[:, hsl[h]]) for r in rows])
        vb_.append(_bf(v_ref[:, hsl[h]]))
    scores = {(ch, h): _dot_nt(qb_[h][rows[ch]], kb_[h][rows[ch]]) * dm_ref[h] for ch, h in prob}
    ktv = {(ch, h): _dot_tn(kd_[h][ch], vb_[h][rows[ch]]) for ch, h in prob}
    intra = {(ch, h): _dot(_bf(scores[ch, h]), vb_[h][rows[ch]]) for ch, h in prob}
    s_at = {}
    for h in heads:
        s = s_ref[h]
        for ch in range(nch):
            s_at[ch, h] = _bf(s)
            s = s * gc_ref[:, hsl[h]] + ktv[ch, h]
        s_ref[h] = s
    cross = {(ch, h): _dot(qb_[h][rows[ch]], s_at[ch, h]) for ch, h in prob}
    for ch in range(nch):
        outs = []
        for h in heads:
            o = intra[ch, h] + cross[ch, h] * qd_ref[:, hsl[h]]
            o = o * lax.rsqrt(jnp.mean(o * o, axis=-1, keepdims=True) + EPS)
            outs.append(jax.nn.silu(g_ref[rows[ch], hsl[h]]) * o)
        y_ref[rows[ch], :] = jnp.concatenate(outs, axis=1)

    @pl.when(c == pl.num_programs(1) - 1)
    def _():
        so_ref[0] = s_ref[...]


def _retention(P, B, L, pos, s0):
    T = P.shape[0]
    c = min(CHUNK, L)
    nc = L // c
    D = RET_HEAD_DIM
    freqs = 1.0 / (RET_ROPE_BASE ** jnp.linspace(0.0, 1.0, D // 2, dtype=F32))
    ang = pos.astype(F32)[:, None] * freqs[None, :]
    cos = jnp.concatenate([jnp.cos(ang)] * 2, axis=1)
    sin = jnp.concatenate([-jnp.sin(ang), jnp.sin(ang)], axis=1)
    log_gamma = jnp.log1p(-jnp.exp2(-5.0 - jnp.arange(RET_HEADS, dtype=F32)))
    idx = jnp.arange(c, dtype=F32)
    dmask = jnp.exp(jnp.abs(idx[:, None] - idx[None, :])[None] * log_gamma[:, None, None])
    lanes = lambda t: jnp.repeat(t, D, axis=1)
    qdec = lanes(jnp.exp((idx[:, None] + 1.0) * log_gamma[None, :]))
    kdec = lanes(jnp.exp((c - 1.0 - idx)[:, None] * log_gamma[None, :]))
    gchunk = lanes(jnp.exp(c * log_gamma)[None, :])

    W = RET_WIDTH
    tt = min(4 * c, L)
    nt = L // tt
    tok = lambda blk: pl.BlockSpec((tt, W), lambda b, i: (b * nt + i, blk))
    full = lambda shape: pl.BlockSpec(shape, lambda b, i: (0,) * len(shape))
    st_spec = pl.BlockSpec((1, RET_HEADS, D, D), lambda b, i: (b, 0, 0, 0))
    y, s_out = pl.pallas_call(
        _ret_kernel,
        grid=(B, nt),
        in_specs=[tok(P_RET // W), tok(P_RET // W + 1), tok(P_RET // W + 2), tok(P_RET // W + 3),
                  pl.BlockSpec((tt, D), lambda b, i: (i, 0)), pl.BlockSpec((tt, D), lambda b, i: (i, 0)),
                  full((RET_HEADS, c, c)), full((c, W)), full((c, W)), full((1, W)), st_spec],
        out_specs=[pl.BlockSpec((tt, W), lambda b, i: (b * nt + i, 0)), st_spec],
        out_shape=[jax.ShapeDtypeStruct((T, W), F32), jax.ShapeDtypeStruct((B, RET_HEADS, D, D), F32)],
        scratch_shapes=[pltpu.VMEM((RET_HEADS, D, D), F32)],
        compiler_params=_params(("parallel", "arbitrary")),
        name="retention",
    )(P, P, P, P, cos, sin, dmask, qdec, kdec, gchunk, s0.astype(F32))
    return y, s_out


def _merge_kernel(x_ref, g0_ref, g1_ref, g2_ref, yr_ref, ys_ref, yt_ref, wr_ref, ws_ref, wt_ref, wo_ref,
                  o_ref):
    m = (jax.nn.sigmoid(g0_ref[...]) * _dot(_bf(yr_ref[...]), wr_ref[...])
         + jax.nn.sigmoid(g1_ref[...]) * _dot_tn(_bf(ys_ref[0]), ws_ref[...])
         + jax.nn.sigmoid(g2_ref[...]) * _dot(_bf(yt_ref[...]), wt_ref[...]))
    o_ref[...] = x_ref[...] + _dot(_bf(m), wo_ref[...])


def _merge(x2d, P, y_rw, y_sa_t, y_ret, w_rw, w_sa, w_ret, w_o):
    T = x2d.shape[0]
    L = y_sa_t.shape[2]
    tm = min(512, L)
    nl = L // tm
    D = D_MODEL
    tok = lambda w, blk: pl.BlockSpec((tm, w), lambda i: (i, blk))
    full = lambda shape: pl.BlockSpec(shape, lambda i: (0, 0))
    return pl.pallas_call(
        _merge_kernel,
        grid=(T // tm,),
        in_specs=[tok(D, 0), tok(D, 0), tok(D, 1), tok(D, 2), tok(RW_WIDTH, 0),
                  pl.BlockSpec((1, SA_WIDTH, tm), lambda i: (i // nl, 0, i % nl)),
                  tok(RET_WIDTH, 0), full((RW_WIDTH, D)), full((SA_WIDTH, D)), full((RET_WIDTH, D)),
                  full((D, D))],
        out_specs=tok(D, 0),
        out_shape=jax.ShapeDtypeStruct((T, D), F32),
        compiler_params=_params(("parallel",)),
        name="merge",
    )(x2d, P, P, P, y_rw, y_sa_t, y_ret, w_rw, w_sa, w_ret, w_o)


def _mlp_kernel(x_ref, g_ref, wu_ref, wd_ref, gf_ref, o_ref, h_ref, acc_ref, *, final_norm):
    j = pl.program_id(1)

    @pl.when(j == 0)
    def _():
        x = x_ref[...]
        ms = jnp.mean(x * x, axis=-1, keepdims=True)
        h_ref[...] = _bf(x * lax.rsqrt(ms + EPS) * g_ref[...])
        acc_ref[...] = jnp.zeros_like(acc_ref)

    u = jnp.maximum(_dot(h_ref[...], wu_ref[...]), 0.0)
    acc_ref[...] += _dot(_bf(u * u), wd_ref[...])

    @pl.when(j == pl.num_programs(1) - 1)
    def _():
        xn = x_ref[...] + acc_ref[...]
        if final_norm:
            ms = jnp.mean(xn * xn, axis=-1, keepdims=True)
            xn = xn * lax.rsqrt(ms + EPS) * gf_ref[...]
        o_ref[...] = xn


def _mlp(x2d, g, w_up, w_down, g_final, final_norm):
    T = x2d.shape[0]
    tm = min(1024, T)
    tf = 1024
    D = D_MODEL
    return pl.pallas_call(
        functools.partial(_mlp_kernel, final_norm=final_norm),
        grid=(T // tm, D_FF // tf),
        in_specs=[pl.BlockSpec((tm, D), lambda i, j: (i, 0)),
                  pl.BlockSpec((1, D), lambda i, j: (0, 0)),
                  pl.BlockSpec((D, tf), lambda i, j: (0, j)),
                  pl.BlockSpec((tf, D), lambda i, j: (j, 0)),
                  pl.BlockSpec((1, D), lambda i, j: (0, 0))],
        out_specs=pl.BlockSpec((tm, D), lambda i, j: (i, 0)),
        out_shape=jax.ShapeDtypeStruct((T, D), F32),
        scratch_shapes=[pltpu.VMEM((tm, D), BF16), pltpu.VMEM((tm, D), F32)],
        compiler_params=_params(("parallel", "arbitrary")),
        name="mlp",
    )(x2d, g, w_up, w_down, g_final)


def _layer(x2d, B, L, past, caches, lp, wts, g_final, final_norm):
    k_past, v_past, ik_past, s_rw, shift_rw, s_ret = caches
    pos = past + jnp.arange(L, dtype=jnp.int32)
    row = lambda t: t.reshape(1, -1).astype(F32)
    P = _in_proj(x2d, row(lp['norm1_g']), wts['w_in'], wts['layer'])
    q_rot, qi_rot, k_rot, kw_rot = _sa_prep(P, L, _sa_tables(pos))
    y_rw, s_rw_new = _rwkv(P, B, L, shift_rw, s_rw, lp)
    y_sa = _dsa(P, B, L, past, q_rot, qi_rot, k_rot, kw_rot, k_past, v_past, ik_past)
    y_ret, s_ret_new = _retention(P, B, L, pos, s_ret)
    x2d = _merge(x2d, P, y_rw, y_sa, y_ret, wts['w_br_rwkv'], wts['w_br_dsa'], wts['w_br_ret'], wts['w_o'])
    x2d = _mlp(x2d, row(lp['norm2_g']), wts['w_up'], wts['w_down'], g_final, final_norm)
    P3 = P.reshape(B, L, P_COLS)
    last = P3[:, L - 1]
    shift_new = jnp.concatenate([last[:, P_RKV:P_RKV + 3 * RW_WIDTH], last[:, P_LORA:P_LORA + RW_LORA]], axis=1)
    k_new = k_rot.reshape(B, L, SA_KV_HEADS, SA_HEAD_DIM)
    v_new = P3[:, :, P_SAV:P_SAV + SA_KV_WIDTH].reshape(B, L, SA_KV_HEADS, SA_HEAD_DIM)
    ik_new = kw_rot.reshape(B, L, LANES)[:, :, :IDX_DIM]
    return x2d, (k_new, v_new, ik_new, s_rw_new, shift_new, s_ret_new)


def kernel(x_prompt, x_sample, cache_dsa_k, cache_dsa_v, cache_dsa_ik, state_rwkv, state_rwkv_shift, state_ret, norm1_g, w_in, rwkv_mu, rwkv_w0, rwkv_w2, rwkv_a0, rwkv_a2, rwkv_g2, rwkv_k_k, rwkv_k_a, rwkv_r_k, rwkv_lnx_g, w_br_rwkv, w_br_dsa, w_br_ret, w_o, norm2_g, w_up, w_down, final_norm_g):
    params = {
        'norm1_g': norm1_g, 'rwkv_mu': rwkv_mu, 'rwkv_w0': rwkv_w0, 'rwkv_w2': rwkv_w2,
        'rwkv_a0': rwkv_a0, 'rwkv_a2': rwkv_a2, 'rwkv_g2': rwkv_g2, 'rwkv_k_k': rwkv_k_k,
        'rwkv_k_a': rwkv_k_a, 'rwkv_r_k': rwkv_r_k, 'rwkv_lnx_g': rwkv_lnx_g, 'norm2_g': norm2_g,
    }
    depth = w_in.shape[0]
    Bp, Lp, D = x_prompt.shape
    Bs, Ls, _ = x_sample.shape
    past_s = cache_dsa_k.shape[2]
    xp = x_prompt.reshape(Bp * Lp, D).astype(F32)
    xs = x_sample.reshape(Bs * Ls, D).astype(F32)
    g_final = final_norm_g.reshape(1, D).astype(F32)
    zero_p = (None, None, None,
              jnp.zeros((Bp, RW_HEADS, RW_HEAD_DIM, RW_HEAD_DIM), F32), jnp.zeros((Bp, RW_COLS), F32),
              jnp.zeros((Bp, RET_HEADS, RET_HEAD_DIM, RET_HEAD_DIM), F32))
    p_states = [[] for _ in range(6)]
    s_states = [[] for _ in range(6)]
    w_all = _w_prep(w_in)
    for i in range(depth):
        lp = {name: arr[i] for name, arr in params.items()}
        wts = {'w_in': w_all, 'layer': i, 'w_br_rwkv': _bf(w_br_rwkv[i]), 'w_br_dsa': _bf(w_br_dsa[i]),
               'w_br_ret': _bf(w_br_ret[i]), 'w_o': _bf(w_o[i]), 'w_up': _bf(w_up[i]), 'w_down': _bf(w_down[i])}
        final = i == depth - 1
        cache_s = (cache_dsa_k[i], cache_dsa_v[i], cache_dsa_ik[i], state_rwkv[i], state_rwkv_shift[i],
                   state_ret[i])
        xp, new_p = _layer(xp, Bp, Lp, 0, zero_p, lp, wts, g_final, final)
        xs, new_s = _layer(xs, Bs, Ls, past_s, cache_s, lp, wts, g_final, final)
        for j in range(6):
            p_states[j].append(new_p[j])
            s_states[j].append(new_s[j])
    y_prompt = xp.reshape(Bp, Lp, D)
    y_sample = xs.reshape(Bs, Ls, D)
    p_out = [jnp.stack(t, axis=0) for t in p_states]
    s_out = [jnp.stack(t, axis=0) for t in s_states]
    return (y_prompt, y_sample, *p_out, *s_out)
```

```python
import functools

import numpy as np
import jax
import jax.numpy as jnp
from jax import lax
from jax.experimental import pallas as pl
from jax.experimental.pallas import tpu as pltpu

F32 = jnp.float32
BF16 = jnp.bfloat16
I32 = jnp.int32
I16 = jnp.int16

D_MODEL = 1024
CHUNK = 64
Q_BLOCK = 128
EPS = 1e-6

RW_HEADS = 8
RW_HEAD_DIM = 64
RW_WIDTH = RW_HEADS * RW_HEAD_DIM
RW_DECAY_LORA = 64
RW_ICLR_LORA = 64
RW_GATE_LORA = 128
RW_LORA = RW_DECAY_LORA + RW_ICLR_LORA + RW_GATE_LORA
RW_COLS = 3 * RW_WIDTH + RW_LORA
RW_GN_EPS = 64e-5

SA_HEADS = 8
SA_KV_HEADS = 2
SA_HEAD_DIM = 64
SA_WIDTH = SA_HEADS * SA_HEAD_DIM
SA_KV_WIDTH = SA_KV_HEADS * SA_HEAD_DIM
IDX_HEADS = 4
IDX_DIM = 64
IDX_WIDTH = IDX_HEADS * IDX_DIM
TOPK_MAX = 256
ROPE_THETA = 500000.0
ROPE_DIM = SA_HEAD_DIM // 4
INDEX_SCALE = (IDX_DIM ** -0.5) * (IDX_HEADS ** -0.5)
SA_COLS = SA_WIDTH + 2 * SA_KV_WIDTH + IDX_WIDTH + IDX_DIM + IDX_HEADS

RET_HEADS = 4
RET_HEAD_DIM = 128
RET_WIDTH = RET_HEADS * RET_HEAD_DIM
RET_ROPE_BASE = 10000.0
RET_COLS = 4 * RET_WIDTH

N_BRANCH = 3
GATE_COLS = N_BRANCH * D_MODEL
IN_COLS = RW_COLS + SA_COLS + RET_COLS + GATE_COLS
D_FF = 4 * D_MODEL

LANES = 128
TILE_ELEMS = 64 * 8 * LANES

P_GATE = 0
P_RET = P_GATE + GATE_COLS
P_RKV = P_RET + RET_COLS
P_SAQ = P_RKV + 3 * RW_WIDTH
P_LORA = P_SAQ + SA_WIDTH
P_QI = P_LORA + RW_LORA
P_SAK = P_QI + IDX_WIDTH
P_SAV = P_SAK + SA_KV_WIDTH
P_KIWI = P_SAV + SA_KV_WIDTH
P_COLS = 8192
INT_MIN = -2 ** 31
NEG_BIG = -1e30
LOG2_E = 1.4426950408889634
VMEM_LIMIT = 56 * 1024 * 1024


def _bf(x):
    return x.astype(BF16)


def _dot(a, b):
    return jnp.dot(a, b, preferred_element_type=F32)


def _dot_nt(a, b):
    return lax.dot_general(a, b, (((1,), (1,)), ((), ())), preferred_element_type=F32)


def _dot_tn(a, b):
    return lax.dot_general(a, b, (((0,), (0,)), ((), ())), preferred_element_type=F32)


def _dot_split2(a_exact, x):
    hi = _bf(x)
    lo = _bf(x - hi.astype(F32))
    return _dot(a_exact, hi) + _dot(a_exact, lo)


def _params(sem):
    return pltpu.CompilerParams(dimension_semantics=sem, vmem_limit_bytes=VMEM_LIMIT)


def _in_proj_kernel(x_ref, g_ref, w_ref, o_ref, h_ref):
    @pl.when(pl.program_id(1) == 0)
    def _():
        x = x_ref[...]
        ms = jnp.mean(x * x, axis=-1, keepdims=True)
        h_ref[...] = _bf(x * lax.rsqrt(ms + EPS) * g_ref[...])

    o_ref[...] = _dot(h_ref[...], w_ref[0])


def _in_proj(x2d, g, w_all, layer):
    T = x2d.shape[0]
    tm = min(1024, T)
    tn = 1024
    return pl.pallas_call(
        _in_proj_kernel,
        grid=(T // tm, P_COLS // tn),
        in_specs=[pl.BlockSpec((tm, D_MODEL), lambda i, j: (i, 0)),
                  pl.BlockSpec((1, D_MODEL), lambda i, j: (0, 0)),
                  pl.BlockSpec((1, D_MODEL, tn), lambda i, j: (layer, 0, j))],
        out_specs=pl.BlockSpec((tm, tn), lambda i, j: (i, j)),
        out_shape=jax.ShapeDtypeStruct((T, P_COLS), F32),
        scratch_shapes=[pltpu.VMEM((tm, D_MODEL), BF16)],
        compiler_params=_params(("parallel", "arbitrary")),
        name="in_proj",
    )(x2d, g, w_all)


KIWI_COLS = IDX_DIM + IDX_HEADS
TAIL0 = RW_COLS + SA_COLS - KIWI_COLS
TAIL_W = (IN_COLS - TAIL0 + LANES - 1) // LANES * LANES


def _w_prep_kernel(w_ref, o_ref):
    x = w_ref[0]
    x = jnp.where(lax.broadcasted_iota(I32, x.shape, 1) < IN_COLS, x, 0.0)
    lane = lax.broadcasted_iota(I32, (x.shape[0], LANES), 1)

    def put(off, v):
        o_ref[0, :, off:off + v.shape[1]] = _bf(v)

    nslab = TAIL_W // LANES
    rolled = [pltpu.roll(x[:, TAIL0 + s * LANES:TAIL0 + (s + 1) * LANES], LANES - KIWI_COLS, 1)
              for s in range(nslab)]
    for s in range((RET_COLS + GATE_COLS) // LANES):
        v = jnp.where(lane < LANES - KIWI_COLS, rolled[s], rolled[s + 1])
        put((P_RET if s < RET_COLS // LANES else P_GATE - RET_COLS) + s * LANES, v)
    sa = RW_COLS
    put(P_RKV, x[:, 0:3 * RW_WIDTH])
    put(P_LORA, x[:, 3 * RW_WIDTH:RW_COLS])
    put(P_SAQ, x[:, sa:sa + SA_WIDTH])
    put(P_SAK, x[:, sa + SA_WIDTH:sa + SA_WIDTH + SA_KV_WIDTH])
    put(P_SAV, x[:, sa + SA_WIDTH + SA_KV_WIDTH:sa + SA_WIDTH + 2 * SA_KV_WIDTH])
    put(P_QI, x[:, sa + SA_WIDTH + 2 * SA_KV_WIDTH:TAIL0])
    put(P_KIWI, jnp.where(lane < KIWI_COLS, x[:, TAIL0:TAIL0 + LANES], 0.0))
    put(P_KIWI + LANES, jnp.zeros((x.shape[0], P_COLS - P_KIWI - LANES), F32))


def _w_prep(w_in):
    depth, d, _ = w_in.shape
    tm = 256
    return pl.pallas_call(
        _w_prep_kernel,
        grid=(depth, d // tm),
        in_specs=[pl.BlockSpec((1, tm, TAIL0 + TAIL_W), lambda l, i: (l, i, 0))],
        out_specs=pl.BlockSpec((1, tm, P_COLS), lambda l, i: (l, i, 0)),
        out_shape=jax.ShapeDtypeStruct((depth, d, P_COLS), BF16),
        compiler_params=_params(("parallel", "parallel")),
        name="w_prep",
    )(w_in)


def _rot_slab(x, c, s_lo, s_hi, shift):
    return x * c + pltpu.roll(x, LANES - shift, 1) * s_lo + pltpu.roll(x, shift, 1) * s_hi


def _sa_prep_kernel(q_ref, qi_ref, k_ref, kw_ref, c_ref, s1_ref, s2_ref, ck_ref, s1k_ref, s2k_ref,
                    qo_ref, qio_ref, ko_ref, kwo_ref):
    c, s1, s2 = c_ref[...], s1_ref[...], s2_ref[...]
    half = ROPE_DIM // 2
    scale = SA_HEAD_DIM ** -0.5 * LOG2_E
    for s in range(SA_WIDTH // LANES):
        sl = slice(s * LANES, (s + 1) * LANES)
        qo_ref[:, sl] = _rot_slab(q_ref[:, sl], c, s1, s2, half) * scale
    for s in range(IDX_WIDTH // LANES):
        sl = slice(s * LANES, (s + 1) * LANES)
        qio_ref[:, sl] = _rot_slab(qi_ref[:, sl], c, s1, s2, half)
    ko_ref[...] = _rot_slab(k_ref[...], c, s1, s2, half)
    kwo_ref[...] = _rot_slab(kw_ref[...], ck_ref[...], s1k_ref[...], s2k_ref[...], half)


def _sa_prep(P, L, tabs):
    T = P.shape[0]
    tm = min(512, L)
    nl = L // tm
    tab_spec = pl.BlockSpec((tm, LANES), lambda i: (i % nl, 0))
    return pl.pallas_call(
        _sa_prep_kernel,
        grid=(T // tm,),
        in_specs=[pl.BlockSpec((tm, SA_WIDTH), lambda i: (i, P_SAQ // SA_WIDTH)),
                  pl.BlockSpec((tm, IDX_WIDTH), lambda i: (i, P_QI // IDX_WIDTH)),
                  pl.BlockSpec((tm, LANES), lambda i: (i, P_SAK // LANES)),
                  pl.BlockSpec((tm, LANES), lambda i: (i, P_KIWI // LANES))] + [tab_spec] * 6,
        out_specs=[pl.BlockSpec((tm, SA_WIDTH), lambda i: (i, 0)),
                   pl.BlockSpec((tm, IDX_WIDTH), lambda i: (i, 0)),
                   pl.BlockSpec((tm, LANES), lambda i: (i, 0)),
                   pl.BlockSpec((tm, LANES), lambda i: (i, 0))],
        out_shape=[jax.ShapeDtypeStruct((T, SA_WIDTH), F32),
                   jax.ShapeDtypeStruct((T, IDX_WIDTH), F32),
                   jax.ShapeDtypeStruct((T, LANES), F32),
                   jax.ShapeDtypeStruct((T, LANES), F32)],
        compiler_params=_params(("parallel",)),
        name="sa_prep",
    )(P, P, P, P, *tabs)


def _sa_tables(pos):
    half = ROPE_DIM // 2
    freqs = 1.0 / (ROPE_THETA ** (jnp.arange(0, ROPE_DIM, 2, dtype=F32) / ROPE_DIM))
    ang = pos.astype(F32)[:, None] * freqs[None, :]
    cos, sin = jnp.cos(ang), jnp.sin(ang)
    n = pos.shape[0]
    pad = SA_HEAD_DIM - ROPE_DIM
    c_head = jnp.concatenate([cos, cos, jnp.ones((n, pad), F32)], axis=1)
    s1_head = jnp.concatenate([-sin, jnp.zeros((n, half + pad), F32)], axis=1)
    s2_head = jnp.concatenate([jnp.zeros((n, half), F32), sin, jnp.zeros((n, pad), F32)], axis=1)
    one, zero = jnp.ones((n, SA_HEAD_DIM), F32), jnp.zeros((n, SA_HEAD_DIM), F32)
    two = lambda t: jnp.concatenate([t, t], axis=1)
    return (two(c_head), two(s1_head), two(s2_head),
            jnp.concatenate([c_head, one], axis=1), jnp.concatenate([s1_head, zero], axis=1),
            jnp.concatenate([s2_head, zero], axis=1))


def _softplus(u):
    return jnp.maximum(u, 0.0) + jnp.log(1.0 + jnp.exp(-jnp.abs(u)))


def _rwkv_kernel(r_ref, k_ref, v_ref, lo_ref, shr_ref, shk_ref, shv_ref, shlo_ref, s0_ref,
                 mur_ref, muk_ref, muv_ref, mulo_ref, w0_ref, w2_ref, a0_ref, a2_ref, g2_ref,
                 kk_ref, ka_ref, rk_ref, lnx_ref, e_ref, tin_ref, tsuf_ref,
                 y_ref, so_ref,
                 cr_ref, ck_ref, cv_ref, clo_ref, s_ref,
                 kap_ref, rt_ref, bh_ref, kh_ref, bt_ref, kt_ref, vv_ref, gc_ref, yy_ref, rr_ref,
                 ac_ref, cc_ref):
    c = pl.program_id(1)
    TT = r_ref.shape[0]
    nch = TT // CHUNK
    N = RW_HEAD_DIM

    @pl.when(c == 0)
    def _():
        cr_ref[0:1, :] = shr_ref[0]
        ck_ref[0:1, :] = shk_ref[0]
        cv_ref[0:1, :] = shv_ref[0]
        clo_ref[0:1, :] = shlo_ref[0]
        s_ref[...] = s0_ref[0]

    def lerp(p_ref, carry_ref, mu_ref):
        p = p_ref[...]
        rolled = pltpu.roll(p, 1, 0)
        row0 = lax.broadcasted_iota(I32, p.shape, 0) == 0
        prev = jnp.where(row0, carry_ref[0:1, :], rolled)
        carry_ref[0:1, :] = p[TT - 1:TT, :]
        return p + (prev - p) * mu_ref[...]

    xr = lerp(r_ref, cr_ref, mur_ref)
    xk = lerp(k_ref, ck_ref, muk_ref)
    xv = lerp(v_ref, cv_ref, muv_ref)
    xlo = lerp(lo_ref, clo_ref, mulo_ref)
    xw = xlo[:, :RW_DECAY_LORA]
    xa = xlo[:, RW_DECAY_LORA:RW_DECAY_LORA + RW_ICLR_LORA]
    xg = xlo[:, RW_DECAY_LORA + RW_ICLR_LORA:]

    z = w0_ref[...] + _dot(_bf(jnp.tanh(xw)), w2_ref[...])
    w = -_softplus(-z) - 0.5
    ld = -jnp.exp(w)
    a = jax.nn.sigmoid(a0_ref[...] + _dot(_bf(xa), a2_ref[...]))
    gate = _dot(_bf(jax.nn.sigmoid(xg)), g2_ref[...])
    e_blk = e_ref[...]
    kk = xk * kk_ref[...]
    kk = kk / jnp.maximum(jnp.sqrt(_dot(_bf(kk * kk), e_blk)), 1e-12)
    k2 = xk * (1.0 + (a - 1.0) * ka_ref[...])
    bb = kk * a
    bonus = _dot(_bf(xr * k2 * rk_ref[...]), e_blk) * xv

    lin = _dot_split2(tin_ref[...], ld)
    lsuf = _dot_split2(tsuf_ref[...], ld)
    e_in = jnp.exp(lin)
    e_ninv = jnp.exp(-lin)
    e_suf = jnp.exp(lsuf)
    kap_ref[...] = kk * jnp.exp(lin - ld)
    rt_ref[...] = xr * e_in
    bh_ref[...] = bb * e_ninv
    kh_ref[...] = k2 * e_ninv
    bt_ref[...] = bb * e_suf
    kt_ref[...] = k2 * e_suf
    vv_ref[...] = xv
    gc_ref[...] = jnp.exp(lin + lsuf)

    ri = lax.broadcasted_iota(I32, (CHUNK, CHUNK), 0)
    ci = lax.broadcasted_iota(I32, (CHUNK, CHUNK), 1)
    strict = ri > ci
    incl = ri >= ci
    eye = (ri == ci).astype(F32)
    heads = range(RW_HEADS)
    hsl = [slice(h * N, (h + 1) * N) for h in heads]

    per_it = 2 if nch % 2 == 0 else 1

    def coef_body(it, carry):
        chs = [it * per_it + t for t in range(per_it)]
        pairs = [(t, h) for t in range(per_it) for h in heads]
        rows = [pl.ds(pl.multiple_of(ch * CHUNK, CHUNK), CHUNK) for ch in chs]
        ld = lambda ref, p: ref[rows[p[0]], hsl[p[1]]]
        kap = [ld(kap_ref, p) for p in pairs]
        rt = [ld(rt_ref, p) for p in pairs]
        vh = [ld(vv_ref, p) for p in pairs]
        idx = range(len(pairs))
        gmat = [_dot_nt(_bf(jnp.concatenate([kap[n], rt[n]], axis=0)),
                        _bf(jnp.concatenate([ld(bh_ref, pairs[n]), ld(kh_ref, pairs[n])], axis=0)))
                for n in idx]
        n_ab = [jnp.where(strict, g[:CHUNK, :CHUNK], 0.0) for g in gmat]
        m_rb = [jnp.where(incl, g[CHUNK:, :CHUNK], 0.0) for g in gmat]
        m_v = [_bf(jnp.concatenate([jnp.where(strict, g[:CHUNK, CHUNK:], 0.0),
                                    jnp.where(incl, g[CHUNK:, CHUNK:], 0.0)], axis=0)) for g in gmat]
        mv = [_dot(m_v[n], _bf(vh[n])) for n in idx]
        x_inv = [eye - t for t in n_ab]
        pw = n_ab
        for _ in range(5):
            pwb = [_bf(p) for p in pw]
            pw = [_dot(p, p) for p in pwb]
            x_inv = [x + _dot(_bf(x), _bf(p)) for x, p in zip(x_inv, pw)]
        w = [_dot(_bf(x_inv[n]), _bf(jnp.concatenate([kap[n], mv[n][:CHUNK]], axis=1))) for n in idx]
        wb = [_bf(t) for t in w]
        ry = [jnp.concatenate([rt[n], mv[n][CHUNK:]], axis=1) - _dot(_bf(m_rb[n]), wb[n]) for n in idx]
        dmat = [_dot_tn(wb[n], _bf(ld(bt_ref, pairs[n]))) for n in idx]
        vtk = [_dot_tn(_bf(vh[n]), _bf(ld(kt_ref, pairs[n]))) for n in idx]
        for n, (t, h) in enumerate(pairs):
            rr_ref[rows[t], hsl[h]] = ry[n][:, :N]
            yy_ref[rows[t], hsl[h]] = ry[n][:, N:]
            ac_ref[chs[t], h] = -dmat[n][:N]
            cc_ref[chs[t], h] = vtk[n] - dmat[n][N:]
        return carry

    lax.fori_loop(0, nch // per_it, coef_body, 0)

    def state_body(ch, carry):
        rows = pl.ds(pl.multiple_of(ch * CHUNK, CHUNK), CHUNK)
        s_old = [s_ref[h] for h in heads]
        sb = [_bf(t) for t in s_old]
        s_new = [_dot(sb[h], _bf(ac_ref[ch, h])) for h in heads]
        y_c = [_dot_nt(_bf(rr_ref[rows, hsl[h]]), sb[h]) for h in heads]
        for h in heads:
            gch = gc_ref[pl.ds(pl.multiple_of(ch * CHUNK, CHUNK), 1), hsl[h]]
            s_ref[h] = s_old[h] * gch + s_new[h] + cc_ref[ch, h]
            yy_ref[rows, hsl[h]] = yy_ref[rows, hsl[h]] + y_c[h]
        return carry

    lax.fori_loop(0, nch, state_body, 0)

    y = yy_ref[...]
    mean = _dot(_bf(y), e_blk) * (1.0 / N)
    d = y - mean
    var = _dot(_bf(d * d), e_blk) * (1.0 / N)
    yn = d * lax.rsqrt(var + RW_GN_EPS) * lnx_ref[...]
    y_ref[...] = (yn + bonus) * gate

    @pl.when(c == pl.num_programs(1) - 1)
    def _():
        so_ref[0] = s_ref[...]


def _rwkv(P, B, L, shift_prev, s0, lp):
    T = P.shape[0]
    TT = min(256, L)
    nt = L // TT
    W = RW_WIDTH
    row = lambda t: t.reshape(1, -1).astype(F32)
    mu = lp['rwkv_mu']
    sh = shift_prev.astype(F32)
    pieces = lambda t: (t[..., 0:W], t[..., W:2 * W], t[..., 2 * W:3 * W], t[..., 3 * W:])
    mu_r, mu_k, mu_v, mu_lo = [row(t) for t in pieces(mu)]
    sh_r, sh_k, sh_v, sh_lo = [t.reshape(B, 1, -1) for t in pieces(sh)]
    hid = jnp.arange(W) // RW_HEAD_DIM
    e_blk = (hid[:, None] == hid[None, :]).astype(BF16)
    ti = jnp.arange(TT)
    same = (ti[:, None] // CHUNK) == (ti[None, :] // CHUNK)
    tri_in = (same & (ti[None, :] <= ti[:, None])).astype(BF16)
    tri_suf = (same & (ti[None, :] > ti[:, None])).astype(BF16)

    tok = lambda w, blk: pl.BlockSpec((TT, w), lambda b, c: (b * nt + c, blk))
    full = lambda shape: pl.BlockSpec(shape, lambda b, c: (0,) * len(shape))
    shs = lambda w: pl.BlockSpec((1, 1, w), lambda b, c: (b, 0, 0))
    st_spec = pl.BlockSpec((1, RW_HEADS, RW_HEAD_DIM, RW_HEAD_DIM), lambda b, c: (b, 0, 0, 0))
    big = lambda: pltpu.VMEM((TT, W), F32)
    y, s_out = pl.pallas_call(
        _rwkv_kernel,
        grid=(B, nt),
        in_specs=[tok(W, P_RKV // W), tok(W, P_RKV // W + 1), tok(W, P_RKV // W + 2),
                  tok(RW_LORA, P_LORA // RW_LORA),
                  shs(W), shs(W), shs(W), shs(RW_LORA), st_spec,
                  full((1, W)), full((1, W)), full((1, W)), full((1, RW_LORA)),
                  full((1, W)), full((RW_DECAY_LORA, W)), full((1, W)), full((RW_ICLR_LORA, W)),
                  full((RW_GATE_LORA, W)), full((1, W)), full((1, W)), full((1, W)), full((1, W)),
                  full((W, W)), full((TT, TT)), full((TT, TT))],
        out_specs=[pl.BlockSpec((TT, W), lambda b, c: (b * nt + c, 0)), st_spec],
        out_shape=[jax.ShapeDtypeStruct((T, W), F32),
                   jax.ShapeDtypeStruct((B, RW_HEADS, RW_HEAD_DIM, RW_HEAD_DIM), F32)],
        scratch_shapes=[pltpu.VMEM((8, W), F32), pltpu.VMEM((8, W), F32), pltpu.VMEM((8, W), F32),
                        pltpu.VMEM((8, RW_LORA), F32),
                        pltpu.VMEM((RW_HEADS, RW_HEAD_DIM, RW_HEAD_DIM), F32)] + [big() for _ in range(10)]
        + [pltpu.VMEM((TT // CHUNK, RW_HEADS, RW_HEAD_DIM, RW_HEAD_DIM), F32) for _ in range(2)],
        compiler_params=_params(("parallel", "arbitrary")),
        name="rwkv",
    )(P, P, P, P, sh_r, sh_k, sh_v, sh_lo, s0.astype(F32),
      mu_r, mu_k, mu_v, mu_lo, row(lp['rwkv_w0']), _bf(lp['rwkv_w2']), row(lp['rwkv_a0']),
      _bf(lp['rwkv_a2']), _bf(lp['rwkv_g2']), row(lp['rwkv_k_k']), row(lp['rwkv_k_a']),
      row(lp['rwkv_r_k']), row(lp['rwkv_lnx_g']), e_blk, tri_in, tri_suf)
    return y, s_out


ONES_ROWS = 16
BOUND_SLACK = 1.05
SAFE_BOUND = 60.0


def _dsa_kernel(q_ref, qi_ref, kw_ref, k_ref, v_ref, ki_ref, y_ref,
                kb_ref, vt_ref, kib_ref, kmax_ref, keys_ref, hi_ref, lo_ref, bias_ref, logit_ref, pstar_ref,
                *, past, lk_real, topk, kc):
    i = pl.program_id(1)
    qb = q_ref.shape[1]
    lk = k_ref.shape[1]
    kf = float(topk)
    HD = SA_HEAD_DIM

    ones_sq = jnp.ones((LANES, LANES), BF16)

    def head_slab(x, odd):
        lane = lax.broadcasted_iota(I32, x.shape, 1)
        return jnp.where(lane < HD, pltpu.roll(x, HD, 1) if odd else x, 0.0)

    @pl.when(i == 0)
    def _():
        kib_ref[...] = _bf(ki_ref[0])
        k = k_ref[0]
        lane = lax.broadcasted_iota(I32, k.shape, 1)
        for c in range(SA_KV_HEADS):
            kc_b = _bf(head_slab(k, c == 1))
            kb_ref[c] = jnp.where(lane == HD, jnp.ones_like(kc_b), kc_b)
            kf32 = kc_b.astype(F32)
            n2 = _dot(_bf(kf32 * kf32), ones_sq)
            kmax_ref[c] = jnp.broadcast_to(jnp.max(n2, axis=0, keepdims=True), (8, LANES))
        vt = v_ref[0].T
        for c in range(SA_KV_HEADS):
            vt_ref[c, 0:HD, :] = _bf(vt[c * HD:(c + 1) * HD, :])
            vt_ref[c, HD:HD + ONES_ROWS, :] = jnp.ones((ONES_ROWS, lk), BF16)

    qpos = past + i * qb + lax.broadcasted_iota(I32, (1, qb), 1)
    limit = jnp.minimum((qpos // CHUNK + 1) * CHUNK, lk_real)
    hi = jnp.minimum(past + (i + 1) * qb, lk_real)
    nk = lax.shift_right_logical(hi + (2 * kc - 1), kc.bit_length())
    sub_idx = lax.broadcasted_iota(I32, (kc, qb), 0)
    kwt = kw_ref[0].T
    wi_h = [kwt[IDX_DIM + h:IDX_DIM + h + 1, :] * INDEX_SCALE for h in range(IDX_HEADS)]
    qi = qi_ref[0]
    qi_h = [_bf(qi[:, h * IDX_DIM:(h + 1) * IDX_DIM]) for h in range(IDX_HEADS)]

    def chunk(j):
        return pl.ds(pl.multiple_of(j * kc, kc), kc)

    def pair(t):
        return (2 * t, 2 * t + 1)

    def score_body(t, carry):
        js = pair(t)
        dots = [[_dot_nt(kib_ref[chunk(j), 0:IDX_DIM], qi_h[h]) for h in range(IDX_HEADS)] for j in js]
        for j, d in zip(js, dots):
            s = jnp.zeros((kc, qb), F32)
            for h in range(IDX_HEADS):
                s = s + jnp.maximum(d[h], 0.0) * wi_h[h]
            s = jnp.where(s == 0.0, 0.0, s)
            bits = lax.bitcast_convert_type(s, I32)
            key = bits ^ ((bits >> 31) & 0x7FFFFFFF)
            key = jnp.where((sub_idx + j * kc) < limit, key, INT_MIN)
            keys_ref[chunk(j), :] = key
            hi_ref[chunk(j), :] = (key >> 16).astype(I16)
        return carry

    lax.fori_loop(0, nk, score_body, 0)

    def fold(m, rows):
        parts = [m[r * rows:(r + 1) * rows] for r in range(kc // rows)]
        while len(parts) > 1:
            parts = [a + b for a, b in zip(parts[0::2], parts[1::2])]
        return parts[0]

    max_trips = lk // (2 * kc)

    def short_trips(body, carry):
        base = jnp.int32(0)
        p = 1 << (max_trips.bit_length() - 1)
        while p:
            take = (nk & p) != 0

            def run(c, base=base, p=p):
                for s in range(p):
                    c = body(base + s, c)
                return c

            carry = lax.cond(take, run, lambda c: c, carry)
            base = base + jnp.where(take, p, 0)
            p //= 2
        return carry

    def count(pred):
        def body(t, acc):
            for j in pair(t):
                acc = acc + fold(jnp.where(pred(keys_ref[chunk(j), :], j), 1.0, 0.0), 8)
            return acc

        return jnp.sum(short_trips(body, jnp.zeros((8, qb), F32)), axis=0, keepdims=True)

    def count16(ref, pred):
        def body(t, acc):
            for j in pair(t):
                acc = acc + fold(jnp.where(pred(ref[chunk(j), :]), jnp.int16(1), jnp.int16(0)), 16)
            return acc

        acc = short_trips(body, jnp.zeros((16, qb), I16))
        return jnp.sum(acc.astype(F32), axis=0, keepdims=True)

    def kth_bits16(ref, need):
        def bit_body(it, u):
            cand = u | lax.shift_left(jnp.int32(1), 15 - it)
            thr = (cand - 32768).astype(I16)
            return jnp.where(count16(ref, lambda x: x >= thr) >= need, cand, u)

        return lax.fori_loop(0, 16, bit_body, jnp.zeros((1, qb), I32))

    t_hi = kth_bits16(hi_ref, kf) - 32768
    t_hi16 = t_hi.astype(I16)
    need_lo = kf - count16(hi_ref, lambda x: x > t_hi16)

    def lo_body(t, carry):
        for j in pair(t):
            lo = ((keys_ref[chunk(j), :] & 0xFFFF) - 32768).astype(I16)
            lo_ref[chunk(j), :] = jnp.where(hi_ref[chunk(j), :] == t_hi16, lo, jnp.int16(-32768))
        return carry

    short_trips(lo_body, jnp.int32(0))
    tau = t_hi * 65536 + kth_bits16(lo_ref, need_lo)
    cnt_ge = count(lambda kj, j: kj >= tau)
    cnt_gt = count(lambda kj, j: kj > tau)
    need = kf - cnt_gt
    excess = jnp.logical_and(cnt_ge > kf, tau != INT_MIN)
    p_default = jnp.where(tau == INT_MIN, -1, 2 ** 30).astype(I32)
    pstar_ref[...] = jnp.broadcast_to(p_default, pstar_ref.shape)

    @pl.when(jnp.max(jnp.where(excess, 1.0, 0.0)) > 0.0)
    def _():
        nbits = max(1, int(lk - 1).bit_length())

        def idx_body(it, p):
            cand = p | lax.shift_left(jnp.int32(1), nbits - 1 - it)
            g = count(lambda kj, j: jnp.logical_and(kj == tau, (sub_idx + j * kc) < cand))
            return jnp.where(g < need, cand, p)

        p = lax.fori_loop(0, nbits, idx_body, jnp.zeros((1, qb), I32))
        pstar_ref[...] = jnp.broadcast_to(jnp.where(excess, p, p_default), pstar_ref.shape)

    pstar = pstar_ref[0:1, :]

    def bias_body(t, carry):
        for j in pair(t):
            kj = keys_ref[chunk(j), :]
            sel = jnp.logical_or(kj > tau, jnp.logical_and(kj == tau, (sub_idx + j * kc) <= pstar))
            bias_ref[chunk(j), :] = jnp.where(sel, 0.0, NEG_BIG)
        return carry

    short_trips(bias_body, jnp.int32(0))

    q = q_ref[0]
    group = SA_HEADS // SA_KV_HEADS
    lane_q = lax.broadcasted_iota(I32, (qb, LANES), 1)
    q_heads, bounds = [], []
    for h in range(SA_HEADS):
        qh = _bf(head_slab(q[:, (h // 2) * LANES:(h // 2 + 1) * LANES], h % 2 == 1))
        qf = qh.astype(F32)
        qn2 = _dot(_bf(qf * qf), ones_sq)
        q_heads.append(qh)
        bounds.append(jnp.sqrt(qn2 * kmax_ref[h // group, 0:1, :]) * BOUND_SLACK)
    worst = bounds[0]
    for b in bounds[1:]:
        worst = jnp.maximum(worst, b)
    safe = jnp.max(worst) <= SAFE_BOUND

    zero_acc = jnp.zeros((HD + ONES_ROWS, qb), F32)

    def vt_chunk(c, j):
        return vt_ref[c, :, pl.ds(pl.multiple_of(j * kc, kc), kc)]

    @pl.when(safe)
    def _():
        hs = range(SA_HEADS)
        qa = [jnp.where(lane_q == HD, _bf(-bounds[h]), q_heads[h]) for h in hs]

        def body(t, accs):
            js = pair(t)
            logit = [[_dot_nt(kb_ref[h // group, chunk(j), :], qa[h]) for h in hs] for j in js]
            p = [[_bf(jnp.exp2(lg + bias_ref[chunk(j), :])) for lg in row]
                 for j, row in zip(js, logit)]
            pv = [[_dot(vt_chunk(h // group, j), row[h]) for h in hs] for j, row in zip(js, p)]
            return tuple(accs[h] + pv[0][h] + pv[1][h] for h in hs)

        accs = lax.fori_loop(0, nk, body, (zero_acc,) * SA_HEADS)
        for h in hs:
            y_ref[0, h * HD:(h + 1) * HD, :] = accs[h][:HD] / accs[h][HD:HD + 1]

    @pl.when(jnp.logical_not(safe))
    def _():
        for h in range(SA_HEADS):
            c = h // group

            def max_body(t, m):
                for j in pair(t):
                    logit = _dot_nt(kb_ref[c, chunk(j), :], q_heads[h]) + bias_ref[chunk(j), :]
                    logit_ref[chunk(j), :] = logit
                    m = jnp.maximum(m, jnp.max(logit, axis=0, keepdims=True))
                return m

            m = lax.fori_loop(0, nk, max_body, jnp.full((1, qb), NEG_BIG, F32))

            def sum_body(t, acc):
                for j in pair(t):
                    acc = acc + _dot(vt_chunk(c, j), _bf(jnp.exp2(logit_ref[chunk(j), :] - m)))
                return acc

            acc = lax.fori_loop(0, nk, sum_body, zero_acc)
            y_ref[0, h * HD:(h + 1) * HD, :] = acc[:HD] / acc[HD:HD + 1]


def _dsa_call(q3, qi3, kw3, k_all, v_all, ki_all, v_blk, *, lk, past, lk_real, topk, qb, kc):
    B, n_q = q3.shape[:2]
    kern = functools.partial(_dsa_kernel, past=past, lk_real=lk_real, topk=topk, kc=kc)
    return pl.pallas_call(
        kern,
        grid=(B, n_q // qb),
        in_specs=[pl.BlockSpec((1, qb, SA_WIDTH), lambda b, i: (b, i, 0)),
                  pl.BlockSpec((1, qb, IDX_WIDTH), lambda b, i: (b, i, 0)),
                  pl.BlockSpec((1, qb, LANES), lambda b, i: (b, i, 0)),
                  pl.BlockSpec((1, lk, LANES), lambda b, i: (b, 0, 0)),
                  pl.BlockSpec((1, lk, LANES), lambda b, i: (b, 0, v_blk)),
                  pl.BlockSpec((1, lk, LANES), lambda b, i: (b, 0, 0))],
        out_specs=pl.BlockSpec((1, SA_WIDTH, qb), lambda b, i: (b, 0, i)),
        out_shape=jax.ShapeDtypeStruct((B, SA_WIDTH, n_q), F32),
        scratch_shapes=[pltpu.VMEM((SA_KV_HEADS, lk, LANES), BF16),
                        pltpu.VMEM((SA_KV_HEADS, SA_HEAD_DIM + ONES_ROWS, lk), BF16),
                        pltpu.VMEM((lk, LANES), BF16),
                        pltpu.VMEM((SA_KV_HEADS, 8, LANES), F32),
                        pltpu.VMEM((lk, qb), I32), pltpu.VMEM((lk, qb), I16), pltpu.VMEM((lk, qb), I16),
                        pltpu.VMEM((lk, qb), F32), pltpu.VMEM((lk, qb), F32),
                        pltpu.VMEM((8, qb), I32)],
        compiler_params=_params(("parallel", "arbitrary")),
        name="dsa",
    )(q3, qi3, kw3, k_all, v_all, ki_all)


def _round_up(x, m):
    return (x + m - 1) // m * m


def _dsa(P, B, L, past, q_rot, qi_rot, k_rot, kw_rot, k_past, v_past, ik_past):
    lk_real = past + L
    topk = min(TOPK_MAX, lk_real // 4)
    qb = 2 * LANES if (past == 0 and L % (2 * LANES) == 0) else LANES
    lq = _round_up(L, qb)
    qpad = lambda t: t if lq == L else jnp.pad(t, ((0, 0), (0, lq - L), (0, 0)))
    q3 = qpad(q_rot.reshape(B, L, SA_WIDTH))
    qi3 = qpad(qi_rot.reshape(B, L, IDX_WIDTH))
    kw3 = kw_rot.reshape(B, L, LANES)
    k3 = k_rot.reshape(B, L, LANES)
    common = dict(past=past, lk_real=lk_real, topk=topk, qb=qb)
    kc = TILE_ELEMS // qb // 2 if qb == LANES else TILE_ELEMS // qb
    if past == 0:
        while L % (2 * kc):
            kc //= 2
        P3 = P.reshape(B, L, P_COLS)
        return _dsa_call(q3, qi3, kw3, k3, P3, kw3, P_SAV // LANES, lk=L, kc=kc, **common)
    lk = _round_up(lk_real, 2 * kc)
    zpad = jnp.zeros((B, lk - lk_real, LANES), F32)
    v_new = P.reshape(B, L, P_COLS)[:, :, P_SAV:P_SAV + LANES]
    ik_p = jnp.concatenate([ik_past.astype(F32), jnp.zeros((B, past, LANES - IDX_DIM), F32)], axis=2)
    k_all = jnp.concatenate([k_past.reshape(B, past, LANES).astype(F32), k3, zpad], axis=1)
    v_all = jnp.concatenate([v_past.reshape(B, past, LANES).astype(F32), v_new, zpad], axis=1)
    ki_all = jnp.concatenate([ik_p, kw3, zpad], axis=1)
    y = _dsa_call(q3, qi3, qpad(kw3), k_all, v_all, ki_all, 0, lk=lk, kc=kc, **common)
    return y[:, :, :L]


def _ret_kernel(q_ref, k_ref, v_ref, g_ref, cos_ref, sin_ref, dm_ref, qd_ref, kd_ref, gc_ref, s0_ref,
                y_ref, so_ref, s_ref):
    c = pl.program_id(1)
    D = RET_HEAD_DIM
    cs = dm_ref.shape[1]
    nch = q_ref.shape[0] // cs
    heads = range(RET_HEADS)
    hsl = [slice(h * D, (h + 1) * D) for h in heads]
    rows = [slice(ch * cs, (ch + 1) * cs) for ch in range(nch)]
    prob = [(ch, h) for ch in range(nch) for h in heads]

    @pl.when(c == 0)
    def _():
        s_ref[...] = s0_ref[0]

    cos, sin = cos_ref[...], sin_ref[...]
    qb_, kb_, kd_, vb_ = [], [], [], []
    for h in heads:
        q = q_ref[:, hsl[h]]
        k = k_ref[:, hsl[h]]
        k = (k * cos + pltpu.roll(k, D // 2, 1) * sin) * (D ** -0.5)
        qb_.append(_bf(q * cos + pltpu.roll(q, D // 2, 1) * sin))
        kb_.append(_bf(k))
        kd_.append([_bf(k[r] * kd_ref[:, hsl[h]]) for r in rows])
        vb_.append(_bf(v_ref[:, hsl[h]]))
    scores = {(ch, h): _dot_nt(qb_[h][rows[ch]], kb_[h][rows[ch]]) * dm_ref[h] for ch, h in prob}
    ktv = {(ch, h): _dot_tn(kd_[h][ch], vb_[h][rows[ch]]) for ch, h in prob}
    intra = {(ch, h): _dot(_bf(scores[ch, h]), vb_[h][rows[ch]]) for ch, h in prob}
    s_at = {}
    for h in heads:
        s = s_ref[h]
        for ch in range(nch):
            s_at[ch, h] = _bf(s)
            s = s * gc_ref[:, hsl[h]] + ktv[ch, h]
        s_ref[h] = s
    cross = {(ch, h): _dot(qb_[h][rows[ch]], s_at[ch, h]) for ch, h in prob}
    for ch in range(nch):
        outs = []
        for h in heads:
            o = intra[ch, h] + cross[ch, h] * qd_ref[:, hsl[h]]
            o = o * lax.rsqrt(jnp.mean(o * o, axis=-1, keepdims=True) + EPS)
            outs.append(jax.nn.silu(g_ref[rows[ch], hsl[h]]) * o)
        y_ref[rows[ch], :] = jnp.concatenate(outs, axis=1)

    @pl.when(c == pl.num_programs(1) - 1)
    def _():
        so_ref[0] = s_ref[...]


def _retention(P, B, L, pos, s0):
    T = P.shape[0]
    c = min(CHUNK, L)
    nc = L // c
    D = RET_HEAD_DIM
    freqs = 1.0 / (RET_ROPE_BASE ** jnp.linspace(0.0, 1.0, D // 2, dtype=F32))
    ang = pos.astype(F32)[:, None] * freqs[None, :]
    cos = jnp.concatenate([jnp.cos(ang)] * 2, axis=1)
    sin = jnp.concatenate([-jnp.sin(ang), jnp.sin(ang)], axis=1)
    log_gamma = jnp.log1p(-jnp.exp2(-5.0 - jnp.arange(RET_HEADS, dtype=F32)))
    idx = jnp.arange(c, dtype=F32)
    dmask = jnp.exp(jnp.abs(idx[:, None] - idx[None, :])[None] * log_gamma[:, None, None])
    lanes = lambda t: jnp.repeat(t, D, axis=1)
    qdec = lanes(jnp.exp((idx[:, None] + 1.0) * log_gamma[None, :]))
    kdec = lanes(jnp.exp((c - 1.0 - idx)[:, None] * log_gamma[None, :]))
    gchunk = lanes(jnp.exp(c * log_gamma)[None, :])

    W = RET_WIDTH
    tt = min(4 * c, L)
    nt = L // tt
    tok = lambda blk: pl.BlockSpec((tt, W), lambda b, i: (b * nt + i, blk))
    full = lambda shape: pl.BlockSpec(shape, lambda b, i: (0,) * len(shape))
    st_spec = pl.BlockSpec((1, RET_HEADS, D, D), lambda b, i: (b, 0, 0, 0))
    y, s_out = pl.pallas_call(
        _ret_kernel,
        grid=(B, nt),
        in_specs=[tok(P_RET // W), tok(P_RET // W + 1), tok(P_RET // W + 2), tok(P_RET // W + 3),
                  pl.BlockSpec((tt, D), lambda b, i: (i, 0)), pl.BlockSpec((tt, D), lambda b, i: (i, 0)),
                  full((RET_HEADS, c, c)), full((c, W)), full((c, W)), full((1, W)), st_spec],
        out_specs=[pl.BlockSpec((tt, W), lambda b, i: (b * nt + i, 0)), st_spec],
        out_shape=[jax.ShapeDtypeStruct((T, W), F32), jax.ShapeDtypeStruct((B, RET_HEADS, D, D), F32)],
        scratch_shapes=[pltpu.VMEM((RET_HEADS, D, D), F32)],
        compiler_params=_params(("parallel", "arbitrary")),
        name="retention",
    )(P, P, P, P, cos, sin, dmask, qdec, kdec, gchunk, s0.astype(F32))
    return y, s_out


def _merge_kernel(x_ref, g0_ref, g1_ref, g2_ref, yr_ref, ys_ref, yt_ref, wr_ref, ws_ref, wt_ref, wo_ref,
                  o_ref):
    m = (jax.nn.sigmoid(g0_ref[...]) * _dot(_bf(yr_ref[...]), wr_ref[...])
         + jax.nn.sigmoid(g1_ref[...]) * _dot_tn(_bf(ys_ref[0]), ws_ref[...])
         + jax.nn.sigmoid(g2_ref[...]) * _dot(_bf(yt_ref[...]), wt_ref[...]))
    o_ref[...] = x_ref[...] + _dot(_bf(m), wo_ref[...])


def _merge(x2d, P, y_rw, y_sa_t, y_ret, w_rw, w_sa, w_ret, w_o):
    T = x2d.shape[0]
    L = y_sa_t.shape[2]
    tm = min(512, L)
    nl = L // tm
    D = D_MODEL
    tok = lambda w, blk: pl.BlockSpec((tm, w), lambda i: (i, blk))
    full = lambda shape: pl.BlockSpec(shape, lambda i: (0, 0))
    return pl.pallas_call(
        _merge_kernel,
        grid=(T // tm,),
        in_specs=[tok(D, 0), tok(D, 0), tok(D, 1), tok(D, 2), tok(RW_WIDTH, 0),
                  pl.BlockSpec((1, SA_WIDTH, tm), lambda i: (i // nl, 0, i % nl)),
                  tok(RET_WIDTH, 0), full((RW_WIDTH, D)), full((SA_WIDTH, D)), full((RET_WIDTH, D)),
                  full((D, D))],
        out_specs=tok(D, 0),
        out_shape=jax.ShapeDtypeStruct((T, D), F32),
        compiler_params=_params(("parallel",)),
        name="merge",
    )(x2d, P, P, P, y_rw, y_sa_t, y_ret, w_rw, w_sa, w_ret, w_o)


def _mlp_kernel(x_ref, g_ref, wu_ref, wd_ref, gf_ref, o_ref, h_ref, acc_ref, *, final_norm):
    j = pl.program_id(1)

    @pl.when(j == 0)
    def _():
        x = x_ref[...]
        ms = jnp.mean(x * x, axis=-1, keepdims=True)
        h_ref[...] = _bf(x * lax.rsqrt(ms + EPS) * g_ref[...])
        acc_ref[...] = jnp.zeros_like(acc_ref)

    u = jnp.maximum(_dot(h_ref[...], wu_ref[...]), 0.0)
    acc_ref[...] += _dot(_bf(u * u), wd_ref[...])

    @pl.when(j == pl.num_programs(1) - 1)
    def _():
        xn = x_ref[...] + acc_ref[...]
        if final_norm:
            ms = jnp.mean(xn * xn, axis=-1, keepdims=True)
            xn = xn * lax.rsqrt(ms + EPS) * gf_ref[...]
        o_ref[...] = xn


def _mlp(x2d, g, w_up, w_down, g_final, final_norm):
    T = x2d.shape[0]
    tm = min(1024, T)
    tf = 1024
    D = D_MODEL
    return pl.pallas_call(
        functools.partial(_mlp_kernel, final_norm=final_norm),
        grid=(T // tm, D_FF // tf),
        in_specs=[pl.BlockSpec((tm, D), lambda i, j: (i, 0)),
                  pl.BlockSpec((1, D), lambda i, j: (0, 0)),
                  pl.BlockSpec((D, tf), lambda i, j: (0, j)),
                  pl.BlockSpec((tf, D), lambda i, j: (j, 0)),
                  pl.BlockSpec((1, D), lambda i, j: (0, 0))],
        out_specs=pl.BlockSpec((tm, D), lambda i, j: (i, 0)),
        out_shape=jax.ShapeDtypeStruct((T, D), F32),
        scratch_shapes=[pltpu.VMEM((tm, D), BF16), pltpu.VMEM((tm, D), F32)],
        compiler_params=_params(("parallel", "arbitrary")),
        name="mlp",
    )(x2d, g, w_up, w_down, g_final)


def _layer(x2d, B, L, past, caches, lp, wts, g_final, final_norm):
    k_past, v_past, ik_past, s_rw, shift_rw, s_ret = caches
    pos = past + jnp.arange(L, dtype=jnp.int32)
    row = lambda t: t.reshape(1, -1).astype(F32)
    P = _in_proj(x2d, row(lp['norm1_g']), wts['w_in'], wts['layer'])
    q_rot, qi_rot, k_rot, kw_rot = _sa_prep(P, L, _sa_tables(pos))
    y_rw, s_rw_new = _rwkv(P, B, L, shift_rw, s_rw, lp)
    y_sa = _dsa(P, B, L, past, q_rot, qi_rot, k_rot, kw_rot, k_past, v_past, ik_past)
    y_ret, s_ret_new = _retention(P, B, L, pos, s_ret)
    x2d = _merge(x2d, P, y_rw, y_sa, y_ret, wts['w_br_rwkv'], wts['w_br_dsa'], wts['w_br_ret'], wts['w_o'])
    x2d = _mlp(x2d, row(lp['norm2_g']), wts['w_up'], wts['w_down'], g_final, final_norm)
    P3 = P.reshape(B, L, P_COLS)
    last = P3[:, L - 1]
    shift_new = jnp.concatenate([last[:, P_RKV:P_RKV + 3 * RW_WIDTH], last[:, P_LORA:P_LORA + RW_LORA]], axis=1)
    k_new = k_rot.reshape(B, L, SA_KV_HEADS, SA_HEAD_DIM)
    v_new = P3[:, :, P_SAV:P_SAV + SA_KV_WIDTH].reshape(B, L, SA_KV_HEADS, SA_HEAD_DIM)
    ik_new = kw_rot.reshape(B, L, LANES)[:, :, :IDX_DIM]
    return x2d, (k_new, v_new, ik_new, s_rw_new, shift_new, s_ret_new)


def kernel(x_prompt, x_sample, cache_dsa_k, cache_dsa_v, cache_dsa_ik, state_rwkv, state_rwkv_shift, state_ret, norm1_g, w_in, rwkv_mu, rwkv_w0, rwkv_w2, rwkv_a0, rwkv_a2, rwkv_g2, rwkv_k_k, rwkv_k_a, rwkv_r_k, rwkv_lnx_g, w_br_rwkv, w_br_dsa, w_br_ret, w_o, norm2_g, w_up, w_down, final_norm_g):
    params = {
        'norm1_g': norm1_g, 'rwkv_mu': rwkv_mu, 'rwkv_w0': rwkv_w0, 'rwkv_w2': rwkv_w2,
        'rwkv_a0': rwkv_a0, 'rwkv_a2': rwkv_a2, 'rwkv_g2': rwkv_g2, 'rwkv_k_k': rwkv_k_k,
        'rwkv_k_a': rwkv_k_a, 'rwkv_r_k': rwkv_r_k, 'rwkv_lnx_g': rwkv_lnx_g, 'norm2_g': norm2_g,
    }
    depth = w_in.shape[0]
    Bp, Lp, D = x_prompt.shape
    Bs, Ls, _ = x_sample.shape
    past_s = cache_dsa_k.shape[2]
    xp = x_prompt.reshape(Bp * Lp, D).astype(F32)
    xs = x_sample.reshape(Bs * Ls, D).astype(F32)
    g_final = final_norm_g.reshape(1, D).astype(F32)
    zero_p = (None, None, None,
              jnp.zeros((Bp, RW_HEADS, RW_HEAD_DIM, RW_HEAD_DIM), F32), jnp.zeros((Bp, RW_COLS), F32),
              jnp.zeros((Bp, RET_HEADS, RET_HEAD_DIM, RET_HEAD_DIM), F32))
    p_states = [[] for _ in range(6)]
    s_states = [[] for _ in range(6)]
    w_all = _w_prep(w_in)
    for i in range(depth):
        lp = {name: arr[i] for name, arr in params.items()}
        wts = {'w_in': w_all, 'layer': i, 'w_br_rwkv': _bf(w_br_rwkv[i]), 'w_br_dsa': _bf(w_br_dsa[i]),
               'w_br_ret': _bf(w_br_ret[i]), 'w_o': _bf(w_o[i]), 'w_up': _bf(w_up[i]), 'w_down': _bf(w_down[i])}
        final = i == depth - 1
        cache_s = (cache_dsa_k[i], cache_dsa_v[i], cache_dsa_ik[i], state_rwkv[i], state_rwkv_shift[i],
                   state_ret[i])
        xp, new_p = _layer(xp, Bp, Lp, 0, zero_p, lp, wts, g_final, final)
        xs, new_s = _layer(xs, Bs, Ls, past_s, cache_s, lp, wts, g_final, final)
        for j in range(6):
            p_states[j].append(new_p[j])
            s_states[j].append(new_s[j])
    y_prompt = xp.reshape(Bp, Lp, D)
    y_sample = xs.reshape(Bs, Ls, D)
    p_out = [jnp.stack(t, axis=0) for t in p_states]
    s_out = [jnp.stack(t, axis=0) for t in s_states]
    return (y_prompt, y_sample, *p_out, *s_out)
```

```python
import functools

import numpy as np
import jax
import jax.numpy as jnp
from jax import lax
from jax.experimental import pallas as pl
from jax.experimental.pallas import tpu as pltpu

F32 = jnp.float32
BF16 = jnp.bfloat16
I32 = jnp.int32
I16 = jnp.int16

D_MODEL = 1024
CHUNK = 64
Q_BLOCK = 128
EPS = 1e-6

RW_HEADS = 8
RW_HEAD_DIM = 64
RW_WIDTH = RW_HEADS * RW_HEAD_DIM
RW_DECAY_LORA = 64
RW_ICLR_LORA = 64
RW_GATE_LORA = 128
RW_LORA = RW_DECAY_LORA + RW_ICLR_LORA + RW_GATE_LORA
RW_COLS = 3 * RW_WIDTH + RW_LORA
RW_GN_EPS = 64e-5

SA_HEADS = 8
SA_KV_HEADS = 2
SA_HEAD_DIM = 64
SA_WIDTH = SA_HEADS * SA_HEAD_DIM
SA_KV_WIDTH = SA_KV_HEADS * SA_HEAD_DIM
IDX_HEADS = 4
IDX_DIM = 64
IDX_WIDTH = IDX_HEADS * IDX_DIM
TOPK_MAX = 256
ROPE_THETA = 500000.0
ROPE_DIM = SA_HEAD_DIM // 4
INDEX_SCALE = (IDX_DIM ** -0.5) * (IDX_HEADS ** -0.5)
SA_COLS = SA_WIDTH + 2 * SA_KV_WIDTH + IDX_WIDTH + IDX_DIM + IDX_HEADS

RET_HEADS = 4
RET_HEAD_DIM = 128
RET_WIDTH = RET_HEADS * RET_HEAD_DIM
RET_ROPE_BASE = 10000.0
RET_COLS = 4 * RET_WIDTH

N_BRANCH = 3
GATE_COLS = N_BRANCH * D_MODEL
IN_COLS = RW_COLS + SA_COLS + RET_COLS + GATE_COLS
D_FF = 4 * D_MODEL

LANES = 128
TILE_ELEMS = 64 * 8 * LANES

P_GATE = 0
P_RET = P_GATE + GATE_COLS
P_RKV = P_RET + RET_COLS
P_SAQ = P_RKV + 3 * RW_WIDTH
P_LORA = P_SAQ + SA_WIDTH
P_QI = P_LORA + RW_LORA
P_SAK = P_QI + IDX_WIDTH
P_SAV = P_SAK + SA_KV_WIDTH
P_KIWI = P_SAV + SA_KV_WIDTH
P_COLS = 8192
INT_MIN = -2 ** 31
NEG_BIG = -1e30
LOG2_E = 1.4426950408889634
VMEM_LIMIT = 56 * 1024 * 1024


def _bf(x):
    return x.astype(BF16)


def _dot(a, b):
    return jnp.dot(a, b, preferred_element_type=F32)


def _dot_nt(a, b):
    return lax.dot_general(a, b, (((1,), (1,)), ((), ())), preferred_element_type=F32)


def _dot_tn(a, b):
    return lax.dot_general(a, b, (((0,), (0,)), ((), ())), preferred_element_type=F32)


def _dot_split2(a_exact, x):
    hi = _bf(x)
    lo = _bf(x - hi.astype(F32))
    return _dot(a_exact, hi) + _dot(a_exact, lo)


def _params(sem):
    return pltpu.CompilerParams(dimension_semantics=sem, vmem_limit_bytes=VMEM_LIMIT)


def _in_proj_kernel(x_ref, g_ref, w_ref, o_ref, h_ref):
    @pl.when(pl.program_id(1) == 0)
    def _():
        x = x_ref[...]
        ms = jnp.mean(x * x, axis=-1, keepdims=True)
        h_ref[...] = _bf(x * lax.rsqrt(ms + EPS) * g_ref[...])

    o_ref[...] = _dot(h_ref[...], w_ref[0])


def _in_proj(x2d, g, w_all, layer):
    T = x2d.shape[0]
    tm = min(1024, T)
    tn = 1024
    return pl.pallas_call(
        _in_proj_kernel,
        grid=(T // tm, P_COLS // tn),
        in_specs=[pl.BlockSpec((tm, D_MODEL), lambda i, j: (i, 0)),
                  pl.BlockSpec((1, D_MODEL), lambda i, j: (0, 0)),
                  pl.BlockSpec((1, D_MODEL, tn), lambda i, j: (layer, 0, j))],
        out_specs=pl.BlockSpec((tm, tn), lambda i, j: (i, j)),
        out_shape=jax.ShapeDtypeStruct((T, P_COLS), F32),
        scratch_shapes=[pltpu.VMEM((tm, D_MODEL), BF16)],
        compiler_params=_params(("parallel", "arbitrary")),
        name="in_proj",
    )(x2d, g, w_all)


KIWI_COLS = IDX_DIM + IDX_HEADS
TAIL0 = RW_COLS + SA_COLS - KIWI_COLS
TAIL_W = (IN_COLS - TAIL0 + LANES - 1) // LANES * LANES


def _w_prep_kernel(w_ref, o_ref):
    x = w_ref[0]
    x = jnp.where(lax.broadcasted_iota(I32, x.shape, 1) < IN_COLS, x, 0.0)
    lane = lax.broadcasted_iota(I32, (x.shape[0], LANES), 1)

    def put(off, v):
        o_ref[0, :, off:off + v.shape[1]] = _bf(v)

    nslab = TAIL_W // LANES
    rolled = [pltpu.roll(x[:, TAIL0 + s * LANES:TAIL0 + (s + 1) * LANES], LANES - KIWI_COLS, 1)
              for s in range(nslab)]
    for s in range((RET_COLS + GATE_COLS) // LANES):
        v = jnp.where(lane < LANES - KIWI_COLS, rolled[s], rolled[s + 1])
        put((P_RET if s < RET_COLS // LANES else P_GATE - RET_COLS) + s * LANES, v)
    sa = RW_COLS
    put(P_RKV, x[:, 0:3 * RW_WIDTH])
    put(P_LORA, x[:, 3 * RW_WIDTH:RW_COLS])
    put(P_SAQ, x[:, sa:sa + SA_WIDTH])
    put(P_SAK, x[:, sa + SA_WIDTH:sa + SA_WIDTH + SA_KV_WIDTH])
    put(P_SAV, x[:, sa + SA_WIDTH + SA_KV_WIDTH:sa + SA_WIDTH + 2 * SA_KV_WIDTH])
    put(P_QI, x[:, sa + SA_WIDTH + 2 * SA_KV_WIDTH:TAIL0])
    put(P_KIWI, jnp.where(lane < KIWI_COLS, x[:, TAIL0:TAIL0 + LANES], 0.0))
    put(P_KIWI + LANES, jnp.zeros((x.shape[0], P_COLS - P_KIWI - LANES), F32))


def _w_prep(w_in):
    depth, d, _ = w_in.shape
    tm = 256
    return pl.pallas_call(
        _w_prep_kernel,
        grid=(depth, d // tm),
        in_specs=[pl.BlockSpec((1, tm, TAIL0 + TAIL_W), lambda l, i: (l, i, 0))],
        out_specs=pl.BlockSpec((1, tm, P_COLS), lambda l, i: (l, i, 0)),
        out_shape=jax.ShapeDtypeStruct((depth, d, P_COLS), BF16),
        compiler_params=_params(("parallel", "parallel")),
        name="w_prep",
    )(w_in)


def _rot_slab(x, c, s_lo, s_hi, shift):
    return x * c + pltpu.roll(x, LANES - shift, 1) * s_lo + pltpu.roll(x, shift, 1) * s_hi


def _sa_prep_kernel(q_ref, qi_ref, k_ref, kw_ref, c_ref, s1_ref, s2_ref, ck_ref, s1k_ref, s2k_ref,
                    qo_ref, qio_ref, ko_ref, kwo_ref):
    c, s1, s2 = c_ref[...], s1_ref[...], s2_ref[...]
    half = ROPE_DIM // 2
    scale = SA_HEAD_DIM ** -0.5 * LOG2_E
    for s in range(SA_WIDTH // LANES):
        sl = slice(s * LANES, (s + 1) * LANES)
        qo_ref[:, sl] = _rot_slab(q_ref[:, sl], c, s1, s2, half) * scale
    for s in range(IDX_WIDTH // LANES):
        sl = slice(s * LANES, (s + 1) * LANES)
        qio_ref[:, sl] = _rot_slab(qi_ref[:, sl], c, s1, s2, half)
    ko_ref[...] = _rot_slab(k_ref[...], c, s1, s2, half)
    kwo_ref[...] = _rot_slab(kw_ref[...], ck_ref[...], s1k_ref[...], s2k_ref[...], half)


def _sa_prep(P, L, tabs):
    T = P.shape[0]
    tm = min(512, L)
    nl = L // tm
    tab_spec = pl.BlockSpec((tm, LANES), lambda i: (i % nl, 0))
    return pl.pallas_call(
        _sa_prep_kernel,
        grid=(T // tm,),
        in_specs=[pl.BlockSpec((tm, SA_WIDTH), lambda i: (i, P_SAQ // SA_WIDTH)),
                  pl.BlockSpec((tm, IDX_WIDTH), lambda i: (i, P_QI // IDX_WIDTH)),
                  pl.BlockSpec((tm, LANES), lambda i: (i, P_SAK // LANES)),
                  pl.BlockSpec((tm, LANES), lambda i: (i, P_KIWI // LANES))] + [tab_spec] * 6,
        out_specs=[pl.BlockSpec((tm, SA_WIDTH), lambda i: (i, 0)),
                   pl.BlockSpec((tm, IDX_WIDTH), lambda i: (i, 0)),
                   pl.BlockSpec((tm, LANES), lambda i: (i, 0)),
                   pl.BlockSpec((tm, LANES), lambda i: (i, 0))],
        out_shape=[jax.ShapeDtypeStruct((T, SA_WIDTH), F32),
                   jax.ShapeDtypeStruct((T, IDX_WIDTH), F32),
                   jax.ShapeDtypeStruct((T, LANES), F32),
                   jax.ShapeDtypeStruct((T, LANES), F32)],
        compiler_params=_params(("parallel",)),
        name="sa_prep",
    )(P, P, P, P, *tabs)


def _sa_tables(pos):
    half = ROPE_DIM // 2
    freqs = 1.0 / (ROPE_THETA ** (jnp.arange(0, ROPE_DIM, 2, dtype=F32) / ROPE_DIM))
    ang = pos.astype(F32)[:, None] * freqs[None, :]
    cos, sin = jnp.cos(ang), jnp.sin(ang)
    n = pos.shape[0]
    pad = SA_HEAD_DIM - ROPE_DIM
    c_head = jnp.concatenate([cos, cos, jnp.ones((n, pad), F32)], axis=1)
    s1_head = jnp.concatenate([-sin, jnp.zeros((n, half + pad), F32)], axis=1)
    s2_head = jnp.concatenate([jnp.zeros((n, half), F32), sin, jnp.zeros((n, pad), F32)], axis=1)
    one, zero = jnp.ones((n, SA_HEAD_DIM), F32), jnp.zeros((n, SA_HEAD_DIM), F32)
    two = lambda t: jnp.concatenate([t, t], axis=1)
    return (two(c_head), two(s1_head), two(s2_head),
            jnp.concatenate([c_head, one], axis=1), jnp.concatenate([s1_head, zero], axis=1),
            jnp.concatenate([s2_head, zero], axis=1))


def _softplus(u):
    return jnp.maximum(u, 0.0) + jnp.log(1.0 + jnp.exp(-jnp.abs(u)))


def _rwkv_kernel(r_ref, k_ref, v_ref, lo_ref, shr_ref, shk_ref, shv_ref, shlo_ref, s0_ref,
                 mur_ref, muk_ref, muv_ref, mulo_ref, w0_ref, w2_ref, a0_ref, a2_ref, g2_ref,
                 kk_ref, ka_ref, rk_ref, lnx_ref, e_ref, tin_ref, tsuf_ref,
                 y_ref, so_ref,
                 cr_ref, ck_ref, cv_ref, clo_ref, s_ref,
                 kap_ref, rt_ref, bh_ref, kh_ref, bt_ref, kt_ref, vv_ref, gc_ref, yy_ref, rr_ref,
                 ac_ref, cc_ref):
    c = pl.program_id(1)
    TT = r_ref.shape[0]
    nch = TT // CHUNK
    N = RW_HEAD_DIM

    @pl.when(c == 0)
    def _():
        cr_ref[0:1, :] = shr_ref[0]
        ck_ref[0:1, :] = shk_ref[0]
        cv_ref[0:1, :] = shv_ref[0]
        clo_ref[0:1, :] = shlo_ref[0]
        s_ref[...] = s0_ref[0]

    def lerp(p_ref, carry_ref, mu_ref):
        p = p_ref[...]
        rolled = pltpu.roll(p, 1, 0)
        row0 = lax.broadcasted_iota(I32, p.shape, 0) == 0
        prev = jnp.where(row0, carry_ref[0:1, :], rolled)
        carry_ref[0:1, :] = p[TT - 1:TT, :]
        return p + (prev - p) * mu_ref[...]

    xr = lerp(r_ref, cr_ref, mur_ref)
    xk = lerp(k_ref, ck_ref, muk_ref)
    xv = lerp(v_ref, cv_ref, muv_ref)
    xlo = lerp(lo_ref, clo_ref, mulo_ref)
    xw = xlo[:, :RW_DECAY_LORA]
    xa = xlo[:, RW_DECAY_LORA:RW_DECAY_LORA + RW_ICLR_LORA]
    xg = xlo[:, RW_DECAY_LORA + RW_ICLR_LORA:]

    z = w0_ref[...] + _dot(_bf(jnp.tanh(xw)), w2_ref[...])
    w = -_softplus(-z) - 0.5
    ld = -jnp.exp(w)
    a = jax.nn.sigmoid(a0_ref[...] + _dot(_bf(xa), a2_ref[...]))
    gate = _dot(_bf(jax.nn.sigmoid(xg)), g2_ref[...])
    e_blk = e_ref[...]
    kk = xk * kk_ref[...]
    kk = kk / jnp.maximum(jnp.sqrt(_dot(_bf(kk * kk), e_blk)), 1e-12)
    k2 = xk * (1.0 + (a - 1.0) * ka_ref[...])
    bb = kk * a
    bonus = _dot(_bf(xr * k2 * rk_ref[...]), e_blk) * xv

    lin = _dot_split2(tin_ref[...], ld)
    lsuf = _dot_split2(tsuf_ref[...], ld)
    e_in = jnp.exp(lin)
    e_ninv = jnp.exp(-lin)
    e_suf = jnp.exp(lsuf)
    kap_ref[...] = kk * jnp.exp(lin - ld)
    rt_ref[...] = xr * e_in
    bh_ref[...] = bb * e_ninv
    kh_ref[...] = k2 * e_ninv
    bt_ref[...] = bb * e_suf
    kt_ref[...] = k2 * e_suf
    vv_ref[...] = xv
    gc_ref[...] = jnp.exp(lin + lsuf)

    ri = lax.broadcasted_iota(I32, (CHUNK, CHUNK), 0)
    ci = lax.broadcasted_iota(I32, (CHUNK, CHUNK), 1)
    strict = ri > ci
    incl = ri >= ci
    eye = (ri == ci).astype(F32)
    heads = range(RW_HEADS)
    hsl = [slice(h * N, (h + 1) * N) for h in heads]

    per_it = 2 if nch % 2 == 0 else 1

    def coef_body(it, carry):
        chs = [it * per_it + t for t in range(per_it)]
        pairs = [(t, h) for t in range(per_it) for h in heads]
        rows = [pl.ds(pl.multiple_of(ch * CHUNK, CHUNK), CHUNK) for ch in chs]
        ld = lambda ref, p: ref[rows[p[0]], hsl[p[1]]]
        kap = [ld(kap_ref, p) for p in pairs]
        rt = [ld(rt_ref, p) for p in pairs]
        vh = [ld(vv_ref, p) for p in pairs]
        idx = range(len(pairs))
        gmat = [_dot_nt(_bf(jnp.concatenate([kap[n], rt[n]], axis=0)),
                        _bf(jnp.concatenate([ld(bh_ref, pairs[n]), ld(kh_ref, pairs[n])], axis=0)))
                for n in idx]
        n_ab = [jnp.where(strict, g[:CHUNK, :CHUNK], 0.0) for g in gmat]
        m_rb = [jnp.where(incl, g[CHUNK:, :CHUNK], 0.0) for g in gmat]
        m_v = [_bf(jnp.concatenate([jnp.where(strict, g[:CHUNK, CHUNK:], 0.0),
                                    jnp.where(incl, g[CHUNK:, CHUNK:], 0.0)], axis=0)) for g in gmat]
        mv = [_dot(m_v[n], _bf(vh[n])) for n in idx]
        x_inv = [eye - t for t in n_ab]
        pw = n_ab
        for _ in range(5):
            pwb = [_bf(p) for p in pw]
            pw = [_dot(p, p) for p in pwb]
            x_inv = [x + _dot(_bf(x), _bf(p)) for x, p in zip(x_inv, pw)]
        w = [_dot(_bf(x_inv[n]), _bf(jnp.concatenate([kap[n], mv[n][:CHUNK]], axis=1))) for n in idx]
        wb = [_bf(t) for t in w]
        ry = [jnp.concatenate([rt[n], mv[n][CHUNK:]], axis=1) - _dot(_bf(m_rb[n]), wb[n]) for n in idx]
        dmat = [_dot_tn(wb[n], _bf(ld(bt_ref, pairs[n]))) for n in idx]
        vtk = [_dot_tn(_bf(vh[n]), _bf(ld(kt_ref, pairs[n]))) for n in idx]
        for n, (t, h) in enumerate(pairs):
            rr_ref[rows[t], hsl[h]] = ry[n][:, :N]
            yy_ref[rows[t], hsl[h]] = ry[n][:, N:]
            ac_ref[chs[t], h] = -dmat[n][:N]
            cc_ref[chs[t], h] = vtk[n] - dmat[n][N:]
        return carry

    lax.fori_loop(0, nch // per_it, coef_body, 0)

    def state_body(ch, carry):
        rows = pl.ds(pl.multiple_of(ch * CHUNK, CHUNK), CHUNK)
        s_old = [s_ref[h] for h in heads]
        sb = [_bf(t) for t in s_old]
        s_new = [_dot(sb[h], _bf(ac_ref[ch, h])) for h in heads]
        y_c = [_dot_nt(_bf(rr_ref[rows, hsl[h]]), sb[h]) for h in heads]
        for h in heads:
            gch = gc_ref[pl.ds(pl.multiple_of(ch * CHUNK, CHUNK), 1), hsl[h]]
            s_ref[h] = s_old[h] * gch + s_new[h] + cc_ref[ch, h]
            yy_ref[rows, hsl[h]] = yy_ref[rows, hsl[h]] + y_c[h]
        return carry

    lax.fori_loop(0, nch, state_body, 0)

    y = yy_ref[...]
    mean = _dot(_bf(y), e_blk) * (1.0 / N)
    d = y - mean
    var = _dot(_bf(d * d), e_blk) * (1.0 / N)
    yn = d * lax.rsqrt(var + RW_GN_EPS) * lnx_ref[...]
    y_ref[...] = (yn + bonus) * gate

    @pl.when(c == pl.num_programs(1) - 1)
    def _():
        so_ref[0] = s_ref[...]


def _rwkv(P, B, L, shift_prev, s0, lp):
    T = P.shape[0]
    TT = min(256, L)
    nt = L // TT
    W = RW_WIDTH
    row = lambda t: t.reshape(1, -1).astype(F32)
    mu = lp['rwkv_mu']
    sh = shift_prev.astype(F32)
    pieces = lambda t: (t[..., 0:W], t[..., W:2 * W], t[..., 2 * W:3 * W], t[..., 3 * W:])
    mu_r, mu_k, mu_v, mu_lo = [row(t) for t in pieces(mu)]
    sh_r, sh_k, sh_v, sh_lo = [t.reshape(B, 1, -1) for t in pieces(sh)]
    hid = jnp.arange(W) // RW_HEAD_DIM
    e_blk = (hid[:, None] == hid[None, :]).astype(BF16)
    ti = jnp.arange(TT)
    same = (ti[:, None] // CHUNK) == (ti[None, :] // CHUNK)
    tri_in = (same & (ti[None, :] <= ti[:, None])).astype(BF16)
    tri_suf = (same & (ti[None, :] > ti[:, None])).astype(BF16)

    tok = lambda w, blk: pl.BlockSpec((TT, w), lambda b, c: (b * nt + c, blk))
    full = lambda shape: pl.BlockSpec(shape, lambda b, c: (0,) * len(shape))
    shs = lambda w: pl.BlockSpec((1, 1, w), lambda b, c: (b, 0, 0))
    st_spec = pl.BlockSpec((1, RW_HEADS, RW_HEAD_DIM, RW_HEAD_DIM), lambda b, c: (b, 0, 0, 0))
    big = lambda: pltpu.VMEM((TT, W), F32)
    y, s_out = pl.pallas_call(
        _rwkv_kernel,
        grid=(B, nt),
        in_specs=[tok(W, P_RKV // W), tok(W, P_RKV // W + 1), tok(W, P_RKV // W + 2),
                  tok(RW_LORA, P_LORA // RW_LORA),
                  shs(W), shs(W), shs(W), shs(RW_LORA), st_spec,
                  full((1, W)), full((1, W)), full((1, W)), full((1, RW_LORA)),
                  full((1, W)), full((RW_DECAY_LORA, W)), full((1, W)), full((RW_ICLR_LORA, W)),
                  full((RW_GATE_LORA, W)), full((1, W)), full((1, W)), full((1, W)), full((1, W)),
                  full((W, W)), full((TT, TT)), full((TT, TT))],
        out_specs=[pl.BlockSpec((TT, W), lambda b, c: (b * nt + c, 0)), st_spec],
        out_shape=[jax.ShapeDtypeStruct((T, W), F32),
                   jax.ShapeDtypeStruct((B, RW_HEADS, RW_HEAD_DIM, RW_HEAD_DIM), F32)],
        scratch_shapes=[pltpu.VMEM((8, W), F32), pltpu.VMEM((8, W), F32), pltpu.VMEM((8, W), F32),
                        pltpu.VMEM((8, RW_LORA), F32),
                        pltpu.VMEM((RW_HEADS, RW_HEAD_DIM, RW_HEAD_DIM), F32)] + [big() for _ in range(10)]
        + [pltpu.VMEM((TT // CHUNK, RW_HEADS, RW_HEAD_DIM, RW_HEAD_DIM), F32) for _ in range(2)],
        compiler_params=_params(("parallel", "arbitrary")),
        name="rwkv",
    )(P, P, P, P, sh_r, sh_k, sh_v, sh_lo, s0.astype(F32),
      mu_r, mu_k, mu_v, mu_lo, row(lp['rwkv_w0']), _bf(lp['rwkv_w2']), row(lp['rwkv_a0']),
      _bf(lp['rwkv_a2']), _bf(lp['rwkv_g2']), row(lp['rwkv_k_k']), row(lp['rwkv_k_a']),
      row(lp['rwkv_r_k']), row(lp['rwkv_lnx_g']), e_blk, tri_in, tri_suf)
    return y, s_out


ONES_ROWS = 16
BOUND_SLACK = 1.05
SAFE_BOUND = 60.0


def _dsa_kernel(q_ref, qi_ref, kw_ref, k_ref, v_ref, ki_ref, y_ref,
                kb_ref, vt_ref, kib_ref, kmax_ref, keys_ref, hi_ref, lo_ref, bias_ref, logit_ref, pstar_ref,
                red_ref, *, past, lk_real, topk, kc):
    i = pl.program_id(1)
    qb = q_ref.shape[1]
    lk = k_ref.shape[1]
    kf = float(topk)
    HD = SA_HEAD_DIM

    ones_sq = jnp.ones((LANES, LANES), BF16)

    def head_slab(x, odd):
        lane = lax.broadcasted_iota(I32, x.shape, 1)
        return jnp.where(lane < HD, pltpu.roll(x, HD, 1) if odd else x, 0.0)

    @pl.when(i == 0)
    def _():
        kib_ref[...] = _bf(ki_ref[0])
        k = k_ref[0]
        lane = lax.broadcasted_iota(I32, k.shape, 1)
        for c in range(SA_KV_HEADS):
            kc_b = _bf(head_slab(k, c == 1))
            kb_ref[c] = jnp.where(lane == HD, jnp.ones_like(kc_b), kc_b)
            kf32 = kc_b.astype(F32)
            n2 = _dot(_bf(kf32 * kf32), ones_sq)
            kmax_ref[c] = jnp.broadcast_to(jnp.max(n2, axis=0, keepdims=True), (8, LANES))
        vt = v_ref[0].T
        for c in range(SA_KV_HEADS):
            vt_ref[c, 0:HD, :] = _bf(vt[c * HD:(c + 1) * HD, :])
            vt_ref[c, HD:HD + ONES_ROWS, :] = jnp.ones((ONES_ROWS, lk), BF16)

    qpos = past + i * qb + lax.broadcasted_iota(I32, (1, qb), 1)
    limit = jnp.minimum((qpos // CHUNK + 1) * CHUNK, lk_real)
    hi = jnp.minimum(past + (i + 1) * qb, lk_real)
    nk = lax.shift_right_logical(hi + (2 * kc - 1), kc.bit_length())
    sub_idx = lax.broadcasted_iota(I32, (kc, qb), 0)
    kwt = kw_ref[0].T
    wi_h = [kwt[IDX_DIM + h:IDX_DIM + h + 1, :] * INDEX_SCALE for h in range(IDX_HEADS)]
    qi = qi_ref[0]
    qi_h = [_bf(qi[:, h * IDX_DIM:(h + 1) * IDX_DIM]) for h in range(IDX_HEADS)]

    def chunk(j):
        return pl.ds(pl.multiple_of(j * kc, kc), kc)

    def pair(t):
        return (2 * t, 2 * t + 1)

    def score_body(t, carry):
        js = pair(t)
        dots = [[_dot_nt(kib_ref[chunk(j), 0:IDX_DIM], qi_h[h]) for h in range(IDX_HEADS)] for j in js]
        for j, d in zip(js, dots):
            s = jnp.zeros((kc, qb), F32)
            for h in range(IDX_HEADS):
                s = s + jnp.maximum(d[h], 0.0) * wi_h[h]
            s = jnp.where(s == 0.0, 0.0, s)
            bits = lax.bitcast_convert_type(s, I32)
            key = bits ^ ((bits >> 31) & 0x7FFFFFFF)
            key = jnp.where((sub_idx + j * kc) < limit, key, INT_MIN)
            keys_ref[chunk(j), :] = key
            hi_ref[chunk(j), :] = (key >> 16).astype(I16)
        return carry

    lax.fori_loop(0, nk, score_body, 0)

    def fold(m, rows):
        parts = [m[r * rows:(r + 1) * rows] for r in range(kc // rows)]
        while len(parts) > 1:
            parts = [a + b for a, b in zip(parts[0::2], parts[1::2])]
        return parts[0]

    max_trips = lk // (2 * kc)

    def short_trips(body, carry):
        base = jnp.int32(0)
        p = 1 << (max_trips.bit_length() - 1)
        while p:
            take = (nk & p) != 0

            def run(c, base=base, p=p):
                for s in range(p):
                    c = body(base + s, c)
                return c

            carry = lax.cond(take, run, lambda c: c, carry)
            base = base + jnp.where(take, p, 0)
            p //= 2
        return carry

    def col_total(part):
        rows = part.shape[0]
        red_ref[0:rows, :] = part
        parts = [red_ref[r:r + 1, :] for r in range(rows)]
        while len(parts) > 1:
            parts = [a + b for a, b in zip(parts[0::2], parts[1::2])]
        return parts[0]

    def count(pred):
        def body(t, acc):
            for j in pair(t):
                acc = acc + fold(jnp.where(pred(keys_ref[chunk(j), :], j), 1.0, 0.0), 8)
            return acc

        return col_total(short_trips(body, jnp.zeros((8, qb), F32)))

    def count16(ref, pred):
        def body(t, acc):
            for j in pair(t):
                acc = acc + fold(jnp.where(pred(ref[chunk(j), :]), jnp.int16(1), jnp.int16(0)), 16)
            return acc

        acc = short_trips(body, jnp.zeros((16, qb), I16))
        return col_total(acc.astype(F32))

    def kth_bits16(ref, need):
        def bit_body(it, u):
            cand = u | lax.shift_left(jnp.int32(1), 15 - it)
            thr = (cand - 32768).astype(I16)
            return jnp.where(count16(ref, lambda x: x >= thr) >= need, cand, u)

        return lax.fori_loop(0, 16, bit_body, jnp.zeros((1, qb), I32))

    t_hi = kth_bits16(hi_ref, kf) - 32768
    t_hi16 = t_hi.astype(I16)
    need_lo = kf - count16(hi_ref, lambda x: x > t_hi16)

    def lo_body(t, carry):
        for j in pair(t):
            lo = ((keys_ref[chunk(j), :] & 0xFFFF) - 32768).astype(I16)
            lo_ref[chunk(j), :] = jnp.where(hi_ref[chunk(j), :] == t_hi16, lo, jnp.int16(-32768))
        return carry

    short_trips(lo_body, jnp.int32(0))
    tau = t_hi * 65536 + kth_bits16(lo_ref, need_lo)
    cnt_ge = count(lambda kj, j: kj >= tau)
    cnt_gt = count(lambda kj, j: kj > tau)
    need = kf - cnt_gt
    excess = jnp.logical_and(cnt_ge > kf, tau != INT_MIN)
    p_default = jnp.where(tau == INT_MIN, -1, 2 ** 30).astype(I32)
    pstar_ref[...] = jnp.broadcast_to(p_default, pstar_ref.shape)

    @pl.when(jnp.max(jnp.where(excess, 1.0, 0.0)) > 0.0)
    def _():
        nbits = max(1, int(lk - 1).bit_length())

        def idx_body(it, p):
            cand = p | lax.shift_left(jnp.int32(1), nbits - 1 - it)
            g = count(lambda kj, j: jnp.logical_and(kj == tau, (sub_idx + j * kc) < cand))
            return jnp.where(g < need, cand, p)

        p = lax.fori_loop(0, nbits, idx_body, jnp.zeros((1, qb), I32))
        pstar_ref[...] = jnp.broadcast_to(jnp.where(excess, p, p_default), pstar_ref.shape)

    pstar = pstar_ref[0:1, :]

    def bias_body(t, carry):
        for j in pair(t):
            kj = keys_ref[chunk(j), :]
            sel = jnp.logical_or(kj > tau, jnp.logical_and(kj == tau, (sub_idx + j * kc) <= pstar))
            bias_ref[chunk(j), :] = jnp.where(sel, 0.0, NEG_BIG)
        return carry

    short_trips(bias_body, jnp.int32(0))

    q = q_ref[0]
    group = SA_HEADS // SA_KV_HEADS
    lane_q = lax.broadcasted_iota(I32, (qb, LANES), 1)
    q_heads, bounds = [], []
    for h in range(SA_HEADS):
        qh = _bf(head_slab(q[:, (h // 2) * LANES:(h // 2 + 1) * LANES], h % 2 == 1))
        qf = qh.astype(F32)
        qn2 = _dot(_bf(qf * qf), ones_sq)
        q_heads.append(qh)
        bounds.append(jnp.sqrt(qn2 * kmax_ref[h // group, 0:1, :]) * BOUND_SLACK)
    worst = bounds[0]
    for b in bounds[1:]:
        worst = jnp.maximum(worst, b)
    safe = jnp.max(worst) <= SAFE_BOUND

    zero_acc = jnp.zeros((HD + ONES_ROWS, qb), F32)

    def vt_chunk(c, j):
        return vt_ref[c, :, pl.ds(pl.multiple_of(j * kc, kc), kc)]

    @pl.when(safe)
    def _():
        hs = range(SA_HEADS)
        qa = [jnp.where(lane_q == HD, _bf(-bounds[h]), q_heads[h]) for h in hs]

        def body(t, accs):
            js = pair(t)
            logit = [[_dot_nt(kb_ref[h // group, chunk(j), :], qa[h]) for h in hs] for j in js]
            p = [[_bf(jnp.exp2(lg + bias_ref[chunk(j), :])) for lg in row]
                 for j, row in zip(js, logit)]
            pv = [[_dot(vt_chunk(h // group, j), row[h]) for h in hs] for j, row in zip(js, p)]
            return tuple(accs[h] + pv[0][h] + pv[1][h] for h in hs)

        accs = lax.fori_loop(0, nk, body, (zero_acc,) * SA_HEADS)
        for h in hs:
            y_ref[0, h * HD:(h + 1) * HD, :] = accs[h][:HD] / accs[h][HD:HD + 1]

    @pl.when(jnp.logical_not(safe))
    def _():
        for h in range(SA_HEADS):
            c = h // group

            def max_body(t, m):
                for j in pair(t):
                    logit = _dot_nt(kb_ref[c, chunk(j), :], q_heads[h]) + bias_ref[chunk(j), :]
                    logit_ref[chunk(j), :] = logit
                    m = jnp.maximum(m, jnp.max(logit, axis=0, keepdims=True))
                return m

            m = lax.fori_loop(0, nk, max_body, jnp.full((1, qb), NEG_BIG, F32))

            def sum_body(t, acc):
                for j in pair(t):
                    acc = acc + _dot(vt_chunk(c, j), _bf(jnp.exp2(logit_ref[chunk(j), :] - m)))
                return acc

            acc = lax.fori_loop(0, nk, sum_body, zero_acc)
            y_ref[0, h * HD:(h + 1) * HD, :] = acc[:HD] / acc[HD:HD + 1]


def _dsa_call(q3, qi3, kw3, k_all, v_all, ki_all, v_blk, *, lk, past, lk_real, topk, qb, kc):
    B, n_q = q3.shape[:2]
    kern = functools.partial(_dsa_kernel, past=past, lk_real=lk_real, topk=topk, kc=kc)
    return pl.pallas_call(
        kern,
        grid=(B, n_q // qb),
        in_specs=[pl.BlockSpec((1, qb, SA_WIDTH), lambda b, i: (b, i, 0)),
                  pl.BlockSpec((1, qb, IDX_WIDTH), lambda b, i: (b, i, 0)),
                  pl.BlockSpec((1, qb, LANES), lambda b, i: (b, i, 0)),
                  pl.BlockSpec((1, lk, LANES), lambda b, i: (b, 0, 0)),
                  pl.BlockSpec((1, lk, LANES), lambda b, i: (b, 0, v_blk)),
                  pl.BlockSpec((1, lk, LANES), lambda b, i: (b, 0, 0))],
        out_specs=pl.BlockSpec((1, SA_WIDTH, qb), lambda b, i: (b, 0, i)),
        out_shape=jax.ShapeDtypeStruct((B, SA_WIDTH, n_q), F32),
        scratch_shapes=[pltpu.VMEM((SA_KV_HEADS, lk, LANES), BF16),
                        pltpu.VMEM((SA_KV_HEADS, SA_HEAD_DIM + ONES_ROWS, lk), BF16),
                        pltpu.VMEM((lk, LANES), BF16),
                        pltpu.VMEM((SA_KV_HEADS, 8, LANES), F32),
                        pltpu.VMEM((lk, qb), I32), pltpu.VMEM((lk, qb), I16), pltpu.VMEM((lk, qb), I16),
                        pltpu.VMEM((lk, qb), F32), pltpu.VMEM((lk, qb), F32),
                        pltpu.VMEM((8, qb), I32), pltpu.VMEM((16, qb), F32)],
        compiler_params=_params(("parallel", "arbitrary")),
        name="dsa",
    )(q3, qi3, kw3, k_all, v_all, ki_all)


def _round_up(x, m):
    return (x + m - 1) // m * m


def _dsa(P, B, L, past, q_rot, qi_rot, k_rot, kw_rot, k_past, v_past, ik_past):
    lk_real = past + L
    topk = min(TOPK_MAX, lk_real // 4)
    qb = 2 * LANES if (past == 0 and L % (2 * LANES) == 0) else LANES
    lq = _round_up(L, qb)
    qpad = lambda t: t if lq == L else jnp.pad(t, ((0, 0), (0, lq - L), (0, 0)))
    q3 = qpad(q_rot.reshape(B, L, SA_WIDTH))
    qi3 = qpad(qi_rot.reshape(B, L, IDX_WIDTH))
    kw3 = kw_rot.reshape(B, L, LANES)
    k3 = k_rot.reshape(B, L, LANES)
    common = dict(past=past, lk_real=lk_real, topk=topk, qb=qb)
    kc = TILE_ELEMS // qb // 2 if qb == LANES else TILE_ELEMS // qb
    if past == 0:
        while L % (2 * kc):
            kc //= 2
        P3 = P.reshape(B, L, P_COLS)
        return _dsa_call(q3, qi3, kw3, k3, P3, kw3, P_SAV // LANES, lk=L, kc=kc, **common)
    lk = _round_up(lk_real, 2 * kc)
    zpad = jnp.zeros((B, lk - lk_real, LANES), F32)
    v_new = P.reshape(B, L, P_COLS)[:, :, P_SAV:P_SAV + LANES]
    ik_p = jnp.concatenate([ik_past.astype(F32), jnp.zeros((B, past, LANES - IDX_DIM), F32)], axis=2)
    k_all = jnp.concatenate([k_past.reshape(B, past, LANES).astype(F32), k3, zpad], axis=1)
    v_all = jnp.concatenate([v_past.reshape(B, past, LANES).astype(F32), v_new, zpad], axis=1)
    ki_all = jnp.concatenate([ik_p, kw3, zpad], axis=1)
    y = _dsa_call(q3, qi3, qpad(kw3), k_all, v_all, ki_all, 0, lk=lk, kc=kc, **common)
    return y[:, :, :L]


def _ret_kernel(q_ref, k_ref, v_ref, g_ref, cos_ref, sin_ref, dm_ref, qd_ref, kd_ref, gc_ref, s0_ref,
                y_ref, so_ref, s_ref):
    c = pl.program_id(1)
    D = RET_HEAD_DIM
    cs = dm_ref.shape[1]
    nch = q_ref.shape[0] // cs
    heads = range(RET_HEADS)
    hsl = [slice(h * D, (h + 1) * D) for h in heads]
    rows = [slice(ch * cs, (ch + 1) * cs) for ch in range(nch)]
    prob = [(ch, h) for ch in range(nch) for h in heads]

    @pl.when(c == 0)
    def _():
        s_ref[...] = s0_ref[0]

    cos, sin = cos_ref[...], sin_ref[...]
    qb_, kb_, kd_, vb_ = [], [], [], []
    for h in heads:
        q = q_ref[:, hsl[h]]
        k = k_ref[:, hsl[h]]
        k = (k * cos + pltpu.roll(k, D // 2, 1) * sin) * (D ** -0.5)
        qb_.append(_bf(q * cos + pltpu.roll(q, D // 2, 1) * sin))
        kb_.append(_bf(k))
        kd_.append([_bf(k[r] * kd_ref[:, hsl[h]]) for r in rows])
        vb_.append(_bf(v_ref[:, hsl[h]]))
    scores = {(ch, h): _dot_nt(qb_[h][rows[ch]], kb_[h][rows[ch]]) * dm_ref[h] for ch, h in prob}
    ktv = {(ch, h): _dot_tn(kd_[h][ch], vb_[h][rows[ch]]) for ch, h in prob}
    intra = {(ch, h): _dot(_bf(scores[ch, h]), vb_[h][rows[ch]]) for ch, h in prob}
    s_at = {}
    for h in heads:
        s = s_ref[h]
        for ch in range(nch):
            s_at[ch, h] = _bf(s)
            s = s * gc_ref[:, hsl[h]] + ktv[ch, h]
        s_ref[h] = s
    cross = {(ch, h): _dot(qb_[h][rows[ch]], s_at[ch, h]) for ch, h in prob}
    for ch in range(nch):
        outs = []
        for h in heads:
            o = intra[ch, h] + cross[ch, h] * qd_ref[:, hsl[h]]
            o = o * lax.rsqrt(jnp.mean(o * o, axis=-1, keepdims=True) + EPS)
            outs.append(jax.nn.silu(g_ref[rows[ch], hsl[h]]) * o)
        y_ref[rows[ch], :] = jnp.concatenate(outs, axis=1)

    @pl.when(c == pl.num_programs(1) - 1)
    def _():
        so_ref[0] = s_ref[...]


def _retention(P, B, L, pos, s0):
    T = P.shape[0]
    c = min(CHUNK, L)
    nc = L // c
    D = RET_HEAD_DIM
    freqs = 1.0 / (RET_ROPE_BASE ** jnp.linspace(0.0, 1.0, D // 2, dtype=F32))
    ang = pos.astype(F32)[:, None] * freqs[None, :]
    cos = jnp.concatenate([jnp.cos(ang)] * 2, axis=1)
    sin = jnp.concatenate([-jnp.sin(ang), jnp.sin(ang)], axis=1)
    log_gamma = jnp.log1p(-jnp.exp2(-5.0 - jnp.arange(RET_HEADS, dtype=F32)))
    idx = jnp.arange(c, dtype=F32)
    dmask = jnp.exp(jnp.abs(idx[:, None] - idx[None, :])[None] * log_gamma[:, None, None])
    lanes = lambda t: jnp.repeat(t, D, axis=1)
    qdec = lanes(jnp.exp((idx[:, None] + 1.0) * log_gamma[None, :]))
    kdec = lanes(jnp.exp((c - 1.0 - idx)[:, None] * log_gamma[None, :]))
    gchunk = lanes(jnp.exp(c * log_gamma)[None, :])

    W = RET_WIDTH
    tt = min(4 * c, L)
    nt = L // tt
    tok = lambda blk: pl.BlockSpec((tt, W), lambda b, i: (b * nt + i, blk))
    full = lambda shape: pl.BlockSpec(shape, lambda b, i: (0,) * len(shape))
    st_spec = pl.BlockSpec((1, RET_HEADS, D, D), lambda b, i: (b, 0, 0, 0))
    y, s_out = pl.pallas_call(
        _ret_kernel,
        grid=(B, nt),
        in_specs=[tok(P_RET // W), tok(P_RET // W + 1), tok(P_RET // W + 2), tok(P_RET // W + 3),
                  pl.BlockSpec((tt, D), lambda b, i: (i, 0)), pl.BlockSpec((tt, D), lambda b, i: (i, 0)),
                  full((RET_HEADS, c, c)), full((c, W)), full((c, W)), full((1, W)), st_spec],
        out_specs=[pl.BlockSpec((tt, W), lambda b, i: (b * nt + i, 0)), st_spec],
        out_shape=[jax.ShapeDtypeStruct((T, W), F32), jax.ShapeDtypeStruct((B, RET_HEADS, D, D), F32)],
        scratch_shapes=[pltpu.VMEM((RET_HEADS, D, D), F32)],
        compiler_params=_params(("parallel", "arbitrary")),
        name="retention",
    )(P, P, P, P, cos, sin, dmask, qdec, kdec, gchunk, s0.astype(F32))
    return y, s_out


def _merge_kernel(x_ref, g0_ref, g1_ref, g2_ref, yr_ref, ys_ref, yt_ref, wr_ref, ws_ref, wt_ref, wo_ref,
                  o_ref):
    m = (jax.nn.sigmoid(g0_ref[...]) * _dot(_bf(yr_ref[...]), wr_ref[...])
         + jax.nn.sigmoid(g1_ref[...]) * _dot_tn(_bf(ys_ref[0]), ws_ref[...])
         + jax.nn.sigmoid(g2_ref[...]) * _dot(_bf(yt_ref[...]), wt_ref[...]))
    o_ref[...] = x_ref[...] + _dot(_bf(m), wo_ref[...])


def _merge(x2d, P, y_rw, y_sa_t, y_ret, w_rw, w_sa, w_ret, w_o):
    T = x2d.shape[0]
    L = y_sa_t.shape[2]
    tm = min(512, L)
    nl = L // tm
    D = D_MODEL
    tok = lambda w, blk: pl.BlockSpec((tm, w), lambda i: (i, blk))
    full = lambda shape: pl.BlockSpec(shape, lambda i: (0, 0))
    return pl.pallas_call(
        _merge_kernel,
        grid=(T // tm,),
        in_specs=[tok(D, 0), tok(D, 0), tok(D, 1), tok(D, 2), tok(RW_WIDTH, 0),
                  pl.BlockSpec((1, SA_WIDTH, tm), lambda i: (i // nl, 0, i % nl)),
                  tok(RET_WIDTH, 0), full((RW_WIDTH, D)), full((SA_WIDTH, D)), full((RET_WIDTH, D)),
                  full((D, D))],
        out_specs=tok(D, 0),
        out_shape=jax.ShapeDtypeStruct((T, D), F32),
        compiler_params=_params(("parallel",)),
        name="merge",
    )(x2d, P, P, P, y_rw, y_sa_t, y_ret, w_rw, w_sa, w_ret, w_o)


def _mlp_kernel(x_ref, g_ref, wu_ref, wd_ref, gf_ref, o_ref, h_ref, acc_ref, *, final_norm):
    j = pl.program_id(1)

    @pl.when(j == 0)
    def _():
        x = x_ref[...]
        ms = jnp.mean(x * x, axis=-1, keepdims=True)
        h_ref[...] = _bf(x * lax.rsqrt(ms + EPS) * g_ref[...])
        acc_ref[...] = jnp.zeros_like(acc_ref)

    u = jnp.maximum(_dot(h_ref[...], wu_ref[...]), 0.0)
    acc_ref[...] += _dot(_bf(u * u), wd_ref[...])

    @pl.when(j == pl.num_programs(1) - 1)
    def _():
        xn = x_ref[...] + acc_ref[...]
        if final_norm:
            ms = jnp.mean(xn * xn, axis=-1, keepdims=True)
            xn = xn * lax.rsqrt(ms + EPS) * gf_ref[...]
        o_ref[...] = xn


def _mlp(x2d, g, w_up, w_down, g_final, final_norm):
    T = x2d.shape[0]
    tm = min(1024, T)
    tf = 1024
    D = D_MODEL
    return pl.pallas_call(
        functools.partial(_mlp_kernel, final_norm=final_norm),
        grid=(T // tm, D_FF // tf),
        in_specs=[pl.BlockSpec((tm, D), lambda i, j: (i, 0)),
                  pl.BlockSpec((1, D), lambda i, j: (0, 0)),
                  pl.BlockSpec((D, tf), lambda i, j: (0, j)),
                  pl.BlockSpec((tf, D), lambda i, j: (j, 0)),
                  pl.BlockSpec((1, D), lambda i, j: (0, 0))],
        out_specs=pl.BlockSpec((tm, D), lambda i, j: (i, 0)),
        out_shape=jax.ShapeDtypeStruct((T, D), F32),
        scratch_shapes=[pltpu.VMEM((tm, D), BF16), pltpu.VMEM((tm, D), F32)],
        compiler_params=_params(("parallel", "arbitrary")),
        name="mlp",
    )(x2d, g, w_up, w_down, g_final)


def _layer(x2d, B, L, past, caches, lp, wts, g_final, final_norm):
    k_past, v_past, ik_past, s_rw, shift_rw, s_ret = caches
    pos = past + jnp.arange(L, dtype=jnp.int32)
    row = lambda t: t.reshape(1, -1).astype(F32)
    P = _in_proj(x2d, row(lp['norm1_g']), wts['w_in'], wts['layer'])
    q_rot, qi_rot, k_rot, kw_rot = _sa_prep(P, L, _sa_tables(pos))
    y_rw, s_rw_new = _rwkv(P, B, L, shift_rw, s_rw, lp)
    y_sa = _dsa(P, B, L, past, q_rot, qi_rot, k_rot, kw_rot, k_past, v_past, ik_past)
    y_ret, s_ret_new = _retention(P, B, L, pos, s_ret)
    x2d = _merge(x2d, P, y_rw, y_sa, y_ret, wts['w_br_rwkv'], wts['w_br_dsa'], wts['w_br_ret'], wts['w_o'])
    x2d = _mlp(x2d, row(lp['norm2_g']), wts['w_up'], wts['w_down'], g_final, final_norm)
    P3 = P.reshape(B, L, P_COLS)
    last = P3[:, L - 1]
    shift_new = jnp.concatenate([last[:, P_RKV:P_RKV + 3 * RW_WIDTH], last[:, P_LORA:P_LORA + RW_LORA]], axis=1)
    k_new = k_rot.reshape(B, L, SA_KV_HEADS, SA_HEAD_DIM)
    v_new = P3[:, :, P_SAV:P_SAV + SA_KV_WIDTH].reshape(B, L, SA_KV_HEADS, SA_HEAD_DIM)
    ik_new = kw_rot.reshape(B, L, LANES)[:, :, :IDX_DIM]
    return x2d, (k_new, v_new, ik_new, s_rw_new, shift_new, s_ret_new)


def kernel(x_prompt, x_sample, cache_dsa_k, cache_dsa_v, cache_dsa_ik, state_rwkv, state_rwkv_shift, state_ret, norm1_g, w_in, rwkv_mu, rwkv_w0, rwkv_w2, rwkv_a0, rwkv_a2, rwkv_g2, rwkv_k_k, rwkv_k_a, rwkv_r_k, rwkv_lnx_g, w_br_rwkv, w_br_dsa, w_br_ret, w_o, norm2_g, w_up, w_down, final_norm_g):
    params = {
        'norm1_g': norm1_g, 'rwkv_mu': rwkv_mu, 'rwkv_w0': rwkv_w0, 'rwkv_w2': rwkv_w2,
        'rwkv_a0': rwkv_a0, 'rwkv_a2': rwkv_a2, 'rwkv_g2': rwkv_g2, 'rwkv_k_k': rwkv_k_k,
        'rwkv_k_a': rwkv_k_a, 'rwkv_r_k': rwkv_r_k, 'rwkv_lnx_g': rwkv_lnx_g, 'norm2_g': norm2_g,
    }
    depth = w_in.shape[0]
    Bp, Lp, D = x_prompt.shape
    Bs, Ls, _ = x_sample.shape
    past_s = cache_dsa_k.shape[2]
    xp = x_prompt.reshape(Bp * Lp, D).astype(F32)
    xs = x_sample.reshape(Bs * Ls, D).astype(F32)
    g_final = final_norm_g.reshape(1, D).astype(F32)
    zero_p = (None, None, None,
              jnp.zeros((Bp, RW_HEADS, RW_HEAD_DIM, RW_HEAD_DIM), F32), jnp.zeros((Bp, RW_COLS), F32),
              jnp.zeros((Bp, RET_HEADS, RET_HEAD_DIM, RET_HEAD_DIM), F32))
    p_states = [[] for _ in range(6)]
    s_states = [[] for _ in range(6)]
    w_all = _w_prep(w_in)
    for i in range(depth):
        lp = {name: arr[i] for name, arr in params.items()}
        wts = {'w_in': w_all, 'layer': i, 'w_br_rwkv': _bf(w_br_rwkv[i]), 'w_br_dsa': _bf(w_br_dsa[i]),
               'w_br_ret': _bf(w_br_ret[i]), 'w_o': _bf(w_o[i]), 'w_up': _bf(w_up[i]), 'w_down': _bf(w_down[i])}
        final = i == depth - 1
        cache_s = (cache_dsa_k[i], cache_dsa_v[i], cache_dsa_ik[i], state_rwkv[i], state_rwkv_shift[i],
                   state_ret[i])
        xp, new_p = _layer(xp, Bp, Lp, 0, zero_p, lp, wts, g_final, final)
        xs, new_s = _layer(xs, Bs, Ls, past_s, cache_s, lp, wts, g_final, final)
        for j in range(6):
            p_states[j].append(new_p[j])
            s_states[j].append(new_s[j])
    y_prompt = xp.reshape(Bp, Lp, D)
    y_sample = xs.reshape(Bs, Ls, D)
    p_out = [jnp.stack(t, axis=0) for t in p_states]
    s_out = [jnp.stack(t, axis=0) for t in s_states]
    return (y_prompt, y_sample, *p_out, *s_out)
```

```python
import functools

import numpy as np
import jax
import jax.numpy as jnp
from jax import lax
from jax.experimental import pallas as pl
from jax.experimental.pallas import tpu as pltpu

F32 = jnp.float32
BF16 = jnp.bfloat16
I32 = jnp.int32
I16 = jnp.int16

D_MODEL = 1024
CHUNK = 64
Q_BLOCK = 128
EPS = 1e-6

RW_HEADS = 8
RW_HEAD_DIM = 64
RW_WIDTH = RW_HEADS * RW_HEAD_DIM
RW_DECAY_LORA = 64
RW_ICLR_LORA = 64
RW_GATE_LORA = 128
RW_LORA = RW_DECAY_LORA + RW_ICLR_LORA + RW_GATE_LORA
RW_COLS = 3 * RW_WIDTH + RW_LORA
RW_GN_EPS = 64e-5

SA_HEADS = 8
SA_KV_HEADS = 2
SA_HEAD_DIM = 64
SA_WIDTH = SA_HEADS * SA_HEAD_DIM
SA_KV_WIDTH = SA_KV_HEADS * SA_HEAD_DIM
IDX_HEADS = 4
IDX_DIM = 64
IDX_WIDTH = IDX_HEADS * IDX_DIM
TOPK_MAX = 256
ROPE_THETA = 500000.0
ROPE_DIM = SA_HEAD_DIM // 4
INDEX_SCALE = (IDX_DIM ** -0.5) * (IDX_HEADS ** -0.5)
SA_COLS = SA_WIDTH + 2 * SA_KV_WIDTH + IDX_WIDTH + IDX_DIM + IDX_HEADS

RET_HEADS = 4
RET_HEAD_DIM = 128
RET_WIDTH = RET_HEADS * RET_HEAD_DIM
RET_ROPE_BASE = 10000.0
RET_COLS = 4 * RET_WIDTH

N_BRANCH = 3
GATE_COLS = N_BRANCH * D_MODEL
IN_COLS = RW_COLS + SA_COLS + RET_COLS + GATE_COLS
D_FF = 4 * D_MODEL

LANES = 128
TILE_ELEMS = 64 * 8 * LANES

P_GATE = 0
P_RET = P_GATE + GATE_COLS
P_RKV = P_RET + RET_COLS
P_SAQ = P_RKV + 3 * RW_WIDTH
P_LORA = P_SAQ + SA_WIDTH
P_QI = P_LORA + RW_LORA
P_SAK = P_QI + IDX_WIDTH
P_SAV = P_SAK + SA_KV_WIDTH
P_KIWI = P_SAV + SA_KV_WIDTH
P_COLS = 8192
INT_MIN = -2 ** 31
NEG_BIG = -1e30
LOG2_E = 1.4426950408889634
VMEM_LIMIT = 56 * 1024 * 1024


def _bf(x):
    return x.astype(BF16)


def _dot(a, b):
    return jnp.dot(a, b, preferred_element_type=F32)


def _dot_nt(a, b):
    return lax.dot_general(a, b, (((1,), (1,)), ((), ())), preferred_element_type=F32)


def _dot_tn(a, b):
    return lax.dot_general(a, b, (((0,), (0,)), ((), ())), preferred_element_type=F32)


def _dot_split2(a_exact, x):
    hi = _bf(x)
    lo = _bf(x - hi.astype(F32))
    return _dot(a_exact, hi) + _dot(a_exact, lo)


def _params(sem):
    return pltpu.CompilerParams(dimension_semantics=sem, vmem_limit_bytes=VMEM_LIMIT)


def _in_proj_kernel(x_ref, g_ref, w_ref, o_ref, h_ref):
    @pl.when(pl.program_id(1) == 0)
    def _():
        x = x_ref[...]
        ms = jnp.mean(x * x, axis=-1, keepdims=True)
        h_ref[...] = _bf(x * lax.rsqrt(ms + EPS) * g_ref[...])

    o_ref[...] = _dot(h_ref[...], w_ref[0])


def _in_proj(x2d, g, w_all, layer):
    T = x2d.shape[0]
    tm = min(1024, T)
    tn = 1024
    return pl.pallas_call(
        _in_proj_kernel,
        grid=(T // tm, P_COLS // tn),
        in_specs=[pl.BlockSpec((tm, D_MODEL), lambda i, j: (i, 0)),
                  pl.BlockSpec((1, D_MODEL), lambda i, j: (0, 0)),
                  pl.BlockSpec((1, D_MODEL, tn), lambda i, j: (layer, 0, j))],
        out_specs=pl.BlockSpec((tm, tn), lambda i, j: (i, j)),
        out_shape=jax.ShapeDtypeStruct((T, P_COLS), F32),
        scratch_shapes=[pltpu.VMEM((tm, D_MODEL), BF16)],
        compiler_params=_params(("parallel", "arbitrary")),
        name="in_proj",
    )(x2d, g, w_all)


KIWI_COLS = IDX_DIM + IDX_HEADS
TAIL0 = RW_COLS + SA_COLS - KIWI_COLS
TAIL_W = (IN_COLS - TAIL0 + LANES - 1) // LANES * LANES


def _w_prep_kernel(w_ref, o_ref):
    x = w_ref[0]
    x = jnp.where(lax.broadcasted_iota(I32, x.shape, 1) < IN_COLS, x, 0.0)
    lane = lax.broadcasted_iota(I32, (x.shape[0], LANES), 1)

    def put(off, v):
        o_ref[0, :, off:off + v.shape[1]] = _bf(v)

    nslab = TAIL_W // LANES
    rolled = [pltpu.roll(x[:, TAIL0 + s * LANES:TAIL0 + (s + 1) * LANES], LANES - KIWI_COLS, 1)
              for s in range(nslab)]
    for s in range((RET_COLS + GATE_COLS) // LANES):
        v = jnp.where(lane < LANES - KIWI_COLS, rolled[s], rolled[s + 1])
        put((P_RET if s < RET_COLS // LANES else P_GATE - RET_COLS) + s * LANES, v)
    sa = RW_COLS
    put(P_RKV, x[:, 0:3 * RW_WIDTH])
    put(P_LORA, x[:, 3 * RW_WIDTH:RW_COLS])
    put(P_SAQ, x[:, sa:sa + SA_WIDTH])
    put(P_SAK, x[:, sa + SA_WIDTH:sa + SA_WIDTH + SA_KV_WIDTH])
    put(P_SAV, x[:, sa + SA_WIDTH + SA_KV_WIDTH:sa + SA_WIDTH + 2 * SA_KV_WIDTH])
    put(P_QI, x[:, sa + SA_WIDTH + 2 * SA_KV_WIDTH:TAIL0])
    put(P_KIWI, jnp.where(lane < KIWI_COLS, x[:, TAIL0:TAIL0 + LANES], 0.0))
    put(P_KIWI + LANES, jnp.zeros((x.shape[0], P_COLS - P_KIWI - LANES), F32))


def _w_prep(w_in):
    depth, d, _ = w_in.shape
    tm = 256
    return pl.pallas_call(
        _w_prep_kernel,
        grid=(depth, d // tm),
        in_specs=[pl.BlockSpec((1, tm, TAIL0 + TAIL_W), lambda l, i: (l, i, 0))],
        out_specs=pl.BlockSpec((1, tm, P_COLS), lambda l, i: (l, i, 0)),
        out_shape=jax.ShapeDtypeStruct((depth, d, P_COLS), BF16),
        compiler_params=_params(("parallel", "parallel")),
        name="w_prep",
    )(w_in)


def _rot_slab(x, c, s_lo, s_hi, shift):
    return x * c + pltpu.roll(x, LANES - shift, 1) * s_lo + pltpu.roll(x, shift, 1) * s_hi


def _sa_prep_kernel(q_ref, qi_ref, k_ref, kw_ref, c_ref, s1_ref, s2_ref, ck_ref, s1k_ref, s2k_ref,
                    qo_ref, qio_ref, ko_ref, kwo_ref):
    c, s1, s2 = c_ref[...], s1_ref[...], s2_ref[...]
    half = ROPE_DIM // 2
    scale = SA_HEAD_DIM ** -0.5 * LOG2_E
    for s in range(SA_WIDTH // LANES):
        sl = slice(s * LANES, (s + 1) * LANES)
        qo_ref[:, sl] = _rot_slab(q_ref[:, sl], c, s1, s2, half) * scale
    for s in range(IDX_WIDTH // LANES):
        sl = slice(s * LANES, (s + 1) * LANES)
        qio_ref[:, sl] = _rot_slab(qi_ref[:, sl], c, s1, s2, half)
    ko_ref[...] = _rot_slab(k_ref[...], c, s1, s2, half)
    kwo_ref[...] = _rot_slab(kw_ref[...], ck_ref[...], s1k_ref[...], s2k_ref[...], half)


def _sa_prep(P, L, tabs):
    T = P.shape[0]
    tm = min(512, L)
    nl = L // tm
    tab_spec = pl.BlockSpec((tm, LANES), lambda i: (i % nl, 0))
    return pl.pallas_call(
        _sa_prep_kernel,
        grid=(T // tm,),
        in_specs=[pl.BlockSpec((tm, SA_WIDTH), lambda i: (i, P_SAQ // SA_WIDTH)),
                  pl.BlockSpec((tm, IDX_WIDTH), lambda i: (i, P_QI // IDX_WIDTH)),
                  pl.BlockSpec((tm, LANES), lambda i: (i, P_SAK // LANES)),
                  pl.BlockSpec((tm, LANES), lambda i: (i, P_KIWI // LANES))] + [tab_spec] * 6,
        out_specs=[pl.BlockSpec((tm, SA_WIDTH), lambda i: (i, 0)),
                   pl.BlockSpec((tm, IDX_WIDTH), lambda i: (i, 0)),
                   pl.BlockSpec((tm, LANES), lambda i: (i, 0)),
                   pl.BlockSpec((tm, LANES), lambda i: (i, 0))],
        out_shape=[jax.ShapeDtypeStruct((T, SA_WIDTH), F32),
                   jax.ShapeDtypeStruct((T, IDX_WIDTH), F32),
                   jax.ShapeDtypeStruct((T, LANES), F32),
                   jax.ShapeDtypeStruct((T, LANES), F32)],
        compiler_params=_params(("parallel",)),
        name="sa_prep",
    )(P, P, P, P, *tabs)


def _sa_tables(pos):
    half = ROPE_DIM // 2
    freqs = 1.0 / (ROPE_THETA ** (jnp.arange(0, ROPE_DIM, 2, dtype=F32) / ROPE_DIM))
    ang = pos.astype(F32)[:, None] * freqs[None, :]
    cos, sin = jnp.cos(ang), jnp.sin(ang)
    n = pos.shape[0]
    pad = SA_HEAD_DIM - ROPE_DIM
    c_head = jnp.concatenate([cos, cos, jnp.ones((n, pad), F32)], axis=1)
    s1_head = jnp.concatenate([-sin, jnp.zeros((n, half + pad), F32)], axis=1)
    s2_head = jnp.concatenate([jnp.zeros((n, half), F32), sin, jnp.zeros((n, pad), F32)], axis=1)
    one, zero = jnp.ones((n, SA_HEAD_DIM), F32), jnp.zeros((n, SA_HEAD_DIM), F32)
    two = lambda t: jnp.concatenate([t, t], axis=1)
    return (two(c_head), two(s1_head), two(s2_head),
            jnp.concatenate([c_head, one], axis=1), jnp.concatenate([s1_head, zero], axis=1),
            jnp.concatenate([s2_head, zero], axis=1))


def _softplus(u):
    return jnp.maximum(u, 0.0) + jnp.log(1.0 + jnp.exp(-jnp.abs(u)))


def _rwkv_kernel(r_ref, k_ref, v_ref, lo_ref, shr_ref, shk_ref, shv_ref, shlo_ref, s0_ref,
                 mur_ref, muk_ref, muv_ref, mulo_ref, w0_ref, w2_ref, a0_ref, a2_ref, g2_ref,
                 kk_ref, ka_ref, rk_ref, lnx_ref, e_ref, tin_ref, tsuf_ref,
                 y_ref, so_ref,
                 cr_ref, ck_ref, cv_ref, clo_ref, s_ref,
                 kap_ref, rt_ref, bh_ref, kh_ref, bt_ref, kt_ref, vv_ref, gc_ref, yy_ref, rr_ref,
                 ac_ref, cc_ref):
    c = pl.program_id(1)
    TT = r_ref.shape[0]
    nch = TT // CHUNK
    N = RW_HEAD_DIM

    @pl.when(c == 0)
    def _():
        cr_ref[0:1, :] = shr_ref[0]
        ck_ref[0:1, :] = shk_ref[0]
        cv_ref[0:1, :] = shv_ref[0]
        clo_ref[0:1, :] = shlo_ref[0]
        s_ref[...] = s0_ref[0]

    def lerp(p_ref, carry_ref, mu_ref):
        p = p_ref[...]
        rolled = pltpu.roll(p, 1, 0)
        row0 = lax.broadcasted_iota(I32, p.shape, 0) == 0
        prev = jnp.where(row0, carry_ref[0:1, :], rolled)
        carry_ref[0:1, :] = p[TT - 1:TT, :]
        return p + (prev - p) * mu_ref[...]

    xr = lerp(r_ref, cr_ref, mur_ref)
    xk = lerp(k_ref, ck_ref, muk_ref)
    xv = lerp(v_ref, cv_ref, muv_ref)
    xlo = lerp(lo_ref, clo_ref, mulo_ref)
    xw = xlo[:, :RW_DECAY_LORA]
    xa = xlo[:, RW_DECAY_LORA:RW_DECAY_LORA + RW_ICLR_LORA]
    xg = xlo[:, RW_DECAY_LORA + RW_ICLR_LORA:]

    z = w0_ref[...] + _dot(_bf(jnp.tanh(xw)), w2_ref[...])
    w = -_softplus(-z) - 0.5
    ld = -jnp.exp(w)
    a = jax.nn.sigmoid(a0_ref[...] + _dot(_bf(xa), a2_ref[...]))
    gate = _dot(_bf(jax.nn.sigmoid(xg)), g2_ref[...])
    e_blk = e_ref[...]
    kk = xk * kk_ref[...]
    kk = kk / jnp.maximum(jnp.sqrt(_dot(_bf(kk * kk), e_blk)), 1e-12)
    k2 = xk * (1.0 + (a - 1.0) * ka_ref[...])
    bb = kk * a
    bonus = _dot(_bf(xr * k2 * rk_ref[...]), e_blk) * xv

    lin = _dot_split2(tin_ref[...], ld)
    lsuf = _dot_split2(tsuf_ref[...], ld)
    e_in = jnp.exp(lin)
    e_ninv = jnp.exp(-lin)
    e_suf = jnp.exp(lsuf)
    kap_ref[...] = kk * jnp.exp(lin - ld)
    rt_ref[...] = xr * e_in
    bh_ref[...] = bb * e_ninv
    kh_ref[...] = k2 * e_ninv
    bt_ref[...] = bb * e_suf
    kt_ref[...] = k2 * e_suf
    vv_ref[...] = xv
    gc_ref[...] = jnp.exp(lin + lsuf)

    ri = lax.broadcasted_iota(I32, (CHUNK, CHUNK), 0)
    ci = lax.broadcasted_iota(I32, (CHUNK, CHUNK), 1)
    strict = ri > ci
    incl = ri >= ci
    eye = (ri == ci).astype(F32)
    heads = range(RW_HEADS)
    hsl = [slice(h * N, (h + 1) * N) for h in heads]

    per_it = 2 if nch % 2 == 0 else 1

    def coef_body(it, carry):
        chs = [it * per_it + t for t in range(per_it)]
        pairs = [(t, h) for t in range(per_it) for h in heads]
        rows = [pl.ds(pl.multiple_of(ch * CHUNK, CHUNK), CHUNK) for ch in chs]
        ld = lambda ref, p: ref[rows[p[0]], hsl[p[1]]]
        kap = [ld(kap_ref, p) for p in pairs]
        rt = [ld(rt_ref, p) for p in pairs]
        vh = [ld(vv_ref, p) for p in pairs]
        idx = range(len(pairs))
        gmat = [_dot_nt(_bf(jnp.concatenate([kap[n], rt[n]], axis=0)),
                        _bf(jnp.concatenate([ld(bh_ref, pairs[n]), ld(kh_ref, pairs[n])], axis=0)))
                for n in idx]
        n_ab = [jnp.where(strict, g[:CHUNK, :CHUNK], 0.0) for g in gmat]
        m_rb = [jnp.where(incl, g[CHUNK:, :CHUNK], 0.0) for g in gmat]
        m_v = [_bf(jnp.concatenate([jnp.where(strict, g[:CHUNK, CHUNK:], 0.0),
                                    jnp.where(incl, g[CHUNK:, CHUNK:], 0.0)], axis=0)) for g in gmat]
        mv = [_dot(m_v[n], _bf(vh[n])) for n in idx]
        x_inv = [eye - t for t in n_ab]
        pw = n_ab
        for _ in range(5):
            pwb = [_bf(p) for p in pw]
            pw = [_dot(p, p) for p in pwb]
            x_inv = [x + _dot(_bf(x), _bf(p)) for x, p in zip(x_inv, pw)]
        w = [_dot(_bf(x_inv[n]), _bf(jnp.concatenate([kap[n], mv[n][:CHUNK]], axis=1))) for n in idx]
        wb = [_bf(t) for t in w]
        ry = [jnp.concatenate([rt[n], mv[n][CHUNK:]], axis=1) - _dot(_bf(m_rb[n]), wb[n]) for n in idx]
        dmat = [_dot_tn(wb[n], _bf(ld(bt_ref, pairs[n]))) for n in idx]
        vtk = [_dot_tn(_bf(vh[n]), _bf(ld(kt_ref, pairs[n]))) for n in idx]
        for n, (t, h) in enumerate(pairs):
            rr_ref[rows[t], hsl[h]] = ry[n][:, :N]
            yy_ref[rows[t], hsl[h]] = ry[n][:, N:]
            ac_ref[chs[t], h] = -dmat[n][:N]
            cc_ref[chs[t], h] = vtk[n] - dmat[n][N:]
        return carry

    lax.fori_loop(0, nch // per_it, coef_body, 0)

    def state_body(ch, carry):
        rows = pl.ds(pl.multiple_of(ch * CHUNK, CHUNK), CHUNK)
        s_old = [s_ref[h] for h in heads]
        sb = [_bf(t) for t in s_old]
        s_new = [_dot(sb[h], _bf(ac_ref[ch, h])) for h in heads]
        y_c = [_dot_nt(_bf(rr_ref[rows, hsl[h]]), sb[h]) for h in heads]
        for h in heads:
            gch = gc_ref[pl.ds(pl.multiple_of(ch * CHUNK, CHUNK), 1), hsl[h]]
            s_ref[h] = s_old[h] * gch + s_new[h] + cc_ref[ch, h]
            yy_ref[rows, hsl[h]] = yy_ref[rows, hsl[h]] + y_c[h]
        return carry

    lax.fori_loop(0, nch, state_body, 0)

    y = yy_ref[...]
    mean = _dot(_bf(y), e_blk) * (1.0 / N)
    d = y - mean
    var = _dot(_bf(d * d), e_blk) * (1.0 / N)
    yn = d * lax.rsqrt(var + RW_GN_EPS) * lnx_ref[...]
    y_ref[...] = (yn + bonus) * gate

    @pl.when(c == pl.num_programs(1) - 1)
    def _():
        so_ref[0] = s_ref[...]


def _rwkv(P, B, L, shift_prev, s0, lp):
    T = P.shape[0]
    TT = min(256, L)
    nt = L // TT
    W = RW_WIDTH
    row = lambda t: t.reshape(1, -1).astype(F32)
    mu = lp['rwkv_mu']
    sh = shift_prev.astype(F32)
    pieces = lambda t: (t[..., 0:W], t[..., W:2 * W], t[..., 2 * W:3 * W], t[..., 3 * W:])
    mu_r, mu_k, mu_v, mu_lo = [row(t) for t in pieces(mu)]
    sh_r, sh_k, sh_v, sh_lo = [t.reshape(B, 1, -1) for t in pieces(sh)]
    hid = jnp.arange(W) // RW_HEAD_DIM
    e_blk = (hid[:, None] == hid[None, :]).astype(BF16)
    ti = jnp.arange(TT)
    same = (ti[:, None] // CHUNK) == (ti[None, :] // CHUNK)
    tri_in = (same & (ti[None, :] <= ti[:, None])).astype(BF16)
    tri_suf = (same & (ti[None, :] > ti[:, None])).astype(BF16)

    tok = lambda w, blk: pl.BlockSpec((TT, w), lambda b, c: (b * nt + c, blk))
    full = lambda shape: pl.BlockSpec(shape, lambda b, c: (0,) * len(shape))
    shs = lambda w: pl.BlockSpec((1, 1, w), lambda b, c: (b, 0, 0))
    st_spec = pl.BlockSpec((1, RW_HEADS, RW_HEAD_DIM, RW_HEAD_DIM), lambda b, c: (b, 0, 0, 0))
    big = lambda: pltpu.VMEM((TT, W), F32)
    y, s_out = pl.pallas_call(
        _rwkv_kernel,
        grid=(B, nt),
        in_specs=[tok(W, P_RKV // W), tok(W, P_RKV // W + 1), tok(W, P_RKV // W + 2),
                  tok(RW_LORA, P_LORA // RW_LORA),
                  shs(W), shs(W), shs(W), shs(RW_LORA), st_spec,
                  full((1, W)), full((1, W)), full((1, W)), full((1, RW_LORA)),
                  full((1, W)), full((RW_DECAY_LORA, W)), full((1, W)), full((RW_ICLR_LORA, W)),
                  full((RW_GATE_LORA, W)), full((1, W)), full((1, W)), full((1, W)), full((1, W)),
                  full((W, W)), full((TT, TT)), full((TT, TT))],
        out_specs=[pl.BlockSpec((TT, W), lambda b, c: (b * nt + c, 0)), st_spec],
        out_shape=[jax.ShapeDtypeStruct((T, W), F32),
                   jax.ShapeDtypeStruct((B, RW_HEADS, RW_HEAD_DIM, RW_HEAD_DIM), F32)],
        scratch_shapes=[pltpu.VMEM((8, W), F32), pltpu.VMEM((8, W), F32), pltpu.VMEM((8, W), F32),
                        pltpu.VMEM((8, RW_LORA), F32),
                        pltpu.VMEM((RW_HEADS, RW_HEAD_DIM, RW_HEAD_DIM), F32)] + [big() for _ in range(10)]
        + [pltpu.VMEM((TT // CHUNK, RW_HEADS, RW_HEAD_DIM, RW_HEAD_DIM), F32) for _ in range(2)],
        compiler_params=_params(("parallel", "arbitrary")),
        name="rwkv",
    )(P, P, P, P, sh_r, sh_k, sh_v, sh_lo, s0.astype(F32),
      mu_r, mu_k, mu_v, mu_lo, row(lp['rwkv_w0']), _bf(lp['rwkv_w2']), row(lp['rwkv_a0']),
      _bf(lp['rwkv_a2']), _bf(lp['rwkv_g2']), row(lp['rwkv_k_k']), row(lp['rwkv_k_a']),
      row(lp['rwkv_r_k']), row(lp['rwkv_lnx_g']), e_blk, tri_in, tri_suf)
    return y, s_out


ONES_ROWS = 16
BOUND_SLACK = 1.05
SAFE_BOUND = 60.0


def _dsa_kernel(q_ref, qi_ref, kw_ref, k_ref, v_ref, ki_ref, y_ref,
                kb_ref, vt_ref, kib_ref, kmax_ref, keys_ref, hi_ref, lo_ref, bias_ref, pstar_ref, red_ref,
                *, past, lk_real, topk, kc):
    i = pl.program_id(1)
    qb = q_ref.shape[1]
    lk = k_ref.shape[1]
    kf = float(topk)
    HD = SA_HEAD_DIM

    ones_sq = jnp.ones((LANES, LANES), BF16)

    def head_slab(x, odd):
        lane = lax.broadcasted_iota(I32, x.shape, 1)
        return jnp.where(lane < HD, pltpu.roll(x, HD, 1) if odd else x, 0.0)

    @pl.when(i == 0)
    def _():
        kib_ref[...] = _bf(ki_ref[0])
        k = k_ref[0]
        lane = lax.broadcasted_iota(I32, k.shape, 1)
        for c in range(SA_KV_HEADS):
            kc_b = _bf(head_slab(k, c == 1))
            kb_ref[c] = jnp.where(lane == HD, jnp.ones_like(kc_b), kc_b)
            kf32 = kc_b.astype(F32)
            n2 = _dot(_bf(kf32 * kf32), ones_sq)
            kmax_ref[c] = jnp.broadcast_to(jnp.max(n2, axis=0, keepdims=True), (8, LANES))
        vt = v_ref[0].T
        for c in range(SA_KV_HEADS):
            vt_ref[c, 0:HD, :] = _bf(vt[c * HD:(c + 1) * HD, :])
            vt_ref[c, HD:HD + ONES_ROWS, :] = jnp.ones((ONES_ROWS, lk), BF16)

    qpos = past + i * qb + lax.broadcasted_iota(I32, (1, qb), 1)
    limit = jnp.minimum((qpos // CHUNK + 1) * CHUNK, lk_real)
    hi = jnp.minimum(past + (i + 1) * qb, lk_real)
    nk = lax.shift_right_logical(hi + (2 * kc - 1), kc.bit_length())
    sub_idx = lax.broadcasted_iota(I32, (kc, qb), 0)
    kwt = kw_ref[0].T
    wi_h = [kwt[IDX_DIM + h:IDX_DIM + h + 1, :] * INDEX_SCALE for h in range(IDX_HEADS)]
    qi = qi_ref[0]
    qi_h = [_bf(qi[:, h * IDX_DIM:(h + 1) * IDX_DIM]) for h in range(IDX_HEADS)]

    def chunk(j):
        return pl.ds(pl.multiple_of(j * kc, kc), kc)

    def pair(t):
        return (2 * t, 2 * t + 1)

    def score_body(t, carry):
        js = pair(t)
        dots = [[_dot_nt(kib_ref[chunk(j), 0:IDX_DIM], qi_h[h]) for h in range(IDX_HEADS)] for j in js]
        for j, d in zip(js, dots):
            s = jnp.zeros((kc, qb), F32)
            for h in range(IDX_HEADS):
                s = s + jnp.maximum(d[h], 0.0) * wi_h[h]
            s = jnp.where(s == 0.0, 0.0, s)
            bits = lax.bitcast_convert_type(s, I32)
            key = bits ^ ((bits >> 31) & 0x7FFFFFFF)
            key = jnp.where((sub_idx + j * kc) < limit, key, INT_MIN)
            keys_ref[chunk(j), :] = key
            hi_ref[chunk(j), :] = (key >> 16).astype(I16)
        return carry

    lax.fori_loop(0, nk, score_body, 0)

    def fold(m, rows):
        parts = [m[r * rows:(r + 1) * rows] for r in range(kc // rows)]
        while len(parts) > 1:
            parts = [a + b for a, b in zip(parts[0::2], parts[1::2])]
        return parts[0]

    max_trips = lk // (2 * kc)

    def short_trips(body, carry):
        base = jnp.int32(0)
        p = 1 << (max_trips.bit_length() - 1)
        while p:
            take = (nk & p) != 0

            def run(c, base=base, p=p):
                for s in range(p):
                    c = body(base + s, c)
                return c

            carry = lax.cond(take, run, lambda c: c, carry)
            base = base + jnp.where(take, p, 0)
            p //= 2
        return carry

    def col_total(part):
        rows = part.shape[0]
        red_ref[0:rows, :] = part
        parts = [red_ref[r:r + 1, :] for r in range(rows)]
        while len(parts) > 1:
            parts = [a + b for a, b in zip(parts[0::2], parts[1::2])]
        return parts[0]

    def count(pred):
        def body(t, acc):
            for j in pair(t):
                acc = acc + fold(jnp.where(pred(keys_ref[chunk(j), :], j), 1.0, 0.0), 8)
            return acc

        return col_total(short_trips(body, jnp.zeros((8, qb), F32)))

    def count16(ref, pred):
        def body(t, acc):
            for j in pair(t):
                acc = acc + fold(jnp.where(pred(ref[chunk(j), :]), jnp.int16(1), jnp.int16(0)), 16)
            return acc

        acc = short_trips(body, jnp.zeros((16, qb), I16))
        return col_total(acc.astype(F32))

    def kth_bits16(ref, need):
        def bit_body(it, u):
            cand = u | lax.shift_left(jnp.int32(1), 15 - it)
            thr = (cand - 32768).astype(I16)
            return jnp.where(count16(ref, lambda x: x >= thr) >= need, cand, u)

        return lax.fori_loop(0, 16, bit_body, jnp.zeros((1, qb), I32))

    t_hi = kth_bits16(hi_ref, kf) - 32768
    t_hi16 = t_hi.astype(I16)
    need_lo = kf - count16(hi_ref, lambda x: x > t_hi16)

    def lo_body(t, carry):
        for j in pair(t):
            lo = ((keys_ref[chunk(j), :] & 0xFFFF) - 32768).astype(I16)
            lo_ref[chunk(j), :] = jnp.where(hi_ref[chunk(j), :] == t_hi16, lo, jnp.int16(-32768))
        return carry

    short_trips(lo_body, jnp.int32(0))
    tau = t_hi * 65536 + kth_bits16(lo_ref, need_lo)
    cnt_ge = count(lambda kj, j: kj >= tau)
    cnt_gt = count(lambda kj, j: kj > tau)
    need = kf - cnt_gt
    excess = jnp.logical_and(cnt_ge > kf, tau != INT_MIN)
    p_default = jnp.where(tau == INT_MIN, -1, 2 ** 30).astype(I32)
    pstar_ref[...] = jnp.broadcast_to(p_default, pstar_ref.shape)

    @pl.when(jnp.max(jnp.where(excess, 1.0, 0.0)) > 0.0)
    def _():
        nbits = max(1, int(lk - 1).bit_length())

        def idx_body(it, p):
            cand = p | lax.shift_left(jnp.int32(1), nbits - 1 - it)
            g = count(lambda kj, j: jnp.logical_and(kj == tau, (sub_idx + j * kc) < cand))
            return jnp.where(g < need, cand, p)

        p = lax.fori_loop(0, nbits, idx_body, jnp.zeros((1, qb), I32))
        pstar_ref[...] = jnp.broadcast_to(jnp.where(excess, p, p_default), pstar_ref.shape)

    pstar = pstar_ref[0:1, :]

    def bias_body(t, carry):
        for j in pair(t):
            kj = keys_ref[chunk(j), :]
            sel = jnp.logical_or(kj > tau, jnp.logical_and(kj == tau, (sub_idx + j * kc) <= pstar))
            bias_ref[chunk(j), :] = _bf(jnp.where(sel, 0.0, NEG_BIG))
        return carry

    short_trips(bias_body, jnp.int32(0))

    q = q_ref[0]
    group = SA_HEADS // SA_KV_HEADS
    lane_q = lax.broadcasted_iota(I32, (qb, LANES), 1)
    q_heads, bounds = [], []
    for h in range(SA_HEADS):
        qh = _bf(head_slab(q[:, (h // 2) * LANES:(h // 2 + 1) * LANES], h % 2 == 1))
        qf = qh.astype(F32)
        qn2 = _dot(_bf(qf * qf), ones_sq)
        q_heads.append(qh)
        bounds.append(jnp.sqrt(qn2 * kmax_ref[h // group, 0:1, :]) * BOUND_SLACK)
    worst = bounds[0]
    for b in bounds[1:]:
        worst = jnp.maximum(worst, b)
    safe = jnp.max(worst) <= SAFE_BOUND

    zero_acc = jnp.zeros((HD + ONES_ROWS, qb), F32)

    def vt_chunk(c, j):
        return vt_ref[c, :, pl.ds(pl.multiple_of(j * kc, kc), kc)]

    @pl.when(safe)
    def _():
        hs = range(SA_HEADS)
        qa = [jnp.where(lane_q == HD, _bf(-bounds[h]), q_heads[h]) for h in hs]

        def body(t, accs):
            js = pair(t)
            logit = [[_dot_nt(kb_ref[h // group, chunk(j), :], qa[h]) for h in hs] for j in js]
            bias = [bias_ref[chunk(j), :].astype(F32) for j in js]
            p = [[_bf(jnp.exp2(lg + b)) for lg in row] for b, row in zip(bias, logit)]
            pv = [[_dot(vt_chunk(h // group, j), row[h]) for h in hs] for j, row in zip(js, p)]
            return tuple(accs[h] + pv[0][h] + pv[1][h] for h in hs)

        accs = lax.fori_loop(0, nk, body, (zero_acc,) * SA_HEADS)
        for h in hs:
            y_ref[0, h * HD:(h + 1) * HD, :] = accs[h][:HD] / accs[h][HD:HD + 1]

    @pl.when(jnp.logical_not(safe))
    def _():
        for h in range(SA_HEADS):
            c = h // group

            def logit(j):
                return _dot_nt(kb_ref[c, chunk(j), :], q_heads[h]) + bias_ref[chunk(j), :].astype(F32)

            def max_body(t, m):
                for j in pair(t):
                    m = jnp.maximum(m, jnp.max(logit(j), axis=0, keepdims=True))
                return m

            m = lax.fori_loop(0, nk, max_body, jnp.full((1, qb), NEG_BIG, F32))

            def sum_body(t, acc):
                for j in pair(t):
                    acc = acc + _dot(vt_chunk(c, j), _bf(jnp.exp2(logit(j) - m)))
                return acc

            acc = lax.fori_loop(0, nk, sum_body, zero_acc)
            y_ref[0, h * HD:(h + 1) * HD, :] = acc[:HD] / acc[HD:HD + 1]


def _dsa_call(q3, qi3, kw3, k_all, v_all, ki_all, v_blk, *, lk, past, lk_real, topk, qb, kc):
    B, n_q = q3.shape[:2]
    kern = functools.partial(_dsa_kernel, past=past, lk_real=lk_real, topk=topk, kc=kc)
    return pl.pallas_call(
        kern,
        grid=(B, n_q // qb),
        in_specs=[pl.BlockSpec((1, qb, SA_WIDTH), lambda b, i: (b, i, 0)),
                  pl.BlockSpec((1, qb, IDX_WIDTH), lambda b, i: (b, i, 0)),
                  pl.BlockSpec((1, qb, LANES), lambda b, i: (b, i, 0)),
                  pl.BlockSpec((1, lk, LANES), lambda b, i: (b, 0, 0), pipeline_mode=pl.Buffered(1)),
                  pl.BlockSpec((1, lk, LANES), lambda b, i: (b, 0, v_blk), pipeline_mode=pl.Buffered(1)),
                  pl.BlockSpec((1, lk, LANES), lambda b, i: (b, 0, 0), pipeline_mode=pl.Buffered(1))],
        out_specs=pl.BlockSpec((1, SA_WIDTH, qb), lambda b, i: (b, 0, i)),
        out_shape=jax.ShapeDtypeStruct((B, SA_WIDTH, n_q), F32),
        scratch_shapes=[pltpu.VMEM((SA_KV_HEADS, lk, LANES), BF16),
                        pltpu.VMEM((SA_KV_HEADS, SA_HEAD_DIM + ONES_ROWS, lk), BF16),
                        pltpu.VMEM((lk, LANES), BF16),
                        pltpu.VMEM((SA_KV_HEADS, 8, LANES), F32),
                        pltpu.VMEM((lk, qb), I32), pltpu.VMEM((lk, qb), I16), pltpu.VMEM((lk, qb), I16),
                        pltpu.VMEM((lk, qb), BF16),
                        pltpu.VMEM((8, qb), I32), pltpu.VMEM((16, qb), F32)],
        compiler_params=_params(("parallel", "arbitrary")),
        name="dsa",
    )(q3, qi3, kw3, k_all, v_all, ki_all)


def _round_up(x, m):
    return (x + m - 1) // m * m


def _dsa(P, B, L, past, q_rot, qi_rot, k_rot, kw_rot, k_past, v_past, ik_past):
    lk_real = past + L
    topk = min(TOPK_MAX, lk_real // 4)
    qb = next(w for w in (4 * LANES, 2 * LANES, LANES) if past == 0 and L % w == 0 or w == LANES)
    lq = _round_up(L, qb)
    qpad = lambda t: t if lq == L else jnp.pad(t, ((0, 0), (0, lq - L), (0, 0)))
    q3 = qpad(q_rot.reshape(B, L, SA_WIDTH))
    qi3 = qpad(qi_rot.reshape(B, L, IDX_WIDTH))
    kw3 = kw_rot.reshape(B, L, LANES)
    k3 = k_rot.reshape(B, L, LANES)
    common = dict(past=past, lk_real=lk_real, topk=topk, qb=qb)
    kc = TILE_ELEMS // qb // 2 if qb == LANES else TILE_ELEMS // qb
    if past == 0:
        while L % (2 * kc):
            kc //= 2
        P3 = P.reshape(B, L, P_COLS)
        return _dsa_call(q3, qi3, kw3, k3, P3, kw3, P_SAV // LANES, lk=L, kc=kc, **common)
    lk = _round_up(lk_real, 2 * kc)
    zpad = jnp.zeros((B, lk - lk_real, LANES), F32)
    v_new = P.reshape(B, L, P_COLS)[:, :, P_SAV:P_SAV + LANES]
    ik_p = jnp.concatenate([ik_past.astype(F32), jnp.zeros((B, past, LANES - IDX_DIM), F32)], axis=2)
    k_all = jnp.concatenate([k_past.reshape(B, past, LANES).astype(F32), k3, zpad], axis=1)
    v_all = jnp.concatenate([v_past.reshape(B, past, LANES).astype(F32), v_new, zpad], axis=1)
    ki_all = jnp.concatenate([ik_p, kw3, zpad], axis=1)
    y = _dsa_call(q3, qi3, qpad(kw3), k_all, v_all, ki_all, 0, lk=lk, kc=kc, **common)
    return y[:, :, :L]


def _ret_kernel(q_ref, k_ref, v_ref, g_ref, cos_ref, sin_ref, dm_ref, qd_ref, kd_ref, gc_ref, s0_ref,
                y_ref, so_ref, s_ref):
    c = pl.program_id(1)
    D = RET_HEAD_DIM
    cs = dm_ref.shape[1]
    nch = q_ref.shape[0] // cs
    heads = range(RET_HEADS)
    hsl = [slice(h * D, (h + 1) * D) for h in heads]
    rows = [slice(ch * cs, (ch + 1) * cs) for ch in range(nch)]
    prob = [(ch, h) for ch in range(nch) for h in heads]

    @pl.when(c == 0)
    def _():
        s_ref[...] = s0_ref[0]

    cos, sin = cos_ref[...], sin_ref[...]
    qb_, kb_, kd_, vb_ = [], [], [], []
    for h in heads:
        q = q_ref[:, hsl[h]]
        k = k_ref[:, hsl[h]]
        k = (k * cos + pltpu.roll(k, D // 2, 1) * sin) * (D ** -0.5)
        qb_.append(_bf(q * cos + pltpu.roll(q, D // 2, 1) * sin))
        kb_.append(_bf(k))
        kd_.append([_bf(k[r] * kd_ref[:, hsl[h]]) for r in rows])
        vb_.append(_bf(v_ref[:, hsl[h]]))
    scores = {(ch, h): _dot_nt(qb_[h][rows[ch]], kb_[h][rows[ch]]) * dm_ref[h] for ch, h in prob}
    ktv = {(ch, h): _dot_tn(kd_[h][ch], vb_[h][rows[ch]]) for ch, h in prob}
    intra = {(ch, h): _dot(_bf(scores[ch, h]), vb_[h][rows[ch]]) for ch, h in prob}
    s_at = {}
    for h in heads:
        s = s_ref[h]
        for ch in range(nch):
            s_at[ch, h] = _bf(s)
            s = s * gc_ref[:, hsl[h]] + ktv[ch, h]
        s_ref[h] = s
    cross = {(ch, h): _dot(qb_[h][rows[ch]], s_at[ch, h]) for ch, h in prob}
    for ch in range(nch):
        outs = []
        for h in heads:
            o = intra[ch, h] + cross[ch, h] * qd_ref[:, hsl[h]]
            o = o * lax.rsqrt(jnp.mean(o * o, axis=-1, keepdims=True) + EPS)
            outs.append(jax.nn.silu(g_ref[rows[ch], hsl[h]]) * o)
        y_ref[rows[ch], :] = jnp.concatenate(outs, axis=1)

    @pl.when(c == pl.num_programs(1) - 1)
    def _():
        so_ref[0] = s_ref[...]


def _retention(P, B, L, pos, s0):
    T = P.shape[0]
    c = min(CHUNK, L)
    nc = L // c
    D = RET_HEAD_DIM
    freqs = 1.0 / (RET_ROPE_BASE ** jnp.linspace(0.0, 1.0, D // 2, dtype=F32))
    ang = pos.astype(F32)[:, None] * freqs[None, :]
    cos = jnp.concatenate([jnp.cos(ang)] * 2, axis=1)
    sin = jnp.concatenate([-jnp.sin(ang), jnp.sin(ang)], axis=1)
    log_gamma = jnp.log1p(-jnp.exp2(-5.0 - jnp.arange(RET_HEADS, dtype=F32)))
    idx = jnp.arange(c, dtype=F32)
    dmask = jnp.exp(jnp.abs(idx[:, None] - idx[None, :])[None] * log_gamma[:, None, None])
    lanes = lambda t: jnp.repeat(t, D, axis=1)
    qdec = lanes(jnp.exp((idx[:, None] + 1.0) * log_gamma[None, :]))
    kdec = lanes(jnp.exp((c - 1.0 - idx)[:, None] * log_gamma[None, :]))
    gchunk = lanes(jnp.exp(c * log_gamma)[None, :])

    W = RET_WIDTH
    tt = min(4 * c, L)
    nt = L // tt
    tok = lambda blk: pl.BlockSpec((tt, W), lambda b, i: (b * nt + i, blk))
    full = lambda shape: pl.BlockSpec(shape, lambda b, i: (0,) * len(shape))
    st_spec = pl.BlockSpec((1, RET_HEADS, D, D), lambda b, i: (b, 0, 0, 0))
    y, s_out = pl.pallas_call(
        _ret_kernel,
        grid=(B, nt),
        in_specs=[tok(P_RET // W), tok(P_RET // W + 1), tok(P_RET // W + 2), tok(P_RET // W + 3),
                  pl.BlockSpec((tt, D), lambda b, i: (i, 0)), pl.BlockSpec((tt, D), lambda b, i: (i, 0)),
                  full((RET_HEADS, c, c)), full((c, W)), full((c, W)), full((1, W)), st_spec],
        out_specs=[pl.BlockSpec((tt, W), lambda b, i: (b * nt + i, 0)), st_spec],
        out_shape=[jax.ShapeDtypeStruct((T, W), F32), jax.ShapeDtypeStruct((B, RET_HEADS, D, D), F32)],
        scratch_shapes=[pltpu.VMEM((RET_HEADS, D, D), F32)],
        compiler_params=_params(("parallel", "arbitrary")),
        name="retention",
    )(P, P, P, P, cos, sin, dmask, qdec, kdec, gchunk, s0.astype(F32))
    return y, s_out


def _merge_kernel(x_ref, g0_ref, g1_ref, g2_ref, yr_ref, ys_ref, yt_ref, wr_ref, ws_ref, wt_ref, wo_ref,
                  o_ref):
    m = (jax.nn.sigmoid(g0_ref[...]) * _dot(_bf(yr_ref[...]), wr_ref[...])
         + jax.nn.sigmoid(g1_ref[...]) * _dot_tn(_bf(ys_ref[0]), ws_ref[...])
         + jax.nn.sigmoid(g2_ref[...]) * _dot(_bf(yt_ref[...]), wt_ref[...]))
    o_ref[...] = x_ref[...] + _dot(_bf(m), wo_ref[...])


def _merge(x2d, P, y_rw, y_sa_t, y_ret, w_rw, w_sa, w_ret, w_o):
    T = x2d.shape[0]
    L = y_sa_t.shape[2]
    tm = min(512, L)
    nl = L // tm
    D = D_MODEL
    tok = lambda w, blk: pl.BlockSpec((tm, w), lambda i: (i, blk))
    full = lambda shape: pl.BlockSpec(shape, lambda i: (0, 0))
    return pl.pallas_call(
        _merge_kernel,
        grid=(T // tm,),
        in_specs=[tok(D, 0), tok(D, 0), tok(D, 1), tok(D, 2), tok(RW_WIDTH, 0),
                  pl.BlockSpec((1, SA_WIDTH, tm), lambda i: (i // nl, 0, i % nl)),
                  tok(RET_WIDTH, 0), full((RW_WIDTH, D)), full((SA_WIDTH, D)), full((RET_WIDTH, D)),
                  full((D, D))],
        out_specs=tok(D, 0),
        out_shape=jax.ShapeDtypeStruct((T, D), F32),
        compiler_params=_params(("parallel",)),
        name="merge",
    )(x2d, P, P, P, y_rw, y_sa_t, y_ret, w_rw, w_sa, w_ret, w_o)


def _mlp_kernel(x_ref, g_ref, wu_ref, wd_ref, gf_ref, o_ref, h_ref, acc_ref, *, final_norm):
    j = pl.program_id(1)

    @pl.when(j == 0)
    def _():
        x = x_ref[...]
        ms = jnp.mean(x * x, axis=-1, keepdims=True)
        h_ref[...] = _bf(x * lax.rsqrt(ms + EPS) * g_ref[...])
        acc_ref[...] = jnp.zeros_like(acc_ref)

    u = jnp.maximum(_dot(h_ref[...], wu_ref[...]), 0.0)
    acc_ref[...] += _dot(_bf(u * u), wd_ref[...])

    @pl.when(j == pl.num_programs(1) - 1)
    def _():
        xn = x_ref[...] + acc_ref[...]
        if final_norm:
            ms = jnp.mean(xn * xn, axis=-1, keepdims=True)
            xn = xn * lax.rsqrt(ms + EPS) * gf_ref[...]
        o_ref[...] = xn


def _mlp(x2d, g, w_up, w_down, g_final, final_norm):
    T = x2d.shape[0]
    tm = min(1024, T)
    tf = 1024
    D = D_MODEL
    return pl.pallas_call(
        functools.partial(_mlp_kernel, final_norm=final_norm),
        grid=(T // tm, D_FF // tf),
        in_specs=[pl.BlockSpec((tm, D), lambda i, j: (i, 0)),
                  pl.BlockSpec((1, D), lambda i, j: (0, 0)),
                  pl.BlockSpec((D, tf), lambda i, j: (0, j)),
                  pl.BlockSpec((tf, D), lambda i, j: (j, 0)),
                  pl.BlockSpec((1, D), lambda i, j: (0, 0))],
        out_specs=pl.BlockSpec((tm, D), lambda i, j: (i, 0)),
        out_shape=jax.ShapeDtypeStruct((T, D), F32),
        scratch_shapes=[pltpu.VMEM((tm, D), BF16), pltpu.VMEM((tm, D), F32)],
        compiler_params=_params(("parallel", "arbitrary")),
        name="mlp",
    )(x2d, g, w_up, w_down, g_final)


def _layer(x2d, B, L, past, caches, lp, wts, g_final, final_norm):
    k_past, v_past, ik_past, s_rw, shift_rw, s_ret = caches
    pos = past + jnp.arange(L, dtype=jnp.int32)
    row = lambda t: t.reshape(1, -1).astype(F32)
    P = _in_proj(x2d, row(lp['norm1_g']), wts['w_in'], wts['layer'])
    q_rot, qi_rot, k_rot, kw_rot = _sa_prep(P, L, _sa_tables(pos))
    y_rw, s_rw_new = _rwkv(P, B, L, shift_rw, s_rw, lp)
    y_sa = _dsa(P, B, L, past, q_rot, qi_rot, k_rot, kw_rot, k_past, v_past, ik_past)
    y_ret, s_ret_new = _retention(P, B, L, pos, s_ret)
    x2d = _merge(x2d, P, y_rw, y_sa, y_ret, wts['w_br_rwkv'], wts['w_br_dsa'], wts['w_br_ret'], wts['w_o'])
    x2d = _mlp(x2d, row(lp['norm2_g']), wts['w_up'], wts['w_down'], g_final, final_norm)
    P3 = P.reshape(B, L, P_COLS)
    last = P3[:, L - 1]
    shift_new = jnp.concatenate([last[:, P_RKV:P_RKV + 3 * RW_WIDTH], last[:, P_LORA:P_LORA + RW_LORA]], axis=1)
    k_new = k_rot.reshape(B, L, SA_KV_HEADS, SA_HEAD_DIM)
    v_new = P3[:, :, P_SAV:P_SAV + SA_KV_WIDTH].reshape(B, L, SA_KV_HEADS, SA_HEAD_DIM)
    ik_new = kw_rot.reshape(B, L, LANES)[:, :, :IDX_DIM]
    return x2d, (k_new, v_new, ik_new, s_rw_new, shift_new, s_ret_new)


def kernel(x_prompt, x_sample, cache_dsa_k, cache_dsa_v, cache_dsa_ik, state_rwkv, state_rwkv_shift, state_ret, norm1_g, w_in, rwkv_mu, rwkv_w0, rwkv_w2, rwkv_a0, rwkv_a2, rwkv_g2, rwkv_k_k, rwkv_k_a, rwkv_r_k, rwkv_lnx_g, w_br_rwkv, w_br_dsa, w_br_ret, w_o, norm2_g, w_up, w_down, final_norm_g):
    params = {
        'norm1_g': norm1_g, 'rwkv_mu': rwkv_mu, 'rwkv_w0': rwkv_w0, 'rwkv_w2': rwkv_w2,
        'rwkv_a0': rwkv_a0, 'rwkv_a2': rwkv_a2, 'rwkv_g2': rwkv_g2, 'rwkv_k_k': rwkv_k_k,
        'rwkv_k_a': rwkv_k_a, 'rwkv_r_k': rwkv_r_k, 'rwkv_lnx_g': rwkv_lnx_g, 'norm2_g': norm2_g,
    }
    depth = w_in.shape[0]
    Bp, Lp, D = x_prompt.shape
    Bs, Ls, _ = x_sample.shape
    past_s = cache_dsa_k.shape[2]
    xp = x_prompt.reshape(Bp * Lp, D).astype(F32)
    xs = x_sample.reshape(Bs * Ls, D).astype(F32)
    g_final = final_norm_g.reshape(1, D).astype(F32)
    zero_p = (None, None, None,
              jnp.zeros((Bp, RW_HEADS, RW_HEAD_DIM, RW_HEAD_DIM), F32), jnp.zeros((Bp, RW_COLS), F32),
              jnp.zeros((Bp, RET_HEADS, RET_HEAD_DIM, RET_HEAD_DIM), F32))
    p_states = [[] for _ in range(6)]
    s_states = [[] for _ in range(6)]
    w_all = _w_prep(w_in)
    for i in range(depth):
        lp = {name: arr[i] for name, arr in params.items()}
        wts = {'w_in': w_all, 'layer': i, 'w_br_rwkv': _bf(w_br_rwkv[i]), 'w_br_dsa': _bf(w_br_dsa[i]),
               'w_br_ret': _bf(w_br_ret[i]), 'w_o': _bf(w_o[i]), 'w_up': _bf(w_up[i]), 'w_down': _bf(w_down[i])}
        final = i == depth - 1
        cache_s = (cache_dsa_k[i], cache_dsa_v[i], cache_dsa_ik[i], state_rwkv[i], state_rwkv_shift[i],
                   state_ret[i])
        xp, new_p = _layer(xp, Bp, Lp, 0, zero_p, lp, wts, g_final, final)
        xs, new_s = _layer(xs, Bs, Ls, past_s, cache_s, lp, wts, g_final, final)
        for j in range(6):
            p_states[j].append(new_p[j])
            s_states[j].append(new_s[j])
    y_prompt = xp.reshape(Bp, Lp, D)
    y_sample = xs.reshape(Bs, Ls, D)
    p_out = [jnp.stack(t, axis=0) for t in p_states]
    s_out = [jnp.stack(t, axis=0) for t in s_states]
    return (y_prompt, y_sample, *p_out, *s_out)
```

```python
import functools

import numpy as np
import jax
import jax.numpy as jnp
from jax import lax
from jax.experimental import pallas as pl
from jax.experimental.pallas import tpu as pltpu

F32 = jnp.float32
BF16 = jnp.bfloat16
I32 = jnp.int32
I16 = jnp.int16

D_MODEL = 1024
CHUNK = 64
Q_BLOCK = 128
EPS = 1e-6

RW_HEADS = 8
RW_HEAD_DIM = 64
RW_WIDTH = RW_HEADS * RW_HEAD_DIM
RW_DECAY_LORA = 64
RW_ICLR_LORA = 64
RW_GATE_LORA = 128
RW_LORA = RW_DECAY_LORA + RW_ICLR_LORA + RW_GATE_LORA
RW_COLS = 3 * RW_WIDTH + RW_LORA
RW_GN_EPS = 64e-5

SA_HEADS = 8
SA_KV_HEADS = 2
SA_HEAD_DIM = 64
SA_WIDTH = SA_HEADS * SA_HEAD_DIM
SA_KV_WIDTH = SA_KV_HEADS * SA_HEAD_DIM
IDX_HEADS = 4
IDX_DIM = 64
IDX_WIDTH = IDX_HEADS * IDX_DIM
TOPK_MAX = 256
ROPE_THETA = 500000.0
ROPE_DIM = SA_HEAD_DIM // 4
INDEX_SCALE = (IDX_DIM ** -0.5) * (IDX_HEADS ** -0.5)
SA_COLS = SA_WIDTH + 2 * SA_KV_WIDTH + IDX_WIDTH + IDX_DIM + IDX_HEADS

RET_HEADS = 4
RET_HEAD_DIM = 128
RET_WIDTH = RET_HEADS * RET_HEAD_DIM
RET_ROPE_BASE = 10000.0
RET_COLS = 4 * RET_WIDTH

N_BRANCH = 3
GATE_COLS = N_BRANCH * D_MODEL
IN_COLS = RW_COLS + SA_COLS + RET_COLS + GATE_COLS
D_FF = 4 * D_MODEL

LANES = 128
TILE_ELEMS = 64 * 8 * LANES

P_GATE = 0
P_RET = P_GATE + GATE_COLS
P_RKV = P_RET + RET_COLS
P_SAQ = P_RKV + 3 * RW_WIDTH
P_LORA = P_SAQ + SA_WIDTH
P_QI = P_LORA + RW_LORA
P_SAK = P_QI + IDX_WIDTH
P_SAV = P_SAK + SA_KV_WIDTH
P_KIWI = P_SAV + SA_KV_WIDTH
P_COLS = 8192
INT_MIN = -2 ** 31
NEG_BIG = -1e30
LOG2_E = 1.4426950408889634
VMEM_LIMIT = 56 * 1024 * 1024


def _bf(x):
    return x.astype(BF16)


def _dot(a, b):
    return jnp.dot(a, b, preferred_element_type=F32)


def _dot_nt(a, b):
    return lax.dot_general(a, b, (((1,), (1,)), ((), ())), preferred_element_type=F32)


def _dot_tn(a, b):
    return lax.dot_general(a, b, (((0,), (0,)), ((), ())), preferred_element_type=F32)


def _dot_split2(a_exact, x):
    hi = _bf(x)
    lo = _bf(x - hi.astype(F32))
    return _dot(a_exact, hi) + _dot(a_exact, lo)


def _params(sem):
    return pltpu.CompilerParams(dimension_semantics=sem, vmem_limit_bytes=VMEM_LIMIT)


def _in_proj_kernel(x_ref, g_ref, w_ref, o_ref, h_ref):
    @pl.when(pl.program_id(1) == 0)
    def _():
        x = x_ref[...]
        ms = jnp.mean(x * x, axis=-1, keepdims=True)
        h_ref[...] = _bf(x * lax.rsqrt(ms + EPS) * g_ref[...])

    o_ref[...] = _dot(h_ref[...], w_ref[0]).astype(o_ref.dtype)


def _in_proj(x2d, g, w_all, layer):
    T = x2d.shape[0]
    tm = min(1024, T)
    tn = 1024
    return pl.pallas_call(
        _in_proj_kernel,
        grid=(T // tm, P_COLS // tn),
        in_specs=[pl.BlockSpec((tm, D_MODEL), lambda i, j: (i, 0)),
                  pl.BlockSpec((1, D_MODEL), lambda i, j: (0, 0)),
                  pl.BlockSpec((1, D_MODEL, tn), lambda i, j: (layer, 0, j))],
        out_specs=pl.BlockSpec((tm, tn), lambda i, j: (i, j)),
        out_shape=jax.ShapeDtypeStruct((T, P_COLS), BF16),
        scratch_shapes=[pltpu.VMEM((tm, D_MODEL), BF16)],
        compiler_params=_params(("parallel", "arbitrary")),
        name="in_proj",
    )(x2d, g, w_all)


KIWI_COLS = IDX_DIM + IDX_HEADS
TAIL0 = RW_COLS + SA_COLS - KIWI_COLS
TAIL_W = (IN_COLS - TAIL0 + LANES - 1) // LANES * LANES


def _w_prep_kernel(w_ref, o_ref):
    x = w_ref[0]
    x = jnp.where(lax.broadcasted_iota(I32, x.shape, 1) < IN_COLS, x, 0.0)
    lane = lax.broadcasted_iota(I32, (x.shape[0], LANES), 1)

    def put(off, v):
        o_ref[0, :, off:off + v.shape[1]] = _bf(v)

    nslab = TAIL_W // LANES
    rolled = [pltpu.roll(x[:, TAIL0 + s * LANES:TAIL0 + (s + 1) * LANES], LANES - KIWI_COLS, 1)
              for s in range(nslab)]
    for s in range((RET_COLS + GATE_COLS) // LANES):
        v = jnp.where(lane < LANES - KIWI_COLS, rolled[s], rolled[s + 1])
        put((P_RET if s < RET_COLS // LANES else P_GATE - RET_COLS) + s * LANES, v)
    sa = RW_COLS
    put(P_RKV, x[:, 0:3 * RW_WIDTH])
    put(P_LORA, x[:, 3 * RW_WIDTH:RW_COLS])
    put(P_SAQ, x[:, sa:sa + SA_WIDTH])
    put(P_SAK, x[:, sa + SA_WIDTH:sa + SA_WIDTH + SA_KV_WIDTH])
    put(P_SAV, x[:, sa + SA_WIDTH + SA_KV_WIDTH:sa + SA_WIDTH + 2 * SA_KV_WIDTH])
    put(P_QI, x[:, sa + SA_WIDTH + 2 * SA_KV_WIDTH:TAIL0])
    put(P_KIWI, jnp.where(lane < KIWI_COLS, x[:, TAIL0:TAIL0 + LANES], 0.0))
    put(P_KIWI + LANES, jnp.zeros((x.shape[0], P_COLS - P_KIWI - LANES), F32))


def _w_prep(w_in):
    depth, d, _ = w_in.shape
    tm = 256
    return pl.pallas_call(
        _w_prep_kernel,
        grid=(depth, d // tm),
        in_specs=[pl.BlockSpec((1, tm, TAIL0 + TAIL_W), lambda l, i: (l, i, 0))],
        out_specs=pl.BlockSpec((1, tm, P_COLS), lambda l, i: (l, i, 0)),
        out_shape=jax.ShapeDtypeStruct((depth, d, P_COLS), BF16),
        compiler_params=_params(("parallel", "parallel")),
        name="w_prep",
    )(w_in)


def _rot_slab(x, c, s_lo, s_hi, shift):
    return x * c + pltpu.roll(x, LANES - shift, 1) * s_lo + pltpu.roll(x, shift, 1) * s_hi


def _sa_prep_kernel(q_ref, qi_ref, k_ref, kw_ref, c_ref, s1_ref, s2_ref, ck_ref, s1k_ref, s2k_ref,
                    qo_ref, qio_ref, ko_ref, kwo_ref):
    c, s1, s2 = c_ref[...], s1_ref[...], s2_ref[...]
    half = ROPE_DIM // 2
    scale = SA_HEAD_DIM ** -0.5 * LOG2_E
    for s in range(SA_WIDTH // LANES):
        sl = slice(s * LANES, (s + 1) * LANES)
        qo_ref[:, sl] = _rot_slab(q_ref[:, sl].astype(F32), c, s1, s2, half) * scale
    for s in range(IDX_WIDTH // LANES):
        sl = slice(s * LANES, (s + 1) * LANES)
        qio_ref[:, sl] = _rot_slab(qi_ref[:, sl].astype(F32), c, s1, s2, half)
    ko_ref[...] = _rot_slab(k_ref[...].astype(F32), c, s1, s2, half)
    kwo_ref[...] = _rot_slab(kw_ref[...].astype(F32), ck_ref[...], s1k_ref[...], s2k_ref[...], half)


def _sa_prep(P, L, tabs):
    T = P.shape[0]
    tm = min(512, L)
    nl = L // tm
    tab_spec = pl.BlockSpec((tm, LANES), lambda i: (i % nl, 0))
    return pl.pallas_call(
        _sa_prep_kernel,
        grid=(T // tm,),
        in_specs=[pl.BlockSpec((tm, SA_WIDTH), lambda i: (i, P_SAQ // SA_WIDTH)),
                  pl.BlockSpec((tm, IDX_WIDTH), lambda i: (i, P_QI // IDX_WIDTH)),
                  pl.BlockSpec((tm, LANES), lambda i: (i, P_SAK // LANES)),
                  pl.BlockSpec((tm, LANES), lambda i: (i, P_KIWI // LANES))] + [tab_spec] * 6,
        out_specs=[pl.BlockSpec((tm, SA_WIDTH), lambda i: (i, 0)),
                   pl.BlockSpec((tm, IDX_WIDTH), lambda i: (i, 0)),
                   pl.BlockSpec((tm, LANES), lambda i: (i, 0)),
                   pl.BlockSpec((tm, LANES), lambda i: (i, 0))],
        out_shape=[jax.ShapeDtypeStruct((T, SA_WIDTH), F32),
                   jax.ShapeDtypeStruct((T, IDX_WIDTH), F32),
                   jax.ShapeDtypeStruct((T, LANES), F32),
                   jax.ShapeDtypeStruct((T, LANES), F32)],
        compiler_params=_params(("parallel",)),
        name="sa_prep",
    )(P, P, P, P, *tabs)


def _sa_tables(pos):
    half = ROPE_DIM // 2
    freqs = 1.0 / (ROPE_THETA ** (jnp.arange(0, ROPE_DIM, 2, dtype=F32) / ROPE_DIM))
    ang = pos.astype(F32)[:, None] * freqs[None, :]
    cos, sin = jnp.cos(ang), jnp.sin(ang)
    n = pos.shape[0]
    pad = SA_HEAD_DIM - ROPE_DIM
    c_head = jnp.concatenate([cos, cos, jnp.ones((n, pad), F32)], axis=1)
    s1_head = jnp.concatenate([-sin, jnp.zeros((n, half + pad), F32)], axis=1)
    s2_head = jnp.concatenate([jnp.zeros((n, half), F32), sin, jnp.zeros((n, pad), F32)], axis=1)
    one, zero = jnp.ones((n, SA_HEAD_DIM), F32), jnp.zeros((n, SA_HEAD_DIM), F32)
    two = lambda t: jnp.concatenate([t, t], axis=1)
    return (two(c_head), two(s1_head), two(s2_head),
            jnp.concatenate([c_head, one], axis=1), jnp.concatenate([s1_head, zero], axis=1),
            jnp.concatenate([s2_head, zero], axis=1))


def _softplus(u):
    return jnp.maximum(u, 0.0) + jnp.log(1.0 + jnp.exp(-jnp.abs(u)))


def _rwkv_kernel(r_ref, k_ref, v_ref, lo_ref, shr_ref, shk_ref, shv_ref, shlo_ref, s0_ref,
                 mur_ref, muk_ref, muv_ref, mulo_ref, w0_ref, w2_ref, a0_ref, a2_ref, g2_ref,
                 kk_ref, ka_ref, rk_ref, lnx_ref, e_ref, tin_ref, tsuf_ref,
                 y_ref, so_ref,
                 cr_ref, ck_ref, cv_ref, clo_ref, s_ref,
                 kap_ref, rt_ref, bh_ref, kh_ref, bt_ref, kt_ref, vv_ref, gc_ref, yy_ref, rr_ref,
                 ac_ref, cc_ref):
    c = pl.program_id(1)
    TT = r_ref.shape[0]
    nch = TT // CHUNK
    N = RW_HEAD_DIM

    @pl.when(c == 0)
    def _():
        cr_ref[0:1, :] = shr_ref[0]
        ck_ref[0:1, :] = shk_ref[0]
        cv_ref[0:1, :] = shv_ref[0]
        clo_ref[0:1, :] = shlo_ref[0]
        s_ref[...] = s0_ref[0]

    def lerp(p_ref, carry_ref, mu_ref):
        p = p_ref[...].astype(F32)
        rolled = pltpu.roll(p, 1, 0)
        row0 = lax.broadcasted_iota(I32, p.shape, 0) == 0
        prev = jnp.where(row0, carry_ref[0:1, :], rolled)
        carry_ref[0:1, :] = p[TT - 1:TT, :]
        return p + (prev - p) * mu_ref[...]

    xr = lerp(r_ref, cr_ref, mur_ref)
    xk = lerp(k_ref, ck_ref, muk_ref)
    xv = lerp(v_ref, cv_ref, muv_ref)
    xlo = lerp(lo_ref, clo_ref, mulo_ref)
    xw = xlo[:, :RW_DECAY_LORA]
    xa = xlo[:, RW_DECAY_LORA:RW_DECAY_LORA + RW_ICLR_LORA]
    xg = xlo[:, RW_DECAY_LORA + RW_ICLR_LORA:]

    z = w0_ref[...] + _dot(_bf(jnp.tanh(xw)), w2_ref[...])
    w = -_softplus(-z) - 0.5
    ld = -jnp.exp(w)
    a = jax.nn.sigmoid(a0_ref[...] + _dot(_bf(xa), a2_ref[...]))
    gate = _dot(_bf(jax.nn.sigmoid(xg)), g2_ref[...])
    e_blk = e_ref[...]
    kk = xk * kk_ref[...]
    kk = kk / jnp.maximum(jnp.sqrt(_dot(_bf(kk * kk), e_blk)), 1e-12)
    k2 = xk * (1.0 + (a - 1.0) * ka_ref[...])
    bb = kk * a
    bonus = _dot(_bf(xr * k2 * rk_ref[...]), e_blk) * xv

    lin = _dot_split2(tin_ref[...], ld)
    lsuf = _dot_split2(tsuf_ref[...], ld)
    e_in = jnp.exp(lin)
    e_ninv = jnp.exp(-lin)
    e_suf = jnp.exp(lsuf)
    kap_ref[...] = kk * jnp.exp(lin - ld)
    rt_ref[...] = xr * e_in
    bh_ref[...] = bb * e_ninv
    kh_ref[...] = k2 * e_ninv
    bt_ref[...] = bb * e_suf
    kt_ref[...] = k2 * e_suf
    vv_ref[...] = xv
    gc_ref[...] = jnp.exp(lin + lsuf)

    ri = lax.broadcasted_iota(I32, (CHUNK, CHUNK), 0)
    ci = lax.broadcasted_iota(I32, (CHUNK, CHUNK), 1)
    strict = ri > ci
    incl = ri >= ci
    eye = (ri == ci).astype(F32)
    heads = range(RW_HEADS)
    hsl = [slice(h * N, (h + 1) * N) for h in heads]

    per_it = next(n for n in (4, 2, 1) if nch % n == 0)

    def coef_body(it, carry):
        chs = [it * per_it + t for t in range(per_it)]
        pairs = [(t, h) for t in range(per_it) for h in heads]
        rows = [pl.ds(pl.multiple_of(ch * CHUNK, CHUNK), CHUNK) for ch in chs]
        ld = lambda ref, p: ref[rows[p[0]], hsl[p[1]]]
        kap = [ld(kap_ref, p) for p in pairs]
        rt = [ld(rt_ref, p) for p in pairs]
        vh = [ld(vv_ref, p) for p in pairs]
        idx = range(len(pairs))
        gmat = [_dot_nt(_bf(jnp.concatenate([kap[n], rt[n]], axis=0)),
                        _bf(jnp.concatenate([ld(bh_ref, pairs[n]), ld(kh_ref, pairs[n])], axis=0)))
                for n in idx]
        n_ab = [jnp.where(strict, g[:CHUNK, :CHUNK], 0.0) for g in gmat]
        m_rb = [jnp.where(incl, g[CHUNK:, :CHUNK], 0.0) for g in gmat]
        m_v = [_bf(jnp.concatenate([jnp.where(strict, g[:CHUNK, CHUNK:], 0.0),
                                    jnp.where(incl, g[CHUNK:, CHUNK:], 0.0)], axis=0)) for g in gmat]
        mv = [_dot(m_v[n], _bf(vh[n])) for n in idx]
        x_inv = [eye - t for t in n_ab]
        pw = n_ab
        for _ in range(5):
            pwb = [_bf(p) for p in pw]
            pw = [_dot(p, p) for p in pwb]
            x_inv = [x + _dot(_bf(x), _bf(p)) for x, p in zip(x_inv, pw)]
        w = [_dot(_bf(x_inv[n]), _bf(jnp.concatenate([kap[n], mv[n][:CHUNK]], axis=1))) for n in idx]
        wb = [_bf(t) for t in w]
        ry = [jnp.concatenate([rt[n], mv[n][CHUNK:]], axis=1) - _dot(_bf(m_rb[n]), wb[n]) for n in idx]
        dmat = [_dot_tn(wb[n], _bf(ld(bt_ref, pairs[n]))) for n in idx]
        vtk = [_dot_tn(_bf(vh[n]), _bf(ld(kt_ref, pairs[n]))) for n in idx]
        for n, (t, h) in enumerate(pairs):
            rr_ref[rows[t], hsl[h]] = ry[n][:, :N]
            yy_ref[rows[t], hsl[h]] = ry[n][:, N:]
            ac_ref[chs[t], h] = -dmat[n][:N]
            cc_ref[chs[t], h] = vtk[n] - dmat[n][N:]
        return carry

    lax.fori_loop(0, nch // per_it, coef_body, 0)

    def state_body(ch, carry):
        rows = pl.ds(pl.multiple_of(ch * CHUNK, CHUNK), CHUNK)
        s_old = [s_ref[h] for h in heads]
        sb = [_bf(t) for t in s_old]
        s_new = [_dot(sb[h], _bf(ac_ref[ch, h])) for h in heads]
        y_c = [_dot_nt(_bf(rr_ref[rows, hsl[h]]), sb[h]) for h in heads]
        for h in heads:
            gch = gc_ref[pl.ds(pl.multiple_of(ch * CHUNK, CHUNK), 1), hsl[h]]
            s_ref[h] = s_old[h] * gch + s_new[h] + cc_ref[ch, h]
            yy_ref[rows, hsl[h]] = yy_ref[rows, hsl[h]] + y_c[h]
        return carry

    lax.fori_loop(0, nch, state_body, 0)

    y = yy_ref[...]
    mean = _dot(_bf(y), e_blk) * (1.0 / N)
    d = y - mean
    var = _dot(_bf(d * d), e_blk) * (1.0 / N)
    yn = d * lax.rsqrt(var + RW_GN_EPS) * lnx_ref[...]
    y_ref[...] = (yn + bonus) * gate

    @pl.when(c == pl.num_programs(1) - 1)
    def _():
        so_ref[0] = s_ref[...]


def _rwkv(P, B, L, shift_prev, s0, lp):
    T = P.shape[0]
    TT = min(256, L)
    nt = L // TT
    W = RW_WIDTH
    row = lambda t: t.reshape(1, -1).astype(F32)
    mu = lp['rwkv_mu']
    sh = shift_prev.astype(F32)
    pieces = lambda t: (t[..., 0:W], t[..., W:2 * W], t[..., 2 * W:3 * W], t[..., 3 * W:])
    mu_r, mu_k, mu_v, mu_lo = [row(t) for t in pieces(mu)]
    sh_r, sh_k, sh_v, sh_lo = [t.reshape(B, 1, -1) for t in pieces(sh)]
    hid = jnp.arange(W) // RW_HEAD_DIM
    e_blk = (hid[:, None] == hid[None, :]).astype(BF16)
    ti = jnp.arange(TT)
    same = (ti[:, None] // CHUNK) == (ti[None, :] // CHUNK)
    tri_in = (same & (ti[None, :] <= ti[:, None])).astype(BF16)
    tri_suf = (same & (ti[None, :] > ti[:, None])).astype(BF16)

    tok = lambda w, blk: pl.BlockSpec((TT, w), lambda b, c: (b * nt + c, blk))
    full = lambda shape: pl.BlockSpec(shape, lambda b, c: (0,) * len(shape))
    shs = lambda w: pl.BlockSpec((1, 1, w), lambda b, c: (b, 0, 0))
    st_spec = pl.BlockSpec((1, RW_HEADS, RW_HEAD_DIM, RW_HEAD_DIM), lambda b, c: (b, 0, 0, 0))
    big = lambda: pltpu.VMEM((TT, W), F32)
    y, s_out = pl.pallas_call(
        _rwkv_kernel,
        grid=(B, nt),
        in_specs=[tok(W, P_RKV // W), tok(W, P_RKV // W + 1), tok(W, P_RKV // W + 2),
                  tok(RW_LORA, P_LORA // RW_LORA),
                  shs(W), shs(W), shs(W), shs(RW_LORA), st_spec,
                  full((1, W)), full((1, W)), full((1, W)), full((1, RW_LORA)),
                  full((1, W)), full((RW_DECAY_LORA, W)), full((1, W)), full((RW_ICLR_LORA, W)),
                  full((RW_GATE_LORA, W)), full((1, W)), full((1, W)), full((1, W)), full((1, W)),
                  full((W, W)), full((TT, TT)), full((TT, TT))],
        out_specs=[pl.BlockSpec((TT, W), lambda b, c: (b * nt + c, 0)), st_spec],
        out_shape=[jax.ShapeDtypeStruct((T, W), F32),
                   jax.ShapeDtypeStruct((B, RW_HEADS, RW_HEAD_DIM, RW_HEAD_DIM), F32)],
        scratch_shapes=[pltpu.VMEM((8, W), F32), pltpu.VMEM((8, W), F32), pltpu.VMEM((8, W), F32),
                        pltpu.VMEM((8, RW_LORA), F32),
                        pltpu.VMEM((RW_HEADS, RW_HEAD_DIM, RW_HEAD_DIM), F32)] + [big() for _ in range(10)]
        + [pltpu.VMEM((TT // CHUNK, RW_HEADS, RW_HEAD_DIM, RW_HEAD_DIM), F32) for _ in range(2)],
        compiler_params=_params(("parallel", "arbitrary")),
        name="rwkv",
    )(P, P, P, P, sh_r, sh_k, sh_v, sh_lo, s0.astype(F32),
      mu_r, mu_k, mu_v, mu_lo, row(lp['rwkv_w0']), _bf(lp['rwkv_w2']), row(lp['rwkv_a0']),
      _bf(lp['rwkv_a2']), _bf(lp['rwkv_g2']), row(lp['rwkv_k_k']), row(lp['rwkv_k_a']),
      row(lp['rwkv_r_k']), row(lp['rwkv_lnx_g']), e_blk, tri_in, tri_suf)
    return y, s_out


ONES_ROWS = 16
BOUND_SLACK = 1.05
SAFE_BOUND = 60.0


def _dsa_kernel(q_ref, qi_ref, kw_ref, k_ref, v_ref, ki_ref, y_ref,
                kb_ref, vt_ref, kib_ref, kmax_ref, keys_ref, hi_ref, lo_ref, bias_ref, pstar_ref, red_ref,
                *, past, lk_real, topk, kc):
    i = pl.program_id(1)
    qb = q_ref.shape[1]
    lk = k_ref.shape[1]
    kf = float(topk)
    HD = SA_HEAD_DIM

    ones_sq = jnp.ones((LANES, LANES), BF16)

    def head_slab(x, odd):
        lane = lax.broadcasted_iota(I32, x.shape, 1)
        return jnp.where(lane < HD, pltpu.roll(x, HD, 1) if odd else x, 0.0)

    @pl.when(i == 0)
    def _():
        kib_ref[...] = _bf(ki_ref[0])
        k = k_ref[0]
        lane = lax.broadcasted_iota(I32, k.shape, 1)
        for c in range(SA_KV_HEADS):
            kc_b = _bf(head_slab(k, c == 1))
            kb_ref[c] = jnp.where(lane == HD, jnp.ones_like(kc_b), kc_b)
            kf32 = kc_b.astype(F32)
            n2 = _dot(_bf(kf32 * kf32), ones_sq)
            kmax_ref[c] = jnp.broadcast_to(jnp.max(n2, axis=0, keepdims=True), (8, LANES))
        vt = v_ref[0].astype(F32).T
        for c in range(SA_KV_HEADS):
            vt_ref[c, 0:HD, :] = _bf(vt[c * HD:(c + 1) * HD, :])
            vt_ref[c, HD:HD + ONES_ROWS, :] = jnp.ones((ONES_ROWS, lk), BF16)

    qpos = past + i * qb + lax.broadcasted_iota(I32, (1, qb), 1)
    limit = jnp.minimum((qpos // CHUNK + 1) * CHUNK, lk_real)
    hi = jnp.minimum(past + (i + 1) * qb, lk_real)
    nk = lax.shift_right_logical(hi + (2 * kc - 1), kc.bit_length())
    sub_idx = lax.broadcasted_iota(I32, (kc, qb), 0)
    kwt = kw_ref[0].T
    wi_h = [kwt[IDX_DIM + h:IDX_DIM + h + 1, :] * INDEX_SCALE for h in range(IDX_HEADS)]
    qi = qi_ref[0]
    qi_h = [_bf(qi[:, h * IDX_DIM:(h + 1) * IDX_DIM]) for h in range(IDX_HEADS)]

    def chunk(j):
        return pl.ds(pl.multiple_of(j * kc, kc), kc)

    def pair(t):
        return (2 * t, 2 * t + 1)

    def score_body(t, carry):
        js = pair(t)
        dots = [[_dot_nt(kib_ref[chunk(j), 0:IDX_DIM], qi_h[h]) for h in range(IDX_HEADS)] for j in js]
        for j, d in zip(js, dots):
            s = jnp.zeros((kc, qb), F32)
            for h in range(IDX_HEADS):
                s = s + jnp.maximum(d[h], 0.0) * wi_h[h]
            s = jnp.where(s == 0.0, 0.0, s)
            bits = lax.bitcast_convert_type(s, I32)
            key = bits ^ ((bits >> 31) & 0x7FFFFFFF)
            key = jnp.where((sub_idx + j * kc) < limit, key, INT_MIN)
            keys_ref[chunk(j), :] = key
            hi_ref[chunk(j), :] = (key >> 16).astype(I16)
        return carry

    lax.fori_loop(0, nk, score_body, 0)

    def fold(m, rows):
        parts = [m[r * rows:(r + 1) * rows] for r in range(kc // rows)]
        while len(parts) > 1:
            parts = [a + b for a, b in zip(parts[0::2], parts[1::2])]
        return parts[0]

    max_trips = lk // (2 * kc)

    def short_trips(body, carry):
        base = jnp.int32(0)
        p = 1 << (max_trips.bit_length() - 1)
        while p:
            take = (nk & p) != 0

            def run(c, base=base, p=p):
                for s in range(p):
                    c = body(base + s, c)
                return c

            carry = lax.cond(take, run, lambda c: c, carry)
            base = base + jnp.where(take, p, 0)
            p //= 2
        return carry

    def col_total(part):
        rows = part.shape[0]
        red_ref[0:rows, :] = part
        parts = [red_ref[r:r + 1, :] for r in range(rows)]
        while len(parts) > 1:
            parts = [a + b for a, b in zip(parts[0::2], parts[1::2])]
        return parts[0]

    def count(pred):
        def body(t, acc):
            for j in pair(t):
                acc = acc + fold(jnp.where(pred(keys_ref[chunk(j), :], j), 1.0, 0.0), 8)
            return acc

        return col_total(short_trips(body, jnp.zeros((8, qb), F32)))

    def count16(ref, pred):
        def body(t, acc):
            for j in pair(t):
                acc = acc + fold(jnp.where(pred(ref[chunk(j), :]), jnp.int16(1), jnp.int16(0)), 16)
            return acc

        acc = short_trips(body, jnp.zeros((16, qb), I16))
        return col_total(acc.astype(F32))

    def kth_bits16(ref, need):
        def bit_body(it, u):
            cand = u | lax.shift_left(jnp.int32(1), 15 - it)
            thr = (cand - 32768).astype(I16)
            return jnp.where(count16(ref, lambda x: x >= thr) >= need, cand, u)

        return lax.fori_loop(0, 16, bit_body, jnp.zeros((1, qb), I32))

    t_hi = kth_bits16(hi_ref, kf) - 32768
    t_hi16 = t_hi.astype(I16)
    need_lo = kf - count16(hi_ref, lambda x: x > t_hi16)

    def lo_body(t, carry):
        for j in pair(t):
            lo = ((keys_ref[chunk(j), :] & 0xFFFF) - 32768).astype(I16)
            lo_ref[chunk(j), :] = jnp.where(hi_ref[chunk(j), :] == t_hi16, lo, jnp.int16(-32768))
        return carry

    short_trips(lo_body, jnp.int32(0))
    tau = t_hi * 65536 + kth_bits16(lo_ref, need_lo)
    cnt_ge = count(lambda kj, j: kj >= tau)
    cnt_gt = count(lambda kj, j: kj > tau)
    need = kf - cnt_gt
    excess = jnp.logical_and(cnt_ge > kf, tau != INT_MIN)
    p_default = jnp.where(tau == INT_MIN, -1, 2 ** 30).astype(I32)
    pstar_ref[...] = jnp.broadcast_to(p_default, pstar_ref.shape)

    @pl.when(jnp.max(jnp.where(excess, 1.0, 0.0)) > 0.0)
    def _():
        nbits = max(1, int(lk - 1).bit_length())

        def idx_body(it, p):
            cand = p | lax.shift_left(jnp.int32(1), nbits - 1 - it)
            g = count(lambda kj, j: jnp.logical_and(kj == tau, (sub_idx + j * kc) < cand))
            return jnp.where(g < need, cand, p)

        p = lax.fori_loop(0, nbits, idx_body, jnp.zeros((1, qb), I32))
        pstar_ref[...] = jnp.broadcast_to(jnp.where(excess, p, p_default), pstar_ref.shape)

    pstar = pstar_ref[0:1, :]

    def bias_body(t, carry):
        for j in pair(t):
            kj = keys_ref[chunk(j), :]
            sel = jnp.logical_or(kj > tau, jnp.logical_and(kj == tau, (sub_idx + j * kc) <= pstar))
            bias_ref[chunk(j), :] = _bf(jnp.where(sel, 0.0, NEG_BIG))
        return carry

    short_trips(bias_body, jnp.int32(0))

    q = q_ref[0]
    group = SA_HEADS // SA_KV_HEADS
    lane_q = lax.broadcasted_iota(I32, (qb, LANES), 1)
    q_heads, bounds = [], []
    for h in range(SA_HEADS):
        qh = _bf(head_slab(q[:, (h // 2) * LANES:(h // 2 + 1) * LANES], h % 2 == 1))
        qf = qh.astype(F32)
        qn2 = _dot(_bf(qf * qf), ones_sq)
        q_heads.append(qh)
        bounds.append(jnp.sqrt(qn2 * kmax_ref[h // group, 0:1, :]) * BOUND_SLACK)
    worst = bounds[0]
    for b in bounds[1:]:
        worst = jnp.maximum(worst, b)
    safe = jnp.max(worst) <= SAFE_BOUND

    zero_acc = jnp.zeros((HD + ONES_ROWS, qb), F32)

    def vt_chunk(c, j):
        return vt_ref[c, :, pl.ds(pl.multiple_of(j * kc, kc), kc)]

    @pl.when(safe)
    def _():
        hs = range(SA_HEADS)
        qa = [jnp.where(lane_q == HD, _bf(-bounds[h]), q_heads[h]) for h in hs]

        def body(t, accs):
            js = pair(t)
            logit = [[_dot_nt(kb_ref[h // group, chunk(j), :], qa[h]) for h in hs] for j in js]
            bias = [bias_ref[chunk(j), :].astype(F32) for j in js]
            p = [[_bf(jnp.exp2(lg + b)) for lg in row] for b, row in zip(bias, logit)]
            pv = [[_dot(vt_chunk(h // group, j), row[h]) for h in hs] for j, row in zip(js, p)]
            return tuple(accs[h] + pv[0][h] + pv[1][h] for h in hs)

        accs = lax.fori_loop(0, nk, body, (zero_acc,) * SA_HEADS)
        for h in hs:
            y_ref[0, h * HD:(h + 1) * HD, :] = accs[h][:HD] / accs[h][HD:HD + 1]

    @pl.when(jnp.logical_not(safe))
    def _():
        for h in range(SA_HEADS):
            c = h // group

            def logit(j):
                return _dot_nt(kb_ref[c, chunk(j), :], q_heads[h]) + bias_ref[chunk(j), :].astype(F32)

            def max_body(t, m):
                for j in pair(t):
                    m = jnp.maximum(m, jnp.max(logit(j), axis=0, keepdims=True))
                return m

            m = lax.fori_loop(0, nk, max_body, jnp.full((1, qb), NEG_BIG, F32))

            def sum_body(t, acc):
                for j in pair(t):
                    acc = acc + _dot(vt_chunk(c, j), _bf(jnp.exp2(logit(j) - m)))
                return acc

            acc = lax.fori_loop(0, nk, sum_body, zero_acc)
            y_ref[0, h * HD:(h + 1) * HD, :] = acc[:HD] / acc[HD:HD + 1]


def _dsa_call(q3, qi3, kw3, k_all, v_all, ki_all, v_blk, *, lk, past, lk_real, topk, qb, kc):
    B, n_q = q3.shape[:2]
    kern = functools.partial(_dsa_kernel, past=past, lk_real=lk_real, topk=topk, kc=kc)
    return pl.pallas_call(
        kern,
        grid=(B, n_q // qb),
        in_specs=[pl.BlockSpec((1, qb, SA_WIDTH), lambda b, i: (b, i, 0)),
                  pl.BlockSpec((1, qb, IDX_WIDTH), lambda b, i: (b, i, 0)),
                  pl.BlockSpec((1, qb, LANES), lambda b, i: (b, i, 0)),
                  pl.BlockSpec((1, lk, LANES), lambda b, i: (b, 0, 0), pipeline_mode=pl.Buffered(1)),
                  pl.BlockSpec((1, lk, LANES), lambda b, i: (b, 0, v_blk), pipeline_mode=pl.Buffered(1)),
                  pl.BlockSpec((1, lk, LANES), lambda b, i: (b, 0, 0), pipeline_mode=pl.Buffered(1))],
        out_specs=pl.BlockSpec((1, SA_WIDTH, qb), lambda b, i: (b, 0, i)),
        out_shape=jax.ShapeDtypeStruct((B, SA_WIDTH, n_q), F32),
        scratch_shapes=[pltpu.VMEM((SA_KV_HEADS, lk, LANES), BF16),
                        pltpu.VMEM((SA_KV_HEADS, SA_HEAD_DIM + ONES_ROWS, lk), BF16),
                        pltpu.VMEM((lk, LANES), BF16),
                        pltpu.VMEM((SA_KV_HEADS, 8, LANES), F32),
                        pltpu.VMEM((lk, qb), I32), pltpu.VMEM((lk, qb), I16), pltpu.VMEM((lk, qb), I16),
                        pltpu.VMEM((lk, qb), BF16),
                        pltpu.VMEM((8, qb), I32), pltpu.VMEM((16, qb), F32)],
        compiler_params=_params(("parallel", "arbitrary")),
        name="dsa",
    )(q3, qi3, kw3, k_all, v_all, ki_all)


def _round_up(x, m):
    return (x + m - 1) // m * m


def _dsa(P, B, L, past, q_rot, qi_rot, k_rot, kw_rot, k_past, v_past, ik_past):
    lk_real = past + L
    topk = min(TOPK_MAX, lk_real // 4)
    qb = 2 * LANES if (past == 0 and L % (2 * LANES) == 0) else LANES
    lq = _round_up(L, qb)
    qpad = lambda t: t if lq == L else jnp.pad(t, ((0, 0), (0, lq - L), (0, 0)))
    q3 = qpad(q_rot.reshape(B, L, SA_WIDTH))
    qi3 = qpad(qi_rot.reshape(B, L, IDX_WIDTH))
    kw3 = kw_rot.reshape(B, L, LANES)
    k3 = k_rot.reshape(B, L, LANES)
    common = dict(past=past, lk_real=lk_real, topk=topk, qb=qb)
    kc = TILE_ELEMS // qb // 2 if qb == LANES else TILE_ELEMS // qb
    if past == 0:
        while L % (2 * kc):
            kc //= 2
        P3 = P.reshape(B, L, P_COLS)
        return _dsa_call(q3, qi3, kw3, k3, P3, kw3, P_SAV // LANES, lk=L, kc=kc, **common)
    lk = _round_up(lk_real, 2 * kc)
    zpad = jnp.zeros((B, lk - lk_real, LANES), F32)
    v_new = P.reshape(B, L, P_COLS)[:, :, P_SAV:P_SAV + LANES].astype(F32)
    ik_p = jnp.concatenate([ik_past.astype(F32), jnp.zeros((B, past, LANES - IDX_DIM), F32)], axis=2)
    k_all = jnp.concatenate([k_past.reshape(B, past, LANES).astype(F32), k3, zpad], axis=1)
    v_all = jnp.concatenate([v_past.reshape(B, past, LANES).astype(F32), v_new, zpad], axis=1)
    ki_all = jnp.concatenate([ik_p, kw3, zpad], axis=1)
    y = _dsa_call(q3, qi3, qpad(kw3), k_all, v_all, ki_all, 0, lk=lk, kc=kc, **common)
    return y[:, :, :L]


def _ret_kernel(q_ref, k_ref, v_ref, g_ref, cos_ref, sin_ref, dm_ref, qd_ref, kd_ref, gc_ref, s0_ref,
                y_ref, so_ref, s_ref):
    c = pl.program_id(1)
    D = RET_HEAD_DIM
    cs = dm_ref.shape[1]
    nch = q_ref.shape[0] // cs
    heads = range(RET_HEADS)
    hsl = [slice(h * D, (h + 1) * D) for h in heads]
    rows = [slice(ch * cs, (ch + 1) * cs) for ch in range(nch)]
    prob = [(ch, h) for ch in range(nch) for h in heads]

    @pl.when(c == 0)
    def _():
        s_ref[...] = s0_ref[0]

    cos, sin = cos_ref[...], sin_ref[...]
    qb_, kb_, kd_, vb_ = [], [], [], []
    for h in heads:
        q = q_ref[:, hsl[h]].astype(F32)
        k = k_ref[:, hsl[h]].astype(F32)
        k = (k * cos + pltpu.roll(k, D // 2, 1) * sin) * (D ** -0.5)
        qb_.append(_bf(q * cos + pltpu.roll(q, D // 2, 1) * sin))
        kb_.append(_bf(k))
        kd_.append([_bf(k[r] * kd_ref[:, hsl[h]]) for r in rows])
        vb_.append(_bf(v_ref[:, hsl[h]]))
    scores = {(ch, h): _dot_nt(qb_[h][rows[ch]], kb_[h][rows[ch]]) * dm_ref[h] for ch, h in prob}
    ktv = {(ch, h): _dot_tn(kd_[h][ch], vb_[h][rows[ch]]) for ch, h in prob}
    intra = {(ch, h): _dot(_bf(scores[ch, h]), vb_[h][rows[ch]]) for ch, h in prob}
    s_at = {}
    for h in heads:
        s = s_ref[h]
        for ch in range(nch):
            s_at[ch, h] = _bf(s)
            s = s * gc_ref[:, hsl[h]] + ktv[ch, h]
        s_ref[h] = s
    cross = {(ch, h): _dot(qb_[h][rows[ch]], s_at[ch, h]) for ch, h in prob}
    for ch in range(nch):
        outs = []
        for h in heads:
            o = intra[ch, h] + cross[ch, h] * qd_ref[:, hsl[h]]
            o = o * lax.rsqrt(jnp.mean(o * o, axis=-1, keepdims=True) + EPS)
            outs.append(jax.nn.silu(g_ref[rows[ch], hsl[h]].astype(F32)) * o)
        y_ref[rows[ch], :] = jnp.concatenate(outs, axis=1)

    @pl.when(c == pl.num_programs(1) - 1)
    def _():
        so_ref[0] = s_ref[...]


def _retention(P, B, L, pos, s0):
    T = P.shape[0]
    c = min(CHUNK, L)
    nc = L // c
    D = RET_HEAD_DIM
    freqs = 1.0 / (RET_ROPE_BASE ** jnp.linspace(0.0, 1.0, D // 2, dtype=F32))
    ang = pos.astype(F32)[:, None] * freqs[None, :]
    cos = jnp.concatenate([jnp.cos(ang)] * 2, axis=1)
    sin = jnp.concatenate([-jnp.sin(ang), jnp.sin(ang)], axis=1)
    log_gamma = jnp.log1p(-jnp.exp2(-5.0 - jnp.arange(RET_HEADS, dtype=F32)))
    idx = jnp.arange(c, dtype=F32)
    dmask = jnp.exp(jnp.abs(idx[:, None] - idx[None, :])[None] * log_gamma[:, None, None])
    lanes = lambda t: jnp.repeat(t, D, axis=1)
    qdec = lanes(jnp.exp((idx[:, None] + 1.0) * log_gamma[None, :]))
    kdec = lanes(jnp.exp((c - 1.0 - idx)[:, None] * log_gamma[None, :]))
    gchunk = lanes(jnp.exp(c * log_gamma)[None, :])

    W = RET_WIDTH
    tt = min(4 * c, L)
    nt = L // tt
    tok = lambda blk: pl.BlockSpec((tt, W), lambda b, i: (b * nt + i, blk))
    full = lambda shape: pl.BlockSpec(shape, lambda b, i: (0,) * len(shape))
    st_spec = pl.BlockSpec((1, RET_HEADS, D, D), lambda b, i: (b, 0, 0, 0))
    y, s_out = pl.pallas_call(
        _ret_kernel,
        grid=(B, nt),
        in_specs=[tok(P_RET // W), tok(P_RET // W + 1), tok(P_RET // W + 2), tok(P_RET // W + 3),
                  pl.BlockSpec((tt, D), lambda b, i: (i, 0)), pl.BlockSpec((tt, D), lambda b, i: (i, 0)),
                  full((RET_HEADS, c, c)), full((c, W)), full((c, W)), full((1, W)), st_spec],
        out_specs=[pl.BlockSpec((tt, W), lambda b, i: (b * nt + i, 0)), st_spec],
        out_shape=[jax.ShapeDtypeStruct((T, W), F32), jax.ShapeDtypeStruct((B, RET_HEADS, D, D), F32)],
        scratch_shapes=[pltpu.VMEM((RET_HEADS, D, D), F32)],
        compiler_params=_params(("parallel", "arbitrary")),
        name="retention",
    )(P, P, P, P, cos, sin, dmask, qdec, kdec, gchunk, s0.astype(F32))
    return y, s_out


def _merge_kernel(x_ref, g0_ref, g1_ref, g2_ref, yr_ref, ys_ref, yt_ref, wr_ref, ws_ref, wt_ref, wo_ref,
                  o_ref):
    gate = lambda g_ref: jax.nn.sigmoid(g_ref[...].astype(F32))
    m = (gate(g0_ref) * _dot(_bf(yr_ref[...]), wr_ref[...])
         + gate(g1_ref) * _dot_tn(_bf(ys_ref[0]), ws_ref[...])
         + gate(g2_ref) * _dot(_bf(yt_ref[...]), wt_ref[...]))
    o_ref[...] = x_ref[...] + _dot(_bf(m), wo_ref[...])


def _merge(x2d, P, y_rw, y_sa_t, y_ret, w_rw, w_sa, w_ret, w_o):
    T = x2d.shape[0]
    L = y_sa_t.shape[2]
    tm = min(512, L)
    nl = L // tm
    D = D_MODEL
    tok = lambda w, blk: pl.BlockSpec((tm, w), lambda i: (i, blk))
    full = lambda shape: pl.BlockSpec(shape, lambda i: (0, 0))
    return pl.pallas_call(
        _merge_kernel,
        grid=(T // tm,),
        in_specs=[tok(D, 0), tok(D, 0), tok(D, 1), tok(D, 2), tok(RW_WIDTH, 0),
                  pl.BlockSpec((1, SA_WIDTH, tm), lambda i: (i // nl, 0, i % nl)),
                  tok(RET_WIDTH, 0), full((RW_WIDTH, D)), full((SA_WIDTH, D)), full((RET_WIDTH, D)),
                  full((D, D))],
        out_specs=tok(D, 0),
        out_shape=jax.ShapeDtypeStruct((T, D), F32),
        compiler_params=_params(("parallel",)),
        name="merge",
    )(x2d, P, P, P, y_rw, y_sa_t, y_ret, w_rw, w_sa, w_ret, w_o)


def _mlp_kernel(x_ref, g_ref, wu_ref, wd_ref, gf_ref, o_ref, h_ref, acc_ref, *, final_norm):
    j = pl.program_id(1)

    @pl.when(j == 0)
    def _():
        x = x_ref[...]
        ms = jnp.mean(x * x, axis=-1, keepdims=True)
        h_ref[...] = _bf(x * lax.rsqrt(ms + EPS) * g_ref[...])
        acc_ref[...] = jnp.zeros_like(acc_ref)

    u = jnp.maximum(_dot(h_ref[...], wu_ref[...]), 0.0)
    acc_ref[...] += _dot(_bf(u * u), wd_ref[...])

    @pl.when(j == pl.num_programs(1) - 1)
    def _():
        xn = x_ref[...] + acc_ref[...]
        if final_norm:
            ms = jnp.mean(xn * xn, axis=-1, keepdims=True)
            xn = xn * lax.rsqrt(ms + EPS) * gf_ref[...]
        o_ref[...] = xn


def _mlp(x2d, g, w_up, w_down, g_final, final_norm):
    T = x2d.shape[0]
    tm = min(1024, T)
    tf = 1024
    D = D_MODEL
    return pl.pallas_call(
        functools.partial(_mlp_kernel, final_norm=final_norm),
        grid=(T // tm, D_FF // tf),
        in_specs=[pl.BlockSpec((tm, D), lambda i, j: (i, 0)),
                  pl.BlockSpec((1, D), lambda i, j: (0, 0)),
                  pl.BlockSpec((D, tf), lambda i, j: (0, j)),
                  pl.BlockSpec((tf, D), lambda i, j: (j, 0)),
                  pl.BlockSpec((1, D), lambda i, j: (0, 0))],
        out_specs=pl.BlockSpec((tm, D), lambda i, j: (i, 0)),
        out_shape=jax.ShapeDtypeStruct((T, D), F32),
        scratch_shapes=[pltpu.VMEM((tm, D), BF16), pltpu.VMEM((tm, D), F32)],
        compiler_params=_params(("parallel", "arbitrary")),
        name="mlp",
    )(x2d, g, w_up, w_down, g_final)


def _layer(x2d, B, L, past, caches, lp, wts, g_final, final_norm):
    k_past, v_past, ik_past, s_rw, shift_rw, s_ret = caches
    pos = past + jnp.arange(L, dtype=jnp.int32)
    row = lambda t: t.reshape(1, -1).astype(F32)
    P = _in_proj(x2d, row(lp['norm1_g']), wts['w_in'], wts['layer'])
    q_rot, qi_rot, k_rot, kw_rot = _sa_prep(P, L, _sa_tables(pos))
    y_rw, s_rw_new = _rwkv(P, B, L, shift_rw, s_rw, lp)
    y_sa = _dsa(P, B, L, past, q_rot, qi_rot, k_rot, kw_rot, k_past, v_past, ik_past)
    y_ret, s_ret_new = _retention(P, B, L, pos, s_ret)
    x2d = _merge(x2d, P, y_rw, y_sa, y_ret, wts['w_br_rwkv'], wts['w_br_dsa'], wts['w_br_ret'], wts['w_o'])
    x2d = _mlp(x2d, row(lp['norm2_g']), wts['w_up'], wts['w_down'], g_final, final_norm)
    P3 = P.reshape(B, L, P_COLS)
    last = P3[:, L - 1].astype(F32)
    shift_new = jnp.concatenate([last[:, P_RKV:P_RKV + 3 * RW_WIDTH], last[:, P_LORA:P_LORA + RW_LORA]], axis=1)
    k_new = k_rot.reshape(B, L, SA_KV_HEADS, SA_HEAD_DIM)
    v_new = P3[:, :, P_SAV:P_SAV + SA_KV_WIDTH].astype(F32).reshape(B, L, SA_KV_HEADS, SA_HEAD_DIM)
    ik_new = kw_rot.reshape(B, L, LANES)[:, :, :IDX_DIM]
    return x2d, (k_new, v_new, ik_new, s_rw_new, shift_new, s_ret_new)


def kernel(x_prompt, x_sample, cache_dsa_k, cache_dsa_v, cache_dsa_ik, state_rwkv, state_rwkv_shift, state_ret, norm1_g, w_in, rwkv_mu, rwkv_w0, rwkv_w2, rwkv_a0, rwkv_a2, rwkv_g2, rwkv_k_k, rwkv_k_a, rwkv_r_k, rwkv_lnx_g, w_br_rwkv, w_br_dsa, w_br_ret, w_o, norm2_g, w_up, w_down, final_norm_g):
    params = {
        'norm1_g': norm1_g, 'rwkv_mu': rwkv_mu, 'rwkv_w0': rwkv_w0, 'rwkv_w2': rwkv_w2,
        'rwkv_a0': rwkv_a0, 'rwkv_a2': rwkv_a2, 'rwkv_g2': rwkv_g2, 'rwkv_k_k': rwkv_k_k,
        'rwkv_k_a': rwkv_k_a, 'rwkv_r_k': rwkv_r_k, 'rwkv_lnx_g': rwkv_lnx_g, 'norm2_g': norm2_g,
    }
    depth = w_in.shape[0]
    Bp, Lp, D = x_prompt.shape
    Bs, Ls, _ = x_sample.shape
    past_s = cache_dsa_k.shape[2]
    xp = x_prompt.reshape(Bp * Lp, D).astype(F32)
    xs = x_sample.reshape(Bs * Ls, D).astype(F32)
    g_final = final_norm_g.reshape(1, D).astype(F32)
    zero_p = (None, None, None,
              jnp.zeros((Bp, RW_HEADS, RW_HEAD_DIM, RW_HEAD_DIM), F32), jnp.zeros((Bp, RW_COLS), F32),
              jnp.zeros((Bp, RET_HEADS, RET_HEAD_DIM, RET_HEAD_DIM), F32))
    p_states = [[] for _ in range(6)]
    s_states = [[] for _ in range(6)]
    w_all = _w_prep(w_in)
    for i in range(depth):
        lp = {name: arr[i] for name, arr in params.items()}
        wts = {'w_in': w_all, 'layer': i, 'w_br_rwkv': _bf(w_br_rwkv[i]), 'w_br_dsa': _bf(w_br_dsa[i]),
               'w_br_ret': _bf(w_br_ret[i]), 'w_o': _bf(w_o[i]), 'w_up': _bf(w_up[i]), 'w_down': _bf(w_down[i])}
        final = i == depth - 1
        cache_s = (cache_dsa_k[i], cache_dsa_v[i], cache_dsa_ik[i], state_rwkv[i], state_rwkv_shift[i],
                   state_ret[i])
        xp, new_p = _layer(xp, Bp, Lp, 0, zero_p, lp, wts, g_final, final)
        xs, new_s = _layer(xs, Bs, Ls, past_s, cache_s, lp, wts, g_final, final)
        for j in range(6):
            p_states[j].append(new_p[j])
            s_states[j].append(new_s[j])
    y_prompt = xp.reshape(Bp, Lp, D)
    y_sample = xs.reshape(Bs, Ls, D)
    p_out = [jnp.stack(t, axis=0) for t in p_states]
    s_out = [jnp.stack(t, axis=0) for t in s_states]
    return (y_prompt, y_sample, *p_out, *s_out)
```

```python
import functools

import numpy as np
import jax
import jax.numpy as jnp
from jax import lax
from jax.experimental import pallas as pl
from jax.experimental.pallas import tpu as pltpu

F32 = jnp.float32
BF16 = jnp.bfloat16
I32 = jnp.int32
I16 = jnp.int16

D_MODEL = 1024
CHUNK = 64
Q_BLOCK = 128
EPS = 1e-6

RW_HEADS = 8
RW_HEAD_DIM = 64
RW_WIDTH = RW_HEADS * RW_HEAD_DIM
RW_DECAY_LORA = 64
RW_ICLR_LORA = 64
RW_GATE_LORA = 128
RW_LORA = RW_DECAY_LORA + RW_ICLR_LORA + RW_GATE_LORA
RW_COLS = 3 * RW_WIDTH + RW_LORA
RW_GN_EPS = 64e-5

SA_HEADS = 8
SA_KV_HEADS = 2
SA_HEAD_DIM = 64
SA_WIDTH = SA_HEADS * SA_HEAD_DIM
SA_KV_WIDTH = SA_KV_HEADS * SA_HEAD_DIM
IDX_HEADS = 4
IDX_DIM = 64
IDX_WIDTH = IDX_HEADS * IDX_DIM
TOPK_MAX = 256
ROPE_THETA = 500000.0
ROPE_DIM = SA_HEAD_DIM // 4
INDEX_SCALE = (IDX_DIM ** -0.5) * (IDX_HEADS ** -0.5)
SA_COLS = SA_WIDTH + 2 * SA_KV_WIDTH + IDX_WIDTH + IDX_DIM + IDX_HEADS

RET_HEADS = 4
RET_HEAD_DIM = 128
RET_WIDTH = RET_HEADS * RET_HEAD_DIM
RET_ROPE_BASE = 10000.0
RET_COLS = 4 * RET_WIDTH

N_BRANCH = 3
GATE_COLS = N_BRANCH * D_MODEL
IN_COLS = RW_COLS + SA_COLS + RET_COLS + GATE_COLS
D_FF = 4 * D_MODEL

LANES = 128
TILE_ELEMS = 64 * 8 * LANES

P_GATE = 0
P_RET = P_GATE + GATE_COLS
P_RKV = P_RET + RET_COLS
P_SAQ = P_RKV + 3 * RW_WIDTH
P_LORA = P_SAQ + SA_WIDTH
P_QI = P_LORA + RW_LORA
P_SAK = P_QI + IDX_WIDTH
P_SAV = P_SAK + SA_KV_WIDTH
P_KIWI = P_SAV + SA_KV_WIDTH
P_COLS = 8192
INT_MIN = -2 ** 31
NEG_BIG = -1e30
LOG2_E = 1.4426950408889634
VMEM_LIMIT = 56 * 1024 * 1024


def _bf(x):
    return x.astype(BF16)


def _dot(a, b):
    return jnp.dot(a, b, preferred_element_type=F32)


def _dot_nt(a, b):
    return lax.dot_general(a, b, (((1,), (1,)), ((), ())), preferred_element_type=F32)


def _dot_tn(a, b):
    return lax.dot_general(a, b, (((0,), (0,)), ((), ())), preferred_element_type=F32)


def _dot_split2(a_exact, x):
    hi = _bf(x)
    lo = _bf(x - hi.astype(F32))
    return _dot(a_exact, hi) + _dot(a_exact, lo)


def _params(sem):
    return pltpu.CompilerParams(dimension_semantics=sem, vmem_limit_bytes=VMEM_LIMIT)


def _in_proj_kernel(x_ref, g_ref, w_ref, o_ref, h_ref):
    @pl.when(pl.program_id(1) == 0)
    def _():
        x = x_ref[...]
        ms = jnp.mean(x * x, axis=-1, keepdims=True)
        h_ref[...] = _bf(x * lax.rsqrt(ms + EPS) * g_ref[...])

    o_ref[...] = _dot(h_ref[...], w_ref[0]).astype(o_ref.dtype)


def _in_proj(x2d, g, w_all, layer):
    T = x2d.shape[0]
    tm = min(1024, T)
    tn = 1024
    return pl.pallas_call(
        _in_proj_kernel,
        grid=(T // tm, P_COLS // tn),
        in_specs=[pl.BlockSpec((tm, D_MODEL), lambda i, j: (i, 0)),
                  pl.BlockSpec((1, D_MODEL), lambda i, j: (0, 0)),
                  pl.BlockSpec((1, D_MODEL, tn), lambda i, j: (layer, 0, j))],
        out_specs=pl.BlockSpec((tm, tn), lambda i, j: (i, j)),
        out_shape=jax.ShapeDtypeStruct((T, P_COLS), BF16),
        scratch_shapes=[pltpu.VMEM((tm, D_MODEL), BF16)],
        compiler_params=_params(("parallel", "arbitrary")),
        name="in_proj",
    )(x2d, g, w_all)


KIWI_COLS = IDX_DIM + IDX_HEADS
TAIL0 = RW_COLS + SA_COLS - KIWI_COLS
TAIL_W = (IN_COLS - TAIL0 + LANES - 1) // LANES * LANES


def _w_prep_kernel(w_ref, o_ref):
    x = w_ref[0]
    x = jnp.where(lax.broadcasted_iota(I32, x.shape, 1) < IN_COLS, x, 0.0)
    lane = lax.broadcasted_iota(I32, (x.shape[0], LANES), 1)

    def put(off, v):
        o_ref[0, :, off:off + v.shape[1]] = _bf(v)

    nslab = TAIL_W // LANES
    rolled = [pltpu.roll(x[:, TAIL0 + s * LANES:TAIL0 + (s + 1) * LANES], LANES - KIWI_COLS, 1)
              for s in range(nslab)]
    for s in range((RET_COLS + GATE_COLS) // LANES):
        v = jnp.where(lane < LANES - KIWI_COLS, rolled[s], rolled[s + 1])
        put((P_RET if s < RET_COLS // LANES else P_GATE - RET_COLS) + s * LANES, v)
    sa = RW_COLS
    put(P_RKV, x[:, 0:3 * RW_WIDTH])
    put(P_LORA, x[:, 3 * RW_WIDTH:RW_COLS])
    put(P_SAQ, x[:, sa:sa + SA_WIDTH])
    put(P_SAK, x[:, sa + SA_WIDTH:sa + SA_WIDTH + SA_KV_WIDTH])
    put(P_SAV, x[:, sa + SA_WIDTH + SA_KV_WIDTH:sa + SA_WIDTH + 2 * SA_KV_WIDTH])
    put(P_QI, x[:, sa + SA_WIDTH + 2 * SA_KV_WIDTH:TAIL0])
    put(P_KIWI, jnp.where(lane < KIWI_COLS, x[:, TAIL0:TAIL0 + LANES], 0.0))
    put(P_KIWI + LANES, jnp.zeros((x.shape[0], P_COLS - P_KIWI - LANES), F32))


def _w_prep(w_in):
    depth, d, _ = w_in.shape
    tm = 256
    return pl.pallas_call(
        _w_prep_kernel,
        grid=(depth, d // tm),
        in_specs=[pl.BlockSpec((1, tm, TAIL0 + TAIL_W), lambda l, i: (l, i, 0))],
        out_specs=pl.BlockSpec((1, tm, P_COLS), lambda l, i: (l, i, 0)),
        out_shape=jax.ShapeDtypeStruct((depth, d, P_COLS), BF16),
        compiler_params=_params(("parallel", "parallel")),
        name="w_prep",
    )(w_in)


def _rot_slab(x, c, s_lo, s_hi, shift):
    return x * c + pltpu.roll(x, LANES - shift, 1) * s_lo + pltpu.roll(x, shift, 1) * s_hi


def _sa_prep_kernel(q_ref, qi_ref, k_ref, kw_ref, c_ref, s1_ref, s2_ref, ck_ref, s1k_ref, s2k_ref,
                    qo_ref, qio_ref, ko_ref, kwo_ref):
    c, s1, s2 = c_ref[...], s1_ref[...], s2_ref[...]
    half = ROPE_DIM // 2
    scale = SA_HEAD_DIM ** -0.5 * LOG2_E
    for s in range(SA_WIDTH // LANES):
        sl = slice(s * LANES, (s + 1) * LANES)
        qo_ref[:, sl] = _rot_slab(q_ref[:, sl].astype(F32), c, s1, s2, half) * scale
    for s in range(IDX_WIDTH // LANES):
        sl = slice(s * LANES, (s + 1) * LANES)
        qio_ref[:, sl] = _rot_slab(qi_ref[:, sl].astype(F32), c, s1, s2, half)
    ko_ref[...] = _rot_slab(k_ref[...].astype(F32), c, s1, s2, half)
    kwo_ref[...] = _rot_slab(kw_ref[...].astype(F32), ck_ref[...], s1k_ref[...], s2k_ref[...], half)


def _sa_prep(P, L, tabs):
    T = P.shape[0]
    tm = min(512, L)
    nl = L // tm
    tab_spec = pl.BlockSpec((tm, LANES), lambda i: (i % nl, 0))
    return pl.pallas_call(
        _sa_prep_kernel,
        grid=(T // tm,),
        in_specs=[pl.BlockSpec((tm, SA_WIDTH), lambda i: (i, P_SAQ // SA_WIDTH)),
                  pl.BlockSpec((tm, IDX_WIDTH), lambda i: (i, P_QI // IDX_WIDTH)),
                  pl.BlockSpec((tm, LANES), lambda i: (i, P_SAK // LANES)),
                  pl.BlockSpec((tm, LANES), lambda i: (i, P_KIWI // LANES))] + [tab_spec] * 6,
        out_specs=[pl.BlockSpec((tm, SA_WIDTH), lambda i: (i, 0)),
                   pl.BlockSpec((tm, IDX_WIDTH), lambda i: (i, 0)),
                   pl.BlockSpec((tm, LANES), lambda i: (i, 0)),
                   pl.BlockSpec((tm, LANES), lambda i: (i, 0))],
        out_shape=[jax.ShapeDtypeStruct((T, SA_WIDTH), F32),
                   jax.ShapeDtypeStruct((T, IDX_WIDTH), F32),
                   jax.ShapeDtypeStruct((T, LANES), F32),
                   jax.ShapeDtypeStruct((T, LANES), F32)],
        compiler_params=_params(("parallel",)),
        name="sa_prep",
    )(P, P, P, P, *tabs)


def _sa_tables(pos):
    half = ROPE_DIM // 2
    freqs = 1.0 / (ROPE_THETA ** (jnp.arange(0, ROPE_DIM, 2, dtype=F32) / ROPE_DIM))
    ang = pos.astype(F32)[:, None] * freqs[None, :]
    cos, sin = jnp.cos(ang), jnp.sin(ang)
    n = pos.shape[0]
    pad = SA_HEAD_DIM - ROPE_DIM
    c_head = jnp.concatenate([cos, cos, jnp.ones((n, pad), F32)], axis=1)
    s1_head = jnp.concatenate([-sin, jnp.zeros((n, half + pad), F32)], axis=1)
    s2_head = jnp.concatenate([jnp.zeros((n, half), F32), sin, jnp.zeros((n, pad), F32)], axis=1)
    one, zero = jnp.ones((n, SA_HEAD_DIM), F32), jnp.zeros((n, SA_HEAD_DIM), F32)
    two = lambda t: jnp.concatenate([t, t], axis=1)
    return (two(c_head), two(s1_head), two(s2_head),
            jnp.concatenate([c_head, one], axis=1), jnp.concatenate([s1_head, zero], axis=1),
            jnp.concatenate([s2_head, zero], axis=1))


def _softplus(u):
    return jnp.maximum(u, 0.0) + jnp.log(1.0 + jnp.exp(-jnp.abs(u)))


def _rwkv_kernel(r_ref, k_ref, v_ref, lo_ref, shr_ref, shk_ref, shv_ref, shlo_ref, s0_ref,
                 mur_ref, muk_ref, muv_ref, mulo_ref, w0_ref, w2_ref, a0_ref, a2_ref, g2_ref,
                 kk_ref, ka_ref, rk_ref, lnx_ref, e_ref, tin_ref, tsuf_ref,
                 y_ref, so_ref,
                 cr_ref, ck_ref, cv_ref, clo_ref, s_ref,
                 kap_ref, rt_ref, bh_ref, kh_ref, bt_ref, kt_ref, vv_ref, gc_ref, yy_ref, rr_ref,
                 ac_ref, cc_ref):
    c = pl.program_id(1)
    TT = r_ref.shape[0]
    nch = TT // CHUNK
    N = RW_HEAD_DIM

    @pl.when(c == 0)
    def _():
        cr_ref[0:1, :] = shr_ref[0]
        ck_ref[0:1, :] = shk_ref[0]
        cv_ref[0:1, :] = shv_ref[0]
        clo_ref[0:1, :] = shlo_ref[0]
        s_ref[...] = s0_ref[0]

    def lerp(p_ref, carry_ref, mu_ref):
        p = p_ref[...].astype(F32)
        rolled = pltpu.roll(p, 1, 0)
        row0 = lax.broadcasted_iota(I32, p.shape, 0) == 0
        prev = jnp.where(row0, carry_ref[0:1, :], rolled)
        carry_ref[0:1, :] = p[TT - 1:TT, :]
        return p + (prev - p) * mu_ref[...]

    xr = lerp(r_ref, cr_ref, mur_ref)
    xk = lerp(k_ref, ck_ref, muk_ref)
    xv = lerp(v_ref, cv_ref, muv_ref)
    xlo = lerp(lo_ref, clo_ref, mulo_ref)
    xw = xlo[:, :RW_DECAY_LORA]
    xa = xlo[:, RW_DECAY_LORA:RW_DECAY_LORA + RW_ICLR_LORA]
    xg = xlo[:, RW_DECAY_LORA + RW_ICLR_LORA:]

    z = w0_ref[...] + _dot(_bf(jnp.tanh(xw)), w2_ref[...])
    w = -_softplus(-z) - 0.5
    ld = -jnp.exp(w)
    a = jax.nn.sigmoid(a0_ref[...] + _dot(_bf(xa), a2_ref[...]))
    gate = _dot(_bf(jax.nn.sigmoid(xg)), g2_ref[...])
    e_blk = e_ref[...]
    kk = xk * kk_ref[...]
    kk = kk / jnp.maximum(jnp.sqrt(_dot(_bf(kk * kk), e_blk)), 1e-12)
    k2 = xk * (1.0 + (a - 1.0) * ka_ref[...])
    bb = kk * a
    bonus = _dot(_bf(xr * k2 * rk_ref[...]), e_blk) * xv

    lin = _dot_split2(tin_ref[...], ld)
    lsuf = _dot_split2(tsuf_ref[...], ld)
    e_in = jnp.exp(lin)
    e_ninv = jnp.exp(-lin)
    e_suf = jnp.exp(lsuf)
    kap_ref[...] = kk * jnp.exp(lin - ld)
    rt_ref[...] = xr * e_in
    bh_ref[...] = bb * e_ninv
    kh_ref[...] = k2 * e_ninv
    bt_ref[...] = bb * e_suf
    kt_ref[...] = k2 * e_suf
    vv_ref[...] = xv
    gc_ref[...] = jnp.exp(lin + lsuf)

    ri = lax.broadcasted_iota(I32, (CHUNK, CHUNK), 0)
    ci = lax.broadcasted_iota(I32, (CHUNK, CHUNK), 1)
    strict = ri > ci
    incl = ri >= ci
    eye = (ri == ci).astype(F32)
    heads = range(RW_HEADS)
    hsl = [slice(h * N, (h + 1) * N) for h in heads]

    per_it = next(n for n in (4, 2, 1) if nch % n == 0)

    def coef_body(it, carry):
        chs = [it * per_it + t for t in range(per_it)]
        pairs = [(t, h) for t in range(per_it) for h in heads]
        rows = [pl.ds(pl.multiple_of(ch * CHUNK, CHUNK), CHUNK) for ch in chs]
        ld = lambda ref, p: ref[rows[p[0]], hsl[p[1]]]
        kap = [ld(kap_ref, p) for p in pairs]
        rt = [ld(rt_ref, p) for p in pairs]
        vh = [ld(vv_ref, p) for p in pairs]
        idx = range(len(pairs))
        gmat = [_dot_nt(_bf(jnp.concatenate([kap[n], rt[n]], axis=0)),
                        _bf(jnp.concatenate([ld(bh_ref, pairs[n]), ld(kh_ref, pairs[n])], axis=0)))
                for n in idx]
        n_ab = [jnp.where(strict, g[:CHUNK, :CHUNK], 0.0) for g in gmat]
        m_rb = [jnp.where(incl, g[CHUNK:, :CHUNK], 0.0) for g in gmat]
        m_v = [_bf(jnp.concatenate([jnp.where(strict, g[:CHUNK, CHUNK:], 0.0),
                                    jnp.where(incl, g[CHUNK:, CHUNK:], 0.0)], axis=0)) for g in gmat]
        mv = [_dot(m_v[n], _bf(vh[n])) for n in idx]
        x_inv = [eye - t for t in n_ab]
        pw = n_ab
        for _ in range(5):
            pwb = [_bf(p) for p in pw]
            pw = [_dot(p, p) for p in pwb]
            x_inv = [x + _dot(_bf(x), _bf(p)) for x, p in zip(x_inv, pw)]
        w = [_dot(_bf(x_inv[n]), _bf(jnp.concatenate([kap[n], mv[n][:CHUNK]], axis=1))) for n in idx]
        wb = [_bf(t) for t in w]
        ry = [jnp.concatenate([rt[n], mv[n][CHUNK:]], axis=1) - _dot(_bf(m_rb[n]), wb[n]) for n in idx]
        dmat = [_dot_tn(wb[n], _bf(ld(bt_ref, pairs[n]))) for n in idx]
        vtk = [_dot_tn(_bf(vh[n]), _bf(ld(kt_ref, pairs[n]))) for n in idx]
        for n, (t, h) in enumerate(pairs):
            rr_ref[rows[t], hsl[h]] = ry[n][:, :N]
            yy_ref[rows[t], hsl[h]] = ry[n][:, N:]
            ac_ref[chs[t], h] = -dmat[n][:N]
            cc_ref[chs[t], h] = vtk[n] - dmat[n][N:]
        return carry

    lax.fori_loop(0, nch // per_it, coef_body, 0)

    def state_body(ch, carry):
        rows = pl.ds(pl.multiple_of(ch * CHUNK, CHUNK), CHUNK)
        s_old = [s_ref[h] for h in heads]
        sb = [_bf(t) for t in s_old]
        s_new = [_dot(sb[h], _bf(ac_ref[ch, h])) for h in heads]
        y_c = [_dot_nt(_bf(rr_ref[rows, hsl[h]]), sb[h]) for h in heads]
        for h in heads:
            gch = gc_ref[pl.ds(pl.multiple_of(ch * CHUNK, CHUNK), 1), hsl[h]]
            s_ref[h] = s_old[h] * gch + s_new[h] + cc_ref[ch, h]
            yy_ref[rows, hsl[h]] = yy_ref[rows, hsl[h]] + y_c[h]
        return carry

    lax.fori_loop(0, nch, state_body, 0)

    y = yy_ref[...]
    mean = _dot(_bf(y), e_blk) * (1.0 / N)
    d = y - mean
    var = _dot(_bf(d * d), e_blk) * (1.0 / N)
    yn = d * lax.rsqrt(var + RW_GN_EPS) * lnx_ref[...]
    y_ref[...] = (yn + bonus) * gate

    @pl.when(c == pl.num_programs(1) - 1)
    def _():
        so_ref[0] = s_ref[...]


def _rwkv(P, B, L, shift_prev, s0, lp):
    T = P.shape[0]
    TT = min(256, L)
    nt = L // TT
    W = RW_WIDTH
    row = lambda t: t.reshape(1, -1).astype(F32)
    mu = lp['rwkv_mu']
    sh = shift_prev.astype(F32)
    pieces = lambda t: (t[..., 0:W], t[..., W:2 * W], t[..., 2 * W:3 * W], t[..., 3 * W:])
    mu_r, mu_k, mu_v, mu_lo = [row(t) for t in pieces(mu)]
    sh_r, sh_k, sh_v, sh_lo = [t.reshape(B, 1, -1) for t in pieces(sh)]
    hid = jnp.arange(W) // RW_HEAD_DIM
    e_blk = (hid[:, None] == hid[None, :]).astype(BF16)
    ti = jnp.arange(TT)
    same = (ti[:, None] // CHUNK) == (ti[None, :] // CHUNK)
    tri_in = (same & (ti[None, :] <= ti[:, None])).astype(BF16)
    tri_suf = (same & (ti[None, :] > ti[:, None])).astype(BF16)

    tok = lambda w, blk: pl.BlockSpec((TT, w), lambda b, c: (b * nt + c, blk))
    full = lambda shape: pl.BlockSpec(shape, lambda b, c: (0,) * len(shape))
    shs = lambda w: pl.BlockSpec((1, 1, w), lambda b, c: (b, 0, 0))
    st_spec = pl.BlockSpec((1, RW_HEADS, RW_HEAD_DIM, RW_HEAD_DIM), lambda b, c: (b, 0, 0, 0))
    big = lambda: pltpu.VMEM((TT, W), F32)
    y, s_out = pl.pallas_call(
        _rwkv_kernel,
        grid=(B, nt),
        in_specs=[tok(W, P_RKV // W), tok(W, P_RKV // W + 1), tok(W, P_RKV // W + 2),
                  tok(RW_LORA, P_LORA // RW_LORA),
                  shs(W), shs(W), shs(W), shs(RW_LORA), st_spec,
                  full((1, W)), full((1, W)), full((1, W)), full((1, RW_LORA)),
                  full((1, W)), full((RW_DECAY_LORA, W)), full((1, W)), full((RW_ICLR_LORA, W)),
                  full((RW_GATE_LORA, W)), full((1, W)), full((1, W)), full((1, W)), full((1, W)),
                  full((W, W)), full((TT, TT)), full((TT, TT))],
        out_specs=[pl.BlockSpec((TT, W), lambda b, c: (b * nt + c, 0)), st_spec],
        out_shape=[jax.ShapeDtypeStruct((T, W), F32),
                   jax.ShapeDtypeStruct((B, RW_HEADS, RW_HEAD_DIM, RW_HEAD_DIM), F32)],
        scratch_shapes=[pltpu.VMEM((8, W), F32), pltpu.VMEM((8, W), F32), pltpu.VMEM((8, W), F32),
                        pltpu.VMEM((8, RW_LORA), F32),
                        pltpu.VMEM((RW_HEADS, RW_HEAD_DIM, RW_HEAD_DIM), F32)] + [big() for _ in range(10)]
        + [pltpu.VMEM((TT // CHUNK, RW_HEADS, RW_HEAD_DIM, RW_HEAD_DIM), F32) for _ in range(2)],
        compiler_params=_params(("parallel", "arbitrary")),
        name="rwkv",
    )(P, P, P, P, sh_r, sh_k, sh_v, sh_lo, s0.astype(F32),
      mu_r, mu_k, mu_v, mu_lo, row(lp['rwkv_w0']), _bf(lp['rwkv_w2']), row(lp['rwkv_a0']),
      _bf(lp['rwkv_a2']), _bf(lp['rwkv_g2']), row(lp['rwkv_k_k']), row(lp['rwkv_k_a']),
      row(lp['rwkv_r_k']), row(lp['rwkv_lnx_g']), e_blk, tri_in, tri_suf)
    return y, s_out


ONES_ROWS = 16
BOUND_SLACK = 1.05
SAFE_BOUND = 60.0


def _dsa_kernel(q_ref, qi_ref, kw_ref, k_ref, v_ref, ki_ref, y_ref,
                kb_ref, vt_ref, kib_ref, kmax_ref, keys_ref, hi_ref, lo_ref, bias_ref, pstar_ref, red_ref,
                *, past, lk_real, topk, kc):
    i = pl.program_id(1)
    qb = q_ref.shape[1]
    lk = k_ref.shape[1]
    kf = float(topk)
    HD = SA_HEAD_DIM

    ones_sq = jnp.ones((LANES, LANES), BF16)

    def head_slab(x, odd):
        lane = lax.broadcasted_iota(I32, x.shape, 1)
        return jnp.where(lane < HD, pltpu.roll(x, HD, 1) if odd else x, 0.0)

    @pl.when(i == 0)
    def _():
        kib_ref[...] = _bf(ki_ref[0])
        k = k_ref[0]
        lane = lax.broadcasted_iota(I32, k.shape, 1)
        for c in range(SA_KV_HEADS):
            kc_b = _bf(head_slab(k, c == 1))
            kb_ref[c] = jnp.where(lane == HD, jnp.ones_like(kc_b), kc_b)
            kf32 = kc_b.astype(F32)
            n2 = _dot(_bf(kf32 * kf32), ones_sq)
            kmax_ref[c] = jnp.broadcast_to(jnp.max(n2, axis=0, keepdims=True), (8, LANES))
        vt = v_ref[0].astype(F32).T
        for c in range(SA_KV_HEADS):
            vt_ref[c, 0:HD, :] = _bf(vt[c * HD:(c + 1) * HD, :])
            vt_ref[c, HD:HD + ONES_ROWS, :] = jnp.ones((ONES_ROWS, lk), BF16)

    qpos = past + i * qb + lax.broadcasted_iota(I32, (1, qb), 1)
    limit = jnp.minimum((qpos // CHUNK + 1) * CHUNK, lk_real)
    hi = jnp.minimum(past + (i + 1) * qb, lk_real)
    nk = lax.shift_right_logical(hi + (2 * kc - 1), kc.bit_length())
    sub_idx = lax.broadcasted_iota(I32, (kc, qb), 0)
    kwt = kw_ref[0].T
    wi_h = [kwt[IDX_DIM + h:IDX_DIM + h + 1, :] * INDEX_SCALE for h in range(IDX_HEADS)]
    qi = qi_ref[0]
    qi_h = [_bf(qi[:, h * IDX_DIM:(h + 1) * IDX_DIM]) for h in range(IDX_HEADS)]

    def chunk(j):
        return pl.ds(pl.multiple_of(j * kc, kc), kc)

    def pair(t):
        return (2 * t, 2 * t + 1)

    def score_body(t, carry):
        js = pair(t)
        dots = [[_dot_nt(kib_ref[chunk(j), 0:IDX_DIM], qi_h[h]) for h in range(IDX_HEADS)] for j in js]
        for j, d in zip(js, dots):
            s = jnp.zeros((kc, qb), F32)
            for h in range(IDX_HEADS):
                s = s + jnp.maximum(d[h], 0.0) * wi_h[h]
            s = jnp.where(s == 0.0, 0.0, s)
            bits = lax.bitcast_convert_type(s, I32)
            key = bits ^ ((bits >> 31) & 0x7FFFFFFF)
            key = jnp.where((sub_idx + j * kc) < limit, key, INT_MIN)
            keys_ref[chunk(j), :] = key
            hi_ref[chunk(j), :] = (key >> 16).astype(I16)
        return carry

    lax.fori_loop(0, nk, score_body, 0)

    def fold(m, rows):
        parts = [m[r * rows:(r + 1) * rows] for r in range(kc // rows)]
        while len(parts) > 1:
            parts = [a + b for a, b in zip(parts[0::2], parts[1::2])]
        return parts[0]

    max_trips = lk // (2 * kc)

    def short_trips(body, carry):
        base = jnp.int32(0)
        p = 1 << (max_trips.bit_length() - 1)
        while p:
            take = (nk & p) != 0

            def run(c, base=base, p=p):
                for s in range(p):
                    c = body(base + s, c)
                return c

            carry = lax.cond(take, run, lambda c: c, carry)
            base = base + jnp.where(take, p, 0)
            p //= 2
        return carry

    def col_total(part):
        rows = part.shape[0]
        red_ref[0:rows, :] = part
        parts = [red_ref[r:r + 1, :] for r in range(rows)]
        while len(parts) > 1:
            parts = [a + b for a, b in zip(parts[0::2], parts[1::2])]
        return parts[0]

    def count(pred):
        def body(t, acc):
            for j in pair(t):
                acc = acc + fold(jnp.where(pred(keys_ref[chunk(j), :], j), 1.0, 0.0), 8)
            return acc

        return col_total(short_trips(body, jnp.zeros((8, qb), F32)))

    ones_red = jnp.ones((8, 32), BF16)
    assert lk // 32 <= 256, "a bf16 accumulator row position must stay exactly representable"

    def rows16(x8):
        return jnp.concatenate([x8, x8], axis=0).astype(I16)

    def slab_count(x, thr16, strict):
        parts = []
        for r in range(kc // 16):
            xs = x[r * 16:(r + 1) * 16]
            hit = xs > thr16 if strict else xs >= thr16
            parts.append(jnp.where(hit, jnp.ones((), BF16), jnp.zeros((), BF16)))
        while len(parts) > 1:
            parts = [a + b for a, b in zip(parts[0::2], parts[1::2])]
        return parts[0]

    def count16(ref, thr16, strict=False):
        def body(t, accs):
            j0, j1 = pair(t)
            return (accs[0] + slab_count(ref[chunk(j0), :], thr16, strict),
                    accs[1] + slab_count(ref[chunk(j1), :], thr16, strict))

        zero = jnp.zeros((16, qb), BF16)
        accs = short_trips(body, (zero, zero))
        return _dot(ones_red, jnp.concatenate(accs, axis=0))

    def kth_bits16(ref, need):
        def bit_body(it, u):
            cand = u | lax.shift_left(jnp.int32(1), 15 - it)
            return jnp.where(count16(ref, rows16(cand - 32768)) >= need, cand, u)

        return lax.fori_loop(0, 16, bit_body, jnp.zeros((8, qb), I32))

    t_hi8 = kth_bits16(hi_ref, kf) - 32768
    t_hi16 = rows16(t_hi8)
    need_lo = kf - count16(hi_ref, t_hi16, strict=True)
    t_hi_tile = jnp.concatenate([t_hi16] * (kc // 16), axis=0)

    def lo_body(t, carry):
        for j in pair(t):
            lo = ((keys_ref[chunk(j), :] & 0xFFFF) - 32768).astype(I16)
            lo_ref[chunk(j), :] = jnp.where(hi_ref[chunk(j), :] == t_hi_tile, lo, jnp.int16(-32768))
        return carry

    short_trips(lo_body, jnp.int32(0))
    tau = (t_hi8 * 65536 + kth_bits16(lo_ref, need_lo))[0:1, :]
    cnt_ge = count(lambda kj, j: kj >= tau)
    cnt_gt = count(lambda kj, j: kj > tau)
    need = kf - cnt_gt
    excess = jnp.logical_and(cnt_ge > kf, tau != INT_MIN)
    p_default = jnp.where(tau == INT_MIN, -1, 2 ** 30).astype(I32)
    pstar_ref[...] = jnp.broadcast_to(p_default, pstar_ref.shape)

    @pl.when(jnp.max(jnp.where(excess, 1.0, 0.0)) > 0.0)
    def _():
        nbits = max(1, int(lk - 1).bit_length())

        def idx_body(it, p):
            cand = p | lax.shift_left(jnp.int32(1), nbits - 1 - it)
            g = count(lambda kj, j: jnp.logical_and(kj == tau, (sub_idx + j * kc) < cand))
            return jnp.where(g < need, cand, p)

        p = lax.fori_loop(0, nbits, idx_body, jnp.zeros((1, qb), I32))
        pstar_ref[...] = jnp.broadcast_to(jnp.where(excess, p, p_default), pstar_ref.shape)

    pstar = pstar_ref[0:1, :]

    def bias_body(t, carry):
        for j in pair(t):
            kj = keys_ref[chunk(j), :]
            sel = jnp.logical_or(kj > tau, jnp.logical_and(kj == tau, (sub_idx + j * kc) <= pstar))
            bias_ref[chunk(j), :] = _bf(jnp.where(sel, 0.0, NEG_BIG))
        return carry

    short_trips(bias_body, jnp.int32(0))

    q = q_ref[0]
    group = SA_HEADS // SA_KV_HEADS
    lane_q = lax.broadcasted_iota(I32, (qb, LANES), 1)
    q_heads, bounds = [], []
    for h in range(SA_HEADS):
        qh = _bf(head_slab(q[:, (h // 2) * LANES:(h // 2 + 1) * LANES], h % 2 == 1))
        qf = qh.astype(F32)
        qn2 = _dot(_bf(qf * qf), ones_sq)
        q_heads.append(qh)
        bounds.append(jnp.sqrt(qn2 * kmax_ref[h // group, 0:1, :]) * BOUND_SLACK)
    worst = bounds[0]
    for b in bounds[1:]:
        worst = jnp.maximum(worst, b)
    safe = jnp.max(worst) <= SAFE_BOUND

    zero_acc = jnp.zeros((HD + ONES_ROWS, qb), F32)

    def vt_chunk(c, j):
        return vt_ref[c, :, pl.ds(pl.multiple_of(j * kc, kc), kc)]

    @pl.when(safe)
    def _():
        hs = range(SA_HEADS)
        qa = [jnp.where(lane_q == HD, _bf(-bounds[h]), q_heads[h]) for h in hs]

        def body(t, accs):
            js = pair(t)
            logit = [[_dot_nt(kb_ref[h // group, chunk(j), :], qa[h]) for h in hs] for j in js]
            bias = [bias_ref[chunk(j), :].astype(F32) for j in js]
            p = [[_bf(jnp.exp2(lg + b)) for lg in row] for b, row in zip(bias, logit)]
            pv = [[_dot(vt_chunk(h // group, j), row[h]) for h in hs] for j, row in zip(js, p)]
            return tuple(accs[h] + pv[0][h] + pv[1][h] for h in hs)

        accs = lax.fori_loop(0, nk, body, (zero_acc,) * SA_HEADS)
        for h in hs:
            y_ref[0, h * HD:(h + 1) * HD, :] = accs[h][:HD] / accs[h][HD:HD + 1]

    @pl.when(jnp.logical_not(safe))
    def _():
        for h in range(SA_HEADS):
            c = h // group

            def logit(j):
                return _dot_nt(kb_ref[c, chunk(j), :], q_heads[h]) + bias_ref[chunk(j), :].astype(F32)

            def max_body(t, m):
                for j in pair(t):
                    m = jnp.maximum(m, jnp.max(logit(j), axis=0, keepdims=True))
                return m

            m = lax.fori_loop(0, nk, max_body, jnp.full((1, qb), NEG_BIG, F32))

            def sum_body(t, acc):
                for j in pair(t):
                    acc = acc + _dot(vt_chunk(c, j), _bf(jnp.exp2(logit(j) - m)))
                return acc

            acc = lax.fori_loop(0, nk, sum_body, zero_acc)
            y_ref[0, h * HD:(h + 1) * HD, :] = acc[:HD] / acc[HD:HD + 1]


def _dsa_call(q3, qi3, kw3, k_all, v_all, ki_all, v_blk, *, lk, past, lk_real, topk, qb, kc):
    B, n_q = q3.shape[:2]
    kern = functools.partial(_dsa_kernel, past=past, lk_real=lk_real, topk=topk, kc=kc)
    return pl.pallas_call(
        kern,
        grid=(B, n_q // qb),
        in_specs=[pl.BlockSpec((1, qb, SA_WIDTH), lambda b, i: (b, i, 0)),
                  pl.BlockSpec((1, qb, IDX_WIDTH), lambda b, i: (b, i, 0)),
                  pl.BlockSpec((1, qb, LANES), lambda b, i: (b, i, 0)),
                  pl.BlockSpec((1, lk, LANES), lambda b, i: (b, 0, 0), pipeline_mode=pl.Buffered(1)),
                  pl.BlockSpec((1, lk, LANES), lambda b, i: (b, 0, v_blk), pipeline_mode=pl.Buffered(1)),
                  pl.BlockSpec((1, lk, LANES), lambda b, i: (b, 0, 0), pipeline_mode=pl.Buffered(1))],
        out_specs=pl.BlockSpec((1, SA_WIDTH, qb), lambda b, i: (b, 0, i)),
        out_shape=jax.ShapeDtypeStruct((B, SA_WIDTH, n_q), F32),
        scratch_shapes=[pltpu.VMEM((SA_KV_HEADS, lk, LANES), BF16),
                        pltpu.VMEM((SA_KV_HEADS, SA_HEAD_DIM + ONES_ROWS, lk), BF16),
                        pltpu.VMEM((lk, LANES), BF16),
                        pltpu.VMEM((SA_KV_HEADS, 8, LANES), F32),
                        pltpu.VMEM((lk, qb), I32), pltpu.VMEM((lk, qb), I16), pltpu.VMEM((lk, qb), I16),
                        pltpu.VMEM((lk, qb), BF16),
                        pltpu.VMEM((8, qb), I32), pltpu.VMEM((16, qb), F32)],
        compiler_params=_params(("parallel", "arbitrary")),
        name="dsa",
    )(q3, qi3, kw3, k_all, v_all, ki_all)


def _round_up(x, m):
    return (x + m - 1) // m * m


def _dsa(P, B, L, past, q_rot, qi_rot, k_rot, kw_rot, k_past, v_past, ik_past):
    lk_real = past + L
    topk = min(TOPK_MAX, lk_real // 4)
    qb = 2 * LANES if (past == 0 and L % (2 * LANES) == 0) else LANES
    lq = _round_up(L, qb)
    qpad = lambda t: t if lq == L else jnp.pad(t, ((0, 0), (0, lq - L), (0, 0)))
    q3 = qpad(q_rot.reshape(B, L, SA_WIDTH))
    qi3 = qpad(qi_rot.reshape(B, L, IDX_WIDTH))
    kw3 = kw_rot.reshape(B, L, LANES)
    k3 = k_rot.reshape(B, L, LANES)
    common = dict(past=past, lk_real=lk_real, topk=topk, qb=qb)
    kc = TILE_ELEMS // qb // 2 if qb == LANES else TILE_ELEMS // qb
    if past == 0:
        while L % (2 * kc):
            kc //= 2
        P3 = P.reshape(B, L, P_COLS)
        return _dsa_call(q3, qi3, kw3, k3, P3, kw3, P_SAV // LANES, lk=L, kc=kc, **common)
    lk = _round_up(lk_real, 2 * kc)
    zpad = jnp.zeros((B, lk - lk_real, LANES), F32)
    v_new = P.reshape(B, L, P_COLS)[:, :, P_SAV:P_SAV + LANES].astype(F32)
    ik_p = jnp.concatenate([ik_past.astype(F32), jnp.zeros((B, past, LANES - IDX_DIM), F32)], axis=2)
    k_all = jnp.concatenate([k_past.reshape(B, past, LANES).astype(F32), k3, zpad], axis=1)
    v_all = jnp.concatenate([v_past.reshape(B, past, LANES).astype(F32), v_new, zpad], axis=1)
    ki_all = jnp.concatenate([ik_p, kw3, zpad], axis=1)
    y = _dsa_call(q3, qi3, qpad(kw3), k_all, v_all, ki_all, 0, lk=lk, kc=kc, **common)
    return y[:, :, :L]


def _ret_kernel(q_ref, k_ref, v_ref, g_ref, cos_ref, sin_ref, dm_ref, qd_ref, kd_ref, gc_ref, s0_ref,
                y_ref, so_ref, s_ref):
    c = pl.program_id(1)
    D = RET_HEAD_DIM
    cs = dm_ref.shape[1]
    nch = q_ref.shape[0] // cs
    heads = range(RET_HEADS)
    hsl = [slice(h * D, (h + 1) * D) for h in heads]
    rows = [slice(ch * cs, (ch + 1) * cs) for ch in range(nch)]
    prob = [(ch, h) for ch in range(nch) for h in heads]

    @pl.when(c == 0)
    def _():
        s_ref[...] = s0_ref[0]

    cos, sin = cos_ref[...], sin_ref[...]
    qb_, kb_, kd_, vb_ = [], [], [], []
    for h in heads:
        q = q_ref[:, hsl[h]].astype(F32)
        k = k_ref[:, hsl[h]].astype(F32)
        k = (k * cos + pltpu.roll(k, D // 2, 1) * sin) * (D ** -0.5)
        qb_.append(_bf(q * cos + pltpu.roll(q, D // 2, 1) * sin))
        kb_.append(_bf(k))
        kd_.append([_bf(k[r] * kd_ref[:, hsl[h]]) for r in rows])
        vb_.append(_bf(v_ref[:, hsl[h]]))
    scores = {(ch, h): _dot_nt(qb_[h][rows[ch]], kb_[h][rows[ch]]) * dm_ref[h] for ch, h in prob}
    ktv = {(ch, h): _dot_tn(kd_[h][ch], vb_[h][rows[ch]]) for ch, h in prob}
    intra = {(ch, h): _dot(_bf(scores[ch, h]), vb_[h][rows[ch]]) for ch, h in prob}
    s_at = {}
    for h in heads:
        s = s_ref[h]
        for ch in range(nch):
            s_at[ch, h] = _bf(s)
            s = s * gc_ref[:, hsl[h]] + ktv[ch, h]
        s_ref[h] = s
    cross = {(ch, h): _dot(qb_[h][rows[ch]], s_at[ch, h]) for ch, h in prob}
    for ch in range(nch):
        outs = []
        for h in heads:
            o = intra[ch, h] + cross[ch, h] * qd_ref[:, hsl[h]]
            o = o * lax.rsqrt(jnp.mean(o * o, axis=-1, keepdims=True) + EPS)
            outs.append(jax.nn.silu(g_ref[rows[ch], hsl[h]].astype(F32)) * o)
        y_ref[rows[ch], :] = jnp.concatenate(outs, axis=1)

    @pl.when(c == pl.num_programs(1) - 1)
    def _():
        so_ref[0] = s_ref[...]


def _retention(P, B, L, pos, s0):
    T = P.shape[0]
    c = min(CHUNK, L)
    nc = L // c
    D = RET_HEAD_DIM
    freqs = 1.0 / (RET_ROPE_BASE ** jnp.linspace(0.0, 1.0, D // 2, dtype=F32))
    ang = pos.astype(F32)[:, None] * freqs[None, :]
    cos = jnp.concatenate([jnp.cos(ang)] * 2, axis=1)
    sin = jnp.concatenate([-jnp.sin(ang), jnp.sin(ang)], axis=1)
    log_gamma = jnp.log1p(-jnp.exp2(-5.0 - jnp.arange(RET_HEADS, dtype=F32)))
    idx = jnp.arange(c, dtype=F32)
    dmask = jnp.exp(jnp.abs(idx[:, None] - idx[None, :])[None] * log_gamma[:, None, None])
    lanes = lambda t: jnp.repeat(t, D, axis=1)
    qdec = lanes(jnp.exp((idx[:, None] + 1.0) * log_gamma[None, :]))
    kdec = lanes(jnp.exp((c - 1.0 - idx)[:, None] * log_gamma[None, :]))
    gchunk = lanes(jnp.exp(c * log_gamma)[None, :])

    W = RET_WIDTH
    tt = min(4 * c, L)
    nt = L // tt
    tok = lambda blk: pl.BlockSpec((tt, W), lambda b, i: (b * nt + i, blk))
    full = lambda shape: pl.BlockSpec(shape, lambda b, i: (0,) * len(shape))
    st_spec = pl.BlockSpec((1, RET_HEADS, D, D), lambda b, i: (b, 0, 0, 0))
    y, s_out = pl.pallas_call(
        _ret_kernel,
        grid=(B, nt),
        in_specs=[tok(P_RET // W), tok(P_RET // W + 1), tok(P_RET // W + 2), tok(P_RET // W + 3),
                  pl.BlockSpec((tt, D), lambda b, i: (i, 0)), pl.BlockSpec((tt, D), lambda b, i: (i, 0)),
                  full((RET_HEADS, c, c)), full((c, W)), full((c, W)), full((1, W)), st_spec],
        out_specs=[pl.BlockSpec((tt, W), lambda b, i: (b * nt + i, 0)), st_spec],
        out_shape=[jax.ShapeDtypeStruct((T, W), F32), jax.ShapeDtypeStruct((B, RET_HEADS, D, D), F32)],
        scratch_shapes=[pltpu.VMEM((RET_HEADS, D, D), F32)],
        compiler_params=_params(("parallel", "arbitrary")),
        name="retention",
    )(P, P, P, P, cos, sin, dmask, qdec, kdec, gchunk, s0.astype(F32))
    return y, s_out


def _merge_kernel(x_ref, g0_ref, g1_ref, g2_ref, yr_ref, ys_ref, yt_ref, wr_ref, ws_ref, wt_ref, wo_ref,
                  o_ref):
    gate = lambda g_ref: jax.nn.sigmoid(g_ref[...].astype(F32))
    m = (gate(g0_ref) * _dot(_bf(yr_ref[...]), wr_ref[...])
         + gate(g1_ref) * _dot_tn(_bf(ys_ref[0]), ws_ref[...])
         + gate(g2_ref) * _dot(_bf(yt_ref[...]), wt_ref[...]))
    o_ref[...] = x_ref[...] + _dot(_bf(m), wo_ref[...])


def _merge(x2d, P, y_rw, y_sa_t, y_ret, w_rw, w_sa, w_ret, w_o):
    T = x2d.shape[0]
    L = y_sa_t.shape[2]
    tm = min(512, L)
    nl = L // tm
    D = D_MODEL
    tok = lambda w, blk: pl.BlockSpec((tm, w), lambda i: (i, blk))
    full = lambda shape: pl.BlockSpec(shape, lambda i: (0, 0))
    return pl.pallas_call(
        _merge_kernel,
        grid=(T // tm,),
        in_specs=[tok(D, 0), tok(D, 0), tok(D, 1), tok(D, 2), tok(RW_WIDTH, 0),
                  pl.BlockSpec((1, SA_WIDTH, tm), lambda i: (i // nl, 0, i % nl)),
                  tok(RET_WIDTH, 0), full((RW_WIDTH, D)), full((SA_WIDTH, D)), full((RET_WIDTH, D)),
                  full((D, D))],
        out_specs=tok(D, 0),
        out_shape=jax.ShapeDtypeStruct((T, D), F32),
        compiler_params=_params(("parallel",)),
        name="merge",
    )(x2d, P, P, P, y_rw, y_sa_t, y_ret, w_rw, w_sa, w_ret, w_o)


def _mlp_kernel(x_ref, g_ref, wu_ref, wd_ref, gf_ref, o_ref, h_ref, acc_ref, *, final_norm):
    j = pl.program_id(1)

    @pl.when(j == 0)
    def _():
        x = x_ref[...]
        ms = jnp.mean(x * x, axis=-1, keepdims=True)
        h_ref[...] = _bf(x * lax.rsqrt(ms + EPS) * g_ref[...])
        acc_ref[...] = jnp.zeros_like(acc_ref)

    u = jnp.maximum(_dot(h_ref[...], wu_ref[...]), 0.0)
    acc_ref[...] += _dot(_bf(u * u), wd_ref[...])

    @pl.when(j == pl.num_programs(1) - 1)
    def _():
        xn = x_ref[...] + acc_ref[...]
        if final_norm:
            ms = jnp.mean(xn * xn, axis=-1, keepdims=True)
            xn = xn * lax.rsqrt(ms + EPS) * gf_ref[...]
        o_ref[...] = xn


def _mlp(x2d, g, w_up, w_down, g_final, final_norm):
    T = x2d.shape[0]
    tm = min(1024, T)
    tf = 1024
    D = D_MODEL
    return pl.pallas_call(
        functools.partial(_mlp_kernel, final_norm=final_norm),
        grid=(T // tm, D_FF // tf),
        in_specs=[pl.BlockSpec((tm, D), lambda i, j: (i, 0)),
                  pl.BlockSpec((1, D), lambda i, j: (0, 0)),
                  pl.BlockSpec((D, tf), lambda i, j: (0, j)),
                  pl.BlockSpec((tf, D), lambda i, j: (j, 0)),
                  pl.BlockSpec((1, D), lambda i, j: (0, 0))],
        out_specs=pl.BlockSpec((tm, D), lambda i, j: (i, 0)),
        out_shape=jax.ShapeDtypeStruct((T, D), F32),
        scratch_shapes=[pltpu.VMEM((tm, D), BF16), pltpu.VMEM((tm, D), F32)],
        compiler_params=_params(("parallel", "arbitrary")),
        name="mlp",
    )(x2d, g, w_up, w_down, g_final)


def _layer(x2d, B, L, past, caches, lp, wts, g_final, final_norm):
    k_past, v_past, ik_past, s_rw, shift_rw, s_ret = caches
    pos = past + jnp.arange(L, dtype=jnp.int32)
    row = lambda t: t.reshape(1, -1).astype(F32)
    P = _in_proj(x2d, row(lp['norm1_g']), wts['w_in'], wts['layer'])
    q_rot, qi_rot, k_rot, kw_rot = _sa_prep(P, L, _sa_tables(pos))
    y_rw, s_rw_new = _rwkv(P, B, L, shift_rw, s_rw, lp)
    y_sa = _dsa(P, B, L, past, q_rot, qi_rot, k_rot, kw_rot, k_past, v_past, ik_past)
    y_ret, s_ret_new = _retention(P, B, L, pos, s_ret)
    x2d = _merge(x2d, P, y_rw, y_sa, y_ret, wts['w_br_rwkv'], wts['w_br_dsa'], wts['w_br_ret'], wts['w_o'])
    x2d = _mlp(x2d, row(lp['norm2_g']), wts['w_up'], wts['w_down'], g_final, final_norm)
    P3 = P.reshape(B, L, P_COLS)
    last = P3[:, L - 1].astype(F32)
    shift_new = jnp.concatenate([last[:, P_RKV:P_RKV + 3 * RW_WIDTH], last[:, P_LORA:P_LORA + RW_LORA]], axis=1)
    k_new = k_rot.reshape(B, L, SA_KV_HEADS, SA_HEAD_DIM)
    v_new = P3[:, :, P_SAV:P_SAV + SA_KV_WIDTH].astype(F32).reshape(B, L, SA_KV_HEADS, SA_HEAD_DIM)
    ik_new = kw_rot.reshape(B, L, LANES)[:, :, :IDX_DIM]
    return x2d, (k_new, v_new, ik_new, s_rw_new, shift_new, s_ret_new)


def kernel(x_prompt, x_sample, cache_dsa_k, cache_dsa_v, cache_dsa_ik, state_rwkv, state_rwkv_shift, state_ret, norm1_g, w_in, rwkv_mu, rwkv_w0, rwkv_w2, rwkv_a0, rwkv_a2, rwkv_g2, rwkv_k_k, rwkv_k_a, rwkv_r_k, rwkv_lnx_g, w_br_rwkv, w_br_dsa, w_br_ret, w_o, norm2_g, w_up, w_down, final_norm_g):
    params = {
        'norm1_g': norm1_g, 'rwkv_mu': rwkv_mu, 'rwkv_w0': rwkv_w0, 'rwkv_w2': rwkv_w2,
        'rwkv_a0': rwkv_a0, 'rwkv_a2': rwkv_a2, 'rwkv_g2': rwkv_g2, 'rwkv_k_k': rwkv_k_k,
        'rwkv_k_a': rwkv_k_a, 'rwkv_r_k': rwkv_r_k, 'rwkv_lnx_g': rwkv_lnx_g, 'norm2_g': norm2_g,
    }
    depth = w_in.shape[0]
    Bp, Lp, D = x_prompt.shape
    Bs, Ls, _ = x_sample.shape
    past_s = cache_dsa_k.shape[2]
    xp = x_prompt.reshape(Bp * Lp, D).astype(F32)
    xs = x_sample.reshape(Bs * Ls, D).astype(F32)
    g_final = final_norm_g.reshape(1, D).astype(F32)
    zero_p = (None, None, None,
              jnp.zeros((Bp, RW_HEADS, RW_HEAD_DIM, RW_HEAD_DIM), F32), jnp.zeros((Bp, RW_COLS), F32),
              jnp.zeros((Bp, RET_HEADS, RET_HEAD_DIM, RET_HEAD_DIM), F32))
    p_states = [[] for _ in range(6)]
    s_states = [[] for _ in range(6)]
    w_all = _w_prep(w_in)
    for i in range(depth):
        lp = {name: arr[i] for name, arr in params.items()}
        wts = {'w_in': w_all, 'layer': i, 'w_br_rwkv': _bf(w_br_rwkv[i]), 'w_br_dsa': _bf(w_br_dsa[i]),
               'w_br_ret': _bf(w_br_ret[i]), 'w_o': _bf(w_o[i]), 'w_up': _bf(w_up[i]), 'w_down': _bf(w_down[i])}
        final = i == depth - 1
        cache_s = (cache_dsa_k[i], cache_dsa_v[i], cache_dsa_ik[i], state_rwkv[i], state_rwkv_shift[i],
                   state_ret[i])
        xp, new_p = _layer(xp, Bp, Lp, 0, zero_p, lp, wts, g_final, final)
        xs, new_s = _layer(xs, Bs, Ls, past_s, cache_s, lp, wts, g_final, final)
        for j in range(6):
            p_states[j].append(new_p[j])
            s_states[j].append(new_s[j])
    y_prompt = xp.reshape(Bp, Lp, D)
    y_sample = xs.reshape(Bs, Ls, D)
    p_out = [jnp.stack(t, axis=0) for t in p_states]
    s_out = [jnp.stack(t, axis=0) for t in s_states]
    return (y_prompt, y_sample, *p_out, *s_out)
```

```python
import functools

import numpy as np
import jax
import jax.numpy as jnp
from jax import lax
from jax.experimental import pallas as pl
from jax.experimental.pallas import tpu as pltpu

F32 = jnp.float32
BF16 = jnp.bfloat16
I32 = jnp.int32
I16 = jnp.int16

D_MODEL = 1024
CHUNK = 64
Q_BLOCK = 128
EPS = 1e-6

RW_HEADS = 8
RW_HEAD_DIM = 64
RW_WIDTH = RW_HEADS * RW_HEAD_DIM
RW_DECAY_LORA = 64
RW_ICLR_LORA = 64
RW_GATE_LORA = 128
RW_LORA = RW_DECAY_LORA + RW_ICLR_LORA + RW_GATE_LORA
RW_COLS = 3 * RW_WIDTH + RW_LORA
RW_GN_EPS = 64e-5

SA_HEADS = 8
SA_KV_HEADS = 2
SA_HEAD_DIM = 64
SA_WIDTH = SA_HEADS * SA_HEAD_DIM
SA_KV_WIDTH = SA_KV_HEADS * SA_HEAD_DIM
IDX_HEADS = 4
IDX_DIM = 64
IDX_WIDTH = IDX_HEADS * IDX_DIM
TOPK_MAX = 256
ROPE_THETA = 500000.0
ROPE_DIM = SA_HEAD_DIM // 4
INDEX_SCALE = (IDX_DIM ** -0.5) * (IDX_HEADS ** -0.5)
SA_COLS = SA_WIDTH + 2 * SA_KV_WIDTH + IDX_WIDTH + IDX_DIM + IDX_HEADS

RET_HEADS = 4
RET_HEAD_DIM = 128
RET_WIDTH = RET_HEADS * RET_HEAD_DIM
RET_ROPE_BASE = 10000.0
RET_COLS = 4 * RET_WIDTH

N_BRANCH = 3
GATE_COLS = N_BRANCH * D_MODEL
IN_COLS = RW_COLS + SA_COLS + RET_COLS + GATE_COLS
D_FF = 4 * D_MODEL

LANES = 128
TILE_ELEMS = 64 * 8 * LANES

P_GATE = 0
P_RET = P_GATE + GATE_COLS
P_RKV = P_RET + RET_COLS
P_SAQ = P_RKV + 3 * RW_WIDTH
P_LORA = P_SAQ + SA_WIDTH
P_QI = P_LORA + RW_LORA
P_SAK = P_QI + IDX_WIDTH
P_SAV = P_SAK + SA_KV_WIDTH
P_KIWI = P_SAV + SA_KV_WIDTH
P_COLS = 8192
INT_MIN = -2 ** 31
NEG_BIG = -1e30
LOG2_E = 1.4426950408889634
VMEM_LIMIT = 56 * 1024 * 1024


def _bf(x):
    return x.astype(BF16)


def _dot(a, b):
    return jnp.dot(a, b, preferred_element_type=F32)


def _dot_nt(a, b):
    return lax.dot_general(a, b, (((1,), (1,)), ((), ())), preferred_element_type=F32)


def _dot_tn(a, b):
    return lax.dot_general(a, b, (((0,), (0,)), ((), ())), preferred_element_type=F32)


def _dot_split2(a_exact, x):
    hi = _bf(x)
    lo = _bf(x - hi.astype(F32))
    return _dot(a_exact, hi) + _dot(a_exact, lo)


def _params(sem):
    return pltpu.CompilerParams(dimension_semantics=sem, vmem_limit_bytes=VMEM_LIMIT)


def _in_proj_kernel(x_ref, g_ref, w_ref, o_ref, h_ref):
    @pl.when(pl.program_id(1) == 0)
    def _():
        x = x_ref[...]
        ms = jnp.mean(x * x, axis=-1, keepdims=True)
        h_ref[...] = _bf(x * lax.rsqrt(ms + EPS) * g_ref[...])

    o_ref[...] = _dot(h_ref[...], w_ref[0]).astype(o_ref.dtype)


def _in_proj(x2d, g, w_all, layer):
    T = x2d.shape[0]
    tm = min(1024, T)
    tn = 1024
    return pl.pallas_call(
        _in_proj_kernel,
        grid=(T // tm, P_COLS // tn),
        in_specs=[pl.BlockSpec((tm, D_MODEL), lambda i, j: (i, 0)),
                  pl.BlockSpec((1, D_MODEL), lambda i, j: (0, 0)),
                  pl.BlockSpec((1, D_MODEL, tn), lambda i, j: (layer, 0, j))],
        out_specs=pl.BlockSpec((tm, tn), lambda i, j: (i, j)),
        out_shape=jax.ShapeDtypeStruct((T, P_COLS), BF16),
        scratch_shapes=[pltpu.VMEM((tm, D_MODEL), BF16)],
        compiler_params=_params(("parallel", "arbitrary")),
        name="in_proj",
    )(x2d, g, w_all)


KIWI_COLS = IDX_DIM + IDX_HEADS
TAIL0 = RW_COLS + SA_COLS - KIWI_COLS
TAIL_W = (IN_COLS - TAIL0 + LANES - 1) // LANES * LANES


def _w_prep_kernel(w_ref, o_ref):
    x = w_ref[0]
    x = jnp.where(lax.broadcasted_iota(I32, x.shape, 1) < IN_COLS, x, 0.0)
    lane = lax.broadcasted_iota(I32, (x.shape[0], LANES), 1)

    def put(off, v):
        o_ref[0, :, off:off + v.shape[1]] = _bf(v)

    nslab = TAIL_W // LANES
    rolled = [pltpu.roll(x[:, TAIL0 + s * LANES:TAIL0 + (s + 1) * LANES], LANES - KIWI_COLS, 1)
              for s in range(nslab)]
    for s in range((RET_COLS + GATE_COLS) // LANES):
        v = jnp.where(lane < LANES - KIWI_COLS, rolled[s], rolled[s + 1])
        put((P_RET if s < RET_COLS // LANES else P_GATE - RET_COLS) + s * LANES, v)
    sa = RW_COLS
    put(P_RKV, x[:, 0:3 * RW_WIDTH])
    put(P_LORA, x[:, 3 * RW_WIDTH:RW_COLS])
    put(P_SAQ, x[:, sa:sa + SA_WIDTH])
    put(P_SAK, x[:, sa + SA_WIDTH:sa + SA_WIDTH + SA_KV_WIDTH])
    put(P_SAV, x[:, sa + SA_WIDTH + SA_KV_WIDTH:sa + SA_WIDTH + 2 * SA_KV_WIDTH])
    put(P_QI, x[:, sa + SA_WIDTH + 2 * SA_KV_WIDTH:TAIL0])
    put(P_KIWI, jnp.where(lane < KIWI_COLS, x[:, TAIL0:TAIL0 + LANES], 0.0))
    put(P_KIWI + LANES, jnp.zeros((x.shape[0], P_COLS - P_KIWI - LANES), F32))


def _w_prep(w_in):
    depth, d, _ = w_in.shape
    tm = 256
    return pl.pallas_call(
        _w_prep_kernel,
        grid=(depth, d // tm),
        in_specs=[pl.BlockSpec((1, tm, TAIL0 + TAIL_W), lambda l, i: (l, i, 0))],
        out_specs=pl.BlockSpec((1, tm, P_COLS), lambda l, i: (l, i, 0)),
        out_shape=jax.ShapeDtypeStruct((depth, d, P_COLS), BF16),
        compiler_params=_params(("parallel", "parallel")),
        name="w_prep",
    )(w_in)


def _rot_slab(x, c, s_lo, s_hi, shift):
    return x * c + pltpu.roll(x, LANES - shift, 1) * s_lo + pltpu.roll(x, shift, 1) * s_hi


def _sa_prep_kernel(q_ref, qi_ref, k_ref, kw_ref, c_ref, s1_ref, s2_ref, ck_ref, s1k_ref, s2k_ref,
                    qo_ref, qio_ref, ko_ref, kwo_ref):
    c, s1, s2 = c_ref[...], s1_ref[...], s2_ref[...]
    half = ROPE_DIM // 2
    scale = SA_HEAD_DIM ** -0.5 * LOG2_E
    for s in range(SA_WIDTH // LANES):
        sl = slice(s * LANES, (s + 1) * LANES)
        qo_ref[:, sl] = _rot_slab(q_ref[:, sl].astype(F32), c, s1, s2, half) * scale
    for s in range(IDX_WIDTH // LANES):
        sl = slice(s * LANES, (s + 1) * LANES)
        qio_ref[:, sl] = _rot_slab(qi_ref[:, sl].astype(F32), c, s1, s2, half)
    ko_ref[...] = _rot_slab(k_ref[...].astype(F32), c, s1, s2, half)
    kwo_ref[...] = _rot_slab(kw_ref[...].astype(F32), ck_ref[...], s1k_ref[...], s2k_ref[...], half)


def _sa_prep(P, L, tabs):
    T = P.shape[0]
    tm = min(512, L)
    nl = L // tm
    tab_spec = pl.BlockSpec((tm, LANES), lambda i: (i % nl, 0))
    return pl.pallas_call(
        _sa_prep_kernel,
        grid=(T // tm,),
        in_specs=[pl.BlockSpec((tm, SA_WIDTH), lambda i: (i, P_SAQ // SA_WIDTH)),
                  pl.BlockSpec((tm, IDX_WIDTH), lambda i: (i, P_QI // IDX_WIDTH)),
                  pl.BlockSpec((tm, LANES), lambda i: (i, P_SAK // LANES)),
                  pl.BlockSpec((tm, LANES), lambda i: (i, P_KIWI // LANES))] + [tab_spec] * 6,
        out_specs=[pl.BlockSpec((tm, SA_WIDTH), lambda i: (i, 0)),
                   pl.BlockSpec((tm, IDX_WIDTH), lambda i: (i, 0)),
                   pl.BlockSpec((tm, LANES), lambda i: (i, 0)),
                   pl.BlockSpec((tm, LANES), lambda i: (i, 0))],
        out_shape=[jax.ShapeDtypeStruct((T, SA_WIDTH), F32),
                   jax.ShapeDtypeStruct((T, IDX_WIDTH), F32),
                   jax.ShapeDtypeStruct((T, LANES), F32),
                   jax.ShapeDtypeStruct((T, LANES), F32)],
        compiler_params=_params(("parallel",)),
        name="sa_prep",
    )(P, P, P, P, *tabs)


def _sa_tables(pos):
    half = ROPE_DIM // 2
    freqs = 1.0 / (ROPE_THETA ** (jnp.arange(0, ROPE_DIM, 2, dtype=F32) / ROPE_DIM))
    ang = pos.astype(F32)[:, None] * freqs[None, :]
    cos, sin = jnp.cos(ang), jnp.sin(ang)
    n = pos.shape[0]
    pad = SA_HEAD_DIM - ROPE_DIM
    c_head = jnp.concatenate([cos, cos, jnp.ones((n, pad), F32)], axis=1)
    s1_head = jnp.concatenate([-sin, jnp.zeros((n, half + pad), F32)], axis=1)
    s2_head = jnp.concatenate([jnp.zeros((n, half), F32), sin, jnp.zeros((n, pad), F32)], axis=1)
    one, zero = jnp.ones((n, SA_HEAD_DIM), F32), jnp.zeros((n, SA_HEAD_DIM), F32)
    two = lambda t: jnp.concatenate([t, t], axis=1)
    return (two(c_head), two(s1_head), two(s2_head),
            jnp.concatenate([c_head, one], axis=1), jnp.concatenate([s1_head, zero], axis=1),
            jnp.concatenate([s2_head, zero], axis=1))


def _softplus(u):
    return jnp.maximum(u, 0.0) + jnp.log(1.0 + jnp.exp(-jnp.abs(u)))


def _rwkv_kernel(r_ref, k_ref, v_ref, lo_ref, shr_ref, shk_ref, shv_ref, shlo_ref, s0_ref,
                 mur_ref, muk_ref, muv_ref, mulo_ref, w0_ref, w2_ref, a0_ref, a2_ref, g2_ref,
                 kk_ref, ka_ref, rk_ref, lnx_ref, e_ref, tin_ref, tsuf_ref,
                 y_ref, so_ref,
                 cr_ref, ck_ref, cv_ref, clo_ref, s_ref,
                 kap_ref, rt_ref, bh_ref, kh_ref, bt_ref, kt_ref, vv_ref, gc_ref, yy_ref, rr_ref,
                 ac_ref, cc_ref):
    c = pl.program_id(1)
    TT = r_ref.shape[0]
    nch = TT // CHUNK
    N = RW_HEAD_DIM

    @pl.when(c == 0)
    def _():
        cr_ref[0:1, :] = shr_ref[0]
        ck_ref[0:1, :] = shk_ref[0]
        cv_ref[0:1, :] = shv_ref[0]
        clo_ref[0:1, :] = shlo_ref[0]
        s_ref[...] = s0_ref[0]

    def lerp(p_ref, carry_ref, mu_ref):
        p = p_ref[...].astype(F32)
        rolled = pltpu.roll(p, 1, 0)
        row0 = lax.broadcasted_iota(I32, p.shape, 0) == 0
        prev = jnp.where(row0, carry_ref[0:1, :], rolled)
        carry_ref[0:1, :] = p[TT - 1:TT, :]
        return p + (prev - p) * mu_ref[...]

    xr = lerp(r_ref, cr_ref, mur_ref)
    xk = lerp(k_ref, ck_ref, muk_ref)
    xv = lerp(v_ref, cv_ref, muv_ref)
    xlo = lerp(lo_ref, clo_ref, mulo_ref)
    xw = xlo[:, :RW_DECAY_LORA]
    xa = xlo[:, RW_DECAY_LORA:RW_DECAY_LORA + RW_ICLR_LORA]
    xg = xlo[:, RW_DECAY_LORA + RW_ICLR_LORA:]

    z = w0_ref[...] + _dot(_bf(jnp.tanh(xw)), w2_ref[...])
    w = -_softplus(-z) - 0.5
    ld = -jnp.exp(w)
    a = jax.nn.sigmoid(a0_ref[...] + _dot(_bf(xa), a2_ref[...]))
    gate = _dot(_bf(jax.nn.sigmoid(xg)), g2_ref[...])
    e_blk = e_ref[...]
    kk = xk * kk_ref[...]
    kk = kk / jnp.maximum(jnp.sqrt(_dot(_bf(kk * kk), e_blk)), 1e-12)
    k2 = xk * (1.0 + (a - 1.0) * ka_ref[...])
    bb = kk * a
    bonus = _dot(_bf(xr * k2 * rk_ref[...]), e_blk) * xv

    lin = _dot_split2(tin_ref[...], ld)
    lsuf = _dot_split2(tsuf_ref[...], ld)
    e_in = jnp.exp(lin)
    e_ninv = jnp.exp(-lin)
    e_suf = jnp.exp(lsuf)
    kap_ref[...] = kk * jnp.exp(lin - ld)
    rt_ref[...] = xr * e_in
    bh_ref[...] = bb * e_ninv
    kh_ref[...] = k2 * e_ninv
    bt_ref[...] = bb * e_suf
    kt_ref[...] = k2 * e_suf
    vv_ref[...] = xv
    gc_ref[...] = jnp.exp(lin + lsuf)

    ri = lax.broadcasted_iota(I32, (CHUNK, CHUNK), 0)
    ci = lax.broadcasted_iota(I32, (CHUNK, CHUNK), 1)
    strict = ri > ci
    incl = ri >= ci
    eye = (ri == ci).astype(F32)
    heads = range(RW_HEADS)
    hsl = [slice(h * N, (h + 1) * N) for h in heads]

    per_it = next(n for n in (4, 2, 1) if nch % n == 0)

    def coef_body(it, carry):
        chs = [it * per_it + t for t in range(per_it)]
        pairs = [(t, h) for t in range(per_it) for h in heads]
        rows = [pl.ds(pl.multiple_of(ch * CHUNK, CHUNK), CHUNK) for ch in chs]
        ld = lambda ref, p: ref[rows[p[0]], hsl[p[1]]]
        kap = [ld(kap_ref, p) for p in pairs]
        rt = [ld(rt_ref, p) for p in pairs]
        vh = [ld(vv_ref, p) for p in pairs]
        idx = range(len(pairs))
        gmat = [_dot_nt(_bf(jnp.concatenate([kap[n], rt[n]], axis=0)),
                        _bf(jnp.concatenate([ld(bh_ref, pairs[n]), ld(kh_ref, pairs[n])], axis=0)))
                for n in idx]
        n_ab = [jnp.where(strict, g[:CHUNK, :CHUNK], 0.0) for g in gmat]
        m_rb = [jnp.where(incl, g[CHUNK:, :CHUNK], 0.0) for g in gmat]
        m_v = [_bf(jnp.concatenate([jnp.where(strict, g[:CHUNK, CHUNK:], 0.0),
                                    jnp.where(incl, g[CHUNK:, CHUNK:], 0.0)], axis=0)) for g in gmat]
        mv = [_dot(m_v[n], _bf(vh[n])) for n in idx]
        x_inv = [eye - t for t in n_ab]
        pw = n_ab
        for _ in range(5):
            pwb = [_bf(p) for p in pw]
            pw = [_dot(p, p) for p in pwb]
            x_inv = [x + _dot(_bf(x), _bf(p)) for x, p in zip(x_inv, pw)]
        w = [_dot(_bf(x_inv[n]), _bf(jnp.concatenate([kap[n], mv[n][:CHUNK]], axis=1))) for n in idx]
        wb = [_bf(t) for t in w]
        ry = [jnp.concatenate([rt[n], mv[n][CHUNK:]], axis=1) - _dot(_bf(m_rb[n]), wb[n]) for n in idx]
        dmat = [_dot_tn(wb[n], _bf(ld(bt_ref, pairs[n]))) for n in idx]
        vtk = [_dot_tn(_bf(vh[n]), _bf(ld(kt_ref, pairs[n]))) for n in idx]
        for n, (t, h) in enumerate(pairs):
            rr_ref[rows[t], hsl[h]] = ry[n][:, :N]
            yy_ref[rows[t], hsl[h]] = ry[n][:, N:]
            ac_ref[chs[t], h] = -dmat[n][:N]
            cc_ref[chs[t], h] = vtk[n] - dmat[n][N:]
        return carry

    lax.fori_loop(0, nch // per_it, coef_body, 0)

    def state_body(ch, carry):
        rows = pl.ds(pl.multiple_of(ch * CHUNK, CHUNK), CHUNK)
        s_old = [s_ref[h] for h in heads]
        sb = [_bf(t) for t in s_old]
        s_new = [_dot(sb[h], _bf(ac_ref[ch, h])) for h in heads]
        y_c = [_dot_nt(_bf(rr_ref[rows, hsl[h]]), sb[h]) for h in heads]
        for h in heads:
            gch = gc_ref[pl.ds(pl.multiple_of(ch * CHUNK, CHUNK), 1), hsl[h]]
            s_ref[h] = s_old[h] * gch + s_new[h] + cc_ref[ch, h]
            yy_ref[rows, hsl[h]] = yy_ref[rows, hsl[h]] + y_c[h]
        return carry

    lax.fori_loop(0, nch, state_body, 0)

    y = yy_ref[...]
    mean = _dot(_bf(y), e_blk) * (1.0 / N)
    d = y - mean
    var = _dot(_bf(d * d), e_blk) * (1.0 / N)
    yn = d * lax.rsqrt(var + RW_GN_EPS) * lnx_ref[...]
    y_ref[...] = (yn + bonus) * gate

    @pl.when(c == pl.num_programs(1) - 1)
    def _():
        so_ref[0] = s_ref[...]


def _rwkv(P, B, L, shift_prev, s0, lp):
    T = P.shape[0]
    TT = min(256, L)
    nt = L // TT
    W = RW_WIDTH
    row = lambda t: t.reshape(1, -1).astype(F32)
    mu = lp['rwkv_mu']
    sh = shift_prev.astype(F32)
    pieces = lambda t: (t[..., 0:W], t[..., W:2 * W], t[..., 2 * W:3 * W], t[..., 3 * W:])
    mu_r, mu_k, mu_v, mu_lo = [row(t) for t in pieces(mu)]
    sh_r, sh_k, sh_v, sh_lo = [t.reshape(B, 1, -1) for t in pieces(sh)]
    hid = jnp.arange(W) // RW_HEAD_DIM
    e_blk = (hid[:, None] == hid[None, :]).astype(BF16)
    ti = jnp.arange(TT)
    same = (ti[:, None] // CHUNK) == (ti[None, :] // CHUNK)
    tri_in = (same & (ti[None, :] <= ti[:, None])).astype(BF16)
    tri_suf = (same & (ti[None, :] > ti[:, None])).astype(BF16)

    tok = lambda w, blk: pl.BlockSpec((TT, w), lambda b, c: (b * nt + c, blk))
    full = lambda shape: pl.BlockSpec(shape, lambda b, c: (0,) * len(shape))
    shs = lambda w: pl.BlockSpec((1, 1, w), lambda b, c: (b, 0, 0))
    st_spec = pl.BlockSpec((1, RW_HEADS, RW_HEAD_DIM, RW_HEAD_DIM), lambda b, c: (b, 0, 0, 0))
    big = lambda: pltpu.VMEM((TT, W), F32)
    y, s_out = pl.pallas_call(
        _rwkv_kernel,
        grid=(B, nt),
        in_specs=[tok(W, P_RKV // W), tok(W, P_RKV // W + 1), tok(W, P_RKV // W + 2),
                  tok(RW_LORA, P_LORA // RW_LORA),
                  shs(W), shs(W), shs(W), shs(RW_LORA), st_spec,
                  full((1, W)), full((1, W)), full((1, W)), full((1, RW_LORA)),
                  full((1, W)), full((RW_DECAY_LORA, W)), full((1, W)), full((RW_ICLR_LORA, W)),
                  full((RW_GATE_LORA, W)), full((1, W)), full((1, W)), full((1, W)), full((1, W)),
                  full((W, W)), full((TT, TT)), full((TT, TT))],
        out_specs=[pl.BlockSpec((TT, W), lambda b, c: (b * nt + c, 0)), st_spec],
        out_shape=[jax.ShapeDtypeStruct((T, W), F32),
                   jax.ShapeDtypeStruct((B, RW_HEADS, RW_HEAD_DIM, RW_HEAD_DIM), F32)],
        scratch_shapes=[pltpu.VMEM((8, W), F32), pltpu.VMEM((8, W), F32), pltpu.VMEM((8, W), F32),
                        pltpu.VMEM((8, RW_LORA), F32),
                        pltpu.VMEM((RW_HEADS, RW_HEAD_DIM, RW_HEAD_DIM), F32)] + [big() for _ in range(10)]
        + [pltpu.VMEM((TT // CHUNK, RW_HEADS, RW_HEAD_DIM, RW_HEAD_DIM), F32) for _ in range(2)],
        compiler_params=_params(("parallel", "arbitrary")),
        name="rwkv",
    )(P, P, P, P, sh_r, sh_k, sh_v, sh_lo, s0.astype(F32),
      mu_r, mu_k, mu_v, mu_lo, row(lp['rwkv_w0']), _bf(lp['rwkv_w2']), row(lp['rwkv_a0']),
      _bf(lp['rwkv_a2']), _bf(lp['rwkv_g2']), row(lp['rwkv_k_k']), row(lp['rwkv_k_a']),
      row(lp['rwkv_r_k']), row(lp['rwkv_lnx_g']), e_blk, tri_in, tri_suf)
    return y, s_out


ONES_ROWS = 16
BOUND_SLACK = 1.05
SAFE_BOUND = 60.0


def _dsa_kernel(q_ref, qi_ref, kw_ref, k_ref, v_ref, ki_ref, y_ref,
                kb_ref, vt_ref, kib_ref, kmax_ref, keys_ref, hi_ref, lo_ref, bias_ref, pstar_ref, red_ref,
                *, past, lk_real, topk, kc):
    i = pl.program_id(1)
    qb = q_ref.shape[1]
    lk = k_ref.shape[1]
    kf = float(topk)
    HD = SA_HEAD_DIM

    ones_sq = jnp.ones((LANES, LANES), BF16)

    def head_slab(x, odd):
        lane = lax.broadcasted_iota(I32, x.shape, 1)
        return jnp.where(lane < HD, pltpu.roll(x, HD, 1) if odd else x, 0.0)

    @pl.when(i == 0)
    def _():
        kib_ref[...] = _bf(ki_ref[0])
        k = k_ref[0]
        lane = lax.broadcasted_iota(I32, k.shape, 1)
        for c in range(SA_KV_HEADS):
            kc_b = _bf(head_slab(k, c == 1))
            kb_ref[c] = jnp.where(lane == HD, jnp.ones_like(kc_b), kc_b)
            kf32 = kc_b.astype(F32)
            n2 = _dot(_bf(kf32 * kf32), ones_sq)
            kmax_ref[c] = jnp.broadcast_to(jnp.max(n2, axis=0, keepdims=True), (8, LANES))
        vt = v_ref[0].astype(F32).T
        for c in range(SA_KV_HEADS):
            vt_ref[c, 0:HD, :] = _bf(vt[c * HD:(c + 1) * HD, :])
            vt_ref[c, HD:HD + ONES_ROWS, :] = jnp.ones((ONES_ROWS, lk), BF16)

    qpos = past + i * qb + lax.broadcasted_iota(I32, (1, qb), 1)
    limit = jnp.minimum((qpos // CHUNK + 1) * CHUNK, lk_real)
    hi = jnp.minimum(past + (i + 1) * qb, lk_real)
    nk = lax.shift_right_logical(hi + (2 * kc - 1), kc.bit_length())
    sub_idx = lax.broadcasted_iota(I32, (kc, qb), 0)
    kwt = kw_ref[0].T
    wi_h = [kwt[IDX_DIM + h:IDX_DIM + h + 1, :] * INDEX_SCALE for h in range(IDX_HEADS)]
    qi = qi_ref[0]
    qi_h = [_bf(qi[:, h * IDX_DIM:(h + 1) * IDX_DIM]) for h in range(IDX_HEADS)]

    def chunk(j):
        return pl.ds(pl.multiple_of(j * kc, kc), kc)

    def pair(t):
        return (2 * t, 2 * t + 1)

    def score_body(t, carry):
        js = pair(t)
        dots = [[_dot_nt(kib_ref[chunk(j), 0:IDX_DIM], qi_h[h]) for h in range(IDX_HEADS)] for j in js]
        for j, d in zip(js, dots):
            s = jnp.zeros((kc, qb), F32)
            for h in range(IDX_HEADS):
                s = s + jnp.maximum(d[h], 0.0) * wi_h[h]
            s = jnp.where(s == 0.0, 0.0, s)
            bits = lax.bitcast_convert_type(s, I32)
            key = bits ^ ((bits >> 31) & 0x7FFFFFFF)
            key = jnp.where((sub_idx + j * kc) < limit, key, INT_MIN)
            keys_ref[chunk(j), :] = key
            hi_ref[chunk(j), :] = (key >> 16).astype(I16)
        return carry

    lax.fori_loop(0, nk, score_body, 0)

    def fold(m, rows):
        parts = [m[r * rows:(r + 1) * rows] for r in range(kc // rows)]
        while len(parts) > 1:
            parts = [a + b for a, b in zip(parts[0::2], parts[1::2])]
        return parts[0]

    max_trips = lk // (2 * kc)

    def short_trips(body, carry):
        base = jnp.int32(0)
        p = 1 << (max_trips.bit_length() - 1)
        while p:
            take = (nk & p) != 0

            def run(c, base=base, p=p):
                for s in range(p):
                    c = body(base + s, c)
                return c

            carry = lax.cond(take, run, lambda c: c, carry)
            base = base + jnp.where(take, p, 0)
            p //= 2
        return carry

    def col_total(part):
        rows = part.shape[0]
        red_ref[0:rows, :] = part
        parts = [red_ref[r:r + 1, :] for r in range(rows)]
        while len(parts) > 1:
            parts = [a + b for a, b in zip(parts[0::2], parts[1::2])]
        return parts[0]

    def count(pred):
        def body(t, acc):
            for j in pair(t):
                acc = acc + fold(jnp.where(pred(keys_ref[chunk(j), :], j), 1.0, 0.0), 8)
            return acc

        return col_total(short_trips(body, jnp.zeros((8, qb), F32)))

    def rows16(x8):
        return jnp.concatenate([x8, x8], axis=0).astype(I16)

    def slab_count(x, thr16, strict):
        parts = []
        for r in range(kc // 16):
            xs = x[r * 16:(r + 1) * 16]
            hit = xs > thr16 if strict else xs >= thr16
            parts.append(jnp.where(hit, jnp.int16(1), jnp.int16(0)))
        while len(parts) > 1:
            parts = [a + b for a, b in zip(parts[0::2], parts[1::2])]
        return parts[0]

    def sublane_allsum(c):
        for sh in (4, 2, 1):
            c = c + pltpu.roll(c, sh, 0)
        return c

    def count16(ref, thr16, strict=False):
        def body(t, acc):
            for j in pair(t):
                acc = acc + slab_count(ref[chunk(j), :], thr16, strict)
            return acc

        w = pltpu.bitcast(short_trips(body, jnp.zeros((16, qb), I16)), I32)
        s = (w & 0xFFFF) + lax.shift_right_logical(w, 16)
        tiles = [s[:, n * LANES:(n + 1) * LANES] for n in range(qb // LANES)]
        out = []
        for a, b in zip(tiles[0::2], tiles[1::2]):
            c = sublane_allsum(a + lax.shift_left(b, 16))
            out += [c & 0xFFFF, lax.shift_right_logical(c, 16)]
        if len(tiles) % 2:
            out.append(sublane_allsum(tiles[-1]))
        return jnp.concatenate(out, axis=1) if len(out) > 1 else out[0]

    def kth_bits16(ref, need):
        def bit_body(it, u):
            cand = u | lax.shift_left(jnp.int32(1), 15 - it)
            return jnp.where(count16(ref, rows16(cand - 32768)) >= need, cand, u)

        return lax.fori_loop(0, 16, bit_body, jnp.zeros((8, qb), I32))

    t_hi8 = kth_bits16(hi_ref, topk) - 32768
    t_hi16 = rows16(t_hi8)
    need_lo = topk - count16(hi_ref, t_hi16, strict=True)
    t_hi_tile = jnp.concatenate([t_hi16] * (kc // 16), axis=0)

    def lo_body(t, carry):
        for j in pair(t):
            lo = ((keys_ref[chunk(j), :] & 0xFFFF) - 32768).astype(I16)
            lo_ref[chunk(j), :] = jnp.where(hi_ref[chunk(j), :] == t_hi_tile, lo, jnp.int16(-32768))
        return carry

    short_trips(lo_body, jnp.int32(0))
    tau = (t_hi8 * 65536 + kth_bits16(lo_ref, need_lo))[0:1, :]
    cnt_ge = count(lambda kj, j: kj >= tau)
    cnt_gt = count(lambda kj, j: kj > tau)
    need = kf - cnt_gt
    excess = jnp.logical_and(cnt_ge > kf, tau != INT_MIN)
    p_default = jnp.where(tau == INT_MIN, -1, 2 ** 30).astype(I32)
    pstar_ref[...] = jnp.broadcast_to(p_default, pstar_ref.shape)

    @pl.when(jnp.max(jnp.where(excess, 1.0, 0.0)) > 0.0)
    def _():
        nbits = max(1, int(lk - 1).bit_length())

        def idx_body(it, p):
            cand = p | lax.shift_left(jnp.int32(1), nbits - 1 - it)
            g = count(lambda kj, j: jnp.logical_and(kj == tau, (sub_idx + j * kc) < cand))
            return jnp.where(g < need, cand, p)

        p = lax.fori_loop(0, nbits, idx_body, jnp.zeros((1, qb), I32))
        pstar_ref[...] = jnp.broadcast_to(jnp.where(excess, p, p_default), pstar_ref.shape)

    pstar = pstar_ref[0:1, :]

    def bias_body(t, carry):
        for j in pair(t):
            kj = keys_ref[chunk(j), :]
            sel = jnp.logical_or(kj > tau, jnp.logical_and(kj == tau, (sub_idx + j * kc) <= pstar))
            bias_ref[chunk(j), :] = _bf(jnp.where(sel, 0.0, NEG_BIG))
        return carry

    short_trips(bias_body, jnp.int32(0))

    q = q_ref[0]
    group = SA_HEADS // SA_KV_HEADS
    lane_q = lax.broadcasted_iota(I32, (qb, LANES), 1)
    q_heads, bounds = [], []
    for h in range(SA_HEADS):
        qh = _bf(head_slab(q[:, (h // 2) * LANES:(h // 2 + 1) * LANES], h % 2 == 1))
        qf = qh.astype(F32)
        qn2 = _dot(_bf(qf * qf), ones_sq)
        q_heads.append(qh)
        bounds.append(jnp.sqrt(qn2 * kmax_ref[h // group, 0:1, :]) * BOUND_SLACK)
    worst = bounds[0]
    for b in bounds[1:]:
        worst = jnp.maximum(worst, b)
    safe = jnp.max(worst) <= SAFE_BOUND

    zero_acc = jnp.zeros((HD + ONES_ROWS, qb), F32)

    def vt_chunk(c, j):
        return vt_ref[c, :, pl.ds(pl.multiple_of(j * kc, kc), kc)]

    @pl.when(safe)
    def _():
        hs = range(SA_HEADS)
        qa = [jnp.where(lane_q == HD, _bf(-bounds[h]), q_heads[h]) for h in hs]

        def body(t, accs):
            js = pair(t)
            logit = [[_dot_nt(kb_ref[h // group, chunk(j), :], qa[h]) for h in hs] for j in js]
            bias = [bias_ref[chunk(j), :].astype(F32) for j in js]
            p = [[_bf(jnp.exp2(lg + b)) for lg in row] for b, row in zip(bias, logit)]
            pv = [[_dot(vt_chunk(h // group, j), row[h]) for h in hs] for j, row in zip(js, p)]
            return tuple(accs[h] + pv[0][h] + pv[1][h] for h in hs)

        accs = lax.fori_loop(0, nk, body, (zero_acc,) * SA_HEADS)
        for h in hs:
            y_ref[0, h * HD:(h + 1) * HD, :] = accs[h][:HD] / accs[h][HD:HD + 1]

    @pl.when(jnp.logical_not(safe))
    def _():
        for h in range(SA_HEADS):
            c = h // group

            def logit(j):
                return _dot_nt(kb_ref[c, chunk(j), :], q_heads[h]) + bias_ref[chunk(j), :].astype(F32)

            def max_body(t, m):
                for j in pair(t):
                    m = jnp.maximum(m, jnp.max(logit(j), axis=0, keepdims=True))
                return m

            m = lax.fori_loop(0, nk, max_body, jnp.full((1, qb), NEG_BIG, F32))

            def sum_body(t, acc):
                for j in pair(t):
                    acc = acc + _dot(vt_chunk(c, j), _bf(jnp.exp2(logit(j) - m)))
                return acc

            acc = lax.fori_loop(0, nk, sum_body, zero_acc)
            y_ref[0, h * HD:(h + 1) * HD, :] = acc[:HD] / acc[HD:HD + 1]


def _dsa_call(q3, qi3, kw3, k_all, v_all, ki_all, v_blk, *, lk, past, lk_real, topk, qb, kc):
    B, n_q = q3.shape[:2]
    kern = functools.partial(_dsa_kernel, past=past, lk_real=lk_real, topk=topk, kc=kc)
    return pl.pallas_call(
        kern,
        grid=(B, n_q // qb),
        in_specs=[pl.BlockSpec((1, qb, SA_WIDTH), lambda b, i: (b, i, 0)),
                  pl.BlockSpec((1, qb, IDX_WIDTH), lambda b, i: (b, i, 0)),
                  pl.BlockSpec((1, qb, LANES), lambda b, i: (b, i, 0)),
                  pl.BlockSpec((1, lk, LANES), lambda b, i: (b, 0, 0), pipeline_mode=pl.Buffered(1)),
                  pl.BlockSpec((1, lk, LANES), lambda b, i: (b, 0, v_blk), pipeline_mode=pl.Buffered(1)),
                  pl.BlockSpec((1, lk, LANES), lambda b, i: (b, 0, 0), pipeline_mode=pl.Buffered(1))],
        out_specs=pl.BlockSpec((1, SA_WIDTH, qb), lambda b, i: (b, 0, i)),
        out_shape=jax.ShapeDtypeStruct((B, SA_WIDTH, n_q), F32),
        scratch_shapes=[pltpu.VMEM((SA_KV_HEADS, lk, LANES), BF16),
                        pltpu.VMEM((SA_KV_HEADS, SA_HEAD_DIM + ONES_ROWS, lk), BF16),
                        pltpu.VMEM((lk, LANES), BF16),
                        pltpu.VMEM((SA_KV_HEADS, 8, LANES), F32),
                        pltpu.VMEM((lk, qb), I32), pltpu.VMEM((lk, qb), I16), pltpu.VMEM((lk, qb), I16),
                        pltpu.VMEM((lk, qb), BF16),
                        pltpu.VMEM((8, qb), I32), pltpu.VMEM((16, qb), F32)],
        compiler_params=_params(("parallel", "arbitrary")),
        name="dsa",
    )(q3, qi3, kw3, k_all, v_all, ki_all)


def _round_up(x, m):
    return (x + m - 1) // m * m


def _dsa(P, B, L, past, q_rot, qi_rot, k_rot, kw_rot, k_past, v_past, ik_past):
    lk_real = past + L
    topk = min(TOPK_MAX, lk_real // 4)
    qb = 2 * LANES if (past == 0 and L % (2 * LANES) == 0) else LANES
    lq = _round_up(L, qb)
    qpad = lambda t: t if lq == L else jnp.pad(t, ((0, 0), (0, lq - L), (0, 0)))
    q3 = qpad(q_rot.reshape(B, L, SA_WIDTH))
    qi3 = qpad(qi_rot.reshape(B, L, IDX_WIDTH))
    kw3 = kw_rot.reshape(B, L, LANES)
    k3 = k_rot.reshape(B, L, LANES)
    common = dict(past=past, lk_real=lk_real, topk=topk, qb=qb)
    kc = TILE_ELEMS // qb // 2 if qb == LANES else TILE_ELEMS // qb
    if past == 0:
        while L % (2 * kc):
            kc //= 2
        P3 = P.reshape(B, L, P_COLS)
        return _dsa_call(q3, qi3, kw3, k3, P3, kw3, P_SAV // LANES, lk=L, kc=kc, **common)
    lk = _round_up(lk_real, 2 * kc)
    zpad = jnp.zeros((B, lk - lk_real, LANES), F32)
    v_new = P.reshape(B, L, P_COLS)[:, :, P_SAV:P_SAV + LANES].astype(F32)
    ik_p = jnp.concatenate([ik_past.astype(F32), jnp.zeros((B, past, LANES - IDX_DIM), F32)], axis=2)
    k_all = jnp.concatenate([k_past.reshape(B, past, LANES).astype(F32), k3, zpad], axis=1)
    v_all = jnp.concatenate([v_past.reshape(B, past, LANES).astype(F32), v_new, zpad], axis=1)
    ki_all = jnp.concatenate([ik_p, kw3, zpad], axis=1)
    y = _dsa_call(q3, qi3, qpad(kw3), k_all, v_all, ki_all, 0, lk=lk, kc=kc, **common)
    return y[:, :, :L]


def _ret_kernel(q_ref, k_ref, v_ref, g_ref, cos_ref, sin_ref, dm_ref, qd_ref, kd_ref, gc_ref, s0_ref,
                y_ref, so_ref, s_ref):
    c = pl.program_id(1)
    D = RET_HEAD_DIM
    cs = dm_ref.shape[1]
    nch = q_ref.shape[0] // cs
    heads = range(RET_HEADS)
    hsl = [slice(h * D, (h + 1) * D) for h in heads]
    rows = [slice(ch * cs, (ch + 1) * cs) for ch in range(nch)]
    prob = [(ch, h) for ch in range(nch) for h in heads]

    @pl.when(c == 0)
    def _():
        s_ref[...] = s0_ref[0]

    cos, sin = cos_ref[...], sin_ref[...]
    qb_, kb_, kd_, vb_ = [], [], [], []
    for h in heads:
        q = q_ref[:, hsl[h]].astype(F32)
        k = k_ref[:, hsl[h]].astype(F32)
        k = (k * cos + pltpu.roll(k, D // 2, 1) * sin) * (D ** -0.5)
        qb_.append(_bf(q * cos + pltpu.roll(q, D // 2, 1) * sin))
        kb_.append(_bf(k))
        kd_.append([_bf(k[r] * kd_ref[:, hsl[h]]) for r in rows])
        vb_.append(_bf(v_ref[:, hsl[h]]))
    scores = {(ch, h): _dot_nt(qb_[h][rows[ch]], kb_[h][rows[ch]]) * dm_ref[h] for ch, h in prob}
    ktv = {(ch, h): _dot_tn(kd_[h][ch], vb_[h][rows[ch]]) for ch, h in prob}
    intra = {(ch, h): _dot(_bf(scores[ch, h]), vb_[h][rows[ch]]) for ch, h in prob}
    s_at = {}
    for h in heads:
        s = s_ref[h]
        for ch in range(nch):
            s_at[ch, h] = _bf(s)
            s = s * gc_ref[:, hsl[h]] + ktv[ch, h]
        s_ref[h] = s
    cross = {(ch, h): _dot(qb_[h][rows[ch]], s_at[ch, h]) for ch, h in prob}
    for ch in range(nch):
        outs = []
        for h in heads:
            o = intra[ch, h] + cross[ch, h] * qd_ref[:, hsl[h]]
            o = o * lax.rsqrt(jnp.mean(o * o, axis=-1, keepdims=True) + EPS)
            outs.append(jax.nn.silu(g_ref[rows[ch], hsl[h]].astype(F32)) * o)
        y_ref[rows[ch], :] = jnp.concatenate(outs, axis=1)

    @pl.when(c == pl.num_programs(1) - 1)
    def _():
        so_ref[0] = s_ref[...]


def _retention(P, B, L, pos, s0):
    T = P.shape[0]
    c = min(CHUNK, L)
    nc = L // c
    D = RET_HEAD_DIM
    freqs = 1.0 / (RET_ROPE_BASE ** jnp.linspace(0.0, 1.0, D // 2, dtype=F32))
    ang = pos.astype(F32)[:, None] * freqs[None, :]
    cos = jnp.concatenate([jnp.cos(ang)] * 2, axis=1)
    sin = jnp.concatenate([-jnp.sin(ang), jnp.sin(ang)], axis=1)
    log_gamma = jnp.log1p(-jnp.exp2(-5.0 - jnp.arange(RET_HEADS, dtype=F32)))
    idx = jnp.arange(c, dtype=F32)
    dmask = jnp.exp(jnp.abs(idx[:, None] - idx[None, :])[None] * log_gamma[:, None, None])
    lanes = lambda t: jnp.repeat(t, D, axis=1)
    qdec = lanes(jnp.exp((idx[:, None] + 1.0) * log_gamma[None, :]))
    kdec = lanes(jnp.exp((c - 1.0 - idx)[:, None] * log_gamma[None, :]))
    gchunk = lanes(jnp.exp(c * log_gamma)[None, :])

    W = RET_WIDTH
    tt = min(4 * c, L)
    nt = L // tt
    tok = lambda blk: pl.BlockSpec((tt, W), lambda b, i: (b * nt + i, blk))
    full = lambda shape: pl.BlockSpec(shape, lambda b, i: (0,) * len(shape))
    st_spec = pl.BlockSpec((1, RET_HEADS, D, D), lambda b, i: (b, 0, 0, 0))
    y, s_out = pl.pallas_call(
        _ret_kernel,
        grid=(B, nt),
        in_specs=[tok(P_RET // W), tok(P_RET // W + 1), tok(P_RET // W + 2), tok(P_RET // W + 3),
                  pl.BlockSpec((tt, D), lambda b, i: (i, 0)), pl.BlockSpec((tt, D), lambda b, i: (i, 0)),
                  full((RET_HEADS, c, c)), full((c, W)), full((c, W)), full((1, W)), st_spec],
        out_specs=[pl.BlockSpec((tt, W), lambda b, i: (b * nt + i, 0)), st_spec],
        out_shape=[jax.ShapeDtypeStruct((T, W), F32), jax.ShapeDtypeStruct((B, RET_HEADS, D, D), F32)],
        scratch_shapes=[pltpu.VMEM((RET_HEADS, D, D), F32)],
        compiler_params=_params(("parallel", "arbitrary")),
        name="retention",
    )(P, P, P, P, cos, sin, dmask, qdec, kdec, gchunk, s0.astype(F32))
    return y, s_out


def _merge_kernel(x_ref, g0_ref, g1_ref, g2_ref, yr_ref, ys_ref, yt_ref, wr_ref, ws_ref, wt_ref, wo_ref,
                  o_ref):
    gate = lambda g_ref: jax.nn.sigmoid(g_ref[...].astype(F32))
    m = (gate(g0_ref) * _dot(_bf(yr_ref[...]), wr_ref[...])
         + gate(g1_ref) * _dot_tn(_bf(ys_ref[0]), ws_ref[...])
         + gate(g2_ref) * _dot(_bf(yt_ref[...]), wt_ref[...]))
    o_ref[...] = x_ref[...] + _dot(_bf(m), wo_ref[...])


def _merge(x2d, P, y_rw, y_sa_t, y_ret, w_rw, w_sa, w_ret, w_o):
    T = x2d.shape[0]
    L = y_sa_t.shape[2]
    tm = min(512, L)
    nl = L // tm
    D = D_MODEL
    tok = lambda w, blk: pl.BlockSpec((tm, w), lambda i: (i, blk))
    full = lambda shape: pl.BlockSpec(shape, lambda i: (0, 0))
    return pl.pallas_call(
        _merge_kernel,
        grid=(T // tm,),
        in_specs=[tok(D, 0), tok(D, 0), tok(D, 1), tok(D, 2), tok(RW_WIDTH, 0),
                  pl.BlockSpec((1, SA_WIDTH, tm), lambda i: (i // nl, 0, i % nl)),
                  tok(RET_WIDTH, 0), full((RW_WIDTH, D)), full((SA_WIDTH, D)), full((RET_WIDTH, D)),
                  full((D, D))],
        out_specs=tok(D, 0),
        out_shape=jax.ShapeDtypeStruct((T, D), F32),
        compiler_params=_params(("parallel",)),
        name="merge",
    )(x2d, P, P, P, y_rw, y_sa_t, y_ret, w_rw, w_sa, w_ret, w_o)


def _mlp_kernel(x_ref, g_ref, wu_ref, wd_ref, gf_ref, o_ref, h_ref, acc_ref, *, final_norm):
    j = pl.program_id(1)

    @pl.when(j == 0)
    def _():
        x = x_ref[...]
        ms = jnp.mean(x * x, axis=-1, keepdims=True)
        h_ref[...] = _bf(x * lax.rsqrt(ms + EPS) * g_ref[...])
        acc_ref[...] = jnp.zeros_like(acc_ref)

    u = jnp.maximum(_dot(h_ref[...], wu_ref[...]), 0.0)
    acc_ref[...] += _dot(_bf(u * u), wd_ref[...])

    @pl.when(j == pl.num_programs(1) - 1)
    def _():
        xn = x_ref[...] + acc_ref[...]
        if final_norm:
            ms = jnp.mean(xn * xn, axis=-1, keepdims=True)
            xn = xn * lax.rsqrt(ms + EPS) * gf_ref[...]
        o_ref[...] = xn


def _mlp(x2d, g, w_up, w_down, g_final, final_norm):
    T = x2d.shape[0]
    tm = min(1024, T)
    tf = 1024
    D = D_MODEL
    return pl.pallas_call(
        functools.partial(_mlp_kernel, final_norm=final_norm),
        grid=(T // tm, D_FF // tf),
        in_specs=[pl.BlockSpec((tm, D), lambda i, j: (i, 0)),
                  pl.BlockSpec((1, D), lambda i, j: (0, 0)),
                  pl.BlockSpec((D, tf), lambda i, j: (0, j)),
                  pl.BlockSpec((tf, D), lambda i, j: (j, 0)),
                  pl.BlockSpec((1, D), lambda i, j: (0, 0))],
        out_specs=pl.BlockSpec((tm, D), lambda i, j: (i, 0)),
        out_shape=jax.ShapeDtypeStruct((T, D), F32),
        scratch_shapes=[pltpu.VMEM((tm, D), BF16), pltpu.VMEM((tm, D), F32)],
        compiler_params=_params(("parallel", "arbitrary")),
        name="mlp",
    )(x2d, g, w_up, w_down, g_final)


def _layer(x2d, B, L, past, caches, lp, wts, g_final, final_norm):
    k_past, v_past, ik_past, s_rw, shift_rw, s_ret = caches
    pos = past + jnp.arange(L, dtype=jnp.int32)
    row = lambda t: t.reshape(1, -1).astype(F32)
    P = _in_proj(x2d, row(lp['norm1_g']), wts['w_in'], wts['layer'])
    q_rot, qi_rot, k_rot, kw_rot = _sa_prep(P, L, _sa_tables(pos))
    y_rw, s_rw_new = _rwkv(P, B, L, shift_rw, s_rw, lp)
    y_sa = _dsa(P, B, L, past, q_rot, qi_rot, k_rot, kw_rot, k_past, v_past, ik_past)
    y_ret, s_ret_new = _retention(P, B, L, pos, s_ret)
    x2d = _merge(x2d, P, y_rw, y_sa, y_ret, wts['w_br_rwkv'], wts['w_br_dsa'], wts['w_br_ret'], wts['w_o'])
    x2d = _mlp(x2d, row(lp['norm2_g']), wts['w_up'], wts['w_down'], g_final, final_norm)
    P3 = P.reshape(B, L, P_COLS)
    last = P3[:, L - 1].astype(F32)
    shift_new = jnp.concatenate([last[:, P_RKV:P_RKV + 3 * RW_WIDTH], last[:, P_LORA:P_LORA + RW_LORA]], axis=1)
    k_new = k_rot.reshape(B, L, SA_KV_HEADS, SA_HEAD_DIM)
    v_new = P3[:, :, P_SAV:P_SAV + SA_KV_WIDTH].astype(F32).reshape(B, L, SA_KV_HEADS, SA_HEAD_DIM)
    ik_new = kw_rot.reshape(B, L, LANES)[:, :, :IDX_DIM]
    return x2d, (k_new, v_new, ik_new, s_rw_new, shift_new, s_ret_new)


def kernel(x_prompt, x_sample, cache_dsa_k, cache_dsa_v, cache_dsa_ik, state_rwkv, state_rwkv_shift, state_ret, norm1_g, w_in, rwkv_mu, rwkv_w0, rwkv_w2, rwkv_a0, rwkv_a2, rwkv_g2, rwkv_k_k, rwkv_k_a, rwkv_r_k, rwkv_lnx_g, w_br_rwkv, w_br_dsa, w_br_ret, w_o, norm2_g, w_up, w_down, final_norm_g):
    params = {
        'norm1_g': norm1_g, 'rwkv_mu': rwkv_mu, 'rwkv_w0': rwkv_w0, 'rwkv_w2': rwkv_w2,
        'rwkv_a0': rwkv_a0, 'rwkv_a2': rwkv_a2, 'rwkv_g2': rwkv_g2, 'rwkv_k_k': rwkv_k_k,
        'rwkv_k_a': rwkv_k_a, 'rwkv_r_k': rwkv_r_k, 'rwkv_lnx_g': rwkv_lnx_g, 'norm2_g': norm2_g,
    }
    depth = w_in.shape[0]
    Bp, Lp, D = x_prompt.shape
    Bs, Ls, _ = x_sample.shape
    past_s = cache_dsa_k.shape[2]
    xp = x_prompt.reshape(Bp * Lp, D).astype(F32)
    xs = x_sample.reshape(Bs * Ls, D).astype(F32)
    g_final = final_norm_g.reshape(1, D).astype(F32)
    zero_p = (None, None, None,
              jnp.zeros((Bp, RW_HEADS, RW_HEAD_DIM, RW_HEAD_DIM), F32), jnp.zeros((Bp, RW_COLS), F32),
              jnp.zeros((Bp, RET_HEADS, RET_HEAD_DIM, RET_HEAD_DIM), F32))
    p_states = [[] for _ in range(6)]
    s_states = [[] for _ in range(6)]
    w_all = _w_prep(w_in)
    for i in range(depth):
        lp = {name: arr[i] for name, arr in params.items()}
        wts = {'w_in': w_all, 'layer': i, 'w_br_rwkv': _bf(w_br_rwkv[i]), 'w_br_dsa': _bf(w_br_dsa[i]),
               'w_br_ret': _bf(w_br_ret[i]), 'w_o': _bf(w_o[i]), 'w_up': _bf(w_up[i]), 'w_down': _bf(w_down[i])}
        final = i == depth - 1
        cache_s = (cache_dsa_k[i], cache_dsa_v[i], cache_dsa_ik[i], state_rwkv[i], state_rwkv_shift[i],
                   state_ret[i])
        xp, new_p = _layer(xp, Bp, Lp, 0, zero_p, lp, wts, g_final, final)
        xs, new_s = _layer(xs, Bs, Ls, past_s, cache_s, lp, wts, g_final, final)
        for j in range(6):
            p_states[j].append(new_p[j])
            s_states[j].append(new_s[j])
    y_prompt = xp.reshape(Bp, Lp, D)
    y_sample = xs.reshape(Bs, Ls, D)
    p_out = [jnp.stack(t, axis=0) for t in p_states]
    s_out = [jnp.stack(t, axis=0) for t in s_states]
    return (y_prompt, y_sample, *p_out, *s_out)
```

```python
import functools

import numpy as np
import jax
import jax.numpy as jnp
from jax import lax
from jax.experimental import pallas as pl
from jax.experimental.pallas import tpu as pltpu

F32 = jnp.float32
BF16 = jnp.bfloat16
I32 = jnp.int32
I16 = jnp.int16

D_MODEL = 1024
CHUNK = 64
Q_BLOCK = 128
EPS = 1e-6

RW_HEADS = 8
RW_HEAD_DIM = 64
RW_WIDTH = RW_HEADS * RW_HEAD_DIM
RW_DECAY_LORA = 64
RW_ICLR_LORA = 64
RW_GATE_LORA = 128
RW_LORA = RW_DECAY_LORA + RW_ICLR_LORA + RW_GATE_LORA
RW_COLS = 3 * RW_WIDTH + RW_LORA
RW_GN_EPS = 64e-5

SA_HEADS = 8
SA_KV_HEADS = 2
SA_HEAD_DIM = 64
SA_WIDTH = SA_HEADS * SA_HEAD_DIM
SA_KV_WIDTH = SA_KV_HEADS * SA_HEAD_DIM
IDX_HEADS = 4
IDX_DIM = 64
IDX_WIDTH = IDX_HEADS * IDX_DIM
TOPK_MAX = 256
ROPE_THETA = 500000.0
ROPE_DIM = SA_HEAD_DIM // 4
INDEX_SCALE = (IDX_DIM ** -0.5) * (IDX_HEADS ** -0.5)
SA_COLS = SA_WIDTH + 2 * SA_KV_WIDTH + IDX_WIDTH + IDX_DIM + IDX_HEADS

RET_HEADS = 4
RET_HEAD_DIM = 128
RET_WIDTH = RET_HEADS * RET_HEAD_DIM
RET_ROPE_BASE = 10000.0
RET_COLS = 4 * RET_WIDTH

N_BRANCH = 3
GATE_COLS = N_BRANCH * D_MODEL
IN_COLS = RW_COLS + SA_COLS + RET_COLS + GATE_COLS
D_FF = 4 * D_MODEL

LANES = 128
TILE_ELEMS = 64 * 8 * LANES

P_GATE = 0
P_RET = P_GATE + GATE_COLS
P_RKV = P_RET + RET_COLS
P_SAQ = P_RKV + 3 * RW_WIDTH
P_LORA = P_SAQ + SA_WIDTH
P_QI = P_LORA + RW_LORA
P_SAK = P_QI + IDX_WIDTH
P_SAV = P_SAK + SA_KV_WIDTH
P_KIWI = P_SAV + SA_KV_WIDTH
P_COLS = 8192
INT_MIN = -2 ** 31
NEG_BIG = -1e30
LOG2_E = 1.4426950408889634
VMEM_LIMIT = 56 * 1024 * 1024


def _bf(x):
    return x.astype(BF16)


def _dot(a, b):
    return jnp.dot(a, b, preferred_element_type=F32)


def _dot_nt(a, b):
    return lax.dot_general(a, b, (((1,), (1,)), ((), ())), preferred_element_type=F32)


def _dot_tn(a, b):
    return lax.dot_general(a, b, (((0,), (0,)), ((), ())), preferred_element_type=F32)


def _dot_split2(a_exact, x):
    hi = _bf(x)
    lo = _bf(x - hi.astype(F32))
    return _dot(a_exact, hi) + _dot(a_exact, lo)


def _params(sem):
    return pltpu.CompilerParams(dimension_semantics=sem, vmem_limit_bytes=VMEM_LIMIT)


def _in_proj_kernel(x_ref, g_ref, w_ref, o_ref, h_ref):
    @pl.when(pl.program_id(1) == 0)
    def _():
        x = x_ref[...]
        ms = jnp.mean(x * x, axis=-1, keepdims=True)
        h_ref[...] = _bf(x * lax.rsqrt(ms + EPS) * g_ref[...])

    o_ref[...] = _dot(h_ref[...], w_ref[0]).astype(o_ref.dtype)


def _in_proj(x2d, g, w_all, layer):
    T = x2d.shape[0]
    tm = min(1024, T)
    tn = 1024
    return pl.pallas_call(
        _in_proj_kernel,
        grid=(T // tm, P_COLS // tn),
        in_specs=[pl.BlockSpec((tm, D_MODEL), lambda i, j: (i, 0)),
                  pl.BlockSpec((1, D_MODEL), lambda i, j: (0, 0)),
                  pl.BlockSpec((1, D_MODEL, tn), lambda i, j: (layer, 0, j))],
        out_specs=pl.BlockSpec((tm, tn), lambda i, j: (i, j)),
        out_shape=jax.ShapeDtypeStruct((T, P_COLS), BF16),
        scratch_shapes=[pltpu.VMEM((tm, D_MODEL), BF16)],
        compiler_params=_params(("parallel", "arbitrary")),
        name="in_proj",
    )(x2d, g, w_all)


KIWI_COLS = IDX_DIM + IDX_HEADS
TAIL0 = RW_COLS + SA_COLS - KIWI_COLS
TAIL_W = (IN_COLS - TAIL0 + LANES - 1) // LANES * LANES


def _w_prep_kernel(w_ref, o_ref):
    x = w_ref[0]
    lane = lax.broadcasted_iota(I32, (x.shape[0], LANES), 1)

    def put(off, v):
        o_ref[0, :, off:off + v.shape[1]] = _bf(v)

    nslab = TAIL_W // LANES
    part = IN_COLS - TAIL0 - (nslab - 1) * LANES
    slabs = [x[:, TAIL0 + s * LANES:TAIL0 + (s + 1) * LANES] for s in range(nslab - 1)]
    slabs.append(jnp.concatenate([x[:, IN_COLS - part:], jnp.zeros((x.shape[0], LANES - part), F32)], axis=1))
    rolled = [pltpu.roll(t, LANES - KIWI_COLS, 1) for t in slabs]
    for s in range((RET_COLS + GATE_COLS) // LANES):
        v = jnp.where(lane < LANES - KIWI_COLS, rolled[s], rolled[s + 1])
        put((P_RET if s < RET_COLS // LANES else P_GATE - RET_COLS) + s * LANES, v)
    sa = RW_COLS
    put(P_RKV, x[:, 0:3 * RW_WIDTH])
    put(P_LORA, x[:, 3 * RW_WIDTH:RW_COLS])
    put(P_SAQ, x[:, sa:sa + SA_WIDTH])
    put(P_SAK, x[:, sa + SA_WIDTH:sa + SA_WIDTH + SA_KV_WIDTH])
    put(P_SAV, x[:, sa + SA_WIDTH + SA_KV_WIDTH:sa + SA_WIDTH + 2 * SA_KV_WIDTH])
    put(P_QI, x[:, sa + SA_WIDTH + 2 * SA_KV_WIDTH:TAIL0])
    put(P_KIWI, jnp.where(lane < KIWI_COLS, x[:, TAIL0:TAIL0 + LANES], 0.0))
    put(P_KIWI + LANES, jnp.zeros((x.shape[0], P_COLS - P_KIWI - LANES), F32))


def _w_prep(w_in):
    depth, d, _ = w_in.shape
    tm = 256
    return pl.pallas_call(
        _w_prep_kernel,
        grid=(depth, d // tm),
        in_specs=[pl.BlockSpec((1, tm, IN_COLS), lambda l, i: (l, i, 0))],
        out_specs=pl.BlockSpec((1, tm, P_COLS), lambda l, i: (l, i, 0)),
        out_shape=jax.ShapeDtypeStruct((depth, d, P_COLS), BF16),
        compiler_params=_params(("parallel", "parallel")),
        name="w_prep",
    )(w_in)


def _rot_slab(x, c, s_lo, s_hi, shift):
    return x * c + pltpu.roll(x, LANES - shift, 1) * s_lo + pltpu.roll(x, shift, 1) * s_hi


def _sa_prep_kernel(q_ref, qi_ref, k_ref, kw_ref, c_ref, s1_ref, s2_ref, ck_ref, s1k_ref, s2k_ref,
                    qo_ref, qio_ref, ko_ref, kwo_ref):
    c, s1, s2 = c_ref[...], s1_ref[...], s2_ref[...]
    half = ROPE_DIM // 2
    scale = SA_HEAD_DIM ** -0.5 * LOG2_E
    for s in range(SA_WIDTH // LANES):
        sl = slice(s * LANES, (s + 1) * LANES)
        qo_ref[:, sl] = _rot_slab(q_ref[:, sl].astype(F32), c, s1, s2, half) * scale
    for s in range(IDX_WIDTH // LANES):
        sl = slice(s * LANES, (s + 1) * LANES)
        qio_ref[:, sl] = _rot_slab(qi_ref[:, sl].astype(F32), c, s1, s2, half)
    ko_ref[...] = _rot_slab(k_ref[...].astype(F32), c, s1, s2, half)
    kwo_ref[...] = _rot_slab(kw_ref[...].astype(F32), ck_ref[...], s1k_ref[...], s2k_ref[...], half)


def _sa_prep(P, L, tabs):
    T = P.shape[0]
    tm = min(512, L)
    nl = L // tm
    tab_spec = pl.BlockSpec((tm, LANES), lambda i: (i % nl, 0))
    return pl.pallas_call(
        _sa_prep_kernel,
        grid=(T // tm,),
        in_specs=[pl.BlockSpec((tm, SA_WIDTH), lambda i: (i, P_SAQ // SA_WIDTH)),
                  pl.BlockSpec((tm, IDX_WIDTH), lambda i: (i, P_QI // IDX_WIDTH)),
                  pl.BlockSpec((tm, LANES), lambda i: (i, P_SAK // LANES)),
                  pl.BlockSpec((tm, LANES), lambda i: (i, P_KIWI // LANES))] + [tab_spec] * 6,
        out_specs=[pl.BlockSpec((tm, SA_WIDTH), lambda i: (i, 0)),
                   pl.BlockSpec((tm, IDX_WIDTH), lambda i: (i, 0)),
                   pl.BlockSpec((tm, LANES), lambda i: (i, 0)),
                   pl.BlockSpec((tm, LANES), lambda i: (i, 0))],
        out_shape=[jax.ShapeDtypeStruct((T, SA_WIDTH), F32),
                   jax.ShapeDtypeStruct((T, IDX_WIDTH), F32),
                   jax.ShapeDtypeStruct((T, LANES), F32),
                   jax.ShapeDtypeStruct((T, LANES), F32)],
        compiler_params=_params(("parallel",)),
        name="sa_prep",
    )(P, P, P, P, *tabs)


def _sa_tables(pos):
    half = ROPE_DIM // 2
    freqs = 1.0 / (ROPE_THETA ** (jnp.arange(0, ROPE_DIM, 2, dtype=F32) / ROPE_DIM))
    ang = pos.astype(F32)[:, None] * freqs[None, :]
    cos, sin = jnp.cos(ang), jnp.sin(ang)
    n = pos.shape[0]
    pad = SA_HEAD_DIM - ROPE_DIM
    c_head = jnp.concatenate([cos, cos, jnp.ones((n, pad), F32)], axis=1)
    s1_head = jnp.concatenate([-sin, jnp.zeros((n, half + pad), F32)], axis=1)
    s2_head = jnp.concatenate([jnp.zeros((n, half), F32), sin, jnp.zeros((n, pad), F32)], axis=1)
    one, zero = jnp.ones((n, SA_HEAD_DIM), F32), jnp.zeros((n, SA_HEAD_DIM), F32)
    two = lambda t: jnp.concatenate([t, t], axis=1)
    return (two(c_head), two(s1_head), two(s2_head),
            jnp.concatenate([c_head, one], axis=1), jnp.concatenate([s1_head, zero], axis=1),
            jnp.concatenate([s2_head, zero], axis=1))


def _softplus(u):
    return jnp.maximum(u, 0.0) + jnp.log(1.0 + jnp.exp(-jnp.abs(u)))


def _rwkv_kernel(r_ref, k_ref, v_ref, lo_ref, shr_ref, shk_ref, shv_ref, shlo_ref, s0_ref,
                 mur_ref, muk_ref, muv_ref, mulo_ref, w0_ref, w2_ref, a0_ref, a2_ref, g2_ref,
                 kk_ref, ka_ref, rk_ref, lnx_ref, e_ref, tin_ref, tsuf_ref,
                 y_ref, so_ref,
                 cr_ref, ck_ref, cv_ref, clo_ref, s_ref,
                 kap_ref, rt_ref, bh_ref, kh_ref, bt_ref, kt_ref, vv_ref, gc_ref, yy_ref, rr_ref,
                 ac_ref, cc_ref):
    c = pl.program_id(1)
    TT = r_ref.shape[0]
    nch = TT // CHUNK
    N = RW_HEAD_DIM

    @pl.when(c == 0)
    def _():
        cr_ref[0:1, :] = shr_ref[0]
        ck_ref[0:1, :] = shk_ref[0]
        cv_ref[0:1, :] = shv_ref[0]
        clo_ref[0:1, :] = shlo_ref[0]
        s_ref[...] = s0_ref[0]

    def lerp(p_ref, carry_ref, mu_ref):
        p = p_ref[...].astype(F32)
        rolled = pltpu.roll(p, 1, 0)
        row0 = lax.broadcasted_iota(I32, p.shape, 0) == 0
        prev = jnp.where(row0, carry_ref[0:1, :], rolled)
        carry_ref[0:1, :] = p[TT - 1:TT, :]
        return p + (prev - p) * mu_ref[...]

    xr = lerp(r_ref, cr_ref, mur_ref)
    xk = lerp(k_ref, ck_ref, muk_ref)
    xv = lerp(v_ref, cv_ref, muv_ref)
    xlo = lerp(lo_ref, clo_ref, mulo_ref)
    xw = xlo[:, :RW_DECAY_LORA]
    xa = xlo[:, RW_DECAY_LORA:RW_DECAY_LORA + RW_ICLR_LORA]
    xg = xlo[:, RW_DECAY_LORA + RW_ICLR_LORA:]

    z = w0_ref[...] + _dot(_bf(jnp.tanh(xw)), w2_ref[...])
    w = -_softplus(-z) - 0.5
    ld = -jnp.exp(w)
    a = jax.nn.sigmoid(a0_ref[...] + _dot(_bf(xa), a2_ref[...]))
    gate = _dot(_bf(jax.nn.sigmoid(xg)), g2_ref[...])
    e_blk = e_ref[...]
    kk = xk * kk_ref[...]
    kk = kk / jnp.maximum(jnp.sqrt(_dot(_bf(kk * kk), e_blk)), 1e-12)
    k2 = xk * (1.0 + (a - 1.0) * ka_ref[...])
    bb = kk * a
    bonus = _dot(_bf(xr * k2 * rk_ref[...]), e_blk) * xv

    lin = _dot_split2(tin_ref[...], ld)
    lsuf = _dot_split2(tsuf_ref[...], ld)
    e_in = jnp.exp(lin)
    e_ninv = jnp.exp(-lin)
    e_suf = jnp.exp(lsuf)
    kap_ref[...] = kk * jnp.exp(lin - ld)
    rt_ref[...] = xr * e_in
    bh_ref[...] = bb * e_ninv
    kh_ref[...] = k2 * e_ninv
    bt_ref[...] = bb * e_suf
    kt_ref[...] = k2 * e_suf
    vv_ref[...] = xv
    gc_ref[...] = jnp.exp(lin + lsuf)

    ri = lax.broadcasted_iota(I32, (CHUNK, CHUNK), 0)
    ci = lax.broadcasted_iota(I32, (CHUNK, CHUNK), 1)
    strict = ri > ci
    incl = ri >= ci
    eye = (ri == ci).astype(F32)
    heads = range(RW_HEADS)
    hsl = [slice(h * N, (h + 1) * N) for h in heads]

    per_it = next(n for n in (4, 2, 1) if nch % n == 0)

    def coef_body(it, carry):
        chs = [it * per_it + t for t in range(per_it)]
        pairs = [(t, h) for t in range(per_it) for h in heads]
        rows = [pl.ds(pl.multiple_of(ch * CHUNK, CHUNK), CHUNK) for ch in chs]
        ld = lambda ref, p: ref[rows[p[0]], hsl[p[1]]]
        kap = [ld(kap_ref, p) for p in pairs]
        rt = [ld(rt_ref, p) for p in pairs]
        vh = [ld(vv_ref, p) for p in pairs]
        idx = range(len(pairs))
        gmat = [_dot_nt(_bf(jnp.concatenate([kap[n], rt[n]], axis=0)),
                        _bf(jnp.concatenate([ld(bh_ref, pairs[n]), ld(kh_ref, pairs[n])], axis=0)))
                for n in idx]
        n_ab = [jnp.where(strict, g[:CHUNK, :CHUNK], 0.0) for g in gmat]
        m_rb = [jnp.where(incl, g[CHUNK:, :CHUNK], 0.0) for g in gmat]
        m_v = [_bf(jnp.concatenate([jnp.where(strict, g[:CHUNK, CHUNK:], 0.0),
                                    jnp.where(incl, g[CHUNK:, CHUNK:], 0.0)], axis=0)) for g in gmat]
        mv = [_dot(m_v[n], _bf(vh[n])) for n in idx]
        x_inv = [eye - t for t in n_ab]
        pw = n_ab
        for _ in range(5):
            pwb = [_bf(p) for p in pw]
            pw = [_dot(p, p) for p in pwb]
            x_inv = [x + _dot(_bf(x), _bf(p)) for x, p in zip(x_inv, pw)]
        w = [_dot(_bf(x_inv[n]), _bf(jnp.concatenate([kap[n], mv[n][:CHUNK]], axis=1))) for n in idx]
        wb = [_bf(t) for t in w]
        ry = [jnp.concatenate([rt[n], mv[n][CHUNK:]], axis=1) - _dot(_bf(m_rb[n]), wb[n]) for n in idx]
        dmat = [_dot_tn(wb[n], _bf(ld(bt_ref, pairs[n]))) for n in idx]
        vtk = [_dot_tn(_bf(vh[n]), _bf(ld(kt_ref, pairs[n]))) for n in idx]
        for n, (t, h) in enumerate(pairs):
            rr_ref[rows[t], hsl[h]] = ry[n][:, :N]
            yy_ref[rows[t], hsl[h]] = ry[n][:, N:]
            ac_ref[chs[t], h] = -dmat[n][:N]
            cc_ref[chs[t], h] = vtk[n] - dmat[n][N:]
        return carry

    lax.fori_loop(0, nch // per_it, coef_body, 0)

    def state_body(ch, carry):
        rows = pl.ds(pl.multiple_of(ch * CHUNK, CHUNK), CHUNK)
        s_old = [s_ref[h] for h in heads]
        sb = [_bf(t) for t in s_old]
        s_new = [_dot(sb[h], _bf(ac_ref[ch, h])) for h in heads]
        y_c = [_dot_nt(_bf(rr_ref[rows, hsl[h]]), sb[h]) for h in heads]
        for h in heads:
            gch = gc_ref[pl.ds(pl.multiple_of(ch * CHUNK, CHUNK), 1), hsl[h]]
            s_ref[h] = s_old[h] * gch + s_new[h] + cc_ref[ch, h]
            yy_ref[rows, hsl[h]] = yy_ref[rows, hsl[h]] + y_c[h]
        return carry

    lax.fori_loop(0, nch, state_body, 0)

    y = yy_ref[...]
    mean = _dot(_bf(y), e_blk) * (1.0 / N)
    d = y - mean
    var = _dot(_bf(d * d), e_blk) * (1.0 / N)
    yn = d * lax.rsqrt(var + RW_GN_EPS) * lnx_ref[...]
    y_ref[...] = (yn + bonus) * gate

    @pl.when(c == pl.num_programs(1) - 1)
    def _():
        so_ref[0] = s_ref[...]


def _rwkv(P, B, L, shift_prev, s0, lp):
    T = P.shape[0]
    TT = min(256, L)
    nt = L // TT
    W = RW_WIDTH
    row = lambda t: t.reshape(1, -1).astype(F32)
    mu = lp['rwkv_mu']
    sh = shift_prev.astype(F32)
    pieces = lambda t: (t[..., 0:W], t[..., W:2 * W], t[..., 2 * W:3 * W], t[..., 3 * W:])
    mu_r, mu_k, mu_v, mu_lo = [row(t) for t in pieces(mu)]
    sh_r, sh_k, sh_v, sh_lo = [t.reshape(B, 1, -1) for t in pieces(sh)]
    hid = jnp.arange(W) // RW_HEAD_DIM
    e_blk = (hid[:, None] == hid[None, :]).astype(BF16)
    ti = jnp.arange(TT)
    same = (ti[:, None] // CHUNK) == (ti[None, :] // CHUNK)
    tri_in = (same & (ti[None, :] <= ti[:, None])).astype(BF16)
    tri_suf = (same & (ti[None, :] > ti[:, None])).astype(BF16)

    tok = lambda w, blk: pl.BlockSpec((TT, w), lambda b, c: (b * nt + c, blk))
    full = lambda shape: pl.BlockSpec(shape, lambda b, c: (0,) * len(shape))
    shs = lambda w: pl.BlockSpec((1, 1, w), lambda b, c: (b, 0, 0))
    st_spec = pl.BlockSpec((1, RW_HEADS, RW_HEAD_DIM, RW_HEAD_DIM), lambda b, c: (b, 0, 0, 0))
    big = lambda: pltpu.VMEM((TT, W), F32)
    y, s_out = pl.pallas_call(
        _rwkv_kernel,
        grid=(B, nt),
        in_specs=[tok(W, P_RKV // W), tok(W, P_RKV // W + 1), tok(W, P_RKV // W + 2),
                  tok(RW_LORA, P_LORA // RW_LORA),
                  shs(W), shs(W), shs(W), shs(RW_LORA), st_spec,
                  full((1, W)), full((1, W)), full((1, W)), full((1, RW_LORA)),
                  full((1, W)), full((RW_DECAY_LORA, W)), full((1, W)), full((RW_ICLR_LORA, W)),
                  full((RW_GATE_LORA, W)), full((1, W)), full((1, W)), full((1, W)), full((1, W)),
                  full((W, W)), full((TT, TT)), full((TT, TT))],
        out_specs=[pl.BlockSpec((TT, W), lambda b, c: (b * nt + c, 0)), st_spec],
        out_shape=[jax.ShapeDtypeStruct((T, W), F32),
                   jax.ShapeDtypeStruct((B, RW_HEADS, RW_HEAD_DIM, RW_HEAD_DIM), F32)],
        scratch_shapes=[pltpu.VMEM((8, W), F32), pltpu.VMEM((8, W), F32), pltpu.VMEM((8, W), F32),
                        pltpu.VMEM((8, RW_LORA), F32),
                        pltpu.VMEM((RW_HEADS, RW_HEAD_DIM, RW_HEAD_DIM), F32)] + [big() for _ in range(10)]
        + [pltpu.VMEM((TT // CHUNK, RW_HEADS, RW_HEAD_DIM, RW_HEAD_DIM), F32) for _ in range(2)],
        compiler_params=_params(("parallel", "arbitrary")),
        name="rwkv",
    )(P, P, P, P, sh_r, sh_k, sh_v, sh_lo, s0.astype(F32),
      mu_r, mu_k, mu_v, mu_lo, row(lp['rwkv_w0']), _bf(lp['rwkv_w2']), row(lp['rwkv_a0']),
      _bf(lp['rwkv_a2']), _bf(lp['rwkv_g2']), row(lp['rwkv_k_k']), row(lp['rwkv_k_a']),
      row(lp['rwkv_r_k']), row(lp['rwkv_lnx_g']), e_blk, tri_in, tri_suf)
    return y, s_out


ONES_ROWS = 16
BOUND_SLACK = 1.05
SAFE_BOUND = 60.0


def _dsa_kernel(q_ref, qi_ref, kw_ref, k_ref, v_ref, ki_ref, y_ref,
                kb_ref, vt_ref, kib_ref, kmax_ref, keys_ref, hi_ref, lo_ref, bias_ref, pstar_ref, red_ref,
                *, past, lk_real, topk, kc):
    i = pl.program_id(1)
    qb = q_ref.shape[1]
    lk = k_ref.shape[1]
    kf = float(topk)
    HD = SA_HEAD_DIM

    ones_sq = jnp.ones((LANES, LANES), BF16)

    def head_slab(x, odd):
        lane = lax.broadcasted_iota(I32, x.shape, 1)
        return jnp.where(lane < HD, pltpu.roll(x, HD, 1) if odd else x, 0.0)

    @pl.when(i == 0)
    def _():
        kib_ref[...] = _bf(ki_ref[0])
        k = k_ref[0]
        lane = lax.broadcasted_iota(I32, k.shape, 1)
        for c in range(SA_KV_HEADS):
            kc_b = _bf(head_slab(k, c == 1))
            kb_ref[c] = jnp.where(lane == HD, jnp.ones_like(kc_b), kc_b)
            kf32 = kc_b.astype(F32)
            n2 = _dot(_bf(kf32 * kf32), ones_sq)
            kmax_ref[c] = jnp.broadcast_to(jnp.max(n2, axis=0, keepdims=True), (8, LANES))
        vt = v_ref[0].astype(F32).T
        for c in range(SA_KV_HEADS):
            vt_ref[c, 0:HD, :] = _bf(vt[c * HD:(c + 1) * HD, :])
            vt_ref[c, HD:HD + ONES_ROWS, :] = jnp.ones((ONES_ROWS, lk), BF16)

    qpos = past + i * qb + lax.broadcasted_iota(I32, (1, qb), 1)
    limit = jnp.minimum((qpos // CHUNK + 1) * CHUNK, lk_real)
    hi = jnp.minimum(past + (i + 1) * qb, lk_real)
    nk = lax.shift_right_logical(hi + (2 * kc - 1), kc.bit_length())
    sub_idx = lax.broadcasted_iota(I32, (kc, qb), 0)
    kwt = kw_ref[0].T
    wi_h = [kwt[IDX_DIM + h:IDX_DIM + h + 1, :] * INDEX_SCALE for h in range(IDX_HEADS)]
    qi = qi_ref[0]
    qi_h = [_bf(qi[:, h * IDX_DIM:(h + 1) * IDX_DIM]) for h in range(IDX_HEADS)]

    def chunk(j):
        return pl.ds(pl.multiple_of(j * kc, kc), kc)

    def pair(t):
        return (2 * t, 2 * t + 1)

    def score_body(t, carry):
        js = pair(t)
        dots = [[_dot_nt(kib_ref[chunk(j), 0:IDX_DIM], qi_h[h]) for h in range(IDX_HEADS)] for j in js]
        for j, d in zip(js, dots):
            s = jnp.zeros((kc, qb), F32)
            for h in range(IDX_HEADS):
                s = s + jnp.maximum(d[h], 0.0) * wi_h[h]
            s = jnp.where(s == 0.0, 0.0, s)
            bits = lax.bitcast_convert_type(s, I32)
            key = bits ^ ((bits >> 31) & 0x7FFFFFFF)
            key = jnp.where((sub_idx + j * kc) < limit, key, INT_MIN)
            keys_ref[chunk(j), :] = key
            hi_ref[chunk(j), :] = (key >> 16).astype(I16)
        return carry

    lax.fori_loop(0, nk, score_body, 0)

    def fold(m, rows):
        parts = [m[r * rows:(r + 1) * rows] for r in range(kc // rows)]
        while len(parts) > 1:
            parts = [a + b for a, b in zip(parts[0::2], parts[1::2])]
        return parts[0]

    max_trips = lk // (2 * kc)

    def short_trips(body, carry):
        base = jnp.int32(0)
        p = 1 << (max_trips.bit_length() - 1)
        while p:
            take = (nk & p) != 0

            def run(c, base=base, p=p):
                for s in range(p):
                    c = body(base + s, c)
                return c

            carry = lax.cond(take, run, lambda c: c, carry)
            base = base + jnp.where(take, p, 0)
            p //= 2
        return carry

    def col_total(part):
        rows = part.shape[0]
        red_ref[0:rows, :] = part
        parts = [red_ref[r:r + 1, :] for r in range(rows)]
        while len(parts) > 1:
            parts = [a + b for a, b in zip(parts[0::2], parts[1::2])]
        return parts[0]

    def count(pred):
        def body(t, acc):
            for j in pair(t):
                acc = acc + fold(jnp.where(pred(keys_ref[chunk(j), :], j), 1.0, 0.0), 8)
            return acc

        return col_total(short_trips(body, jnp.zeros((8, qb), F32)))

    def rows16(x8):
        return jnp.concatenate([x8, x8], axis=0).astype(I16)

    def slab_count(x, thr16, strict):
        parts = []
        for r in range(kc // 16):
            xs = x[r * 16:(r + 1) * 16]
            hit = xs > thr16 if strict else xs >= thr16
            parts.append(jnp.where(hit, jnp.int16(1), jnp.int16(0)))
        while len(parts) > 1:
            parts = [a + b for a, b in zip(parts[0::2], parts[1::2])]
        return parts[0]

    def sublane_allsum(c):
        parts = [c] + [pltpu.roll(c, sh, 0) for sh in range(1, 8)]
        while len(parts) > 1:
            parts = [a + b for a, b in zip(parts[0::2], parts[1::2])]
        return parts[0]

    def count16(ref, thr16, strict=False):
        def body(t, acc):
            for j in pair(t):
                acc = acc + slab_count(ref[chunk(j), :], thr16, strict)
            return acc

        w = pltpu.bitcast(short_trips(body, jnp.zeros((16, qb), I16)), I32)
        s = (w & 0xFFFF) + lax.shift_right_logical(w, 16)
        tiles = [s[:, n * LANES:(n + 1) * LANES] for n in range(qb // LANES)]
        out = []
        for a, b in zip(tiles[0::2], tiles[1::2]):
            c = sublane_allsum(a + lax.shift_left(b, 16))
            out += [c & 0xFFFF, lax.shift_right_logical(c, 16)]
        if len(tiles) % 2:
            out.append(sublane_allsum(tiles[-1]))
        return jnp.concatenate(out, axis=1) if len(out) > 1 else out[0]

    def kth_bits16(ref, need):
        def bit_body(it, u):
            cand = u | lax.shift_left(jnp.int32(1), 15 - it)
            return jnp.where(count16(ref, rows16(cand - 32768)) >= need, cand, u)

        return lax.fori_loop(0, 16, bit_body, jnp.zeros((8, qb), I32))

    t_hi8 = kth_bits16(hi_ref, topk) - 32768
    t_hi16 = rows16(t_hi8)
    need_lo = topk - count16(hi_ref, t_hi16, strict=True)
    t_hi_tile = jnp.concatenate([t_hi16] * (kc // 16), axis=0)

    def lo_body(t, carry):
        for j in pair(t):
            lo = ((keys_ref[chunk(j), :] & 0xFFFF) - 32768).astype(I16)
            lo_ref[chunk(j), :] = jnp.where(hi_ref[chunk(j), :] == t_hi_tile, lo, jnp.int16(-32768))
        return carry

    short_trips(lo_body, jnp.int32(0))
    tau = (t_hi8 * 65536 + kth_bits16(lo_ref, need_lo))[0:1, :]
    cnt_ge = count(lambda kj, j: kj >= tau)
    cnt_gt = count(lambda kj, j: kj > tau)
    need = kf - cnt_gt
    excess = jnp.logical_and(cnt_ge > kf, tau != INT_MIN)
    p_default = jnp.where(tau == INT_MIN, -1, 2 ** 30).astype(I32)
    pstar_ref[...] = jnp.broadcast_to(p_default, pstar_ref.shape)

    @pl.when(jnp.max(jnp.where(excess, 1.0, 0.0)) > 0.0)
    def _():
        nbits = max(1, int(lk - 1).bit_length())

        def idx_body(it, p):
            cand = p | lax.shift_left(jnp.int32(1), nbits - 1 - it)
            g = count(lambda kj, j: jnp.logical_and(kj == tau, (sub_idx + j * kc) < cand))
            return jnp.where(g < need, cand, p)

        p = lax.fori_loop(0, nbits, idx_body, jnp.zeros((1, qb), I32))
        pstar_ref[...] = jnp.broadcast_to(jnp.where(excess, p, p_default), pstar_ref.shape)

    pstar = pstar_ref[0:1, :]

    def bias_body(t, carry):
        for j in pair(t):
            kj = keys_ref[chunk(j), :]
            sel = jnp.logical_or(kj > tau, jnp.logical_and(kj == tau, (sub_idx + j * kc) <= pstar))
            bias_ref[chunk(j), :] = _bf(jnp.where(sel, 0.0, NEG_BIG))
        return carry

    short_trips(bias_body, jnp.int32(0))

    q = q_ref[0]
    group = SA_HEADS // SA_KV_HEADS
    lane_q = lax.broadcasted_iota(I32, (qb, LANES), 1)
    q_heads, bounds = [], []
    for h in range(SA_HEADS):
        qh = _bf(head_slab(q[:, (h // 2) * LANES:(h // 2 + 1) * LANES], h % 2 == 1))
        qf = qh.astype(F32)
        qn2 = _dot(_bf(qf * qf), ones_sq)
        q_heads.append(qh)
        bounds.append(jnp.sqrt(qn2 * kmax_ref[h // group, 0:1, :]) * BOUND_SLACK)
    worst = bounds[0]
    for b in bounds[1:]:
        worst = jnp.maximum(worst, b)
    safe = jnp.max(worst) <= SAFE_BOUND

    zero_acc = jnp.zeros((HD + ONES_ROWS, qb), F32)

    def vt_chunk(c, j):
        return vt_ref[c, :, pl.ds(pl.multiple_of(j * kc, kc), kc)]

    @pl.when(safe)
    def _():
        hs = range(SA_HEADS)
        qa = [jnp.where(lane_q == HD, _bf(-bounds[h]), q_heads[h]) for h in hs]

        def body(t, accs):
            js = pair(t)
            logit = [[_dot_nt(kb_ref[h // group, chunk(j), :], qa[h]) for h in hs] for j in js]
            bias = [bias_ref[chunk(j), :].astype(F32) for j in js]
            p = [[_bf(jnp.exp2(lg + b)) for lg in row] for b, row in zip(bias, logit)]
            pv = [[_dot(vt_chunk(h // group, j), row[h]) for h in hs] for j, row in zip(js, p)]
            return tuple(accs[h] + pv[0][h] + pv[1][h] for h in hs)

        accs = lax.fori_loop(0, nk, body, (zero_acc,) * SA_HEADS)
        for h in hs:
            y_ref[0, h * HD:(h + 1) * HD, :] = accs[h][:HD] / accs[h][HD:HD + 1]

    @pl.when(jnp.logical_not(safe))
    def _():
        for h in range(SA_HEADS):
            c = h // group

            def logit(j):
                return _dot_nt(kb_ref[c, chunk(j), :], q_heads[h]) + bias_ref[chunk(j), :].astype(F32)

            def max_body(t, m):
                for j in pair(t):
                    m = jnp.maximum(m, jnp.max(logit(j), axis=0, keepdims=True))
                return m

            m = lax.fori_loop(0, nk, max_body, jnp.full((1, qb), NEG_BIG, F32))

            def sum_body(t, acc):
                for j in pair(t):
                    acc = acc + _dot(vt_chunk(c, j), _bf(jnp.exp2(logit(j) - m)))
                return acc

            acc = lax.fori_loop(0, nk, sum_body, zero_acc)
            y_ref[0, h * HD:(h + 1) * HD, :] = acc[:HD] / acc[HD:HD + 1]


def _dsa_call(q3, qi3, kw3, k_all, v_all, ki_all, v_blk, *, lk, past, lk_real, topk, qb, kc):
    B, n_q = q3.shape[:2]
    kern = functools.partial(_dsa_kernel, past=past, lk_real=lk_real, topk=topk, kc=kc)
    return pl.pallas_call(
        kern,
        grid=(B, n_q // qb),
        in_specs=[pl.BlockSpec((1, qb, SA_WIDTH), lambda b, i: (b, i, 0)),
                  pl.BlockSpec((1, qb, IDX_WIDTH), lambda b, i: (b, i, 0)),
                  pl.BlockSpec((1, qb, LANES), lambda b, i: (b, i, 0)),
                  pl.BlockSpec((1, lk, LANES), lambda b, i: (b, 0, 0), pipeline_mode=pl.Buffered(1)),
                  pl.BlockSpec((1, lk, LANES), lambda b, i: (b, 0, v_blk), pipeline_mode=pl.Buffered(1)),
                  pl.BlockSpec((1, lk, LANES), lambda b, i: (b, 0, 0), pipeline_mode=pl.Buffered(1))],
        out_specs=pl.BlockSpec((1, SA_WIDTH, qb), lambda b, i: (b, 0, i)),
        out_shape=jax.ShapeDtypeStruct((B, SA_WIDTH, n_q), F32),
        scratch_shapes=[pltpu.VMEM((SA_KV_HEADS, lk, LANES), BF16),
                        pltpu.VMEM((SA_KV_HEADS, SA_HEAD_DIM + ONES_ROWS, lk), BF16),
                        pltpu.VMEM((lk, LANES), BF16),
                        pltpu.VMEM((SA_KV_HEADS, 8, LANES), F32),
                        pltpu.VMEM((lk, qb), I32), pltpu.VMEM((lk, qb), I16), pltpu.VMEM((lk, qb), I16),
                        pltpu.VMEM((lk, qb), BF16),
                        pltpu.VMEM((8, qb), I32), pltpu.VMEM((16, qb), F32)],
        compiler_params=_params(("parallel", "arbitrary")),
        name="dsa",
    )(q3, qi3, kw3, k_all, v_all, ki_all)


def _round_up(x, m):
    return (x + m - 1) // m * m


def _dsa(P, B, L, past, q_rot, qi_rot, k_rot, kw_rot, k_past, v_past, ik_past):
    lk_real = past + L
    topk = min(TOPK_MAX, lk_real // 4)
    qb = 2 * LANES if (past == 0 and L % (2 * LANES) == 0) else LANES
    lq = _round_up(L, qb)
    qpad = lambda t: t if lq == L else jnp.pad(t, ((0, 0), (0, lq - L), (0, 0)))
    q3 = qpad(q_rot.reshape(B, L, SA_WIDTH))
    qi3 = qpad(qi_rot.reshape(B, L, IDX_WIDTH))
    kw3 = kw_rot.reshape(B, L, LANES)
    k3 = k_rot.reshape(B, L, LANES)
    common = dict(past=past, lk_real=lk_real, topk=topk, qb=qb)
    kc = TILE_ELEMS // qb // 2 if qb == LANES else TILE_ELEMS // qb
    if past == 0:
        while L % (2 * kc):
            kc //= 2
        P3 = P.reshape(B, L, P_COLS)
        return _dsa_call(q3, qi3, kw3, k3, P3, kw3, P_SAV // LANES, lk=L, kc=kc, **common)
    lk = _round_up(lk_real, 2 * kc)
    zpad = jnp.zeros((B, lk - lk_real, LANES), F32)
    v_new = P.reshape(B, L, P_COLS)[:, :, P_SAV:P_SAV + LANES].astype(F32)
    ik_p = jnp.concatenate([ik_past.astype(F32), jnp.zeros((B, past, LANES - IDX_DIM), F32)], axis=2)
    k_all = jnp.concatenate([k_past.reshape(B, past, LANES).astype(F32), k3, zpad], axis=1)
    v_all = jnp.concatenate([v_past.reshape(B, past, LANES).astype(F32), v_new, zpad], axis=1)
    ki_all = jnp.concatenate([ik_p, kw3, zpad], axis=1)
    y = _dsa_call(q3, qi3, qpad(kw3), k_all, v_all, ki_all, 0, lk=lk, kc=kc, **common)
    return y[:, :, :L]


def _ret_kernel(q_ref, k_ref, v_ref, g_ref, cos_ref, sin_ref, dm_ref, qd_ref, kd_ref, gc_ref, s0_ref,
                y_ref, so_ref, s_ref):
    c = pl.program_id(1)
    D = RET_HEAD_DIM
    cs = dm_ref.shape[1]
    nch = q_ref.shape[0] // cs
    heads = range(RET_HEADS)
    hsl = [slice(h * D, (h + 1) * D) for h in heads]
    rows = [slice(ch * cs, (ch + 1) * cs) for ch in range(nch)]
    prob = [(ch, h) for ch in range(nch) for h in heads]

    @pl.when(c == 0)
    def _():
        s_ref[...] = s0_ref[0]

    cos, sin = cos_ref[...], sin_ref[...]
    qb_, kb_, kd_, vb_ = [], [], [], []
    for h in heads:
        q = q_ref[:, hsl[h]].astype(F32)
        k = k_ref[:, hsl[h]].astype(F32)
        k = (k * cos + pltpu.roll(k, D // 2, 1) * sin) * (D ** -0.5)
        qb_.append(_bf(q * cos + pltpu.roll(q, D // 2, 1) * sin))
        kb_.append(_bf(k))
        kd_.append([_bf(k[r] * kd_ref[:, hsl[h]]) for r in rows])
        vb_.append(_bf(v_ref[:, hsl[h]]))
    scores = {(ch, h): _dot_nt(qb_[h][rows[ch]], kb_[h][rows[ch]]) * dm_ref[h] for ch, h in prob}
    ktv = {(ch, h): _dot_tn(kd_[h][ch], vb_[h][rows[ch]]) for ch, h in prob}
    intra = {(ch, h): _dot(_bf(scores[ch, h]), vb_[h][rows[ch]]) for ch, h in prob}
    s_at = {}
    for h in heads:
        s = s_ref[h]
        for ch in range(nch):
            s_at[ch, h] = _bf(s)
            s = s * gc_ref[:, hsl[h]] + ktv[ch, h]
        s_ref[h] = s
    cross = {(ch, h): _dot(qb_[h][rows[ch]], s_at[ch, h]) for ch, h in prob}
    for ch in range(nch):
        outs = []
        for h in heads:
            o = intra[ch, h] + cross[ch, h] * qd_ref[:, hsl[h]]
            o = o * lax.rsqrt(jnp.mean(o * o, axis=-1, keepdims=True) + EPS)
            outs.append(jax.nn.silu(g_ref[rows[ch], hsl[h]].astype(F32)) * o)
        y_ref[rows[ch], :] = jnp.concatenate(outs, axis=1)

    @pl.when(c == pl.num_programs(1) - 1)
    def _():
        so_ref[0] = s_ref[...]


def _retention(P, B, L, pos, s0):
    T = P.shape[0]
    c = min(CHUNK, L)
    nc = L // c
    D = RET_HEAD_DIM
    freqs = 1.0 / (RET_ROPE_BASE ** jnp.linspace(0.0, 1.0, D // 2, dtype=F32))
    ang = pos.astype(F32)[:, None] * freqs[None, :]
    cos = jnp.concatenate([jnp.cos(ang)] * 2, axis=1)
    sin = jnp.concatenate([-jnp.sin(ang), jnp.sin(ang)], axis=1)
    log_gamma = jnp.log1p(-jnp.exp2(-5.0 - jnp.arange(RET_HEADS, dtype=F32)))
    idx = jnp.arange(c, dtype=F32)
    dmask = jnp.exp(jnp.abs(idx[:, None] - idx[None, :])[None] * log_gamma[:, None, None])
    lanes = lambda t: jnp.repeat(t, D, axis=1)
    qdec = lanes(jnp.exp((idx[:, None] + 1.0) * log_gamma[None, :]))
    kdec = lanes(jnp.exp((c - 1.0 - idx)[:, None] * log_gamma[None, :]))
    gchunk = lanes(jnp.exp(c * log_gamma)[None, :])

    W = RET_WIDTH
    tt = min(4 * c, L)
    nt = L // tt
    tok = lambda blk: pl.BlockSpec((tt, W), lambda b, i: (b * nt + i, blk))
    full = lambda shape: pl.BlockSpec(shape, lambda b, i: (0,) * len(shape))
    st_spec = pl.BlockSpec((1, RET_HEADS, D, D), lambda b, i: (b, 0, 0, 0))
    y, s_out = pl.pallas_call(
        _ret_kernel,
        grid=(B, nt),
        in_specs=[tok(P_RET // W), tok(P_RET // W + 1), tok(P_RET // W + 2), tok(P_RET // W + 3),
                  pl.BlockSpec((tt, D), lambda b, i: (i, 0)), pl.BlockSpec((tt, D), lambda b, i: (i, 0)),
                  full((RET_HEADS, c, c)), full((c, W)), full((c, W)), full((1, W)), st_spec],
        out_specs=[pl.BlockSpec((tt, W), lambda b, i: (b * nt + i, 0)), st_spec],
        out_shape=[jax.ShapeDtypeStruct((T, W), F32), jax.ShapeDtypeStruct((B, RET_HEADS, D, D), F32)],
        scratch_shapes=[pltpu.VMEM((RET_HEADS, D, D), F32)],
        compiler_params=_params(("parallel", "arbitrary")),
        name="retention",
    )(P, P, P, P, cos, sin, dmask, qdec, kdec, gchunk, s0.astype(F32))
    return y, s_out


def _merge_kernel(x_ref, g0_ref, g1_ref, g2_ref, yr_ref, ys_ref, yt_ref, wr_ref, ws_ref, wt_ref, wo_ref,
                  o_ref):
    gate = lambda g_ref: jax.nn.sigmoid(g_ref[...].astype(F32))
    m = (gate(g0_ref) * _dot(_bf(yr_ref[...]), wr_ref[...])
         + gate(g1_ref) * _dot_tn(_bf(ys_ref[0]), ws_ref[...])
         + gate(g2_ref) * _dot(_bf(yt_ref[...]), wt_ref[...]))
    o_ref[...] = x_ref[...] + _dot(_bf(m), wo_ref[...])


def _merge(x2d, P, y_rw, y_sa_t, y_ret, w_rw, w_sa, w_ret, w_o):
    T = x2d.shape[0]
    L = y_sa_t.shape[2]
    tm = min(512, L)
    nl = L // tm
    D = D_MODEL
    tok = lambda w, blk: pl.BlockSpec((tm, w), lambda i: (i, blk))
    full = lambda shape: pl.BlockSpec(shape, lambda i: (0, 0))
    return pl.pallas_call(
        _merge_kernel,
        grid=(T // tm,),
        in_specs=[tok(D, 0), tok(D, 0), tok(D, 1), tok(D, 2), tok(RW_WIDTH, 0),
                  pl.BlockSpec((1, SA_WIDTH, tm), lambda i: (i // nl, 0, i % nl)),
                  tok(RET_WIDTH, 0), full((RW_WIDTH, D)), full((SA_WIDTH, D)), full((RET_WIDTH, D)),
                  full((D, D))],
        out_specs=tok(D, 0),
        out_shape=jax.ShapeDtypeStruct((T, D), F32),
        compiler_params=_params(("parallel",)),
        name="merge",
    )(x2d, P, P, P, y_rw, y_sa_t, y_ret, w_rw, w_sa, w_ret, w_o)


def _mlp_kernel(x_ref, g_ref, wu_ref, wd_ref, gf_ref, o_ref, h_ref, acc_ref, *, final_norm):
    j = pl.program_id(1)

    @pl.when(j == 0)
    def _():
        x = x_ref[...]
        ms = jnp.mean(x * x, axis=-1, keepdims=True)
        h_ref[...] = _bf(x * lax.rsqrt(ms + EPS) * g_ref[...])
        acc_ref[...] = jnp.zeros_like(acc_ref)

    u = jnp.maximum(_dot(h_ref[...], wu_ref[...]), 0.0)
    acc_ref[...] += _dot(_bf(u * u), wd_ref[...])

    @pl.when(j == pl.num_programs(1) - 1)
    def _():
        xn = x_ref[...] + acc_ref[...]
        if final_norm:
            ms = jnp.mean(xn * xn, axis=-1, keepdims=True)
            xn = xn * lax.rsqrt(ms + EPS) * gf_ref[...]
        o_ref[...] = xn


def _mlp(x2d, g, w_up, w_down, g_final, final_norm):
    T = x2d.shape[0]
    tm = min(1024, T)
    tf = 1024
    D = D_MODEL
    return pl.pallas_call(
        functools.partial(_mlp_kernel, final_norm=final_norm),
        grid=(T // tm, D_FF // tf),
        in_specs=[pl.BlockSpec((tm, D), lambda i, j: (i, 0)),
                  pl.BlockSpec((1, D), lambda i, j: (0, 0)),
                  pl.BlockSpec((D, tf), lambda i, j: (0, j)),
                  pl.BlockSpec((tf, D), lambda i, j: (j, 0)),
                  pl.BlockSpec((1, D), lambda i, j: (0, 0))],
        out_specs=pl.BlockSpec((tm, D), lambda i, j: (i, 0)),
        out_shape=jax.ShapeDtypeStruct((T, D), F32),
        scratch_shapes=[pltpu.VMEM((tm, D), BF16), pltpu.VMEM((tm, D), F32)],
        compiler_params=_params(("parallel", "arbitrary")),
        name="mlp",
    )(x2d, g, w_up, w_down, g_final)


def _layer(x2d, B, L, past, caches, lp, wts, g_final, final_norm):
    k_past, v_past, ik_past, s_rw, shift_rw, s_ret = caches
    pos = past + jnp.arange(L, dtype=jnp.int32)
    row = lambda t: t.reshape(1, -1).astype(F32)
    P = _in_proj(x2d, row(lp['norm1_g']), wts['w_in'], wts['layer'])
    q_rot, qi_rot, k_rot, kw_rot = _sa_prep(P, L, _sa_tables(pos))
    y_rw, s_rw_new = _rwkv(P, B, L, shift_rw, s_rw, lp)
    y_sa = _dsa(P, B, L, past, q_rot, qi_rot, k_rot, kw_rot, k_past, v_past, ik_past)
    y_ret, s_ret_new = _retention(P, B, L, pos, s_ret)
    x2d = _merge(x2d, P, y_rw, y_sa, y_ret, wts['w_br_rwkv'], wts['w_br_dsa'], wts['w_br_ret'], wts['w_o'])
    x2d = _mlp(x2d, row(lp['norm2_g']), wts['w_up'], wts['w_down'], g_final, final_norm)
    P3 = P.reshape(B, L, P_COLS)
    last = P3[:, L - 1].astype(F32)
    shift_new = jnp.concatenate([last[:, P_RKV:P_RKV + 3 * RW_WIDTH], last[:, P_LORA:P_LORA + RW_LORA]], axis=1)
    k_new = k_rot.reshape(B, L, SA_KV_HEADS, SA_HEAD_DIM)
    v_new = P3[:, :, P_SAV:P_SAV + SA_KV_WIDTH].astype(F32).reshape(B, L, SA_KV_HEADS, SA_HEAD_DIM)
    ik_new = kw_rot.reshape(B, L, LANES)[:, :, :IDX_DIM]
    return x2d, (k_new, v_new, ik_new, s_rw_new, shift_new, s_ret_new)


def kernel(x_prompt, x_sample, cache_dsa_k, cache_dsa_v, cache_dsa_ik, state_rwkv, state_rwkv_shift, state_ret, norm1_g, w_in, rwkv_mu, rwkv_w0, rwkv_w2, rwkv_a0, rwkv_a2, rwkv_g2, rwkv_k_k, rwkv_k_a, rwkv_r_k, rwkv_lnx_g, w_br_rwkv, w_br_dsa, w_br_ret, w_o, norm2_g, w_up, w_down, final_norm_g):
    params = {
        'norm1_g': norm1_g, 'rwkv_mu': rwkv_mu, 'rwkv_w0': rwkv_w0, 'rwkv_w2': rwkv_w2,
        'rwkv_a0': rwkv_a0, 'rwkv_a2': rwkv_a2, 'rwkv_g2': rwkv_g2, 'rwkv_k_k': rwkv_k_k,
        'rwkv_k_a': rwkv_k_a, 'rwkv_r_k': rwkv_r_k, 'rwkv_lnx_g': rwkv_lnx_g, 'norm2_g': norm2_g,
    }
    depth = w_in.shape[0]
    Bp, Lp, D = x_prompt.shape
    Bs, Ls, _ = x_sample.shape
    past_s = cache_dsa_k.shape[2]
    xp = x_prompt.reshape(Bp * Lp, D).astype(F32)
    xs = x_sample.reshape(Bs * Ls, D).astype(F32)
    g_final = final_norm_g.reshape(1, D).astype(F32)
    zero_p = (None, None, None,
              jnp.zeros((Bp, RW_HEADS, RW_HEAD_DIM, RW_HEAD_DIM), F32), jnp.zeros((Bp, RW_COLS), F32),
              jnp.zeros((Bp, RET_HEADS, RET_HEAD_DIM, RET_HEAD_DIM), F32))
    p_states = [[] for _ in range(6)]
    s_states = [[] for _ in range(6)]
    w_all = _w_prep(w_in)
    for i in range(depth):
        lp = {name: arr[i] for name, arr in params.items()}
        wts = {'w_in': w_all, 'layer': i, 'w_br_rwkv': _bf(w_br_rwkv[i]), 'w_br_dsa': _bf(w_br_dsa[i]),
               'w_br_ret': _bf(w_br_ret[i]), 'w_o': _bf(w_o[i]), 'w_up': _bf(w_up[i]), 'w_down': _bf(w_down[i])}
        final = i == depth - 1
        cache_s = (cache_dsa_k[i], cache_dsa_v[i], cache_dsa_ik[i], state_rwkv[i], state_rwkv_shift[i],
                   state_ret[i])
        xp, new_p = _layer(xp, Bp, Lp, 0, zero_p, lp, wts, g_final, final)
        xs, new_s = _layer(xs, Bs, Ls, past_s, cache_s, lp, wts, g_final, final)
        for j in range(6):
            p_states[j].append(new_p[j])
            s_states[j].append(new_s[j])
    y_prompt = xp.reshape(Bp, Lp, D)
    y_sample = xs.reshape(Bs, Ls, D)
    p_out = [jnp.stack(t, axis=0) for t in p_states]
    s_out = [jnp.stack(t, axis=0) for t in s_states]
    return (y_prompt, y_sample, *p_out, *s_out)
```

```python
import functools

import numpy as np
import jax
import jax.numpy as jnp
from jax import lax
from jax.experimental import pallas as pl
from jax.experimental.pallas import tpu as pltpu

F32 = jnp.float32
BF16 = jnp.bfloat16
I32 = jnp.int32
I16 = jnp.int16

D_MODEL = 1024
CHUNK = 64
Q_BLOCK = 128
EPS = 1e-6

RW_HEADS = 8
RW_HEAD_DIM = 64
RW_WIDTH = RW_HEADS * RW_HEAD_DIM
RW_DECAY_LORA = 64
RW_ICLR_LORA = 64
RW_GATE_LORA = 128
RW_LORA = RW_DECAY_LORA + RW_ICLR_LORA + RW_GATE_LORA
RW_COLS = 3 * RW_WIDTH + RW_LORA
RW_GN_EPS = 64e-5

SA_HEADS = 8
SA_KV_HEADS = 2
SA_HEAD_DIM = 64
SA_WIDTH = SA_HEADS * SA_HEAD_DIM
SA_KV_WIDTH = SA_KV_HEADS * SA_HEAD_DIM
IDX_HEADS = 4
IDX_DIM = 64
IDX_WIDTH = IDX_HEADS * IDX_DIM
TOPK_MAX = 256
ROPE_THETA = 500000.0
ROPE_DIM = SA_HEAD_DIM // 4
INDEX_SCALE = (IDX_DIM ** -0.5) * (IDX_HEADS ** -0.5)
SA_COLS = SA_WIDTH + 2 * SA_KV_WIDTH + IDX_WIDTH + IDX_DIM + IDX_HEADS

RET_HEADS = 4
RET_HEAD_DIM = 128
RET_WIDTH = RET_HEADS * RET_HEAD_DIM
RET_ROPE_BASE = 10000.0
RET_COLS = 4 * RET_WIDTH

N_BRANCH = 3
GATE_COLS = N_BRANCH * D_MODEL
IN_COLS = RW_COLS + SA_COLS + RET_COLS + GATE_COLS
D_FF = 4 * D_MODEL

LANES = 128
TILE_ELEMS = 64 * 8 * LANES

P_GATE = 0
P_RET = P_GATE + GATE_COLS
P_RKV = P_RET + RET_COLS
P_SAQ = P_RKV + 3 * RW_WIDTH
P_LORA = P_SAQ + SA_WIDTH
P_QI = P_LORA + RW_LORA
P_SAK = P_QI + IDX_WIDTH
P_SAV = P_SAK + SA_KV_WIDTH
P_KIWI = P_SAV + SA_KV_WIDTH
P_COLS = 8192
INT_MIN = -2 ** 31
NEG_BIG = -1e30
LOG2_E = 1.4426950408889634
VMEM_LIMIT = 56 * 1024 * 1024


def _bf(x):
    return x.astype(BF16)


def _dot(a, b):
    return jnp.dot(a, b, preferred_element_type=F32)


def _dot_nt(a, b):
    return lax.dot_general(a, b, (((1,), (1,)), ((), ())), preferred_element_type=F32)


def _dot_tn(a, b):
    return lax.dot_general(a, b, (((0,), (0,)), ((), ())), preferred_element_type=F32)


def _dot_split2(a_exact, x):
    hi = _bf(x)
    lo = _bf(x - hi.astype(F32))
    return _dot(a_exact, hi) + _dot(a_exact, lo)


def _params(sem):
    return pltpu.CompilerParams(dimension_semantics=sem, vmem_limit_bytes=VMEM_LIMIT)


def _in_proj_kernel(x_ref, g_ref, w_ref, o_ref, h_ref):
    @pl.when(pl.program_id(1) == 0)
    def _():
        x = x_ref[...]
        ms = jnp.mean(x * x, axis=-1, keepdims=True)
        h_ref[...] = _bf(x * lax.rsqrt(ms + EPS) * g_ref[...])

    o_ref[...] = _dot(h_ref[...], w_ref[0]).astype(o_ref.dtype)


def _in_proj(x2d, g, w_all, layer):
    T = x2d.shape[0]
    tm = min(1024, T)
    tn = 1024
    return pl.pallas_call(
        _in_proj_kernel,
        grid=(T // tm, P_COLS // tn),
        in_specs=[pl.BlockSpec((tm, D_MODEL), lambda i, j: (i, 0)),
                  pl.BlockSpec((1, D_MODEL), lambda i, j: (0, 0)),
                  pl.BlockSpec((1, D_MODEL, tn), lambda i, j: (layer, 0, j))],
        out_specs=pl.BlockSpec((tm, tn), lambda i, j: (i, j)),
        out_shape=jax.ShapeDtypeStruct((T, P_COLS), BF16),
        scratch_shapes=[pltpu.VMEM((tm, D_MODEL), BF16)],
        compiler_params=_params(("parallel", "arbitrary")),
        name="in_proj",
    )(x2d, g, w_all)


KIWI_COLS = IDX_DIM + IDX_HEADS
TAIL0 = RW_COLS + SA_COLS - KIWI_COLS
TAIL_W = (IN_COLS - TAIL0 + LANES - 1) // LANES * LANES


def _w_prep_kernel(w_ref, o_ref):
    x = w_ref[0]
    lane = lax.broadcasted_iota(I32, (x.shape[0], LANES), 1)

    def put(off, v):
        o_ref[0, :, off:off + v.shape[1]] = _bf(v)

    nslab = TAIL_W // LANES
    part = IN_COLS - TAIL0 - (nslab - 1) * LANES
    slabs = [x[:, TAIL0 + s * LANES:TAIL0 + (s + 1) * LANES] for s in range(nslab - 1)]
    slabs.append(jnp.concatenate([x[:, IN_COLS - part:], jnp.zeros((x.shape[0], LANES - part), F32)], axis=1))
    rolled = [pltpu.roll(t, LANES - KIWI_COLS, 1) for t in slabs]
    for s in range((RET_COLS + GATE_COLS) // LANES):
        v = jnp.where(lane < LANES - KIWI_COLS, rolled[s], rolled[s + 1])
        put((P_RET if s < RET_COLS // LANES else P_GATE - RET_COLS) + s * LANES, v)
    sa = RW_COLS
    put(P_RKV, x[:, 0:3 * RW_WIDTH])
    put(P_LORA, x[:, 3 * RW_WIDTH:RW_COLS])
    put(P_SAQ, x[:, sa:sa + SA_WIDTH])
    put(P_SAK, x[:, sa + SA_WIDTH:sa + SA_WIDTH + SA_KV_WIDTH])
    put(P_SAV, x[:, sa + SA_WIDTH + SA_KV_WIDTH:sa + SA_WIDTH + 2 * SA_KV_WIDTH])
    put(P_QI, x[:, sa + SA_WIDTH + 2 * SA_KV_WIDTH:TAIL0])
    put(P_KIWI, jnp.where(lane < KIWI_COLS, x[:, TAIL0:TAIL0 + LANES], 0.0))
    put(P_KIWI + LANES, jnp.zeros((x.shape[0], P_COLS - P_KIWI - LANES), F32))


def _w_prep(w_in):
    depth, d, _ = w_in.shape
    tm = 256
    return pl.pallas_call(
        _w_prep_kernel,
        grid=(depth, d // tm),
        in_specs=[pl.BlockSpec((1, tm, IN_COLS), lambda l, i: (l, i, 0))],
        out_specs=pl.BlockSpec((1, tm, P_COLS), lambda l, i: (l, i, 0)),
        out_shape=jax.ShapeDtypeStruct((depth, d, P_COLS), BF16),
        compiler_params=_params(("parallel", "parallel")),
        name="w_prep",
    )(w_in)


def _rot_slab(x, c, s_lo, s_hi, shift):
    return x * c + pltpu.roll(x, LANES - shift, 1) * s_lo + pltpu.roll(x, shift, 1) * s_hi


def _sa_prep_kernel(q_ref, qi_ref, k_ref, kw_ref, c_ref, s1_ref, s2_ref, ck_ref, s1k_ref, s2k_ref,
                    qo_ref, qio_ref, ko_ref, kwo_ref):
    c, s1, s2 = c_ref[...], s1_ref[...], s2_ref[...]
    half = ROPE_DIM // 2
    scale = SA_HEAD_DIM ** -0.5 * LOG2_E
    for s in range(SA_WIDTH // LANES):
        sl = slice(s * LANES, (s + 1) * LANES)
        qo_ref[:, sl] = _rot_slab(q_ref[:, sl].astype(F32), c, s1, s2, half) * scale
    for s in range(IDX_WIDTH // LANES):
        sl = slice(s * LANES, (s + 1) * LANES)
        qio_ref[:, sl] = _rot_slab(qi_ref[:, sl].astype(F32), c, s1, s2, half)
    ko_ref[...] = _rot_slab(k_ref[...].astype(F32), c, s1, s2, half)
    kwo_ref[...] = _rot_slab(kw_ref[...].astype(F32), ck_ref[...], s1k_ref[...], s2k_ref[...], half)


def _sa_prep(P, L, tabs):
    T = P.shape[0]
    tm = min(512, L)
    nl = L // tm
    tab_spec = pl.BlockSpec((tm, LANES), lambda i: (i % nl, 0))
    return pl.pallas_call(
        _sa_prep_kernel,
        grid=(T // tm,),
        in_specs=[pl.BlockSpec((tm, SA_WIDTH), lambda i: (i, P_SAQ // SA_WIDTH)),
                  pl.BlockSpec((tm, IDX_WIDTH), lambda i: (i, P_QI // IDX_WIDTH)),
                  pl.BlockSpec((tm, LANES), lambda i: (i, P_SAK // LANES)),
                  pl.BlockSpec((tm, LANES), lambda i: (i, P_KIWI // LANES))] + [tab_spec] * 6,
        out_specs=[pl.BlockSpec((tm, SA_WIDTH), lambda i: (i, 0)),
                   pl.BlockSpec((tm, IDX_WIDTH), lambda i: (i, 0)),
                   pl.BlockSpec((tm, LANES), lambda i: (i, 0)),
                   pl.BlockSpec((tm, LANES), lambda i: (i, 0))],
        out_shape=[jax.ShapeDtypeStruct((T, SA_WIDTH), F32),
                   jax.ShapeDtypeStruct((T, IDX_WIDTH), F32),
                   jax.ShapeDtypeStruct((T, LANES), F32),
                   jax.ShapeDtypeStruct((T, LANES), F32)],
        compiler_params=_params(("parallel",)),
        name="sa_prep",
    )(P, P, P, P, *tabs)


def _sa_tables(pos):
    half = ROPE_DIM // 2
    freqs = 1.0 / (ROPE_THETA ** (jnp.arange(0, ROPE_DIM, 2, dtype=F32) / ROPE_DIM))
    ang = pos.astype(F32)[:, None] * freqs[None, :]
    cos, sin = jnp.cos(ang), jnp.sin(ang)
    n = pos.shape[0]
    pad = SA_HEAD_DIM - ROPE_DIM
    c_head = jnp.concatenate([cos, cos, jnp.ones((n, pad), F32)], axis=1)
    s1_head = jnp.concatenate([-sin, jnp.zeros((n, half + pad), F32)], axis=1)
    s2_head = jnp.concatenate([jnp.zeros((n, half), F32), sin, jnp.zeros((n, pad), F32)], axis=1)
    one, zero = jnp.ones((n, SA_HEAD_DIM), F32), jnp.zeros((n, SA_HEAD_DIM), F32)
    two = lambda t: jnp.concatenate([t, t], axis=1)
    return (two(c_head), two(s1_head), two(s2_head),
            jnp.concatenate([c_head, one], axis=1), jnp.concatenate([s1_head, zero], axis=1),
            jnp.concatenate([s2_head, zero], axis=1))


def _softplus(u):
    return jnp.maximum(u, 0.0) + jnp.log(1.0 + jnp.exp(-jnp.abs(u)))


def _rwkv_kernel(r_ref, k_ref, v_ref, lo_ref, shr_ref, shk_ref, shv_ref, shlo_ref, s0_ref,
                 mur_ref, muk_ref, muv_ref, mulo_ref, w0_ref, w2_ref, a0_ref, a2_ref, g2_ref,
                 kk_ref, ka_ref, rk_ref, lnx_ref, e_ref, tin_ref, tsuf_ref,
                 y_ref, so_ref,
                 cr_ref, ck_ref, cv_ref, clo_ref, s_ref,
                 kap_ref, rt_ref, bh_ref, kh_ref, bt_ref, kt_ref, vv_ref, gc_ref, yy_ref, rr_ref,
                 ac_ref, cc_ref):
    c = pl.program_id(1)
    TT = r_ref.shape[0]
    nch = TT // CHUNK
    N = RW_HEAD_DIM

    @pl.when(c == 0)
    def _():
        cr_ref[0:1, :] = shr_ref[0]
        ck_ref[0:1, :] = shk_ref[0]
        cv_ref[0:1, :] = shv_ref[0]
        clo_ref[0:1, :] = shlo_ref[0]
        s_ref[...] = s0_ref[0]

    def lerp(p_ref, carry_ref, mu_ref):
        p = p_ref[...].astype(F32)
        rolled = pltpu.roll(p, 1, 0)
        row0 = lax.broadcasted_iota(I32, p.shape, 0) == 0
        prev = jnp.where(row0, carry_ref[0:1, :], rolled)
        carry_ref[0:1, :] = p[TT - 1:TT, :]
        return p + (prev - p) * mu_ref[...]

    xr = lerp(r_ref, cr_ref, mur_ref)
    xk = lerp(k_ref, ck_ref, muk_ref)
    xv = lerp(v_ref, cv_ref, muv_ref)
    xlo = lerp(lo_ref, clo_ref, mulo_ref)
    xw = xlo[:, :RW_DECAY_LORA]
    xa = xlo[:, RW_DECAY_LORA:RW_DECAY_LORA + RW_ICLR_LORA]
    xg = xlo[:, RW_DECAY_LORA + RW_ICLR_LORA:]

    z = w0_ref[...] + _dot(_bf(jnp.tanh(xw)), w2_ref[...])
    w = -_softplus(-z) - 0.5
    ld = -jnp.exp(w)
    a = jax.nn.sigmoid(a0_ref[...] + _dot(_bf(xa), a2_ref[...]))
    gate = _dot(_bf(jax.nn.sigmoid(xg)), g2_ref[...])
    e_blk = e_ref[...]
    kk = xk * kk_ref[...]
    kk = kk / jnp.maximum(jnp.sqrt(_dot(_bf(kk * kk), e_blk)), 1e-12)
    k2 = xk * (1.0 + (a - 1.0) * ka_ref[...])
    bb = kk * a
    bonus = _dot(_bf(xr * k2 * rk_ref[...]), e_blk) * xv

    lin = _dot_split2(tin_ref[...], ld)
    lsuf = _dot_split2(tsuf_ref[...], ld)
    e_in = jnp.exp(lin)
    e_ninv = jnp.exp(-lin)
    e_suf = jnp.exp(lsuf)
    kap_ref[...] = kk * jnp.exp(lin - ld)
    rt_ref[...] = xr * e_in
    bh_ref[...] = bb * e_ninv
    kh_ref[...] = k2 * e_ninv
    bt_ref[...] = bb * e_suf
    kt_ref[...] = k2 * e_suf
    vv_ref[...] = xv
    gc_ref[...] = jnp.exp(lin + lsuf)

    ri = lax.broadcasted_iota(I32, (CHUNK, CHUNK), 0)
    ci = lax.broadcasted_iota(I32, (CHUNK, CHUNK), 1)
    strict = ri > ci
    incl = ri >= ci
    eye = (ri == ci).astype(F32)
    heads = range(RW_HEADS)
    hsl = [slice(h * N, (h + 1) * N) for h in heads]

    per_it = next(n for n in (4, 2, 1) if nch % n == 0)

    def coef_body(it, carry):
        chs = [it * per_it + t for t in range(per_it)]
        pairs = [(t, h) for t in range(per_it) for h in heads]
        rows = [pl.ds(pl.multiple_of(ch * CHUNK, CHUNK), CHUNK) for ch in chs]
        ld = lambda ref, p: ref[rows[p[0]], hsl[p[1]]]
        kap = [ld(kap_ref, p) for p in pairs]
        rt = [ld(rt_ref, p) for p in pairs]
        vh = [ld(vv_ref, p) for p in pairs]
        idx = range(len(pairs))
        gmat = [_dot_nt(_bf(jnp.concatenate([kap[n], rt[n]], axis=0)),
                        _bf(jnp.concatenate([ld(bh_ref, pairs[n]), ld(kh_ref, pairs[n])], axis=0)))
                for n in idx]
        n_ab = [jnp.where(strict, g[:CHUNK, :CHUNK], 0.0) for g in gmat]
        m_rb = [jnp.where(incl, g[CHUNK:, :CHUNK], 0.0) for g in gmat]
        m_v = [_bf(jnp.concatenate([jnp.where(strict, g[:CHUNK, CHUNK:], 0.0),
                                    jnp.where(incl, g[CHUNK:, CHUNK:], 0.0)], axis=0)) for g in gmat]
        mv = [_dot(m_v[n], _bf(vh[n])) for n in idx]
        x_inv = [eye - t for t in n_ab]
        pw = n_ab
        for _ in range(5):
            pwb = [_bf(p) for p in pw]
            pw = [_dot(p, p) for p in pwb]
            x_inv = [x + _dot(_bf(x), _bf(p)) for x, p in zip(x_inv, pw)]
        w = [_dot(_bf(x_inv[n]), _bf(jnp.concatenate([kap[n], mv[n][:CHUNK]], axis=1))) for n in idx]
        wb = [_bf(t) for t in w]
        ry = [jnp.concatenate([rt[n], mv[n][CHUNK:]], axis=1) - _dot(_bf(m_rb[n]), wb[n]) for n in idx]
        dmat = [_dot_tn(wb[n], _bf(ld(bt_ref, pairs[n]))) for n in idx]
        vtk = [_dot_tn(_bf(vh[n]), _bf(ld(kt_ref, pairs[n]))) for n in idx]
        for n, (t, h) in enumerate(pairs):
            rr_ref[rows[t], hsl[h]] = ry[n][:, :N]
            yy_ref[rows[t], hsl[h]] = ry[n][:, N:]
            ac_ref[chs[t], h] = -dmat[n][:N]
            cc_ref[chs[t], h] = vtk[n] - dmat[n][N:]
        return carry

    lax.fori_loop(0, nch // per_it, coef_body, 0)

    def state_body(ch, carry):
        rows = pl.ds(pl.multiple_of(ch * CHUNK, CHUNK), CHUNK)
        s_old = [s_ref[h] for h in heads]
        sb = [_bf(t) for t in s_old]
        s_new = [_dot(sb[h], _bf(ac_ref[ch, h])) for h in heads]
        y_c = [_dot_nt(_bf(rr_ref[rows, hsl[h]]), sb[h]) for h in heads]
        for h in heads:
            gch = gc_ref[pl.ds(pl.multiple_of(ch * CHUNK, CHUNK), 1), hsl[h]]
            s_ref[h] = s_old[h] * gch + s_new[h] + cc_ref[ch, h]
            yy_ref[rows, hsl[h]] = yy_ref[rows, hsl[h]] + y_c[h]
        return carry

    lax.fori_loop(0, nch, state_body, 0)

    y = yy_ref[...]
    mean = _dot(_bf(y), e_blk) * (1.0 / N)
    d = y - mean
    var = _dot(_bf(d * d), e_blk) * (1.0 / N)
    yn = d * lax.rsqrt(var + RW_GN_EPS) * lnx_ref[...]
    y_ref[...] = (yn + bonus) * gate

    @pl.when(c == pl.num_programs(1) - 1)
    def _():
        so_ref[0] = s_ref[...]


def _rwkv(P, B, L, shift_prev, s0, lp):
    T = P.shape[0]
    TT = min(256, L)
    nt = L // TT
    W = RW_WIDTH
    row = lambda t: t.reshape(1, -1).astype(F32)
    mu = lp['rwkv_mu']
    sh = shift_prev.astype(F32)
    pieces = lambda t: (t[..., 0:W], t[..., W:2 * W], t[..., 2 * W:3 * W], t[..., 3 * W:])
    mu_r, mu_k, mu_v, mu_lo = [row(t) for t in pieces(mu)]
    sh_r, sh_k, sh_v, sh_lo = [t.reshape(B, 1, -1) for t in pieces(sh)]
    hid = jnp.arange(W) // RW_HEAD_DIM
    e_blk = (hid[:, None] == hid[None, :]).astype(BF16)
    ti = jnp.arange(TT)
    same = (ti[:, None] // CHUNK) == (ti[None, :] // CHUNK)
    tri_in = (same & (ti[None, :] <= ti[:, None])).astype(BF16)
    tri_suf = (same & (ti[None, :] > ti[:, None])).astype(BF16)

    tok = lambda w, blk: pl.BlockSpec((TT, w), lambda b, c: (b * nt + c, blk))
    full = lambda shape: pl.BlockSpec(shape, lambda b, c: (0,) * len(shape))
    shs = lambda w: pl.BlockSpec((1, 1, w), lambda b, c: (b, 0, 0))
    st_spec = pl.BlockSpec((1, RW_HEADS, RW_HEAD_DIM, RW_HEAD_DIM), lambda b, c: (b, 0, 0, 0))
    big = lambda: pltpu.VMEM((TT, W), F32)
    y, s_out = pl.pallas_call(
        _rwkv_kernel,
        grid=(B, nt),
        in_specs=[tok(W, P_RKV // W), tok(W, P_RKV // W + 1), tok(W, P_RKV // W + 2),
                  tok(RW_LORA, P_LORA // RW_LORA),
                  shs(W), shs(W), shs(W), shs(RW_LORA), st_spec,
                  full((1, W)), full((1, W)), full((1, W)), full((1, RW_LORA)),
                  full((1, W)), full((RW_DECAY_LORA, W)), full((1, W)), full((RW_ICLR_LORA, W)),
                  full((RW_GATE_LORA, W)), full((1, W)), full((1, W)), full((1, W)), full((1, W)),
                  full((W, W)), full((TT, TT)), full((TT, TT))],
        out_specs=[pl.BlockSpec((TT, W), lambda b, c: (b * nt + c, 0)), st_spec],
        out_shape=[jax.ShapeDtypeStruct((T, W), F32),
                   jax.ShapeDtypeStruct((B, RW_HEADS, RW_HEAD_DIM, RW_HEAD_DIM), F32)],
        scratch_shapes=[pltpu.VMEM((8, W), F32), pltpu.VMEM((8, W), F32), pltpu.VMEM((8, W), F32),
                        pltpu.VMEM((8, RW_LORA), F32),
                        pltpu.VMEM((RW_HEADS, RW_HEAD_DIM, RW_HEAD_DIM), F32)] + [big() for _ in range(10)]
        + [pltpu.VMEM((TT // CHUNK, RW_HEADS, RW_HEAD_DIM, RW_HEAD_DIM), F32) for _ in range(2)],
        compiler_params=_params(("parallel", "arbitrary")),
        name="rwkv",
    )(P, P, P, P, sh_r, sh_k, sh_v, sh_lo, s0.astype(F32),
      mu_r, mu_k, mu_v, mu_lo, row(lp['rwkv_w0']), _bf(lp['rwkv_w2']), row(lp['rwkv_a0']),
      _bf(lp['rwkv_a2']), _bf(lp['rwkv_g2']), row(lp['rwkv_k_k']), row(lp['rwkv_k_a']),
      row(lp['rwkv_r_k']), row(lp['rwkv_lnx_g']), e_blk, tri_in, tri_suf)
    return y, s_out


ONES_ROWS = 16
BOUND_SLACK = 1.05
SAFE_BOUND = 60.0


def _dsa_kernel(q_ref, qi_ref, kw_ref, k_ref, v_ref, ki_ref, y_ref,
                kb_ref, vt_ref, kib_ref, kmax_ref, keys_ref, hi_ref, lo_ref, bias_ref, pstar_ref, red_ref,
                *, past, lk_real, topk, kc):
    i = pl.program_id(1)
    qb = q_ref.shape[1]
    lk = k_ref.shape[1]
    kf = float(topk)
    HD = SA_HEAD_DIM

    ones_sq = jnp.ones((LANES, LANES), BF16)

    def head_slab(x, odd):
        lane = lax.broadcasted_iota(I32, x.shape, 1)
        return jnp.where(lane < HD, pltpu.roll(x, HD, 1) if odd else x, 0.0)

    @pl.when(i == 0)
    def _():
        kib_ref[...] = _bf(ki_ref[0])
        k = k_ref[0]
        lane = lax.broadcasted_iota(I32, k.shape, 1)
        for c in range(SA_KV_HEADS):
            kc_b = _bf(head_slab(k, c == 1))
            kb_ref[c] = jnp.where(lane == HD, jnp.ones_like(kc_b), kc_b)
            kf32 = kc_b.astype(F32)
            n2 = _dot(_bf(kf32 * kf32), ones_sq)
            kmax_ref[c] = jnp.broadcast_to(jnp.max(n2, axis=0, keepdims=True), (8, LANES))
        vt = v_ref[0].astype(F32).T
        for c in range(SA_KV_HEADS):
            vt_ref[c, 0:HD, :] = _bf(vt[c * HD:(c + 1) * HD, :])
            vt_ref[c, HD:HD + ONES_ROWS, :] = jnp.ones((ONES_ROWS, lk), BF16)

    qpos = past + i * qb + lax.broadcasted_iota(I32, (1, qb), 1)
    limit = jnp.minimum((qpos // CHUNK + 1) * CHUNK, lk_real)
    hi = jnp.minimum(past + (i + 1) * qb, lk_real)
    nk = lax.shift_right_logical(hi + (2 * kc - 1), kc.bit_length())
    sub_idx = lax.broadcasted_iota(I32, (kc, qb), 0)
    kwt = kw_ref[0].T
    wi_h = [kwt[IDX_DIM + h:IDX_DIM + h + 1, :] * INDEX_SCALE for h in range(IDX_HEADS)]
    qi = qi_ref[0]
    qi_h = [_bf(qi[:, h * IDX_DIM:(h + 1) * IDX_DIM]) for h in range(IDX_HEADS)]

    def chunk(j):
        return pl.ds(pl.multiple_of(j * kc, kc), kc)

    def pair(t):
        return (2 * t, 2 * t + 1)

    def score_body(t, carry):
        js = pair(t)
        dots = [[_dot_nt(kib_ref[chunk(j), 0:IDX_DIM], qi_h[h]) for h in range(IDX_HEADS)] for j in js]
        for j, d in zip(js, dots):
            s = jnp.zeros((kc, qb), F32)
            for h in range(IDX_HEADS):
                s = s + jnp.maximum(d[h], 0.0) * wi_h[h]
            s = jnp.where(s == 0.0, 0.0, s)
            bits = lax.bitcast_convert_type(s, I32)
            key = bits ^ ((bits >> 31) & 0x7FFFFFFF)
            key = jnp.where((sub_idx + j * kc) < limit, key, INT_MIN)
            keys_ref[chunk(j), :] = key
            hi_ref[chunk(j), :] = (key >> 16).astype(I16)
        return carry

    lax.fori_loop(0, nk, score_body, 0)

    def fold(m, rows):
        parts = [m[r * rows:(r + 1) * rows] for r in range(kc // rows)]
        while len(parts) > 1:
            parts = [a + b for a, b in zip(parts[0::2], parts[1::2])]
        return parts[0]

    max_trips = lk // (2 * kc)

    def short_trips(body, carry):
        base = jnp.int32(0)
        p = 1 << (max_trips.bit_length() - 1)
        while p:
            take = (nk & p) != 0

            def run(c, base=base, p=p):
                for s in range(p):
                    c = body(base + s, c)
                return c

            carry = lax.cond(take, run, lambda c: c, carry)
            base = base + jnp.where(take, p, 0)
            p //= 2
        return carry

    def col_total(part):
        rows = part.shape[0]
        red_ref[0:rows, :] = part
        parts = [red_ref[r:r + 1, :] for r in range(rows)]
        while len(parts) > 1:
            parts = [a + b for a, b in zip(parts[0::2], parts[1::2])]
        return parts[0]

    def count(pred):
        def body(t, acc):
            for j in pair(t):
                acc = acc + fold(jnp.where(pred(keys_ref[chunk(j), :], j), 1.0, 0.0), 8)
            return acc

        return col_total(short_trips(body, jnp.zeros((8, qb), F32)))

    def rows16(x8):
        return jnp.concatenate([x8, x8], axis=0).astype(I16)

    def slab_count(x, thr16, strict):
        parts = []
        for r in range(kc // 16):
            xs = x[r * 16:(r + 1) * 16]
            hit = xs > thr16 if strict else xs >= thr16
            parts.append(jnp.where(hit, jnp.int16(1), jnp.int16(0)))
        while len(parts) > 1:
            parts = [a + b for a, b in zip(parts[0::2], parts[1::2])]
        return parts[0]

    def sublane_allsum(c):
        parts = [c] + [pltpu.roll(c, sh, 0) for sh in range(1, 8)]
        while len(parts) > 1:
            parts = [a + b for a, b in zip(parts[0::2], parts[1::2])]
        return parts[0]

    def count16(ref, thr16, strict=False):
        def body(t, acc):
            for j in pair(t):
                acc = acc + slab_count(ref[chunk(j), :], thr16, strict)
            return acc

        w = pltpu.bitcast(short_trips(body, jnp.zeros((16, qb), I16)), I32)
        s = (w & 0xFFFF) + lax.shift_right_logical(w, 16)
        tiles = [s[:, n * LANES:(n + 1) * LANES] for n in range(qb // LANES)]
        out = []
        for a, b in zip(tiles[0::2], tiles[1::2]):
            c = sublane_allsum(a + lax.shift_left(b, 16))
            out += [c & 0xFFFF, lax.shift_right_logical(c, 16)]
        if len(tiles) % 2:
            out.append(sublane_allsum(tiles[-1]))
        return jnp.concatenate(out, axis=1) if len(out) > 1 else out[0]

    def kth_bits16(ref, need):
        def bit_body(it, u):
            cand = u | lax.shift_left(jnp.int32(1), 15 - it)
            return jnp.where(count16(ref, rows16(cand - 32768)) >= need, cand, u)

        return lax.fori_loop(0, 16, bit_body, jnp.zeros((8, qb), I32))

    t_hi8 = kth_bits16(hi_ref, topk) - 32768
    t_hi16 = rows16(t_hi8)
    need_lo = topk - count16(hi_ref, t_hi16, strict=True)
    t_hi_tile = jnp.concatenate([t_hi16] * (kc // 16), axis=0)

    def lo_body(t, carry):
        for j in pair(t):
            lo = ((keys_ref[chunk(j), :] & 0xFFFF) - 32768).astype(I16)
            lo_ref[chunk(j), :] = jnp.where(hi_ref[chunk(j), :] == t_hi_tile, lo, jnp.int16(-32768))
        return carry

    short_trips(lo_body, jnp.int32(0))
    tau = (t_hi8 * 65536 + kth_bits16(lo_ref, need_lo))[0:1, :]
    cnt_ge = count(lambda kj, j: kj >= tau)
    cnt_gt = count(lambda kj, j: kj > tau)
    need = kf - cnt_gt
    excess = jnp.logical_and(cnt_ge > kf, tau != INT_MIN)
    p_default = jnp.where(tau == INT_MIN, -1, 2 ** 30).astype(I32)
    pstar_ref[...] = jnp.broadcast_to(p_default, pstar_ref.shape)

    assert lk <= 32768

    @pl.when(jnp.max(jnp.where(excess, 1.0, 0.0)) > 0.0)
    def _():
        def rev_body(t, carry):
            for j in pair(t):
                rev = jnp.where(keys_ref[chunk(j), :] == tau, 32767 - (sub_idx + j * kc), -32768)
                lo_ref[chunk(j), :] = rev.astype(I16)
            return carry

        short_trips(rev_body, jnp.int32(0))
        need8 = jnp.broadcast_to(need.astype(I32), (8, qb))
        p = (65535 - kth_bits16(lo_ref, need8))[0:1, :]
        pstar_ref[...] = jnp.broadcast_to(jnp.where(excess, p, p_default), pstar_ref.shape)

    pstar = pstar_ref[0:1, :]

    def bias_body(t, carry):
        for j in pair(t):
            kj = keys_ref[chunk(j), :]
            sel = jnp.logical_or(kj > tau, jnp.logical_and(kj == tau, (sub_idx + j * kc) <= pstar))
            bias_ref[chunk(j), :] = _bf(jnp.where(sel, 0.0, NEG_BIG))
        return carry

    short_trips(bias_body, jnp.int32(0))

    q = q_ref[0]
    group = SA_HEADS // SA_KV_HEADS
    lane_q = lax.broadcasted_iota(I32, (qb, LANES), 1)
    q_heads, bounds = [], []
    for h in range(SA_HEADS):
        qh = _bf(head_slab(q[:, (h // 2) * LANES:(h // 2 + 1) * LANES], h % 2 == 1))
        qf = qh.astype(F32)
        qn2 = _dot(_bf(qf * qf), ones_sq)
        q_heads.append(qh)
        bounds.append(jnp.sqrt(qn2 * kmax_ref[h // group, 0:1, :]) * BOUND_SLACK)
    worst = bounds[0]
    for b in bounds[1:]:
        worst = jnp.maximum(worst, b)
    safe = jnp.max(worst) <= SAFE_BOUND

    zero_acc = jnp.zeros((HD + ONES_ROWS, qb), F32)

    def vt_chunk(c, j):
        return vt_ref[c, :, pl.ds(pl.multiple_of(j * kc, kc), kc)]

    @pl.when(safe)
    def _():
        hs = range(SA_HEADS)
        qa = [jnp.where(lane_q == HD, _bf(-bounds[h]), q_heads[h]) for h in hs]

        def body(t, accs):
            js = pair(t)
            logit = [[_dot_nt(kb_ref[h // group, chunk(j), :], qa[h]) for h in hs] for j in js]
            bias = [bias_ref[chunk(j), :].astype(F32) for j in js]
            p = [[_bf(jnp.exp2(lg + b)) for lg in row] for b, row in zip(bias, logit)]
            pv = [[_dot(vt_chunk(h // group, j), row[h]) for h in hs] for j, row in zip(js, p)]
            return tuple(accs[h] + pv[0][h] + pv[1][h] for h in hs)

        accs = lax.fori_loop(0, nk, body, (zero_acc,) * SA_HEADS)
        for h in hs:
            y_ref[0, h * HD:(h + 1) * HD, :] = accs[h][:HD] / accs[h][HD:HD + 1]

    @pl.when(jnp.logical_not(safe))
    def _():
        for h in range(SA_HEADS):
            c = h // group

            def logit(j):
                return _dot_nt(kb_ref[c, chunk(j), :], q_heads[h]) + bias_ref[chunk(j), :].astype(F32)

            def max_body(t, m):
                for j in pair(t):
                    m = jnp.maximum(m, jnp.max(logit(j), axis=0, keepdims=True))
                return m

            m = lax.fori_loop(0, nk, max_body, jnp.full((1, qb), NEG_BIG, F32))

            def sum_body(t, acc):
                for j in pair(t):
                    acc = acc + _dot(vt_chunk(c, j), _bf(jnp.exp2(logit(j) - m)))
                return acc

            acc = lax.fori_loop(0, nk, sum_body, zero_acc)
            y_ref[0, h * HD:(h + 1) * HD, :] = acc[:HD] / acc[HD:HD + 1]


def _dsa_call(q3, qi3, kw3, k_all, v_all, ki_all, v_blk, *, lk, past, lk_real, topk, qb, kc):
    B, n_q = q3.shape[:2]
    kern = functools.partial(_dsa_kernel, past=past, lk_real=lk_real, topk=topk, kc=kc)
    return pl.pallas_call(
        kern,
        grid=(B, n_q // qb),
        in_specs=[pl.BlockSpec((1, qb, SA_WIDTH), lambda b, i: (b, i, 0)),
                  pl.BlockSpec((1, qb, IDX_WIDTH), lambda b, i: (b, i, 0)),
                  pl.BlockSpec((1, qb, LANES), lambda b, i: (b, i, 0)),
                  pl.BlockSpec((1, lk, LANES), lambda b, i: (b, 0, 0), pipeline_mode=pl.Buffered(1)),
                  pl.BlockSpec((1, lk, LANES), lambda b, i: (b, 0, v_blk), pipeline_mode=pl.Buffered(1)),
                  pl.BlockSpec((1, lk, LANES), lambda b, i: (b, 0, 0), pipeline_mode=pl.Buffered(1))],
        out_specs=pl.BlockSpec((1, SA_WIDTH, qb), lambda b, i: (b, 0, i)),
        out_shape=jax.ShapeDtypeStruct((B, SA_WIDTH, n_q), F32),
        scratch_shapes=[pltpu.VMEM((SA_KV_HEADS, lk, LANES), BF16),
                        pltpu.VMEM((SA_KV_HEADS, SA_HEAD_DIM + ONES_ROWS, lk), BF16),
                        pltpu.VMEM((lk, LANES), BF16),
                        pltpu.VMEM((SA_KV_HEADS, 8, LANES), F32),
                        pltpu.VMEM((lk, qb), I32), pltpu.VMEM((lk, qb), I16), pltpu.VMEM((lk, qb), I16),
                        pltpu.VMEM((lk, qb), BF16),
                        pltpu.VMEM((8, qb), I32), pltpu.VMEM((16, qb), F32)],
        compiler_params=_params(("parallel", "arbitrary")),
        name="dsa",
    )(q3, qi3, kw3, k_all, v_all, ki_all)


def _round_up(x, m):
    return (x + m - 1) // m * m


def _dsa(P, B, L, past, q_rot, qi_rot, k_rot, kw_rot, k_past, v_past, ik_past):
    lk_real = past + L
    topk = min(TOPK_MAX, lk_real // 4)
    qb = 2 * LANES if (past == 0 and L % (2 * LANES) == 0) else LANES
    lq = _round_up(L, qb)
    qpad = lambda t: t if lq == L else jnp.pad(t, ((0, 0), (0, lq - L), (0, 0)))
    q3 = qpad(q_rot.reshape(B, L, SA_WIDTH))
    qi3 = qpad(qi_rot.reshape(B, L, IDX_WIDTH))
    kw3 = kw_rot.reshape(B, L, LANES)
    k3 = k_rot.reshape(B, L, LANES)
    common = dict(past=past, lk_real=lk_real, topk=topk, qb=qb)
    kc = TILE_ELEMS // qb // 2 if qb == LANES else TILE_ELEMS // qb
    if past == 0:
        while L % (2 * kc):
            kc //= 2
        P3 = P.reshape(B, L, P_COLS)
        return _dsa_call(q3, qi3, kw3, k3, P3, kw3, P_SAV // LANES, lk=L, kc=kc, **common)
    lk = _round_up(lk_real, 2 * kc)
    zpad = jnp.zeros((B, lk - lk_real, LANES), F32)
    v_new = P.reshape(B, L, P_COLS)[:, :, P_SAV:P_SAV + LANES].astype(F32)
    ik_p = jnp.concatenate([ik_past.astype(F32), jnp.zeros((B, past, LANES - IDX_DIM), F32)], axis=2)
    k_all = jnp.concatenate([k_past.reshape(B, past, LANES).astype(F32), k3, zpad], axis=1)
    v_all = jnp.concatenate([v_past.reshape(B, past, LANES).astype(F32), v_new, zpad], axis=1)
    ki_all = jnp.concatenate([ik_p, kw3, zpad], axis=1)
    y = _dsa_call(q3, qi3, qpad(kw3), k_all, v_all, ki_all, 0, lk=lk, kc=kc, **common)
    return y[:, :, :L]


def _ret_kernel(q_ref, k_ref, v_ref, g_ref, cos_ref, sin_ref, dm_ref, qd_ref, kd_ref, gc_ref, s0_ref,
                y_ref, so_ref, s_ref):
    c = pl.program_id(1)
    D = RET_HEAD_DIM
    cs = dm_ref.shape[1]
    nch = q_ref.shape[0] // cs
    heads = range(RET_HEADS)
    hsl = [slice(h * D, (h + 1) * D) for h in heads]
    rows = [slice(ch * cs, (ch + 1) * cs) for ch in range(nch)]
    prob = [(ch, h) for ch in range(nch) for h in heads]

    @pl.when(c == 0)
    def _():
        s_ref[...] = s0_ref[0]

    cos, sin = cos_ref[...], sin_ref[...]
    qb_, kb_, kd_, vb_ = [], [], [], []
    for h in heads:
        q = q_ref[:, hsl[h]].astype(F32)
        k = k_ref[:, hsl[h]].astype(F32)
        k = (k * cos + pltpu.roll(k, D // 2, 1) * sin) * (D ** -0.5)
        qb_.append(_bf(q * cos + pltpu.roll(q, D // 2, 1) * sin))
        kb_.append(_bf(k))
        kd_.append([_bf(k[r] * kd_ref[:, hsl[h]]) for r in rows])
        vb_.append(_bf(v_ref[:, hsl[h]]))
    scores = {(ch, h): _dot_nt(qb_[h][rows[ch]], kb_[h][rows[ch]]) * dm_ref[h] for ch, h in prob}
    ktv = {(ch, h): _dot_tn(kd_[h][ch], vb_[h][rows[ch]]) for ch, h in prob}
    intra = {(ch, h): _dot(_bf(scores[ch, h]), vb_[h][rows[ch]]) for ch, h in prob}
    s_at = {}
    for h in heads:
        s = s_ref[h]
        for ch in range(nch):
            s_at[ch, h] = _bf(s)
            s = s * gc_ref[:, hsl[h]] + ktv[ch, h]
        s_ref[h] = s
    cross = {(ch, h): _dot(qb_[h][rows[ch]], s_at[ch, h]) for ch, h in prob}
    for ch in range(nch):
        outs = []
        for h in heads:
            o = intra[ch, h] + cross[ch, h] * qd_ref[:, hsl[h]]
            o = o * lax.rsqrt(jnp.mean(o * o, axis=-1, keepdims=True) + EPS)
            outs.append(jax.nn.silu(g_ref[rows[ch], hsl[h]].astype(F32)) * o)
        y_ref[rows[ch], :] = jnp.concatenate(outs, axis=1)

    @pl.when(c == pl.num_programs(1) - 1)
    def _():
        so_ref[0] = s_ref[...]


def _retention(P, B, L, pos, s0):
    T = P.shape[0]
    c = min(CHUNK, L)
    nc = L // c
    D = RET_HEAD_DIM
    freqs = 1.0 / (RET_ROPE_BASE ** jnp.linspace(0.0, 1.0, D // 2, dtype=F32))
    ang = pos.astype(F32)[:, None] * freqs[None, :]
    cos = jnp.concatenate([jnp.cos(ang)] * 2, axis=1)
    sin = jnp.concatenate([-jnp.sin(ang), jnp.sin(ang)], axis=1)
    log_gamma = jnp.log1p(-jnp.exp2(-5.0 - jnp.arange(RET_HEADS, dtype=F32)))
    idx = jnp.arange(c, dtype=F32)
    dmask = jnp.exp(jnp.abs(idx[:, None] - idx[None, :])[None] * log_gamma[:, None, None])
    lanes = lambda t: jnp.repeat(t, D, axis=1)
    qdec = lanes(jnp.exp((idx[:, None] + 1.0) * log_gamma[None, :]))
    kdec = lanes(jnp.exp((c - 1.0 - idx)[:, None] * log_gamma[None, :]))
    gchunk = lanes(jnp.exp(c * log_gamma)[None, :])

    W = RET_WIDTH
    tt = min(4 * c, L)
    nt = L // tt
    tok = lambda blk: pl.BlockSpec((tt, W), lambda b, i: (b * nt + i, blk))
    full = lambda shape: pl.BlockSpec(shape, lambda b, i: (0,) * len(shape))
    st_spec = pl.BlockSpec((1, RET_HEADS, D, D), lambda b, i: (b, 0, 0, 0))
    y, s_out = pl.pallas_call(
        _ret_kernel,
        grid=(B, nt),
        in_specs=[tok(P_RET // W), tok(P_RET // W + 1), tok(P_RET // W + 2), tok(P_RET // W + 3),
                  pl.BlockSpec((tt, D), lambda b, i: (i, 0)), pl.BlockSpec((tt, D), lambda b, i: (i, 0)),
                  full((RET_HEADS, c, c)), full((c, W)), full((c, W)), full((1, W)), st_spec],
        out_specs=[pl.BlockSpec((tt, W), lambda b, i: (b * nt + i, 0)), st_spec],
        out_shape=[jax.ShapeDtypeStruct((T, W), F32), jax.ShapeDtypeStruct((B, RET_HEADS, D, D), F32)],
        scratch_shapes=[pltpu.VMEM((RET_HEADS, D, D), F32)],
        compiler_params=_params(("parallel", "arbitrary")),
        name="retention",
    )(P, P, P, P, cos, sin, dmask, qdec, kdec, gchunk, s0.astype(F32))
    return y, s_out


def _merge_kernel(x_ref, g0_ref, g1_ref, g2_ref, yr_ref, ys_ref, yt_ref, wr_ref, ws_ref, wt_ref, wo_ref,
                  o_ref):
    gate = lambda g_ref: jax.nn.sigmoid(g_ref[...].astype(F32))
    m = (gate(g0_ref) * _dot(_bf(yr_ref[...]), wr_ref[...])
         + gate(g1_ref) * _dot_tn(_bf(ys_ref[0]), ws_ref[...])
         + gate(g2_ref) * _dot(_bf(yt_ref[...]), wt_ref[...]))
    o_ref[...] = x_ref[...] + _dot(_bf(m), wo_ref[...])


def _merge(x2d, P, y_rw, y_sa_t, y_ret, w_rw, w_sa, w_ret, w_o):
    T = x2d.shape[0]
    L = y_sa_t.shape[2]
    tm = min(512, L)
    nl = L // tm
    D = D_MODEL
    tok = lambda w, blk: pl.BlockSpec((tm, w), lambda i: (i, blk))
    full = lambda shape: pl.BlockSpec(shape, lambda i: (0, 0))
    return pl.pallas_call(
        _merge_kernel,
        grid=(T // tm,),
        in_specs=[tok(D, 0), tok(D, 0), tok(D, 1), tok(D, 2), tok(RW_WIDTH, 0),
                  pl.BlockSpec((1, SA_WIDTH, tm), lambda i: (i // nl, 0, i % nl)),
                  tok(RET_WIDTH, 0), full((RW_WIDTH, D)), full((SA_WIDTH, D)), full((RET_WIDTH, D)),
                  full((D, D))],
        out_specs=tok(D, 0),
        out_shape=jax.ShapeDtypeStruct((T, D), F32),
        compiler_params=_params(("parallel",)),
        name="merge",
    )(x2d, P, P, P, y_rw, y_sa_t, y_ret, w_rw, w_sa, w_ret, w_o)


def _mlp_kernel(x_ref, g_ref, wu_ref, wd_ref, gf_ref, o_ref, h_ref, acc_ref, *, final_norm):
    j = pl.program_id(1)

    @pl.when(j == 0)
    def _():
        x = x_ref[...]
        ms = jnp.mean(x * x, axis=-1, keepdims=True)
        h_ref[...] = _bf(x * lax.rsqrt(ms + EPS) * g_ref[...])
        acc_ref[...] = jnp.zeros_like(acc_ref)

    u = jnp.maximum(_dot(h_ref[...], wu_ref[...]), 0.0)
    acc_ref[...] += _dot(_bf(u * u), wd_ref[...])

    @pl.when(j == pl.num_programs(1) - 1)
    def _():
        xn = x_ref[...] + acc_ref[...]
        if final_norm:
            ms = jnp.mean(xn * xn, axis=-1, keepdims=True)
            xn = xn * lax.rsqrt(ms + EPS) * gf_ref[...]
        o_ref[...] = xn


def _mlp(x2d, g, w_up, w_down, g_final, final_norm):
    T = x2d.shape[0]
    tm = min(1024, T)
    tf = 1024
    D = D_MODEL
    return pl.pallas_call(
        functools.partial(_mlp_kernel, final_norm=final_norm),
        grid=(T // tm, D_FF // tf),
        in_specs=[pl.BlockSpec((tm, D), lambda i, j: (i, 0)),
                  pl.BlockSpec((1, D), lambda i, j: (0, 0)),
                  pl.BlockSpec((D, tf), lambda i, j: (0, j)),
                  pl.BlockSpec((tf, D), lambda i, j: (j, 0)),
                  pl.BlockSpec((1, D), lambda i, j: (0, 0))],
        out_specs=pl.BlockSpec((tm, D), lambda i, j: (i, 0)),
        out_shape=jax.ShapeDtypeStruct((T, D), F32),
        scratch_shapes=[pltpu.VMEM((tm, D), BF16), pltpu.VMEM((tm, D), F32)],
        compiler_params=_params(("parallel", "arbitrary")),
        name="mlp",
    )(x2d, g, w_up, w_down, g_final)


def _layer(x2d, B, L, past, caches, lp, wts, g_final, final_norm):
    k_past, v_past, ik_past, s_rw, shift_rw, s_ret = caches
    pos = past + jnp.arange(L, dtype=jnp.int32)
    row = lambda t: t.reshape(1, -1).astype(F32)
    P = _in_proj(x2d, row(lp['norm1_g']), wts['w_in'], wts['layer'])
    q_rot, qi_rot, k_rot, kw_rot = _sa_prep(P, L, _sa_tables(pos))
    y_rw, s_rw_new = _rwkv(P, B, L, shift_rw, s_rw, lp)
    y_sa = _dsa(P, B, L, past, q_rot, qi_rot, k_rot, kw_rot, k_past, v_past, ik_past)
    y_ret, s_ret_new = _retention(P, B, L, pos, s_ret)
    x2d = _merge(x2d, P, y_rw, y_sa, y_ret, wts['w_br_rwkv'], wts['w_br_dsa'], wts['w_br_ret'], wts['w_o'])
    x2d = _mlp(x2d, row(lp['norm2_g']), wts['w_up'], wts['w_down'], g_final, final_norm)
    P3 = P.reshape(B, L, P_COLS)
    last = P3[:, L - 1].astype(F32)
    shift_new = jnp.concatenate([last[:, P_RKV:P_RKV + 3 * RW_WIDTH], last[:, P_LORA:P_LORA + RW_LORA]], axis=1)
    k_new = k_rot.reshape(B, L, SA_KV_HEADS, SA_HEAD_DIM)
    v_new = P3[:, :, P_SAV:P_SAV + SA_KV_WIDTH].astype(F32).reshape(B, L, SA_KV_HEADS, SA_HEAD_DIM)
    ik_new = kw_rot.reshape(B, L, LANES)[:, :, :IDX_DIM]
    return x2d, (k_new, v_new, ik_new, s_rw_new, shift_new, s_ret_new)


def kernel(x_prompt, x_sample, cache_dsa_k, cache_dsa_v, cache_dsa_ik, state_rwkv, state_rwkv_shift, state_ret, norm1_g, w_in, rwkv_mu, rwkv_w0, rwkv_w2, rwkv_a0, rwkv_a2, rwkv_g2, rwkv_k_k, rwkv_k_a, rwkv_r_k, rwkv_lnx_g, w_br_rwkv, w_br_dsa, w_br_ret, w_o, norm2_g, w_up, w_down, final_norm_g):
    params = {
        'norm1_g': norm1_g, 'rwkv_mu': rwkv_mu, 'rwkv_w0': rwkv_w0, 'rwkv_w2': rwkv_w2,
        'rwkv_a0': rwkv_a0, 'rwkv_a2': rwkv_a2, 'rwkv_g2': rwkv_g2, 'rwkv_k_k': rwkv_k_k,
        'rwkv_k_a': rwkv_k_a, 'rwkv_r_k': rwkv_r_k, 'rwkv_lnx_g': rwkv_lnx_g, 'norm2_g': norm2_g,
    }
    depth = w_in.shape[0]
    Bp, Lp, D = x_prompt.shape
    Bs, Ls, _ = x_sample.shape
    past_s = cache_dsa_k.shape[2]
    xp = x_prompt.reshape(Bp * Lp, D).astype(F32)
    xs = x_sample.reshape(Bs * Ls, D).astype(F32)
    g_final = final_norm_g.reshape(1, D).astype(F32)
    zero_p = (None, None, None,
              jnp.zeros((Bp, RW_HEADS, RW_HEAD_DIM, RW_HEAD_DIM), F32), jnp.zeros((Bp, RW_COLS), F32),
              jnp.zeros((Bp, RET_HEADS, RET_HEAD_DIM, RET_HEAD_DIM), F32))
    p_states = [[] for _ in range(6)]
    s_states = [[] for _ in range(6)]
    w_all = _w_prep(w_in)
    for i in range(depth):
        lp = {name: arr[i] for name, arr in params.items()}
        wts = {'w_in': w_all, 'layer': i, 'w_br_rwkv': _bf(w_br_rwkv[i]), 'w_br_dsa': _bf(w_br_dsa[i]),
               'w_br_ret': _bf(w_br_ret[i]), 'w_o': _bf(w_o[i]), 'w_up': _bf(w_up[i]), 'w_down': _bf(w_down[i])}
        final = i == depth - 1
        cache_s = (cache_dsa_k[i], cache_dsa_v[i], cache_dsa_ik[i], state_rwkv[i], state_rwkv_shift[i],
                   state_ret[i])
        xp, new_p = _layer(xp, Bp, Lp, 0, zero_p, lp, wts, g_final, final)
        xs, new_s = _layer(xs, Bs, Ls, past_s, cache_s, lp, wts, g_final, final)
        for j in range(6):
            p_states[j].append(new_p[j])
            s_states[j].append(new_s[j])
    y_prompt = xp.reshape(Bp, Lp, D)
    y_sample = xs.reshape(Bs, Ls, D)
    p_out = [jnp.stack(t, axis=0) for t in p_states]
    s_out = [jnp.stack(t, axis=0) for t in s_states]
    return (y_prompt, y_sample, *p_out, *s_out)
```

```python
import functools

import numpy as np
import jax
import jax.numpy as jnp
from jax import lax
from jax.experimental import pallas as pl
from jax.experimental.pallas import tpu as pltpu

F32 = jnp.float32
BF16 = jnp.bfloat16
I32 = jnp.int32
I16 = jnp.int16

D_MODEL = 1024
CHUNK = 64
Q_BLOCK = 128
EPS = 1e-6

RW_HEADS = 8
RW_HEAD_DIM = 64
RW_WIDTH = RW_HEADS * RW_HEAD_DIM
RW_DECAY_LORA = 64
RW_ICLR_LORA = 64
RW_GATE_LORA = 128
RW_LORA = RW_DECAY_LORA + RW_ICLR_LORA + RW_GATE_LORA
RW_COLS = 3 * RW_WIDTH + RW_LORA
RW_GN_EPS = 64e-5

SA_HEADS = 8
SA_KV_HEADS = 2
SA_HEAD_DIM = 64
SA_WIDTH = SA_HEADS * SA_HEAD_DIM
SA_KV_WIDTH = SA_KV_HEADS * SA_HEAD_DIM
IDX_HEADS = 4
IDX_DIM = 64
IDX_WIDTH = IDX_HEADS * IDX_DIM
TOPK_MAX = 256
ROPE_THETA = 500000.0
ROPE_DIM = SA_HEAD_DIM // 4
INDEX_SCALE = (IDX_DIM ** -0.5) * (IDX_HEADS ** -0.5)
SA_COLS = SA_WIDTH + 2 * SA_KV_WIDTH + IDX_WIDTH + IDX_DIM + IDX_HEADS

RET_HEADS = 4
RET_HEAD_DIM = 128
RET_WIDTH = RET_HEADS * RET_HEAD_DIM
RET_ROPE_BASE = 10000.0
RET_COLS = 4 * RET_WIDTH

N_BRANCH = 3
GATE_COLS = N_BRANCH * D_MODEL
IN_COLS = RW_COLS + SA_COLS + RET_COLS + GATE_COLS
D_FF = 4 * D_MODEL

LANES = 128
TILE_ELEMS = 64 * 8 * LANES

P_GATE = 0
P_RET = P_GATE + GATE_COLS
P_RKV = P_RET + RET_COLS
P_SAQ = P_RKV + 3 * RW_WIDTH
P_LORA = P_SAQ + SA_WIDTH
P_QI = P_LORA + RW_LORA
P_SAK = P_QI + IDX_WIDTH
P_SAV = P_SAK + SA_KV_WIDTH
P_KIWI = P_SAV + SA_KV_WIDTH
P_COLS = 8192
INT_MIN = -2 ** 31
NEG_BIG = -1e30
LOG2_E = 1.4426950408889634
VMEM_LIMIT = 56 * 1024 * 1024


def _bf(x):
    return x.astype(BF16)


def _dot(a, b):
    return jnp.dot(a, b, preferred_element_type=F32)


def _dot_nt(a, b):
    return lax.dot_general(a, b, (((1,), (1,)), ((), ())), preferred_element_type=F32)


def _dot_tn(a, b):
    return lax.dot_general(a, b, (((0,), (0,)), ((), ())), preferred_element_type=F32)


def _dot_split2(a_exact, x):
    hi = _bf(x)
    lo = _bf(x - hi.astype(F32))
    return _dot(a_exact, hi) + _dot(a_exact, lo)


def _params(sem):
    return pltpu.CompilerParams(dimension_semantics=sem, vmem_limit_bytes=VMEM_LIMIT)


def _in_proj_kernel(x_ref, g_ref, w_ref, o_ref, h_ref):
    @pl.when(pl.program_id(1) == 0)
    def _():
        x = x_ref[...]
        ms = jnp.mean(x * x, axis=-1, keepdims=True)
        h_ref[...] = _bf(x * lax.rsqrt(ms + EPS) * g_ref[...])

    o_ref[...] = _dot(h_ref[...], w_ref[0]).astype(o_ref.dtype)


def _in_proj(x2d, g, w_all, layer):
    T = x2d.shape[0]
    tm = min(1024, T)
    tn = 1024
    return pl.pallas_call(
        _in_proj_kernel,
        grid=(T // tm, P_COLS // tn),
        in_specs=[pl.BlockSpec((tm, D_MODEL), lambda i, j: (i, 0)),
                  pl.BlockSpec((1, D_MODEL), lambda i, j: (0, 0)),
                  pl.BlockSpec((1, D_MODEL, tn), lambda i, j: (layer, 0, j))],
        out_specs=pl.BlockSpec((tm, tn), lambda i, j: (i, j)),
        out_shape=jax.ShapeDtypeStruct((T, P_COLS), BF16),
        scratch_shapes=[pltpu.VMEM((tm, D_MODEL), BF16)],
        compiler_params=_params(("parallel", "arbitrary")),
        name="in_proj",
    )(x2d, g, w_all)


KIWI_COLS = IDX_DIM + IDX_HEADS
TAIL0 = RW_COLS + SA_COLS - KIWI_COLS
TAIL_W = (IN_COLS - TAIL0 + LANES - 1) // LANES * LANES


def _w_prep_kernel(w_ref, o_ref):
    x = w_ref[0]
    lane = lax.broadcasted_iota(I32, (x.shape[0], LANES), 1)

    def put(off, v):
        o_ref[0, :, off:off + v.shape[1]] = _bf(v)

    nslab = TAIL_W // LANES
    part = IN_COLS - TAIL0 - (nslab - 1) * LANES
    slabs = [x[:, TAIL0 + s * LANES:TAIL0 + (s + 1) * LANES] for s in range(nslab - 1)]
    slabs.append(jnp.concatenate([x[:, IN_COLS - part:], jnp.zeros((x.shape[0], LANES - part), F32)], axis=1))
    rolled = [pltpu.roll(t, LANES - KIWI_COLS, 1) for t in slabs]
    for s in range((RET_COLS + GATE_COLS) // LANES):
        v = jnp.where(lane < LANES - KIWI_COLS, rolled[s], rolled[s + 1])
        put((P_RET if s < RET_COLS // LANES else P_GATE - RET_COLS) + s * LANES, v)
    sa = RW_COLS
    put(P_RKV, x[:, 0:3 * RW_WIDTH])
    put(P_LORA, x[:, 3 * RW_WIDTH:RW_COLS])
    put(P_SAQ, x[:, sa:sa + SA_WIDTH])
    put(P_SAK, x[:, sa + SA_WIDTH:sa + SA_WIDTH + SA_KV_WIDTH])
    put(P_SAV, x[:, sa + SA_WIDTH + SA_KV_WIDTH:sa + SA_WIDTH + 2 * SA_KV_WIDTH])
    put(P_QI, x[:, sa + SA_WIDTH + 2 * SA_KV_WIDTH:TAIL0])
    put(P_KIWI, jnp.where(lane < KIWI_COLS, x[:, TAIL0:TAIL0 + LANES], 0.0))
    put(P_KIWI + LANES, jnp.zeros((x.shape[0], P_COLS - P_KIWI - LANES), F32))


def _w_prep(w_in):
    depth, d, _ = w_in.shape
    tm = 256
    return pl.pallas_call(
        _w_prep_kernel,
        grid=(depth, d // tm),
        in_specs=[pl.BlockSpec((1, tm, IN_COLS), lambda l, i: (l, i, 0))],
        out_specs=pl.BlockSpec((1, tm, P_COLS), lambda l, i: (l, i, 0)),
        out_shape=jax.ShapeDtypeStruct((depth, d, P_COLS), BF16),
        compiler_params=_params(("parallel", "parallel")),
        name="w_prep",
    )(w_in)


def _rot_slab(x, c, s_lo, s_hi, shift):
    return x * c + pltpu.roll(x, LANES - shift, 1) * s_lo + pltpu.roll(x, shift, 1) * s_hi


def _sa_prep_kernel(q_ref, qi_ref, k_ref, kw_ref, c_ref, s1_ref, s2_ref, ck_ref, s1k_ref, s2k_ref,
                    qo_ref, qio_ref, ko_ref, kwo_ref):
    c, s1, s2 = c_ref[...], s1_ref[...], s2_ref[...]
    half = ROPE_DIM // 2
    scale = SA_HEAD_DIM ** -0.5 * LOG2_E
    for s in range(SA_WIDTH // LANES):
        sl = slice(s * LANES, (s + 1) * LANES)
        qo_ref[:, sl] = _rot_slab(q_ref[:, sl].astype(F32), c, s1, s2, half) * scale
    for s in range(IDX_WIDTH // LANES):
        sl = slice(s * LANES, (s + 1) * LANES)
        qio_ref[:, sl] = _rot_slab(qi_ref[:, sl].astype(F32), c, s1, s2, half)
    ko_ref[...] = _rot_slab(k_ref[...].astype(F32), c, s1, s2, half)
    kwo_ref[...] = _rot_slab(kw_ref[...].astype(F32), ck_ref[...], s1k_ref[...], s2k_ref[...], half)


def _sa_prep(P, L, tabs):
    T = P.shape[0]
    tm = min(512, L)
    nl = L // tm
    tab_spec = pl.BlockSpec((tm, LANES), lambda i: (i % nl, 0))
    return pl.pallas_call(
        _sa_prep_kernel,
        grid=(T // tm,),
        in_specs=[pl.BlockSpec((tm, SA_WIDTH), lambda i: (i, P_SAQ // SA_WIDTH)),
                  pl.BlockSpec((tm, IDX_WIDTH), lambda i: (i, P_QI // IDX_WIDTH)),
                  pl.BlockSpec((tm, LANES), lambda i: (i, P_SAK // LANES)),
                  pl.BlockSpec((tm, LANES), lambda i: (i, P_KIWI // LANES))] + [tab_spec] * 6,
        out_specs=[pl.BlockSpec((tm, SA_WIDTH), lambda i: (i, 0)),
                   pl.BlockSpec((tm, IDX_WIDTH), lambda i: (i, 0)),
                   pl.BlockSpec((tm, LANES), lambda i: (i, 0)),
                   pl.BlockSpec((tm, LANES), lambda i: (i, 0))],
        out_shape=[jax.ShapeDtypeStruct((T, SA_WIDTH), F32),
                   jax.ShapeDtypeStruct((T, IDX_WIDTH), F32),
                   jax.ShapeDtypeStruct((T, LANES), F32),
                   jax.ShapeDtypeStruct((T, LANES), F32)],
        compiler_params=_params(("parallel",)),
        name="sa_prep",
    )(P, P, P, P, *tabs)


def _sa_tables(pos):
    half = ROPE_DIM // 2
    freqs = 1.0 / (ROPE_THETA ** (jnp.arange(0, ROPE_DIM, 2, dtype=F32) / ROPE_DIM))
    ang = pos.astype(F32)[:, None] * freqs[None, :]
    cos, sin = jnp.cos(ang), jnp.sin(ang)
    n = pos.shape[0]
    pad = SA_HEAD_DIM - ROPE_DIM
    c_head = jnp.concatenate([cos, cos, jnp.ones((n, pad), F32)], axis=1)
    s1_head = jnp.concatenate([-sin, jnp.zeros((n, half + pad), F32)], axis=1)
    s2_head = jnp.concatenate([jnp.zeros((n, half), F32), sin, jnp.zeros((n, pad), F32)], axis=1)
    one, zero = jnp.ones((n, SA_HEAD_DIM), F32), jnp.zeros((n, SA_HEAD_DIM), F32)
    two = lambda t: jnp.concatenate([t, t], axis=1)
    return (two(c_head), two(s1_head), two(s2_head),
            jnp.concatenate([c_head, one], axis=1), jnp.concatenate([s1_head, zero], axis=1),
            jnp.concatenate([s2_head, zero], axis=1))


def _softplus(u):
    return jnp.maximum(u, 0.0) + jnp.log(1.0 + jnp.exp(-jnp.abs(u)))


def _rwkv_kernel(r_ref, k_ref, v_ref, lo_ref, shr_ref, shk_ref, shv_ref, shlo_ref, s0_ref,
                 mur_ref, muk_ref, muv_ref, mulo_ref, w0_ref, w2_ref, a0_ref, a2_ref, g2_ref,
                 kk_ref, ka_ref, rk_ref, lnx_ref, e_ref, tin_ref, tsuf_ref,
                 y_ref, so_ref,
                 cr_ref, ck_ref, cv_ref, clo_ref, s_ref,
                 kap_ref, rt_ref, bh_ref, kh_ref, bt_ref, kt_ref, vv_ref, gc_ref, yy_ref, rr_ref,
                 ac_ref, cc_ref):
    c = pl.program_id(1)
    TT = r_ref.shape[0]
    nch = TT // CHUNK
    N = RW_HEAD_DIM

    @pl.when(c == 0)
    def _():
        cr_ref[0:1, :] = shr_ref[0]
        ck_ref[0:1, :] = shk_ref[0]
        cv_ref[0:1, :] = shv_ref[0]
        clo_ref[0:1, :] = shlo_ref[0]
        s_ref[...] = s0_ref[0]

    def lerp(p_ref, carry_ref, mu_ref):
        p = p_ref[...].astype(F32)
        rolled = pltpu.roll(p, 1, 0)
        row0 = lax.broadcasted_iota(I32, p.shape, 0) == 0
        prev = jnp.where(row0, carry_ref[0:1, :], rolled)
        carry_ref[0:1, :] = p[TT - 1:TT, :]
        return p + (prev - p) * mu_ref[...]

    xr = lerp(r_ref, cr_ref, mur_ref)
    xk = lerp(k_ref, ck_ref, muk_ref)
    xv = lerp(v_ref, cv_ref, muv_ref)
    xlo = lerp(lo_ref, clo_ref, mulo_ref)
    xw = xlo[:, :RW_DECAY_LORA]
    xa = xlo[:, RW_DECAY_LORA:RW_DECAY_LORA + RW_ICLR_LORA]
    xg = xlo[:, RW_DECAY_LORA + RW_ICLR_LORA:]

    z = w0_ref[...] + _dot(_bf(jnp.tanh(xw)), w2_ref[...])
    w = -_softplus(-z) - 0.5
    ld = -jnp.exp(w)
    a = jax.nn.sigmoid(a0_ref[...] + _dot(_bf(xa), a2_ref[...]))
    gate = _dot(_bf(jax.nn.sigmoid(xg)), g2_ref[...])
    e_blk = e_ref[...]
    kk = xk * kk_ref[...]
    kk = kk / jnp.maximum(jnp.sqrt(_dot(_bf(kk * kk), e_blk)), 1e-12)
    k2 = xk * (1.0 + (a - 1.0) * ka_ref[...])
    bb = kk * a
    bonus = _dot(_bf(xr * k2 * rk_ref[...]), e_blk) * xv

    lin = _dot_split2(tin_ref[...], ld)
    lsuf = _dot_split2(tsuf_ref[...], ld)
    e_in = jnp.exp(lin)
    e_ninv = jnp.exp(-lin)
    e_suf = jnp.exp(lsuf)
    kap_ref[...] = kk * jnp.exp(lin - ld)
    rt_ref[...] = xr * e_in
    bh_ref[...] = bb * e_ninv
    kh_ref[...] = k2 * e_ninv
    bt_ref[...] = bb * e_suf
    kt_ref[...] = k2 * e_suf
    vv_ref[...] = xv
    gc_ref[...] = jnp.exp(lin + lsuf)

    ri = lax.broadcasted_iota(I32, (CHUNK, CHUNK), 0)
    ci = lax.broadcasted_iota(I32, (CHUNK, CHUNK), 1)
    strict = ri > ci
    incl = ri >= ci
    eye = (ri == ci).astype(F32)
    heads = range(RW_HEADS)
    hsl = [slice(h * N, (h + 1) * N) for h in heads]

    per_it = next(n for n in (4, 2, 1) if nch % n == 0)

    def coef_body(it, carry):
        chs = [it * per_it + t for t in range(per_it)]
        pairs = [(t, h) for t in range(per_it) for h in heads]
        rows = [pl.ds(pl.multiple_of(ch * CHUNK, CHUNK), CHUNK) for ch in chs]
        ld = lambda ref, p: ref[rows[p[0]], hsl[p[1]]]
        kap = [ld(kap_ref, p) for p in pairs]
        rt = [ld(rt_ref, p) for p in pairs]
        vh = [ld(vv_ref, p) for p in pairs]
        idx = range(len(pairs))
        gmat = [_dot_nt(_bf(jnp.concatenate([kap[n], rt[n]], axis=0)),
                        _bf(jnp.concatenate([ld(bh_ref, pairs[n]), ld(kh_ref, pairs[n])], axis=0)))
                for n in idx]
        n_ab = [jnp.where(strict, g[:CHUNK, :CHUNK], 0.0) for g in gmat]
        m_rb = [jnp.where(incl, g[CHUNK:, :CHUNK], 0.0) for g in gmat]
        m_v = [_bf(jnp.concatenate([jnp.where(strict, g[:CHUNK, CHUNK:], 0.0),
                                    jnp.where(incl, g[CHUNK:, CHUNK:], 0.0)], axis=0)) for g in gmat]
        mv = [_dot(m_v[n], _bf(vh[n])) for n in idx]
        x_inv = [eye - t for t in n_ab]
        pw = n_ab
        for _ in range(5):
            pwb = [_bf(p) for p in pw]
            pw = [_dot(p, p) for p in pwb]
            x_inv = [x + _dot(_bf(x), _bf(p)) for x, p in zip(x_inv, pw)]
        w = [_dot(_bf(x_inv[n]), _bf(jnp.concatenate([kap[n], mv[n][:CHUNK]], axis=1))) for n in idx]
        wb = [_bf(t) for t in w]
        ry = [jnp.concatenate([rt[n], mv[n][CHUNK:]], axis=1) - _dot(_bf(m_rb[n]), wb[n]) for n in idx]
        dmat = [_dot_tn(wb[n], _bf(ld(bt_ref, pairs[n]))) for n in idx]
        vtk = [_dot_tn(_bf(vh[n]), _bf(ld(kt_ref, pairs[n]))) for n in idx]
        for n, (t, h) in enumerate(pairs):
            rr_ref[rows[t], hsl[h]] = ry[n][:, :N]
            yy_ref[rows[t], hsl[h]] = ry[n][:, N:]
            ac_ref[chs[t], h] = -dmat[n][:N]
            cc_ref[chs[t], h] = vtk[n] - dmat[n][N:]
        return carry

    lax.fori_loop(0, nch // per_it, coef_body, 0)

    def state_body(ch, carry):
        rows = pl.ds(pl.multiple_of(ch * CHUNK, CHUNK), CHUNK)
        s_old = [s_ref[h] for h in heads]
        sb = [_bf(t) for t in s_old]
        s_new = [_dot(sb[h], _bf(ac_ref[ch, h])) for h in heads]
        y_c = [_dot_nt(_bf(rr_ref[rows, hsl[h]]), sb[h]) for h in heads]
        for h in heads:
            gch = gc_ref[pl.ds(pl.multiple_of(ch * CHUNK, CHUNK), 1), hsl[h]]
            s_ref[h] = s_old[h] * gch + s_new[h] + cc_ref[ch, h]
            yy_ref[rows, hsl[h]] = yy_ref[rows, hsl[h]] + y_c[h]
        return carry

    lax.fori_loop(0, nch, state_body, 0)

    y = yy_ref[...]
    mean = _dot(_bf(y), e_blk) * (1.0 / N)
    d = y - mean
    var = _dot(_bf(d * d), e_blk) * (1.0 / N)
    yn = d * lax.rsqrt(var + RW_GN_EPS) * lnx_ref[...]
    y_ref[...] = (yn + bonus) * gate

    @pl.when(c == pl.num_programs(1) - 1)
    def _():
        so_ref[0] = s_ref[...]


def _rwkv(P, B, L, shift_prev, s0, lp):
    T = P.shape[0]
    TT = min(256, L)
    nt = L // TT
    W = RW_WIDTH
    row = lambda t: t.reshape(1, -1).astype(F32)
    mu = lp['rwkv_mu']
    sh = shift_prev.astype(F32)
    pieces = lambda t: (t[..., 0:W], t[..., W:2 * W], t[..., 2 * W:3 * W], t[..., 3 * W:])
    mu_r, mu_k, mu_v, mu_lo = [row(t) for t in pieces(mu)]
    sh_r, sh_k, sh_v, sh_lo = [t.reshape(B, 1, -1) for t in pieces(sh)]
    hid = jnp.arange(W) // RW_HEAD_DIM
    e_blk = (hid[:, None] == hid[None, :]).astype(BF16)
    ti = jnp.arange(TT)
    same = (ti[:, None] // CHUNK) == (ti[None, :] // CHUNK)
    tri_in = (same & (ti[None, :] <= ti[:, None])).astype(BF16)
    tri_suf = (same & (ti[None, :] > ti[:, None])).astype(BF16)

    tok = lambda w, blk: pl.BlockSpec((TT, w), lambda b, c: (b * nt + c, blk))
    full = lambda shape: pl.BlockSpec(shape, lambda b, c: (0,) * len(shape))
    shs = lambda w: pl.BlockSpec((1, 1, w), lambda b, c: (b, 0, 0))
    st_spec = pl.BlockSpec((1, RW_HEADS, RW_HEAD_DIM, RW_HEAD_DIM), lambda b, c: (b, 0, 0, 0))
    big = lambda: pltpu.VMEM((TT, W), F32)
    y, s_out = pl.pallas_call(
        _rwkv_kernel,
        grid=(B, nt),
        in_specs=[tok(W, P_RKV // W), tok(W, P_RKV // W + 1), tok(W, P_RKV // W + 2),
                  tok(RW_LORA, P_LORA // RW_LORA),
                  shs(W), shs(W), shs(W), shs(RW_LORA), st_spec,
                  full((1, W)), full((1, W)), full((1, W)), full((1, RW_LORA)),
                  full((1, W)), full((RW_DECAY_LORA, W)), full((1, W)), full((RW_ICLR_LORA, W)),
                  full((RW_GATE_LORA, W)), full((1, W)), full((1, W)), full((1, W)), full((1, W)),
                  full((W, W)), full((TT, TT)), full((TT, TT))],
        out_specs=[pl.BlockSpec((TT, W), lambda b, c: (b * nt + c, 0)), st_spec],
        out_shape=[jax.ShapeDtypeStruct((T, W), F32),
                   jax.ShapeDtypeStruct((B, RW_HEADS, RW_HEAD_DIM, RW_HEAD_DIM), F32)],
        scratch_shapes=[pltpu.VMEM((8, W), F32), pltpu.VMEM((8, W), F32), pltpu.VMEM((8, W), F32),
                        pltpu.VMEM((8, RW_LORA), F32),
                        pltpu.VMEM((RW_HEADS, RW_HEAD_DIM, RW_HEAD_DIM), F32)] + [big() for _ in range(10)]
        + [pltpu.VMEM((TT // CHUNK, RW_HEADS, RW_HEAD_DIM, RW_HEAD_DIM), F32) for _ in range(2)],
        compiler_params=_params(("parallel", "arbitrary")),
        name="rwkv",
    )(P, P, P, P, sh_r, sh_k, sh_v, sh_lo, s0.astype(F32),
      mu_r, mu_k, mu_v, mu_lo, row(lp['rwkv_w0']), _bf(lp['rwkv_w2']), row(lp['rwkv_a0']),
      _bf(lp['rwkv_a2']), _bf(lp['rwkv_g2']), row(lp['rwkv_k_k']), row(lp['rwkv_k_a']),
      row(lp['rwkv_r_k']), row(lp['rwkv_lnx_g']), e_blk, tri_in, tri_suf)
    return y, s_out


ONES_ROWS = 16
BOUND_SLACK = 1.05
SAFE_BOUND = 60.0


def _dsa_kernel(q_ref, qi_ref, kw_ref, k_ref, v_ref, ki_ref, y_ref,
                kb_ref, vt_ref, kib_ref, kmax_ref, keys_ref, hi_ref, lo_ref, bias_ref, pstar_ref, red_ref,
                *, past, lk_real, topk, kc):
    i = pl.program_id(1)
    qb = q_ref.shape[1]
    lk = k_ref.shape[1]
    kf = float(topk)
    HD = SA_HEAD_DIM

    ones_sq = jnp.ones((LANES, LANES), BF16)

    def head_slab(x, odd):
        lane = lax.broadcasted_iota(I32, x.shape, 1)
        return jnp.where(lane < HD, pltpu.roll(x, HD, 1) if odd else x, 0.0)

    @pl.when(i == 0)
    def _():
        kib_ref[...] = _bf(ki_ref[0])
        k = k_ref[0]
        lane = lax.broadcasted_iota(I32, k.shape, 1)
        for c in range(SA_KV_HEADS):
            kc_b = _bf(head_slab(k, c == 1))
            kb_ref[c] = jnp.where(lane == HD, jnp.ones_like(kc_b), kc_b)
            kf32 = kc_b.astype(F32)
            n2 = _dot(_bf(kf32 * kf32), ones_sq)
            kmax_ref[c] = jnp.broadcast_to(jnp.max(n2, axis=0, keepdims=True), (8, LANES))
        vt = v_ref[0].astype(F32).T
        for c in range(SA_KV_HEADS):
            vt_ref[c, 0:HD, :] = _bf(vt[c * HD:(c + 1) * HD, :])
            vt_ref[c, HD:HD + ONES_ROWS, :] = jnp.ones((ONES_ROWS, lk), BF16)

    qpos = past + i * qb + lax.broadcasted_iota(I32, (1, qb), 1)
    limit = jnp.minimum((qpos // CHUNK + 1) * CHUNK, lk_real)
    hi = jnp.minimum(past + (i + 1) * qb, lk_real)
    nk = lax.shift_right_logical(hi + (2 * kc - 1), kc.bit_length())
    sub_idx = lax.broadcasted_iota(I32, (kc, qb), 0)
    kwt = kw_ref[0].T
    wi_h = [kwt[IDX_DIM + h:IDX_DIM + h + 1, :] * INDEX_SCALE for h in range(IDX_HEADS)]
    qi = qi_ref[0]
    qi_h = [_bf(qi[:, h * IDX_DIM:(h + 1) * IDX_DIM]) for h in range(IDX_HEADS)]

    def chunk(j):
        return pl.ds(pl.multiple_of(j * kc, kc), kc)

    def pair(t):
        return (2 * t, 2 * t + 1)

    def score_body(t, carry):
        js = pair(t)
        dots = [[_dot_nt(kib_ref[chunk(j), 0:IDX_DIM], qi_h[h]) for h in range(IDX_HEADS)] for j in js]
        for j, d in zip(js, dots):
            s = jnp.zeros((kc, qb), F32)
            for h in range(IDX_HEADS):
                s = s + jnp.maximum(d[h], 0.0) * wi_h[h]
            s = jnp.where(s == 0.0, 0.0, s)
            bits = lax.bitcast_convert_type(s, I32)
            key = bits ^ ((bits >> 31) & 0x7FFFFFFF)
            key = jnp.where((sub_idx + j * kc) < limit, key, INT_MIN)
            keys_ref[chunk(j), :] = key
            hi_ref[chunk(j), :] = (key >> 16).astype(I16)
        return carry

    lax.fori_loop(0, nk, score_body, 0)

    def fold(m, rows):
        parts = [m[r * rows:(r + 1) * rows] for r in range(kc // rows)]
        while len(parts) > 1:
            parts = [a + b for a, b in zip(parts[0::2], parts[1::2])]
        return parts[0]

    max_trips = lk // (2 * kc)

    def short_trips(body, carry):
        base = jnp.int32(0)
        p = 1 << (max_trips.bit_length() - 1)
        while p:
            take = (nk & p) != 0

            def run(c, base=base, p=p):
                for s in range(p):
                    c = body(base + s, c)
                return c

            carry = lax.cond(take, run, lambda c: c, carry)
            base = base + jnp.where(take, p, 0)
            p //= 2
        return carry

    def col_total(part):
        rows = part.shape[0]
        red_ref[0:rows, :] = part
        parts = [red_ref[r:r + 1, :] for r in range(rows)]
        while len(parts) > 1:
            parts = [a + b for a, b in zip(parts[0::2], parts[1::2])]
        return parts[0]

    def count(pred):
        def body(t, acc):
            for j in pair(t):
                acc = acc + fold(jnp.where(pred(keys_ref[chunk(j), :], j), 1.0, 0.0), 8)
            return acc

        return col_total(short_trips(body, jnp.zeros((8, qb), F32)))

    def rows16(x8):
        return jnp.concatenate([x8, x8], axis=0).astype(I16)

    def slab_count(x, thr16, strict):
        parts = []
        for r in range(kc // 16):
            xs = x[r * 16:(r + 1) * 16]
            hit = xs > thr16 if strict else xs >= thr16
            parts.append(jnp.where(hit, jnp.int16(1), jnp.int16(0)))
        while len(parts) > 1:
            parts = [a + b for a, b in zip(parts[0::2], parts[1::2])]
        return parts[0]

    def sublane_allsum(c):
        parts = [c] + [pltpu.roll(c, sh, 0) for sh in range(1, 8)]
        while len(parts) > 1:
            parts = [a + b for a, b in zip(parts[0::2], parts[1::2])]
        return parts[0]

    def count16(ref, thr16, strict=False):
        def body(t, acc):
            for j in pair(t):
                acc = acc + slab_count(ref[chunk(j), :], thr16, strict)
            return acc

        w = pltpu.bitcast(short_trips(body, jnp.zeros((16, qb), I16)), I32)
        s = (w & 0xFFFF) + lax.shift_right_logical(w, 16)
        tiles = [s[:, n * LANES:(n + 1) * LANES] for n in range(qb // LANES)]
        out = []
        for a, b in zip(tiles[0::2], tiles[1::2]):
            c = sublane_allsum(a + lax.shift_left(b, 16))
            out += [c & 0xFFFF, lax.shift_right_logical(c, 16)]
        if len(tiles) % 2:
            out.append(sublane_allsum(tiles[-1]))
        return jnp.concatenate(out, axis=1) if len(out) > 1 else out[0]

    def kth_bits16(ref, need, known=0, nbits=16):
        def bit_body(it, u):
            cand = u | lax.shift_left(jnp.int32(1), nbits - 1 - it)
            return jnp.where(count16(ref, rows16(cand - 32768)) >= need, cand, u)

        return lax.fori_loop(0, nbits, bit_body, jnp.full((8, qb), known, I32))

    t_hi8 = kth_bits16(hi_ref, topk) - 32768
    t_hi16 = rows16(t_hi8)
    need_lo = topk - count16(hi_ref, t_hi16, strict=True)
    t_hi_tile = jnp.concatenate([t_hi16] * (kc // 16), axis=0)

    def lo_body(t, carry):
        for j in pair(t):
            lo = ((keys_ref[chunk(j), :] & 0xFFFF) - 32768).astype(I16)
            lo_ref[chunk(j), :] = jnp.where(hi_ref[chunk(j), :] == t_hi_tile, lo, jnp.int16(-32768))
        return carry

    short_trips(lo_body, jnp.int32(0))
    tau = (t_hi8 * 65536 + kth_bits16(lo_ref, need_lo))[0:1, :]
    cnt_ge = count(lambda kj, j: kj >= tau)
    cnt_gt = count(lambda kj, j: kj > tau)
    need = kf - cnt_gt
    excess = jnp.logical_and(cnt_ge > kf, tau != INT_MIN)
    p_default = jnp.where(tau == INT_MIN, -1, 2 ** 30).astype(I32)
    pstar_ref[...] = jnp.broadcast_to(p_default, pstar_ref.shape)

    assert lk <= 32768

    @pl.when(jnp.max(jnp.where(excess, 1.0, 0.0)) > 0.0)
    def _():
        def rev_body(t, carry):
            for j in pair(t):
                rev = jnp.where(keys_ref[chunk(j), :] == tau, 32767 - (sub_idx + j * kc), -32768)
                lo_ref[chunk(j), :] = rev.astype(I16)
            return carry

        short_trips(rev_body, jnp.int32(0))
        need8 = jnp.broadcast_to(need.astype(I32), (8, qb))
        nb = max(1, int(lk - 1).bit_length())
        known = (0xFFFF >> nb) << nb
        p = (65535 - kth_bits16(lo_ref, need8, known, nb))[0:1, :]
        pstar_ref[...] = jnp.broadcast_to(jnp.where(excess, p, p_default), pstar_ref.shape)

    pstar = pstar_ref[0:1, :]

    def bias_body(t, carry):
        for j in pair(t):
            kj = keys_ref[chunk(j), :]
            sel = jnp.logical_or(kj > tau, jnp.logical_and(kj == tau, (sub_idx + j * kc) <= pstar))
            bias_ref[chunk(j), :] = _bf(jnp.where(sel, 0.0, NEG_BIG))
        return carry

    short_trips(bias_body, jnp.int32(0))

    q = q_ref[0]
    group = SA_HEADS // SA_KV_HEADS
    lane_q = lax.broadcasted_iota(I32, (qb, LANES), 1)
    q_heads, bounds = [], []
    for h in range(SA_HEADS):
        qh = _bf(head_slab(q[:, (h // 2) * LANES:(h // 2 + 1) * LANES], h % 2 == 1))
        qf = qh.astype(F32)
        qn2 = _dot(_bf(qf * qf), ones_sq)
        q_heads.append(qh)
        bounds.append(jnp.sqrt(qn2 * kmax_ref[h // group, 0:1, :]) * BOUND_SLACK)
    worst = bounds[0]
    for b in bounds[1:]:
        worst = jnp.maximum(worst, b)
    safe = jnp.max(worst) <= SAFE_BOUND

    zero_acc = jnp.zeros((HD + ONES_ROWS, qb), F32)

    def vt_chunk(c, j):
        return vt_ref[c, :, pl.ds(pl.multiple_of(j * kc, kc), kc)]

    @pl.when(safe)
    def _():
        hs = range(SA_HEADS)
        qa = [jnp.where(lane_q == HD, _bf(-bounds[h]), q_heads[h]) for h in hs]

        def body(t, accs):
            js = pair(t)
            logit = [[_dot_nt(kb_ref[h // group, chunk(j), :], qa[h]) for h in hs] for j in js]
            bias = [bias_ref[chunk(j), :].astype(F32) for j in js]
            p = [[_bf(jnp.exp2(lg + b)) for lg in row] for b, row in zip(bias, logit)]
            pv = [[_dot(vt_chunk(h // group, j), row[h]) for h in hs] for j, row in zip(js, p)]
            return tuple(accs[h] + pv[0][h] + pv[1][h] for h in hs)

        accs = lax.fori_loop(0, nk, body, (zero_acc,) * SA_HEADS)
        for h in hs:
            y_ref[0, h * HD:(h + 1) * HD, :] = accs[h][:HD] / accs[h][HD:HD + 1]

    @pl.when(jnp.logical_not(safe))
    def _():
        for h in range(SA_HEADS):
            c = h // group

            def logit(j):
                return _dot_nt(kb_ref[c, chunk(j), :], q_heads[h]) + bias_ref[chunk(j), :].astype(F32)

            def max_body(t, m):
                for j in pair(t):
                    m = jnp.maximum(m, jnp.max(logit(j), axis=0, keepdims=True))
                return m

            m = lax.fori_loop(0, nk, max_body, jnp.full((1, qb), NEG_BIG, F32))

            def sum_body(t, acc):
                for j in pair(t):
                    acc = acc + _dot(vt_chunk(c, j), _bf(jnp.exp2(logit(j) - m)))
                return acc

            acc = lax.fori_loop(0, nk, sum_body, zero_acc)
            y_ref[0, h * HD:(h + 1) * HD, :] = acc[:HD] / acc[HD:HD + 1]


def _dsa_call(q3, qi3, kw3, k_all, v_all, ki_all, v_blk, *, lk, past, lk_real, topk, qb, kc):
    B, n_q = q3.shape[:2]
    kern = functools.partial(_dsa_kernel, past=past, lk_real=lk_real, topk=topk, kc=kc)
    once = dict(pipeline_mode=pl.Buffered(1)) if n_q // qb > 1 else {}
    return pl.pallas_call(
        kern,
        grid=(B, n_q // qb),
        in_specs=[pl.BlockSpec((1, qb, SA_WIDTH), lambda b, i: (b, i, 0)),
                  pl.BlockSpec((1, qb, IDX_WIDTH), lambda b, i: (b, i, 0)),
                  pl.BlockSpec((1, qb, LANES), lambda b, i: (b, i, 0)),
                  pl.BlockSpec((1, lk, LANES), lambda b, i: (b, 0, 0), **once),
                  pl.BlockSpec((1, lk, LANES), lambda b, i: (b, 0, v_blk), **once),
                  pl.BlockSpec((1, lk, LANES), lambda b, i: (b, 0, 0), **once)],
        out_specs=pl.BlockSpec((1, SA_WIDTH, qb), lambda b, i: (b, 0, i)),
        out_shape=jax.ShapeDtypeStruct((B, SA_WIDTH, n_q), F32),
        scratch_shapes=[pltpu.VMEM((SA_KV_HEADS, lk, LANES), BF16),
                        pltpu.VMEM((SA_KV_HEADS, SA_HEAD_DIM + ONES_ROWS, lk), BF16),
                        pltpu.VMEM((lk, LANES), BF16),
                        pltpu.VMEM((SA_KV_HEADS, 8, LANES), F32),
                        pltpu.VMEM((lk, qb), I32), pltpu.VMEM((lk, qb), I16), pltpu.VMEM((lk, qb), I16),
                        pltpu.VMEM((lk, qb), BF16),
                        pltpu.VMEM((8, qb), I32), pltpu.VMEM((16, qb), F32)],
        compiler_params=_params(("parallel", "arbitrary")),
        name="dsa",
    )(q3, qi3, kw3, k_all, v_all, ki_all)


def _round_up(x, m):
    return (x + m - 1) // m * m


def _dsa(P, B, L, past, q_rot, qi_rot, k_rot, kw_rot, k_past, v_past, ik_past):
    lk_real = past + L
    topk = min(TOPK_MAX, lk_real // 4)
    qb = 2 * LANES if (past == 0 and L % (2 * LANES) == 0) else LANES
    lq = _round_up(L, qb)
    qpad = lambda t: t if lq == L else jnp.pad(t, ((0, 0), (0, lq - L), (0, 0)))
    q3 = qpad(q_rot.reshape(B, L, SA_WIDTH))
    qi3 = qpad(qi_rot.reshape(B, L, IDX_WIDTH))
    kw3 = kw_rot.reshape(B, L, LANES)
    k3 = k_rot.reshape(B, L, LANES)
    common = dict(past=past, lk_real=lk_real, topk=topk, qb=qb)
    kc = TILE_ELEMS // qb // 2 if qb == LANES else TILE_ELEMS // qb
    if past == 0:
        while L % (2 * kc):
            kc //= 2
        P3 = P.reshape(B, L, P_COLS)
        return _dsa_call(q3, qi3, kw3, k3, P3, kw3, P_SAV // LANES, lk=L, kc=kc, **common)
    lk = _round_up(lk_real, 2 * kc)
    zpad = jnp.zeros((B, lk - lk_real, LANES), F32)
    v_new = P.reshape(B, L, P_COLS)[:, :, P_SAV:P_SAV + LANES].astype(F32)
    ik_p = jnp.concatenate([ik_past.astype(F32), jnp.zeros((B, past, LANES - IDX_DIM), F32)], axis=2)
    k_all = jnp.concatenate([k_past.reshape(B, past, LANES).astype(F32), k3, zpad], axis=1)
    v_all = jnp.concatenate([v_past.reshape(B, past, LANES).astype(F32), v_new, zpad], axis=1)
    ki_all = jnp.concatenate([ik_p, kw3, zpad], axis=1)
    y = _dsa_call(q3, qi3, qpad(kw3), k_all, v_all, ki_all, 0, lk=lk, kc=kc, **common)
    return y[:, :, :L]


def _ret_kernel(q_ref, k_ref, v_ref, g_ref, cos_ref, sin_ref, dm_ref, qd_ref, kd_ref, gc_ref, s0_ref,
                y_ref, so_ref, s_ref):
    c = pl.program_id(1)
    D = RET_HEAD_DIM
    cs = dm_ref.shape[1]
    nch = q_ref.shape[0] // cs
    heads = range(RET_HEADS)
    hsl = [slice(h * D, (h + 1) * D) for h in heads]
    rows = [slice(ch * cs, (ch + 1) * cs) for ch in range(nch)]
    prob = [(ch, h) for ch in range(nch) for h in heads]

    @pl.when(c == 0)
    def _():
        s_ref[...] = s0_ref[0]

    cos, sin = cos_ref[...], sin_ref[...]
    qb_, kb_, kd_, vb_ = [], [], [], []
    for h in heads:
        q = q_ref[:, hsl[h]].astype(F32)
        k = k_ref[:, hsl[h]].astype(F32)
        k = (k * cos + pltpu.roll(k, D // 2, 1) * sin) * (D ** -0.5)
        qb_.append(_bf(q * cos + pltpu.roll(q, D // 2, 1) * sin))
        kb_.append(_bf(k))
        kd_.append([_bf(k[r] * kd_ref[:, hsl[h]]) for r in rows])
        vb_.append(_bf(v_ref[:, hsl[h]]))
    scores = {(ch, h): _dot_nt(qb_[h][rows[ch]], kb_[h][rows[ch]]) * dm_ref[h] for ch, h in prob}
    ktv = {(ch, h): _dot_tn(kd_[h][ch], vb_[h][rows[ch]]) for ch, h in prob}
    intra = {(ch, h): _dot(_bf(scores[ch, h]), vb_[h][rows[ch]]) for ch, h in prob}
    s_at = {}
    for h in heads:
        s = s_ref[h]
        for ch in range(nch):
            s_at[ch, h] = _bf(s)
            s = s * gc_ref[:, hsl[h]] + ktv[ch, h]
        s_ref[h] = s
    cross = {(ch, h): _dot(qb_[h][rows[ch]], s_at[ch, h]) for ch, h in prob}
    for ch in range(nch):
        outs = []
        for h in heads:
            o = intra[ch, h] + cross[ch, h] * qd_ref[:, hsl[h]]
            o = o * lax.rsqrt(jnp.mean(o * o, axis=-1, keepdims=True) + EPS)
            outs.append(jax.nn.silu(g_ref[rows[ch], hsl[h]].astype(F32)) * o)
        y_ref[rows[ch], :] = jnp.concatenate(outs, axis=1)

    @pl.when(c == pl.num_programs(1) - 1)
    def _():
        so_ref[0] = s_ref[...]


def _retention(P, B, L, pos, s0):
    T = P.shape[0]
    c = min(CHUNK, L)
    nc = L // c
    D = RET_HEAD_DIM
    freqs = 1.0 / (RET_ROPE_BASE ** jnp.linspace(0.0, 1.0, D // 2, dtype=F32))
    ang = pos.astype(F32)[:, None] * freqs[None, :]
    cos = jnp.concatenate([jnp.cos(ang)] * 2, axis=1)
    sin = jnp.concatenate([-jnp.sin(ang), jnp.sin(ang)], axis=1)
    log_gamma = jnp.log1p(-jnp.exp2(-5.0 - jnp.arange(RET_HEADS, dtype=F32)))
    idx = jnp.arange(c, dtype=F32)
    dmask = jnp.exp(jnp.abs(idx[:, None] - idx[None, :])[None] * log_gamma[:, None, None])
    lanes = lambda t: jnp.repeat(t, D, axis=1)
    qdec = lanes(jnp.exp((idx[:, None] + 1.0) * log_gamma[None, :]))
    kdec = lanes(jnp.exp((c - 1.0 - idx)[:, None] * log_gamma[None, :]))
    gchunk = lanes(jnp.exp(c * log_gamma)[None, :])

    W = RET_WIDTH
    tt = min(4 * c, L)
    nt = L // tt
    tok = lambda blk: pl.BlockSpec((tt, W), lambda b, i: (b * nt + i, blk))
    full = lambda shape: pl.BlockSpec(shape, lambda b, i: (0,) * len(shape))
    st_spec = pl.BlockSpec((1, RET_HEADS, D, D), lambda b, i: (b, 0, 0, 0))
    y, s_out = pl.pallas_call(
        _ret_kernel,
        grid=(B, nt),
        in_specs=[tok(P_RET // W), tok(P_RET // W + 1), tok(P_RET // W + 2), tok(P_RET // W + 3),
                  pl.BlockSpec((tt, D), lambda b, i: (i, 0)), pl.BlockSpec((tt, D), lambda b, i: (i, 0)),
                  full((RET_HEADS, c, c)), full((c, W)), full((c, W)), full((1, W)), st_spec],
        out_specs=[pl.BlockSpec((tt, W), lambda b, i: (b * nt + i, 0)), st_spec],
        out_shape=[jax.ShapeDtypeStruct((T, W), F32), jax.ShapeDtypeStruct((B, RET_HEADS, D, D), F32)],
        scratch_shapes=[pltpu.VMEM((RET_HEADS, D, D), F32)],
        compiler_params=_params(("parallel", "arbitrary")),
        name="retention",
    )(P, P, P, P, cos, sin, dmask, qdec, kdec, gchunk, s0.astype(F32))
    return y, s_out


def _merge_kernel(x_ref, g0_ref, g1_ref, g2_ref, yr_ref, ys_ref, yt_ref, wr_ref, ws_ref, wt_ref, wo_ref,
                  o_ref):
    gate = lambda g_ref: jax.nn.sigmoid(g_ref[...].astype(F32))
    m = (gate(g0_ref) * _dot(_bf(yr_ref[...]), wr_ref[...])
         + gate(g1_ref) * _dot_tn(_bf(ys_ref[0]), ws_ref[...])
         + gate(g2_ref) * _dot(_bf(yt_ref[...]), wt_ref[...]))
    o_ref[...] = x_ref[...] + _dot(_bf(m), wo_ref[...])


def _merge(x2d, P, y_rw, y_sa_t, y_ret, w_rw, w_sa, w_ret, w_o):
    T = x2d.shape[0]
    L = y_sa_t.shape[2]
    tm = min(512, L)
    nl = L // tm
    D = D_MODEL
    tok = lambda w, blk: pl.BlockSpec((tm, w), lambda i: (i, blk))
    full = lambda shape: pl.BlockSpec(shape, lambda i: (0, 0))
    return pl.pallas_call(
        _merge_kernel,
        grid=(T // tm,),
        in_specs=[tok(D, 0), tok(D, 0), tok(D, 1), tok(D, 2), tok(RW_WIDTH, 0),
                  pl.BlockSpec((1, SA_WIDTH, tm), lambda i: (i // nl, 0, i % nl)),
                  tok(RET_WIDTH, 0), full((RW_WIDTH, D)), full((SA_WIDTH, D)), full((RET_WIDTH, D)),
                  full((D, D))],
        out_specs=tok(D, 0),
        out_shape=jax.ShapeDtypeStruct((T, D), F32),
        compiler_params=_params(("parallel",)),
        name="merge",
    )(x2d, P, P, P, y_rw, y_sa_t, y_ret, w_rw, w_sa, w_ret, w_o)


def _mlp_kernel(x_ref, g_ref, wu_ref, wd_ref, gf_ref, o_ref, h_ref, acc_ref, *, final_norm):
    j = pl.program_id(1)

    @pl.when(j == 0)
    def _():
        x = x_ref[...]
        ms = jnp.mean(x * x, axis=-1, keepdims=True)
        h_ref[...] = _bf(x * lax.rsqrt(ms + EPS) * g_ref[...])
        acc_ref[...] = jnp.zeros_like(acc_ref)

    u = jnp.maximum(_dot(h_ref[...], wu_ref[...]), 0.0)
    acc_ref[...] += _dot(_bf(u * u), wd_ref[...])

    @pl.when(j == pl.num_programs(1) - 1)
    def _():
        xn = x_ref[...] + acc_ref[...]
        if final_norm:
            ms = jnp.mean(xn * xn, axis=-1, keepdims=True)
            xn = xn * lax.rsqrt(ms + EPS) * gf_ref[...]
        o_ref[...] = xn


def _mlp(x2d, g, w_up, w_down, g_final, final_norm):
    T = x2d.shape[0]
    tm = min(1024, T)
    tf = 1024
    D = D_MODEL
    return pl.pallas_call(
        functools.partial(_mlp_kernel, final_norm=final_norm),
        grid=(T // tm, D_FF // tf),
        in_specs=[pl.BlockSpec((tm, D), lambda i, j: (i, 0)),
                  pl.BlockSpec((1, D), lambda i, j: (0, 0)),
                  pl.BlockSpec((D, tf), lambda i, j: (0, j)),
                  pl.BlockSpec((tf, D), lambda i, j: (j, 0)),
                  pl.BlockSpec((1, D), lambda i, j: (0, 0))],
        out_specs=pl.BlockSpec((tm, D), lambda i, j: (i, 0)),
        out_shape=jax.ShapeDtypeStruct((T, D), F32),
        scratch_shapes=[pltpu.VMEM((tm, D), BF16), pltpu.VMEM((tm, D), F32)],
        compiler_params=_params(("parallel", "arbitrary")),
        name="mlp",
    )(x2d, g, w_up, w_down, g_final)


def _layer(x2d, B, L, past, caches, lp, wts, g_final, final_norm):
    k_past, v_past, ik_past, s_rw, shift_rw, s_ret = caches
    pos = past + jnp.arange(L, dtype=jnp.int32)
    row = lambda t: t.reshape(1, -1).astype(F32)
    P = _in_proj(x2d, row(lp['norm1_g']), wts['w_in'], wts['layer'])
    q_rot, qi_rot, k_rot, kw_rot = _sa_prep(P, L, _sa_tables(pos))
    y_rw, s_rw_new = _rwkv(P, B, L, shift_rw, s_rw, lp)
    y_sa = _dsa(P, B, L, past, q_rot, qi_rot, k_rot, kw_rot, k_past, v_past, ik_past)
    y_ret, s_ret_new = _retention(P, B, L, pos, s_ret)
    x2d = _merge(x2d, P, y_rw, y_sa, y_ret, wts['w_br_rwkv'], wts['w_br_dsa'], wts['w_br_ret'], wts['w_o'])
    x2d = _mlp(x2d, row(lp['norm2_g']), wts['w_up'], wts['w_down'], g_final, final_norm)
    P3 = P.reshape(B, L, P_COLS)
    last = P3[:, L - 1].astype(F32)
    shift_new = jnp.concatenate([last[:, P_RKV:P_RKV + 3 * RW_WIDTH], last[:, P_LORA:P_LORA + RW_LORA]], axis=1)
    k_new = k_rot.reshape(B, L, SA_KV_HEADS, SA_HEAD_DIM)
    v_new = P3[:, :, P_SAV:P_SAV + SA_KV_WIDTH].astype(F32).reshape(B, L, SA_KV_HEADS, SA_HEAD_DIM)
    ik_new = kw_rot.reshape(B, L, LANES)[:, :, :IDX_DIM]
    return x2d, (k_new, v_new, ik_new, s_rw_new, shift_new, s_ret_new)


def kernel(x_prompt, x_sample, cache_dsa_k, cache_dsa_v, cache_dsa_ik, state_rwkv, state_rwkv_shift, state_ret, norm1_g, w_in, rwkv_mu, rwkv_w0, rwkv_w2, rwkv_a0, rwkv_a2, rwkv_g2, rwkv_k_k, rwkv_k_a, rwkv_r_k, rwkv_lnx_g, w_br_rwkv, w_br_dsa, w_br_ret, w_o, norm2_g, w_up, w_down, final_norm_g):
    params = {
        'norm1_g': norm1_g, 'rwkv_mu': rwkv_mu, 'rwkv_w0': rwkv_w0, 'rwkv_w2': rwkv_w2,
        'rwkv_a0': rwkv_a0, 'rwkv_a2': rwkv_a2, 'rwkv_g2': rwkv_g2, 'rwkv_k_k': rwkv_k_k,
        'rwkv_k_a': rwkv_k_a, 'rwkv_r_k': rwkv_r_k, 'rwkv_lnx_g': rwkv_lnx_g, 'norm2_g': norm2_g,
    }
    depth = w_in.shape[0]
    Bp, Lp, D = x_prompt.shape
    Bs, Ls, _ = x_sample.shape
    past_s = cache_dsa_k.shape[2]
    xp = x_prompt.reshape(Bp * Lp, D).astype(F32)
    xs = x_sample.reshape(Bs * Ls, D).astype(F32)
    g_final = final_norm_g.reshape(1, D).astype(F32)
    zero_p = (None, None, None,
              jnp.zeros((Bp, RW_HEADS, RW_HEAD_DIM, RW_HEAD_DIM), F32), jnp.zeros((Bp, RW_COLS), F32),
              jnp.zeros((Bp, RET_HEADS, RET_HEAD_DIM, RET_HEAD_DIM), F32))
    p_states = [[] for _ in range(6)]
    s_states = [[] for _ in range(6)]
    w_all = _w_prep(w_in)
    for i in range(depth):
        lp = {name: arr[i] for name, arr in params.items()}
        wts = {'w_in': w_all, 'layer': i, 'w_br_rwkv': _bf(w_br_rwkv[i]), 'w_br_dsa': _bf(w_br_dsa[i]),
               'w_br_ret': _bf(w_br_ret[i]), 'w_o': _bf(w_o[i]), 'w_up': _bf(w_up[i]), 'w_down': _bf(w_down[i])}
        final = i == depth - 1
        cache_s = (cache_dsa_k[i], cache_dsa_v[i], cache_dsa_ik[i], state_rwkv[i], state_rwkv_shift[i],
                   state_ret[i])
        xp, new_p = _layer(xp, Bp, Lp, 0, zero_p, lp, wts, g_final, final)
        xs, new_s = _layer(xs, Bs, Ls, past_s, cache_s, lp, wts, g_final, final)
        for j in range(6):
            p_states[j].append(new_p[j])
            s_states[j].append(new_s[j])
    y_prompt = xp.reshape(Bp, Lp, D)
    y_sample = xs.reshape(Bs, Ls, D)
    p_out = [jnp.stack(t, axis=0) for t in p_states]
    s_out = [jnp.stack(t, axis=0) for t in s_states]
    return (y_prompt, y_sample, *p_out, *s_out)
```

```python
import functools

import numpy as np
import jax
import jax.numpy as jnp
from jax import lax
from jax.experimental import pallas as pl
from jax.experimental.pallas import tpu as pltpu

F32 = jnp.float32
BF16 = jnp.bfloat16
I32 = jnp.int32
I16 = jnp.int16

D_MODEL = 1024
CHUNK = 64
Q_BLOCK = 128
EPS = 1e-6

RW_HEADS = 8
RW_HEAD_DIM = 64
RW_WIDTH = RW_HEADS * RW_HEAD_DIM
RW_DECAY_LORA = 64
RW_ICLR_LORA = 64
RW_GATE_LORA = 128
RW_LORA = RW_DECAY_LORA + RW_ICLR_LORA + RW_GATE_LORA
RW_COLS = 3 * RW_WIDTH + RW_LORA
RW_GN_EPS = 64e-5

SA_HEADS = 8
SA_KV_HEADS = 2
SA_HEAD_DIM = 64
SA_WIDTH = SA_HEADS * SA_HEAD_DIM
SA_KV_WIDTH = SA_KV_HEADS * SA_HEAD_DIM
IDX_HEADS = 4
IDX_DIM = 64
IDX_WIDTH = IDX_HEADS * IDX_DIM
TOPK_MAX = 256
ROPE_THETA = 500000.0
ROPE_DIM = SA_HEAD_DIM // 4
INDEX_SCALE = (IDX_DIM ** -0.5) * (IDX_HEADS ** -0.5)
SA_COLS = SA_WIDTH + 2 * SA_KV_WIDTH + IDX_WIDTH + IDX_DIM + IDX_HEADS

RET_HEADS = 4
RET_HEAD_DIM = 128
RET_WIDTH = RET_HEADS * RET_HEAD_DIM
RET_ROPE_BASE = 10000.0
RET_COLS = 4 * RET_WIDTH

N_BRANCH = 3
GATE_COLS = N_BRANCH * D_MODEL
IN_COLS = RW_COLS + SA_COLS + RET_COLS + GATE_COLS
D_FF = 4 * D_MODEL

LANES = 128
TILE_ELEMS = 64 * 8 * LANES

P_GATE = 0
P_RET = P_GATE + GATE_COLS
P_RKV = P_RET + RET_COLS
P_SAQ = P_RKV + 3 * RW_WIDTH
P_LORA = P_SAQ + SA_WIDTH
P_QI = P_LORA + RW_LORA
P_SAK = P_QI + IDX_WIDTH
P_SAV = P_SAK + SA_KV_WIDTH
P_KIWI = P_SAV + SA_KV_WIDTH
P_COLS = 8192
INT_MIN = -2 ** 31
NEG_BIG = -1e30
LOG2_E = 1.4426950408889634
VMEM_LIMIT = 56 * 1024 * 1024


def _bf(x):
    return x.astype(BF16)


def _dot(a, b):
    return jnp.dot(a, b, preferred_element_type=F32)


def _dot_nt(a, b):
    return lax.dot_general(a, b, (((1,), (1,)), ((), ())), preferred_element_type=F32)


def _dot_tn(a, b):
    return lax.dot_general(a, b, (((0,), (0,)), ((), ())), preferred_element_type=F32)


def _dot_split2(a_exact, x):
    hi = _bf(x)
    lo = _bf(x - hi.astype(F32))
    return _dot(a_exact, hi) + _dot(a_exact, lo)


def _params(sem):
    return pltpu.CompilerParams(dimension_semantics=sem, vmem_limit_bytes=VMEM_LIMIT)


def _in_proj_kernel(x_ref, g_ref, w_ref, o_ref, h_ref):
    @pl.when(pl.program_id(1) == 0)
    def _():
        x = x_ref[...]
        ms = jnp.mean(x * x, axis=-1, keepdims=True)
        h_ref[...] = _bf(x * lax.rsqrt(ms + EPS) * g_ref[...])

    o_ref[...] = _dot(h_ref[...], w_ref[0]).astype(o_ref.dtype)


def _in_proj(x2d, g, w_all, layer):
    T = x2d.shape[0]
    tm = min(1024, T)
    tn = 1024
    return pl.pallas_call(
        _in_proj_kernel,
        grid=(T // tm, P_COLS // tn),
        in_specs=[pl.BlockSpec((tm, D_MODEL), lambda i, j: (i, 0)),
                  pl.BlockSpec((1, D_MODEL), lambda i, j: (0, 0)),
                  pl.BlockSpec((1, D_MODEL, tn), lambda i, j: (layer, 0, j))],
        out_specs=pl.BlockSpec((tm, tn), lambda i, j: (i, j)),
        out_shape=jax.ShapeDtypeStruct((T, P_COLS), BF16),
        scratch_shapes=[pltpu.VMEM((tm, D_MODEL), BF16)],
        compiler_params=_params(("parallel", "arbitrary")),
        name="in_proj",
    )(x2d, g, w_all)


KIWI_COLS = IDX_DIM + IDX_HEADS
TAIL0 = RW_COLS + SA_COLS - KIWI_COLS
TAIL_W = (IN_COLS - TAIL0 + LANES - 1) // LANES * LANES


def _w_prep_kernel(w_ref, o_ref):
    x = w_ref[0]
    lane = lax.broadcasted_iota(I32, (x.shape[0], LANES), 1)

    def put(off, v):
        o_ref[0, :, off:off + v.shape[1]] = _bf(v)

    nslab = TAIL_W // LANES
    part = IN_COLS - TAIL0 - (nslab - 1) * LANES
    slabs = [x[:, TAIL0 + s * LANES:TAIL0 + (s + 1) * LANES] for s in range(nslab - 1)]
    slabs.append(jnp.concatenate([x[:, IN_COLS - part:], jnp.zeros((x.shape[0], LANES - part), F32)], axis=1))
    rolled = [pltpu.roll(t, LANES - KIWI_COLS, 1) for t in slabs]
    for s in range((RET_COLS + GATE_COLS) // LANES):
        v = jnp.where(lane < LANES - KIWI_COLS, rolled[s], rolled[s + 1])
        put((P_RET if s < RET_COLS // LANES else P_GATE - RET_COLS) + s * LANES, v)
    sa = RW_COLS
    put(P_RKV, x[:, 0:3 * RW_WIDTH])
    put(P_LORA, x[:, 3 * RW_WIDTH:RW_COLS])
    put(P_SAQ, x[:, sa:sa + SA_WIDTH])
    put(P_SAK, x[:, sa + SA_WIDTH:sa + SA_WIDTH + SA_KV_WIDTH])
    put(P_SAV, x[:, sa + SA_WIDTH + SA_KV_WIDTH:sa + SA_WIDTH + 2 * SA_KV_WIDTH])
    put(P_QI, x[:, sa + SA_WIDTH + 2 * SA_KV_WIDTH:TAIL0])
    put(P_KIWI, jnp.where(lane < KIWI_COLS, x[:, TAIL0:TAIL0 + LANES], 0.0))
    put(P_KIWI + LANES, jnp.zeros((x.shape[0], P_COLS - P_KIWI - LANES), F32))


def _w_prep(w_in):
    depth, d, _ = w_in.shape
    tm = 256
    return pl.pallas_call(
        _w_prep_kernel,
        grid=(depth, d // tm),
        in_specs=[pl.BlockSpec((1, tm, IN_COLS), lambda l, i: (l, i, 0))],
        out_specs=pl.BlockSpec((1, tm, P_COLS), lambda l, i: (l, i, 0)),
        out_shape=jax.ShapeDtypeStruct((depth, d, P_COLS), BF16),
        compiler_params=_params(("parallel", "parallel")),
        name="w_prep",
    )(w_in)


def _rot_slab(x, c, s_lo, s_hi, shift):
    return x * c + pltpu.roll(x, LANES - shift, 1) * s_lo + pltpu.roll(x, shift, 1) * s_hi


def _sa_prep_kernel(q_ref, qi_ref, k_ref, kw_ref, c_ref, s1_ref, s2_ref, ck_ref, s1k_ref, s2k_ref,
                    qo_ref, qio_ref, ko_ref, kwo_ref):
    c, s1, s2 = c_ref[...], s1_ref[...], s2_ref[...]
    half = ROPE_DIM // 2
    scale = SA_HEAD_DIM ** -0.5 * LOG2_E
    for s in range(SA_WIDTH // LANES):
        sl = slice(s * LANES, (s + 1) * LANES)
        qo_ref[:, sl] = _rot_slab(q_ref[:, sl].astype(F32), c, s1, s2, half) * scale
    for s in range(IDX_WIDTH // LANES):
        sl = slice(s * LANES, (s + 1) * LANES)
        qio_ref[:, sl] = _rot_slab(qi_ref[:, sl].astype(F32), c, s1, s2, half)
    ko_ref[...] = _rot_slab(k_ref[...].astype(F32), c, s1, s2, half)
    kwo_ref[...] = _rot_slab(kw_ref[...].astype(F32), ck_ref[...], s1k_ref[...], s2k_ref[...], half)


def _sa_prep(P, L, tabs):
    T = P.shape[0]
    tm = min(512, L)
    nl = L // tm
    tab_spec = pl.BlockSpec((tm, LANES), lambda i: (i % nl, 0))
    return pl.pallas_call(
        _sa_prep_kernel,
        grid=(T // tm,),
        in_specs=[pl.BlockSpec((tm, SA_WIDTH), lambda i: (i, P_SAQ // SA_WIDTH)),
                  pl.BlockSpec((tm, IDX_WIDTH), lambda i: (i, P_QI // IDX_WIDTH)),
                  pl.BlockSpec((tm, LANES), lambda i: (i, P_SAK // LANES)),
                  pl.BlockSpec((tm, LANES), lambda i: (i, P_KIWI // LANES))] + [tab_spec] * 6,
        out_specs=[pl.BlockSpec((tm, SA_WIDTH), lambda i: (i, 0)),
                   pl.BlockSpec((tm, IDX_WIDTH), lambda i: (i, 0)),
                   pl.BlockSpec((tm, LANES), lambda i: (i, 0)),
                   pl.BlockSpec((tm, LANES), lambda i: (i, 0))],
        out_shape=[jax.ShapeDtypeStruct((T, SA_WIDTH), F32),
                   jax.ShapeDtypeStruct((T, IDX_WIDTH), F32),
                   jax.ShapeDtypeStruct((T, LANES), F32),
                   jax.ShapeDtypeStruct((T, LANES), F32)],
        compiler_params=_params(("parallel",)),
        name="sa_prep",
    )(P, P, P, P, *tabs)


def _sa_tables(pos):
    half = ROPE_DIM // 2
    freqs = 1.0 / (ROPE_THETA ** (jnp.arange(0, ROPE_DIM, 2, dtype=F32) / ROPE_DIM))
    ang = pos.astype(F32)[:, None] * freqs[None, :]
    cos, sin = jnp.cos(ang), jnp.sin(ang)
    n = pos.shape[0]
    pad = SA_HEAD_DIM - ROPE_DIM
    c_head = jnp.concatenate([cos, cos, jnp.ones((n, pad), F32)], axis=1)
    s1_head = jnp.concatenate([-sin, jnp.zeros((n, half + pad), F32)], axis=1)
    s2_head = jnp.concatenate([jnp.zeros((n, half), F32), sin, jnp.zeros((n, pad), F32)], axis=1)
    one, zero = jnp.ones((n, SA_HEAD_DIM), F32), jnp.zeros((n, SA_HEAD_DIM), F32)
    two = lambda t: jnp.concatenate([t, t], axis=1)
    return (two(c_head), two(s1_head), two(s2_head),
            jnp.concatenate([c_head, one], axis=1), jnp.concatenate([s1_head, zero], axis=1),
            jnp.concatenate([s2_head, zero], axis=1))


def _softplus(u):
    return jnp.maximum(u, 0.0) + jnp.log(1.0 + jnp.exp(-jnp.abs(u)))


def _rwkv_kernel(r_ref, k_ref, v_ref, lo_ref, shr_ref, shk_ref, shv_ref, shlo_ref, s0_ref,
                 mur_ref, muk_ref, muv_ref, mulo_ref, w0_ref, w2_ref, a0_ref, a2_ref, g2_ref,
                 kk_ref, ka_ref, rk_ref, lnx_ref, e_ref, tin_ref, tsuf_ref,
                 y_ref, so_ref,
                 cr_ref, ck_ref, cv_ref, clo_ref, s_ref,
                 kap_ref, rt_ref, bh_ref, kh_ref, bt_ref, kt_ref, vv_ref, gc_ref, yy_ref, rr_ref,
                 ac_ref, cc_ref):
    c = pl.program_id(1)
    TT = r_ref.shape[0]
    nch = TT // CHUNK
    N = RW_HEAD_DIM

    @pl.when(c == 0)
    def _():
        cr_ref[0:1, :] = shr_ref[0]
        ck_ref[0:1, :] = shk_ref[0]
        cv_ref[0:1, :] = shv_ref[0]
        clo_ref[0:1, :] = shlo_ref[0]
        s_ref[...] = s0_ref[0]

    def lerp(p_ref, carry_ref, mu_ref):
        p = p_ref[...].astype(F32)
        rolled = pltpu.roll(p, 1, 0)
        row0 = lax.broadcasted_iota(I32, p.shape, 0) == 0
        prev = jnp.where(row0, carry_ref[0:1, :], rolled)
        carry_ref[0:1, :] = p[TT - 1:TT, :]
        return p + (prev - p) * mu_ref[...]

    xr = lerp(r_ref, cr_ref, mur_ref)
    xk = lerp(k_ref, ck_ref, muk_ref)
    xv = lerp(v_ref, cv_ref, muv_ref)
    xlo = lerp(lo_ref, clo_ref, mulo_ref)
    xw = xlo[:, :RW_DECAY_LORA]
    xa = xlo[:, RW_DECAY_LORA:RW_DECAY_LORA + RW_ICLR_LORA]
    xg = xlo[:, RW_DECAY_LORA + RW_ICLR_LORA:]

    z = w0_ref[...] + _dot(_bf(jnp.tanh(xw)), w2_ref[...])
    w = -_softplus(-z) - 0.5
    ld = -jnp.exp(w)
    a = jax.nn.sigmoid(a0_ref[...] + _dot(_bf(xa), a2_ref[...]))
    gate = _dot(_bf(jax.nn.sigmoid(xg)), g2_ref[...])
    e_blk = e_ref[...]
    kk = xk * kk_ref[...]
    kk = kk / jnp.maximum(jnp.sqrt(_dot(_bf(kk * kk), e_blk)), 1e-12)
    k2 = xk * (1.0 + (a - 1.0) * ka_ref[...])
    bb = kk * a
    bonus = _dot(_bf(xr * k2 * rk_ref[...]), e_blk) * xv

    lin = _dot_split2(tin_ref[...], ld)
    lsuf = _dot_split2(tsuf_ref[...], ld)
    e_in = jnp.exp(lin)
    e_ninv = jnp.exp(-lin)
    e_suf = jnp.exp(lsuf)
    kap_ref[...] = kk * jnp.exp(lin - ld)
    rt_ref[...] = xr * e_in
    bh_ref[...] = bb * e_ninv
    kh_ref[...] = k2 * e_ninv
    bt_ref[...] = bb * e_suf
    kt_ref[...] = k2 * e_suf
    vv_ref[...] = xv
    gc_ref[...] = jnp.exp(lin + lsuf)

    ri = lax.broadcasted_iota(I32, (CHUNK, CHUNK), 0)
    ci = lax.broadcasted_iota(I32, (CHUNK, CHUNK), 1)
    strict = ri > ci
    incl = ri >= ci
    eye = (ri == ci).astype(F32)
    heads = range(RW_HEADS)
    hsl = [slice(h * N, (h + 1) * N) for h in heads]

    per_it = next(n for n in (4, 2, 1) if nch % n == 0)

    def coef_body(it, carry):
        chs = [it * per_it + t for t in range(per_it)]
        pairs = [(t, h) for t in range(per_it) for h in heads]
        rows = [pl.ds(pl.multiple_of(ch * CHUNK, CHUNK), CHUNK) for ch in chs]
        ld = lambda ref, p: ref[rows[p[0]], hsl[p[1]]]
        kap = [ld(kap_ref, p) for p in pairs]
        rt = [ld(rt_ref, p) for p in pairs]
        vh = [ld(vv_ref, p) for p in pairs]
        idx = range(len(pairs))
        gmat = [_dot_nt(_bf(jnp.concatenate([kap[n], rt[n]], axis=0)),
                        _bf(jnp.concatenate([ld(bh_ref, pairs[n]), ld(kh_ref, pairs[n])], axis=0)))
                for n in idx]
        n_ab = [jnp.where(strict, g[:CHUNK, :CHUNK], 0.0) for g in gmat]
        m_rb = [jnp.where(incl, g[CHUNK:, :CHUNK], 0.0) for g in gmat]
        m_v = [_bf(jnp.concatenate([jnp.where(strict, g[:CHUNK, CHUNK:], 0.0),
                                    jnp.where(incl, g[CHUNK:, CHUNK:], 0.0)], axis=0)) for g in gmat]
        mv = [_dot(m_v[n], _bf(vh[n])) for n in idx]
        x_inv = [eye - t for t in n_ab]
        pw = n_ab
        for _ in range(5):
            pwb = [_bf(p) for p in pw]
            pw = [_dot(p, p) for p in pwb]
            x_inv = [x + _dot(_bf(x), _bf(p)) for x, p in zip(x_inv, pw)]
        w = [_dot(_bf(x_inv[n]), _bf(jnp.concatenate([kap[n], mv[n][:CHUNK]], axis=1))) for n in idx]
        wb = [_bf(t) for t in w]
        ry = [jnp.concatenate([rt[n], mv[n][CHUNK:]], axis=1) - _dot(_bf(m_rb[n]), wb[n]) for n in idx]
        dmat = [_dot_tn(wb[n], _bf(ld(bt_ref, pairs[n]))) for n in idx]
        vtk = [_dot_tn(_bf(vh[n]), _bf(ld(kt_ref, pairs[n]))) for n in idx]
        for n, (t, h) in enumerate(pairs):
            rr_ref[rows[t], hsl[h]] = ry[n][:, :N]
            yy_ref[rows[t], hsl[h]] = ry[n][:, N:]
            ac_ref[chs[t], h] = -dmat[n][:N]
            cc_ref[chs[t], h] = vtk[n] - dmat[n][N:]
        return carry

    lax.fori_loop(0, nch // per_it, coef_body, 0)

    def state_body(ch, carry):
        rows = pl.ds(pl.multiple_of(ch * CHUNK, CHUNK), CHUNK)
        s_old = [s_ref[h] for h in heads]
        sb = [_bf(t) for t in s_old]
        s_new = [_dot(sb[h], _bf(ac_ref[ch, h])) for h in heads]
        y_c = [_dot_nt(_bf(rr_ref[rows, hsl[h]]), sb[h]) for h in heads]
        for h in heads:
            gch = gc_ref[pl.ds(pl.multiple_of(ch * CHUNK, CHUNK), 1), hsl[h]]
            s_ref[h] = s_old[h] * gch + s_new[h] + cc_ref[ch, h]
            yy_ref[rows, hsl[h]] = yy_ref[rows, hsl[h]] + y_c[h]
        return carry

    lax.fori_loop(0, nch, state_body, 0)

    y = yy_ref[...]
    mean = _dot(_bf(y), e_blk) * (1.0 / N)
    d = y - mean
    var = _dot(_bf(d * d), e_blk) * (1.0 / N)
    yn = d * lax.rsqrt(var + RW_GN_EPS) * lnx_ref[...]
    y_ref[...] = (yn + bonus) * gate

    @pl.when(c == pl.num_programs(1) - 1)
    def _():
        so_ref[0] = s_ref[...]


def _rwkv(P, B, L, shift_prev, s0, lp):
    T = P.shape[0]
    TT = min(256, L)
    nt = L // TT
    W = RW_WIDTH
    row = lambda t: t.reshape(1, -1).astype(F32)
    mu = lp['rwkv_mu']
    sh = shift_prev.astype(F32)
    pieces = lambda t: (t[..., 0:W], t[..., W:2 * W], t[..., 2 * W:3 * W], t[..., 3 * W:])
    mu_r, mu_k, mu_v, mu_lo = [row(t) for t in pieces(mu)]
    sh_r, sh_k, sh_v, sh_lo = [t.reshape(B, 1, -1) for t in pieces(sh)]
    hid = jnp.arange(W) // RW_HEAD_DIM
    e_blk = (hid[:, None] == hid[None, :]).astype(BF16)
    ti = jnp.arange(TT)
    same = (ti[:, None] // CHUNK) == (ti[None, :] // CHUNK)
    tri_in = (same & (ti[None, :] <= ti[:, None])).astype(BF16)
    tri_suf = (same & (ti[None, :] > ti[:, None])).astype(BF16)

    tok = lambda w, blk: pl.BlockSpec((TT, w), lambda b, c: (b * nt + c, blk))
    full = lambda shape: pl.BlockSpec(shape, lambda b, c: (0,) * len(shape))
    shs = lambda w: pl.BlockSpec((1, 1, w), lambda b, c: (b, 0, 0))
    st_spec = pl.BlockSpec((1, RW_HEADS, RW_HEAD_DIM, RW_HEAD_DIM), lambda b, c: (b, 0, 0, 0))
    big = lambda: pltpu.VMEM((TT, W), F32)
    y, s_out = pl.pallas_call(
        _rwkv_kernel,
        grid=(B, nt),
        in_specs=[tok(W, P_RKV // W), tok(W, P_RKV // W + 1), tok(W, P_RKV // W + 2),
                  tok(RW_LORA, P_LORA // RW_LORA),
                  shs(W), shs(W), shs(W), shs(RW_LORA), st_spec,
                  full((1, W)), full((1, W)), full((1, W)), full((1, RW_LORA)),
                  full((1, W)), full((RW_DECAY_LORA, W)), full((1, W)), full((RW_ICLR_LORA, W)),
                  full((RW_GATE_LORA, W)), full((1, W)), full((1, W)), full((1, W)), full((1, W)),
                  full((W, W)), full((TT, TT)), full((TT, TT))],
        out_specs=[pl.BlockSpec((TT, W), lambda b, c: (b * nt + c, 0)), st_spec],
        out_shape=[jax.ShapeDtypeStruct((T, W), F32),
                   jax.ShapeDtypeStruct((B, RW_HEADS, RW_HEAD_DIM, RW_HEAD_DIM), F32)],
        scratch_shapes=[pltpu.VMEM((8, W), F32), pltpu.VMEM((8, W), F32), pltpu.VMEM((8, W), F32),
                        pltpu.VMEM((8, RW_LORA), F32),
                        pltpu.VMEM((RW_HEADS, RW_HEAD_DIM, RW_HEAD_DIM), F32)] + [big() for _ in range(10)]
        + [pltpu.VMEM((TT // CHUNK, RW_HEADS, RW_HEAD_DIM, RW_HEAD_DIM), F32) for _ in range(2)],
        compiler_params=_params(("parallel", "arbitrary")),
        name="rwkv",
    )(P, P, P, P, sh_r, sh_k, sh_v, sh_lo, s0.astype(F32),
      mu_r, mu_k, mu_v, mu_lo, row(lp['rwkv_w0']), _bf(lp['rwkv_w2']), row(lp['rwkv_a0']),
      _bf(lp['rwkv_a2']), _bf(lp['rwkv_g2']), row(lp['rwkv_k_k']), row(lp['rwkv_k_a']),
      row(lp['rwkv_r_k']), row(lp['rwkv_lnx_g']), e_blk, tri_in, tri_suf)
    return y, s_out


ONES_ROWS = 16
BOUND_SLACK = 1.05
SAFE_BOUND = 60.0


def _dsa_kernel(q_ref, qi_ref, kw_ref, k_ref, v_ref, ki_ref, y_ref,
                kb_ref, vt_ref, kib_ref, kmax_ref, keys_ref, hi_ref, lo_ref, bias_ref, pstar_ref,
                *, past, lk_real, topk, kc):
    i = pl.program_id(1)
    qb = q_ref.shape[1]
    lk = k_ref.shape[1]
    kf = float(topk)
    HD = SA_HEAD_DIM

    ones_sq = jnp.ones((LANES, LANES), BF16)

    def head_slab(x, odd):
        lane = lax.broadcasted_iota(I32, x.shape, 1)
        return jnp.where(lane < HD, pltpu.roll(x, HD, 1) if odd else x, 0.0)

    @pl.when(i == 0)
    def _():
        kib_ref[...] = _bf(ki_ref[0])
        k = k_ref[0]
        lane = lax.broadcasted_iota(I32, k.shape, 1)
        for c in range(SA_KV_HEADS):
            kc_b = _bf(head_slab(k, c == 1))
            kb_ref[c] = jnp.where(lane == HD, jnp.ones_like(kc_b), kc_b)
            kf32 = kc_b.astype(F32)
            n2 = _dot(_bf(kf32 * kf32), ones_sq)
            kmax_ref[c] = jnp.broadcast_to(jnp.max(n2, axis=0, keepdims=True), (8, LANES))
        vt = v_ref[0].astype(F32).T
        for c in range(SA_KV_HEADS):
            vt_ref[c, 0:HD, :] = _bf(vt[c * HD:(c + 1) * HD, :])
            vt_ref[c, HD:HD + ONES_ROWS, :] = jnp.ones((ONES_ROWS, lk), BF16)

    qpos = past + i * qb + lax.broadcasted_iota(I32, (1, qb), 1)
    limit = jnp.minimum((qpos // CHUNK + 1) * CHUNK, lk_real)
    hi = jnp.minimum(past + (i + 1) * qb, lk_real)
    nk = lax.shift_right_logical(hi + (2 * kc - 1), kc.bit_length())
    sub_idx = lax.broadcasted_iota(I32, (kc, qb), 0)
    kwt = kw_ref[0].T
    wi_h = [kwt[IDX_DIM + h:IDX_DIM + h + 1, :] * INDEX_SCALE for h in range(IDX_HEADS)]
    qi = qi_ref[0]
    qi_h = [_bf(qi[:, h * IDX_DIM:(h + 1) * IDX_DIM]) for h in range(IDX_HEADS)]

    def chunk(j):
        return pl.ds(pl.multiple_of(j * kc, kc), kc)

    def pair(t):
        return (2 * t, 2 * t + 1)

    def score_body(t, carry):
        js = pair(t)
        dots = [[_dot_nt(kib_ref[chunk(j), 0:IDX_DIM], qi_h[h]) for h in range(IDX_HEADS)] for j in js]
        for j, d in zip(js, dots):
            s = jnp.zeros((kc, qb), F32)
            for h in range(IDX_HEADS):
                s = s + jnp.maximum(d[h], 0.0) * wi_h[h]
            s = jnp.where(s == 0.0, 0.0, s)
            bits = lax.bitcast_convert_type(s, I32)
            key = bits ^ ((bits >> 31) & 0x7FFFFFFF)
            key = jnp.where((sub_idx + j * kc) < limit, key, INT_MIN)
            keys_ref[chunk(j), :] = key
            hi_ref[chunk(j), :] = (key >> 16).astype(I16)
        return carry

    lax.fori_loop(0, nk, score_body, 0)

    def fold(m, rows):
        parts = [m[r * rows:(r + 1) * rows] for r in range(kc // rows)]
        while len(parts) > 1:
            parts = [a + b for a, b in zip(parts[0::2], parts[1::2])]
        return parts[0]

    max_trips = lk // (2 * kc)

    def short_trips(body, carry):
        base = jnp.int32(0)
        p = 1 << (max_trips.bit_length() - 1)
        while p:
            take = (nk & p) != 0

            def run(c, base=base, p=p):
                for s in range(p):
                    c = body(base + s, c)
                return c

            carry = lax.cond(take, run, lambda c: c, carry)
            base = base + jnp.where(take, p, 0)
            p //= 2
        return carry

    def count(pred):
        def body(t, acc):
            for j in pair(t):
                acc = acc + fold(jnp.where(pred(keys_ref[chunk(j), :], j), 1.0, 0.0), 8)
            return acc

        return jnp.sum(short_trips(body, jnp.zeros((8, qb), F32)), axis=0, keepdims=True)

    def rows16(x8):
        return jnp.concatenate([x8, x8], axis=0).astype(I16)

    def slab_count(x, thr16, strict):
        parts = []
        for r in range(kc // 16):
            xs = x[r * 16:(r + 1) * 16]
            hit = xs > thr16 if strict else xs >= thr16
            parts.append(jnp.where(hit, jnp.int16(1), jnp.int16(0)))
        while len(parts) > 1:
            parts = [a + b for a, b in zip(parts[0::2], parts[1::2])]
        return parts[0]

    def sublane_allsum(c):
        for sh in (4, 2, 1):
            c = c + pltpu.roll(c, sh, 0)
        return c

    def count16(ref, thr16, strict=False):
        def body(t, acc):
            for j in pair(t):
                acc = acc + slab_count(ref[chunk(j), :], thr16, strict)
            return acc

        w = pltpu.bitcast(short_trips(body, jnp.zeros((16, qb), I16)), I32)
        s = (w & 0xFFFF) + lax.shift_right_logical(w, 16)
        tiles = [s[:, n * LANES:(n + 1) * LANES] for n in range(qb // LANES)]
        out = []
        for a, b in zip(tiles[0::2], tiles[1::2]):
            c = sublane_allsum(a + lax.shift_left(b, 16))
            out += [c & 0xFFFF, lax.shift_right_logical(c, 16)]
        if len(tiles) % 2:
            out.append(sublane_allsum(tiles[-1]))
        return jnp.concatenate(out, axis=1) if len(out) > 1 else out[0]

    def kth_bits16(ref, need, known=0, nbits=16):
        def bit_body(it, u):
            cand = u | lax.shift_left(jnp.int32(1), nbits - 1 - it)
            return jnp.where(count16(ref, rows16(cand - 32768)) >= need, cand, u)

        return lax.fori_loop(0, nbits, bit_body, jnp.full((8, qb), known, I32))

    t_hi8 = kth_bits16(hi_ref, topk) - 32768
    t_hi16 = rows16(t_hi8)
    need_lo = topk - count16(hi_ref, t_hi16, strict=True)
    t_hi_tile = jnp.concatenate([t_hi16] * (kc // 16), axis=0)

    def lo_body(t, carry):
        for j in pair(t):
            lo = ((keys_ref[chunk(j), :] & 0xFFFF) - 32768).astype(I16)
            lo_ref[chunk(j), :] = jnp.where(hi_ref[chunk(j), :] == t_hi_tile, lo, jnp.int16(-32768))
        return carry

    short_trips(lo_body, jnp.int32(0))
    tau = (t_hi8 * 65536 + kth_bits16(lo_ref, need_lo))[0:1, :]
    cnt_ge = count(lambda kj, j: kj >= tau)
    cnt_gt = count(lambda kj, j: kj > tau)
    need = kf - cnt_gt
    excess = jnp.logical_and(cnt_ge > kf, tau != INT_MIN)
    p_default = jnp.where(tau == INT_MIN, -1, 2 ** 30).astype(I32)
    pstar_ref[...] = jnp.broadcast_to(p_default, pstar_ref.shape)

    assert lk <= 32768

    @pl.when(jnp.max(jnp.where(excess, 1.0, 0.0)) > 0.0)
    def _():
        def rev_body(t, carry):
            for j in pair(t):
                rev = jnp.where(keys_ref[chunk(j), :] == tau, 32767 - (sub_idx + j * kc), -32768)
                lo_ref[chunk(j), :] = rev.astype(I16)
            return carry

        short_trips(rev_body, jnp.int32(0))
        need8 = jnp.broadcast_to(need.astype(I32), (8, qb))
        nb = max(1, int(lk - 1).bit_length())
        known = (0xFFFF >> nb) << nb
        p = (65535 - kth_bits16(lo_ref, need8, known, nb))[0:1, :]
        pstar_ref[...] = jnp.broadcast_to(jnp.where(excess, p, p_default), pstar_ref.shape)

    pstar = pstar_ref[0:1, :]

    def bias_body(t, carry):
        for j in pair(t):
            kj = keys_ref[chunk(j), :]
            sel = jnp.logical_or(kj > tau, jnp.logical_and(kj == tau, (sub_idx + j * kc) <= pstar))
            bias_ref[chunk(j), :] = _bf(jnp.where(sel, 0.0, NEG_BIG))
        return carry

    short_trips(bias_body, jnp.int32(0))

    q = q_ref[0]
    group = SA_HEADS // SA_KV_HEADS
    lane_q = lax.broadcasted_iota(I32, (qb, LANES), 1)
    q_heads, bounds = [], []
    for h in range(SA_HEADS):
        qh = _bf(head_slab(q[:, (h // 2) * LANES:(h // 2 + 1) * LANES], h % 2 == 1))
        qf = qh.astype(F32)
        qn2 = _dot(_bf(qf * qf), ones_sq)
        q_heads.append(qh)
        bounds.append(jnp.sqrt(qn2 * kmax_ref[h // group, 0:1, :]) * BOUND_SLACK)
    worst = bounds[0]
    for b in bounds[1:]:
        worst = jnp.maximum(worst, b)
    safe = jnp.max(worst) <= SAFE_BOUND

    zero_acc = jnp.zeros((HD + ONES_ROWS, qb), F32)

    def vt_chunk(c, j):
        return vt_ref[c, :, pl.ds(pl.multiple_of(j * kc, kc), kc)]

    @pl.when(safe)
    def _():
        hs = range(SA_HEADS)
        qa = [jnp.where(lane_q == HD, _bf(-bounds[h]), q_heads[h]) for h in hs]

        def body(t, accs):
            js = pair(t)
            logit = [[_dot_nt(kb_ref[h // group, chunk(j), :], qa[h]) for h in hs] for j in js]
            bias = [bias_ref[chunk(j), :].astype(F32) for j in js]
            p = [[_bf(jnp.exp2(lg + b)) for lg in row] for b, row in zip(bias, logit)]
            pv = [[_dot(vt_chunk(h // group, j), row[h]) for h in hs] for j, row in zip(js, p)]
            return tuple(accs[h] + pv[0][h] + pv[1][h] for h in hs)

        accs = lax.fori_loop(0, nk, body, (zero_acc,) * SA_HEADS)
        for h in hs:
            y_ref[0, h * HD:(h + 1) * HD, :] = accs[h][:HD] / accs[h][HD:HD + 1]

    @pl.when(jnp.logical_not(safe))
    def _():
        for h in range(SA_HEADS):
            c = h // group

            def logit(j):
                return _dot_nt(kb_ref[c, chunk(j), :], q_heads[h]) + bias_ref[chunk(j), :].astype(F32)

            def max_body(t, m):
                for j in pair(t):
                    m = jnp.maximum(m, jnp.max(logit(j), axis=0, keepdims=True))
                return m

            m = lax.fori_loop(0, nk, max_body, jnp.full((1, qb), NEG_BIG, F32))

            def sum_body(t, acc):
                for j in pair(t):
                    acc = acc + _dot(vt_chunk(c, j), _bf(jnp.exp2(logit(j) - m)))
                return acc

            acc = lax.fori_loop(0, nk, sum_body, zero_acc)
            y_ref[0, h * HD:(h + 1) * HD, :] = acc[:HD] / acc[HD:HD + 1]


def _dsa_call(q3, qi3, kw3, k_all, v_all, ki_all, v_blk, *, lk, past, lk_real, topk, qb, kc):
    B, n_q = q3.shape[:2]
    kern = functools.partial(_dsa_kernel, past=past, lk_real=lk_real, topk=topk, kc=kc)
    once = dict(pipeline_mode=pl.Buffered(1)) if n_q // qb > 1 else {}
    return pl.pallas_call(
        kern,
        grid=(B, n_q // qb),
        in_specs=[pl.BlockSpec((1, qb, SA_WIDTH), lambda b, i: (b, i, 0)),
                  pl.BlockSpec((1, qb, IDX_WIDTH), lambda b, i: (b, i, 0)),
                  pl.BlockSpec((1, qb, LANES), lambda b, i: (b, i, 0)),
                  pl.BlockSpec((1, lk, LANES), lambda b, i: (b, 0, 0), **once),
                  pl.BlockSpec((1, lk, LANES), lambda b, i: (b, 0, v_blk), **once),
                  pl.BlockSpec((1, lk, LANES), lambda b, i: (b, 0, 0), **once)],
        out_specs=pl.BlockSpec((1, SA_WIDTH, qb), lambda b, i: (b, 0, i)),
        out_shape=jax.ShapeDtypeStruct((B, SA_WIDTH, n_q), F32),
        scratch_shapes=[pltpu.VMEM((SA_KV_HEADS, lk, LANES), BF16),
                        pltpu.VMEM((SA_KV_HEADS, SA_HEAD_DIM + ONES_ROWS, lk), BF16),
                        pltpu.VMEM((lk, LANES), BF16),
                        pltpu.VMEM((SA_KV_HEADS, 8, LANES), F32),
                        pltpu.VMEM((lk, qb), I32), pltpu.VMEM((lk, qb), I16), pltpu.VMEM((lk, qb), I16),
                        pltpu.VMEM((lk, qb), BF16),
                        pltpu.VMEM((8, qb), I32)],
        compiler_params=_params(("parallel", "arbitrary")),
        name="dsa",
    )(q3, qi3, kw3, k_all, v_all, ki_all)


def _round_up(x, m):
    return (x + m - 1) // m * m


def _dsa(P, B, L, past, q_rot, qi_rot, k_rot, kw_rot, k_past, v_past, ik_past):
    lk_real = past + L
    topk = min(TOPK_MAX, lk_real // 4)
    qb = 2 * LANES if (past == 0 and L % (2 * LANES) == 0) else LANES
    lq = _round_up(L, qb)
    qpad = lambda t: t if lq == L else jnp.pad(t, ((0, 0), (0, lq - L), (0, 0)))
    q3 = qpad(q_rot.reshape(B, L, SA_WIDTH))
    qi3 = qpad(qi_rot.reshape(B, L, IDX_WIDTH))
    kw3 = kw_rot.reshape(B, L, LANES)
    k3 = k_rot.reshape(B, L, LANES)
    common = dict(past=past, lk_real=lk_real, topk=topk, qb=qb)
    kc = TILE_ELEMS // qb // 2 if qb == LANES else TILE_ELEMS // qb
    if past == 0:
        while L % (2 * kc):
            kc //= 2
        P3 = P.reshape(B, L, P_COLS)
        return _dsa_call(q3, qi3, kw3, k3, P3, kw3, P_SAV // LANES, lk=L, kc=kc, **common)
    lk = _round_up(lk_real, 2 * kc)
    zpad = jnp.zeros((B, lk - lk_real, LANES), F32)
    v_new = P.reshape(B, L, P_COLS)[:, :, P_SAV:P_SAV + LANES].astype(F32)
    ik_p = jnp.concatenate([ik_past.astype(F32), jnp.zeros((B, past, LANES - IDX_DIM), F32)], axis=2)
    k_all = jnp.concatenate([k_past.reshape(B, past, LANES).astype(F32), k3, zpad], axis=1)
    v_all = jnp.concatenate([v_past.reshape(B, past, LANES).astype(F32), v_new, zpad], axis=1)
    ki_all = jnp.concatenate([ik_p, kw3, zpad], axis=1)
    y = _dsa_call(q3, qi3, qpad(kw3), k_all, v_all, ki_all, 0, lk=lk, kc=kc, **common)
    return y[:, :, :L]


def _ret_kernel(q_ref, k_ref, v_ref, g_ref, cos_ref, sin_ref, dm_ref, qd_ref, kd_ref, gc_ref, s0_ref,
                y_ref, so_ref, s_ref):
    c = pl.program_id(1)
    D = RET_HEAD_DIM
    cs = dm_ref.shape[1]
    nch = q_ref.shape[0] // cs
    heads = range(RET_HEADS)
    hsl = [slice(h * D, (h + 1) * D) for h in heads]
    rows = [slice(ch * cs, (ch + 1) * cs) for ch in range(nch)]
    prob = [(ch, h) for ch in range(nch) for h in heads]

    @pl.when(c == 0)
    def _():
        s_ref[...] = s0_ref[0]

    cos, sin = cos_ref[...], sin_ref[...]
    qb_, kb_, kd_, vb_ = [], [], [], []
    for h in heads:
        q = q_ref[:, hsl[h]].astype(F32)
        k = k_ref[:, hsl[h]].astype(F32)
        k = (k * cos + pltpu.roll(k, D // 2, 1) * sin) * (D ** -0.5)
        qb_.append(_bf(q * cos + pltpu.roll(q, D // 2, 1) * sin))
        kb_.append(_bf(k))
        kd_.append([_bf(k[r] * kd_ref[:, hsl[h]]) for r in rows])
        vb_.append(_bf(v_ref[:, hsl[h]]))
    scores = {(ch, h): _dot_nt(qb_[h][rows[ch]], kb_[h][rows[ch]]) * dm_ref[h] for ch, h in prob}
    ktv = {(ch, h): _dot_tn(kd_[h][ch], vb_[h][rows[ch]]) for ch, h in prob}
    intra = {(ch, h): _dot(_bf(scores[ch, h]), vb_[h][rows[ch]]) for ch, h in prob}
    s_at = {}
    for h in heads:
        s = s_ref[h]
        for ch in range(nch):
            s_at[ch, h] = _bf(s)
            s = s * gc_ref[:, hsl[h]] + ktv[ch, h]
        s_ref[h] = s
    cross = {(ch, h): _dot(qb_[h][rows[ch]], s_at[ch, h]) for ch, h in prob}
    for ch in range(nch):
        outs = []
        for h in heads:
            o = intra[ch, h] + cross[ch, h] * qd_ref[:, hsl[h]]
            o = o * lax.rsqrt(jnp.mean(o * o, axis=-1, keepdims=True) + EPS)
            outs.append(jax.nn.silu(g_ref[rows[ch], hsl[h]].astype(F32)) * o)
        y_ref[rows[ch], :] = jnp.concatenate(outs, axis=1)

    @pl.when(c == pl.num_programs(1) - 1)
    def _():
        so_ref[0] = s_ref[...]


def _retention(P, B, L, pos, s0):
    T = P.shape[0]
    c = min(CHUNK, L)
    nc = L // c
    D = RET_HEAD_DIM
    freqs = 1.0 / (RET_ROPE_BASE ** jnp.linspace(0.0, 1.0, D // 2, dtype=F32))
    ang = pos.astype(F32)[:, None] * freqs[None, :]
    cos = jnp.concatenate([jnp.cos(ang)] * 2, axis=1)
    sin = jnp.concatenate([-jnp.sin(ang), jnp.sin(ang)], axis=1)
    log_gamma = jnp.log1p(-jnp.exp2(-5.0 - jnp.arange(RET_HEADS, dtype=F32)))
    idx = jnp.arange(c, dtype=F32)
    dmask = jnp.exp(jnp.abs(idx[:, None] - idx[None, :])[None] * log_gamma[:, None, None])
    lanes = lambda t: jnp.repeat(t, D, axis=1)
    qdec = lanes(jnp.exp((idx[:, None] + 1.0) * log_gamma[None, :]))
    kdec = lanes(jnp.exp((c - 1.0 - idx)[:, None] * log_gamma[None, :]))
    gchunk = lanes(jnp.exp(c * log_gamma)[None, :])

    W = RET_WIDTH
    tt = min(4 * c, L)
    nt = L // tt
    tok = lambda blk: pl.BlockSpec((tt, W), lambda b, i: (b * nt + i, blk))
    full = lambda shape: pl.BlockSpec(shape, lambda b, i: (0,) * len(shape))
    st_spec = pl.BlockSpec((1, RET_HEADS, D, D), lambda b, i: (b, 0, 0, 0))
    y, s_out = pl.pallas_call(
        _ret_kernel,
        grid=(B, nt),
        in_specs=[tok(P_RET // W), tok(P_RET // W + 1), tok(P_RET // W + 2), tok(P_RET // W + 3),
                  pl.BlockSpec((tt, D), lambda b, i: (i, 0)), pl.BlockSpec((tt, D), lambda b, i: (i, 0)),
                  full((RET_HEADS, c, c)), full((c, W)), full((c, W)), full((1, W)), st_spec],
        out_specs=[pl.BlockSpec((tt, W), lambda b, i: (b * nt + i, 0)), st_spec],
        out_shape=[jax.ShapeDtypeStruct((T, W), F32), jax.ShapeDtypeStruct((B, RET_HEADS, D, D), F32)],
        scratch_shapes=[pltpu.VMEM((RET_HEADS, D, D), F32)],
        compiler_params=_params(("parallel", "arbitrary")),
        name="retention",
    )(P, P, P, P, cos, sin, dmask, qdec, kdec, gchunk, s0.astype(F32))
    return y, s_out


def _merge_kernel(x_ref, g0_ref, g1_ref, g2_ref, yr_ref, ys_ref, yt_ref, wr_ref, ws_ref, wt_ref, wo_ref,
                  o_ref):
    gate = lambda g_ref: jax.nn.sigmoid(g_ref[...].astype(F32))
    m = (gate(g0_ref) * _dot(_bf(yr_ref[...]), wr_ref[...])
         + gate(g1_ref) * _dot_tn(_bf(ys_ref[0]), ws_ref[...])
         + gate(g2_ref) * _dot(_bf(yt_ref[...]), wt_ref[...]))
    o_ref[...] = x_ref[...] + _dot(_bf(m), wo_ref[...])


def _merge(x2d, P, y_rw, y_sa_t, y_ret, w_rw, w_sa, w_ret, w_o):
    T = x2d.shape[0]
    L = y_sa_t.shape[2]
    tm = min(512, L)
    nl = L // tm
    D = D_MODEL
    tok = lambda w, blk: pl.BlockSpec((tm, w), lambda i: (i, blk))
    full = lambda shape: pl.BlockSpec(shape, lambda i: (0, 0))
    return pl.pallas_call(
        _merge_kernel,
        grid=(T // tm,),
        in_specs=[tok(D, 0), tok(D, 0), tok(D, 1), tok(D, 2), tok(RW_WIDTH, 0),
                  pl.BlockSpec((1, SA_WIDTH, tm), lambda i: (i // nl, 0, i % nl)),
                  tok(RET_WIDTH, 0), full((RW_WIDTH, D)), full((SA_WIDTH, D)), full((RET_WIDTH, D)),
                  full((D, D))],
        out_specs=tok(D, 0),
        out_shape=jax.ShapeDtypeStruct((T, D), F32),
        compiler_params=_params(("parallel",)),
        name="merge",
    )(x2d, P, P, P, y_rw, y_sa_t, y_ret, w_rw, w_sa, w_ret, w_o)


def _mlp_kernel(x_ref, g_ref, wu_ref, wd_ref, gf_ref, o_ref, h_ref, acc_ref, *, final_norm):
    j = pl.program_id(1)

    @pl.when(j == 0)
    def _():
        x = x_ref[...]
        ms = jnp.mean(x * x, axis=-1, keepdims=True)
        h_ref[...] = _bf(x * lax.rsqrt(ms + EPS) * g_ref[...])
        acc_ref[...] = jnp.zeros_like(acc_ref)

    u = jnp.maximum(_dot(h_ref[...], wu_ref[...]), 0.0)
    acc_ref[...] += _dot(_bf(u * u), wd_ref[...])

    @pl.when(j == pl.num_programs(1) - 1)
    def _():
        xn = x_ref[...] + acc_ref[...]
        if final_norm:
            ms = jnp.mean(xn * xn, axis=-1, keepdims=True)
            xn = xn * lax.rsqrt(ms + EPS) * gf_ref[...]
        o_ref[...] = xn


def _mlp(x2d, g, w_up, w_down, g_final, final_norm):
    T = x2d.shape[0]
    tm = min(1024, T)
    tf = 1024
    D = D_MODEL
    return pl.pallas_call(
        functools.partial(_mlp_kernel, final_norm=final_norm),
        grid=(T // tm, D_FF // tf),
        in_specs=[pl.BlockSpec((tm, D), lambda i, j: (i, 0)),
                  pl.BlockSpec((1, D), lambda i, j: (0, 0)),
                  pl.BlockSpec((D, tf), lambda i, j: (0, j)),
                  pl.BlockSpec((tf, D), lambda i, j: (j, 0)),
                  pl.BlockSpec((1, D), lambda i, j: (0, 0))],
        out_specs=pl.BlockSpec((tm, D), lambda i, j: (i, 0)),
        out_shape=jax.ShapeDtypeStruct((T, D), F32),
        scratch_shapes=[pltpu.VMEM((tm, D), BF16), pltpu.VMEM((tm, D), F32)],
        compiler_params=_params(("parallel", "arbitrary")),
        name="mlp",
    )(x2d, g, w_up, w_down, g_final)


def _layer(x2d, B, L, past, caches, lp, wts, g_final, final_norm):
    k_past, v_past, ik_past, s_rw, shift_rw, s_ret = caches
    pos = past + jnp.arange(L, dtype=jnp.int32)
    row = lambda t: t.reshape(1, -1).astype(F32)
    P = _in_proj(x2d, row(lp['norm1_g']), wts['w_in'], wts['layer'])
    q_rot, qi_rot, k_rot, kw_rot = _sa_prep(P, L, _sa_tables(pos))
    y_rw, s_rw_new = _rwkv(P, B, L, shift_rw, s_rw, lp)
    y_sa = _dsa(P, B, L, past, q_rot, qi_rot, k_rot, kw_rot, k_past, v_past, ik_past)
    y_ret, s_ret_new = _retention(P, B, L, pos, s_ret)
    x2d = _merge(x2d, P, y_rw, y_sa, y_ret, wts['w_br_rwkv'], wts['w_br_dsa'], wts['w_br_ret'], wts['w_o'])
    x2d = _mlp(x2d, row(lp['norm2_g']), wts['w_up'], wts['w_down'], g_final, final_norm)
    P3 = P.reshape(B, L, P_COLS)
    last = P3[:, L - 1].astype(F32)
    shift_new = jnp.concatenate([last[:, P_RKV:P_RKV + 3 * RW_WIDTH], last[:, P_LORA:P_LORA + RW_LORA]], axis=1)
    k_new = k_rot.reshape(B, L, SA_KV_HEADS, SA_HEAD_DIM)
    v_new = P3[:, :, P_SAV:P_SAV + SA_KV_WIDTH].astype(F32).reshape(B, L, SA_KV_HEADS, SA_HEAD_DIM)
    ik_new = kw_rot.reshape(B, L, LANES)[:, :, :IDX_DIM]
    return x2d, (k_new, v_new, ik_new, s_rw_new, shift_new, s_ret_new)


def kernel(x_prompt, x_sample, cache_dsa_k, cache_dsa_v, cache_dsa_ik, state_rwkv, state_rwkv_shift, state_ret, norm1_g, w_in, rwkv_mu, rwkv_w0, rwkv_w2, rwkv_a0, rwkv_a2, rwkv_g2, rwkv_k_k, rwkv_k_a, rwkv_r_k, rwkv_lnx_g, w_br_rwkv, w_br_dsa, w_br_ret, w_o, norm2_g, w_up, w_down, final_norm_g):
    params = {
        'norm1_g': norm1_g, 'rwkv_mu': rwkv_mu, 'rwkv_w0': rwkv_w0, 'rwkv_w2': rwkv_w2,
        'rwkv_a0': rwkv_a0, 'rwkv_a2': rwkv_a2, 'rwkv_g2': rwkv_g2, 'rwkv_k_k': rwkv_k_k,
        'rwkv_k_a': rwkv_k_a, 'rwkv_r_k': rwkv_r_k, 'rwkv_lnx_g': rwkv_lnx_g, 'norm2_g': norm2_g,
    }
    depth = w_in.shape[0]
    Bp, Lp, D = x_prompt.shape
    Bs, Ls, _ = x_sample.shape
    past_s = cache_dsa_k.shape[2]
    xp = x_prompt.reshape(Bp * Lp, D).astype(F32)
    xs = x_sample.reshape(Bs * Ls, D).astype(F32)
    g_final = final_norm_g.reshape(1, D).astype(F32)
    zero_p = (None, None, None,
              jnp.zeros((Bp, RW_HEADS, RW_HEAD_DIM, RW_HEAD_DIM), F32), jnp.zeros((Bp, RW_COLS), F32),
              jnp.zeros((Bp, RET_HEADS, RET_HEAD_DIM, RET_HEAD_DIM), F32))
    p_states = [[] for _ in range(6)]
    s_states = [[] for _ in range(6)]
    w_all = _w_prep(w_in)
    for i in range(depth):
        lp = {name: arr[i] for name, arr in params.items()}
        wts = {'w_in': w_all, 'layer': i, 'w_br_rwkv': _bf(w_br_rwkv[i]), 'w_br_dsa': _bf(w_br_dsa[i]),
               'w_br_ret': _bf(w_br_ret[i]), 'w_o': _bf(w_o[i]), 'w_up': _bf(w_up[i]), 'w_down': _bf(w_down[i])}
        final = i == depth - 1
        cache_s = (cache_dsa_k[i], cache_dsa_v[i], cache_dsa_ik[i], state_rwkv[i], state_rwkv_shift[i],
                   state_ret[i])
        xp, new_p = _layer(xp, Bp, Lp, 0, zero_p, lp, wts, g_final, final)
        xs, new_s = _layer(xs, Bs, Ls, past_s, cache_s, lp, wts, g_final, final)
        for j in range(6):
            p_states[j].append(new_p[j])
            s_states[j].append(new_s[j])
    y_prompt = xp.reshape(Bp, Lp, D)
    y_sample = xs.reshape(Bs, Ls, D)
    p_out = [jnp.stack(t, axis=0) for t in p_states]
    s_out = [jnp.stack(t, axis=0) for t in s_states]
    return (y_prompt, y_sample, *p_out, *s_out)
```

```python
import functools

import numpy as np
import jax
import jax.numpy as jnp
from jax import lax
from jax.experimental import pallas as pl
from jax.experimental.pallas import tpu as pltpu

F32 = jnp.float32
BF16 = jnp.bfloat16
I32 = jnp.int32
I16 = jnp.int16

D_MODEL = 1024
CHUNK = 64
Q_BLOCK = 128
EPS = 1e-6

RW_HEADS = 8
RW_HEAD_DIM = 64
RW_WIDTH = RW_HEADS * RW_HEAD_DIM
RW_DECAY_LORA = 64
RW_ICLR_LORA = 64
RW_GATE_LORA = 128
RW_LORA = RW_DECAY_LORA + RW_ICLR_LORA + RW_GATE_LORA
RW_COLS = 3 * RW_WIDTH + RW_LORA
RW_GN_EPS = 64e-5

SA_HEADS = 8
SA_KV_HEADS = 2
SA_HEAD_DIM = 64
SA_WIDTH = SA_HEADS * SA_HEAD_DIM
SA_KV_WIDTH = SA_KV_HEADS * SA_HEAD_DIM
IDX_HEADS = 4
IDX_DIM = 64
IDX_WIDTH = IDX_HEADS * IDX_DIM
TOPK_MAX = 256
ROPE_THETA = 500000.0
ROPE_DIM = SA_HEAD_DIM // 4
INDEX_SCALE = (IDX_DIM ** -0.5) * (IDX_HEADS ** -0.5)
SA_COLS = SA_WIDTH + 2 * SA_KV_WIDTH + IDX_WIDTH + IDX_DIM + IDX_HEADS

RET_HEADS = 4
RET_HEAD_DIM = 128
RET_WIDTH = RET_HEADS * RET_HEAD_DIM
RET_ROPE_BASE = 10000.0
RET_COLS = 4 * RET_WIDTH

N_BRANCH = 3
GATE_COLS = N_BRANCH * D_MODEL
IN_COLS = RW_COLS + SA_COLS + RET_COLS + GATE_COLS
D_FF = 4 * D_MODEL

LANES = 128
TILE_ELEMS = 64 * 8 * LANES

P_GATE = 0
P_RET = P_GATE + GATE_COLS
P_RKV = P_RET + RET_COLS
P_SAQ = P_RKV + 3 * RW_WIDTH
P_LORA = P_SAQ + SA_WIDTH
P_QI = P_LORA + RW_LORA
P_SAK = P_QI + IDX_WIDTH
P_SAV = P_SAK + SA_KV_WIDTH
P_KIWI = P_SAV + SA_KV_WIDTH
P_COLS = 8192
INT_MIN = -2 ** 31
NEG_BIG = -1e30
LOG2_E = 1.4426950408889634
VMEM_LIMIT = 56 * 1024 * 1024


def _bf(x):
    return x.astype(BF16)


def _dot(a, b):
    return jnp.dot(a, b, preferred_element_type=F32)


def _dot_nt(a, b):
    return lax.dot_general(a, b, (((1,), (1,)), ((), ())), preferred_element_type=F32)


def _dot_tn(a, b):
    return lax.dot_general(a, b, (((0,), (0,)), ((), ())), preferred_element_type=F32)


def _dot_split2(a_exact, x):
    hi = _bf(x)
    lo = _bf(x - hi.astype(F32))
    return _dot(a_exact, hi) + _dot(a_exact, lo)


def _params(sem):
    return pltpu.CompilerParams(dimension_semantics=sem, vmem_limit_bytes=VMEM_LIMIT)


def _in_proj_kernel(x_ref, g_ref, w_ref, o_ref, h_ref):
    @pl.when(pl.program_id(1) == 0)
    def _():
        x = x_ref[...]
        ms = jnp.mean(x * x, axis=-1, keepdims=True)
        h_ref[...] = _bf(x * lax.rsqrt(ms + EPS) * g_ref[...])

    o_ref[...] = _dot(h_ref[...], w_ref[0]).astype(o_ref.dtype)


def _in_proj(x2d, g, w_all, layer):
    T = x2d.shape[0]
    tm = min(1024, T)
    tn = 2048
    return pl.pallas_call(
        _in_proj_kernel,
        grid=(T // tm, P_COLS // tn),
        in_specs=[pl.BlockSpec((tm, D_MODEL), lambda i, j: (i, 0)),
                  pl.BlockSpec((1, D_MODEL), lambda i, j: (0, 0)),
                  pl.BlockSpec((1, D_MODEL, tn), lambda i, j: (layer, 0, j))],
        out_specs=pl.BlockSpec((tm, tn), lambda i, j: (i, j)),
        out_shape=jax.ShapeDtypeStruct((T, P_COLS), BF16),
        scratch_shapes=[pltpu.VMEM((tm, D_MODEL), BF16)],
        compiler_params=_params(("parallel", "arbitrary")),
        name="in_proj",
    )(x2d, g, w_all)


KIWI_COLS = IDX_DIM + IDX_HEADS
TAIL0 = RW_COLS + SA_COLS - KIWI_COLS
TAIL_W = (IN_COLS - TAIL0 + LANES - 1) // LANES * LANES


def _w_prep_kernel(w_ref, o_ref):
    x = w_ref[0]
    lane = lax.broadcasted_iota(I32, (x.shape[0], LANES), 1)

    def put(off, v):
        o_ref[0, :, off:off + v.shape[1]] = _bf(v)

    nslab = TAIL_W // LANES
    part = IN_COLS - TAIL0 - (nslab - 1) * LANES
    slabs = [x[:, TAIL0 + s * LANES:TAIL0 + (s + 1) * LANES] for s in range(nslab - 1)]
    slabs.append(jnp.concatenate([x[:, IN_COLS - part:], jnp.zeros((x.shape[0], LANES - part), F32)], axis=1))
    rolled = [pltpu.roll(t, LANES - KIWI_COLS, 1) for t in slabs]
    for s in range((RET_COLS + GATE_COLS) // LANES):
        v = jnp.where(lane < LANES - KIWI_COLS, rolled[s], rolled[s + 1])
        put((P_RET if s < RET_COLS // LANES else P_GATE - RET_COLS) + s * LANES, v)
    sa = RW_COLS
    put(P_RKV, x[:, 0:3 * RW_WIDTH])
    put(P_LORA, x[:, 3 * RW_WIDTH:RW_COLS])
    put(P_SAQ, x[:, sa:sa + SA_WIDTH])
    put(P_SAK, x[:, sa + SA_WIDTH:sa + SA_WIDTH + SA_KV_WIDTH])
    put(P_SAV, x[:, sa + SA_WIDTH + SA_KV_WIDTH:sa + SA_WIDTH + 2 * SA_KV_WIDTH])
    put(P_QI, x[:, sa + SA_WIDTH + 2 * SA_KV_WIDTH:TAIL0])
    put(P_KIWI, jnp.where(lane < KIWI_COLS, x[:, TAIL0:TAIL0 + LANES], 0.0))
    put(P_KIWI + LANES, jnp.zeros((x.shape[0], P_COLS - P_KIWI - LANES), F32))


def _w_prep(w_in):
    depth, d, _ = w_in.shape
    tm = 256
    return pl.pallas_call(
        _w_prep_kernel,
        grid=(depth, d // tm),
        in_specs=[pl.BlockSpec((1, tm, IN_COLS), lambda l, i: (l, i, 0))],
        out_specs=pl.BlockSpec((1, tm, P_COLS), lambda l, i: (l, i, 0)),
        out_shape=jax.ShapeDtypeStruct((depth, d, P_COLS), BF16),
        compiler_params=_params(("parallel", "parallel")),
        name="w_prep",
    )(w_in)


def _rot_slab(x, c, s_lo, s_hi, shift):
    return x * c + pltpu.roll(x, LANES - shift, 1) * s_lo + pltpu.roll(x, shift, 1) * s_hi


def _sa_prep_kernel(q_ref, qi_ref, k_ref, kw_ref, c_ref, s1_ref, s2_ref, ck_ref, s1k_ref, s2k_ref,
                    qo_ref, qio_ref, ko_ref, kwo_ref):
    c, s1, s2 = c_ref[...], s1_ref[...], s2_ref[...]
    half = ROPE_DIM // 2
    scale = SA_HEAD_DIM ** -0.5 * LOG2_E
    for s in range(SA_WIDTH // LANES):
        sl = slice(s * LANES, (s + 1) * LANES)
        qo_ref[:, sl] = _rot_slab(q_ref[:, sl].astype(F32), c, s1, s2, half) * scale
    for s in range(IDX_WIDTH // LANES):
        sl = slice(s * LANES, (s + 1) * LANES)
        qio_ref[:, sl] = _rot_slab(qi_ref[:, sl].astype(F32), c, s1, s2, half)
    ko_ref[...] = _rot_slab(k_ref[...].astype(F32), c, s1, s2, half)
    kwo_ref[...] = _rot_slab(kw_ref[...].astype(F32), ck_ref[...], s1k_ref[...], s2k_ref[...], half)


def _sa_prep(P, L, tabs):
    T = P.shape[0]
    tm = min(512, L)
    nl = L // tm
    tab_spec = pl.BlockSpec((tm, LANES), lambda i: (i % nl, 0))
    return pl.pallas_call(
        _sa_prep_kernel,
        grid=(T // tm,),
        in_specs=[pl.BlockSpec((tm, SA_WIDTH), lambda i: (i, P_SAQ // SA_WIDTH)),
                  pl.BlockSpec((tm, IDX_WIDTH), lambda i: (i, P_QI // IDX_WIDTH)),
                  pl.BlockSpec((tm, LANES), lambda i: (i, P_SAK // LANES)),
                  pl.BlockSpec((tm, LANES), lambda i: (i, P_KIWI // LANES))] + [tab_spec] * 6,
        out_specs=[pl.BlockSpec((tm, SA_WIDTH), lambda i: (i, 0)),
                   pl.BlockSpec((tm, IDX_WIDTH), lambda i: (i, 0)),
                   pl.BlockSpec((tm, LANES), lambda i: (i, 0)),
                   pl.BlockSpec((tm, LANES), lambda i: (i, 0))],
        out_shape=[jax.ShapeDtypeStruct((T, SA_WIDTH), F32),
                   jax.ShapeDtypeStruct((T, IDX_WIDTH), F32),
                   jax.ShapeDtypeStruct((T, LANES), F32),
                   jax.ShapeDtypeStruct((T, LANES), F32)],
        compiler_params=_params(("parallel",)),
        name="sa_prep",
    )(P, P, P, P, *tabs)


def _sa_tables(pos):
    half = ROPE_DIM // 2
    freqs = 1.0 / (ROPE_THETA ** (jnp.arange(0, ROPE_DIM, 2, dtype=F32) / ROPE_DIM))
    ang = pos.astype(F32)[:, None] * freqs[None, :]
    cos, sin = jnp.cos(ang), jnp.sin(ang)
    n = pos.shape[0]
    pad = SA_HEAD_DIM - ROPE_DIM
    c_head = jnp.concatenate([cos, cos, jnp.ones((n, pad), F32)], axis=1)
    s1_head = jnp.concatenate([-sin, jnp.zeros((n, half + pad), F32)], axis=1)
    s2_head = jnp.concatenate([jnp.zeros((n, half), F32), sin, jnp.zeros((n, pad), F32)], axis=1)
    one, zero = jnp.ones((n, SA_HEAD_DIM), F32), jnp.zeros((n, SA_HEAD_DIM), F32)
    two = lambda t: jnp.concatenate([t, t], axis=1)
    return (two(c_head), two(s1_head), two(s2_head),
            jnp.concatenate([c_head, one], axis=1), jnp.concatenate([s1_head, zero], axis=1),
            jnp.concatenate([s2_head, zero], axis=1))


def _softplus(u):
    return jnp.maximum(u, 0.0) + jnp.log(1.0 + jnp.exp(-jnp.abs(u)))


def _rwkv_kernel(r_ref, k_ref, v_ref, lo_ref, shr_ref, shk_ref, shv_ref, shlo_ref, s0_ref,
                 mur_ref, muk_ref, muv_ref, mulo_ref, w0_ref, w2_ref, a0_ref, a2_ref, g2_ref,
                 kk_ref, ka_ref, rk_ref, lnx_ref, e_ref, tin_ref, tsuf_ref,
                 y_ref, so_ref,
                 cr_ref, ck_ref, cv_ref, clo_ref, s_ref,
                 kap_ref, rt_ref, bh_ref, kh_ref, bt_ref, kt_ref, vv_ref, gc_ref, yy_ref, rr_ref,
                 ac_ref, cc_ref):
    c = pl.program_id(1)
    TT = r_ref.shape[0]
    nch = TT // CHUNK
    N = RW_HEAD_DIM

    @pl.when(c == 0)
    def _():
        cr_ref[0:1, :] = shr_ref[0]
        ck_ref[0:1, :] = shk_ref[0]
        cv_ref[0:1, :] = shv_ref[0]
        clo_ref[0:1, :] = shlo_ref[0]
        s_ref[...] = s0_ref[0]

    def lerp(p_ref, carry_ref, mu_ref):
        p = p_ref[...].astype(F32)
        rolled = pltpu.roll(p, 1, 0)
        row0 = lax.broadcasted_iota(I32, p.shape, 0) == 0
        prev = jnp.where(row0, carry_ref[0:1, :], rolled)
        carry_ref[0:1, :] = p[TT - 1:TT, :]
        return p + (prev - p) * mu_ref[...]

    xr = lerp(r_ref, cr_ref, mur_ref)
    xk = lerp(k_ref, ck_ref, muk_ref)
    xv = lerp(v_ref, cv_ref, muv_ref)
    xlo = lerp(lo_ref, clo_ref, mulo_ref)
    xw = xlo[:, :RW_DECAY_LORA]
    xa = xlo[:, RW_DECAY_LORA:RW_DECAY_LORA + RW_ICLR_LORA]
    xg = xlo[:, RW_DECAY_LORA + RW_ICLR_LORA:]

    z = w0_ref[...] + _dot(_bf(jnp.tanh(xw)), w2_ref[...])
    w = -_softplus(-z) - 0.5
    ld = -jnp.exp(w)
    a = jax.nn.sigmoid(a0_ref[...] + _dot(_bf(xa), a2_ref[...]))
    gate = _dot(_bf(jax.nn.sigmoid(xg)), g2_ref[...])
    e_blk = e_ref[...]
    kk = xk * kk_ref[...]
    kk = kk / jnp.maximum(jnp.sqrt(_dot(_bf(kk * kk), e_blk)), 1e-12)
    k2 = xk * (1.0 + (a - 1.0) * ka_ref[...])
    bb = kk * a
    bonus = _dot(_bf(xr * k2 * rk_ref[...]), e_blk) * xv

    lin = _dot_split2(tin_ref[...], ld)
    lsuf = _dot_split2(tsuf_ref[...], ld)
    e_in = jnp.exp(lin)
    e_ninv = jnp.exp(-lin)
    e_suf = jnp.exp(lsuf)
    kap_ref[...] = kk * jnp.exp(lin - ld)
    rt_ref[...] = xr * e_in
    bh_ref[...] = bb * e_ninv
    kh_ref[...] = k2 * e_ninv
    bt_ref[...] = bb * e_suf
    kt_ref[...] = k2 * e_suf
    vv_ref[...] = xv
    gc_ref[...] = jnp.exp(lin + lsuf)

    ri = lax.broadcasted_iota(I32, (CHUNK, CHUNK), 0)
    ci = lax.broadcasted_iota(I32, (CHUNK, CHUNK), 1)
    strict = ri > ci
    incl = ri >= ci
    eye = (ri == ci).astype(F32)
    heads = range(RW_HEADS)
    hsl = [slice(h * N, (h + 1) * N) for h in heads]

    per_it = next(n for n in (4, 2, 1) if nch % n == 0)

    def coef_body(it, carry):
        chs = [it * per_it + t for t in range(per_it)]
        pairs = [(t, h) for t in range(per_it) for h in heads]
        rows = [pl.ds(pl.multiple_of(ch * CHUNK, CHUNK), CHUNK) for ch in chs]
        ld = lambda ref, p: ref[rows[p[0]], hsl[p[1]]]
        kap = [ld(kap_ref, p) for p in pairs]
        rt = [ld(rt_ref, p) for p in pairs]
        vh = [ld(vv_ref, p) for p in pairs]
        idx = range(len(pairs))
        gmat = [_dot_nt(_bf(jnp.concatenate([kap[n], rt[n]], axis=0)),
                        _bf(jnp.concatenate([ld(bh_ref, pairs[n]), ld(kh_ref, pairs[n])], axis=0)))
                for n in idx]
        n_ab = [jnp.where(strict, g[:CHUNK, :CHUNK], 0.0) for g in gmat]
        m_rb = [jnp.where(incl, g[CHUNK:, :CHUNK], 0.0) for g in gmat]
        m_v = [_bf(jnp.concatenate([jnp.where(strict, g[:CHUNK, CHUNK:], 0.0),
                                    jnp.where(incl, g[CHUNK:, CHUNK:], 0.0)], axis=0)) for g in gmat]
        mv = [_dot(m_v[n], _bf(vh[n])) for n in idx]
        x_inv = [eye - t for t in n_ab]
        pw = n_ab
        for _ in range(5):
            pwb = [_bf(p) for p in pw]
            pw = [_dot(p, p) for p in pwb]
            x_inv = [x + _dot(_bf(x), _bf(p)) for x, p in zip(x_inv, pw)]
        w = [_dot(_bf(x_inv[n]), _bf(jnp.concatenate([kap[n], mv[n][:CHUNK]], axis=1))) for n in idx]
        wb = [_bf(t) for t in w]
        ry = [jnp.concatenate([rt[n], mv[n][CHUNK:]], axis=1) - _dot(_bf(m_rb[n]), wb[n]) for n in idx]
        dmat = [_dot_tn(wb[n], _bf(ld(bt_ref, pairs[n]))) for n in idx]
        vtk = [_dot_tn(_bf(vh[n]), _bf(ld(kt_ref, pairs[n]))) for n in idx]
        for n, (t, h) in enumerate(pairs):
            rr_ref[rows[t], hsl[h]] = ry[n][:, :N]
            yy_ref[rows[t], hsl[h]] = ry[n][:, N:]
            ac_ref[chs[t], h] = -dmat[n][:N]
            cc_ref[chs[t], h] = vtk[n] - dmat[n][N:]
        return carry

    lax.fori_loop(0, nch // per_it, coef_body, 0)

    def state_body(ch, carry):
        rows = pl.ds(pl.multiple_of(ch * CHUNK, CHUNK), CHUNK)
        s_old = [s_ref[h] for h in heads]
        sb = [_bf(t) for t in s_old]
        s_new = [_dot(sb[h], _bf(ac_ref[ch, h])) for h in heads]
        y_c = [_dot_nt(_bf(rr_ref[rows, hsl[h]]), sb[h]) for h in heads]
        for h in heads:
            gch = gc_ref[pl.ds(pl.multiple_of(ch * CHUNK, CHUNK), 1), hsl[h]]
            s_ref[h] = s_old[h] * gch + s_new[h] + cc_ref[ch, h]
            yy_ref[rows, hsl[h]] = yy_ref[rows, hsl[h]] + y_c[h]
        return carry

    lax.fori_loop(0, nch, state_body, 0)

    y = yy_ref[...]
    mean = _dot(_bf(y), e_blk) * (1.0 / N)
    d = y - mean
    var = _dot(_bf(d * d), e_blk) * (1.0 / N)
    yn = d * lax.rsqrt(var + RW_GN_EPS) * lnx_ref[...]
    y_ref[...] = (yn + bonus) * gate

    @pl.when(c == pl.num_programs(1) - 1)
    def _():
        so_ref[0] = s_ref[...]


def _rwkv(P, B, L, shift_prev, s0, lp):
    T = P.shape[0]
    TT = min(256, L)
    nt = L // TT
    W = RW_WIDTH
    row = lambda t: t.reshape(1, -1).astype(F32)
    mu = lp['rwkv_mu']
    sh = shift_prev.astype(F32)
    pieces = lambda t: (t[..., 0:W], t[..., W:2 * W], t[..., 2 * W:3 * W], t[..., 3 * W:])
    mu_r, mu_k, mu_v, mu_lo = [row(t) for t in pieces(mu)]
    sh_r, sh_k, sh_v, sh_lo = [t.reshape(B, 1, -1) for t in pieces(sh)]
    hid = jnp.arange(W) // RW_HEAD_DIM
    e_blk = (hid[:, None] == hid[None, :]).astype(BF16)
    ti = jnp.arange(TT)
    same = (ti[:, None] // CHUNK) == (ti[None, :] // CHUNK)
    tri_in = (same & (ti[None, :] <= ti[:, None])).astype(BF16)
    tri_suf = (same & (ti[None, :] > ti[:, None])).astype(BF16)

    tok = lambda w, blk: pl.BlockSpec((TT, w), lambda b, c: (b * nt + c, blk))
    full = lambda shape: pl.BlockSpec(shape, lambda b, c: (0,) * len(shape))
    shs = lambda w: pl.BlockSpec((1, 1, w), lambda b, c: (b, 0, 0))
    st_spec = pl.BlockSpec((1, RW_HEADS, RW_HEAD_DIM, RW_HEAD_DIM), lambda b, c: (b, 0, 0, 0))
    big = lambda: pltpu.VMEM((TT, W), F32)
    y, s_out = pl.pallas_call(
        _rwkv_kernel,
        grid=(B, nt),
        in_specs=[tok(W, P_RKV // W), tok(W, P_RKV // W + 1), tok(W, P_RKV // W + 2),
                  tok(RW_LORA, P_LORA // RW_LORA),
                  shs(W), shs(W), shs(W), shs(RW_LORA), st_spec,
                  full((1, W)), full((1, W)), full((1, W)), full((1, RW_LORA)),
                  full((1, W)), full((RW_DECAY_LORA, W)), full((1, W)), full((RW_ICLR_LORA, W)),
                  full((RW_GATE_LORA, W)), full((1, W)), full((1, W)), full((1, W)), full((1, W)),
                  full((W, W)), full((TT, TT)), full((TT, TT))],
        out_specs=[pl.BlockSpec((TT, W), lambda b, c: (b * nt + c, 0)), st_spec],
        out_shape=[jax.ShapeDtypeStruct((T, W), F32),
                   jax.ShapeDtypeStruct((B, RW_HEADS, RW_HEAD_DIM, RW_HEAD_DIM), F32)],
        scratch_shapes=[pltpu.VMEM((8, W), F32), pltpu.VMEM((8, W), F32), pltpu.VMEM((8, W), F32),
                        pltpu.VMEM((8, RW_LORA), F32),
                        pltpu.VMEM((RW_HEADS, RW_HEAD_DIM, RW_HEAD_DIM), F32)] + [big() for _ in range(10)]
        + [pltpu.VMEM((TT // CHUNK, RW_HEADS, RW_HEAD_DIM, RW_HEAD_DIM), F32) for _ in range(2)],
        compiler_params=_params(("parallel", "arbitrary")),
        name="rwkv",
    )(P, P, P, P, sh_r, sh_k, sh_v, sh_lo, s0.astype(F32),
      mu_r, mu_k, mu_v, mu_lo, row(lp['rwkv_w0']), _bf(lp['rwkv_w2']), row(lp['rwkv_a0']),
      _bf(lp['rwkv_a2']), _bf(lp['rwkv_g2']), row(lp['rwkv_k_k']), row(lp['rwkv_k_a']),
      row(lp['rwkv_r_k']), row(lp['rwkv_lnx_g']), e_blk, tri_in, tri_suf)
    return y, s_out


ONES_ROWS = 16
BOUND_SLACK = 1.05
SAFE_BOUND = 60.0


def _dsa_kernel(q_ref, qi_ref, kw_ref, k_ref, v_ref, ki_ref, y_ref,
                kb_ref, vt_ref, kib_ref, kmax_ref, keys_ref, hi_ref, lo_ref, bias_ref, pstar_ref,
                *, past, lk_real, topk, kc):
    i = pl.program_id(1)
    qb = q_ref.shape[1]
    lk = k_ref.shape[1]
    kf = float(topk)
    HD = SA_HEAD_DIM

    ones_sq = jnp.ones((LANES, LANES), BF16)

    def head_slab(x, odd):
        lane = lax.broadcasted_iota(I32, x.shape, 1)
        return jnp.where(lane < HD, pltpu.roll(x, HD, 1) if odd else x, 0.0)

    @pl.when(i == 0)
    def _():
        kib_ref[...] = _bf(ki_ref[0])
        k = k_ref[0]
        lane = lax.broadcasted_iota(I32, k.shape, 1)
        for c in range(SA_KV_HEADS):
            kc_b = _bf(head_slab(k, c == 1))
            kb_ref[c] = jnp.where(lane == HD, jnp.ones_like(kc_b), kc_b)
            kf32 = kc_b.astype(F32)
            n2 = _dot(_bf(kf32 * kf32), ones_sq)
            kmax_ref[c] = jnp.broadcast_to(jnp.max(n2, axis=0, keepdims=True), (8, LANES))
        vt = v_ref[0].astype(F32).T
        for c in range(SA_KV_HEADS):
            vt_ref[c, 0:HD, :] = _bf(vt[c * HD:(c + 1) * HD, :])
            vt_ref[c, HD:HD + ONES_ROWS, :] = jnp.ones((ONES_ROWS, lk), BF16)

    qpos = past + i * qb + lax.broadcasted_iota(I32, (1, qb), 1)
    limit = jnp.minimum((qpos // CHUNK + 1) * CHUNK, lk_real)
    hi = jnp.minimum(past + (i + 1) * qb, lk_real)
    nk = lax.shift_right_logical(hi + (2 * kc - 1), kc.bit_length())
    sub_idx = lax.broadcasted_iota(I32, (kc, qb), 0)
    kwt = kw_ref[0].T
    wi_h = [kwt[IDX_DIM + h:IDX_DIM + h + 1, :] * INDEX_SCALE for h in range(IDX_HEADS)]
    qi = qi_ref[0]
    qi_h = [_bf(qi[:, h * IDX_DIM:(h + 1) * IDX_DIM]) for h in range(IDX_HEADS)]

    def chunk(j):
        return pl.ds(pl.multiple_of(j * kc, kc), kc)

    def pair(t):
        return (2 * t, 2 * t + 1)

    def score_body(t, carry):
        js = pair(t)
        dots = [[_dot_nt(kib_ref[chunk(j), 0:IDX_DIM], qi_h[h]) for h in range(IDX_HEADS)] for j in js]
        for j, d in zip(js, dots):
            s = jnp.zeros((kc, qb), F32)
            for h in range(IDX_HEADS):
                s = s + jnp.maximum(d[h], 0.0) * wi_h[h]
            s = jnp.where(s == 0.0, 0.0, s)
            bits = lax.bitcast_convert_type(s, I32)
            key = bits ^ ((bits >> 31) & 0x7FFFFFFF)
            key = jnp.where((sub_idx + j * kc) < limit, key, INT_MIN)
            keys_ref[chunk(j), :] = key
            hi_ref[chunk(j), :] = (key >> 16).astype(I16)
        return carry

    lax.fori_loop(0, nk, score_body, 0)

    def fold(m, rows):
        parts = [m[r * rows:(r + 1) * rows] for r in range(kc // rows)]
        while len(parts) > 1:
            parts = [a + b for a, b in zip(parts[0::2], parts[1::2])]
        return parts[0]

    max_trips = lk // (2 * kc)

    def short_trips(body, carry):
        base = jnp.int32(0)
        p = 1 << (max_trips.bit_length() - 1)
        while p:
            take = (nk & p) != 0

            def run(c, base=base, p=p):
                for s in range(p):
                    c = body(base + s, c)
                return c

            carry = lax.cond(take, run, lambda c: c, carry)
            base = base + jnp.where(take, p, 0)
            p //= 2
        return carry

    def count(pred):
        def body(t, acc):
            for j in pair(t):
                acc = acc + fold(jnp.where(pred(keys_ref[chunk(j), :], j), 1.0, 0.0), 8)
            return acc

        return jnp.sum(short_trips(body, jnp.zeros((8, qb), F32)), axis=0, keepdims=True)

    def rows16(x8):
        return jnp.concatenate([x8, x8], axis=0).astype(I16)

    def slab_count(x, thr16, strict):
        parts = []
        for r in range(kc // 16):
            xs = x[r * 16:(r + 1) * 16]
            hit = xs > thr16 if strict else xs >= thr16
            parts.append(jnp.where(hit, jnp.int16(1), jnp.int16(0)))
        while len(parts) > 1:
            parts = [a + b for a, b in zip(parts[0::2], parts[1::2])]
        return parts[0]

    def sublane_allsum(c):
        for sh in (4, 2, 1):
            c = c + pltpu.roll(c, sh, 0)
        return c

    def count16(ref, thr16, strict=False):
        def body(t, acc):
            for j in pair(t):
                acc = acc + slab_count(ref[chunk(j), :], thr16, strict)
            return acc

        w = pltpu.bitcast(short_trips(body, jnp.zeros((16, qb), I16)), I32)
        s = (w & 0xFFFF) + lax.shift_right_logical(w, 16)
        tiles = [s[:, n * LANES:(n + 1) * LANES] for n in range(qb // LANES)]
        out = []
        for a, b in zip(tiles[0::2], tiles[1::2]):
            c = sublane_allsum(a + lax.shift_left(b, 16))
            out += [c & 0xFFFF, lax.shift_right_logical(c, 16)]
        if len(tiles) % 2:
            out.append(sublane_allsum(tiles[-1]))
        return jnp.concatenate(out, axis=1) if len(out) > 1 else out[0]

    def kth_bits16(ref, need, known=0, nbits=16):
        def bit_body(it, u):
            cand = u | lax.shift_left(jnp.int32(1), nbits - 1 - it)
            return jnp.where(count16(ref, rows16(cand - 32768)) >= need, cand, u)

        return lax.fori_loop(0, nbits, bit_body, jnp.full((8, qb), known, I32))

    t_hi8 = kth_bits16(hi_ref, topk) - 32768
    t_hi16 = rows16(t_hi8)
    need_lo = topk - count16(hi_ref, t_hi16, strict=True)
    t_hi_tile = jnp.concatenate([t_hi16] * (kc // 16), axis=0)

    def lo_body(t, carry):
        for j in pair(t):
            lo = ((keys_ref[chunk(j), :] & 0xFFFF) - 32768).astype(I16)
            lo_ref[chunk(j), :] = jnp.where(hi_ref[chunk(j), :] == t_hi_tile, lo, jnp.int16(-32768))
        return carry

    short_trips(lo_body, jnp.int32(0))
    tau = (t_hi8 * 65536 + kth_bits16(lo_ref, need_lo))[0:1, :]
    cnt_ge = count(lambda kj, j: kj >= tau)
    cnt_gt = count(lambda kj, j: kj > tau)
    need = kf - cnt_gt
    excess = jnp.logical_and(cnt_ge > kf, tau != INT_MIN)
    p_default = jnp.where(tau == INT_MIN, -1, 2 ** 30).astype(I32)
    pstar_ref[...] = jnp.broadcast_to(p_default, pstar_ref.shape)

    assert lk <= 32768

    @pl.when(jnp.max(jnp.where(excess, 1.0, 0.0)) > 0.0)
    def _():
        def rev_body(t, carry):
            for j in pair(t):
                rev = jnp.where(keys_ref[chunk(j), :] == tau, 32767 - (sub_idx + j * kc), -32768)
                lo_ref[chunk(j), :] = rev.astype(I16)
            return carry

        short_trips(rev_body, jnp.int32(0))
        need8 = jnp.broadcast_to(need.astype(I32), (8, qb))
        nb = max(1, int(lk - 1).bit_length())
        known = (0xFFFF >> nb) << nb
        p = (65535 - kth_bits16(lo_ref, need8, known, nb))[0:1, :]
        pstar_ref[...] = jnp.broadcast_to(jnp.where(excess, p, p_default), pstar_ref.shape)

    pstar = pstar_ref[0:1, :]

    def bias_body(t, carry):
        for j in pair(t):
            kj = keys_ref[chunk(j), :]
            sel = jnp.logical_or(kj > tau, jnp.logical_and(kj == tau, (sub_idx + j * kc) <= pstar))
            bias_ref[chunk(j), :] = _bf(jnp.where(sel, 0.0, NEG_BIG))
        return carry

    short_trips(bias_body, jnp.int32(0))

    q = q_ref[0]
    group = SA_HEADS // SA_KV_HEADS
    lane_q = lax.broadcasted_iota(I32, (qb, LANES), 1)
    q_heads, bounds = [], []
    for h in range(SA_HEADS):
        qh = _bf(head_slab(q[:, (h // 2) * LANES:(h // 2 + 1) * LANES], h % 2 == 1))
        qf = qh.astype(F32)
        qn2 = _dot(_bf(qf * qf), ones_sq)
        q_heads.append(qh)
        bounds.append(jnp.sqrt(qn2 * kmax_ref[h // group, 0:1, :]) * BOUND_SLACK)
    worst = bounds[0]
    for b in bounds[1:]:
        worst = jnp.maximum(worst, b)
    safe = jnp.max(worst) <= SAFE_BOUND

    zero_acc = jnp.zeros((HD + ONES_ROWS, qb), F32)

    def vt_chunk(c, j):
        return vt_ref[c, :, pl.ds(pl.multiple_of(j * kc, kc), kc)]

    @pl.when(safe)
    def _():
        hs = range(SA_HEADS)
        qa = [jnp.where(lane_q == HD, _bf(-bounds[h]), q_heads[h]) for h in hs]

        def body(t, accs):
            js = pair(t)
            logit = [[_dot_nt(kb_ref[h // group, chunk(j), :], qa[h]) for h in hs] for j in js]
            bias = [bias_ref[chunk(j), :].astype(F32) for j in js]
            p = [[_bf(jnp.exp2(lg + b)) for lg in row] for b, row in zip(bias, logit)]
            pv = [[_dot(vt_chunk(h // group, j), row[h]) for h in hs] for j, row in zip(js, p)]
            return tuple(accs[h] + pv[0][h] + pv[1][h] for h in hs)

        accs = lax.fori_loop(0, nk, body, (zero_acc,) * SA_HEADS)
        for h in hs:
            y_ref[0, h * HD:(h + 1) * HD, :] = accs[h][:HD] / accs[h][HD:HD + 1]

    @pl.when(jnp.logical_not(safe))
    def _():
        for h in range(SA_HEADS):
            c = h // group

            def logit(j):
                return _dot_nt(kb_ref[c, chunk(j), :], q_heads[h]) + bias_ref[chunk(j), :].astype(F32)

            def max_body(t, m):
                for j in pair(t):
                    m = jnp.maximum(m, jnp.max(logit(j), axis=0, keepdims=True))
                return m

            m = lax.fori_loop(0, nk, max_body, jnp.full((1, qb), NEG_BIG, F32))

            def sum_body(t, acc):
                for j in pair(t):
                    acc = acc + _dot(vt_chunk(c, j), _bf(jnp.exp2(logit(j) - m)))
                return acc

            acc = lax.fori_loop(0, nk, sum_body, zero_acc)
            y_ref[0, h * HD:(h + 1) * HD, :] = acc[:HD] / acc[HD:HD + 1]


def _dsa_call(q3, qi3, kw3, k_all, v_all, ki_all, v_blk, *, lk, past, lk_real, topk, qb, kc):
    B, n_q = q3.shape[:2]
    kern = functools.partial(_dsa_kernel, past=past, lk_real=lk_real, topk=topk, kc=kc)
    once = dict(pipeline_mode=pl.Buffered(1)) if n_q // qb > 1 else {}
    return pl.pallas_call(
        kern,
        grid=(B, n_q // qb),
        in_specs=[pl.BlockSpec((1, qb, SA_WIDTH), lambda b, i: (b, i, 0)),
                  pl.BlockSpec((1, qb, IDX_WIDTH), lambda b, i: (b, i, 0)),
                  pl.BlockSpec((1, qb, LANES), lambda b, i: (b, i, 0)),
                  pl.BlockSpec((1, lk, LANES), lambda b, i: (b, 0, 0), **once),
                  pl.BlockSpec((1, lk, LANES), lambda b, i: (b, 0, v_blk), **once),
                  pl.BlockSpec((1, lk, LANES), lambda b, i: (b, 0, 0), **once)],
        out_specs=pl.BlockSpec((1, SA_WIDTH, qb), lambda b, i: (b, 0, i)),
        out_shape=jax.ShapeDtypeStruct((B, SA_WIDTH, n_q), F32),
        scratch_shapes=[pltpu.VMEM((SA_KV_HEADS, lk, LANES), BF16),
                        pltpu.VMEM((SA_KV_HEADS, SA_HEAD_DIM + ONES_ROWS, lk), BF16),
                        pltpu.VMEM((lk, LANES), BF16),
                        pltpu.VMEM((SA_KV_HEADS, 8, LANES), F32),
                        pltpu.VMEM((lk, qb), I32), pltpu.VMEM((lk, qb), I16), pltpu.VMEM((lk, qb), I16),
                        pltpu.VMEM((lk, qb), BF16),
                        pltpu.VMEM((8, qb), I32)],
        compiler_params=_params(("parallel", "arbitrary")),
        name="dsa",
    )(q3, qi3, kw3, k_all, v_all, ki_all)


def _round_up(x, m):
    return (x + m - 1) // m * m


def _dsa(P, B, L, past, q_rot, qi_rot, k_rot, kw_rot, k_past, v_past, ik_past):
    lk_real = past + L
    topk = min(TOPK_MAX, lk_real // 4)
    qb = 2 * LANES if (past == 0 and L % (2 * LANES) == 0) else LANES
    lq = _round_up(L, qb)
    qpad = lambda t: t if lq == L else jnp.pad(t, ((0, 0), (0, lq - L), (0, 0)))
    q3 = qpad(q_rot.reshape(B, L, SA_WIDTH))
    qi3 = qpad(qi_rot.reshape(B, L, IDX_WIDTH))
    kw3 = kw_rot.reshape(B, L, LANES)
    k3 = k_rot.reshape(B, L, LANES)
    common = dict(past=past, lk_real=lk_real, topk=topk, qb=qb)
    kc = TILE_ELEMS // qb // 2 if qb == LANES else TILE_ELEMS // qb
    if past == 0:
        while L % (2 * kc):
            kc //= 2
        P3 = P.reshape(B, L, P_COLS)
        return _dsa_call(q3, qi3, kw3, k3, P3, kw3, P_SAV // LANES, lk=L, kc=kc, **common)
    lk = _round_up(lk_real, 2 * kc)
    zpad = jnp.zeros((B, lk - lk_real, LANES), F32)
    v_new = P.reshape(B, L, P_COLS)[:, :, P_SAV:P_SAV + LANES].astype(F32)
    ik_p = jnp.concatenate([ik_past.astype(F32), jnp.zeros((B, past, LANES - IDX_DIM), F32)], axis=2)
    k_all = jnp.concatenate([k_past.reshape(B, past, LANES).astype(F32), k3, zpad], axis=1)
    v_all = jnp.concatenate([v_past.reshape(B, past, LANES).astype(F32), v_new, zpad], axis=1)
    ki_all = jnp.concatenate([ik_p, kw3, zpad], axis=1)
    y = _dsa_call(q3, qi3, qpad(kw3), k_all, v_all, ki_all, 0, lk=lk, kc=kc, **common)
    return y[:, :, :L]


def _ret_kernel(q_ref, k_ref, v_ref, g_ref, cos_ref, sin_ref, dm_ref, qd_ref, kd_ref, gc_ref, s0_ref,
                y_ref, so_ref, s_ref):
    c = pl.program_id(1)
    D = RET_HEAD_DIM
    cs = dm_ref.shape[1]
    nch = q_ref.shape[0] // cs
    heads = range(RET_HEADS)
    hsl = [slice(h * D, (h + 1) * D) for h in heads]
    rows = [slice(ch * cs, (ch + 1) * cs) for ch in range(nch)]
    prob = [(ch, h) for ch in range(nch) for h in heads]

    @pl.when(c == 0)
    def _():
        s_ref[...] = s0_ref[0]

    cos, sin = cos_ref[...], sin_ref[...]
    qb_, kb_, kd_, vb_ = [], [], [], []
    for h in heads:
        q = q_ref[:, hsl[h]].astype(F32)
        k = k_ref[:, hsl[h]].astype(F32)
        k = (k * cos + pltpu.roll(k, D // 2, 1) * sin) * (D ** -0.5)
        qb_.append(_bf(q * cos + pltpu.roll(q, D // 2, 1) * sin))
        kb_.append(_bf(k))
        kd_.append([_bf(k[r] * kd_ref[:, hsl[h]]) for r in rows])
        vb_.append(_bf(v_ref[:, hsl[h]]))
    scores = {(ch, h): _dot_nt(qb_[h][rows[ch]], kb_[h][rows[ch]]) * dm_ref[h] for ch, h in prob}
    ktv = {(ch, h): _dot_tn(kd_[h][ch], vb_[h][rows[ch]]) for ch, h in prob}
    intra = {(ch, h): _dot(_bf(scores[ch, h]), vb_[h][rows[ch]]) for ch, h in prob}
    s_at = {}
    for h in heads:
        s = s_ref[h]
        for ch in range(nch):
            s_at[ch, h] = _bf(s)
            s = s * gc_ref[:, hsl[h]] + ktv[ch, h]
        s_ref[h] = s
    cross = {(ch, h): _dot(qb_[h][rows[ch]], s_at[ch, h]) for ch, h in prob}
    for ch in range(nch):
        outs = []
        for h in heads:
            o = intra[ch, h] + cross[ch, h] * qd_ref[:, hsl[h]]
            o = o * lax.rsqrt(jnp.mean(o * o, axis=-1, keepdims=True) + EPS)
            outs.append(jax.nn.silu(g_ref[rows[ch], hsl[h]].astype(F32)) * o)
        y_ref[rows[ch], :] = jnp.concatenate(outs, axis=1)

    @pl.when(c == pl.num_programs(1) - 1)
    def _():
        so_ref[0] = s_ref[...]


def _retention(P, B, L, pos, s0):
    T = P.shape[0]
    c = min(CHUNK, L)
    nc = L // c
    D = RET_HEAD_DIM
    freqs = 1.0 / (RET_ROPE_BASE ** jnp.linspace(0.0, 1.0, D // 2, dtype=F32))
    ang = pos.astype(F32)[:, None] * freqs[None, :]
    cos = jnp.concatenate([jnp.cos(ang)] * 2, axis=1)
    sin = jnp.concatenate([-jnp.sin(ang), jnp.sin(ang)], axis=1)
    log_gamma = jnp.log1p(-jnp.exp2(-5.0 - jnp.arange(RET_HEADS, dtype=F32)))
    idx = jnp.arange(c, dtype=F32)
    dmask = jnp.exp(jnp.abs(idx[:, None] - idx[None, :])[None] * log_gamma[:, None, None])
    lanes = lambda t: jnp.repeat(t, D, axis=1)
    qdec = lanes(jnp.exp((idx[:, None] + 1.0) * log_gamma[None, :]))
    kdec = lanes(jnp.exp((c - 1.0 - idx)[:, None] * log_gamma[None, :]))
    gchunk = lanes(jnp.exp(c * log_gamma)[None, :])

    W = RET_WIDTH
    tt = min(4 * c, L)
    nt = L // tt
    tok = lambda blk: pl.BlockSpec((tt, W), lambda b, i: (b * nt + i, blk))
    full = lambda shape: pl.BlockSpec(shape, lambda b, i: (0,) * len(shape))
    st_spec = pl.BlockSpec((1, RET_HEADS, D, D), lambda b, i: (b, 0, 0, 0))
    y, s_out = pl.pallas_call(
        _ret_kernel,
        grid=(B, nt),
        in_specs=[tok(P_RET // W), tok(P_RET // W + 1), tok(P_RET // W + 2), tok(P_RET // W + 3),
                  pl.BlockSpec((tt, D), lambda b, i: (i, 0)), pl.BlockSpec((tt, D), lambda b, i: (i, 0)),
                  full((RET_HEADS, c, c)), full((c, W)), full((c, W)), full((1, W)), st_spec],
        out_specs=[pl.BlockSpec((tt, W), lambda b, i: (b * nt + i, 0)), st_spec],
        out_shape=[jax.ShapeDtypeStruct((T, W), F32), jax.ShapeDtypeStruct((B, RET_HEADS, D, D), F32)],
        scratch_shapes=[pltpu.VMEM((RET_HEADS, D, D), F32)],
        compiler_params=_params(("parallel", "arbitrary")),
        name="retention",
    )(P, P, P, P, cos, sin, dmask, qdec, kdec, gchunk, s0.astype(F32))
    return y, s_out


def _merge_kernel(x_ref, g0_ref, g1_ref, g2_ref, yr_ref, ys_ref, yt_ref, wr_ref, ws_ref, wt_ref, wo_ref,
                  o_ref):
    gate = lambda g_ref: jax.nn.sigmoid(g_ref[...].astype(F32))
    m = (gate(g0_ref) * _dot(_bf(yr_ref[...]), wr_ref[...])
         + gate(g1_ref) * _dot_tn(_bf(ys_ref[0]), ws_ref[...])
         + gate(g2_ref) * _dot(_bf(yt_ref[...]), wt_ref[...]))
    o_ref[...] = x_ref[...] + _dot(_bf(m), wo_ref[...])


def _merge(x2d, P, y_rw, y_sa_t, y_ret, w_rw, w_sa, w_ret, w_o):
    T = x2d.shape[0]
    L = y_sa_t.shape[2]
    tm = min(512, L)
    nl = L // tm
    D = D_MODEL
    tok = lambda w, blk: pl.BlockSpec((tm, w), lambda i: (i, blk))
    full = lambda shape: pl.BlockSpec(shape, lambda i: (0, 0))
    return pl.pallas_call(
        _merge_kernel,
        grid=(T // tm,),
        in_specs=[tok(D, 0), tok(D, 0), tok(D, 1), tok(D, 2), tok(RW_WIDTH, 0),
                  pl.BlockSpec((1, SA_WIDTH, tm), lambda i: (i // nl, 0, i % nl)),
                  tok(RET_WIDTH, 0), full((RW_WIDTH, D)), full((SA_WIDTH, D)), full((RET_WIDTH, D)),
                  full((D, D))],
        out_specs=tok(D, 0),
        out_shape=jax.ShapeDtypeStruct((T, D), F32),
        compiler_params=_params(("parallel",)),
        name="merge",
    )(x2d, P, P, P, y_rw, y_sa_t, y_ret, w_rw, w_sa, w_ret, w_o)


def _mlp_kernel(x_ref, g_ref, wu_ref, wd_ref, gf_ref, o_ref, h_ref, acc_ref, *, final_norm):
    j = pl.program_id(1)

    @pl.when(j == 0)
    def _():
        x = x_ref[...]
        ms = jnp.mean(x * x, axis=-1, keepdims=True)
        h_ref[...] = _bf(x * lax.rsqrt(ms + EPS) * g_ref[...])
        acc_ref[...] = jnp.zeros_like(acc_ref)

    u = jnp.maximum(_dot(h_ref[...], wu_ref[...]), 0.0)
    acc_ref[...] += _dot(_bf(u * u), wd_ref[...])

    @pl.when(j == pl.num_programs(1) - 1)
    def _():
        xn = x_ref[...] + acc_ref[...]
        if final_norm:
            ms = jnp.mean(xn * xn, axis=-1, keepdims=True)
            xn = xn * lax.rsqrt(ms + EPS) * gf_ref[...]
        o_ref[...] = xn


def _mlp(x2d, g, w_up, w_down, g_final, final_norm):
    T = x2d.shape[0]
    tm = min(1024, T)
    tf = 2048
    D = D_MODEL
    return pl.pallas_call(
        functools.partial(_mlp_kernel, final_norm=final_norm),
        grid=(T // tm, D_FF // tf),
        in_specs=[pl.BlockSpec((tm, D), lambda i, j: (i, 0)),
                  pl.BlockSpec((1, D), lambda i, j: (0, 0)),
                  pl.BlockSpec((D, tf), lambda i, j: (0, j)),
                  pl.BlockSpec((tf, D), lambda i, j: (j, 0)),
                  pl.BlockSpec((1, D), lambda i, j: (0, 0))],
        out_specs=pl.BlockSpec((tm, D), lambda i, j: (i, 0)),
        out_shape=jax.ShapeDtypeStruct((T, D), F32),
        scratch_shapes=[pltpu.VMEM((tm, D), BF16), pltpu.VMEM((tm, D), F32)],
        compiler_params=_params(("parallel", "arbitrary")),
        name="mlp",
    )(x2d, g, w_up, w_down, g_final)


def _layer(x2d, B, L, past, caches, lp, wts, g_final, final_norm):
    k_past, v_past, ik_past, s_rw, shift_rw, s_ret = caches
    pos = past + jnp.arange(L, dtype=jnp.int32)
    row = lambda t: t.reshape(1, -1).astype(F32)
    P = _in_proj(x2d, row(lp['norm1_g']), wts['w_in'], wts['layer'])
    q_rot, qi_rot, k_rot, kw_rot = _sa_prep(P, L, _sa_tables(pos))
    y_rw, s_rw_new = _rwkv(P, B, L, shift_rw, s_rw, lp)
    y_sa = _dsa(P, B, L, past, q_rot, qi_rot, k_rot, kw_rot, k_past, v_past, ik_past)
    y_ret, s_ret_new = _retention(P, B, L, pos, s_ret)
    x2d = _merge(x2d, P, y_rw, y_sa, y_ret, wts['w_br_rwkv'], wts['w_br_dsa'], wts['w_br_ret'], wts['w_o'])
    x2d = _mlp(x2d, row(lp['norm2_g']), wts['w_up'], wts['w_down'], g_final, final_norm)
    P3 = P.reshape(B, L, P_COLS)
    last = P3[:, L - 1].astype(F32)
    shift_new = jnp.concatenate([last[:, P_RKV:P_RKV + 3 * RW_WIDTH], last[:, P_LORA:P_LORA + RW_LORA]], axis=1)
    k_new = k_rot.reshape(B, L, SA_KV_HEADS, SA_HEAD_DIM)
    v_new = P3[:, :, P_SAV:P_SAV + SA_KV_WIDTH].astype(F32).reshape(B, L, SA_KV_HEADS, SA_HEAD_DIM)
    ik_new = kw_rot.reshape(B, L, LANES)[:, :, :IDX_DIM]
    return x2d, (k_new, v_new, ik_new, s_rw_new, shift_new, s_ret_new)


def kernel(x_prompt, x_sample, cache_dsa_k, cache_dsa_v, cache_dsa_ik, state_rwkv, state_rwkv_shift, state_ret, norm1_g, w_in, rwkv_mu, rwkv_w0, rwkv_w2, rwkv_a0, rwkv_a2, rwkv_g2, rwkv_k_k, rwkv_k_a, rwkv_r_k, rwkv_lnx_g, w_br_rwkv, w_br_dsa, w_br_ret, w_o, norm2_g, w_up, w_down, final_norm_g):
    params = {
        'norm1_g': norm1_g, 'rwkv_mu': rwkv_mu, 'rwkv_w0': rwkv_w0, 'rwkv_w2': rwkv_w2,
        'rwkv_a0': rwkv_a0, 'rwkv_a2': rwkv_a2, 'rwkv_g2': rwkv_g2, 'rwkv_k_k': rwkv_k_k,
        'rwkv_k_a': rwkv_k_a, 'rwkv_r_k': rwkv_r_k, 'rwkv_lnx_g': rwkv_lnx_g, 'norm2_g': norm2_g,
    }
    depth = w_in.shape[0]
    Bp, Lp, D = x_prompt.shape
    Bs, Ls, _ = x_sample.shape
    past_s = cache_dsa_k.shape[2]
    xp = x_prompt.reshape(Bp * Lp, D).astype(F32)
    xs = x_sample.reshape(Bs * Ls, D).astype(F32)
    g_final = final_norm_g.reshape(1, D).astype(F32)
    zero_p = (None, None, None,
              jnp.zeros((Bp, RW_HEADS, RW_HEAD_DIM, RW_HEAD_DIM), F32), jnp.zeros((Bp, RW_COLS), F32),
              jnp.zeros((Bp, RET_HEADS, RET_HEAD_DIM, RET_HEAD_DIM), F32))
    p_states = [[] for _ in range(6)]
    s_states = [[] for _ in range(6)]
    w_all = _w_prep(w_in)
    for i in range(depth):
        lp = {name: arr[i] for name, arr in params.items()}
        wts = {'w_in': w_all, 'layer': i, 'w_br_rwkv': _bf(w_br_rwkv[i]), 'w_br_dsa': _bf(w_br_dsa[i]),
               'w_br_ret': _bf(w_br_ret[i]), 'w_o': _bf(w_o[i]), 'w_up': _bf(w_up[i]), 'w_down': _bf(w_down[i])}
        final = i == depth - 1
        cache_s = (cache_dsa_k[i], cache_dsa_v[i], cache_dsa_ik[i], state_rwkv[i], state_rwkv_shift[i],
                   state_ret[i])
        xp, new_p = _layer(xp, Bp, Lp, 0, zero_p, lp, wts, g_final, final)
        xs, new_s = _layer(xs, Bs, Ls, past_s, cache_s, lp, wts, g_final, final)
        for j in range(6):
            p_states[j].append(new_p[j])
            s_states[j].append(new_s[j])
    y_prompt = xp.reshape(Bp, Lp, D)
    y_sample = xs.reshape(Bs, Ls, D)
    p_out = [jnp.stack(t, axis=0) for t in p_states]
    s_out = [jnp.stack(t, axis=0) for t in s_states]
    return (y_prompt, y_sample, *p_out, *s_out)
```

```python
import functools

import numpy as np
import jax
import jax.numpy as jnp
from jax import lax
from jax.experimental import pallas as pl
from jax.experimental.pallas import tpu as pltpu

F32 = jnp.float32
BF16 = jnp.bfloat16
I32 = jnp.int32
I16 = jnp.int16

D_MODEL = 1024
CHUNK = 64
Q_BLOCK = 128
EPS = 1e-6

RW_HEADS = 8
RW_HEAD_DIM = 64
RW_WIDTH = RW_HEADS * RW_HEAD_DIM
RW_DECAY_LORA = 64
RW_ICLR_LORA = 64
RW_GATE_LORA = 128
RW_LORA = RW_DECAY_LORA + RW_ICLR_LORA + RW_GATE_LORA
RW_COLS = 3 * RW_WIDTH + RW_LORA
RW_GN_EPS = 64e-5

SA_HEADS = 8
SA_KV_HEADS = 2
SA_HEAD_DIM = 64
SA_WIDTH = SA_HEADS * SA_HEAD_DIM
SA_KV_WIDTH = SA_KV_HEADS * SA_HEAD_DIM
IDX_HEADS = 4
IDX_DIM = 64
IDX_WIDTH = IDX_HEADS * IDX_DIM
TOPK_MAX = 256
ROPE_THETA = 500000.0
ROPE_DIM = SA_HEAD_DIM // 4
INDEX_SCALE = (IDX_DIM ** -0.5) * (IDX_HEADS ** -0.5)
SA_COLS = SA_WIDTH + 2 * SA_KV_WIDTH + IDX_WIDTH + IDX_DIM + IDX_HEADS

RET_HEADS = 4
RET_HEAD_DIM = 128
RET_WIDTH = RET_HEADS * RET_HEAD_DIM
RET_ROPE_BASE = 10000.0
RET_COLS = 4 * RET_WIDTH

N_BRANCH = 3
GATE_COLS = N_BRANCH * D_MODEL
IN_COLS = RW_COLS + SA_COLS + RET_COLS + GATE_COLS
D_FF = 4 * D_MODEL

LANES = 128
TILE_ELEMS = 64 * 8 * LANES

P_GATE = 0
P_RET = P_GATE + GATE_COLS
P_RKV = P_RET + RET_COLS
P_SAQ = P_RKV + 3 * RW_WIDTH
P_LORA = P_SAQ + SA_WIDTH
P_QI = P_LORA + RW_LORA
P_SAK = P_QI + IDX_WIDTH
P_SAV = P_SAK + SA_KV_WIDTH
P_KIWI = P_SAV + SA_KV_WIDTH
P_COLS = 8192
INT_MIN = -2 ** 31
NEG_BIG = -1e30
LOG2_E = 1.4426950408889634
VMEM_LIMIT = 56 * 1024 * 1024


def _bf(x):
    return x.astype(BF16)


def _dot(a, b):
    return jnp.dot(a, b, preferred_element_type=F32)


def _dot_nt(a, b):
    return lax.dot_general(a, b, (((1,), (1,)), ((), ())), preferred_element_type=F32)


def _dot_tn(a, b):
    return lax.dot_general(a, b, (((0,), (0,)), ((), ())), preferred_element_type=F32)


def _dot_split2(a_exact, x):
    hi = _bf(x)
    lo = _bf(x - hi.astype(F32))
    return _dot(a_exact, hi) + _dot(a_exact, lo)


def _params(sem):
    return pltpu.CompilerParams(dimension_semantics=sem, vmem_limit_bytes=VMEM_LIMIT)


def _in_proj_kernel(x_ref, g_ref, w_ref, o_ref, h_ref):
    @pl.when(pl.program_id(1) == 0)
    def _():
        x = x_ref[...]
        ms = jnp.mean(x * x, axis=-1, keepdims=True)
        h_ref[...] = _bf(x * lax.rsqrt(ms + EPS) * g_ref[...])

    o_ref[...] = _dot(h_ref[...], w_ref[0]).astype(o_ref.dtype)


def _in_proj(x2d, g, w_all, layer):
    T = x2d.shape[0]
    tm = min(1024, T)
    tn = 4096
    return pl.pallas_call(
        _in_proj_kernel,
        grid=(T // tm, P_COLS // tn),
        in_specs=[pl.BlockSpec((tm, D_MODEL), lambda i, j: (i, 0)),
                  pl.BlockSpec((1, D_MODEL), lambda i, j: (0, 0)),
                  pl.BlockSpec((1, D_MODEL, tn), lambda i, j: (layer, 0, j))],
        out_specs=pl.BlockSpec((tm, tn), lambda i, j: (i, j)),
        out_shape=jax.ShapeDtypeStruct((T, P_COLS), BF16),
        scratch_shapes=[pltpu.VMEM((tm, D_MODEL), BF16)],
        compiler_params=_params(("parallel", "arbitrary")),
        name="in_proj",
    )(x2d, g, w_all)


KIWI_COLS = IDX_DIM + IDX_HEADS
TAIL0 = RW_COLS + SA_COLS - KIWI_COLS
TAIL_W = (IN_COLS - TAIL0 + LANES - 1) // LANES * LANES


def _w_prep_kernel(w_ref, o_ref):
    x = w_ref[0]
    lane = lax.broadcasted_iota(I32, (x.shape[0], LANES), 1)

    def put(off, v):
        o_ref[0, :, off:off + v.shape[1]] = _bf(v)

    nslab = TAIL_W // LANES
    part = IN_COLS - TAIL0 - (nslab - 1) * LANES
    slabs = [x[:, TAIL0 + s * LANES:TAIL0 + (s + 1) * LANES] for s in range(nslab - 1)]
    slabs.append(jnp.concatenate([x[:, IN_COLS - part:], jnp.zeros((x.shape[0], LANES - part), F32)], axis=1))
    rolled = [pltpu.roll(t, LANES - KIWI_COLS, 1) for t in slabs]
    for s in range((RET_COLS + GATE_COLS) // LANES):
        v = jnp.where(lane < LANES - KIWI_COLS, rolled[s], rolled[s + 1])
        put((P_RET if s < RET_COLS // LANES else P_GATE - RET_COLS) + s * LANES, v)
    sa = RW_COLS
    put(P_RKV, x[:, 0:3 * RW_WIDTH])
    put(P_LORA, x[:, 3 * RW_WIDTH:RW_COLS])
    put(P_SAQ, x[:, sa:sa + SA_WIDTH])
    put(P_SAK, x[:, sa + SA_WIDTH:sa + SA_WIDTH + SA_KV_WIDTH])
    put(P_SAV, x[:, sa + SA_WIDTH + SA_KV_WIDTH:sa + SA_WIDTH + 2 * SA_KV_WIDTH])
    put(P_QI, x[:, sa + SA_WIDTH + 2 * SA_KV_WIDTH:TAIL0])
    put(P_KIWI, jnp.where(lane < KIWI_COLS, x[:, TAIL0:TAIL0 + LANES], 0.0))
    put(P_KIWI + LANES, jnp.zeros((x.shape[0], P_COLS - P_KIWI - LANES), F32))


def _w_prep(w_in):
    depth, d, _ = w_in.shape
    tm = 256
    return pl.pallas_call(
        _w_prep_kernel,
        grid=(depth, d // tm),
        in_specs=[pl.BlockSpec((1, tm, IN_COLS), lambda l, i: (l, i, 0))],
        out_specs=pl.BlockSpec((1, tm, P_COLS), lambda l, i: (l, i, 0)),
        out_shape=jax.ShapeDtypeStruct((depth, d, P_COLS), BF16),
        compiler_params=_params(("parallel", "parallel")),
        name="w_prep",
    )(w_in)


def _rot_slab(x, c, s_lo, s_hi, shift):
    return x * c + pltpu.roll(x, LANES - shift, 1) * s_lo + pltpu.roll(x, shift, 1) * s_hi


def _sa_prep_kernel(q_ref, qi_ref, k_ref, kw_ref, c_ref, s1_ref, s2_ref, ck_ref, s1k_ref, s2k_ref,
                    qo_ref, qio_ref, ko_ref, kwo_ref):
    c, s1, s2 = c_ref[...], s1_ref[...], s2_ref[...]
    half = ROPE_DIM // 2
    scale = SA_HEAD_DIM ** -0.5 * LOG2_E
    for s in range(SA_WIDTH // LANES):
        sl = slice(s * LANES, (s + 1) * LANES)
        qo_ref[:, sl] = _rot_slab(q_ref[:, sl].astype(F32), c, s1, s2, half) * scale
    for s in range(IDX_WIDTH // LANES):
        sl = slice(s * LANES, (s + 1) * LANES)
        qio_ref[:, sl] = _rot_slab(qi_ref[:, sl].astype(F32), c, s1, s2, half)
    ko_ref[...] = _rot_slab(k_ref[...].astype(F32), c, s1, s2, half)
    kwo_ref[...] = _rot_slab(kw_ref[...].astype(F32), ck_ref[...], s1k_ref[...], s2k_ref[...], half)


def _sa_prep(P, L, tabs):
    T = P.shape[0]
    tm = min(512, L)
    nl = L // tm
    tab_spec = pl.BlockSpec((tm, LANES), lambda i: (i % nl, 0))
    return pl.pallas_call(
        _sa_prep_kernel,
        grid=(T // tm,),
        in_specs=[pl.BlockSpec((tm, SA_WIDTH), lambda i: (i, P_SAQ // SA_WIDTH)),
                  pl.BlockSpec((tm, IDX_WIDTH), lambda i: (i, P_QI // IDX_WIDTH)),
                  pl.BlockSpec((tm, LANES), lambda i: (i, P_SAK // LANES)),
                  pl.BlockSpec((tm, LANES), lambda i: (i, P_KIWI // LANES))] + [tab_spec] * 6,
        out_specs=[pl.BlockSpec((tm, SA_WIDTH), lambda i: (i, 0)),
                   pl.BlockSpec((tm, IDX_WIDTH), lambda i: (i, 0)),
                   pl.BlockSpec((tm, LANES), lambda i: (i, 0)),
                   pl.BlockSpec((tm, LANES), lambda i: (i, 0))],
        out_shape=[jax.ShapeDtypeStruct((T, SA_WIDTH), F32),
                   jax.ShapeDtypeStruct((T, IDX_WIDTH), F32),
                   jax.ShapeDtypeStruct((T, LANES), F32),
                   jax.ShapeDtypeStruct((T, LANES), F32)],
        compiler_params=_params(("parallel",)),
        name="sa_prep",
    )(P, P, P, P, *tabs)


def _sa_tables(pos):
    half = ROPE_DIM // 2
    freqs = 1.0 / (ROPE_THETA ** (jnp.arange(0, ROPE_DIM, 2, dtype=F32) / ROPE_DIM))
    ang = pos.astype(F32)[:, None] * freqs[None, :]
    cos, sin = jnp.cos(ang), jnp.sin(ang)
    n = pos.shape[0]
    pad = SA_HEAD_DIM - ROPE_DIM
    c_head = jnp.concatenate([cos, cos, jnp.ones((n, pad), F32)], axis=1)
    s1_head = jnp.concatenate([-sin, jnp.zeros((n, half + pad), F32)], axis=1)
    s2_head = jnp.concatenate([jnp.zeros((n, half), F32), sin, jnp.zeros((n, pad), F32)], axis=1)
    one, zero = jnp.ones((n, SA_HEAD_DIM), F32), jnp.zeros((n, SA_HEAD_DIM), F32)
    two = lambda t: jnp.concatenate([t, t], axis=1)
    return (two(c_head), two(s1_head), two(s2_head),
            jnp.concatenate([c_head, one], axis=1), jnp.concatenate([s1_head, zero], axis=1),
            jnp.concatenate([s2_head, zero], axis=1))


def _softplus(u):
    return jnp.maximum(u, 0.0) + jnp.log(1.0 + jnp.exp(-jnp.abs(u)))


def _rwkv_kernel(r_ref, k_ref, v_ref, lo_ref, shr_ref, shk_ref, shv_ref, shlo_ref, s0_ref,
                 mur_ref, muk_ref, muv_ref, mulo_ref, w0_ref, w2_ref, a0_ref, a2_ref, g2_ref,
                 kk_ref, ka_ref, rk_ref, lnx_ref, e_ref, tin_ref, tsuf_ref,
                 y_ref, so_ref,
                 cr_ref, ck_ref, cv_ref, clo_ref, s_ref,
                 kap_ref, rt_ref, bh_ref, kh_ref, bt_ref, kt_ref, vv_ref, gc_ref, yy_ref, rr_ref,
                 ac_ref, cc_ref):
    c = pl.program_id(1)
    TT = r_ref.shape[0]
    nch = TT // CHUNK
    N = RW_HEAD_DIM

    @pl.when(c == 0)
    def _():
        cr_ref[0:1, :] = shr_ref[0]
        ck_ref[0:1, :] = shk_ref[0]
        cv_ref[0:1, :] = shv_ref[0]
        clo_ref[0:1, :] = shlo_ref[0]
        s_ref[...] = s0_ref[0]

    def lerp(p_ref, carry_ref, mu_ref):
        p = p_ref[...].astype(F32)
        rolled = pltpu.roll(p, 1, 0)
        row0 = lax.broadcasted_iota(I32, p.shape, 0) == 0
        prev = jnp.where(row0, carry_ref[0:1, :], rolled)
        carry_ref[0:1, :] = p[TT - 1:TT, :]
        return p + (prev - p) * mu_ref[...]

    xr = lerp(r_ref, cr_ref, mur_ref)
    xk = lerp(k_ref, ck_ref, muk_ref)
    xv = lerp(v_ref, cv_ref, muv_ref)
    xlo = lerp(lo_ref, clo_ref, mulo_ref)
    xw = xlo[:, :RW_DECAY_LORA]
    xa = xlo[:, RW_DECAY_LORA:RW_DECAY_LORA + RW_ICLR_LORA]
    xg = xlo[:, RW_DECAY_LORA + RW_ICLR_LORA:]

    z = w0_ref[...] + _dot(_bf(jnp.tanh(xw)), w2_ref[...])
    w = -_softplus(-z) - 0.5
    ld = -jnp.exp(w)
    a = jax.nn.sigmoid(a0_ref[...] + _dot(_bf(xa), a2_ref[...]))
    gate = _dot(_bf(jax.nn.sigmoid(xg)), g2_ref[...])
    e_blk = e_ref[...]
    kk = xk * kk_ref[...]
    kk = kk / jnp.maximum(jnp.sqrt(_dot(_bf(kk * kk), e_blk)), 1e-12)
    k2 = xk * (1.0 + (a - 1.0) * ka_ref[...])
    bb = kk * a
    bonus = _dot(_bf(xr * k2 * rk_ref[...]), e_blk) * xv

    lin = _dot_split2(tin_ref[...], ld)
    lsuf = _dot_split2(tsuf_ref[...], ld)
    e_in = jnp.exp(lin)
    e_ninv = jnp.exp(-lin)
    e_suf = jnp.exp(lsuf)
    kap_ref[...] = kk * jnp.exp(lin - ld)
    rt_ref[...] = xr * e_in
    bh_ref[...] = bb * e_ninv
    kh_ref[...] = k2 * e_ninv
    bt_ref[...] = bb * e_suf
    kt_ref[...] = k2 * e_suf
    vv_ref[...] = xv
    gc_ref[...] = jnp.exp(lin + lsuf)

    ri = lax.broadcasted_iota(I32, (CHUNK, CHUNK), 0)
    ci = lax.broadcasted_iota(I32, (CHUNK, CHUNK), 1)
    strict = ri > ci
    incl = ri >= ci
    eye = (ri == ci).astype(F32)
    heads = range(RW_HEADS)
    hsl = [slice(h * N, (h + 1) * N) for h in heads]

    per_it = next(n for n in (4, 2, 1) if nch % n == 0)

    def coef_body(it, carry):
        chs = [it * per_it + t for t in range(per_it)]
        pairs = [(t, h) for t in range(per_it) for h in heads]
        rows = [pl.ds(pl.multiple_of(ch * CHUNK, CHUNK), CHUNK) for ch in chs]
        ld = lambda ref, p: ref[rows[p[0]], hsl[p[1]]]
        kap = [ld(kap_ref, p) for p in pairs]
        rt = [ld(rt_ref, p) for p in pairs]
        vh = [ld(vv_ref, p) for p in pairs]
        idx = range(len(pairs))
        gmat = [_dot_nt(_bf(jnp.concatenate([kap[n], rt[n]], axis=0)),
                        _bf(jnp.concatenate([ld(bh_ref, pairs[n]), ld(kh_ref, pairs[n])], axis=0)))
                for n in idx]
        n_ab = [jnp.where(strict, g[:CHUNK, :CHUNK], 0.0) for g in gmat]
        m_rb = [jnp.where(incl, g[CHUNK:, :CHUNK], 0.0) for g in gmat]
        m_v = [_bf(jnp.concatenate([jnp.where(strict, g[:CHUNK, CHUNK:], 0.0),
                                    jnp.where(incl, g[CHUNK:, CHUNK:], 0.0)], axis=0)) for g in gmat]
        mv = [_dot(m_v[n], _bf(vh[n])) for n in idx]
        x_inv = [eye - t for t in n_ab]
        pw = n_ab
        for _ in range(5):
            pwb = [_bf(p) for p in pw]
            pw = [_dot(p, p) for p in pwb]
            x_inv = [x + _dot(_bf(x), _bf(p)) for x, p in zip(x_inv, pw)]
        w = [_dot(_bf(x_inv[n]), _bf(jnp.concatenate([kap[n], mv[n][:CHUNK]], axis=1))) for n in idx]
        wb = [_bf(t) for t in w]
        ry = [jnp.concatenate([rt[n], mv[n][CHUNK:]], axis=1) - _dot(_bf(m_rb[n]), wb[n]) for n in idx]
        dmat = [_dot_tn(wb[n], _bf(ld(bt_ref, pairs[n]))) for n in idx]
        vtk = [_dot_tn(_bf(vh[n]), _bf(ld(kt_ref, pairs[n]))) for n in idx]
        for n, (t, h) in enumerate(pairs):
            rr_ref[rows[t], hsl[h]] = ry[n][:, :N]
            yy_ref[rows[t], hsl[h]] = ry[n][:, N:]
            ac_ref[chs[t], h] = -dmat[n][:N]
            cc_ref[chs[t], h] = vtk[n] - dmat[n][N:]
        return carry

    lax.fori_loop(0, nch // per_it, coef_body, 0)

    def state_body(ch, carry):
        rows = pl.ds(pl.multiple_of(ch * CHUNK, CHUNK), CHUNK)
        s_old = [s_ref[h] for h in heads]
        sb = [_bf(t) for t in s_old]
        s_new = [_dot(sb[h], _bf(ac_ref[ch, h])) for h in heads]
        y_c = [_dot_nt(_bf(rr_ref[rows, hsl[h]]), sb[h]) for h in heads]
        for h in heads:
            gch = gc_ref[pl.ds(pl.multiple_of(ch * CHUNK, CHUNK), 1), hsl[h]]
            s_ref[h] = s_old[h] * gch + s_new[h] + cc_ref[ch, h]
            yy_ref[rows, hsl[h]] = yy_ref[rows, hsl[h]] + y_c[h]
        return carry

    lax.fori_loop(0, nch, state_body, 0)

    y = yy_ref[...]
    mean = _dot(_bf(y), e_blk) * (1.0 / N)
    d = y - mean
    var = _dot(_bf(d * d), e_blk) * (1.0 / N)
    yn = d * lax.rsqrt(var + RW_GN_EPS) * lnx_ref[...]
    y_ref[...] = (yn + bonus) * gate

    @pl.when(c == pl.num_programs(1) - 1)
    def _():
        so_ref[0] = s_ref[...]


def _rwkv(P, B, L, shift_prev, s0, lp):
    T = P.shape[0]
    TT = min(256, L)
    nt = L // TT
    W = RW_WIDTH
    row = lambda t: t.reshape(1, -1).astype(F32)
    mu = lp['rwkv_mu']
    sh = shift_prev.astype(F32)
    pieces = lambda t: (t[..., 0:W], t[..., W:2 * W], t[..., 2 * W:3 * W], t[..., 3 * W:])
    mu_r, mu_k, mu_v, mu_lo = [row(t) for t in pieces(mu)]
    sh_r, sh_k, sh_v, sh_lo = [t.reshape(B, 1, -1) for t in pieces(sh)]
    hid = jnp.arange(W) // RW_HEAD_DIM
    e_blk = (hid[:, None] == hid[None, :]).astype(BF16)
    ti = jnp.arange(TT)
    same = (ti[:, None] // CHUNK) == (ti[None, :] // CHUNK)
    tri_in = (same & (ti[None, :] <= ti[:, None])).astype(BF16)
    tri_suf = (same & (ti[None, :] > ti[:, None])).astype(BF16)

    tok = lambda w, blk: pl.BlockSpec((TT, w), lambda b, c: (b * nt + c, blk))
    full = lambda shape: pl.BlockSpec(shape, lambda b, c: (0,) * len(shape))
    shs = lambda w: pl.BlockSpec((1, 1, w), lambda b, c: (b, 0, 0))
    st_spec = pl.BlockSpec((1, RW_HEADS, RW_HEAD_DIM, RW_HEAD_DIM), lambda b, c: (b, 0, 0, 0))
    big = lambda: pltpu.VMEM((TT, W), F32)
    y, s_out = pl.pallas_call(
        _rwkv_kernel,
        grid=(B, nt),
        in_specs=[tok(W, P_RKV // W), tok(W, P_RKV // W + 1), tok(W, P_RKV // W + 2),
                  tok(RW_LORA, P_LORA // RW_LORA),
                  shs(W), shs(W), shs(W), shs(RW_LORA), st_spec,
                  full((1, W)), full((1, W)), full((1, W)), full((1, RW_LORA)),
                  full((1, W)), full((RW_DECAY_LORA, W)), full((1, W)), full((RW_ICLR_LORA, W)),
                  full((RW_GATE_LORA, W)), full((1, W)), full((1, W)), full((1, W)), full((1, W)),
                  full((W, W)), full((TT, TT)), full((TT, TT))],
        out_specs=[pl.BlockSpec((TT, W), lambda b, c: (b * nt + c, 0)), st_spec],
        out_shape=[jax.ShapeDtypeStruct((T, W), F32),
                   jax.ShapeDtypeStruct((B, RW_HEADS, RW_HEAD_DIM, RW_HEAD_DIM), F32)],
        scratch_shapes=[pltpu.VMEM((8, W), F32), pltpu.VMEM((8, W), F32), pltpu.VMEM((8, W), F32),
                        pltpu.VMEM((8, RW_LORA), F32),
                        pltpu.VMEM((RW_HEADS, RW_HEAD_DIM, RW_HEAD_DIM), F32)] + [big() for _ in range(10)]
        + [pltpu.VMEM((TT // CHUNK, RW_HEADS, RW_HEAD_DIM, RW_HEAD_DIM), F32) for _ in range(2)],
        compiler_params=_params(("parallel", "arbitrary")),
        name="rwkv",
    )(P, P, P, P, sh_r, sh_k, sh_v, sh_lo, s0.astype(F32),
      mu_r, mu_k, mu_v, mu_lo, row(lp['rwkv_w0']), _bf(lp['rwkv_w2']), row(lp['rwkv_a0']),
      _bf(lp['rwkv_a2']), _bf(lp['rwkv_g2']), row(lp['rwkv_k_k']), row(lp['rwkv_k_a']),
      row(lp['rwkv_r_k']), row(lp['rwkv_lnx_g']), e_blk, tri_in, tri_suf)
    return y, s_out


ONES_ROWS = 16
BOUND_SLACK = 1.05
SAFE_BOUND = 60.0


def _dsa_kernel(q_ref, qi_ref, kw_ref, k_ref, v_ref, ki_ref, y_ref,
                kb_ref, vt_ref, kib_ref, kmax_ref, keys_ref, hi_ref, lo_ref, bias_ref, pstar_ref,
                *, past, lk_real, topk, kc):
    i = pl.program_id(1)
    qb = q_ref.shape[1]
    lk = k_ref.shape[1]
    kf = float(topk)
    HD = SA_HEAD_DIM

    ones_sq = jnp.ones((LANES, LANES), BF16)

    def head_slab(x, odd):
        lane = lax.broadcasted_iota(I32, x.shape, 1)
        return jnp.where(lane < HD, pltpu.roll(x, HD, 1) if odd else x, 0.0)

    @pl.when(i == 0)
    def _():
        kib_ref[...] = _bf(ki_ref[0])
        k = k_ref[0]
        lane = lax.broadcasted_iota(I32, k.shape, 1)
        for c in range(SA_KV_HEADS):
            kc_b = _bf(head_slab(k, c == 1))
            kb_ref[c] = jnp.where(lane == HD, jnp.ones_like(kc_b), kc_b)
            kf32 = kc_b.astype(F32)
            n2 = _dot(_bf(kf32 * kf32), ones_sq)
            kmax_ref[c] = jnp.broadcast_to(jnp.max(n2, axis=0, keepdims=True), (8, LANES))
        vt = v_ref[0].astype(F32).T
        for c in range(SA_KV_HEADS):
            vt_ref[c, 0:HD, :] = _bf(vt[c * HD:(c + 1) * HD, :])
            vt_ref[c, HD:HD + ONES_ROWS, :] = jnp.ones((ONES_ROWS, lk), BF16)

    qpos = past + i * qb + lax.broadcasted_iota(I32, (1, qb), 1)
    limit = jnp.minimum((qpos // CHUNK + 1) * CHUNK, lk_real)
    hi = jnp.minimum(past + (i + 1) * qb, lk_real)
    nk = lax.shift_right_logical(hi + (2 * kc - 1), kc.bit_length())
    sub_idx = lax.broadcasted_iota(I32, (kc, qb), 0)
    kwt = kw_ref[0].T
    wi_h = [kwt[IDX_DIM + h:IDX_DIM + h + 1, :] * INDEX_SCALE for h in range(IDX_HEADS)]
    qi = qi_ref[0]
    qi_h = [_bf(qi[:, h * IDX_DIM:(h + 1) * IDX_DIM]) for h in range(IDX_HEADS)]

    def chunk(j):
        return pl.ds(pl.multiple_of(j * kc, kc), kc)

    def pair(t):
        return (2 * t, 2 * t + 1)

    def score_body(t, carry):
        js = pair(t)
        dots = [[_dot_nt(kib_ref[chunk(j), 0:IDX_DIM], qi_h[h]) for h in range(IDX_HEADS)] for j in js]
        for j, d in zip(js, dots):
            s = jnp.zeros((kc, qb), F32)
            for h in range(IDX_HEADS):
                s = s + jnp.maximum(d[h], 0.0) * wi_h[h]
            s = jnp.where(s == 0.0, 0.0, s)
            bits = lax.bitcast_convert_type(s, I32)
            key = bits ^ ((bits >> 31) & 0x7FFFFFFF)
            key = jnp.where((sub_idx + j * kc) < limit, key, INT_MIN)
            keys_ref[chunk(j), :] = key
            hi_ref[chunk(j), :] = (key >> 16).astype(I16)
        return carry

    lax.fori_loop(0, nk, score_body, 0)

    def fold(m, rows):
        parts = [m[r * rows:(r + 1) * rows] for r in range(kc // rows)]
        while len(parts) > 1:
            parts = [a + b for a, b in zip(parts[0::2], parts[1::2])]
        return parts[0]

    max_trips = lk // (2 * kc)

    def short_trips(body, carry):
        base = jnp.int32(0)
        p = 1 << (max_trips.bit_length() - 1)
        while p:
            take = (nk & p) != 0

            def run(c, base=base, p=p):
                for s in range(p):
                    c = body(base + s, c)
                return c

            carry = lax.cond(take, run, lambda c: c, carry)
            base = base + jnp.where(take, p, 0)
            p //= 2
        return carry

    def count(pred):
        def body(t, acc):
            for j in pair(t):
                acc = acc + fold(jnp.where(pred(keys_ref[chunk(j), :], j), 1.0, 0.0), 8)
            return acc

        return jnp.sum(short_trips(body, jnp.zeros((8, qb), F32)), axis=0, keepdims=True)

    def rows16(x8):
        return jnp.concatenate([x8, x8], axis=0).astype(I16)

    def slab_count(x, thr16, strict):
        parts = []
        for r in range(kc // 16):
            xs = x[r * 16:(r + 1) * 16]
            hit = xs > thr16 if strict else xs >= thr16
            parts.append(jnp.where(hit, jnp.int16(1), jnp.int16(0)))
        while len(parts) > 1:
            parts = [a + b for a, b in zip(parts[0::2], parts[1::2])]
        return parts[0]

    def sublane_allsum(c):
        for sh in (4, 2, 1):
            c = c + pltpu.roll(c, sh, 0)
        return c

    def count16(ref, thr16, strict=False):
        def body(t, acc):
            for j in pair(t):
                acc = acc + slab_count(ref[chunk(j), :], thr16, strict)
            return acc

        w = pltpu.bitcast(short_trips(body, jnp.zeros((16, qb), I16)), I32)
        s = (w & 0xFFFF) + lax.shift_right_logical(w, 16)
        tiles = [s[:, n * LANES:(n + 1) * LANES] for n in range(qb // LANES)]
        out = []
        for a, b in zip(tiles[0::2], tiles[1::2]):
            c = sublane_allsum(a + lax.shift_left(b, 16))
            out += [c & 0xFFFF, lax.shift_right_logical(c, 16)]
        if len(tiles) % 2:
            out.append(sublane_allsum(tiles[-1]))
        return jnp.concatenate(out, axis=1) if len(out) > 1 else out[0]

    def kth_bits16(ref, need, known=0, nbits=16):
        def bit_body(it, u):
            cand = u | lax.shift_left(jnp.int32(1), nbits - 1 - it)
            return jnp.where(count16(ref, rows16(cand - 32768)) >= need, cand, u)

        return lax.fori_loop(0, nbits, bit_body, jnp.full((8, qb), known, I32))

    t_hi8 = kth_bits16(hi_ref, topk) - 32768
    t_hi16 = rows16(t_hi8)
    need_lo = topk - count16(hi_ref, t_hi16, strict=True)
    t_hi_tile = jnp.concatenate([t_hi16] * (kc // 16), axis=0)

    def lo_body(t, carry):
        for j in pair(t):
            lo = ((keys_ref[chunk(j), :] & 0xFFFF) - 32768).astype(I16)
            lo_ref[chunk(j), :] = jnp.where(hi_ref[chunk(j), :] == t_hi_tile, lo, jnp.int16(-32768))
        return carry

    short_trips(lo_body, jnp.int32(0))
    tau = (t_hi8 * 65536 + kth_bits16(lo_ref, need_lo))[0:1, :]
    cnt_ge = count(lambda kj, j: kj >= tau)
    cnt_gt = count(lambda kj, j: kj > tau)
    need = kf - cnt_gt
    excess = jnp.logical_and(cnt_ge > kf, tau != INT_MIN)
    p_default = jnp.where(tau == INT_MIN, -1, 2 ** 30).astype(I32)
    pstar_ref[...] = jnp.broadcast_to(p_default, pstar_ref.shape)

    assert lk <= 32768

    @pl.when(jnp.max(jnp.where(excess, 1.0, 0.0)) > 0.0)
    def _():
        def rev_body(t, carry):
            for j in pair(t):
                rev = jnp.where(keys_ref[chunk(j), :] == tau, 32767 - (sub_idx + j * kc), -32768)
                lo_ref[chunk(j), :] = rev.astype(I16)
            return carry

        short_trips(rev_body, jnp.int32(0))
        need8 = jnp.broadcast_to(need.astype(I32), (8, qb))
        nb = max(1, int(lk - 1).bit_length())
        known = (0xFFFF >> nb) << nb
        p = (65535 - kth_bits16(lo_ref, need8, known, nb))[0:1, :]
        pstar_ref[...] = jnp.broadcast_to(jnp.where(excess, p, p_default), pstar_ref.shape)

    pstar = pstar_ref[0:1, :]

    def bias_body(t, carry):
        for j in pair(t):
            kj = keys_ref[chunk(j), :]
            sel = jnp.logical_or(kj > tau, jnp.logical_and(kj == tau, (sub_idx + j * kc) <= pstar))
            bias_ref[chunk(j), :] = _bf(jnp.where(sel, 0.0, NEG_BIG))
        return carry

    short_trips(bias_body, jnp.int32(0))

    q = q_ref[0]
    group = SA_HEADS // SA_KV_HEADS
    lane_q = lax.broadcasted_iota(I32, (qb, LANES), 1)
    q_heads, bounds = [], []
    for h in range(SA_HEADS):
        qh = _bf(head_slab(q[:, (h // 2) * LANES:(h // 2 + 1) * LANES], h % 2 == 1))
        qf = qh.astype(F32)
        qn2 = _dot(_bf(qf * qf), ones_sq)
        q_heads.append(qh)
        bounds.append(jnp.sqrt(qn2 * kmax_ref[h // group, 0:1, :]) * BOUND_SLACK)
    worst = bounds[0]
    for b in bounds[1:]:
        worst = jnp.maximum(worst, b)
    safe = jnp.max(worst) <= SAFE_BOUND

    zero_acc = jnp.zeros((HD + ONES_ROWS, qb), F32)

    def vt_chunk(c, j):
        return vt_ref[c, :, pl.ds(pl.multiple_of(j * kc, kc), kc)]

    @pl.when(safe)
    def _():
        hs = range(SA_HEADS)
        qa = [jnp.where(lane_q == HD, _bf(-bounds[h]), q_heads[h]) for h in hs]

        def body(t, accs):
            js = pair(t)
            logit = [[_dot_nt(kb_ref[h // group, chunk(j), :], qa[h]) for h in hs] for j in js]
            bias = [bias_ref[chunk(j), :].astype(F32) for j in js]
            p = [[_bf(jnp.exp2(lg + b)) for lg in row] for b, row in zip(bias, logit)]
            pv = [[_dot(vt_chunk(h // group, j), row[h]) for h in hs] for j, row in zip(js, p)]
            return tuple(accs[h] + pv[0][h] + pv[1][h] for h in hs)

        accs = lax.fori_loop(0, nk, body, (zero_acc,) * SA_HEADS)
        for h in hs:
            y_ref[0, h * HD:(h + 1) * HD, :] = accs[h][:HD] / accs[h][HD:HD + 1]

    @pl.when(jnp.logical_not(safe))
    def _():
        for h in range(SA_HEADS):
            c = h // group

            def logit(j):
                return _dot_nt(kb_ref[c, chunk(j), :], q_heads[h]) + bias_ref[chunk(j), :].astype(F32)

            def max_body(t, m):
                for j in pair(t):
                    m = jnp.maximum(m, jnp.max(logit(j), axis=0, keepdims=True))
                return m

            m = lax.fori_loop(0, nk, max_body, jnp.full((1, qb), NEG_BIG, F32))

            def sum_body(t, acc):
                for j in pair(t):
                    acc = acc + _dot(vt_chunk(c, j), _bf(jnp.exp2(logit(j) - m)))
                return acc

            acc = lax.fori_loop(0, nk, sum_body, zero_acc)
            y_ref[0, h * HD:(h + 1) * HD, :] = acc[:HD] / acc[HD:HD + 1]


def _dsa_call(q3, qi3, kw3, k_all, v_all, ki_all, v_blk, *, lk, past, lk_real, topk, qb, kc):
    B, n_q = q3.shape[:2]
    kern = functools.partial(_dsa_kernel, past=past, lk_real=lk_real, topk=topk, kc=kc)
    once = dict(pipeline_mode=pl.Buffered(1)) if n_q // qb > 1 else {}
    return pl.pallas_call(
        kern,
        grid=(B, n_q // qb),
        in_specs=[pl.BlockSpec((1, qb, SA_WIDTH), lambda b, i: (b, i, 0)),
                  pl.BlockSpec((1, qb, IDX_WIDTH), lambda b, i: (b, i, 0)),
                  pl.BlockSpec((1, qb, LANES), lambda b, i: (b, i, 0)),
                  pl.BlockSpec((1, lk, LANES), lambda b, i: (b, 0, 0), **once),
                  pl.BlockSpec((1, lk, LANES), lambda b, i: (b, 0, v_blk), **once),
                  pl.BlockSpec((1, lk, LANES), lambda b, i: (b, 0, 0), **once)],
        out_specs=pl.BlockSpec((1, SA_WIDTH, qb), lambda b, i: (b, 0, i)),
        out_shape=jax.ShapeDtypeStruct((B, SA_WIDTH, n_q), F32),
        scratch_shapes=[pltpu.VMEM((SA_KV_HEADS, lk, LANES), BF16),
                        pltpu.VMEM((SA_KV_HEADS, SA_HEAD_DIM + ONES_ROWS, lk), BF16),
                        pltpu.VMEM((lk, LANES), BF16),
                        pltpu.VMEM((SA_KV_HEADS, 8, LANES), F32),
                        pltpu.VMEM((lk, qb), I32), pltpu.VMEM((lk, qb), I16), pltpu.VMEM((lk, qb), I16),
                        pltpu.VMEM((lk, qb), BF16),
                        pltpu.VMEM((8, qb), I32)],
        compiler_params=_params(("parallel", "arbitrary")),
        name="dsa",
    )(q3, qi3, kw3, k_all, v_all, ki_all)


def _round_up(x, m):
    return (x + m - 1) // m * m


def _dsa(P, B, L, past, q_rot, qi_rot, k_rot, kw_rot, k_past, v_past, ik_past):
    lk_real = past + L
    topk = min(TOPK_MAX, lk_real // 4)
    qb = 2 * LANES if (past == 0 and L % (2 * LANES) == 0) else LANES
    lq = _round_up(L, qb)
    qpad = lambda t: t if lq == L else jnp.pad(t, ((0, 0), (0, lq - L), (0, 0)))
    q3 = qpad(q_rot.reshape(B, L, SA_WIDTH))
    qi3 = qpad(qi_rot.reshape(B, L, IDX_WIDTH))
    kw3 = kw_rot.reshape(B, L, LANES)
    k3 = k_rot.reshape(B, L, LANES)
    common = dict(past=past, lk_real=lk_real, topk=topk, qb=qb)
    kc = TILE_ELEMS // qb // 2 if qb == LANES else TILE_ELEMS // qb
    if past == 0:
        while L % (2 * kc):
            kc //= 2
        P3 = P.reshape(B, L, P_COLS)
        return _dsa_call(q3, qi3, kw3, k3, P3, kw3, P_SAV // LANES, lk=L, kc=kc, **common)
    lk = _round_up(lk_real, 2 * kc)
    zpad = jnp.zeros((B, lk - lk_real, LANES), F32)
    v_new = P.reshape(B, L, P_COLS)[:, :, P_SAV:P_SAV + LANES].astype(F32)
    ik_p = jnp.concatenate([ik_past.astype(F32), jnp.zeros((B, past, LANES - IDX_DIM), F32)], axis=2)
    k_all = jnp.concatenate([k_past.reshape(B, past, LANES).astype(F32), k3, zpad], axis=1)
    v_all = jnp.concatenate([v_past.reshape(B, past, LANES).astype(F32), v_new, zpad], axis=1)
    ki_all = jnp.concatenate([ik_p, kw3, zpad], axis=1)
    y = _dsa_call(q3, qi3, qpad(kw3), k_all, v_all, ki_all, 0, lk=lk, kc=kc, **common)
    return y[:, :, :L]


def _ret_kernel(q_ref, k_ref, v_ref, g_ref, cos_ref, sin_ref, dm_ref, qd_ref, kd_ref, gc_ref, s0_ref,
                y_ref, so_ref, s_ref):
    c = pl.program_id(1)
    D = RET_HEAD_DIM
    cs = dm_ref.shape[1]
    nch = q_ref.shape[0] // cs
    heads = range(RET_HEADS)
    hsl = [slice(h * D, (h + 1) * D) for h in heads]
    rows = [slice(ch * cs, (ch + 1) * cs) for ch in range(nch)]
    prob = [(ch, h) for ch in range(nch) for h in heads]

    @pl.when(c == 0)
    def _():
        s_ref[...] = s0_ref[0]

    cos, sin = cos_ref[...], sin_ref[...]
    qb_, kb_, kd_, vb_ = [], [], [], []
    for h in heads:
        q = q_ref[:, hsl[h]].astype(F32)
        k = k_ref[:, hsl[h]].astype(F32)
        k = (k * cos + pltpu.roll(k, D // 2, 1) * sin) * (D ** -0.5)
        qb_.append(_bf(q * cos + pltpu.roll(q, D // 2, 1) * sin))
        kb_.append(_bf(k))
        kd_.append([_bf(k[r] * kd_ref[:, hsl[h]]) for r in rows])
        vb_.append(_bf(v_ref[:, hsl[h]]))
    scores = {(ch, h): _dot_nt(qb_[h][rows[ch]], kb_[h][rows[ch]]) * dm_ref[h] for ch, h in prob}
    ktv = {(ch, h): _dot_tn(kd_[h][ch], vb_[h][rows[ch]]) for ch, h in prob}
    intra = {(ch, h): _dot(_bf(scores[ch, h]), vb_[h][rows[ch]]) for ch, h in prob}
    s_at = {}
    for h in heads:
        s = s_ref[h]
        for ch in range(nch):
            s_at[ch, h] = _bf(s)
            s = s * gc_ref[:, hsl[h]] + ktv[ch, h]
        s_ref[h] = s
    cross = {(ch, h): _dot(qb_[h][rows[ch]], s_at[ch, h]) for ch, h in prob}
    for ch in range(nch):
        outs = []
        for h in heads:
            o = intra[ch, h] + cross[ch, h] * qd_ref[:, hsl[h]]
            o = o * lax.rsqrt(jnp.mean(o * o, axis=-1, keepdims=True) + EPS)
            outs.append(jax.nn.silu(g_ref[rows[ch], hsl[h]].astype(F32)) * o)
        y_ref[rows[ch], :] = jnp.concatenate(outs, axis=1)

    @pl.when(c == pl.num_programs(1) - 1)
    def _():
        so_ref[0] = s_ref[...]


def _retention(P, B, L, pos, s0):
    T = P.shape[0]
    c = min(CHUNK, L)
    nc = L // c
    D = RET_HEAD_DIM
    freqs = 1.0 / (RET_ROPE_BASE ** jnp.linspace(0.0, 1.0, D // 2, dtype=F32))
    ang = pos.astype(F32)[:, None] * freqs[None, :]
    cos = jnp.concatenate([jnp.cos(ang)] * 2, axis=1)
    sin = jnp.concatenate([-jnp.sin(ang), jnp.sin(ang)], axis=1)
    log_gamma = jnp.log1p(-jnp.exp2(-5.0 - jnp.arange(RET_HEADS, dtype=F32)))
    idx = jnp.arange(c, dtype=F32)
    dmask = jnp.exp(jnp.abs(idx[:, None] - idx[None, :])[None] * log_gamma[:, None, None])
    lanes = lambda t: jnp.repeat(t, D, axis=1)
    qdec = lanes(jnp.exp((idx[:, None] + 1.0) * log_gamma[None, :]))
    kdec = lanes(jnp.exp((c - 1.0 - idx)[:, None] * log_gamma[None, :]))
    gchunk = lanes(jnp.exp(c * log_gamma)[None, :])

    W = RET_WIDTH
    tt = min(4 * c, L)
    nt = L // tt
    tok = lambda blk: pl.BlockSpec((tt, W), lambda b, i: (b * nt + i, blk))
    full = lambda shape: pl.BlockSpec(shape, lambda b, i: (0,) * len(shape))
    st_spec = pl.BlockSpec((1, RET_HEADS, D, D), lambda b, i: (b, 0, 0, 0))
    y, s_out = pl.pallas_call(
        _ret_kernel,
        grid=(B, nt),
        in_specs=[tok(P_RET // W), tok(P_RET // W + 1), tok(P_RET // W + 2), tok(P_RET // W + 3),
                  pl.BlockSpec((tt, D), lambda b, i: (i, 0)), pl.BlockSpec((tt, D), lambda b, i: (i, 0)),
                  full((RET_HEADS, c, c)), full((c, W)), full((c, W)), full((1, W)), st_spec],
        out_specs=[pl.BlockSpec((tt, W), lambda b, i: (b * nt + i, 0)), st_spec],
        out_shape=[jax.ShapeDtypeStruct((T, W), F32), jax.ShapeDtypeStruct((B, RET_HEADS, D, D), F32)],
        scratch_shapes=[pltpu.VMEM((RET_HEADS, D, D), F32)],
        compiler_params=_params(("parallel", "arbitrary")),
        name="retention",
    )(P, P, P, P, cos, sin, dmask, qdec, kdec, gchunk, s0.astype(F32))
    return y, s_out


def _merge_kernel(x_ref, g0_ref, g1_ref, g2_ref, yr_ref, ys_ref, yt_ref, wr_ref, ws_ref, wt_ref, wo_ref,
                  o_ref):
    gate = lambda g_ref: jax.nn.sigmoid(g_ref[...].astype(F32))
    m = (gate(g0_ref) * _dot(_bf(yr_ref[...]), wr_ref[...])
         + gate(g1_ref) * _dot_tn(_bf(ys_ref[0]), ws_ref[...])
         + gate(g2_ref) * _dot(_bf(yt_ref[...]), wt_ref[...]))
    o_ref[...] = x_ref[...] + _dot(_bf(m), wo_ref[...])


def _merge(x2d, P, y_rw, y_sa_t, y_ret, w_rw, w_sa, w_ret, w_o):
    T = x2d.shape[0]
    L = y_sa_t.shape[2]
    tm = min(512, L)
    nl = L // tm
    D = D_MODEL
    tok = lambda w, blk: pl.BlockSpec((tm, w), lambda i: (i, blk))
    full = lambda shape: pl.BlockSpec(shape, lambda i: (0, 0))
    return pl.pallas_call(
        _merge_kernel,
        grid=(T // tm,),
        in_specs=[tok(D, 0), tok(D, 0), tok(D, 1), tok(D, 2), tok(RW_WIDTH, 0),
                  pl.BlockSpec((1, SA_WIDTH, tm), lambda i: (i // nl, 0, i % nl)),
                  tok(RET_WIDTH, 0), full((RW_WIDTH, D)), full((SA_WIDTH, D)), full((RET_WIDTH, D)),
                  full((D, D))],
        out_specs=tok(D, 0),
        out_shape=jax.ShapeDtypeStruct((T, D), F32),
        compiler_params=_params(("parallel",)),
        name="merge",
    )(x2d, P, P, P, y_rw, y_sa_t, y_ret, w_rw, w_sa, w_ret, w_o)


def _mlp_kernel(x_ref, g_ref, wu_ref, wd_ref, gf_ref, o_ref, h_ref, acc_ref, *, final_norm):
    j = pl.program_id(1)

    @pl.when(j == 0)
    def _():
        x = x_ref[...]
        ms = jnp.mean(x * x, axis=-1, keepdims=True)
        h_ref[...] = _bf(x * lax.rsqrt(ms + EPS) * g_ref[...])
        acc_ref[...] = jnp.zeros_like(acc_ref)

    u = jnp.maximum(_dot(h_ref[...], wu_ref[...]), 0.0)
    acc_ref[...] += _dot(_bf(u * u), wd_ref[...])

    @pl.when(j == pl.num_programs(1) - 1)
    def _():
        xn = x_ref[...] + acc_ref[...]
        if final_norm:
            ms = jnp.mean(xn * xn, axis=-1, keepdims=True)
            xn = xn * lax.rsqrt(ms + EPS) * gf_ref[...]
        o_ref[...] = xn


def _mlp(x2d, g, w_up, w_down, g_final, final_norm):
    T = x2d.shape[0]
    tm = min(1024, T)
    tf = 2048
    D = D_MODEL
    return pl.pallas_call(
        functools.partial(_mlp_kernel, final_norm=final_norm),
        grid=(T // tm, D_FF // tf),
        in_specs=[pl.BlockSpec((tm, D), lambda i, j: (i, 0)),
                  pl.BlockSpec((1, D), lambda i, j: (0, 0)),
                  pl.BlockSpec((D, tf), lambda i, j: (0, j)),
                  pl.BlockSpec((tf, D), lambda i, j: (j, 0)),
                  pl.BlockSpec((1, D), lambda i, j: (0, 0))],
        out_specs=pl.BlockSpec((tm, D), lambda i, j: (i, 0)),
        out_shape=jax.ShapeDtypeStruct((T, D), F32),
        scratch_shapes=[pltpu.VMEM((tm, D), BF16), pltpu.VMEM((tm, D), F32)],
        compiler_params=_params(("parallel", "arbitrary")),
        name="mlp",
    )(x2d, g, w_up, w_down, g_final)


def _layer(x2d, B, L, past, caches, lp, wts, g_final, final_norm):
    k_past, v_past, ik_past, s_rw, shift_rw, s_ret = caches
    pos = past + jnp.arange(L, dtype=jnp.int32)
    row = lambda t: t.reshape(1, -1).astype(F32)
    P = _in_proj(x2d, row(lp['norm1_g']), wts['w_in'], wts['layer'])
    q_rot, qi_rot, k_rot, kw_rot = _sa_prep(P, L, _sa_tables(pos))
    y_rw, s_rw_new = _rwkv(P, B, L, shift_rw, s_rw, lp)
    y_sa = _dsa(P, B, L, past, q_rot, qi_rot, k_rot, kw_rot, k_past, v_past, ik_past)
    y_ret, s_ret_new = _retention(P, B, L, pos, s_ret)
    x2d = _merge(x2d, P, y_rw, y_sa, y_ret, wts['w_br_rwkv'], wts['w_br_dsa'], wts['w_br_ret'], wts['w_o'])
    x2d = _mlp(x2d, row(lp['norm2_g']), wts['w_up'], wts['w_down'], g_final, final_norm)
    P3 = P.reshape(B, L, P_COLS)
    last = P3[:, L - 1].astype(F32)
    shift_new = jnp.concatenate([last[:, P_RKV:P_RKV + 3 * RW_WIDTH], last[:, P_LORA:P_LORA + RW_LORA]], axis=1)
    k_new = k_rot.reshape(B, L, SA_KV_HEADS, SA_HEAD_DIM)
    v_new = P3[:, :, P_SAV:P_SAV + SA_KV_WIDTH].astype(F32).reshape(B, L, SA_KV_HEADS, SA_HEAD_DIM)
    ik_new = kw_rot.reshape(B, L, LANES)[:, :, :IDX_DIM]
    return x2d, (k_new, v_new, ik_new, s_rw_new, shift_new, s_ret_new)


def kernel(x_prompt, x_sample, cache_dsa_k, cache_dsa_v, cache_dsa_ik, state_rwkv, state_rwkv_shift, state_ret, norm1_g, w_in, rwkv_mu, rwkv_w0, rwkv_w2, rwkv_a0, rwkv_a2, rwkv_g2, rwkv_k_k, rwkv_k_a, rwkv_r_k, rwkv_lnx_g, w_br_rwkv, w_br_dsa, w_br_ret, w_o, norm2_g, w_up, w_down, final_norm_g):
    params = {
        'norm1_g': norm1_g, 'rwkv_mu': rwkv_mu, 'rwkv_w0': rwkv_w0, 'rwkv_w2': rwkv_w2,
        'rwkv_a0': rwkv_a0, 'rwkv_a2': rwkv_a2, 'rwkv_g2': rwkv_g2, 'rwkv_k_k': rwkv_k_k,
        'rwkv_k_a': rwkv_k_a, 'rwkv_r_k': rwkv_r_k, 'rwkv_lnx_g': rwkv_lnx_g, 'norm2_g': norm2_g,
    }
    depth = w_in.shape[0]
    Bp, Lp, D = x_prompt.shape
    Bs, Ls, _ = x_sample.shape
    past_s = cache_dsa_k.shape[2]
    xp = x_prompt.reshape(Bp * Lp, D).astype(F32)
    xs = x_sample.reshape(Bs * Ls, D).astype(F32)
    g_final = final_norm_g.reshape(1, D).astype(F32)
    zero_p = (None, None, None,
              jnp.zeros((Bp, RW_HEADS, RW_HEAD_DIM, RW_HEAD_DIM), F32), jnp.zeros((Bp, RW_COLS), F32),
              jnp.zeros((Bp, RET_HEADS, RET_HEAD_DIM, RET_HEAD_DIM), F32))
    p_states = [[] for _ in range(6)]
    s_states = [[] for _ in range(6)]
    w_all = _w_prep(w_in)
    for i in range(depth):
        lp = {name: arr[i] for name, arr in params.items()}
        wts = {'w_in': w_all, 'layer': i, 'w_br_rwkv': _bf(w_br_rwkv[i]), 'w_br_dsa': _bf(w_br_dsa[i]),
               'w_br_ret': _bf(w_br_ret[i]), 'w_o': _bf(w_o[i]), 'w_up': _bf(w_up[i]), 'w_down': _bf(w_down[i])}
        final = i == depth - 1
        cache_s = (cache_dsa_k[i], cache_dsa_v[i], cache_dsa_ik[i], state_rwkv[i], state_rwkv_shift[i],
                   state_ret[i])
        xp, new_p = _layer(xp, Bp, Lp, 0, zero_p, lp, wts, g_final, final)
        xs, new_s = _layer(xs, Bs, Ls, past_s, cache_s, lp, wts, g_final, final)
        for j in range(6):
            p_states[j].append(new_p[j])
            s_states[j].append(new_s[j])
    y_prompt = xp.reshape(Bp, Lp, D)
    y_sample = xs.reshape(Bs, Ls, D)
    p_out = [jnp.stack(t, axis=0) for t in p_states]
    s_out = [jnp.stack(t, axis=0) for t in s_states]
    return (y_prompt, y_sample, *p_out, *s_out)
```
